```python
import math
import jax, jax.numpy as jnp
from jax import lax
import numpy as np

D_MODEL = 1024
BATCH = 8
SEQ = 8192
DEPTH = 1

EPS = 1e-6
HEAD_DIM = 64
N_Q_HEADS = D_MODEL // 64
N_KV_HEADS = max(1, N_Q_HEADS // 8)
GROUP = N_Q_HEADS // N_KV_HEADS
WINDOW = 128
BLOCK = 128
RET_HEADS = D_MODEL // 256
RET_QK_DIM = 256
RET_V_DIM = 512
RET_CHUNK = 128
RET_ROT_BASE = 10000.0
D_FF = ((-(-8 * D_MODEL // 3) + 255) // 256) * 256

ATT_Q = N_Q_HEADS * HEAD_DIM
ATT_KV = N_KV_HEADS * HEAD_DIM
RET_QK = RET_HEADS * RET_QK_DIM
RET_V = RET_HEADS * RET_V_DIM
IN_SPLITS = (ATT_Q, ATT_KV, ATT_KV, RET_QK, RET_QK, RET_V, RET_V, D_MODEL, D_MODEL)
D_IN = sum(IN_SPLITS)

kernel_name = "hybrid_swa_sink_retention_gated_block"


def rmsnorm(x, gain):
    xf = x.astype(jnp.float32)
    xf = xf * lax.rsqrt(jnp.mean(xf * xf, axis=-1, keepdims=True) + EPS)
    return (xf * gain.astype(jnp.float32)).astype(x.dtype)


def rms_group_norm(x):
    xf = x.astype(jnp.float32)
    return xf * lax.rsqrt(jnp.mean(xf * xf, axis=-1, keepdims=True) + EPS)


def sliding_window_attention(q, k, v, q_gain, k_gain, sinks):
    b, s, _, _ = q.shape
    nb = s // BLOCK
    q = rmsnorm(q, q_gain)
    k = rmsnorm(k, k_gain)
    qb = q.reshape(b, nb, BLOCK, N_KV_HEADS, GROUP, HEAD_DIM)
    kb = k.reshape(b, nb, BLOCK, N_KV_HEADS, HEAD_DIM)
    vb = v.reshape(b, nb, BLOCK, N_KV_HEADS, HEAD_DIM)
    pad = ((0, 0), (1, 0), (0, 0), (0, 0), (0, 0))
    k_band = jnp.concatenate([jnp.pad(kb, pad)[:, :-1], kb], axis=2)
    v_band = jnp.concatenate([jnp.pad(vb, pad)[:, :-1], vb], axis=2)
    scores = jnp.einsum('bnqkgd,bnskd->bnkgqs', qb, k_band).astype(jnp.float32) * (HEAD_DIM ** -0.5)
    blk = jnp.arange(nb)[:, None, None]
    q_pos = blk * BLOCK + jnp.arange(BLOCK)[None, :, None]
    k_pos = blk * BLOCK - BLOCK + jnp.arange(2 * BLOCK)[None, None, :]
    allowed = (k_pos <= q_pos) & (k_pos > q_pos - WINDOW) & (k_pos >= 0)
    scores = jnp.where(allowed[None, :, None, None], scores, -jnp.inf)
    sink = sinks.astype(jnp.float32).reshape(N_KV_HEADS, GROUP)[None, None, :, :, None, None]
    sink = jnp.broadcast_to(sink, scores.shape[:-1] + (1,))
    probs = jax.nn.softmax(jnp.concatenate([scores, sink], axis=-1), axis=-1)[..., :-1]
    out = jnp.einsum('bnkgqs,bnskd->bnqkgd', probs.astype(v.dtype), v_band)
    return out.reshape(b, s, N_Q_HEADS * HEAD_DIM)


def rotate_every_two(x):
    x1 = x[..., 0::2]
    x2 = x[..., 1::2]
    return jnp.stack((-x2, x1), axis=-1).reshape(x.shape)


def retention_chunkwise(q, k, v):
    b, s, _, _ = q.shape
    n = s // RET_CHUNK
    q = q.astype(jnp.float32)
    k = k.astype(jnp.float32) * (RET_QK_DIM ** -0.5)
    v = v.astype(jnp.float32)
    pos = jnp.arange(s, dtype=jnp.float32)
    theta = 1.0 / (RET_ROT_BASE ** jnp.linspace(0.0, 1.0, RET_QK_DIM // 2, dtype=jnp.float32))
    ang = jnp.repeat(pos[:, None] * theta[None, :], 2, axis=-1)[None, :, None, :]
    cos, sin = jnp.cos(ang), jnp.sin(ang)
    q = q * cos + rotate_every_two(q) * sin
    k = k * cos + rotate_every_two(k) * sin
    log_gamma = jnp.log(1.0 - 2.0 ** (-5.0 - jnp.arange(RET_HEADS, dtype=jnp.float32)))
    i = jnp.arange(RET_CHUNK, dtype=jnp.float32)
    diff = i[:, None] - i[None, :]
    causal = diff >= 0
    decay_inner = jnp.where(causal[None], jnp.exp(jnp.where(causal, diff, 0.0)[None] * log_gamma[:, None, None]), 0.0)
    xi = jnp.exp((i + 1.0)[None, :] * log_gamma[:, None])
    zeta = jnp.exp((RET_CHUNK - 1.0 - i)[None, :] * log_gamma[:, None])
    gamma_chunk = jnp.exp(RET_CHUNK * log_gamma)

    def to_chunks(t):
        return t.reshape(b, n, RET_CHUNK, RET_HEADS, t.shape[-1]).transpose(1, 0, 3, 2, 4)

    def step(state, qkv):
        qc, kc, vc = qkv
        inner = jnp.einsum('bhid,bhjd->bhij', qc, kc) * decay_inner
        out = jnp.einsum('bhij,bhjv->bhiv', inner, vc)
        out = out + jnp.einsum('bhid,bhdv->bhiv', qc, state) * xi[None, :, :, None]
        state = gamma_chunk[None, :, None, None] * state + jnp.einsum('bhjd,bhjv->bhdv', kc * zeta[None, :, :, None], vc)
        return state, out

    state0 = jnp.zeros((b, RET_HEADS, RET_QK_DIM, RET_V_DIM), jnp.float32)
    _, out = lax.scan(step, state0, (to_chunks(q), to_chunks(k), to_chunks(v)))
    return out.transpose(1, 0, 3, 2, 4).reshape(b, s, RET_HEADS, RET_V_DIM)


def _fwd_setup_inputs(seed: int = 0) -> dict:
    key = jax.random.key(seed)
    ks = jax.random.split(key, 14)
    L = DEPTH
    nrm = lambda k, shape, fan_in: jax.random.normal(k, shape, jnp.float32) * (fan_in ** -0.5)
    gain = lambda k, shape: 1.0 + 0.02 * jax.random.normal(k, shape, jnp.float32)
    return {
        "x": jax.random.normal(ks[0], (BATCH, SEQ, D_MODEL), jnp.float32),
        "norm_mix_gain": gain(ks[1], (L, D_MODEL)),
        "w_in": nrm(ks[2], (L, D_MODEL, D_IN), D_MODEL),
        "q_norm_gain": gain(ks[3], (L, HEAD_DIM)),
        "k_norm_gain": gain(ks[4], (L, HEAD_DIM)),
        "attn_sinks": 0.5 * jax.random.normal(ks[5], (L, N_Q_HEADS), jnp.float32),
        "w_branch_attn": nrm(ks[6], (L, ATT_Q, D_MODEL), ATT_Q),
        "w_branch_ret": nrm(ks[7], (L, RET_V, D_MODEL), RET_V),
        "w_out": nrm(ks[8], (L, D_MODEL, D_MODEL), D_MODEL),
        "norm_ffn_gain": gain(ks[9], (L, D_MODEL)),
        "w_ffn_gate": nrm(ks[10], (L, D_MODEL, D_FF), D_MODEL),
        "w_ffn_up": nrm(ks[11], (L, D_MODEL, D_FF), D_MODEL),
        "w_ffn_down": nrm(ks[12], (L, D_FF, D_MODEL), D_FF),
    }


def _fwd_reference(x, norm_mix_gain, w_in, q_norm_gain, k_norm_gain, attn_sinks, w_branch_attn, w_branch_ret, w_out, norm_ffn_gain, w_ffn_gate, w_ffn_up, w_ffn_down):
    b, s, _ = x.shape
    split_points = list(np.cumsum(IN_SPLITS)[:-1])
    for l in range(DEPTH):
        h = rmsnorm(x, norm_mix_gain[l])
        proj = jnp.einsum('bsd,de->bse', h, w_in[l])
        q_a, k_a, v_a, q_r, k_r, v_r, g_r, z_a, z_r = jnp.split(proj, split_points, axis=-1)
        attn = sliding_window_attention(
            q_a.reshape(b, s, N_Q_HEADS, HEAD_DIM),
            k_a.reshape(b, s, N_KV_HEADS, HEAD_DIM),
            v_a.reshape(b, s, N_KV_HEADS, HEAD_DIM),
            q_norm_gain[l], k_norm_gain[l], attn_sinks[l])
        ret = retention_chunkwise(
            q_r.reshape(b, s, RET_HEADS, RET_QK_DIM),
            k_r.reshape(b, s, RET_HEADS, RET_QK_DIM),
            v_r.reshape(b, s, RET_HEADS, RET_V_DIM))
        ret = (jax.nn.silu(g_r.astype(jnp.float32)) * rms_group_norm(ret).reshape(b, s, RET_V)).astype(x.dtype)
        branch_a = jnp.einsum('bse,ed->bsd', attn, w_branch_attn[l])
        branch_r = jnp.einsum('bse,ed->bsd', ret, w_branch_ret[l])
        merged = jax.nn.sigmoid(z_a) * branch_a + jax.nn.sigmoid(z_r) * branch_r
        x = x + jnp.einsum('bsd,de->bse', merged, w_out[l])
        h = rmsnorm(x, norm_ffn_gain[l])
        gate = jnp.einsum('bsd,df->bsf', h, w_ffn_gate[l])
        up = jnp.einsum('bsd,df->bsf', h, w_ffn_up[l])
        x = x + jnp.einsum('bsf,fd->bsd', jax.nn.silu(gate) * up, w_ffn_down[l])
    return x


import jax as _jax
import jax.numpy as _jnp

TWIN_FORMAT = 'train_step'
FWD_PARAMS = ['x', 'norm_mix_gain', 'w_in', 'q_norm_gain', 'k_norm_gain', 'attn_sinks', 'w_branch_attn', 'w_branch_ret', 'w_out', 'norm_ffn_gain', 'w_ffn_gate', 'w_ffn_up', 'w_ffn_down']
TWIN_WEIGHTS = ['norm_mix_gain', 'w_in', 'q_norm_gain', 'k_norm_gain', 'attn_sinks', 'w_branch_attn', 'w_branch_ret', 'w_out', 'norm_ffn_gain', 'w_ffn_gate', 'w_ffn_up', 'w_ffn_down']
TWIN_DIFF_INPUT = 'x'
TWIN_INPUTS = ['x', 'norm_mix_gain', 'w_in', 'q_norm_gain', 'k_norm_gain', 'attn_sinks', 'w_branch_attn', 'w_branch_ret', 'w_out', 'norm_ffn_gain', 'w_ffn_gate', 'w_ffn_up', 'w_ffn_down', 'loss_target', 'm_norm_mix_gain', 'm_w_in', 'm_q_norm_gain', 'm_k_norm_gain', 'm_attn_sinks', 'm_w_branch_attn', 'm_w_branch_ret', 'm_w_out', 'm_norm_ffn_gain', 'm_w_ffn_gate', 'm_w_ffn_up', 'm_w_ffn_down', 'v_norm_mix_gain', 'v_w_in', 'v_q_norm_gain', 'v_k_norm_gain', 'v_attn_sinks', 'v_w_branch_attn', 'v_w_branch_ret', 'v_w_out', 'v_norm_ffn_gain', 'v_w_ffn_gate', 'v_w_ffn_up', 'v_w_ffn_down']
TWIN_OUTPUTS = ['loss', 'grad_x', 'grad_norm_mix_gain', 'grad_w_in', 'grad_q_norm_gain', 'grad_k_norm_gain', 'grad_attn_sinks', 'grad_w_branch_attn', 'grad_w_branch_ret', 'grad_w_out', 'grad_norm_ffn_gain', 'grad_w_ffn_gate', 'grad_w_ffn_up', 'grad_w_ffn_down', 'delta_norm_mix_gain', 'delta_w_in', 'delta_q_norm_gain', 'delta_k_norm_gain', 'delta_attn_sinks', 'delta_w_branch_attn', 'delta_w_branch_ret', 'delta_w_out', 'delta_norm_ffn_gain', 'delta_w_ffn_gate', 'delta_w_ffn_up', 'delta_w_ffn_down', 'new_m_norm_mix_gain', 'new_m_w_in', 'new_m_q_norm_gain', 'new_m_k_norm_gain', 'new_m_attn_sinks', 'new_m_w_branch_attn', 'new_m_w_branch_ret', 'new_m_w_out', 'new_m_norm_ffn_gain', 'new_m_w_ffn_gate', 'new_m_w_ffn_up', 'new_m_w_ffn_down', 'new_v_norm_mix_gain', 'new_v_w_in', 'new_v_q_norm_gain', 'new_v_k_norm_gain', 'new_v_attn_sinks', 'new_v_w_branch_attn', 'new_v_w_branch_ret', 'new_v_w_out', 'new_v_norm_ffn_gain', 'new_v_w_ffn_gate', 'new_v_w_ffn_up', 'new_v_w_ffn_down']
TWIN_LEAF_KINDS = {'loss': 'loss', 'grad_x': 'grad_x', 'grad_norm_mix_gain': 'grad_w', 'grad_w_in': 'grad_w', 'grad_q_norm_gain': 'grad_w', 'grad_k_norm_gain': 'grad_w', 'grad_attn_sinks': 'grad_w', 'grad_w_branch_attn': 'grad_w', 'grad_w_branch_ret': 'grad_w', 'grad_w_out': 'grad_w', 'grad_norm_ffn_gain': 'grad_w', 'grad_w_ffn_gate': 'grad_w', 'grad_w_ffn_up': 'grad_w', 'grad_w_ffn_down': 'grad_w', 'delta_norm_mix_gain': 'delta_w', 'delta_w_in': 'delta_w', 'delta_q_norm_gain': 'delta_w', 'delta_k_norm_gain': 'delta_w', 'delta_attn_sinks': 'delta_w', 'delta_w_branch_attn': 'delta_w', 'delta_w_branch_ret': 'delta_w', 'delta_w_out': 'delta_w', 'delta_norm_ffn_gain': 'delta_w', 'delta_w_ffn_gate': 'delta_w', 'delta_w_ffn_up': 'delta_w', 'delta_w_ffn_down': 'delta_w', 'new_m_norm_mix_gain': 'new_m', 'new_m_w_in': 'new_m', 'new_m_q_norm_gain': 'new_m', 'new_m_k_norm_gain': 'new_m', 'new_m_attn_sinks': 'new_m', 'new_m_w_branch_attn': 'new_m', 'new_m_w_branch_ret': 'new_m', 'new_m_w_out': 'new_m', 'new_m_norm_ffn_gain': 'new_m', 'new_m_w_ffn_gate': 'new_m', 'new_m_w_ffn_up': 'new_m', 'new_m_w_ffn_down': 'new_m', 'new_v_norm_mix_gain': 'new_v', 'new_v_w_in': 'new_v', 'new_v_q_norm_gain': 'new_v', 'new_v_k_norm_gain': 'new_v', 'new_v_attn_sinks': 'new_v', 'new_v_w_branch_attn': 'new_v', 'new_v_w_branch_ret': 'new_v', 'new_v_w_out': 'new_v', 'new_v_norm_ffn_gain': 'new_v', 'new_v_w_ffn_gate': 'new_v', 'new_v_w_ffn_up': 'new_v', 'new_v_w_ffn_down': 'new_v'}


def _forward(args):
    return _fwd_reference(*[args[k] for k in FWD_PARAMS])


def _output_shape():
    out = _jax.eval_shape(lambda: _forward(_fwd_setup_inputs(0)))
    return out.shape, out.dtype

N_MICROBATCH = 1
ADAM_LR = 0.001
ADAM_B1 = 0.9
ADAM_B2 = 0.999
ADAM_EPS = 1e-08
ADAM_WD = 0.01
ADAM_STEP = 10
PER_EXAMPLE_BATCH_AXIS = {'x': 0, 'loss_target': 0}
SHARED_INPUTS = []
_WEIGHT_DTYPES = {'norm_mix_gain': _jnp.float32, 'w_in': _jnp.float32, 'q_norm_gain': _jnp.float32, 'k_norm_gain': _jnp.float32, 'attn_sinks': _jnp.float32, 'w_branch_attn': _jnp.float32, 'w_branch_ret': _jnp.float32, 'w_out': _jnp.float32, 'norm_ffn_gain': _jnp.float32, 'w_ffn_gate': _jnp.float32, 'w_ffn_up': _jnp.float32, 'w_ffn_down': _jnp.float32}
MOMENT_SCALE = {'norm_mix_gain': 7.867374e+00, 'w_in': 1.315690e-01, 'q_norm_gain': 2.898358e+00, 'k_norm_gain': 2.901890e+00, 'attn_sinks': 3.659057e-01, 'w_branch_attn': 6.872229e-02, 'w_branch_ret': 1.985751e-01, 'w_out': 2.077816e-01, 'norm_ffn_gain': 4.940307e+01, 'w_ffn_gate': 2.153439e-01, 'w_ffn_up': 2.331410e-01, 'w_ffn_down': 3.640387e-01}


def _to_microbatches(a, axis):
    t = _jnp.moveaxis(a, axis, 0)
    t = t.reshape((N_MICROBATCH, t.shape[0] // N_MICROBATCH) + t.shape[1:])
    return _jnp.moveaxis(t, 1, axis + 1)


def setup_inputs(seed: int = 0) -> dict:
    inp = _fwd_setup_inputs(seed)
    key = _jax.random.fold_in(_jax.random.key(seed), 7919)
    shape, _ = _output_shape()
    out = dict(inp)
    out["loss_target"] = _jax.random.normal(_jax.random.fold_in(key, 0), shape, _jnp.float32)
    for i, name in enumerate(TWIN_WEIGHTS):
        w = inp[name].astype(_jnp.float32)
        if MOMENT_SCALE is None:
            s = _jnp.sqrt(_jnp.mean(_jnp.square(w)) + 1e-30)
        else:
            s = MOMENT_SCALE[name]
        km, kv = _jax.random.split(_jax.random.fold_in(key, i + 1))
        out[name] = w
        out["m_" + name] = s * _jax.random.normal(km, w.shape, _jnp.float32)
        out["v_" + name] = (s * s) * _jax.random.uniform(kv, w.shape, _jnp.float32, 0.5, 1.5)
    if N_MICROBATCH > 1:
        for name, axis in PER_EXAMPLE_BATCH_AXIS.items():
            out[name] = _to_microbatches(out[name], axis)
    return {'x': out['x'], 'norm_mix_gain': out['norm_mix_gain'], 'w_in': out['w_in'], 'q_norm_gain': out['q_norm_gain'], 'k_norm_gain': out['k_norm_gain'], 'attn_sinks': out['attn_sinks'], 'w_branch_attn': out['w_branch_attn'], 'w_branch_ret': out['w_branch_ret'], 'w_out': out['w_out'], 'norm_ffn_gain': out['norm_ffn_gain'], 'w_ffn_gate': out['w_ffn_gate'], 'w_ffn_up': out['w_ffn_up'], 'w_ffn_down': out['w_ffn_down'], 'loss_target': out['loss_target'], 'm_norm_mix_gain': out['m_norm_mix_gain'], 'm_w_in': out['m_w_in'], 'm_q_norm_gain': out['m_q_norm_gain'], 'm_k_norm_gain': out['m_k_norm_gain'], 'm_attn_sinks': out['m_attn_sinks'], 'm_w_branch_attn': out['m_w_branch_attn'], 'm_w_branch_ret': out['m_w_branch_ret'], 'm_w_out': out['m_w_out'], 'm_norm_ffn_gain': out['m_norm_ffn_gain'], 'm_w_ffn_gate': out['m_w_ffn_gate'], 'm_w_ffn_up': out['m_w_ffn_up'], 'm_w_ffn_down': out['m_w_ffn_down'], 'v_norm_mix_gain': out['v_norm_mix_gain'], 'v_w_in': out['v_w_in'], 'v_q_norm_gain': out['v_q_norm_gain'], 'v_k_norm_gain': out['v_k_norm_gain'], 'v_attn_sinks': out['v_attn_sinks'], 'v_w_branch_attn': out['v_w_branch_attn'], 'v_w_branch_ret': out['v_w_branch_ret'], 'v_w_out': out['v_w_out'], 'v_norm_ffn_gain': out['v_norm_ffn_gain'], 'v_w_ffn_gate': out['v_w_ffn_gate'], 'v_w_ffn_up': out['v_w_ffn_up'], 'v_w_ffn_down': out['v_w_ffn_down']}


def _loss(weights, diff, rest, loss_target):
    with _jax.named_scope("forward"):
        args = {**rest, TWIN_DIFF_INPUT: diff, **{k: w.astype(_WEIGHT_DTYPES[k]) for k, w in weights.items()}}
        y = _forward(args)
    with _jax.named_scope("loss_head"):
        err = _jnp.square(y.astype(_jnp.float32) - loss_target)
        return 0.5 * _jnp.sum(_jnp.mean(err, axis=-1)) if err.ndim else 0.5 * err


def _adamw(w, g, m, v):
    m = ADAM_B1 * m + (1.0 - ADAM_B1) * g
    v = ADAM_B2 * v + (1.0 - ADAM_B2) * _jnp.square(g)
    m_hat = m / (1.0 - ADAM_B1 ** ADAM_STEP)
    v_hat = v / (1.0 - ADAM_B2 ** ADAM_STEP)
    delta = -ADAM_LR * (m_hat / (_jnp.sqrt(v_hat) + ADAM_EPS) + ADAM_WD * w)
    return delta, m, v


def reference(x, norm_mix_gain, w_in, q_norm_gain, k_norm_gain, attn_sinks, w_branch_attn, w_branch_ret, w_out, norm_ffn_gain, w_ffn_gate, w_ffn_up, w_ffn_down, loss_target, m_norm_mix_gain, m_w_in, m_q_norm_gain, m_k_norm_gain, m_attn_sinks, m_w_branch_attn, m_w_branch_ret, m_w_out, m_norm_ffn_gain, m_w_ffn_gate, m_w_ffn_up, m_w_ffn_down, v_norm_mix_gain, v_w_in, v_q_norm_gain, v_k_norm_gain, v_attn_sinks, v_w_branch_attn, v_w_branch_ret, v_w_out, v_norm_ffn_gain, v_w_ffn_gate, v_w_ffn_up, v_w_ffn_down):
    given = dict(x=x, norm_mix_gain=norm_mix_gain, w_in=w_in, q_norm_gain=q_norm_gain, k_norm_gain=k_norm_gain, attn_sinks=attn_sinks, w_branch_attn=w_branch_attn, w_branch_ret=w_branch_ret, w_out=w_out, norm_ffn_gain=norm_ffn_gain, w_ffn_gate=w_ffn_gate, w_ffn_up=w_ffn_up, w_ffn_down=w_ffn_down, loss_target=loss_target, m_norm_mix_gain=m_norm_mix_gain, m_w_in=m_w_in, m_q_norm_gain=m_q_norm_gain, m_k_norm_gain=m_k_norm_gain, m_attn_sinks=m_attn_sinks, m_w_branch_attn=m_w_branch_attn, m_w_branch_ret=m_w_branch_ret, m_w_out=m_w_out, m_norm_ffn_gain=m_norm_ffn_gain, m_w_ffn_gate=m_w_ffn_gate, m_w_ffn_up=m_w_ffn_up, m_w_ffn_down=m_w_ffn_down, v_norm_mix_gain=v_norm_mix_gain, v_w_in=v_w_in, v_q_norm_gain=v_q_norm_gain, v_k_norm_gain=v_k_norm_gain, v_attn_sinks=v_attn_sinks, v_w_branch_attn=v_w_branch_attn, v_w_branch_ret=v_w_branch_ret, v_w_out=v_w_out, v_norm_ffn_gain=v_norm_ffn_gain, v_w_ffn_gate=v_w_ffn_gate, v_w_ffn_up=v_w_ffn_up, v_w_ffn_down=v_w_ffn_down)
    weights = {n: given[n] for n in TWIN_WEIGHTS}
    shared = {n: given[n] for n in SHARED_INPUTS}
    per_example = {n: given[n] for n in ['x']}
    grad_fn = _jax.value_and_grad(_loss, argnums=(0, 1))

    def one_microbatch(ex, loss_target):
        ex = dict(ex)
        diff = ex.pop(TWIN_DIFF_INPUT)
        return grad_fn(weights, diff, {**shared, **ex}, loss_target)

    if N_MICROBATCH == 1:
        loss, (grad_w, grad_x) = one_microbatch(per_example, given["loss_target"])
    else:
        def body(carry, xs):
            loss_sum, grad_sum = carry
            l_k, (gw_k, gx_k) = one_microbatch(xs[0], xs[1])
            with _jax.named_scope("update"):
                return (loss_sum + l_k, _jax.tree.map(_jnp.add, grad_sum, gw_k)), gx_k

        init = (_jnp.zeros((), _jnp.float32), _jax.tree.map(_jnp.zeros_like, weights))
        (loss, grad_w), grad_x = _jax.lax.scan(body, init, (per_example, given["loss_target"]))
    with _jax.named_scope("update"):
        delta_w, new_m, new_v = {}, {}, {}
        for n in TWIN_WEIGHTS:
            delta_w[n], new_m[n], new_v[n] = _adamw(weights[n], grad_w[n], given["m_" + n], given["v_" + n])
    return (loss, grad_x, *[grad_w[n] for n in TWIN_WEIGHTS], *[delta_w[n] for n in TWIN_WEIGHTS],
            *[new_m[n] for n in TWIN_WEIGHTS], *[new_v[n] for n in TWIN_WEIGHTS])
```

```python
import functools

import jax
import jax.numpy as jnp
from jax import lax
from jax.experimental import pallas as pl
from jax.experimental.pallas import tpu as pltpu

F32 = jnp.float32
BF16 = jnp.bfloat16
MESH = pl.DeviceIdType.MESH

D_MODEL = 1024
EPS = 1e-6
HEAD_DIM = 64
N_Q_HEADS = 16
N_KV_HEADS = 2
GROUP = 8
BLOCK = 128
RET_HEADS = 4
RET_QK_DIM = 256
RET_V_DIM = 512
RET_CHUNK = 128
RET_ROT_BASE = 10000.0
D_FF = 2816
ATT_Q = N_Q_HEADS * HEAD_DIM
ATT_KV = N_KV_HEADS * HEAD_DIM
RET_QK = RET_HEADS * RET_QK_DIM
RET_V = RET_HEADS * RET_V_DIM
D_IN = 9472
ADAM_LR = 0.001
ADAM_B1 = 0.9
ADAM_B2 = 0.999
ADAM_EPS = 1e-08
ADAM_WD = 0.01
ADAM_STEP = 10

N_CHIPS = 4
N_DEV = 8
VMEM_LIMIT_BYTES = 60 * 1024 * 1024

P_QA = (0, 1024)
P_KVA = (1024, 256)
P_QR = (1280, 1024)
P_KR = (2304, 1024)
P_VR = (3328, 2048)
P_GR = (5376, 2048)
P_ZA = (7424, 1024)
P_ZR = (8448, 1024)

W_IN_SH = D_IN // N_CHIPS
FF_SH = D_FF // N_CHIPS
PACK_SEGS = (2368, 256, 512, 256, 704, 704, 704)
PACK_OFFS = (0, 2368, 2624, 3136, 3392, 4096, 4800)
PACK_ROWS = 5504
PACK_HALF = PACK_ROWS // 2
ROW_TILE = 688

SMALL_ROWS = 24
SM_G1, SM_G2, SM_GQ, SM_GK, SM_SINK, SM_LOSS = 0, 8, 16, 17, 18, 19


def _dot(a, b):
    return jnp.dot(a, b, preferred_element_type=F32)


def _dot_nt(a, b):
    return lax.dot_general(a, b, (((1,), (1,)), ((), ())), preferred_element_type=F32)


def _dot_tn(a, b):
    return lax.dot_general(a, b, (((0,), (0,)), ((), ())), preferred_element_type=F32)


def _bf(x):
    return x.astype(BF16)


def _rms_stats(x):
    r = lax.rsqrt(jnp.mean(x * x, axis=-1, keepdims=True) + EPS)
    return r, x * r


def _rms_bwd(dy, xhat, r, gain):
    u = dy * gain
    dx = r * (u - xhat * jnp.mean(u * xhat, axis=-1, keepdims=True))
    return dx, dy * xhat


def _params(sem):
    return pltpu.CompilerParams(dimension_semantics=sem, vmem_limit_bytes=VMEM_LIMIT_BYTES)


def _row_call(body, *, tm, row_ins, res_ins, row_outs, part_outs=(), name):
    t = row_ins[0].shape[0]
    n_tiles = t // tm
    in_specs = [pl.BlockSpec((tm, a.shape[1]), lambda i: (i, 0)) for a in row_ins]
    in_specs += [pl.BlockSpec(a.shape, lambda i: (0, 0), pipeline_mode=pl.Buffered(1)) for a in res_ins]
    out_shape = [jax.ShapeDtypeStruct((t, w), dt) for (w, dt) in row_outs]
    out_shape += [jax.ShapeDtypeStruct((n_tiles, 1, w), F32) for w in part_outs]
    out_specs = [pl.BlockSpec((tm, w), lambda i: (i, 0)) for (w, _) in row_outs]
    out_specs += [pl.BlockSpec((1, 1, w), lambda i: (i, 0, 0)) for w in part_outs]
    n_ri, n_re, n_ro = len(row_ins), len(res_ins), len(row_outs)

    def kern(*refs):
        body(refs[:n_ri], refs[n_ri:n_ri + n_re], refs[n_ri + n_re:n_ri + n_re + n_ro], refs[n_ri + n_re + n_ro:])

    return pl.pallas_call(
        kern, grid=(n_tiles,), in_specs=in_specs, out_specs=out_specs, out_shape=out_shape, name=name,
        compiler_params=_params(("parallel",)),
    )(*row_ins, *res_ins)


def _proj_fwd(x, g1, w_in):
    pieces = ((P_QA, F32), (P_KVA, F32), (P_QR, F32), (P_KR, F32), (P_VR, BF16), (P_GR, F32), (P_ZA, F32), (P_ZR, F32))

    def body(ri, re, ro, po):
        x_t = ri[0][...]
        r, xhat = _rms_stats(x_t)
        hb = _bf(xhat * re[0][...])
        ro[0][...] = hb
        for k, ((off, w), dt) in enumerate(pieces):
            ro[1 + k][...] = _dot(hb, re[1][:, off:off + w]).astype(dt)

    outs = [(D_MODEL, BF16)] + [(w, dt) for ((_, w), dt) in pieces]
    return _row_call(body, tm=256, row_ins=[x], res_ins=[g1, w_in], row_outs=outs, name="proj_fwd")


def _mix_fwd(attn, ret, z_a, z_r, x, wba, wbr, wout, g2):
    def body(ri, re, ro, po):
        ba = _dot(ri[0][...], re[0][...])
        br = _dot(ri[1][...], re[1][...])
        m = jax.nn.sigmoid(ri[2][...]) * ba + jax.nn.sigmoid(ri[3][...]) * br
        mb = _bf(m)
        x1 = ri[4][...] + _dot(mb, re[2][...])
        r, xhat = _rms_stats(x1)
        ro[0][...] = ba
        ro[1][...] = br
        ro[2][...] = mb
        ro[3][...] = x1
        ro[4][...] = _bf(xhat * re[3][...])

    outs = [(D_MODEL, F32), (D_MODEL, F32), (D_MODEL, BF16), (D_MODEL, F32), (D_MODEL, BF16)]
    return _row_call(body, tm=256, row_ins=[attn, ret, z_a, z_r, x], res_ins=[wba, wbr, wout, g2], row_outs=outs,
                     name="mix_fwd")


def _ffn_fwd_bwd(h2, x1, target, wg, wu, wd, g2):
    def body(ri, re, ro, po):
        h2_t = ri[0][...]
        x1_t = ri[1][...]
        gate = _dot(h2_t, re[0][...])
        up = _dot(h2_t, re[1][...])
        sg = jax.nn.sigmoid(gate)
        sl = gate * sg
        actb = _bf(sl * up)
        ro[0][...] = actb
        y = x1_t + _dot(actb, re[2][...])
        e = y - ri[2][...]
        po[0][0] = jnp.broadcast_to(0.5 * jnp.sum(jnp.sum(e * e, axis=1, keepdims=True), axis=0, keepdims=True)
                                    * (1.0 / D_MODEL), (1, 128))
        dy = e * (1.0 / D_MODEL)
        dyb = _bf(dy)
        ro[3][...] = dyb
        dact = _dot_nt(dyb, re[2][...])
        dupb = _bf(dact * sl)
        dgateb = _bf(dact * up * (sg * (1.0 + gate * (1.0 - sg))))
        ro[1][...] = dgateb
        ro[2][...] = dupb
        dh2 = _dot_nt(dgateb, re[0][...]) + _dot_nt(dupb, re[1][...])
        r, xhat = _rms_stats(x1_t)
        dxn, dgain = _rms_bwd(dh2, xhat, r, re[3][...])
        dx1 = dy + dxn
        ro[4][...] = dx1
        ro[5][...] = _bf(dx1)
        po[1][0] = jnp.sum(dgain, axis=0, keepdims=True)

    outs = [(D_FF, BF16), (D_FF, BF16), (D_FF, BF16), (D_MODEL, BF16), (D_MODEL, F32), (D_MODEL, BF16)]
    return _row_call(body, tm=256, row_ins=[h2, x1, target], res_ins=[wg, wu, wd, g2], row_outs=outs,
                     part_outs=(128, D_MODEL), name="ffn_fwd_bwd")


def _mix_bwd(dx1b, z_a, z_r, ba, br, g_r, o_ret, wout, wba, wbr):
    def body(ri, re, ro, po):
        dm = _dot_nt(ri[0][...], re[0][...])
        sa = jax.nn.sigmoid(ri[1][...])
        sr = jax.nn.sigmoid(ri[2][...])
        dbab = _bf(sa * dm)
        dbrb = _bf(sr * dm)
        ro[0][...] = dbab
        ro[1][...] = dbrb
        ro[2][...] = _bf(dm * ri[3][...] * (sa * (1.0 - sa)))
        ro[3][...] = _bf(dm * ri[4][...] * (sr * (1.0 - sr)))
        ro[4][...] = _bf(_dot_nt(dbab, re[1][...]))
        dret = _dot_nt(dbrb, re[2][...])
        for h in range(RET_HEADS):
            cols = slice(h * RET_V_DIM, (h + 1) * RET_V_DIM)
            g = ri[5][:, cols]
            r, rn = _rms_stats(ri[6][:, cols])
            sg = jax.nn.sigmoid(g)
            dret_h = dret[:, cols]
            d_rn = dret_h * (g * sg)
            ro[6][:, cols] = _bf(dret_h * rn * (sg * (1.0 + g * (1.0 - sg))))
            ro[5][:, cols] = r * (d_rn - rn * jnp.mean(d_rn * rn, axis=-1, keepdims=True))

    outs = [(D_MODEL, BF16), (D_MODEL, BF16), (D_MODEL, BF16), (D_MODEL, BF16), (ATT_Q, BF16), (RET_V, F32),
            (RET_V, BF16)]
    return _row_call(body, tm=256, row_ins=[dx1b, z_a, z_r, ba, br, g_r, o_ret], res_ins=[wout, wba, wbr],
                     row_outs=outs, name="mix_bwd")


def _proj_bwd(d_pieces, x, dx1, w_in, g1):
    groups = (P_QA, P_KVA, P_QR, P_KR, P_VR, P_GR, P_ZA, P_ZR)
    n_p = len(groups)

    def body(ri, re, ro, po):
        dh = None
        for k, (off, w) in enumerate(groups):
            term = _dot_nt(ri[k][...], re[0][:, off:off + w])
            dh = term if dh is None else dh + term
        r, xhat = _rms_stats(ri[n_p][...])
        dxn, dgain = _rms_bwd(dh, xhat, r, re[1][...])
        ro[0][...] = ri[n_p + 1][...] + dxn
        po[0][0] = jnp.sum(dgain, axis=0, keepdims=True)

    return _row_call(body, tm=256, row_ins=[*d_pieces, x, dx1], res_ins=[w_in, g1], row_outs=[(D_MODEL, F32)],
                     part_outs=(D_MODEL,), name="proj_bwd")


def _matmul_tn(a, b, *, tm, tn, name):
    t, m = a.shape
    n = b.shape[1]
    tk = min(2048, t)

    def kern(a_ref, b_ref, o_ref):
        k = pl.program_id(2)

        @pl.when(k == 0)
        def _():
            o_ref[...] = jnp.zeros_like(o_ref)

        o_ref[...] += _dot_tn(a_ref[...], b_ref[...])

    return pl.pallas_call(
        kern, grid=(m // tm, n // tn, t // tk),
        in_specs=[pl.BlockSpec((tk, tm), lambda i, j, k: (k, i)), pl.BlockSpec((tk, tn), lambda i, j, k: (k, j))],
        out_specs=pl.BlockSpec((tm, tn), lambda i, j, k: (i, j)),
        out_shape=jax.ShapeDtypeStruct((m, n), F32), name=name,
        compiler_params=_params(("parallel", "parallel", "arbitrary")),
    )(a, b)


def _attn_group(n, kvh, q_ref, kvp_ref, kvc_ref, gq, gk, sink_ref):
    heads = [kvh * GROUP + g for g in range(GROUP)]
    q = jnp.concatenate([q_ref[:, h * HEAD_DIM:(h + 1) * HEAD_DIM] for h in heads], axis=0)
    rq, qhat = _rms_stats(q)
    qnb = _bf(qhat * gq)
    kcols = slice(kvh * HEAD_DIM, (kvh + 1) * HEAD_DIM)
    vcols = slice(ATT_KV + kvh * HEAD_DIM, ATT_KV + (kvh + 1) * HEAD_DIM)
    k = jnp.concatenate([kvp_ref[:, kcols], kvc_ref[:, kcols]], axis=0)
    rk, khat = _rms_stats(k)
    knb = _bf(khat * gk)
    vb = _bf(jnp.concatenate([kvp_ref[:, vcols], kvc_ref[:, vcols]], axis=0))
    s = _dot_nt(qnb, knb) * (HEAD_DIM ** -0.5)
    rows = GROUP * BLOCK
    i = lax.broadcasted_iota(jnp.int32, (rows, 2 * BLOCK), 0) & (BLOCK - 1)
    j = lax.broadcasted_iota(jnp.int32, (rows, 2 * BLOCK), 1)
    allowed = (j > i) & (j <= i + BLOCK) & ((j >= BLOCK) | (n > 0))
    s = jnp.where(allowed, s, -1e30)
    sink = jnp.concatenate([jnp.broadcast_to(sink_ref[0:1, h:h + 1], (BLOCK, 1)) for h in heads], axis=0)
    m = jnp.maximum(jnp.max(s, axis=1, keepdims=True), sink)
    e = jnp.exp(s - m)
    es = jnp.exp(sink - m)
    z = jnp.sum(e, axis=1, keepdims=True) + es
    return dict(heads=heads, qhat=qhat, rq=rq, qnb=qnb, khat=khat, rk=rk, knb=knb, vb=vb, p=e / z, psink=es / z)


def _attn_fwd(q_a, kv_a, gq, gk, sinks):
    t = q_a.shape[0]
    nb = t // BLOCK

    def kern(q_ref, kvp_ref, kvc_ref, gq_ref, gk_ref, sink_ref, o_ref):
        n = pl.program_id(0)
        for kvh in range(N_KV_HEADS):
            a = _attn_group(n, kvh, q_ref, kvp_ref, kvc_ref, gq_ref[...], gk_ref[...], sink_ref)
            out = _dot(_bf(a["p"]), a["vb"])
            for g, h in enumerate(a["heads"]):
                o_ref[:, h * HEAD_DIM:(h + 1) * HEAD_DIM] = _bf(out[g * BLOCK:(g + 1) * BLOCK, :])

    small = lambda a: pl.BlockSpec(a.shape, lambda n: (0, 0))
    return pl.pallas_call(
        kern, grid=(nb,),
        in_specs=[pl.BlockSpec((BLOCK, ATT_Q), lambda n: (n, 0)),
                  pl.BlockSpec((BLOCK, 2 * ATT_KV), lambda n: (jnp.maximum(n - 1, 0), 0)),
                  pl.BlockSpec((BLOCK, 2 * ATT_KV), lambda n: (n, 0)),
                  small(gq), small(gk), small(sinks)],
        out_specs=pl.BlockSpec((BLOCK, ATT_Q), lambda n: (n, 0)),
        out_shape=jax.ShapeDtypeStruct((t, ATT_Q), BF16), name="attn_fwd",
        compiler_params=_params(("parallel",)),
    )(q_a, kv_a, kv_a, gq, gk, sinks)


def _attn_bwd(q_a, kv_a, d_attn, gq, gk, sinks):
    t = q_a.shape[0]
    nb = t // BLOCK

    def kern(q_ref, kvp_ref, kvc_ref, do_ref, gq_ref, gk_ref, sink_ref,
             dq_ref, dkv_ref, dgq_ref, dgk_ref, dsink_ref, band_k, band_v, carry_k, carry_v):
        n = pl.program_id(0)
        gq_v = gq_ref[...]
        gk_v = gk_ref[...]

        @pl.when(n == 0)
        def _():
            carry_k[...] = jnp.zeros_like(carry_k)
            carry_v[...] = jnp.zeros_like(carry_v)
            dgq_ref[...] = jnp.zeros_like(dgq_ref)
            dgk_ref[...] = jnp.zeros_like(dgk_ref)
            dsink_ref[...] = jnp.zeros_like(dsink_ref)

        @pl.when(n == nb)
        def _():
            band_k[...] = jnp.zeros_like(band_k)
            band_v[...] = jnp.zeros_like(band_v)

        @pl.when(n < nb)
        def _():
            lane16 = lax.broadcasted_iota(jnp.int32, (1, N_Q_HEADS), 1)
            dsink = jnp.zeros((1, N_Q_HEADS), F32)
            dgq = jnp.zeros((1, HEAD_DIM), F32)
            for kvh in range(N_KV_HEADS):
                a = _attn_group(n, kvh, q_ref, kvp_ref, kvc_ref, gq_v, gk_v, sink_ref)
                p = a["p"]
                dob = jnp.concatenate([do_ref[:, h * HEAD_DIM:(h + 1) * HEAD_DIM] for h in a["heads"]], axis=0)
                dp = _dot_nt(dob, a["vb"])
                delta = jnp.sum(p * dp, axis=1, keepdims=True)
                dsb = _bf(p * (dp - delta))
                dsk = a["psink"] * delta
                for g, h in enumerate(a["heads"]):
                    tot = jnp.sum(dsk[g * BLOCK:(g + 1) * BLOCK, :], axis=0, keepdims=True)
                    dsink = dsink - jnp.where(lane16 == h, tot, 0.0)
                scale = HEAD_DIM ** -0.5
                dqn = _dot(dsb, a["knb"]) * scale
                band_k[kvh] = _dot_tn(dsb, a["qnb"]) * scale
                band_v[kvh] = _dot_tn(_bf(p), dob)
                dq, dgain = _rms_bwd(dqn, a["qhat"], a["rq"], gq_v)
                dgq = dgq + jnp.sum(dgain, axis=0, keepdims=True)
                for g, h in enumerate(a["heads"]):
                    dq_ref[:, h * HEAD_DIM:(h + 1) * HEAD_DIM] = _bf(dq[g * BLOCK:(g + 1) * BLOCK, :])
            dsink_ref[...] += dsink
            dgq_ref[...] += dgq

        dgk = jnp.zeros((1, HEAD_DIM), F32)
        for kvh in range(N_KV_HEADS):
            kcols = slice(kvh * HEAD_DIM, (kvh + 1) * HEAD_DIM)
            vcols = slice(ATT_KV + kvh * HEAD_DIM, ATT_KV + (kvh + 1) * HEAD_DIM)
            dkn = carry_k[kvh] + band_k[kvh, 0:BLOCK, :]
            dv = carry_v[kvh] + band_v[kvh, 0:BLOCK, :]
            rk, khat = _rms_stats(kvp_ref[:, kcols])
            dk, dgain = _rms_bwd(dkn, khat, rk, gk_v)
            dgk = dgk + jnp.sum(dgain, axis=0, keepdims=True)
            dkv_ref[:, kcols] = _bf(dk)
            dkv_ref[:, vcols] = _bf(dv)
            carry_k[kvh] = band_k[kvh, BLOCK:2 * BLOCK, :]
            carry_v[kvh] = band_v[kvh, BLOCK:2 * BLOCK, :]
        dgk_ref[...] += dgk

    small = lambda a: pl.BlockSpec(a.shape, lambda n: (0, 0))
    last = nb - 1
    return pl.pallas_call(
        kern, grid=(nb + 1,),
        in_specs=[pl.BlockSpec((BLOCK, ATT_Q), lambda n: (jnp.minimum(n, last), 0)),
                  pl.BlockSpec((BLOCK, 2 * ATT_KV), lambda n: (jnp.maximum(n - 1, 0), 0)),
                  pl.BlockSpec((BLOCK, 2 * ATT_KV), lambda n: (jnp.minimum(n, last), 0)),
                  pl.BlockSpec((BLOCK, ATT_Q), lambda n: (jnp.minimum(n, last), 0)),
                  small(gq), small(gk), small(sinks)],
        out_specs=[pl.BlockSpec((BLOCK, ATT_Q), lambda n: (jnp.minimum(n, last), 0)),
                   pl.BlockSpec((BLOCK, 2 * ATT_KV), lambda n: (jnp.maximum(n - 1, 0), 0)),
                   pl.BlockSpec((1, HEAD_DIM), lambda n: (0, 0)),
                   pl.BlockSpec((1, HEAD_DIM), lambda n: (0, 0)),
                   pl.BlockSpec((1, N_Q_HEADS), lambda n: (0, 0))],
        out_shape=[jax.ShapeDtypeStruct((t, ATT_Q), BF16), jax.ShapeDtypeStruct((t, 2 * ATT_KV), BF16),
                   jax.ShapeDtypeStruct((1, HEAD_DIM), F32), jax.ShapeDtypeStruct((1, HEAD_DIM), F32),
                   jax.ShapeDtypeStruct((1, N_Q_HEADS), F32)],
        scratch_shapes=[pltpu.VMEM((N_KV_HEADS, 2 * BLOCK, HEAD_DIM), F32),
                        pltpu.VMEM((N_KV_HEADS, 2 * BLOCK, HEAD_DIM), F32),
                        pltpu.VMEM((N_KV_HEADS, BLOCK, HEAD_DIM), F32),
                        pltpu.VMEM((N_KV_HEADS, BLOCK, HEAD_DIM), F32)],
        name="attn_bwd", compiler_params=_params(("arbitrary",)),
    )(q_a, kv_a, kv_a, d_attn, gq, gk, sinks)


def _ret_tables(t):
    pos = jnp.arange(t, dtype=F32)
    theta = 1.0 / (RET_ROT_BASE ** jnp.linspace(0.0, 1.0, RET_QK_DIM // 2, dtype=F32))
    ang = jnp.repeat(pos[:, None] * theta[None, :], 2, axis=-1)
    sign = jnp.tile(jnp.array([-1.0, 1.0], F32), RET_QK_DIM // 2)
    log_gamma = jnp.log(1.0 - 2.0 ** (-5.0 - jnp.arange(RET_HEADS, dtype=F32)))
    i = jnp.arange(RET_CHUNK, dtype=F32)
    diff = i[:, None] - i[None, :]
    causal = diff >= 0
    decay = jnp.where(causal[None], jnp.exp(jnp.where(causal, diff, 0.0)[None] * log_gamma[:, None, None]), 0.0)
    xi = jnp.exp((i + 1.0)[None, :] * log_gamma[:, None])[:, :, None]
    zeta = jnp.exp((RET_CHUNK - 1.0 - i)[None, :] * log_gamma[:, None])[:, :, None]
    gch = jnp.broadcast_to(jnp.exp(RET_CHUNK * log_gamma)[:, None, None], (RET_HEADS, 1, 128))
    return jnp.cos(ang), jnp.sin(ang) * sign[None, :], decay, xi, zeta, gch


def _swap_pairs(x):
    lane = lax.broadcasted_iota(jnp.int32, x.shape, 1)
    return jnp.where((lane & 1) == 0, pltpu.roll(x, RET_QK_DIM - 1, 1), pltpu.roll(x, 1, 1))


def _rotate(x, cos, sin_s):
    return x * cos + _swap_pairs(x) * sin_s


def _rotate_bwd(dy, cos, sin_s):
    return dy * cos + _swap_pairs(dy * sin_s)


def _ret_specs(order):
    qk = pl.BlockSpec((RET_CHUNK, RET_QK_DIM), lambda h, j: (order(j), h))
    v = pl.BlockSpec((RET_CHUNK, RET_V_DIM), lambda h, j: (order(j), h))
    pos = pl.BlockSpec((RET_CHUNK, RET_QK_DIM), lambda h, j: (order(j), 0))
    dec = pl.BlockSpec((None, RET_CHUNK, RET_CHUNK), lambda h, j: (h, 0, 0))
    col = pl.BlockSpec((None, RET_CHUNK, 1), lambda h, j: (h, 0, 0))
    gch = pl.BlockSpec((None, 1, 128), lambda h, j: (h, 0, 0))
    return qk, v, pos, dec, col, gch


def _ret_fwd(q_r, k_r, v_r, g_r, tables):
    t = q_r.shape[0]
    nc = t // RET_CHUNK
    cos, sin_s, decay, xi, zeta, gch = tables

    def kern(q_ref, k_ref, v_ref, g_ref, cos_ref, sin_ref, dec_ref, xi_ref, zeta_ref, gch_ref,
             o_ref, ret_ref, st_ref, state):
        c = pl.program_id(1)

        @pl.when(c == 0)
        def _():
            state[...] = jnp.zeros_like(state)

        cos_t = cos_ref[...]
        sin_t = sin_ref[...]
        qs = _bf(_rotate(q_ref[...], cos_t, sin_t))
        ks = _rotate(k_ref[...] * (RET_QK_DIM ** -0.5), cos_t, sin_t)
        vb = v_ref[...]
        s_old = state[...]
        sb = _bf(s_old)
        st_ref[...] = sb
        inner = _dot_nt(qs, _bf(ks)) * dec_ref[...]
        out = _dot(_bf(inner), vb) + _dot(qs, sb) * xi_ref[...]
        state[...] = gch_ref[:, 0:1] * s_old + _dot_tn(_bf(ks * zeta_ref[...]), vb)
        o_ref[...] = out
        r, rn = _rms_stats(out)
        g = g_ref[...]
        ret_ref[...] = _bf(g * jax.nn.sigmoid(g) * rn)

    qk, v, pos, dec, col, gsp = _ret_specs(lambda j: j)
    return pl.pallas_call(
        kern, grid=(RET_HEADS, nc),
        in_specs=[qk, qk, v, v, pos, pos, dec, col, col, gsp],
        out_specs=[v, v, pl.BlockSpec((None, None, RET_QK_DIM, RET_V_DIM), lambda h, j: (h, j, 0, 0))],
        out_shape=[jax.ShapeDtypeStruct((t, RET_V), F32), jax.ShapeDtypeStruct((t, RET_V), BF16),
                   jax.ShapeDtypeStruct((RET_HEADS, nc, RET_QK_DIM, RET_V_DIM), BF16)],
        scratch_shapes=[pltpu.VMEM((RET_QK_DIM, RET_V_DIM), F32)],
        name="ret_fwd", compiler_params=_params(("parallel", "arbitrary")),
    )(q_r, k_r, v_r, g_r, cos, sin_s, decay, xi, zeta, gch)


def _ret_bwd(q_r, k_r, v_r, d_o, states, tables):
    t = q_r.shape[0]
    nc = t // RET_CHUNK
    cos, sin_s, decay, xi, zeta, gch = tables

    def kern(q_ref, k_ref, v_ref, do_ref, st_ref, cos_ref, sin_ref, dec_ref, xi_ref, zeta_ref, gch_ref,
             dq_ref, dk_ref, dv_ref, dstate):
        j = pl.program_id(1)

        @pl.when(j == 0)
        def _():
            dstate[...] = jnp.zeros_like(dstate)

        cos_t = cos_ref[...]
        sin_t = sin_ref[...]
        scale = RET_QK_DIM ** -0.5
        qs = _bf(_rotate(q_ref[...], cos_t, sin_t))
        ks = _rotate(k_ref[...] * scale, cos_t, sin_t)
        ksb = _bf(ks)
        vb = v_ref[...]
        d_o_t = do_ref[...]
        dob = _bf(d_o_t)
        doxb = _bf(d_o_t * xi_ref[...])
        dec = dec_ref[...]
        ds_old = dstate[...]
        dsb = _bf(ds_old)
        pb = _bf(_dot_nt(qs, ksb) * dec)
        dpb = _bf(_dot_nt(dob, vb) * dec)
        dqs = _dot(dpb, ksb) + _dot_nt(doxb, st_ref[...])
        dks = _dot_tn(dpb, qs) + _dot_nt(vb, dsb) * zeta_ref[...]
        dv_ref[...] = _bf(_dot_tn(pb, dob) + _dot(_bf(ks * zeta_ref[...]), dsb))
        dstate[...] = gch_ref[:, 0:1] * ds_old + _dot_tn(qs, doxb)
        dq_ref[...] = _bf(_rotate_bwd(dqs, cos_t, sin_t))
        dk_ref[...] = _bf(_rotate_bwd(dks, cos_t, sin_t) * scale)

    rev = lambda j: nc - 1 - j
    qk, v, pos, dec, col, gsp = _ret_specs(rev)
    return pl.pallas_call(
        kern, grid=(RET_HEADS, nc),
        in_specs=[qk, qk, v, v, pl.BlockSpec((None, None, RET_QK_DIM, RET_V_DIM), lambda h, j: (h, rev(j), 0, 0)),
                  pos, pos, dec, col, col, gsp],
        out_specs=[qk, qk, v],
        out_shape=[jax.ShapeDtypeStruct((t, RET_QK), BF16), jax.ShapeDtypeStruct((t, RET_QK), BF16),
                   jax.ShapeDtypeStruct((t, RET_V), BF16)],
        scratch_shapes=[pltpu.VMEM((RET_QK_DIM, RET_V_DIM), F32)],
        name="ret_bwd", compiler_params=_params(("parallel", "arbitrary")),
    )(q_r, k_r, v_r, d_o, states, cos, sin_s, decay, xi, zeta, gch)


def _local_step(x, target, w, g1, g2, gq, gk, sinks):
    t = x.shape[0]
    tables = _ret_tables(t)
    h1, q_a, kv_a, q_r, k_r, v_r, g_r, z_a, z_r = _proj_fwd(x, g1, w["w_in"])
    attn = _attn_fwd(q_a, kv_a, gq, gk, sinks)
    o_ret, ret, states = _ret_fwd(q_r, k_r, v_r, g_r, tables)
    ba, br, merged, x1, h2 = _mix_fwd(attn, ret, z_a, z_r, x, w["wba"], w["wbr"], w["wout"], g2)
    act, dgate, dup, dyb, dx1, dx1b, loss_p, dg2_p = _ffn_fwd_bwd(h2, x1, target, w["wg"], w["wu"], w["wd"], g2)
    dba, dbr, dz_a, dz_r, d_attn, d_o, dg_r = _mix_bwd(dx1b, z_a, z_r, ba, br, g_r, o_ret, w["wout"], w["wba"], w["wbr"])
    dq_r, dk_r, dv_r = _ret_bwd(q_r, k_r, v_r, d_o, states, tables)
    dq_a, dkv_a, dgq, dgk, dsinks = _attn_bwd(q_a, kv_a, d_attn, gq, gk, sinks)
    d_pieces = [dq_a, dkv_a, dq_r, dk_r, dv_r, dg_r, dz_a, dz_r]
    grad_x, dg1_p = _proj_bwd(d_pieces, x, dx1, w["w_in"], g1)
    dw_in = [_matmul_tn(h1, dp, tm=D_MODEL, tn=min(dp.shape[1], 1024), name=f"dw_in_{k}")
             for k, dp in enumerate(d_pieces)]
    grads = dict(
        w_in=dw_in,
        wba=_matmul_tn(attn, dba, tm=1024, tn=1024, name="dw_ba"),
        wbr=_matmul_tn(ret, dbr, tm=1024, tn=1024, name="dw_br"),
        wout=_matmul_tn(merged, dx1b, tm=1024, tn=1024, name="dw_out"),
        wg=_matmul_tn(h2, dgate, tm=1024, tn=D_FF // 2, name="dw_gate"),
        wu=_matmul_tn(h2, dup, tm=1024, tn=D_FF // 2, name="dw_up"),
        wd=_matmul_tn(act, dyb, tm=D_FF // 2, tn=1024, name="dw_down"),
    )
    small = dict(loss=loss_p, dg1=dg1_p, dg2=dg2_p, dgq=dgq, dgk=dgk, dsinks=dsinks)
    return grad_x, grads, small


def _position():
    return lax.axis_index("x"), lax.axis_index("y"), lax.axis_index("c")


_ANY = pl.BlockSpec(memory_space=pl.ANY)


def _gather_packs(pack):
    rows, width = pack.shape
    half = rows // 2

    def body(p_ref, o_ref, send_sems, recv_sems, local_sem):
        x, y, c = _position()
        me = (x, y, c)
        sibling = (x, y, 1 - c)
        chips = [(1 - x, y), (x, 1 - y), (1 - x, 1 - y)]
        my_chip = 2 * x + y

        def slab(chip, hf):
            return o_ref.at[chip, pl.ds(hf * half, half), :]

        def copy(k, src, dst, to):
            return pltpu.make_async_remote_copy(src_ref=src, dst_ref=dst, send_sem=send_sems.at[k],
                                                recv_sem=recv_sems.at[k], device_id=to, device_id_type=MESH)

        mine = pltpu.make_async_copy(p_ref, o_ref.at[my_chip], local_sem)
        mine.start()
        first = [copy(k, p_ref.at[pl.ds(c * half, half), :], slab(my_chip, c), (cx, cy, c))
                 for k, (cx, cy) in enumerate(chips)]
        for cp in first:
            cp.start()
        passed = []
        for k, (cx, cy) in enumerate(chips):
            landed = slab(2 * cx + cy, c)
            copy(k, landed, landed, me).wait_recv()
            cp = copy(3 + k, landed, landed, sibling)
            cp.start()
            passed.append(cp)
        for k, (cx, cy) in enumerate(chips):
            from_sibling = slab(2 * cx + cy, 1 - c)
            copy(3 + k, from_sibling, from_sibling, me).wait_recv()
        for cp in first + passed:
            cp.wait_send()
        mine.wait()

    return pl.pallas_call(
        body, in_specs=[_ANY], out_specs=_ANY,
        out_shape=jax.ShapeDtypeStruct((N_CHIPS, rows, width), pack.dtype),
        scratch_shapes=[pltpu.SemaphoreType.DMA((6,)), pltpu.SemaphoreType.DMA((6,)), pltpu.SemaphoreType.DMA],
        name="gather_weights",
    )(pack)


def _pair_exchange(g):
    n, rows, width = g.shape
    half = rows // 2

    def body(g_ref, o_ref, send_sems, recv_sems):
        x, y, c = _position()
        copies = [pltpu.make_async_remote_copy(
            src_ref=g_ref.at[k, pl.ds((1 - c) * half, half), :], dst_ref=o_ref.at[k], send_sem=send_sems.at[k],
            recv_sem=recv_sems.at[k], device_id=(x, y, 1 - c), device_id_type=MESH) for k in range(n)]
        for cp in copies:
            cp.start()
        for cp in copies:
            cp.wait()

    return pl.pallas_call(
        body, in_specs=[_ANY], out_specs=_ANY, out_shape=jax.ShapeDtypeStruct((n, half, width), g.dtype),
        scratch_shapes=[pltpu.SemaphoreType.DMA((n,)), pltpu.SemaphoreType.DMA((n,))], name="pair_exchange",
    )(g)


def _pair_sum(g, from_sibling, c_arr):
    n, rows, width = g.shape
    tiles = (rows // 2) // ROW_TILE

    def kern(c_ref, g_ref, s_ref, o_ref):
        o_ref[...] = _bf(g_ref[...] + s_ref[...])

    return pl.pallas_call(
        kern,
        grid_spec=pltpu.PrefetchScalarGridSpec(
            num_scalar_prefetch=1, grid=(n, tiles),
            in_specs=[pl.BlockSpec((None, ROW_TILE, width), lambda k, i, c: (k, c[0] * tiles + i, 0)),
                      pl.BlockSpec((None, ROW_TILE, width), lambda k, i, c: (k, i, 0))],
            out_specs=pl.BlockSpec((None, ROW_TILE, width), lambda k, i, c: (k, i, 0))),
        out_shape=jax.ShapeDtypeStruct((n, rows // 2, width), BF16), name="pair_sum",
        compiler_params=_params(("parallel", "parallel")),
    )(c_arr, g, from_sibling)


def _scatter_to_owners(hsum):
    n, half, width = hsum.shape

    def body(h_ref, o_ref, send_sems, recv_sems, local_sem):
        x, y, c = _position()
        chips = [(1 - x, y), (x, 1 - y), (1 - x, 1 - y)]
        my_chip = 2 * x + y
        mine = pltpu.make_async_copy(h_ref.at[my_chip], o_ref.at[my_chip], local_sem)
        mine.start()
        copies = [pltpu.make_async_remote_copy(
            src_ref=h_ref.at[2 * cx + cy], dst_ref=o_ref.at[my_chip], send_sem=send_sems.at[k],
            recv_sem=recv_sems.at[k], device_id=(cx, cy, c), device_id_type=MESH) for k, (cx, cy) in enumerate(chips)]
        for cp in copies:
            cp.start()
        for k, (cx, cy) in enumerate(chips):
            landed = o_ref.at[2 * cx + cy]
            pltpu.make_async_remote_copy(src_ref=landed, dst_ref=landed, send_sem=send_sems.at[k],
                                         recv_sem=recv_sems.at[k], device_id=(x, y, c),
                                         device_id_type=MESH).wait_recv()
        for cp in copies:
            cp.wait_send()
        mine.wait()

    return pl.pallas_call(
        body, in_specs=[_ANY], out_specs=_ANY, out_shape=jax.ShapeDtypeStruct((n, half, width), hsum.dtype),
        scratch_shapes=[pltpu.SemaphoreType.DMA((3,)), pltpu.SemaphoreType.DMA((3,)), pltpu.SemaphoreType.DMA],
        name="scatter_to_owners",
    )(hsum)


def _sum_chips(parts):
    n, half, width = parts.shape

    def kern(p_ref, o_ref):
        acc = p_ref[0].astype(F32)
        for k in range(1, n):
            acc = acc + p_ref[k].astype(F32)
        o_ref[...] = acc

    return pl.pallas_call(
        kern, grid=(half // ROW_TILE,),
        in_specs=[pl.BlockSpec((n, ROW_TILE, width), lambda i: (0, i, 0))],
        out_specs=pl.BlockSpec((ROW_TILE, width), lambda i: (i, 0)),
        out_shape=jax.ShapeDtypeStruct((half, width), F32), name="sum_chips",
        compiler_params=_params(("parallel",)),
    )(parts)


def _share_halves(fhalf):
    half, width = fhalf.shape

    def body(f_ref, o_ref, send_sem, recv_sem, local_sem):
        x, y, c = _position()
        mine = pltpu.make_async_copy(f_ref, o_ref.at[pl.ds(c * half, half), :], local_sem)
        mine.start()
        cp = pltpu.make_async_remote_copy(
            src_ref=f_ref, dst_ref=o_ref.at[pl.ds(c * half, half), :], send_sem=send_sem, recv_sem=recv_sem,
            device_id=(x, y, 1 - c), device_id_type=MESH)
        cp.start()
        other = o_ref.at[pl.ds((1 - c) * half, half), :]
        pltpu.make_async_remote_copy(src_ref=other, dst_ref=other, send_sem=send_sem, recv_sem=recv_sem,
                                     device_id=(x, y, c), device_id_type=MESH).wait_recv()
        cp.wait_send()
        mine.wait()

    return pl.pallas_call(
        body, in_specs=[_ANY], out_specs=_ANY, out_shape=jax.ShapeDtypeStruct((2 * half, width), fhalf.dtype),
        scratch_shapes=[pltpu.SemaphoreType.DMA, pltpu.SemaphoreType.DMA, pltpu.SemaphoreType.DMA],
        name="share_halves",
    )(fhalf)


def _adamw_math(w, g, m, v):
    m = ADAM_B1 * m + (1.0 - ADAM_B1) * g
    v = ADAM_B2 * v + (1.0 - ADAM_B2) * (g * g)
    m_hat = m / (1.0 - ADAM_B1 ** ADAM_STEP)
    v_hat = v / (1.0 - ADAM_B2 ** ADAM_STEP)
    delta = -ADAM_LR * (m_hat / (jnp.sqrt(v_hat) + ADAM_EPS) + ADAM_WD * w)
    return delta, m, v


def _adamw(w, g, m, v):
    rows, width = w.shape

    def kern(w_ref, g_ref, m_ref, v_ref, d_ref, nm_ref, nv_ref):
        d_ref[...], nm_ref[...], nv_ref[...] = _adamw_math(w_ref[...], g_ref[...], m_ref[...], v_ref[...])

    spec = pl.BlockSpec((ROW_TILE, width), lambda i: (i, 0))
    return pl.pallas_call(
        kern, grid=(rows // ROW_TILE,), in_specs=[spec] * 4, out_specs=[spec] * 3,
        out_shape=[jax.ShapeDtypeStruct((rows, width), F32)] * 3, name="adamw",
        compiler_params=_params(("parallel",)),
    )(w, g, m, v)


def _small_step(part, tile_sums, w, m, v):
    loss_p, dg1_p, dg2_p = tile_sums

    def body(part_ref, loss_ref, dg1_ref, dg2_ref, w_ref, m_ref, v_ref, g_ref, d_ref, nm_ref, nv_ref,
             mine, gathered, send_sems, recv_sems):
        x, y, c = _position()
        me = 4 * x + 2 * y + c
        mine[...] = part_ref[...]
        for r in range(8):
            lanes = slice(128 * r, 128 * (r + 1))
            mine[SM_G1 + r:SM_G1 + r + 1, :] = jnp.sum(dg1_ref[:, lanes], axis=0, keepdims=True)
            mine[SM_G2 + r:SM_G2 + r + 1, :] = jnp.sum(dg2_ref[:, lanes], axis=0, keepdims=True)
        mine[SM_LOSS:SM_LOSS + 1, :] = jnp.sum(loss_ref[...], axis=0, keepdims=True)
        copies = []
        for k in range(1, N_DEV):
            flip = (k >> 2) & 1, (k >> 1) & 1, k & 1
            to = (x ^ flip[0], y ^ flip[1], c ^ flip[2])
            cp = pltpu.make_async_remote_copy(
                src_ref=mine, dst_ref=gathered.at[me], send_sem=send_sems.at[k - 1], recv_sem=recv_sems.at[k - 1],
                device_id=to, device_id_type=MESH)
            cp.start()
            copies.append(cp)
        gathered[me] = mine[...]
        for k in range(1, N_DEV):
            flip = (k >> 2) & 1, (k >> 1) & 1, k & 1
            src = 4 * (x ^ flip[0]) + 2 * (y ^ flip[1]) + (c ^ flip[2])
            pltpu.make_async_remote_copy(
                src_ref=mine, dst_ref=gathered.at[src], send_sem=send_sems.at[k - 1], recv_sem=recv_sems.at[k - 1],
                device_id=(x, y, c), device_id_type=MESH).wait_recv()
        for cp in copies:
            cp.wait_send()
        total = gathered[0]
        for k in range(1, N_DEV):
            total = total + gathered[k]
        g_ref[...] = total
        d_ref[...], nm_ref[...], nv_ref[...] = _adamw_math(w_ref[...], total, m_ref[...], v_ref[...])

    vm = pl.BlockSpec(memory_space=pltpu.VMEM)
    blk = jax.ShapeDtypeStruct((SMALL_ROWS, 128), F32)
    return pl.pallas_call(
        body, in_specs=[vm] * 7, out_specs=[vm] * 4, out_shape=[blk] * 4,
        scratch_shapes=[pltpu.VMEM((SMALL_ROWS, 128), F32), pltpu.VMEM((N_DEV, SMALL_ROWS, 128), F32),
                        pltpu.SemaphoreType.DMA((N_DEV - 1,)), pltpu.SemaphoreType.DMA((N_DEV - 1,))],
        name="small_step",
    )(part, loss_p.reshape(-1, 128), dg1_p.reshape(-1, D_MODEL), dg2_p.reshape(-1, D_MODEL), w, m, v)


def _pack_shard(w_in, wba, wbr, wout, wg, wu, wd):
    return jnp.concatenate([a.reshape(-1, D_MODEL) for a in (w_in, wba, wbr, wout, wg, wu, wd)], axis=0)


def _unpack_shard(p):
    seg = lambda k: p[PACK_OFFS[k]:PACK_OFFS[k] + PACK_SEGS[k]]
    return (seg(0).reshape(D_MODEL, W_IN_SH), seg(1), seg(2), seg(3), seg(4).reshape(D_MODEL, FF_SH),
            seg(5).reshape(D_MODEL, FF_SH), seg(6))


def _full_weights(gathered):
    per_chip = [_unpack_shard(gathered[k]) for k in range(N_CHIPS)]
    cat = lambda i, axis: jnp.concatenate([pc[i] for pc in per_chip], axis=axis)
    return dict(w_in=cat(0, 1), wba=cat(1, 0), wbr=cat(2, 0), wout=cat(3, 0), wg=cat(4, 1), wu=cat(5, 1), wd=cat(6, 0))


def _pack_grads(grads):
    dw_in = jnp.concatenate(grads["w_in"], axis=1)
    slabs = []
    for k in range(N_CHIPS):
        slabs.append(_pack_shard(
            dw_in[:, k * W_IN_SH:(k + 1) * W_IN_SH], grads["wba"][k * 256:(k + 1) * 256],
            grads["wbr"][k * 512:(k + 1) * 512], grads["wout"][k * 256:(k + 1) * 256],
            grads["wg"][:, k * FF_SH:(k + 1) * FF_SH], grads["wu"][:, k * FF_SH:(k + 1) * FF_SH],
            grads["wd"][k * FF_SH:(k + 1) * FF_SH]))
    return jnp.stack(slabs, axis=0)


def _pack_small(g1, g2, gq, gk, sinks):
    blk = jnp.zeros((SMALL_ROWS, 128), F32)
    blk = blk.at[SM_G1:SM_G1 + 8].set(g1.reshape(8, 128))
    blk = blk.at[SM_G2:SM_G2 + 8].set(g2.reshape(8, 128))
    blk = blk.at[SM_GQ, :HEAD_DIM].set(gq.reshape(-1))
    blk = blk.at[SM_GK, :HEAD_DIM].set(gk.reshape(-1))
    blk = blk.at[SM_SINK, :N_Q_HEADS].set(sinks.reshape(-1))
    return blk


def _unpack_small(blk):
    return (blk[SM_G1:SM_G1 + 8].reshape(1, D_MODEL), blk[SM_G2:SM_G2 + 8].reshape(1, D_MODEL),
            blk[SM_GQ, :HEAD_DIM].reshape(1, HEAD_DIM), blk[SM_GK, :HEAD_DIM].reshape(1, HEAD_DIM),
            blk[SM_SINK, :N_Q_HEADS].reshape(1, N_Q_HEADS))


def kernel(x, norm_mix_gain, w_in, q_norm_gain, k_norm_gain, attn_sinks, w_branch_attn, w_branch_ret, w_out, norm_ffn_gain, w_ffn_gate, w_ffn_up, w_ffn_down, loss_target, m_norm_mix_gain, m_w_in, m_q_norm_gain, m_k_norm_gain, m_attn_sinks, m_w_branch_attn, m_w_branch_ret, m_w_out, m_norm_ffn_gain, m_w_ffn_gate, m_w_ffn_up, m_w_ffn_down, v_norm_mix_gain, v_w_in, v_q_norm_gain, v_k_norm_gain, v_attn_sinks, v_w_branch_attn, v_w_branch_ret, v_w_out, v_norm_ffn_gain, v_w_ffn_gate, v_w_ffn_up, v_w_ffn_down):
    big = lambda *ws: _pack_shard(*[a[0] for a in ws])
    w_pack = big(w_in, w_branch_attn, w_branch_ret, w_out, w_ffn_gate, w_ffn_up, w_ffn_down)
    m_pack = big(m_w_in, m_w_branch_attn, m_w_branch_ret, m_w_out, m_w_ffn_gate, m_w_ffn_up, m_w_ffn_down)
    v_pack = big(v_w_in, v_w_branch_attn, v_w_branch_ret, v_w_out, v_w_ffn_gate, v_w_ffn_up, v_w_ffn_down)

    weights = _full_weights(_gather_packs(w_pack.astype(BF16)))
    grad_x, grads, small = _local_step(x[0], loss_target[0], weights, norm_mix_gain, norm_ffn_gain, q_norm_gain,
                                       k_norm_gain, attn_sinks)

    g_pack = _pack_grads(grads)
    c_arr = lax.axis_index("c").astype(jnp.int32).reshape(1)
    chip_sum = _pair_sum(g_pack, _pair_exchange(g_pack), c_arr)
    g_shard = _share_halves(_sum_chips(_scatter_to_owners(chip_sum)))
    d_shard, nm_shard, nv_shard = _adamw(w_pack, g_shard, m_pack, v_pack)

    zeros = jnp.zeros((1, D_MODEL), F32)
    part = _pack_small(zeros, zeros, small["dgq"], small["dgk"], small["dsinks"])
    sm_w = _pack_small(norm_mix_gain, norm_ffn_gain, q_norm_gain, k_norm_gain, attn_sinks)
    sm_m = _pack_small(m_norm_mix_gain, m_norm_ffn_gain, m_q_norm_gain, m_k_norm_gain, m_attn_sinks)
    sm_v = _pack_small(v_norm_mix_gain, v_norm_ffn_gain, v_q_norm_gain, v_k_norm_gain, v_attn_sinks)
    sm_g, sm_d, sm_nm, sm_nv = _small_step(part, (small["loss"], small["dg1"], small["dg2"]), sm_w, sm_m, sm_v)
    loss = sm_g[SM_LOSS, 0]

    def leaves(shard, sm):
        b = [a[None] for a in _unpack_shard(shard)]
        s1, s2, sq, sk, ss = _unpack_small(sm)
        return [s1, b[0], sq, sk, ss, b[1], b[2], b[3], s2, b[4], b[5], b[6]]

    return (loss, grad_x[None], *leaves(g_shard, sm_g), *leaves(d_shard, sm_d), *leaves(nm_shard, sm_nm),
            *leaves(nv_shard, sm_nv))
```

```python
import functools

import jax
import jax.numpy as jnp
from jax import lax
from jax.experimental import pallas as pl
from jax.experimental.pallas import tpu as pltpu

F32 = jnp.float32
BF16 = jnp.bfloat16
MESH = pl.DeviceIdType.MESH

D_MODEL = 1024
EPS = 1e-6
HEAD_DIM = 64
N_Q_HEADS = 16
N_KV_HEADS = 2
GROUP = 8
BLOCK = 128
RET_HEADS = 4
RET_QK_DIM = 256
RET_V_DIM = 512
RET_CHUNK = 128
RET_ROT_BASE = 10000.0
D_FF = 2816
ATT_Q = N_Q_HEADS * HEAD_DIM
ATT_KV = N_KV_HEADS * HEAD_DIM
RET_QK = RET_HEADS * RET_QK_DIM
RET_V = RET_HEADS * RET_V_DIM
D_IN = 9472
ADAM_LR = 0.001
ADAM_B1 = 0.9
ADAM_B2 = 0.999
ADAM_EPS = 1e-08
ADAM_WD = 0.01
ADAM_STEP = 10

N_CHIPS = 4
N_DEV = 8
VMEM_LIMIT_BYTES = 60 * 1024 * 1024

P_QA = (0, 1024)
P_KVA = (1024, 256)
P_QR = (1280, 1024)
P_KR = (2304, 1024)
P_VR = (3328, 2048)
P_GR = (5376, 2048)
P_ZA = (7424, 1024)
P_ZR = (8448, 1024)

W_IN_SH = D_IN // N_CHIPS
FF_SH = D_FF // N_CHIPS
PACK_SEGS = (2368, 256, 512, 256, 704, 704, 704)
PACK_OFFS = (0, 2368, 2624, 3136, 3392, 4096, 4800)
PACK_ROWS = 5504
PACK_HALF = PACK_ROWS // 2
ROW_TILE = 688
LOCAL_COPY_CHUNKS = 8

SMALL_ROWS = 24
SM_G1, SM_G2, SM_GQ, SM_GK, SM_SINK, SM_LOSS = 0, 8, 16, 17, 18, 19


def _dot(a, b):
    return jnp.dot(a, b, preferred_element_type=F32)


def _dot_nt(a, b):
    return lax.dot_general(a, b, (((1,), (1,)), ((), ())), preferred_element_type=F32)


def _dot_tn(a, b):
    return lax.dot_general(a, b, (((0,), (0,)), ((), ())), preferred_element_type=F32)


def _bf(x):
    return x.astype(BF16)


def _rms_stats(x):
    r = lax.rsqrt(jnp.mean(x * x, axis=-1, keepdims=True) + EPS)
    return r, x * r


def _rms_bwd(dy, xhat, r, gain):
    u = dy * gain
    dx = r * (u - xhat * jnp.mean(u * xhat, axis=-1, keepdims=True))
    return dx, dy * xhat


def _params(sem):
    return pltpu.CompilerParams(dimension_semantics=sem, vmem_limit_bytes=VMEM_LIMIT_BYTES)


def _row_call(body, *, tm, row_ins, res_ins, row_outs, part_outs=(), name):
    t = row_ins[0].shape[0]
    n_tiles = t // tm
    in_specs = [pl.BlockSpec((tm, a.shape[1]), lambda i: (i, 0)) for a in row_ins]
    in_specs += [pl.BlockSpec(a.shape, lambda i: (0, 0), pipeline_mode=pl.Buffered(1)) for a in res_ins]
    out_shape = [jax.ShapeDtypeStruct((t, w), dt) for (w, dt) in row_outs]
    out_shape += [jax.ShapeDtypeStruct((n_tiles, 1, w), F32) for w in part_outs]
    out_specs = [pl.BlockSpec((tm, w), lambda i: (i, 0)) for (w, _) in row_outs]
    out_specs += [pl.BlockSpec((1, 1, w), lambda i: (i, 0, 0)) for w in part_outs]
    n_ri, n_re, n_ro = len(row_ins), len(res_ins), len(row_outs)

    def kern(*refs):
        body(refs[:n_ri], refs[n_ri:n_ri + n_re], refs[n_ri + n_re:n_ri + n_re + n_ro], refs[n_ri + n_re + n_ro:])

    return pl.pallas_call(
        kern, grid=(n_tiles,), in_specs=in_specs, out_specs=out_specs, out_shape=out_shape, name=name,
        compiler_params=_params(("parallel",)),
    )(*row_ins, *res_ins)


def _proj_fwd(x, g1, w_in):
    pieces = ((P_QA, F32), (P_KVA, F32), (P_QR, F32), (P_KR, F32), (P_VR, BF16), (P_GR, F32), (P_ZA, F32), (P_ZR, F32))

    def body(ri, re, ro, po):
        x_t = ri[0][...]
        r, xhat = _rms_stats(x_t)
        hb = _bf(xhat * re[0][...])
        ro[0][...] = hb
        for k, ((off, w), dt) in enumerate(pieces):
            ro[1 + k][...] = _dot(hb, re[1][:, off:off + w]).astype(dt)

    outs = [(D_MODEL, BF16)] + [(w, dt) for ((_, w), dt) in pieces]
    return _row_call(body, tm=256, row_ins=[x], res_ins=[g1, w_in], row_outs=outs, name="proj_fwd")


def _mix_fwd(attn, ret, z_a, z_r, x, wba, wbr, wout, g2):
    def body(ri, re, ro, po):
        ba = _dot(ri[0][...], re[0][...])
        br = _dot(ri[1][...], re[1][...])
        m = jax.nn.sigmoid(ri[2][...]) * ba + jax.nn.sigmoid(ri[3][...]) * br
        mb = _bf(m)
        x1 = ri[4][...] + _dot(mb, re[2][...])
        r, xhat = _rms_stats(x1)
        ro[0][...] = ba
        ro[1][...] = br
        ro[2][...] = mb
        ro[3][...] = x1
        ro[4][...] = _bf(xhat * re[3][...])

    outs = [(D_MODEL, F32), (D_MODEL, F32), (D_MODEL, BF16), (D_MODEL, F32), (D_MODEL, BF16)]
    return _row_call(body, tm=256, row_ins=[attn, ret, z_a, z_r, x], res_ins=[wba, wbr, wout, g2], row_outs=outs,
                     name="mix_fwd")


def _ffn_fwd_bwd(h2, x1, target, wg, wu, wd, g2):
    def body(ri, re, ro, po):
        h2_t = ri[0][...]
        x1_t = ri[1][...]
        gate = _dot(h2_t, re[0][...])
        up = _dot(h2_t, re[1][...])
        sg = jax.nn.sigmoid(gate)
        sl = gate * sg
        actb = _bf(sl * up)
        ro[0][...] = actb
        y = x1_t + _dot(actb, re[2][...])
        e = y - ri[2][...]
        po[0][0] = jnp.broadcast_to(0.5 * jnp.sum(jnp.sum(e * e, axis=1, keepdims=True), axis=0, keepdims=True)
                                    * (1.0 / D_MODEL), (1, 128))
        dy = e * (1.0 / D_MODEL)
        dyb = _bf(dy)
        ro[3][...] = dyb
        dact = _dot_nt(dyb, re[2][...])
        dupb = _bf(dact * sl)
        dgateb = _bf(dact * up * (sg * (1.0 + gate * (1.0 - sg))))
        ro[1][...] = dgateb
        ro[2][...] = dupb
        dh2 = _dot_nt(dgateb, re[0][...]) + _dot_nt(dupb, re[1][...])
        r, xhat = _rms_stats(x1_t)
        dxn, dgain = _rms_bwd(dh2, xhat, r, re[3][...])
        dx1 = dy + dxn
        ro[4][...] = dx1
        ro[5][...] = _bf(dx1)
        po[1][0] = jnp.sum(dgain, axis=0, keepdims=True)

    outs = [(D_FF, BF16), (D_FF, BF16), (D_FF, BF16), (D_MODEL, BF16), (D_MODEL, F32), (D_MODEL, BF16)]
    return _row_call(body, tm=256, row_ins=[h2, x1, target], res_ins=[wg, wu, wd, g2], row_outs=outs,
                     part_outs=(128, D_MODEL), name="ffn_fwd_bwd")


def _mix_bwd(dx1b, z_a, z_r, ba, br, g_r, o_ret, wout, wba, wbr):
    def body(ri, re, ro, po):
        dm = _dot_nt(ri[0][...], re[0][...])
        sa = jax.nn.sigmoid(ri[1][...])
        sr = jax.nn.sigmoid(ri[2][...])
        dbab = _bf(sa * dm)
        dbrb = _bf(sr * dm)
        ro[0][...] = dbab
        ro[1][...] = dbrb
        ro[2][...] = _bf(dm * ri[3][...] * (sa * (1.0 - sa)))
        ro[3][...] = _bf(dm * ri[4][...] * (sr * (1.0 - sr)))
        ro[4][...] = _bf(_dot_nt(dbab, re[1][...]))
        dret = _dot_nt(dbrb, re[2][...])
        for h in range(RET_HEADS):
            cols = slice(h * RET_V_DIM, (h + 1) * RET_V_DIM)
            g = ri[5][:, cols]
            r, rn = _rms_stats(ri[6][:, cols])
            sg = jax.nn.sigmoid(g)
            dret_h = dret[:, cols]
            d_rn = dret_h * (g * sg)
            ro[6][:, cols] = _bf(dret_h * rn * (sg * (1.0 + g * (1.0 - sg))))
            ro[5][:, cols] = r * (d_rn - rn * jnp.mean(d_rn * rn, axis=-1, keepdims=True))

    outs = [(D_MODEL, BF16), (D_MODEL, BF16), (D_MODEL, BF16), (D_MODEL, BF16), (ATT_Q, BF16), (RET_V, F32),
            (RET_V, BF16)]
    return _row_call(body, tm=256, row_ins=[dx1b, z_a, z_r, ba, br, g_r, o_ret], res_ins=[wout, wba, wbr],
                     row_outs=outs, name="mix_bwd")


def _proj_bwd(d_pieces, x, dx1, w_in, g1):
    groups = (P_QA, P_KVA, P_QR, P_KR, P_VR, P_GR, P_ZA, P_ZR)
    n_p = len(groups)

    def body(ri, re, ro, po):
        dh = None
        for k, (off, w) in enumerate(groups):
            term = _dot_nt(ri[k][...], re[0][:, off:off + w])
            dh = term if dh is None else dh + term
        r, xhat = _rms_stats(ri[n_p][...])
        dxn, dgain = _rms_bwd(dh, xhat, r, re[1][...])
        ro[0][...] = ri[n_p + 1][...] + dxn
        po[0][0] = jnp.sum(dgain, axis=0, keepdims=True)

    return _row_call(body, tm=256, row_ins=[*d_pieces, x, dx1], res_ins=[w_in, g1], row_outs=[(D_MODEL, F32)],
                     part_outs=(D_MODEL,), name="proj_bwd")


def _matmul_tn(a, b, *, tm, tn, name):
    t, m = a.shape
    n = b.shape[1]
    tk = min(2048, t)

    def kern(a_ref, b_ref, o_ref):
        k = pl.program_id(2)

        @pl.when(k == 0)
        def _():
            o_ref[...] = jnp.zeros_like(o_ref)

        o_ref[...] += _dot_tn(a_ref[...], b_ref[...])

    return pl.pallas_call(
        kern, grid=(m // tm, n // tn, t // tk),
        in_specs=[pl.BlockSpec((tk, tm), lambda i, j, k: (k, i)), pl.BlockSpec((tk, tn), lambda i, j, k: (k, j))],
        out_specs=pl.BlockSpec((tm, tn), lambda i, j, k: (i, j)),
        out_shape=jax.ShapeDtypeStruct((m, n), F32), name=name,
        compiler_params=_params(("parallel", "parallel", "arbitrary")),
    )(a, b)


def _attn_group(n, kvh, q_ref, kvp_ref, kvc_ref, gq, gk, sink_ref):
    heads = [kvh * GROUP + g for g in range(GROUP)]
    q = jnp.concatenate([q_ref[:, h * HEAD_DIM:(h + 1) * HEAD_DIM] for h in heads], axis=0)
    rq, qhat = _rms_stats(q)
    qnb = _bf(qhat * gq)
    kcols = slice(kvh * HEAD_DIM, (kvh + 1) * HEAD_DIM)
    vcols = slice(ATT_KV + kvh * HEAD_DIM, ATT_KV + (kvh + 1) * HEAD_DIM)
    k = jnp.concatenate([kvp_ref[:, kcols], kvc_ref[:, kcols]], axis=0)
    rk, khat = _rms_stats(k)
    knb = _bf(khat * gk)
    vb = _bf(jnp.concatenate([kvp_ref[:, vcols], kvc_ref[:, vcols]], axis=0))
    s = _dot_nt(qnb, knb) * (HEAD_DIM ** -0.5)
    rows = GROUP * BLOCK
    i = lax.broadcasted_iota(jnp.int32, (rows, 2 * BLOCK), 0) & (BLOCK - 1)
    j = lax.broadcasted_iota(jnp.int32, (rows, 2 * BLOCK), 1)
    allowed = (j > i) & (j <= i + BLOCK) & ((j >= BLOCK) | (n > 0))
    s = jnp.where(allowed, s, -1e30)
    sink = jnp.concatenate([jnp.broadcast_to(sink_ref[0:1, h:h + 1], (BLOCK, 1)) for h in heads], axis=0)
    m = jnp.maximum(jnp.max(s, axis=1, keepdims=True), sink)
    e = jnp.exp(s - m)
    es = jnp.exp(sink - m)
    z = jnp.sum(e, axis=1, keepdims=True) + es
    return dict(heads=heads, qhat=qhat, rq=rq, qnb=qnb, khat=khat, rk=rk, knb=knb, vb=vb, p=e / z, psink=es / z)


def _attn_fwd(q_a, kv_a, gq, gk, sinks):
    t = q_a.shape[0]
    nb = t // BLOCK

    def kern(q_ref, kvp_ref, kvc_ref, gq_ref, gk_ref, sink_ref, o_ref):
        n = pl.program_id(0)
        for kvh in range(N_KV_HEADS):
            a = _attn_group(n, kvh, q_ref, kvp_ref, kvc_ref, gq_ref[...], gk_ref[...], sink_ref)
            out = _dot(_bf(a["p"]), a["vb"])
            for g, h in enumerate(a["heads"]):
                o_ref[:, h * HEAD_DIM:(h + 1) * HEAD_DIM] = _bf(out[g * BLOCK:(g + 1) * BLOCK, :])

    small = lambda a: pl.BlockSpec(a.shape, lambda n: (0, 0))
    return pl.pallas_call(
        kern, grid=(nb,),
        in_specs=[pl.BlockSpec((BLOCK, ATT_Q), lambda n: (n, 0)),
                  pl.BlockSpec((BLOCK, 2 * ATT_KV), lambda n: (jnp.maximum(n - 1, 0), 0)),
                  pl.BlockSpec((BLOCK, 2 * ATT_KV), lambda n: (n, 0)),
                  small(gq), small(gk), small(sinks)],
        out_specs=pl.BlockSpec((BLOCK, ATT_Q), lambda n: (n, 0)),
        out_shape=jax.ShapeDtypeStruct((t, ATT_Q), BF16), name="attn_fwd",
        compiler_params=_params(("parallel",)),
    )(q_a, kv_a, kv_a, gq, gk, sinks)


def _attn_bwd(q_a, kv_a, d_attn, gq, gk, sinks):
    t = q_a.shape[0]
    nb = t // BLOCK

    def kern(q_ref, kvp_ref, kvc_ref, do_ref, gq_ref, gk_ref, sink_ref,
             dq_ref, dkv_ref, dgq_ref, dgk_ref, dsink_ref, band_k, band_v, carry_k, carry_v):
        n = pl.program_id(0)
        gq_v = gq_ref[...]
        gk_v = gk_ref[...]

        @pl.when(n == 0)
        def _():
            carry_k[...] = jnp.zeros_like(carry_k)
            carry_v[...] = jnp.zeros_like(carry_v)
            dgq_ref[...] = jnp.zeros_like(dgq_ref)
            dgk_ref[...] = jnp.zeros_like(dgk_ref)
            dsink_ref[...] = jnp.zeros_like(dsink_ref)

        @pl.when(n == nb)
        def _():
            band_k[...] = jnp.zeros_like(band_k)
            band_v[...] = jnp.zeros_like(band_v)

        @pl.when(n < nb)
        def _():
            lane16 = lax.broadcasted_iota(jnp.int32, (1, N_Q_HEADS), 1)
            dsink = jnp.zeros((1, N_Q_HEADS), F32)
            dgq = jnp.zeros((1, HEAD_DIM), F32)
            for kvh in range(N_KV_HEADS):
                a = _attn_group(n, kvh, q_ref, kvp_ref, kvc_ref, gq_v, gk_v, sink_ref)
                p = a["p"]
                dob = jnp.concatenate([do_ref[:, h * HEAD_DIM:(h + 1) * HEAD_DIM] for h in a["heads"]], axis=0)
                dp = _dot_nt(dob, a["vb"])
                delta = jnp.sum(p * dp, axis=1, keepdims=True)
                dsb = _bf(p * (dp - delta))
                dsk = a["psink"] * delta
                for g, h in enumerate(a["heads"]):
                    tot = jnp.sum(dsk[g * BLOCK:(g + 1) * BLOCK, :], axis=0, keepdims=True)
                    dsink = dsink - jnp.where(lane16 == h, tot, 0.0)
                scale = HEAD_DIM ** -0.5
                dqn = _dot(dsb, a["knb"]) * scale
                band_k[kvh] = _dot_tn(dsb, a["qnb"]) * scale
                band_v[kvh] = _dot_tn(_bf(p), dob)
                dq, dgain = _rms_bwd(dqn, a["qhat"], a["rq"], gq_v)
                dgq = dgq + jnp.sum(dgain, axis=0, keepdims=True)
                for g, h in enumerate(a["heads"]):
                    dq_ref[:, h * HEAD_DIM:(h + 1) * HEAD_DIM] = _bf(dq[g * BLOCK:(g + 1) * BLOCK, :])
            dsink_ref[...] += dsink
            dgq_ref[...] += dgq

        dgk = jnp.zeros((1, HEAD_DIM), F32)
        for kvh in range(N_KV_HEADS):
            kcols = slice(kvh * HEAD_DIM, (kvh + 1) * HEAD_DIM)
            vcols = slice(ATT_KV + kvh * HEAD_DIM, ATT_KV + (kvh + 1) * HEAD_DIM)
            dkn = carry_k[kvh] + band_k[kvh, 0:BLOCK, :]
            dv = carry_v[kvh] + band_v[kvh, 0:BLOCK, :]
            rk, khat = _rms_stats(kvp_ref[:, kcols])
            dk, dgain = _rms_bwd(dkn, khat, rk, gk_v)
            dgk = dgk + jnp.sum(dgain, axis=0, keepdims=True)
            dkv_ref[:, kcols] = _bf(dk)
            dkv_ref[:, vcols] = _bf(dv)
            carry_k[kvh] = band_k[kvh, BLOCK:2 * BLOCK, :]
            carry_v[kvh] = band_v[kvh, BLOCK:2 * BLOCK, :]
        dgk_ref[...] += dgk

    small = lambda a: pl.BlockSpec(a.shape, lambda n: (0, 0))
    last = nb - 1
    return pl.pallas_call(
        kern, grid=(nb + 1,),
        in_specs=[pl.BlockSpec((BLOCK, ATT_Q), lambda n: (jnp.minimum(n, last), 0)),
                  pl.BlockSpec((BLOCK, 2 * ATT_KV), lambda n: (jnp.maximum(n - 1, 0), 0)),
                  pl.BlockSpec((BLOCK, 2 * ATT_KV), lambda n: (jnp.minimum(n, last), 0)),
                  pl.BlockSpec((BLOCK, ATT_Q), lambda n: (jnp.minimum(n, last), 0)),
                  small(gq), small(gk), small(sinks)],
        out_specs=[pl.BlockSpec((BLOCK, ATT_Q), lambda n: (jnp.minimum(n, last), 0)),
                   pl.BlockSpec((BLOCK, 2 * ATT_KV), lambda n: (jnp.maximum(n - 1, 0), 0)),
                   pl.BlockSpec((1, HEAD_DIM), lambda n: (0, 0)),
                   pl.BlockSpec((1, HEAD_DIM), lambda n: (0, 0)),
                   pl.BlockSpec((1, N_Q_HEADS), lambda n: (0, 0))],
        out_shape=[jax.ShapeDtypeStruct((t, ATT_Q), BF16), jax.ShapeDtypeStruct((t, 2 * ATT_KV), BF16),
                   jax.ShapeDtypeStruct((1, HEAD_DIM), F32), jax.ShapeDtypeStruct((1, HEAD_DIM), F32),
                   jax.ShapeDtypeStruct((1, N_Q_HEADS), F32)],
        scratch_shapes=[pltpu.VMEM((N_KV_HEADS, 2 * BLOCK, HEAD_DIM), F32),
                        pltpu.VMEM((N_KV_HEADS, 2 * BLOCK, HEAD_DIM), F32),
                        pltpu.VMEM((N_KV_HEADS, BLOCK, HEAD_DIM), F32),
                        pltpu.VMEM((N_KV_HEADS, BLOCK, HEAD_DIM), F32)],
        name="attn_bwd", compiler_params=_params(("arbitrary",)),
    )(q_a, kv_a, kv_a, d_attn, gq, gk, sinks)


def _ret_tables(t):
    pos = jnp.arange(t, dtype=F32)
    theta = 1.0 / (RET_ROT_BASE ** jnp.linspace(0.0, 1.0, RET_QK_DIM // 2, dtype=F32))
    ang = jnp.repeat(pos[:, None] * theta[None, :], 2, axis=-1)
    sign = jnp.tile(jnp.array([-1.0, 1.0], F32), RET_QK_DIM // 2)
    log_gamma = jnp.log(1.0 - 2.0 ** (-5.0 - jnp.arange(RET_HEADS, dtype=F32)))
    i = jnp.arange(RET_CHUNK, dtype=F32)
    diff = i[:, None] - i[None, :]
    causal = diff >= 0
    decay = jnp.where(causal[None], jnp.exp(jnp.where(causal, diff, 0.0)[None] * log_gamma[:, None, None]), 0.0)
    xi = jnp.exp((i + 1.0)[None, :] * log_gamma[:, None])[:, :, None]
    zeta = jnp.exp((RET_CHUNK - 1.0 - i)[None, :] * log_gamma[:, None])[:, :, None]
    gch = jnp.broadcast_to(jnp.exp(RET_CHUNK * log_gamma)[:, None, None], (RET_HEADS, 1, 128))
    return jnp.cos(ang), jnp.sin(ang) * sign[None, :], decay, xi, zeta, gch


def _swap_pairs(x):
    lane = lax.broadcasted_iota(jnp.int32, x.shape, 1)
    return jnp.where((lane & 1) == 0, pltpu.roll(x, RET_QK_DIM - 1, 1), pltpu.roll(x, 1, 1))


def _rotate(x, cos, sin_s):
    return x * cos + _swap_pairs(x) * sin_s


def _rotate_bwd(dy, cos, sin_s):
    return dy * cos + _swap_pairs(dy * sin_s)


def _ret_specs(order):
    qk = pl.BlockSpec((RET_CHUNK, RET_QK), lambda j: (order(j), 0))
    v = pl.BlockSpec((RET_CHUNK, RET_V), lambda j: (order(j), 0))
    dec = pl.BlockSpec((RET_HEADS, RET_CHUNK, RET_CHUNK), lambda j: (0, 0, 0))
    col = pl.BlockSpec((RET_HEADS, RET_CHUNK, 1), lambda j: (0, 0, 0))
    gch = pl.BlockSpec((RET_HEADS, 1, 128), lambda j: (0, 0, 0))
    st = pl.BlockSpec((RET_HEADS, None, RET_QK_DIM, RET_V_DIM), lambda j: (0, order(j), 0, 0))
    pos = pl.BlockSpec((RET_CHUNK, RET_QK_DIM), lambda j: (order(j), 0))
    return qk, v, dec, col, gch, st, pos


def _ret_fwd(q_r, k_r, v_r, g_r, tables):
    t = q_r.shape[0]
    nc = t // RET_CHUNK
    cos, sin_s, decay, xi, zeta, gch = tables

    def kern(q_ref, k_ref, v_ref, g_ref, cos_ref, sin_ref, dec_ref, xi_ref, zeta_ref, gch_ref,
             o_ref, ret_ref, st_ref, state):
        @pl.when(pl.program_id(0) == 0)
        def _():
            state[...] = jnp.zeros_like(state)

        cos_t = cos_ref[...]
        sin_t = sin_ref[...]
        for h in range(RET_HEADS):
            qc = slice(h * RET_QK_DIM, (h + 1) * RET_QK_DIM)
            vc = slice(h * RET_V_DIM, (h + 1) * RET_V_DIM)
            qs = _bf(_rotate(q_ref[:, qc], cos_t, sin_t))
            ks = _rotate(k_ref[:, qc] * (RET_QK_DIM ** -0.5), cos_t, sin_t)
            vb = v_ref[:, vc]
            s_old = state[h]
            sb = _bf(s_old)
            st_ref[h] = sb
            inner = _dot_nt(qs, _bf(ks)) * dec_ref[h]
            out = _dot(_bf(inner), vb) + _dot(qs, sb) * xi_ref[h]
            state[h] = gch_ref[h, :, 0:1] * s_old + _dot_tn(_bf(ks * zeta_ref[h]), vb)
            o_ref[:, vc] = out
            r, rn = _rms_stats(out)
            g = g_ref[:, vc]
            ret_ref[:, vc] = _bf(g * jax.nn.sigmoid(g) * rn)

    qk, v, dec, col, gsp, st, pos = _ret_specs(lambda j: j)
    return pl.pallas_call(
        kern, grid=(nc,),
        in_specs=[qk, qk, v, v, pos, pos, dec, col, col, gsp],
        out_specs=[v, v, st],
        out_shape=[jax.ShapeDtypeStruct((t, RET_V), F32), jax.ShapeDtypeStruct((t, RET_V), BF16),
                   jax.ShapeDtypeStruct((RET_HEADS, nc, RET_QK_DIM, RET_V_DIM), BF16)],
        scratch_shapes=[pltpu.VMEM((RET_HEADS, RET_QK_DIM, RET_V_DIM), F32)],
        name="ret_fwd", compiler_params=_params(("arbitrary",)),
    )(q_r, k_r, v_r, g_r, cos, sin_s, decay, xi, zeta, gch)


def _ret_bwd(q_r, k_r, v_r, d_o, states, tables):
    t = q_r.shape[0]
    nc = t // RET_CHUNK
    cos, sin_s, decay, xi, zeta, gch = tables

    def kern(q_ref, k_ref, v_ref, do_ref, st_ref, cos_ref, sin_ref, dec_ref, xi_ref, zeta_ref, gch_ref,
             dq_ref, dk_ref, dv_ref, dstate):
        @pl.when(pl.program_id(0) == 0)
        def _():
            dstate[...] = jnp.zeros_like(dstate)

        cos_t = cos_ref[...]
        sin_t = sin_ref[...]
        scale = RET_QK_DIM ** -0.5
        for h in range(RET_HEADS):
            qc = slice(h * RET_QK_DIM, (h + 1) * RET_QK_DIM)
            vc = slice(h * RET_V_DIM, (h + 1) * RET_V_DIM)
            qs = _bf(_rotate(q_ref[:, qc], cos_t, sin_t))
            ks = _rotate(k_ref[:, qc] * scale, cos_t, sin_t)
            ksb = _bf(ks)
            vb = v_ref[:, vc]
            d_o_t = do_ref[:, vc]
            dob = _bf(d_o_t)
            doxb = _bf(d_o_t * xi_ref[h])
            dec = dec_ref[h]
            ds_old = dstate[h]
            dsb = _bf(ds_old)
            pb = _bf(_dot_nt(qs, ksb) * dec)
            dpb = _bf(_dot_nt(dob, vb) * dec)
            dqs = _dot(dpb, ksb) + _dot_nt(doxb, st_ref[h])
            dks = _dot_tn(dpb, qs) + _dot_nt(vb, dsb) * zeta_ref[h]
            dv_ref[:, vc] = _bf(_dot_tn(pb, dob) + _dot(_bf(ks * zeta_ref[h]), dsb))
            dstate[h] = gch_ref[h, :, 0:1] * ds_old + _dot_tn(qs, doxb)
            dq_ref[:, qc] = _bf(_rotate_bwd(dqs, cos_t, sin_t))
            dk_ref[:, qc] = _bf(_rotate_bwd(dks, cos_t, sin_t) * scale)

    qk, v, dec, col, gsp, st, pos = _ret_specs(lambda j: nc - 1 - j)
    return pl.pallas_call(
        kern, grid=(nc,),
        in_specs=[qk, qk, v, v, st, pos, pos, dec, col, col, gsp],
        out_specs=[qk, qk, v],
        out_shape=[jax.ShapeDtypeStruct((t, RET_QK), BF16), jax.ShapeDtypeStruct((t, RET_QK), BF16),
                   jax.ShapeDtypeStruct((t, RET_V), BF16)],
        scratch_shapes=[pltpu.VMEM((RET_HEADS, RET_QK_DIM, RET_V_DIM), F32)],
        name="ret_bwd", compiler_params=_params(("arbitrary",)),
    )(q_r, k_r, v_r, d_o, states, cos, sin_s, decay, xi, zeta, gch)


def _local_step(x, target, w, g1, g2, gq, gk, sinks):
    t = x.shape[0]
    tables = _ret_tables(t)
    h1, q_a, kv_a, q_r, k_r, v_r, g_r, z_a, z_r = _proj_fwd(x, g1, w["w_in"])
    attn = _attn_fwd(q_a, kv_a, gq, gk, sinks)
    o_ret, ret, states = _ret_fwd(q_r, k_r, v_r, g_r, tables)
    ba, br, merged, x1, h2 = _mix_fwd(attn, ret, z_a, z_r, x, w["wba"], w["wbr"], w["wout"], g2)
    act, dgate, dup, dyb, dx1, dx1b, loss_p, dg2_p = _ffn_fwd_bwd(h2, x1, target, w["wg"], w["wu"], w["wd"], g2)
    dba, dbr, dz_a, dz_r, d_attn, d_o, dg_r = _mix_bwd(dx1b, z_a, z_r, ba, br, g_r, o_ret, w["wout"], w["wba"], w["wbr"])
    dq_r, dk_r, dv_r = _ret_bwd(q_r, k_r, v_r, d_o, states, tables)
    dq_a, dkv_a, dgq, dgk, dsinks = _attn_bwd(q_a, kv_a, d_attn, gq, gk, sinks)
    d_pieces = [dq_a, dkv_a, dq_r, dk_r, dv_r, dg_r, dz_a, dz_r]
    grad_x, dg1_p = _proj_bwd(d_pieces, x, dx1, w["w_in"], g1)
    dw_in = [_matmul_tn(h1, dp, tm=D_MODEL, tn=min(dp.shape[1], 1024), name=f"dw_in_{k}")
             for k, dp in enumerate(d_pieces)]
    grads = dict(
        w_in=dw_in,
        wba=_matmul_tn(attn, dba, tm=1024, tn=1024, name="dw_ba"),
        wbr=_matmul_tn(ret, dbr, tm=1024, tn=1024, name="dw_br"),
        wout=_matmul_tn(merged, dx1b, tm=1024, tn=1024, name="dw_out"),
        wg=_matmul_tn(h2, dgate, tm=1024, tn=D_FF // 2, name="dw_gate"),
        wu=_matmul_tn(h2, dup, tm=1024, tn=D_FF // 2, name="dw_up"),
        wd=_matmul_tn(act, dyb, tm=D_FF // 2, tn=1024, name="dw_down"),
    )
    small = dict(loss=loss_p, dg1=dg1_p, dg2=dg2_p, dgq=dgq, dgk=dgk, dsinks=dsinks)
    return grad_x, grads, small


def _position():
    return lax.axis_index("x"), lax.axis_index("y"), lax.axis_index("c")


_ANY = pl.BlockSpec(memory_space=pl.ANY)


def _gather_packs(pack):
    rows, width = pack.shape
    half = rows // 2

    def body(p_ref, o_ref, send_sems, recv_sems, local_sem):
        x, y, c = _position()
        me = (x, y, c)
        sibling = (x, y, 1 - c)
        chips = [(1 - x, y), (x, 1 - y), (1 - x, 1 - y)]
        my_chip = 2 * x + y

        def slab(chip, hf):
            return o_ref.at[chip, pl.ds(hf * half, half), :]

        def copy(k, src, dst, to):
            return pltpu.make_async_remote_copy(src_ref=src, dst_ref=dst, send_sem=send_sems.at[k],
                                                recv_sem=recv_sems.at[k], device_id=to, device_id_type=MESH)

        for q in range(LOCAL_COPY_CHUNKS):
            part = pl.ds(q * (rows // LOCAL_COPY_CHUNKS), rows // LOCAL_COPY_CHUNKS)
            pltpu.make_async_copy(p_ref.at[part, :], o_ref.at[my_chip, part, :], local_sem).start()
        mine = pltpu.make_async_copy(p_ref, o_ref.at[my_chip], local_sem)
        first = [copy(k, p_ref.at[pl.ds(c * half, half), :], slab(my_chip, c), (cx, cy, c))
                 for k, (cx, cy) in enumerate(chips)]
        for cp in first:
            cp.start()
        passed = []
        for k, (cx, cy) in enumerate(chips):
            landed = slab(2 * cx + cy, c)
            copy(k, landed, landed, me).wait_recv()
            cp = copy(3 + k, landed, landed, sibling)
            cp.start()
            passed.append(cp)
        for k, (cx, cy) in enumerate(chips):
            from_sibling = slab(2 * cx + cy, 1 - c)
            copy(3 + k, from_sibling, from_sibling, me).wait_recv()
        for cp in first + passed:
            cp.wait_send()
        mine.wait()

    return pl.pallas_call(
        body, in_specs=[_ANY], out_specs=_ANY,
        out_shape=jax.ShapeDtypeStruct((N_CHIPS, rows, width), pack.dtype),
        scratch_shapes=[pltpu.SemaphoreType.DMA((6,)), pltpu.SemaphoreType.DMA((6,)), pltpu.SemaphoreType.DMA],
        name="gather_weights",
    )(pack)


def _pair_exchange(g):
    n, rows, width = g.shape
    half = rows // 2

    def body(g_ref, o_ref, send_sems, recv_sems):
        x, y, c = _position()
        copies = [pltpu.make_async_remote_copy(
            src_ref=g_ref.at[k, pl.ds((1 - c) * half, half), :], dst_ref=o_ref.at[k], send_sem=send_sems.at[k],
            recv_sem=recv_sems.at[k], device_id=(x, y, 1 - c), device_id_type=MESH) for k in range(n)]
        for cp in copies:
            cp.start()
        for cp in copies:
            cp.wait()

    return pl.pallas_call(
        body, in_specs=[_ANY], out_specs=_ANY, out_shape=jax.ShapeDtypeStruct((n, half, width), g.dtype),
        scratch_shapes=[pltpu.SemaphoreType.DMA((n,)), pltpu.SemaphoreType.DMA((n,))], name="pair_exchange",
    )(g)


def _pair_sum(g, from_sibling, c_arr):
    n, rows, width = g.shape
    tiles = (rows // 2) // ROW_TILE

    def kern(c_ref, g_ref, s_ref, o_ref):
        o_ref[...] = _bf(g_ref[...] + s_ref[...])

    return pl.pallas_call(
        kern,
        grid_spec=pltpu.PrefetchScalarGridSpec(
            num_scalar_prefetch=1, grid=(n, tiles),
            in_specs=[pl.BlockSpec((None, ROW_TILE, width), lambda k, i, c: (k, c[0] * tiles + i, 0)),
                      pl.BlockSpec((None, ROW_TILE, width), lambda k, i, c: (k, i, 0))],
            out_specs=pl.BlockSpec((None, ROW_TILE, width), lambda k, i, c: (k, i, 0))),
        out_shape=jax.ShapeDtypeStruct((n, rows // 2, width), BF16), name="pair_sum",
        compiler_params=_params(("parallel", "parallel")),
    )(c_arr, g, from_sibling)


def _scatter_to_owners(hsum):
    n, half, width = hsum.shape

    def body(h_ref, o_ref, send_sems, recv_sems):
        x, y, c = _position()
        chips = [(1 - x, y), (x, 1 - y), (1 - x, 1 - y)]
        copies = [pltpu.make_async_remote_copy(
            src_ref=h_ref.at[2 * cx + cy], dst_ref=o_ref.at[k], send_sem=send_sems.at[k],
            recv_sem=recv_sems.at[k], device_id=(cx, cy, c), device_id_type=MESH) for k, (cx, cy) in enumerate(chips)]
        for cp in copies:
            cp.start()
        for cp in copies:
            cp.wait()

    return pl.pallas_call(
        body, in_specs=[_ANY], out_specs=_ANY, out_shape=jax.ShapeDtypeStruct((n - 1, half, width), hsum.dtype),
        scratch_shapes=[pltpu.SemaphoreType.DMA((3,)), pltpu.SemaphoreType.DMA((3,))],
        name="scatter_to_owners",
    )(hsum)


def _sum_chips(hsum, parts, chip_arr):
    n, half, width = parts.shape

    def kern(chip_ref, h_ref, p_ref, o_ref):
        acc = h_ref[...].astype(F32)
        for k in range(n):
            acc = acc + p_ref[k].astype(F32)
        o_ref[...] = acc

    return pl.pallas_call(
        kern,
        grid_spec=pltpu.PrefetchScalarGridSpec(
            num_scalar_prefetch=1, grid=(half // ROW_TILE,),
            in_specs=[pl.BlockSpec((None, ROW_TILE, width), lambda i, chip: (chip[0], i, 0)),
                      pl.BlockSpec((n, ROW_TILE, width), lambda i, chip: (0, i, 0))],
            out_specs=pl.BlockSpec((ROW_TILE, width), lambda i, chip: (i, 0))),
        out_shape=jax.ShapeDtypeStruct((half, width), F32), name="sum_chips",
        compiler_params=_params(("parallel",)),
    )(chip_arr, hsum, parts)


def _share_halves(fhalf):
    def body(f_ref, o_ref, send_sem, recv_sem):
        x, y, c = _position()
        cp = pltpu.make_async_remote_copy(src_ref=f_ref, dst_ref=o_ref, send_sem=send_sem, recv_sem=recv_sem,
                                          device_id=(x, y, 1 - c), device_id_type=MESH)
        cp.start()
        cp.wait()

    return pl.pallas_call(
        body, in_specs=[_ANY], out_specs=_ANY, out_shape=jax.ShapeDtypeStruct(fhalf.shape, fhalf.dtype),
        scratch_shapes=[pltpu.SemaphoreType.DMA, pltpu.SemaphoreType.DMA], name="share_halves",
    )(fhalf)


def _adamw_math(w, g, m, v):
    m = ADAM_B1 * m + (1.0 - ADAM_B1) * g
    v = ADAM_B2 * v + (1.0 - ADAM_B2) * (g * g)
    m_hat = m / (1.0 - ADAM_B1 ** ADAM_STEP)
    v_hat = v / (1.0 - ADAM_B2 ** ADAM_STEP)
    delta = -ADAM_LR * (m_hat / (jnp.sqrt(v_hat) + ADAM_EPS) + ADAM_WD * w)
    return delta, m, v


def _adamw(w, g_mine, g_other, m, v, c_arr):
    rows, width = w.shape
    tiles = (rows // 2) // ROW_TILE

    def kern(c_ref, w_ref, gm_ref, go_ref, m_ref, v_ref, g_ref, d_ref, nm_ref, nv_ref):
        in_my_half = (pl.program_id(0) // tiles) == c_ref[0]
        g = jnp.where(in_my_half, gm_ref[...], go_ref[...])
        g_ref[...] = g
        d_ref[...], nm_ref[...], nv_ref[...] = _adamw_math(w_ref[...], g, m_ref[...], v_ref[...])

    full = pl.BlockSpec((ROW_TILE, width), lambda i, c: (i, 0))
    half = pl.BlockSpec((ROW_TILE, width), lambda i, c: (i % tiles, 0))
    return pl.pallas_call(
        kern,
        grid_spec=pltpu.PrefetchScalarGridSpec(
            num_scalar_prefetch=1, grid=(rows // ROW_TILE,), in_specs=[full, half, half, full, full],
            out_specs=[full] * 4),
        out_shape=[jax.ShapeDtypeStruct((rows, width), F32)] * 4, name="adamw",
        compiler_params=_params(("parallel",)),
    )(c_arr, w, g_mine, g_other, m, v)


def _small_step(part, tile_sums, w, m, v):
    loss_p, dg1_p, dg2_p = tile_sums

    def body(part_ref, loss_ref, dg1_ref, dg2_ref, w_ref, m_ref, v_ref, g_ref, d_ref, nm_ref, nv_ref,
             mine, gathered, send_sems, recv_sems):
        x, y, c = _position()
        me = 4 * x + 2 * y + c
        mine[...] = part_ref[...]
        for r in range(8):
            lanes = slice(128 * r, 128 * (r + 1))
            mine[SM_G1 + r:SM_G1 + r + 1, :] = jnp.sum(dg1_ref[:, lanes], axis=0, keepdims=True)
            mine[SM_G2 + r:SM_G2 + r + 1, :] = jnp.sum(dg2_ref[:, lanes], axis=0, keepdims=True)
        mine[SM_LOSS:SM_LOSS + 1, :] = jnp.sum(loss_ref[...], axis=0, keepdims=True)
        copies = []
        for k in range(1, N_DEV):
            flip = (k >> 2) & 1, (k >> 1) & 1, k & 1
            to = (x ^ flip[0], y ^ flip[1], c ^ flip[2])
            cp = pltpu.make_async_remote_copy(
                src_ref=mine, dst_ref=gathered.at[me], send_sem=send_sems.at[k - 1], recv_sem=recv_sems.at[k - 1],
                device_id=to, device_id_type=MESH)
            cp.start()
            copies.append(cp)
        gathered[me] = mine[...]
        for k in range(1, N_DEV):
            flip = (k >> 2) & 1, (k >> 1) & 1, k & 1
            src = 4 * (x ^ flip[0]) + 2 * (y ^ flip[1]) + (c ^ flip[2])
            pltpu.make_async_remote_copy(
                src_ref=mine, dst_ref=gathered.at[src], send_sem=send_sems.at[k - 1], recv_sem=recv_sems.at[k - 1],
                device_id=(x, y, c), device_id_type=MESH).wait_recv()
        for cp in copies:
            cp.wait_send()
        total = gathered[0]
        for k in range(1, N_DEV):
            total = total + gathered[k]
        g_ref[...] = total
        d_ref[...], nm_ref[...], nv_ref[...] = _adamw_math(w_ref[...], total, m_ref[...], v_ref[...])

    vm = pl.BlockSpec(memory_space=pltpu.VMEM)
    blk = jax.ShapeDtypeStruct((SMALL_ROWS, 128), F32)
    return pl.pallas_call(
        body, in_specs=[vm] * 7, out_specs=[vm] * 4, out_shape=[blk] * 4,
        scratch_shapes=[pltpu.VMEM((SMALL_ROWS, 128), F32), pltpu.VMEM((N_DEV, SMALL_ROWS, 128), F32),
                        pltpu.SemaphoreType.DMA((N_DEV - 1,)), pltpu.SemaphoreType.DMA((N_DEV - 1,))],
        name="small_step",
    )(part, loss_p.reshape(-1, 128), dg1_p.reshape(-1, D_MODEL), dg2_p.reshape(-1, D_MODEL), w, m, v)


def _pack_shard(w_in, wba, wbr, wout, wg, wu, wd):
    return jnp.concatenate([a.reshape(-1, D_MODEL) for a in (w_in, wba, wbr, wout, wg, wu, wd)], axis=0)


def _unpack_shard(p):
    seg = lambda k: p[PACK_OFFS[k]:PACK_OFFS[k] + PACK_SEGS[k]]
    return (seg(0).reshape(D_MODEL, W_IN_SH), seg(1), seg(2), seg(3), seg(4).reshape(D_MODEL, FF_SH),
            seg(5).reshape(D_MODEL, FF_SH), seg(6))


def _full_weights(gathered):
    per_chip = [_unpack_shard(gathered[k]) for k in range(N_CHIPS)]
    cat = lambda i, axis: jnp.concatenate([pc[i] for pc in per_chip], axis=axis)
    return dict(w_in=cat(0, 1), wba=cat(1, 0), wbr=cat(2, 0), wout=cat(3, 0), wg=cat(4, 1), wu=cat(5, 1), wd=cat(6, 0))


def _pack_grads(grads):
    dw_in = jnp.concatenate(grads["w_in"], axis=1)
    slabs = []
    for k in range(N_CHIPS):
        slabs.append(_pack_shard(
            dw_in[:, k * W_IN_SH:(k + 1) * W_IN_SH], grads["wba"][k * 256:(k + 1) * 256],
            grads["wbr"][k * 512:(k + 1) * 512], grads["wout"][k * 256:(k + 1) * 256],
            grads["wg"][:, k * FF_SH:(k + 1) * FF_SH], grads["wu"][:, k * FF_SH:(k + 1) * FF_SH],
            grads["wd"][k * FF_SH:(k + 1) * FF_SH]))
    return jnp.stack(slabs, axis=0)


def _pack_small(g1, g2, gq, gk, sinks):
    blk = jnp.zeros((SMALL_ROWS, 128), F32)
    blk = blk.at[SM_G1:SM_G1 + 8].set(g1.reshape(8, 128))
    blk = blk.at[SM_G2:SM_G2 + 8].set(g2.reshape(8, 128))
    blk = blk.at[SM_GQ, :HEAD_DIM].set(gq.reshape(-1))
    blk = blk.at[SM_GK, :HEAD_DIM].set(gk.reshape(-1))
    blk = blk.at[SM_SINK, :N_Q_HEADS].set(sinks.reshape(-1))
    return blk


def _unpack_small(blk):
    return (blk[SM_G1:SM_G1 + 8].reshape(1, D_MODEL), blk[SM_G2:SM_G2 + 8].reshape(1, D_MODEL),
            blk[SM_GQ, :HEAD_DIM].reshape(1, HEAD_DIM), blk[SM_GK, :HEAD_DIM].reshape(1, HEAD_DIM),
            blk[SM_SINK, :N_Q_HEADS].reshape(1, N_Q_HEADS))


def kernel(x, norm_mix_gain, w_in, q_norm_gain, k_norm_gain, attn_sinks, w_branch_attn, w_branch_ret, w_out, norm_ffn_gain, w_ffn_gate, w_ffn_up, w_ffn_down, loss_target, m_norm_mix_gain, m_w_in, m_q_norm_gain, m_k_norm_gain, m_attn_sinks, m_w_branch_attn, m_w_branch_ret, m_w_out, m_norm_ffn_gain, m_w_ffn_gate, m_w_ffn_up, m_w_ffn_down, v_norm_mix_gain, v_w_in, v_q_norm_gain, v_k_norm_gain, v_attn_sinks, v_w_branch_attn, v_w_branch_ret, v_w_out, v_norm_ffn_gain, v_w_ffn_gate, v_w_ffn_up, v_w_ffn_down):
    big = lambda *ws: _pack_shard(*[a[0] for a in ws])
    w_pack = big(w_in, w_branch_attn, w_branch_ret, w_out, w_ffn_gate, w_ffn_up, w_ffn_down)
    m_pack = big(m_w_in, m_w_branch_attn, m_w_branch_ret, m_w_out, m_w_ffn_gate, m_w_ffn_up, m_w_ffn_down)
    v_pack = big(v_w_in, v_w_branch_attn, v_w_branch_ret, v_w_out, v_w_ffn_gate, v_w_ffn_up, v_w_ffn_down)

    weights = _full_weights(_gather_packs(w_pack.astype(BF16)))
    grad_x, grads, small = _local_step(x[0], loss_target[0], weights, norm_mix_gain, norm_ffn_gain, q_norm_gain,
                                       k_norm_gain, attn_sinks)

    g_pack = _pack_grads(grads)
    c_arr = lax.axis_index("c").astype(jnp.int32).reshape(1)
    chip_arr = (2 * lax.axis_index("x") + lax.axis_index("y")).astype(jnp.int32).reshape(1)
    chip_sum = _pair_sum(g_pack, _pair_exchange(g_pack), c_arr)
    g_half = _sum_chips(chip_sum, _scatter_to_owners(chip_sum), chip_arr)
    g_shard, d_shard, nm_shard, nv_shard = _adamw(w_pack, g_half, _share_halves(g_half), m_pack, v_pack, c_arr)

    zeros = jnp.zeros((1, D_MODEL), F32)
    part = _pack_small(zeros, zeros, small["dgq"], small["dgk"], small["dsinks"])
    sm_w = _pack_small(norm_mix_gain, norm_ffn_gain, q_norm_gain, k_norm_gain, attn_sinks)
    sm_m = _pack_small(m_norm_mix_gain, m_norm_ffn_gain, m_q_norm_gain, m_k_norm_gain, m_attn_sinks)
    sm_v = _pack_small(v_norm_mix_gain, v_norm_ffn_gain, v_q_norm_gain, v_k_norm_gain, v_attn_sinks)
    sm_g, sm_d, sm_nm, sm_nv = _small_step(part, (small["loss"], small["dg1"], small["dg2"]), sm_w, sm_m, sm_v)
    loss = sm_g[SM_LOSS, 0]

    def leaves(shard, sm):
        b = [a[None] for a in _unpack_shard(shard)]
        s1, s2, sq, sk, ss = _unpack_small(sm)
        return [s1, b[0], sq, sk, ss, b[1], b[2], b[3], s2, b[4], b[5], b[6]]

    return (loss, grad_x[None], *leaves(g_shard, sm_g), *leaves(d_shard, sm_d), *leaves(nm_shard, sm_nm),
            *leaves(nv_shard, sm_nv))
```

```python
import jax
import jax.numpy as jnp
from jax import lax
from jax.experimental import pallas as pl
from jax.experimental.pallas import tpu as pltpu

F32 = jnp.float32
BF16 = jnp.bfloat16
MESH = pl.DeviceIdType.MESH

D_MODEL = 1024
EPS = 1e-6
HEAD_DIM = 64
N_Q_HEADS = 16
N_KV_HEADS = 2
GROUP = 8
BLOCK = 128
RET_HEADS = 4
RET_QK_DIM = 256
RET_V_DIM = 512
RET_CHUNK = 128
RET_ROT_BASE = 10000.0
D_FF = 2816
ATT_Q = N_Q_HEADS * HEAD_DIM
ATT_KV = N_KV_HEADS * HEAD_DIM
RET_QK = RET_HEADS * RET_QK_DIM
RET_V = RET_HEADS * RET_V_DIM
D_IN = 9472
ADAM_LR = 0.001
ADAM_B1 = 0.9
ADAM_B2 = 0.999
ADAM_EPS = 1e-08
ADAM_WD = 0.01
ADAM_STEP = 10

N_CHIPS = 4
N_DEV = 8
VMEM_LIMIT_BYTES = 60 * 1024 * 1024

P_QA = (0, 1024)
P_KVA = (1024, 256)
P_QR = (1280, 1024)
P_KR = (2304, 1024)
P_VR = (3328, 2048)
P_GR = (5376, 2048)
P_ZA = (7424, 1024)
P_ZR = (8448, 1024)

W_IN_SH = D_IN // N_CHIPS
FF_SH = D_FF // N_CHIPS
ROWS_SEGS = (256, 512, 256, 704)
ROWS_OFFS = (0, 256, 768, 1024)
ROWS_BLOCK = 1728
EXCH_TILES = (432, 256, 512)
ADAM_ROWS_TILE = 32

SMALL_ROWS = 24
SM_G1, SM_G2, SM_GQ, SM_GK, SM_SINK, SM_LOSS = 0, 8, 16, 17, 18, 19


def _dot(a, b):
    return jnp.dot(a, b, preferred_element_type=F32)


def _dot_nt(a, b):
    return lax.dot_general(a, b, (((1,), (1,)), ((), ())), preferred_element_type=F32)


def _dot_tn(a, b):
    return lax.dot_general(a, b, (((0,), (0,)), ((), ())), preferred_element_type=F32)


def _bf(x):
    return x.astype(BF16)


def _rms_stats(x):
    r = lax.rsqrt(jnp.mean(x * x, axis=-1, keepdims=True) + EPS)
    return r, x * r


def _rms_bwd(dy, xhat, r, gain):
    u = dy * gain
    dx = r * (u - xhat * jnp.mean(u * xhat, axis=-1, keepdims=True))
    return dx, dy * xhat


def _params(sem):
    return pltpu.CompilerParams(dimension_semantics=sem, vmem_limit_bytes=VMEM_LIMIT_BYTES)


def _row_call(body, *, tm, row_ins, res_ins, row_outs, part_outs=(), name):
    t = row_ins[0].shape[0]
    n_tiles = t // tm
    in_specs = [pl.BlockSpec((tm, a.shape[1]), lambda i: (i, 0)) for a in row_ins]
    in_specs += [pl.BlockSpec(a.shape, lambda i: (0, 0), pipeline_mode=pl.Buffered(1)) for a in res_ins]
    out_shape = [jax.ShapeDtypeStruct((t, w), dt) for (w, dt) in row_outs]
    out_shape += [jax.ShapeDtypeStruct((n_tiles, 1, w), F32) for w in part_outs]
    out_specs = [pl.BlockSpec((tm, w), lambda i: (i, 0)) for (w, _) in row_outs]
    out_specs += [pl.BlockSpec((1, 1, w), lambda i: (i, 0, 0)) for w in part_outs]
    n_ri, n_re, n_ro = len(row_ins), len(res_ins), len(row_outs)

    def kern(*refs):
        body(refs[:n_ri], refs[n_ri:n_ri + n_re], refs[n_ri + n_re:n_ri + n_re + n_ro], refs[n_ri + n_re + n_ro:])

    return pl.pallas_call(
        kern, grid=(n_tiles,), in_specs=in_specs, out_specs=out_specs, out_shape=out_shape, name=name,
        compiler_params=_params(("parallel",)),
    )(*row_ins, *res_ins)


def _proj_fwd(x, g1, w_in):
    pieces = ((P_QA, F32), (P_KVA, F32), (P_QR, F32), (P_KR, F32), (P_VR, BF16), (P_GR, F32), (P_ZA, F32), (P_ZR, F32))

    def body(ri, re, ro, po):
        x_t = ri[0][...]
        r, xhat = _rms_stats(x_t)
        hb = _bf(xhat * re[0][...])
        ro[0][...] = hb
        for k, ((off, w), dt) in enumerate(pieces):
            ro[1 + k][...] = _dot(hb, re[1][:, off:off + w]).astype(dt)

    outs = [(D_MODEL, BF16)] + [(w, dt) for ((_, w), dt) in pieces]
    return _row_call(body, tm=256, row_ins=[x], res_ins=[g1, w_in], row_outs=outs, name="proj_fwd")


def _mix_fwd(attn, ret, z_a, z_r, x, wba, wbr, wout, g2):
    def body(ri, re, ro, po):
        ba = _dot(ri[0][...], re[0][...])
        br = _dot(ri[1][...], re[1][...])
        m = jax.nn.sigmoid(ri[2][...]) * ba + jax.nn.sigmoid(ri[3][...]) * br
        mb = _bf(m)
        x1 = ri[4][...] + _dot(mb, re[2][...])
        r, xhat = _rms_stats(x1)
        ro[0][...] = ba
        ro[1][...] = br
        ro[2][...] = mb
        ro[3][...] = x1
        ro[4][...] = _bf(xhat * re[3][...])

    outs = [(D_MODEL, F32), (D_MODEL, F32), (D_MODEL, BF16), (D_MODEL, F32), (D_MODEL, BF16)]
    return _row_call(body, tm=256, row_ins=[attn, ret, z_a, z_r, x], res_ins=[wba, wbr, wout, g2], row_outs=outs,
                     name="mix_fwd")


def _ffn_fwd_bwd(h2, x1, target, wg, wu, wd, g2):
    def body(ri, re, ro, po):
        h2_t = ri[0][...]
        x1_t = ri[1][...]
        gate = _dot(h2_t, re[0][...])
        up = _dot(h2_t, re[1][...])
        sg = jax.nn.sigmoid(gate)
        sl = gate * sg
        actb = _bf(sl * up)
        ro[0][...] = actb
        y = x1_t + _dot(actb, re[2][...])
        e = y - ri[2][...]
        po[0][0] = jnp.broadcast_to(0.5 * jnp.sum(jnp.sum(e * e, axis=1, keepdims=True), axis=0, keepdims=True)
                                    * (1.0 / D_MODEL), (1, 128))
        dy = e * (1.0 / D_MODEL)
        dyb = _bf(dy)
        ro[3][...] = dyb
        dact = _dot_nt(dyb, re[2][...])
        dupb = _bf(dact * sl)
        dgateb = _bf(dact * up * (sg * (1.0 + gate * (1.0 - sg))))
        ro[1][...] = dgateb
        ro[2][...] = dupb
        dh2 = _dot_nt(dgateb, re[0][...]) + _dot_nt(dupb, re[1][...])
        r, xhat = _rms_stats(x1_t)
        dxn, dgain = _rms_bwd(dh2, xhat, r, re[3][...])
        dx1 = dy + dxn
        ro[4][...] = dx1
        ro[5][...] = _bf(dx1)
        po[1][0] = jnp.sum(dgain, axis=0, keepdims=True)

    outs = [(D_FF, BF16), (D_FF, BF16), (D_FF, BF16), (D_MODEL, BF16), (D_MODEL, F32), (D_MODEL, BF16)]
    return _row_call(body, tm=256, row_ins=[h2, x1, target], res_ins=[wg, wu, wd, g2], row_outs=outs,
                     part_outs=(128, D_MODEL), name="ffn_fwd_bwd")


def _mix_bwd(dx1b, z_a, z_r, ba, br, g_r, o_ret, wout, wba, wbr):
    def body(ri, re, ro, po):
        dm = _dot_nt(ri[0][...], re[0][...])
        sa = jax.nn.sigmoid(ri[1][...])
        sr = jax.nn.sigmoid(ri[2][...])
        dbab = _bf(sa * dm)
        dbrb = _bf(sr * dm)
        ro[0][...] = dbab
        ro[1][...] = dbrb
        ro[2][...] = _bf(dm * ri[3][...] * (sa * (1.0 - sa)))
        ro[3][...] = _bf(dm * ri[4][...] * (sr * (1.0 - sr)))
        ro[4][...] = _bf(_dot_nt(dbab, re[1][...]))
        dret = _dot_nt(dbrb, re[2][...])
        for h in range(RET_HEADS):
            cols = slice(h * RET_V_DIM, (h + 1) * RET_V_DIM)
            g = ri[5][:, cols]
            r, rn = _rms_stats(ri[6][:, cols])
            sg = jax.nn.sigmoid(g)
            dret_h = dret[:, cols]
            d_rn = dret_h * (g * sg)
            ro[6][:, cols] = _bf(dret_h * rn * (sg * (1.0 + g * (1.0 - sg))))
            ro[5][:, cols] = r * (d_rn - rn * jnp.mean(d_rn * rn, axis=-1, keepdims=True))

    outs = [(D_MODEL, BF16), (D_MODEL, BF16), (D_MODEL, BF16), (D_MODEL, BF16), (ATT_Q, BF16), (RET_V, F32),
            (RET_V, BF16)]
    return _row_call(body, tm=256, row_ins=[dx1b, z_a, z_r, ba, br, g_r, o_ret], res_ins=[wout, wba, wbr],
                     row_outs=outs, name="mix_bwd")


def _proj_bwd(d_pieces, x, dx1, w_in, g1):
    groups = (P_QA, P_KVA, P_QR, P_KR, P_VR, P_GR, P_ZA, P_ZR)
    n_p = len(groups)

    def body(ri, re, ro, po):
        dh = None
        for k, (off, w) in enumerate(groups):
            term = _dot_nt(ri[k][...], re[0][:, off:off + w])
            dh = term if dh is None else dh + term
        r, xhat = _rms_stats(ri[n_p][...])
        dxn, dgain = _rms_bwd(dh, xhat, r, re[1][...])
        ro[0][...] = ri[n_p + 1][...] + dxn
        po[0][0] = jnp.sum(dgain, axis=0, keepdims=True)

    return _row_call(body, tm=256, row_ins=[*d_pieces, x, dx1], res_ins=[w_in, g1], row_outs=[(D_MODEL, F32)],
                     part_outs=(D_MODEL,), name="proj_bwd")


def _matmul_tn(a, b, *, tm, tn, name):
    t, m = a.shape
    n = b.shape[1]
    tk = min(2048, t)

    def kern(a_ref, b_ref, o_ref):
        k = pl.program_id(2)

        @pl.when(k == 0)
        def _():
            o_ref[...] = jnp.zeros_like(o_ref)

        o_ref[...] += _dot_tn(a_ref[...], b_ref[...])

    return pl.pallas_call(
        kern, grid=(m // tm, n // tn, t // tk),
        in_specs=[pl.BlockSpec((tk, tm), lambda i, j, k: (k, i)), pl.BlockSpec((tk, tn), lambda i, j, k: (k, j))],
        out_specs=pl.BlockSpec((tm, tn), lambda i, j, k: (i, j)),
        out_shape=jax.ShapeDtypeStruct((m, n), F32), name=name,
        compiler_params=_params(("parallel", "parallel", "arbitrary")),
    )(a, b)


def _attn_group(n, kvh, q_ref, kvp_ref, kvc_ref, gq, gk, sink_ref):
    heads = [kvh * GROUP + g for g in range(GROUP)]
    q = jnp.concatenate([q_ref[:, h * HEAD_DIM:(h + 1) * HEAD_DIM] for h in heads], axis=0)
    rq, qhat = _rms_stats(q)
    qnb = _bf(qhat * gq)
    kcols = slice(kvh * HEAD_DIM, (kvh + 1) * HEAD_DIM)
    vcols = slice(ATT_KV + kvh * HEAD_DIM, ATT_KV + (kvh + 1) * HEAD_DIM)
    k = jnp.concatenate([kvp_ref[:, kcols], kvc_ref[:, kcols]], axis=0)
    rk, khat = _rms_stats(k)
    knb = _bf(khat * gk)
    vb = _bf(jnp.concatenate([kvp_ref[:, vcols], kvc_ref[:, vcols]], axis=0))
    s = _dot_nt(qnb, knb) * (HEAD_DIM ** -0.5)
    rows = GROUP * BLOCK
    i = lax.broadcasted_iota(jnp.int32, (rows, 2 * BLOCK), 0) & (BLOCK - 1)
    j = lax.broadcasted_iota(jnp.int32, (rows, 2 * BLOCK), 1)
    allowed = (j > i) & (j <= i + BLOCK) & ((j >= BLOCK) | (n > 0))
    s = jnp.where(allowed, s, -1e30)
    sink = jnp.concatenate([jnp.broadcast_to(sink_ref[0:1, h:h + 1], (BLOCK, 1)) for h in heads], axis=0)
    m = jnp.maximum(jnp.max(s, axis=1, keepdims=True), sink)
    e = jnp.exp(s - m)
    es = jnp.exp(sink - m)
    z = jnp.sum(e, axis=1, keepdims=True) + es
    return dict(heads=heads, qhat=qhat, rq=rq, qnb=qnb, khat=khat, rk=rk, knb=knb, vb=vb, p=e / z, psink=es / z)


def _attn_fwd(q_a, kv_a, gq, gk, sinks):
    t = q_a.shape[0]
    nb = t // BLOCK

    def kern(q_ref, kvp_ref, kvc_ref, gq_ref, gk_ref, sink_ref, o_ref):
        n = pl.program_id(0)
        for kvh in range(N_KV_HEADS):
            a = _attn_group(n, kvh, q_ref, kvp_ref, kvc_ref, gq_ref[...], gk_ref[...], sink_ref)
            out = _dot(_bf(a["p"]), a["vb"])
            for g, h in enumerate(a["heads"]):
                o_ref[:, h * HEAD_DIM:(h + 1) * HEAD_DIM] = _bf(out[g * BLOCK:(g + 1) * BLOCK, :])

    small = lambda a: pl.BlockSpec(a.shape, lambda n: (0, 0))
    return pl.pallas_call(
        kern, grid=(nb,),
        in_specs=[pl.BlockSpec((BLOCK, ATT_Q), lambda n: (n, 0)),
                  pl.BlockSpec((BLOCK, 2 * ATT_KV), lambda n: (jnp.maximum(n - 1, 0), 0)),
                  pl.BlockSpec((BLOCK, 2 * ATT_KV), lambda n: (n, 0)),
                  small(gq), small(gk), small(sinks)],
        out_specs=pl.BlockSpec((BLOCK, ATT_Q), lambda n: (n, 0)),
        out_shape=jax.ShapeDtypeStruct((t, ATT_Q), BF16), name="attn_fwd",
        compiler_params=_params(("parallel",)),
    )(q_a, kv_a, kv_a, gq, gk, sinks)


def _attn_bwd(q_a, kv_a, d_attn, gq, gk, sinks):
    t = q_a.shape[0]
    nb = t // BLOCK

    def kern(q_ref, kvp_ref, kvc_ref, do_ref, gq_ref, gk_ref, sink_ref,
             dq_ref, dkv_ref, dgq_ref, dgk_ref, dsink_ref, band_k, band_v, carry_k, carry_v):
        n = pl.program_id(0)
        gq_v = gq_ref[...]
        gk_v = gk_ref[...]

        @pl.when(n == 0)
        def _():
            carry_k[...] = jnp.zeros_like(carry_k)
            carry_v[...] = jnp.zeros_like(carry_v)
            dgq_ref[...] = jnp.zeros_like(dgq_ref)
            dgk_ref[...] = jnp.zeros_like(dgk_ref)
            dsink_ref[...] = jnp.zeros_like(dsink_ref)

        @pl.when(n == nb)
        def _():
            band_k[...] = jnp.zeros_like(band_k)
            band_v[...] = jnp.zeros_like(band_v)

        @pl.when(n < nb)
        def _():
            lane16 = lax.broadcasted_iota(jnp.int32, (1, N_Q_HEADS), 1)
            dsink = jnp.zeros((1, N_Q_HEADS), F32)
            dgq = jnp.zeros((1, HEAD_DIM), F32)
            for kvh in range(N_KV_HEADS):
                a = _attn_group(n, kvh, q_ref, kvp_ref, kvc_ref, gq_v, gk_v, sink_ref)
                p = a["p"]
                dob = jnp.concatenate([do_ref[:, h * HEAD_DIM:(h + 1) * HEAD_DIM] for h in a["heads"]], axis=0)
                dp = _dot_nt(dob, a["vb"])
                delta = jnp.sum(p * dp, axis=1, keepdims=True)
                dsb = _bf(p * (dp - delta))
                dsk = a["psink"] * delta
                for g, h in enumerate(a["heads"]):
                    tot = jnp.sum(dsk[g * BLOCK:(g + 1) * BLOCK, :], axis=0, keepdims=True)
                    dsink = dsink - jnp.where(lane16 == h, tot, 0.0)
                scale = HEAD_DIM ** -0.5
                dqn = _dot(dsb, a["knb"]) * scale
                band_k[kvh] = _dot_tn(dsb, a["qnb"]) * scale
                band_v[kvh] = _dot_tn(_bf(p), dob)
                dq, dgain = _rms_bwd(dqn, a["qhat"], a["rq"], gq_v)
                dgq = dgq + jnp.sum(dgain, axis=0, keepdims=True)
                for g, h in enumerate(a["heads"]):
                    dq_ref[:, h * HEAD_DIM:(h + 1) * HEAD_DIM] = _bf(dq[g * BLOCK:(g + 1) * BLOCK, :])
            dsink_ref[...] += dsink
            dgq_ref[...] += dgq

        dgk = jnp.zeros((1, HEAD_DIM), F32)
        for kvh in range(N_KV_HEADS):
            kcols = slice(kvh * HEAD_DIM, (kvh + 1) * HEAD_DIM)
            vcols = slice(ATT_KV + kvh * HEAD_DIM, ATT_KV + (kvh + 1) * HEAD_DIM)
            dkn = carry_k[kvh] + band_k[kvh, 0:BLOCK, :]
            dv = carry_v[kvh] + band_v[kvh, 0:BLOCK, :]
            rk, khat = _rms_stats(kvp_ref[:, kcols])
            dk, dgain = _rms_bwd(dkn, khat, rk, gk_v)
            dgk = dgk + jnp.sum(dgain, axis=0, keepdims=True)
            dkv_ref[:, kcols] = _bf(dk)
            dkv_ref[:, vcols] = _bf(dv)
            carry_k[kvh] = band_k[kvh, BLOCK:2 * BLOCK, :]
            carry_v[kvh] = band_v[kvh, BLOCK:2 * BLOCK, :]
        dgk_ref[...] += dgk

    small = lambda a: pl.BlockSpec(a.shape, lambda n: (0, 0))
    last = nb - 1
    return pl.pallas_call(
        kern, grid=(nb + 1,),
        in_specs=[pl.BlockSpec((BLOCK, ATT_Q), lambda n: (jnp.minimum(n, last), 0)),
                  pl.BlockSpec((BLOCK, 2 * ATT_KV), lambda n: (jnp.maximum(n - 1, 0), 0)),
                  pl.BlockSpec((BLOCK, 2 * ATT_KV), lambda n: (jnp.minimum(n, last), 0)),
                  pl.BlockSpec((BLOCK, ATT_Q), lambda n: (jnp.minimum(n, last), 0)),
                  small(gq), small(gk), small(sinks)],
        out_specs=[pl.BlockSpec((BLOCK, ATT_Q), lambda n: (jnp.minimum(n, last), 0)),
                   pl.BlockSpec((BLOCK, 2 * ATT_KV), lambda n: (jnp.maximum(n - 1, 0), 0)),
                   pl.BlockSpec((1, HEAD_DIM), lambda n: (0, 0)),
                   pl.BlockSpec((1, HEAD_DIM), lambda n: (0, 0)),
                   pl.BlockSpec((1, N_Q_HEADS), lambda n: (0, 0))],
        out_shape=[jax.ShapeDtypeStruct((t, ATT_Q), BF16), jax.ShapeDtypeStruct((t, 2 * ATT_KV), BF16),
                   jax.ShapeDtypeStruct((1, HEAD_DIM), F32), jax.ShapeDtypeStruct((1, HEAD_DIM), F32),
                   jax.ShapeDtypeStruct((1, N_Q_HEADS), F32)],
        scratch_shapes=[pltpu.VMEM((N_KV_HEADS, 2 * BLOCK, HEAD_DIM), F32),
                        pltpu.VMEM((N_KV_HEADS, 2 * BLOCK, HEAD_DIM), F32),
                        pltpu.VMEM((N_KV_HEADS, BLOCK, HEAD_DIM), F32),
                        pltpu.VMEM((N_KV_HEADS, BLOCK, HEAD_DIM), F32)],
        name="attn_bwd", compiler_params=_params(("arbitrary",)),
    )(q_a, kv_a, kv_a, d_attn, gq, gk, sinks)


def _ret_tables(t):
    pos = jnp.arange(t, dtype=F32)
    theta = 1.0 / (RET_ROT_BASE ** jnp.linspace(0.0, 1.0, RET_QK_DIM // 2, dtype=F32))
    ang = jnp.repeat(pos[:, None] * theta[None, :], 2, axis=-1)
    sign = jnp.tile(jnp.array([-1.0, 1.0], F32), RET_QK_DIM // 2)
    log_gamma = jnp.log(1.0 - 2.0 ** (-5.0 - jnp.arange(RET_HEADS, dtype=F32)))
    i = jnp.arange(RET_CHUNK, dtype=F32)
    diff = i[:, None] - i[None, :]
    causal = diff >= 0
    decay = jnp.where(causal[None], jnp.exp(jnp.where(causal, diff, 0.0)[None] * log_gamma[:, None, None]), 0.0)
    xi = jnp.exp((i + 1.0)[None, :] * log_gamma[:, None])[:, :, None]
    zeta = jnp.exp((RET_CHUNK - 1.0 - i)[None, :] * log_gamma[:, None])[:, :, None]
    gch = jnp.broadcast_to(jnp.exp(RET_CHUNK * log_gamma)[:, None, None], (RET_HEADS, 1, 128))
    return jnp.cos(ang), jnp.sin(ang) * sign[None, :], decay, xi, zeta, gch


def _swap_pairs(x):
    lane = lax.broadcasted_iota(jnp.int32, x.shape, 1)
    return jnp.where((lane & 1) == 0, pltpu.roll(x, RET_QK_DIM - 1, 1), pltpu.roll(x, 1, 1))


def _rotate(x, cos, sin_s):
    return x * cos + _swap_pairs(x) * sin_s


def _rotate_bwd(dy, cos, sin_s):
    return dy * cos + _swap_pairs(dy * sin_s)


def _ret_specs(order):
    qk = pl.BlockSpec((RET_CHUNK, RET_QK), lambda j: (order(j), 0))
    v = pl.BlockSpec((RET_CHUNK, RET_V), lambda j: (order(j), 0))
    dec = pl.BlockSpec((RET_HEADS, RET_CHUNK, RET_CHUNK), lambda j: (0, 0, 0))
    col = pl.BlockSpec((RET_HEADS, RET_CHUNK, 1), lambda j: (0, 0, 0))
    gch = pl.BlockSpec((RET_HEADS, 1, 128), lambda j: (0, 0, 0))
    st = pl.BlockSpec((RET_HEADS, None, RET_QK_DIM, RET_V_DIM), lambda j: (0, order(j), 0, 0))
    pos = pl.BlockSpec((RET_CHUNK, RET_QK_DIM), lambda j: (order(j), 0))
    return qk, v, dec, col, gch, st, pos


def _ret_fwd(q_r, k_r, v_r, g_r, tables):
    t = q_r.shape[0]
    nc = t // RET_CHUNK
    cos, sin_s, decay, xi, zeta, gch = tables

    def kern(q_ref, k_ref, v_ref, g_ref, cos_ref, sin_ref, dec_ref, xi_ref, zeta_ref, gch_ref,
             o_ref, ret_ref, st_ref, state):
        @pl.when(pl.program_id(0) == 0)
        def _():
            state[...] = jnp.zeros_like(state)

        cos_t = cos_ref[...]
        sin_t = sin_ref[...]
        for h in range(RET_HEADS):
            qc = slice(h * RET_QK_DIM, (h + 1) * RET_QK_DIM)
            vc = slice(h * RET_V_DIM, (h + 1) * RET_V_DIM)
            qs = _bf(_rotate(q_ref[:, qc], cos_t, sin_t))
            ks = _rotate(k_ref[:, qc] * (RET_QK_DIM ** -0.5), cos_t, sin_t)
            vb = v_ref[:, vc]
            s_old = state[h]
            sb = _bf(s_old)
            st_ref[h] = sb
            inner = _dot_nt(qs, _bf(ks)) * dec_ref[h]
            out = _dot(_bf(inner), vb) + _dot(qs, sb) * xi_ref[h]
            state[h] = gch_ref[h, :, 0:1] * s_old + _dot_tn(_bf(ks * zeta_ref[h]), vb)
            o_ref[:, vc] = out
            r, rn = _rms_stats(out)
            g = g_ref[:, vc]
            ret_ref[:, vc] = _bf(g * jax.nn.sigmoid(g) * rn)

    qk, v, dec, col, gsp, st, pos = _ret_specs(lambda j: j)
    return pl.pallas_call(
        kern, grid=(nc,),
        in_specs=[qk, qk, v, v, pos, pos, dec, col, col, gsp],
        out_specs=[v, v, st],
        out_shape=[jax.ShapeDtypeStruct((t, RET_V), F32), jax.ShapeDtypeStruct((t, RET_V), BF16),
                   jax.ShapeDtypeStruct((RET_HEADS, nc, RET_QK_DIM, RET_V_DIM), BF16)],
        scratch_shapes=[pltpu.VMEM((RET_HEADS, RET_QK_DIM, RET_V_DIM), F32)],
        name="ret_fwd", compiler_params=_params(("arbitrary",)),
    )(q_r, k_r, v_r, g_r, cos, sin_s, decay, xi, zeta, gch)


def _ret_bwd(q_r, k_r, v_r, d_o, states, tables):
    t = q_r.shape[0]
    nc = t // RET_CHUNK
    cos, sin_s, decay, xi, zeta, gch = tables

    def kern(q_ref, k_ref, v_ref, do_ref, st_ref, cos_ref, sin_ref, dec_ref, xi_ref, zeta_ref, gch_ref,
             dq_ref, dk_ref, dv_ref, dstate):
        @pl.when(pl.program_id(0) == 0)
        def _():
            dstate[...] = jnp.zeros_like(dstate)

        cos_t = cos_ref[...]
        sin_t = sin_ref[...]
        scale = RET_QK_DIM ** -0.5
        for h in range(RET_HEADS):
            qc = slice(h * RET_QK_DIM, (h + 1) * RET_QK_DIM)
            vc = slice(h * RET_V_DIM, (h + 1) * RET_V_DIM)
            qs = _bf(_rotate(q_ref[:, qc], cos_t, sin_t))
            ks = _rotate(k_ref[:, qc] * scale, cos_t, sin_t)
            ksb = _bf(ks)
            vb = v_ref[:, vc]
            d_o_t = do_ref[:, vc]
            dob = _bf(d_o_t)
            doxb = _bf(d_o_t * xi_ref[h])
            dec = dec_ref[h]
            ds_old = dstate[h]
            dsb = _bf(ds_old)
            pb = _bf(_dot_nt(qs, ksb) * dec)
            dpb = _bf(_dot_nt(dob, vb) * dec)
            dqs = _dot(dpb, ksb) + _dot_nt(doxb, st_ref[h])
            dks = _dot_tn(dpb, qs) + _dot_nt(vb, dsb) * zeta_ref[h]
            dv_ref[:, vc] = _bf(_dot_tn(pb, dob) + _dot(_bf(ks * zeta_ref[h]), dsb))
            dstate[h] = gch_ref[h, :, 0:1] * ds_old + _dot_tn(qs, doxb)
            dq_ref[:, qc] = _bf(_rotate_bwd(dqs, cos_t, sin_t))
            dk_ref[:, qc] = _bf(_rotate_bwd(dks, cos_t, sin_t) * scale)

    qk, v, dec, col, gsp, st, pos = _ret_specs(lambda j: nc - 1 - j)
    return pl.pallas_call(
        kern, grid=(nc,),
        in_specs=[qk, qk, v, v, st, pos, pos, dec, col, col, gsp],
        out_specs=[qk, qk, v],
        out_shape=[jax.ShapeDtypeStruct((t, RET_QK), BF16), jax.ShapeDtypeStruct((t, RET_QK), BF16),
                   jax.ShapeDtypeStruct((t, RET_V), BF16)],
        scratch_shapes=[pltpu.VMEM((RET_HEADS, RET_QK_DIM, RET_V_DIM), F32)],
        name="ret_bwd", compiler_params=_params(("arbitrary",)),
    )(q_r, k_r, v_r, d_o, states, cos, sin_s, decay, xi, zeta, gch)


def _local_step(x, target, w, g1, g2, gq, gk, sinks):
    t = x.shape[0]
    tables = _ret_tables(t)
    h1, q_a, kv_a, q_r, k_r, v_r, g_r, z_a, z_r = _proj_fwd(x, g1, w["w_in"])
    attn = _attn_fwd(q_a, kv_a, gq, gk, sinks)
    o_ret, ret, states = _ret_fwd(q_r, k_r, v_r, g_r, tables)
    ba, br, merged, x1, h2 = _mix_fwd(attn, ret, z_a, z_r, x, w["wba"], w["wbr"], w["wout"], g2)
    act, dgate, dup, dyb, dx1, dx1b, loss_p, dg2_p = _ffn_fwd_bwd(h2, x1, target, w["wg"], w["wu"], w["wd"], g2)
    dba, dbr, dz_a, dz_r, d_attn, d_o, dg_r = _mix_bwd(dx1b, z_a, z_r, ba, br, g_r, o_ret, w["wout"], w["wba"], w["wbr"])
    dq_r, dk_r, dv_r = _ret_bwd(q_r, k_r, v_r, d_o, states, tables)
    dq_a, dkv_a, dgq, dgk, dsinks = _attn_bwd(q_a, kv_a, d_attn, gq, gk, sinks)
    d_pieces = [dq_a, dkv_a, dq_r, dk_r, dv_r, dg_r, dz_a, dz_r]
    grad_x, dg1_p = _proj_bwd(d_pieces, x, dx1, w["w_in"], g1)
    dw_in = [_matmul_tn(h1, dp, tm=D_MODEL, tn=min(dp.shape[1], 1024), name=f"dw_in_{k}")
             for k, dp in enumerate(d_pieces)]
    grads = dict(
        w_in=dw_in,
        wba=_matmul_tn(attn, dba, tm=1024, tn=1024, name="dw_ba"),
        wbr=_matmul_tn(ret, dbr, tm=1024, tn=1024, name="dw_br"),
        wout=_matmul_tn(merged, dx1b, tm=1024, tn=1024, name="dw_out"),
        wg=_matmul_tn(h2, dgate, tm=1024, tn=D_FF // 2, name="dw_gate"),
        wu=_matmul_tn(h2, dup, tm=1024, tn=D_FF // 2, name="dw_up"),
        wd=_matmul_tn(act, dyb, tm=D_FF // 2, tn=1024, name="dw_down"),
    )
    small = dict(loss=loss_p, dg1=dg1_p, dg2=dg2_p, dgq=dgq, dgk=dgk, dsinks=dsinks)
    return grad_x, grads, small


def _position():
    return lax.axis_index("x"), lax.axis_index("y"), lax.axis_index("c")


_ANY = pl.BlockSpec(memory_space=pl.ANY)


def _gather_shards(owns):
    n = len(owns)

    def body(*refs):
        p_refs, o_refs, (send_sems, recv_sems) = refs[:n], refs[n:2 * n], refs[2 * n:]
        x, y, c = _position()
        me = (x, y, c)
        sibling = (x, y, 1 - c)
        chips = [(1 - x, y), (x, 1 - y), (1 - x, 1 - y)]
        my_chip = 2 * x + y

        def slab(a, chip, hf):
            half = owns[a].shape[0] // 2
            return o_refs[a].at[chip, pl.ds(hf * half, half), :]

        def copy(k, src, dst, to):
            return pltpu.make_async_remote_copy(src_ref=src, dst_ref=dst, send_sem=send_sems.at[k],
                                                recv_sem=recv_sems.at[k], device_id=to, device_id_type=MESH)

        first, passed = [], []
        for a in range(n):
            half = owns[a].shape[0] // 2
            for k, (cx, cy) in enumerate(chips):
                cp = copy(6 * a + k, p_refs[a].at[pl.ds(c * half, half), :], slab(a, my_chip, c), (cx, cy, c))
                cp.start()
                first.append(cp)
        for a in range(n):
            for k, (cx, cy) in enumerate(chips):
                landed = slab(a, 2 * cx + cy, c)
                copy(6 * a + k, landed, landed, me).wait_recv()
                cp = copy(6 * a + 3 + k, landed, landed, sibling)
                cp.start()
                passed.append(cp)
        for a in range(n):
            for k, (cx, cy) in enumerate(chips):
                from_sibling = slab(a, 2 * cx + cy, 1 - c)
                copy(6 * a + 3 + k, from_sibling, from_sibling, me).wait_recv()
        for cp in first + passed:
            cp.wait_send()

    return pl.pallas_call(
        body, in_specs=[_ANY] * n, out_specs=[_ANY] * n,
        out_shape=[jax.ShapeDtypeStruct((N_CHIPS, *a.shape), a.dtype) for a in owns],
        scratch_shapes=[pltpu.SemaphoreType.DMA((6 * n,)), pltpu.SemaphoreType.DMA((6 * n,))],
        name="gather_weights",
    )(*owns)


def _pair_exchange(gs):
    n = len(gs)

    def body(*refs):
        g_refs, o_refs, (send_sems, recv_sems) = refs[:n], refs[n:2 * n], refs[2 * n:]
        x, y, c = _position()
        copies = []
        for a in range(n):
            half = gs[a].shape[1] // 2
            for k in range(N_CHIPS):
                cp = pltpu.make_async_remote_copy(
                    src_ref=g_refs[a].at[k, pl.ds((1 - c) * half, half), :], dst_ref=o_refs[a].at[k],
                    send_sem=send_sems.at[N_CHIPS * a + k], recv_sem=recv_sems.at[N_CHIPS * a + k],
                    device_id=(x, y, 1 - c), device_id_type=MESH)
                cp.start()
                copies.append(cp)
        for cp in copies:
            cp.wait()

    return pl.pallas_call(
        body, in_specs=[_ANY] * n, out_specs=[_ANY] * n,
        out_shape=[jax.ShapeDtypeStruct((g.shape[0], g.shape[1] // 2, g.shape[2]), g.dtype) for g in gs],
        scratch_shapes=[pltpu.SemaphoreType.DMA((N_CHIPS * n,)), pltpu.SemaphoreType.DMA((N_CHIPS * n,))],
        name="pair_exchange",
    )(*gs)


def _pair_sum(g, from_sibling, c_arr, *, tile, name):
    n, rows, width = g.shape
    tiles = (rows // 2) // tile

    def kern(c_ref, g_ref, s_ref, o_ref):
        o_ref[...] = _bf(g_ref[...] + s_ref[...])

    return pl.pallas_call(
        kern,
        grid_spec=pltpu.PrefetchScalarGridSpec(
            num_scalar_prefetch=1, grid=(n, tiles),
            in_specs=[pl.BlockSpec((None, tile, width), lambda k, i, c: (k, c[0] * tiles + i, 0)),
                      pl.BlockSpec((None, tile, width), lambda k, i, c: (k, i, 0))],
            out_specs=pl.BlockSpec((None, tile, width), lambda k, i, c: (k, i, 0))),
        out_shape=jax.ShapeDtypeStruct((n, rows // 2, width), BF16), name=name,
        compiler_params=_params(("parallel", "parallel")),
    )(c_arr, g, from_sibling)


def _scatter_to_owners(hsums):
    n = len(hsums)

    def body(*refs):
        h_refs, o_refs, (send_sems, recv_sems) = refs[:n], refs[n:2 * n], refs[2 * n:]
        x, y, c = _position()
        chips = [(1 - x, y), (x, 1 - y), (1 - x, 1 - y)]
        copies = []
        for a in range(n):
            for k, (cx, cy) in enumerate(chips):
                cp = pltpu.make_async_remote_copy(
                    src_ref=h_refs[a].at[2 * cx + cy], dst_ref=o_refs[a].at[k], send_sem=send_sems.at[3 * a + k],
                    recv_sem=recv_sems.at[3 * a + k], device_id=(cx, cy, c), device_id_type=MESH)
                cp.start()
                copies.append(cp)
        for cp in copies:
            cp.wait()

    return pl.pallas_call(
        body, in_specs=[_ANY] * n, out_specs=[_ANY] * n,
        out_shape=[jax.ShapeDtypeStruct((3, *h.shape[1:]), h.dtype) for h in hsums],
        scratch_shapes=[pltpu.SemaphoreType.DMA((3 * n,)), pltpu.SemaphoreType.DMA((3 * n,))],
        name="scatter_to_owners",
    )(*hsums)


def _sum_chips(hsum, parts, chip_arr, *, tile, name):
    n, half, width = parts.shape

    def kern(chip_ref, h_ref, p_ref, o_ref):
        acc = h_ref[...].astype(F32)
        for k in range(n):
            acc = acc + p_ref[k].astype(F32)
        o_ref[...] = acc

    return pl.pallas_call(
        kern,
        grid_spec=pltpu.PrefetchScalarGridSpec(
            num_scalar_prefetch=1, grid=(half // tile,),
            in_specs=[pl.BlockSpec((None, tile, width), lambda i, chip: (chip[0], i, 0)),
                      pl.BlockSpec((n, tile, width), lambda i, chip: (0, i, 0))],
            out_specs=pl.BlockSpec((tile, width), lambda i, chip: (i, 0))),
        out_shape=jax.ShapeDtypeStruct((half, width), F32), name=name,
        compiler_params=_params(("parallel",)),
    )(chip_arr, hsum, parts)


def _share_halves(fhalves):
    n = len(fhalves)

    def body(*refs):
        f_refs, o_refs, (send_sems, recv_sems) = refs[:n], refs[n:2 * n], refs[2 * n:]
        x, y, c = _position()
        copies = [pltpu.make_async_remote_copy(
            src_ref=f_refs[a], dst_ref=o_refs[a], send_sem=send_sems.at[a], recv_sem=recv_sems.at[a],
            device_id=(x, y, 1 - c), device_id_type=MESH) for a in range(n)]
        for cp in copies:
            cp.start()
        for cp in copies:
            cp.wait()

    return pl.pallas_call(
        body, in_specs=[_ANY] * n, out_specs=[_ANY] * n,
        out_shape=[jax.ShapeDtypeStruct(f.shape, f.dtype) for f in fhalves],
        scratch_shapes=[pltpu.SemaphoreType.DMA((n,)), pltpu.SemaphoreType.DMA((n,))], name="share_halves",
    )(*fhalves)


def _adamw_math(w, g, m, v):
    m = ADAM_B1 * m + (1.0 - ADAM_B1) * g
    v = ADAM_B2 * v + (1.0 - ADAM_B2) * (g * g)
    m_hat = m / (1.0 - ADAM_B1 ** ADAM_STEP)
    v_hat = v / (1.0 - ADAM_B2 ** ADAM_STEP)
    delta = -ADAM_LR * (m_hat / (jnp.sqrt(v_hat) + ADAM_EPS) + ADAM_WD * w)
    return delta, m, v


def _adamw(w, m, v, g_mine, g_other, c_arr, *, row_off, tile, name):
    rows, width = w.shape
    tiles_per_half = g_mine.shape[0] // tile
    first = row_off // tile

    def kern(c_ref, w_ref, gm_ref, go_ref, m_ref, v_ref, g_ref, d_ref, nm_ref, nv_ref):
        in_my_half = ((first + pl.program_id(0)) // tiles_per_half) == c_ref[0]
        g = jnp.where(in_my_half, gm_ref[...], go_ref[...])
        g_ref[...] = g
        d_ref[...], nm_ref[...], nv_ref[...] = _adamw_math(w_ref[...], g, m_ref[...], v_ref[...])

    full = pl.BlockSpec((tile, width), lambda i, c: (i, 0))
    half = pl.BlockSpec((tile, width), lambda i, c: ((first + i) % tiles_per_half, 0))
    return pl.pallas_call(
        kern,
        grid_spec=pltpu.PrefetchScalarGridSpec(
            num_scalar_prefetch=1, grid=(rows // tile,), in_specs=[full, half, half, full, full],
            out_specs=[full] * 4),
        out_shape=[jax.ShapeDtypeStruct((rows, width), F32)] * 4, name=name,
        compiler_params=_params(("parallel",)),
    )(c_arr, w, g_mine, g_other, m, v)


def _small_step(part, tile_sums, w, m, v):
    loss_p, dg1_p, dg2_p = tile_sums

    def body(part_ref, loss_ref, dg1_ref, dg2_ref, w_ref, m_ref, v_ref, g_ref, d_ref, nm_ref, nv_ref,
             mine, gathered, send_sems, recv_sems):
        x, y, c = _position()
        me = 4 * x + 2 * y + c
        mine[...] = part_ref[...]
        for r in range(8):
            lanes = slice(128 * r, 128 * (r + 1))
            mine[SM_G1 + r:SM_G1 + r + 1, :] = jnp.sum(dg1_ref[:, lanes], axis=0, keepdims=True)
            mine[SM_G2 + r:SM_G2 + r + 1, :] = jnp.sum(dg2_ref[:, lanes], axis=0, keepdims=True)
        mine[SM_LOSS:SM_LOSS + 1, :] = jnp.sum(loss_ref[...], axis=0, keepdims=True)
        copies = []
        for k in range(1, N_DEV):
            flip = (k >> 2) & 1, (k >> 1) & 1, k & 1
            to = (x ^ flip[0], y ^ flip[1], c ^ flip[2])
            cp = pltpu.make_async_remote_copy(
                src_ref=mine, dst_ref=gathered.at[me], send_sem=send_sems.at[k - 1], recv_sem=recv_sems.at[k - 1],
                device_id=to, device_id_type=MESH)
            cp.start()
            copies.append(cp)
        gathered[me] = mine[...]
        for k in range(1, N_DEV):
            flip = (k >> 2) & 1, (k >> 1) & 1, k & 1
            src = 4 * (x ^ flip[0]) + 2 * (y ^ flip[1]) + (c ^ flip[2])
            pltpu.make_async_remote_copy(
                src_ref=mine, dst_ref=gathered.at[src], send_sem=send_sems.at[k - 1], recv_sem=recv_sems.at[k - 1],
                device_id=(x, y, c), device_id_type=MESH).wait_recv()
        for cp in copies:
            cp.wait_send()
        total = gathered[0]
        for k in range(1, N_DEV):
            total = total + gathered[k]
        g_ref[...] = total
        d_ref[...], nm_ref[...], nv_ref[...] = _adamw_math(w_ref[...], total, m_ref[...], v_ref[...])

    vm = pl.BlockSpec(memory_space=pltpu.VMEM)
    blk = jax.ShapeDtypeStruct((SMALL_ROWS, 128), F32)
    return pl.pallas_call(
        body, in_specs=[vm] * 7, out_specs=[vm] * 4, out_shape=[blk] * 4,
        scratch_shapes=[pltpu.VMEM((SMALL_ROWS, 128), F32), pltpu.VMEM((N_DEV, SMALL_ROWS, 128), F32),
                        pltpu.SemaphoreType.DMA((N_DEV - 1,)), pltpu.SemaphoreType.DMA((N_DEV - 1,))],
        name="small_step",
    )(part, loss_p.reshape(-1, 128), dg1_p.reshape(-1, D_MODEL), dg2_p.reshape(-1, D_MODEL), w, m, v)


def _row_block(wba, wbr, wout, wd):
    return jnp.concatenate([wba, wbr, wout, wd], axis=0)


def _full_weights(own, gathered, my_chip):
    def shard(i, k):
        return jnp.where(my_chip == k, own[i], gathered[i][k])

    rows = [shard(0, k) for k in range(N_CHIPS)]
    ffn = [shard(2, k) for k in range(N_CHIPS)]
    seg = lambda k: slice(ROWS_OFFS[k], ROWS_OFFS[k] + ROWS_SEGS[k])
    cat_rows = lambda k: jnp.concatenate([r[seg(k)] for r in rows], axis=0)
    return dict(
        w_in=jnp.concatenate([shard(1, k) for k in range(N_CHIPS)], axis=1),
        wba=cat_rows(0), wbr=cat_rows(1), wout=cat_rows(2), wd=cat_rows(3),
        wg=jnp.concatenate([f[:D_MODEL] for f in ffn], axis=1),
        wu=jnp.concatenate([f[D_MODEL:] for f in ffn], axis=1))


def _grad_blocks(grads):
    dw_in = jnp.concatenate(grads["w_in"], axis=1)
    rows = jnp.stack([_row_block(*[grads[n][k * s:(k + 1) * s] for n, s in zip(("wba", "wbr", "wout", "wd"), ROWS_SEGS)])
                      for k in range(N_CHIPS)], axis=0)
    w_in = jnp.stack([dw_in[:, k * W_IN_SH:(k + 1) * W_IN_SH] for k in range(N_CHIPS)], axis=0)
    ffn = jnp.stack([jnp.concatenate([grads["wg"][:, k * FF_SH:(k + 1) * FF_SH],
                                      grads["wu"][:, k * FF_SH:(k + 1) * FF_SH]], axis=0)
                     for k in range(N_CHIPS)], axis=0)
    return rows, w_in, ffn


def _pack_small(g1, g2, gq, gk, sinks):
    blk = jnp.zeros((SMALL_ROWS, 128), F32)
    blk = blk.at[SM_G1:SM_G1 + 8].set(g1.reshape(8, 128))
    blk = blk.at[SM_G2:SM_G2 + 8].set(g2.reshape(8, 128))
    blk = blk.at[SM_GQ, :HEAD_DIM].set(gq.reshape(-1))
    blk = blk.at[SM_GK, :HEAD_DIM].set(gk.reshape(-1))
    blk = blk.at[SM_SINK, :N_Q_HEADS].set(sinks.reshape(-1))
    return blk


def _unpack_small(blk):
    return (blk[SM_G1:SM_G1 + 8].reshape(1, D_MODEL), blk[SM_G2:SM_G2 + 8].reshape(1, D_MODEL),
            blk[SM_GQ, :HEAD_DIM].reshape(1, HEAD_DIM), blk[SM_GK, :HEAD_DIM].reshape(1, HEAD_DIM),
            blk[SM_SINK, :N_Q_HEADS].reshape(1, N_Q_HEADS))


def kernel(x, norm_mix_gain, w_in, q_norm_gain, k_norm_gain, attn_sinks, w_branch_attn, w_branch_ret, w_out, norm_ffn_gain, w_ffn_gate, w_ffn_up, w_ffn_down, loss_target, m_norm_mix_gain, m_w_in, m_q_norm_gain, m_k_norm_gain, m_attn_sinks, m_w_branch_attn, m_w_branch_ret, m_w_out, m_norm_ffn_gain, m_w_ffn_gate, m_w_ffn_up, m_w_ffn_down, v_norm_mix_gain, v_w_in, v_q_norm_gain, v_k_norm_gain, v_attn_sinks, v_w_branch_attn, v_w_branch_ret, v_w_out, v_norm_ffn_gain, v_w_ffn_gate, v_w_ffn_up, v_w_ffn_down):
    my_chip = 2 * lax.axis_index("x") + lax.axis_index("y")
    c_arr = lax.axis_index("c").astype(jnp.int32).reshape(1)
    chip_arr = my_chip.astype(jnp.int32).reshape(1)

    own = (_bf(_row_block(w_branch_attn[0], w_branch_ret[0], w_out[0], w_ffn_down[0])), _bf(w_in[0]),
           _bf(jnp.concatenate([w_ffn_gate[0], w_ffn_up[0]], axis=0)))
    weights = _full_weights(own, _gather_shards(own), my_chip)
    grad_x, grads, small = _local_step(x[0], loss_target[0], weights, norm_mix_gain, norm_ffn_gain, q_norm_gain,
                                       k_norm_gain, attn_sinks)

    g_blocks = _grad_blocks(grads)
    from_sibling = _pair_exchange(g_blocks)
    names = ("rows", "w_in", "ffn")
    chip_sums = [_pair_sum(g, s, c_arr, tile=t, name=f"pair_sum_{n}")
                 for g, s, t, n in zip(g_blocks, from_sibling, EXCH_TILES, names)]
    received = _scatter_to_owners(chip_sums)
    g_halves = [_sum_chips(h, r, chip_arr, tile=t, name=f"sum_chips_{n}")
                for h, r, t, n in zip(chip_sums, received, EXCH_TILES, names)]
    g_others = _share_halves(g_halves)

    def update(name, blk, row_off, tile):
        w, m, v = {"w_in": (w_in, m_w_in, v_w_in), "wba": (w_branch_attn, m_w_branch_attn, v_w_branch_attn),
                   "wbr": (w_branch_ret, m_w_branch_ret, v_w_branch_ret), "wout": (w_out, m_w_out, v_w_out),
                   "wg": (w_ffn_gate, m_w_ffn_gate, v_w_ffn_gate), "wu": (w_ffn_up, m_w_ffn_up, v_w_ffn_up),
                   "wd": (w_ffn_down, m_w_ffn_down, v_w_ffn_down)}[name]
        return _adamw(w[0], m[0], v[0], g_halves[blk], g_others[blk], c_arr, row_off=row_off, tile=tile,
                      name=f"adamw_{name}")

    big = dict(w_in=update("w_in", 1, 0, 256), wg=update("wg", 2, 0, 512), wu=update("wu", 2, D_MODEL, 512),
               **{n: update(n, 0, ROWS_OFFS[k], ADAM_ROWS_TILE) for k, n in enumerate(("wba", "wbr", "wout", "wd"))})

    zeros = jnp.zeros((1, D_MODEL), F32)
    part = _pack_small(zeros, zeros, small["dgq"], small["dgk"], small["dsinks"])
    sm_w = _pack_small(norm_mix_gain, norm_ffn_gain, q_norm_gain, k_norm_gain, attn_sinks)
    sm_m = _pack_small(m_norm_mix_gain, m_norm_ffn_gain, m_q_norm_gain, m_k_norm_gain, m_attn_sinks)
    sm_v = _pack_small(v_norm_mix_gain, v_norm_ffn_gain, v_q_norm_gain, v_k_norm_gain, v_attn_sinks)
    sm_g, sm_d, sm_nm, sm_nv = _small_step(part, (small["loss"], small["dg1"], small["dg2"]), sm_w, sm_m, sm_v)
    loss = sm_g[SM_LOSS, 0]

    def leaves(i, sm):
        b = [big[n][i][None] for n in ("w_in", "wba", "wbr", "wout", "wg", "wu", "wd")]
        s1, s2, sq, sk, ss = _unpack_small(sm)
        return [s1, b[0], sq, sk, ss, b[1], b[2], b[3], s2, b[4], b[5], b[6]]

    return (loss, grad_x[None], *leaves(0, sm_g), *leaves(1, sm_d), *leaves(2, sm_nm), *leaves(3, sm_nv))
```

```python
import jax
import jax.numpy as jnp
from jax import lax
from jax.experimental import pallas as pl
from jax.experimental.pallas import tpu as pltpu

F32 = jnp.float32
BF16 = jnp.bfloat16
MESH = pl.DeviceIdType.MESH

D_MODEL = 1024
EPS = 1e-6
HEAD_DIM = 64
N_Q_HEADS = 16
N_KV_HEADS = 2
GROUP = 8
BLOCK = 128
RET_HEADS = 4
RET_QK_DIM = 256
RET_V_DIM = 512
RET_CHUNK = 128
RET_ROT_BASE = 10000.0
D_FF = 2816
ATT_Q = N_Q_HEADS * HEAD_DIM
ATT_KV = N_KV_HEADS * HEAD_DIM
RET_QK = RET_HEADS * RET_QK_DIM
RET_V = RET_HEADS * RET_V_DIM
D_IN = 9472
ADAM_LR = 0.001
ADAM_B1 = 0.9
ADAM_B2 = 0.999
ADAM_EPS = 1e-08
ADAM_WD = 0.01
ADAM_STEP = 10

N_CHIPS = 4
N_DEV = 8
VMEM_LIMIT_BYTES = 60 * 1024 * 1024

P_QA = (0, 1024)
P_KVA = (1024, 256)
P_QR = (1280, 1024)
P_KR = (2304, 1024)
P_VR = (3328, 2048)
P_GR = (5376, 2048)
P_ZA = (7424, 1024)
P_ZR = (8448, 1024)

W_IN_SH = D_IN // N_CHIPS
FF_SH = D_FF // N_CHIPS

SMALL_ROWS = 24
SM_G1, SM_G2, SM_GQ, SM_GK, SM_SINK, SM_LOSS = 0, 8, 16, 17, 18, 19


def _dot(a, b):
    return jnp.dot(a, b, preferred_element_type=F32)


def _dot_nt(a, b):
    return lax.dot_general(a, b, (((1,), (1,)), ((), ())), preferred_element_type=F32)


def _dot_tn(a, b):
    return lax.dot_general(a, b, (((0,), (0,)), ((), ())), preferred_element_type=F32)


def _bf(x):
    return x.astype(BF16)


def _rms_stats(x):
    r = lax.rsqrt(jnp.mean(x * x, axis=-1, keepdims=True) + EPS)
    return r, x * r


def _rms_bwd(dy, xhat, r, gain):
    u = dy * gain
    dx = r * (u - xhat * jnp.mean(u * xhat, axis=-1, keepdims=True))
    return dx, dy * xhat


def _params(sem):
    return pltpu.CompilerParams(dimension_semantics=sem, vmem_limit_bytes=VMEM_LIMIT_BYTES)


_ANY = pl.BlockSpec(memory_space=pl.ANY)


class _Exchange:
    def __init__(self, ins, outs, n_sems, phases):
        self.ins, self.outs, self.n_sems, self.phases = list(ins), list(outs), n_sems, list(phases)


def _merge_exchanges(a, b):
    na_i, na_o, shift = len(a.ins), len(a.outs), a.n_sems

    def first(fn):
        return lambda i, o, s, r, base: fn(i[:na_i], o[:na_o], s, r, base)

    def second(fn):
        return lambda i, o, s, r, base: fn(i[na_i:], o[na_o:], s, r, base + shift)

    phases = [(f, first(fn)) for f, fn in a.phases] + [(f, second(fn)) for f, fn in b.phases]
    return _Exchange(a.ins + b.ins, a.outs + b.outs, a.n_sems + b.n_sems, sorted(phases, key=lambda p: p[0]))


def _pallas(kern, *, grid, in_specs, out_specs, out_shape, args, name, scratch=(), exchange=None):
    if exchange is None:
        return pl.pallas_call(
            kern, grid=grid, in_specs=in_specs, out_specs=out_specs, out_shape=out_shape, name=name,
            scratch_shapes=list(scratch), compiler_params=_params(("arbitrary",) * len(grid)))(*args)
    n_in, n_out, n_sc = len(in_specs), len(out_specs), len(scratch)
    n_xi, n_xo = len(exchange.ins), len(exchange.outs)
    n_steps = 1
    for g in grid:
        n_steps *= g

    def wrapped(*refs):
        ins, refs = refs[:n_in], refs[n_in:]
        x_ins, refs = refs[:n_xi], refs[n_xi:]
        outs, refs = refs[:n_out], refs[n_out:]
        x_outs, refs = refs[:n_xo], refs[n_xo:]
        scr, (send_sems, recv_sems) = refs[:n_sc], refs[n_sc:]
        step = pl.program_id(0)
        for d in range(1, len(grid)):
            step = step * grid[d] + pl.program_id(d)
        for frac, fn in exchange.phases:
            at = min(int(frac * n_steps), n_steps - 1)

            @pl.when(step == at)
            def _(fn=fn):
                fn(x_ins, x_outs, send_sems, recv_sems, 0)

        kern(*ins, *outs, *scr)

    sems = [pltpu.SemaphoreType.DMA((exchange.n_sems,)), pltpu.SemaphoreType.DMA((exchange.n_sems,))]
    return pl.pallas_call(
        wrapped, grid=grid, in_specs=list(in_specs) + [_ANY] * n_xi, out_specs=list(out_specs) + [_ANY] * n_xo,
        out_shape=list(out_shape) + exchange.outs, name=name, scratch_shapes=list(scratch) + sems,
        compiler_params=_params(("arbitrary",) * len(grid)))(*args, *exchange.ins)


def _run_exchange(exchange, name):
    def body(*refs):
        n_i, n_o = len(exchange.ins), len(exchange.outs)
        for _, fn in exchange.phases:
            fn(refs[:n_i], refs[n_i:n_i + n_o], refs[n_i + n_o], refs[n_i + n_o + 1], 0)

    sems = [pltpu.SemaphoreType.DMA((exchange.n_sems,)), pltpu.SemaphoreType.DMA((exchange.n_sems,))]
    return pl.pallas_call(body, in_specs=[_ANY] * len(exchange.ins), out_specs=[_ANY] * len(exchange.outs),
                          out_shape=exchange.outs, scratch_shapes=sems, name=name)(*exchange.ins)


def _row_call(body, *, tm, row_ins, res_ins, row_outs, part_outs=(), name, exchange=None):
    t = row_ins[0].shape[0]
    n_tiles = t // tm
    in_specs = [pl.BlockSpec((tm, a.shape[1]), lambda i: (i, 0)) for a in row_ins]
    in_specs += [pl.BlockSpec(a.shape, lambda i: (0, 0), pipeline_mode=pl.Buffered(1)) for a in res_ins]
    out_shape = [jax.ShapeDtypeStruct((t, w), dt) for (w, dt) in row_outs]
    out_shape += [jax.ShapeDtypeStruct((n_tiles, 1, w), F32) for w in part_outs]
    out_specs = [pl.BlockSpec((tm, w), lambda i: (i, 0)) for (w, _) in row_outs]
    out_specs += [pl.BlockSpec((1, 1, w), lambda i: (i, 0, 0)) for w in part_outs]
    n_ri, n_re, n_ro = len(row_ins), len(res_ins), len(row_outs)

    def kern(*refs):
        body(refs[:n_ri], refs[n_ri:n_ri + n_re], refs[n_ri + n_re:n_ri + n_re + n_ro], refs[n_ri + n_re + n_ro:])

    return _pallas(kern, grid=(n_tiles,), in_specs=in_specs, out_specs=out_specs, out_shape=out_shape,
                   args=[*row_ins, *res_ins], name=name, exchange=exchange)


def _proj_fwd(x, g1, w_in, exchange):
    pieces = ((P_QA, F32), (P_KVA, F32), (P_QR, F32), (P_KR, F32), (P_VR, BF16), (P_GR, F32), (P_ZA, F32), (P_ZR, F32))

    def body(ri, re, ro, po):
        x_t = ri[0][...]
        r, xhat = _rms_stats(x_t)
        hb = _bf(xhat * re[0][...])
        ro[0][...] = hb
        for k, ((off, w), dt) in enumerate(pieces):
            ro[1 + k][...] = _dot(hb, re[1][:, off:off + w]).astype(dt)

    outs = [(D_MODEL, BF16)] + [(w, dt) for ((_, w), dt) in pieces]
    return _row_call(body, tm=256, row_ins=[x], res_ins=[g1, w_in], row_outs=outs, name="proj_fwd", exchange=exchange)


def _mix_fwd(attn, ret, z_a, z_r, x, wba, wbr, wout, g2):
    def body(ri, re, ro, po):
        ba = _dot(ri[0][...], re[0][...])
        br = _dot(ri[1][...], re[1][...])
        m = jax.nn.sigmoid(ri[2][...]) * ba + jax.nn.sigmoid(ri[3][...]) * br
        mb = _bf(m)
        x1 = ri[4][...] + _dot(mb, re[2][...])
        r, xhat = _rms_stats(x1)
        ro[0][...] = ba
        ro[1][...] = br
        ro[2][...] = mb
        ro[3][...] = x1
        ro[4][...] = _bf(xhat * re[3][...])

    outs = [(D_MODEL, F32), (D_MODEL, F32), (D_MODEL, BF16), (D_MODEL, F32), (D_MODEL, BF16)]
    return _row_call(body, tm=256, row_ins=[attn, ret, z_a, z_r, x], res_ins=[wba, wbr, wout, g2], row_outs=outs,
                     name="mix_fwd")


def _ffn_fwd_bwd(h2, x1, target, wg, wu, wd, g2):
    def body(ri, re, ro, po):
        h2_t = ri[0][...]
        x1_t = ri[1][...]
        gate = _dot(h2_t, re[0][...])
        up = _dot(h2_t, re[1][...])
        sg = jax.nn.sigmoid(gate)
        sl = gate * sg
        actb = _bf(sl * up)
        ro[0][...] = actb
        y = x1_t + _dot(actb, re[2][...])
        e = y - ri[2][...]
        po[0][0] = jnp.broadcast_to(0.5 * jnp.sum(jnp.sum(e * e, axis=1, keepdims=True), axis=0, keepdims=True)
                                    * (1.0 / D_MODEL), (1, 128))
        dy = e * (1.0 / D_MODEL)
        dyb = _bf(dy)
        ro[3][...] = dyb
        dact = _dot_nt(dyb, re[2][...])
        dupb = _bf(dact * sl)
        dgateb = _bf(dact * up * (sg * (1.0 + gate * (1.0 - sg))))
        ro[1][...] = dgateb
        ro[2][...] = dupb
        dh2 = _dot_nt(dgateb, re[0][...]) + _dot_nt(dupb, re[1][...])
        r, xhat = _rms_stats(x1_t)
        dxn, dgain = _rms_bwd(dh2, xhat, r, re[3][...])
        dx1 = dy + dxn
        ro[4][...] = dx1
        ro[5][...] = _bf(dx1)
        po[1][0] = jnp.sum(dgain, axis=0, keepdims=True)

    outs = [(D_FF, BF16), (D_FF, BF16), (D_FF, BF16), (D_MODEL, BF16), (D_MODEL, F32), (D_MODEL, BF16)]
    return _row_call(body, tm=256, row_ins=[h2, x1, target], res_ins=[wg, wu, wd, g2], row_outs=outs,
                     part_outs=(128, D_MODEL), name="ffn_fwd_bwd")


def _mix_bwd(dx1b, z_a, z_r, ba, br, g_r, o_ret, wout, wba, wbr, exchange):
    def body(ri, re, ro, po):
        dm = _dot_nt(ri[0][...], re[0][...])
        sa = jax.nn.sigmoid(ri[1][...])
        sr = jax.nn.sigmoid(ri[2][...])
        dbab = _bf(sa * dm)
        dbrb = _bf(sr * dm)
        ro[0][...] = dbab
        ro[1][...] = dbrb
        ro[2][...] = _bf(dm * ri[3][...] * (sa * (1.0 - sa)))
        ro[3][...] = _bf(dm * ri[4][...] * (sr * (1.0 - sr)))
        ro[4][...] = _bf(_dot_nt(dbab, re[1][...]))
        dret = _dot_nt(dbrb, re[2][...])
        for h in range(RET_HEADS):
            cols = slice(h * RET_V_DIM, (h + 1) * RET_V_DIM)
            g = ri[5][:, cols]
            r, rn = _rms_stats(ri[6][:, cols])
            sg = jax.nn.sigmoid(g)
            dret_h = dret[:, cols]
            d_rn = dret_h * (g * sg)
            ro[6][:, cols] = _bf(dret_h * rn * (sg * (1.0 + g * (1.0 - sg))))
            ro[5][:, cols] = r * (d_rn - rn * jnp.mean(d_rn * rn, axis=-1, keepdims=True))

    outs = [(D_MODEL, BF16), (D_MODEL, BF16), (D_MODEL, BF16), (D_MODEL, BF16), (ATT_Q, BF16), (RET_V, F32),
            (RET_V, BF16)]
    return _row_call(body, tm=256, row_ins=[dx1b, z_a, z_r, ba, br, g_r, o_ret], res_ins=[wout, wba, wbr],
                     row_outs=outs, name="mix_bwd", exchange=exchange)


def _proj_bwd(d_pieces, x, dx1, w_in, g1, exchange):
    groups = (P_QA, P_KVA, P_QR, P_KR, P_VR, P_GR, P_ZA, P_ZR)
    n_p = len(groups)

    def body(ri, re, ro, po):
        dh = None
        for k, (off, w) in enumerate(groups):
            term = _dot_nt(ri[k][...], re[0][:, off:off + w])
            dh = term if dh is None else dh + term
        r, xhat = _rms_stats(ri[n_p][...])
        dxn, dgain = _rms_bwd(dh, xhat, r, re[1][...])
        ro[0][...] = ri[n_p + 1][...] + dxn
        po[0][0] = jnp.sum(dgain, axis=0, keepdims=True)

    return _row_call(body, tm=256, row_ins=[*d_pieces, x, dx1], res_ins=[w_in, g1], row_outs=[(D_MODEL, F32)],
                     part_outs=(D_MODEL,), name="proj_bwd", exchange=exchange)


def _matmul_tn(a, b, *, tm, tn, name, exchange=None):
    t, m = a.shape
    n = b.shape[1]
    tk = min(2048, t)

    def kern(a_ref, b_ref, o_ref):
        k = pl.program_id(2)

        @pl.when(k == 0)
        def _():
            o_ref[...] = jnp.zeros_like(o_ref)

        o_ref[...] += _dot_tn(a_ref[...], b_ref[...])

    return _pallas(
        kern, grid=(m // tm, n // tn, t // tk),
        in_specs=[pl.BlockSpec((tk, tm), lambda i, j, k: (k, i)), pl.BlockSpec((tk, tn), lambda i, j, k: (k, j))],
        out_specs=[pl.BlockSpec((tm, tn), lambda i, j, k: (i, j))],
        out_shape=[jax.ShapeDtypeStruct((m, n), F32)], args=[a, b], name=name, exchange=exchange)


def _attn_group(n, kvh, q_ref, kvp_ref, kvc_ref, gq, gk, sink_ref):
    heads = [kvh * GROUP + g for g in range(GROUP)]
    q = jnp.concatenate([q_ref[:, h * HEAD_DIM:(h + 1) * HEAD_DIM] for h in heads], axis=0)
    rq, qhat = _rms_stats(q)
    qnb = _bf(qhat * gq)
    kcols = slice(kvh * HEAD_DIM, (kvh + 1) * HEAD_DIM)
    vcols = slice(ATT_KV + kvh * HEAD_DIM, ATT_KV + (kvh + 1) * HEAD_DIM)
    k = jnp.concatenate([kvp_ref[:, kcols], kvc_ref[:, kcols]], axis=0)
    rk, khat = _rms_stats(k)
    knb = _bf(khat * gk)
    vb = _bf(jnp.concatenate([kvp_ref[:, vcols], kvc_ref[:, vcols]], axis=0))
    s = _dot_nt(qnb, knb) * (HEAD_DIM ** -0.5)
    rows = GROUP * BLOCK
    i = lax.broadcasted_iota(jnp.int32, (rows, 2 * BLOCK), 0) & (BLOCK - 1)
    j = lax.broadcasted_iota(jnp.int32, (rows, 2 * BLOCK), 1)
    allowed = (j > i) & (j <= i + BLOCK) & ((j >= BLOCK) | (n > 0))
    s = jnp.where(allowed, s, -1e30)
    sink = jnp.concatenate([jnp.broadcast_to(sink_ref[0:1, h:h + 1], (BLOCK, 1)) for h in heads], axis=0)
    m = jnp.maximum(jnp.max(s, axis=1, keepdims=True), sink)
    e = jnp.exp(s - m)
    es = jnp.exp(sink - m)
    z = jnp.sum(e, axis=1, keepdims=True) + es
    return dict(heads=heads, qhat=qhat, rq=rq, qnb=qnb, khat=khat, rk=rk, knb=knb, vb=vb, p=e / z, psink=es / z)


def _attn_fwd(q_a, kv_a, gq, gk, sinks):
    t = q_a.shape[0]
    nb = t // BLOCK

    def kern(q_ref, kvp_ref, kvc_ref, gq_ref, gk_ref, sink_ref, o_ref):
        n = pl.program_id(0)
        for kvh in range(N_KV_HEADS):
            a = _attn_group(n, kvh, q_ref, kvp_ref, kvc_ref, gq_ref[...], gk_ref[...], sink_ref)
            out = _dot(_bf(a["p"]), a["vb"])
            for g, h in enumerate(a["heads"]):
                o_ref[:, h * HEAD_DIM:(h + 1) * HEAD_DIM] = _bf(out[g * BLOCK:(g + 1) * BLOCK, :])

    small = lambda a: pl.BlockSpec(a.shape, lambda n: (0, 0))
    return pl.pallas_call(
        kern, grid=(nb,),
        in_specs=[pl.BlockSpec((BLOCK, ATT_Q), lambda n: (n, 0)),
                  pl.BlockSpec((BLOCK, 2 * ATT_KV), lambda n: (jnp.maximum(n - 1, 0), 0)),
                  pl.BlockSpec((BLOCK, 2 * ATT_KV), lambda n: (n, 0)),
                  small(gq), small(gk), small(sinks)],
        out_specs=pl.BlockSpec((BLOCK, ATT_Q), lambda n: (n, 0)),
        out_shape=jax.ShapeDtypeStruct((t, ATT_Q), BF16), name="attn_fwd",
        compiler_params=_params(("parallel",)),
    )(q_a, kv_a, kv_a, gq, gk, sinks)


def _attn_bwd(q_a, kv_a, d_attn, gq, gk, sinks, exchange):
    t = q_a.shape[0]
    nb = t // BLOCK

    def kern(q_ref, kvp_ref, kvc_ref, do_ref, gq_ref, gk_ref, sink_ref,
             dq_ref, dkv_ref, dgq_ref, dgk_ref, dsink_ref, band_k, band_v, carry_k, carry_v):
        n = pl.program_id(0)
        gq_v = gq_ref[...]
        gk_v = gk_ref[...]

        @pl.when(n == 0)
        def _():
            carry_k[...] = jnp.zeros_like(carry_k)
            carry_v[...] = jnp.zeros_like(carry_v)
            dgq_ref[...] = jnp.zeros_like(dgq_ref)
            dgk_ref[...] = jnp.zeros_like(dgk_ref)
            dsink_ref[...] = jnp.zeros_like(dsink_ref)

        @pl.when(n == nb)
        def _():
            band_k[...] = jnp.zeros_like(band_k)
            band_v[...] = jnp.zeros_like(band_v)

        @pl.when(n < nb)
        def _():
            lane16 = lax.broadcasted_iota(jnp.int32, (1, N_Q_HEADS), 1)
            dsink = jnp.zeros((1, N_Q_HEADS), F32)
            dgq = jnp.zeros((1, HEAD_DIM), F32)
            for kvh in range(N_KV_HEADS):
                a = _attn_group(n, kvh, q_ref, kvp_ref, kvc_ref, gq_v, gk_v, sink_ref)
                p = a["p"]
                dob = jnp.concatenate([do_ref[:, h * HEAD_DIM:(h + 1) * HEAD_DIM] for h in a["heads"]], axis=0)
                dp = _dot_nt(dob, a["vb"])
                delta = jnp.sum(p * dp, axis=1, keepdims=True)
                dsb = _bf(p * (dp - delta))
                dsk = a["psink"] * delta
                for g, h in enumerate(a["heads"]):
                    tot = jnp.sum(dsk[g * BLOCK:(g + 1) * BLOCK, :], axis=0, keepdims=True)
                    dsink = dsink - jnp.where(lane16 == h, tot, 0.0)
                scale = HEAD_DIM ** -0.5
                dqn = _dot(dsb, a["knb"]) * scale
                band_k[kvh] = _dot_tn(dsb, a["qnb"]) * scale
                band_v[kvh] = _dot_tn(_bf(p), dob)
                dq, dgain = _rms_bwd(dqn, a["qhat"], a["rq"], gq_v)
                dgq = dgq + jnp.sum(dgain, axis=0, keepdims=True)
                for g, h in enumerate(a["heads"]):
                    dq_ref[:, h * HEAD_DIM:(h + 1) * HEAD_DIM] = _bf(dq[g * BLOCK:(g + 1) * BLOCK, :])
            dsink_ref[...] += dsink
            dgq_ref[...] += dgq

        dgk = jnp.zeros((1, HEAD_DIM), F32)
        for kvh in range(N_KV_HEADS):
            kcols = slice(kvh * HEAD_DIM, (kvh + 1) * HEAD_DIM)
            vcols = slice(ATT_KV + kvh * HEAD_DIM, ATT_KV + (kvh + 1) * HEAD_DIM)
            dkn = carry_k[kvh] + band_k[kvh, 0:BLOCK, :]
            dv = carry_v[kvh] + band_v[kvh, 0:BLOCK, :]
            rk, khat = _rms_stats(kvp_ref[:, kcols])
            dk, dgain = _rms_bwd(dkn, khat, rk, gk_v)
            dgk = dgk + jnp.sum(dgain, axis=0, keepdims=True)
            dkv_ref[:, kcols] = _bf(dk)
            dkv_ref[:, vcols] = _bf(dv)
            carry_k[kvh] = band_k[kvh, BLOCK:2 * BLOCK, :]
            carry_v[kvh] = band_v[kvh, BLOCK:2 * BLOCK, :]
        dgk_ref[...] += dgk

    small = lambda a: pl.BlockSpec(a.shape, lambda n: (0, 0))
    last = nb - 1
    return _pallas(
        kern, grid=(nb + 1,),
        in_specs=[pl.BlockSpec((BLOCK, ATT_Q), lambda n: (jnp.minimum(n, last), 0)),
                  pl.BlockSpec((BLOCK, 2 * ATT_KV), lambda n: (jnp.maximum(n - 1, 0), 0)),
                  pl.BlockSpec((BLOCK, 2 * ATT_KV), lambda n: (jnp.minimum(n, last), 0)),
                  pl.BlockSpec((BLOCK, ATT_Q), lambda n: (jnp.minimum(n, last), 0)),
                  small(gq), small(gk), small(sinks)],
        out_specs=[pl.BlockSpec((BLOCK, ATT_Q), lambda n: (jnp.minimum(n, last), 0)),
                   pl.BlockSpec((BLOCK, 2 * ATT_KV), lambda n: (jnp.maximum(n - 1, 0), 0)),
                   pl.BlockSpec((1, HEAD_DIM), lambda n: (0, 0)),
                   pl.BlockSpec((1, HEAD_DIM), lambda n: (0, 0)),
                   pl.BlockSpec((1, N_Q_HEADS), lambda n: (0, 0))],
        out_shape=[jax.ShapeDtypeStruct((t, ATT_Q), BF16), jax.ShapeDtypeStruct((t, 2 * ATT_KV), BF16),
                   jax.ShapeDtypeStruct((1, HEAD_DIM), F32), jax.ShapeDtypeStruct((1, HEAD_DIM), F32),
                   jax.ShapeDtypeStruct((1, N_Q_HEADS), F32)],
        scratch=[pltpu.VMEM((N_KV_HEADS, 2 * BLOCK, HEAD_DIM), F32),
                 pltpu.VMEM((N_KV_HEADS, 2 * BLOCK, HEAD_DIM), F32),
                 pltpu.VMEM((N_KV_HEADS, BLOCK, HEAD_DIM), F32),
                 pltpu.VMEM((N_KV_HEADS, BLOCK, HEAD_DIM), F32)],
        args=[q_a, kv_a, kv_a, d_attn, gq, gk, sinks], name="attn_bwd", exchange=exchange)


def _ret_tables(t):
    pos = jnp.arange(t, dtype=F32)
    theta = 1.0 / (RET_ROT_BASE ** jnp.linspace(0.0, 1.0, RET_QK_DIM // 2, dtype=F32))
    ang = jnp.repeat(pos[:, None] * theta[None, :], 2, axis=-1)
    sign = jnp.tile(jnp.array([-1.0, 1.0], F32), RET_QK_DIM // 2)
    log_gamma = jnp.log(1.0 - 2.0 ** (-5.0 - jnp.arange(RET_HEADS, dtype=F32)))
    i = jnp.arange(RET_CHUNK, dtype=F32)
    diff = i[:, None] - i[None, :]
    causal = diff >= 0
    decay = jnp.where(causal[None], jnp.exp(jnp.where(causal, diff, 0.0)[None] * log_gamma[:, None, None]), 0.0)
    xi = jnp.exp((i + 1.0)[None, :] * log_gamma[:, None])[:, :, None]
    zeta = jnp.exp((RET_CHUNK - 1.0 - i)[None, :] * log_gamma[:, None])[:, :, None]
    gch = jnp.broadcast_to(jnp.exp(RET_CHUNK * log_gamma)[:, None, None], (RET_HEADS, 1, 128))
    return jnp.cos(ang), jnp.sin(ang) * sign[None, :], decay, xi, zeta, gch


def _swap_pairs(x):
    lane = lax.broadcasted_iota(jnp.int32, x.shape, 1)
    return jnp.where((lane & 1) == 0, pltpu.roll(x, RET_QK_DIM - 1, 1), pltpu.roll(x, 1, 1))


def _rotate(x, cos, sin_s):
    return x * cos + _swap_pairs(x) * sin_s


def _rotate_bwd(dy, cos, sin_s):
    return dy * cos + _swap_pairs(dy * sin_s)


def _ret_specs(order):
    qk = pl.BlockSpec((RET_CHUNK, RET_QK), lambda j: (order(j), 0))
    v = pl.BlockSpec((RET_CHUNK, RET_V), lambda j: (order(j), 0))
    dec = pl.BlockSpec((RET_HEADS, RET_CHUNK, RET_CHUNK), lambda j: (0, 0, 0))
    col = pl.BlockSpec((RET_HEADS, RET_CHUNK, 1), lambda j: (0, 0, 0))
    gch = pl.BlockSpec((RET_HEADS, 1, 128), lambda j: (0, 0, 0))
    st = pl.BlockSpec((RET_HEADS, None, RET_QK_DIM, RET_V_DIM), lambda j: (0, order(j), 0, 0))
    pos = pl.BlockSpec((RET_CHUNK, RET_QK_DIM), lambda j: (order(j), 0))
    return qk, v, dec, col, gch, st, pos


def _ret_fwd(q_r, k_r, v_r, g_r, tables):
    t = q_r.shape[0]
    nc = t // RET_CHUNK
    cos, sin_s, decay, xi, zeta, gch = tables

    def kern(q_ref, k_ref, v_ref, g_ref, cos_ref, sin_ref, dec_ref, xi_ref, zeta_ref, gch_ref,
             o_ref, ret_ref, st_ref, state):
        @pl.when(pl.program_id(0) == 0)
        def _():
            state[...] = jnp.zeros_like(state)

        cos_t = cos_ref[...]
        sin_t = sin_ref[...]
        for h in range(RET_HEADS):
            qc = slice(h * RET_QK_DIM, (h + 1) * RET_QK_DIM)
            vc = slice(h * RET_V_DIM, (h + 1) * RET_V_DIM)
            qs = _bf(_rotate(q_ref[:, qc], cos_t, sin_t))
            ks = _rotate(k_ref[:, qc] * (RET_QK_DIM ** -0.5), cos_t, sin_t)
            vb = v_ref[:, vc]
            s_old = state[h]
            sb = _bf(s_old)
            st_ref[h] = sb
            inner = _dot_nt(qs, _bf(ks)) * dec_ref[h]
            out = _dot(_bf(inner), vb) + _dot(qs, sb) * xi_ref[h]
            state[h] = gch_ref[h, :, 0:1] * s_old + _dot_tn(_bf(ks * zeta_ref[h]), vb)
            o_ref[:, vc] = out
            r, rn = _rms_stats(out)
            g = g_ref[:, vc]
            ret_ref[:, vc] = _bf(g * jax.nn.sigmoid(g) * rn)

    qk, v, dec, col, gsp, st, pos = _ret_specs(lambda j: j)
    return pl.pallas_call(
        kern, grid=(nc,),
        in_specs=[qk, qk, v, v, pos, pos, dec, col, col, gsp],
        out_specs=[v, v, st],
        out_shape=[jax.ShapeDtypeStruct((t, RET_V), F32), jax.ShapeDtypeStruct((t, RET_V), BF16),
                   jax.ShapeDtypeStruct((RET_HEADS, nc, RET_QK_DIM, RET_V_DIM), BF16)],
        scratch_shapes=[pltpu.VMEM((RET_HEADS, RET_QK_DIM, RET_V_DIM), F32)],
        name="ret_fwd", compiler_params=_params(("arbitrary",)),
    )(q_r, k_r, v_r, g_r, cos, sin_s, decay, xi, zeta, gch)


def _ret_bwd(q_r, k_r, v_r, d_o, states, tables, exchange):
    t = q_r.shape[0]
    nc = t // RET_CHUNK
    cos, sin_s, decay, xi, zeta, gch = tables

    def kern(q_ref, k_ref, v_ref, do_ref, st_ref, cos_ref, sin_ref, dec_ref, xi_ref, zeta_ref, gch_ref,
             dq_ref, dk_ref, dv_ref, dstate):
        @pl.when(pl.program_id(0) == 0)
        def _():
            dstate[...] = jnp.zeros_like(dstate)

        cos_t = cos_ref[...]
        sin_t = sin_ref[...]
        scale = RET_QK_DIM ** -0.5
        for h in range(RET_HEADS):
            qc = slice(h * RET_QK_DIM, (h + 1) * RET_QK_DIM)
            vc = slice(h * RET_V_DIM, (h + 1) * RET_V_DIM)
            qs = _bf(_rotate(q_ref[:, qc], cos_t, sin_t))
            ks = _rotate(k_ref[:, qc] * scale, cos_t, sin_t)
            ksb = _bf(ks)
            vb = v_ref[:, vc]
            d_o_t = do_ref[:, vc]
            dob = _bf(d_o_t)
            doxb = _bf(d_o_t * xi_ref[h])
            dec = dec_ref[h]
            ds_old = dstate[h]
            dsb = _bf(ds_old)
            pb = _bf(_dot_nt(qs, ksb) * dec)
            dpb = _bf(_dot_nt(dob, vb) * dec)
            dqs = _dot(dpb, ksb) + _dot_nt(doxb, st_ref[h])
            dks = _dot_tn(dpb, qs) + _dot_nt(vb, dsb) * zeta_ref[h]
            dv_ref[:, vc] = _bf(_dot_tn(pb, dob) + _dot(_bf(ks * zeta_ref[h]), dsb))
            dstate[h] = gch_ref[h, :, 0:1] * ds_old + _dot_tn(qs, doxb)
            dq_ref[:, qc] = _bf(_rotate_bwd(dqs, cos_t, sin_t))
            dk_ref[:, qc] = _bf(_rotate_bwd(dks, cos_t, sin_t) * scale)

    qk, v, dec, col, gsp, st, pos = _ret_specs(lambda j: nc - 1 - j)
    return _pallas(
        kern, grid=(nc,),
        in_specs=[qk, qk, v, v, st, pos, pos, dec, col, col, gsp],
        out_specs=[qk, qk, v],
        out_shape=[jax.ShapeDtypeStruct((t, RET_QK), BF16), jax.ShapeDtypeStruct((t, RET_QK), BF16),
                   jax.ShapeDtypeStruct((t, RET_V), BF16)],
        scratch=[pltpu.VMEM((RET_HEADS, RET_QK_DIM, RET_V_DIM), F32)],
        args=[q_r, k_r, v_r, d_o, states, cos, sin_s, decay, xi, zeta, gch], name="ret_bwd", exchange=exchange)


def _position():
    return lax.axis_index("x"), lax.axis_index("y"), lax.axis_index("c")


def _gather_exchange(owns, forward_at):
    n = len(owns)

    def copies(ins, outs, send_sems, recv_sems, base):
        x, y, c = _position()
        sibling = (x, y, 1 - c)
        chips = [(1 - x, y), (x, 1 - y), (1 - x, 1 - y)]
        my_chip = 2 * x + y

        def slab(a, chip, hf):
            half = owns[a].shape[0] // 2
            return outs[a].at[chip, pl.ds(hf * half, half), :]

        def copy(k, src, dst, to):
            return pltpu.make_async_remote_copy(src_ref=src, dst_ref=dst, send_sem=send_sems.at[base + k],
                                                recv_sem=recv_sems.at[base + k], device_id=to, device_id_type=MESH)

        first, passed, from_sibling = [], [], []
        for a in range(n):
            half = owns[a].shape[0] // 2
            for k, (cx, cy) in enumerate(chips):
                first.append(copy(6 * a + k, ins[a].at[pl.ds(c * half, half), :], slab(a, my_chip, c), (cx, cy, c)))
                landed = slab(a, 2 * cx + cy, c)
                passed.append(copy(6 * a + 3 + k, landed, landed, sibling))
                theirs = slab(a, 2 * cx + cy, 1 - c)
                from_sibling.append(copy(6 * a + 3 + k, theirs, theirs, sibling))
        return first, passed, from_sibling

    def start(*args):
        first, _, _ = copies(*args)
        for cp in first:
            cp.start()

    def forward(*args):
        first, passed, _ = copies(*args)
        for arrived, cp in zip(first, passed):
            arrived.wait_recv()
            cp.start()

    def finish(*args):
        first, passed, from_sibling = copies(*args)
        for cp in from_sibling:
            cp.wait_recv()
        for cp in first + passed:
            cp.wait_send()

    outs = [jax.ShapeDtypeStruct((N_CHIPS, *a.shape), a.dtype) for a in owns]
    return _Exchange(owns, outs, 6 * n, [(0.0, start), (forward_at, forward), (1.0, finish)])


def _symmetric_exchange(ins, outs, plan):
    n_sems = len(plan([None] * len(ins), [None] * len(outs), 0, 0, 0, dry=True))

    def copies(in_refs, out_refs, send_sems, recv_sems, base):
        x, y, c = _position()
        return [pltpu.make_async_remote_copy(src_ref=src, dst_ref=dst, send_sem=send_sems.at[base + k],
                                             recv_sem=recv_sems.at[base + k], device_id=dev, device_id_type=MESH)
                for k, (src, dst, dev) in enumerate(plan(in_refs, out_refs, x, y, c, dry=False))]

    def start(*args):
        for cp in copies(*args):
            cp.start()

    def finish(*args):
        for cp in copies(*args):
            cp.wait()

    return _Exchange(ins, outs, n_sems, [(0.0, start), (1.0, finish)])


def _pair_exchange(gs):
    def plan(in_refs, out_refs, x, y, c, dry):
        out = []
        for a, g in enumerate(gs):
            half = g.shape[1] // 2
            for k in range(N_CHIPS):
                out.append(None if dry else (in_refs[a].at[k, pl.ds((1 - c) * half, half), :], out_refs[a].at[k],
                                             (x, y, 1 - c)))
        return out

    outs = [jax.ShapeDtypeStruct((g.shape[0], g.shape[1] // 2, g.shape[2]), g.dtype) for g in gs]
    return _symmetric_exchange(gs, outs, plan)


def _pair_sum(g, from_sibling, c_arr, *, tile, name):
    n, rows, width = g.shape
    tiles = (rows // 2) // tile

    def kern(c_ref, g_ref, s_ref, o_ref):
        o_ref[...] = _bf(g_ref[...] + s_ref[...])

    return pl.pallas_call(
        kern,
        grid_spec=pltpu.PrefetchScalarGridSpec(
            num_scalar_prefetch=1, grid=(n, tiles),
            in_specs=[pl.BlockSpec((None, tile, width), lambda k, i, c: (k, c[0] * tiles + i, 0)),
                      pl.BlockSpec((None, tile, width), lambda k, i, c: (k, i, 0))],
            out_specs=pl.BlockSpec((None, tile, width), lambda k, i, c: (k, i, 0))),
        out_shape=jax.ShapeDtypeStruct((n, rows // 2, width), BF16), name=name,
        compiler_params=_params(("parallel", "parallel")),
    )(c_arr, g, from_sibling)


def _scatter_to_owners(hsums):
    def plan(in_refs, out_refs, x, y, c, dry):
        out = []
        for a in range(len(hsums)):
            for k, (cx, cy) in enumerate([(1 - x, y), (x, 1 - y), (1 - x, 1 - y)]):
                out.append(None if dry else (in_refs[a].at[2 * cx + cy], out_refs[a].at[k], (cx, cy, c)))
        return out

    outs = [jax.ShapeDtypeStruct((3, *h.shape[1:]), h.dtype) for h in hsums]
    return _symmetric_exchange(hsums, outs, plan)


def _sum_chips(hsum, parts, chip_arr, *, tile, name):
    n, half, width = parts.shape

    def kern(chip_ref, h_ref, p_ref, o_ref):
        acc = h_ref[...].astype(F32)
        for k in range(n):
            acc = acc + p_ref[k].astype(F32)
        o_ref[...] = acc

    return pl.pallas_call(
        kern,
        grid_spec=pltpu.PrefetchScalarGridSpec(
            num_scalar_prefetch=1, grid=(half // tile,),
            in_specs=[pl.BlockSpec((None, tile, width), lambda i, chip: (chip[0], i, 0)),
                      pl.BlockSpec((n, tile, width), lambda i, chip: (0, i, 0))],
            out_specs=pl.BlockSpec((tile, width), lambda i, chip: (i, 0))),
        out_shape=jax.ShapeDtypeStruct((half, width), F32), name=name,
        compiler_params=_params(("parallel",)),
    )(chip_arr, hsum, parts)


def _share_halves(fhalves):
    def plan(in_refs, out_refs, x, y, c, dry):
        return [None if dry else (in_refs[a], out_refs[a], (x, y, 1 - c)) for a in range(len(fhalves))]

    return _symmetric_exchange(fhalves, [jax.ShapeDtypeStruct(f.shape, f.dtype) for f in fhalves], plan)


def _adamw_math(w, g, m, v):
    m = ADAM_B1 * m + (1.0 - ADAM_B1) * g
    v = ADAM_B2 * v + (1.0 - ADAM_B2) * (g * g)
    m_hat = m / (1.0 - ADAM_B1 ** ADAM_STEP)
    v_hat = v / (1.0 - ADAM_B2 ** ADAM_STEP)
    delta = -ADAM_LR * (m_hat / (jnp.sqrt(v_hat) + ADAM_EPS) + ADAM_WD * w)
    return delta, m, v


def _adamw(w, m, v, g_mine, g_other, c_arr, *, row_off, tile, name):
    rows, width = w.shape
    tiles_per_half = g_mine.shape[0] // tile
    first = row_off // tile

    def kern(c_ref, w_ref, gm_ref, go_ref, m_ref, v_ref, g_ref, d_ref, nm_ref, nv_ref):
        in_my_half = ((first + pl.program_id(0)) // tiles_per_half) == c_ref[0]
        g = jnp.where(in_my_half, gm_ref[...], go_ref[...])
        g_ref[...] = g
        d_ref[...], nm_ref[...], nv_ref[...] = _adamw_math(w_ref[...], g, m_ref[...], v_ref[...])

    full = pl.BlockSpec((tile, width), lambda i, c: (i, 0))
    half = pl.BlockSpec((tile, width), lambda i, c: ((first + i) % tiles_per_half, 0))
    return pl.pallas_call(
        kern,
        grid_spec=pltpu.PrefetchScalarGridSpec(
            num_scalar_prefetch=1, grid=(rows // tile,), in_specs=[full, half, half, full, full],
            out_specs=[full] * 4),
        out_shape=[jax.ShapeDtypeStruct((rows, width), F32)] * 4, name=name,
        compiler_params=_params(("parallel",)),
    )(c_arr, w, g_mine, g_other, m, v)


def _small_step(part, tile_sums, w, m, v):
    loss_p, dg1_p, dg2_p = tile_sums

    def body(part_ref, loss_ref, dg1_ref, dg2_ref, w_ref, m_ref, v_ref, g_ref, d_ref, nm_ref, nv_ref,
             mine, gathered, send_sems, recv_sems):
        x, y, c = _position()
        me = 4 * x + 2 * y + c
        mine[...] = part_ref[...]
        for r in range(8):
            lanes = slice(128 * r, 128 * (r + 1))
            mine[SM_G1 + r:SM_G1 + r + 1, :] = jnp.sum(dg1_ref[:, lanes], axis=0, keepdims=True)
            mine[SM_G2 + r:SM_G2 + r + 1, :] = jnp.sum(dg2_ref[:, lanes], axis=0, keepdims=True)
        mine[SM_LOSS:SM_LOSS + 1, :] = jnp.sum(loss_ref[...], axis=0, keepdims=True)
        copies = []
        for k in range(1, N_DEV):
            flip = (k >> 2) & 1, (k >> 1) & 1, k & 1
            to = (x ^ flip[0], y ^ flip[1], c ^ flip[2])
            cp = pltpu.make_async_remote_copy(
                src_ref=mine, dst_ref=gathered.at[me], send_sem=send_sems.at[k - 1], recv_sem=recv_sems.at[k - 1],
                device_id=to, device_id_type=MESH)
            cp.start()
            copies.append(cp)
        gathered[me] = mine[...]
        for k in range(1, N_DEV):
            flip = (k >> 2) & 1, (k >> 1) & 1, k & 1
            src = 4 * (x ^ flip[0]) + 2 * (y ^ flip[1]) + (c ^ flip[2])
            pltpu.make_async_remote_copy(
                src_ref=mine, dst_ref=gathered.at[src], send_sem=send_sems.at[k - 1], recv_sem=recv_sems.at[k - 1],
                device_id=(x, y, c), device_id_type=MESH).wait_recv()
        for cp in copies:
            cp.wait_send()
        total = gathered[0]
        for k in range(1, N_DEV):
            total = total + gathered[k]
        g_ref[...] = total
        d_ref[...], nm_ref[...], nv_ref[...] = _adamw_math(w_ref[...], total, m_ref[...], v_ref[...])

    vm = pl.BlockSpec(memory_space=pltpu.VMEM)
    blk = jax.ShapeDtypeStruct((SMALL_ROWS, 128), F32)
    return pl.pallas_call(
        body, in_specs=[vm] * 7, out_specs=[vm] * 4, out_shape=[blk] * 4,
        scratch_shapes=[pltpu.VMEM((SMALL_ROWS, 128), F32), pltpu.VMEM((N_DEV, SMALL_ROWS, 128), F32),
                        pltpu.SemaphoreType.DMA((N_DEV - 1,)), pltpu.SemaphoreType.DMA((N_DEV - 1,))],
        name="small_step",
    )(part, loss_p.reshape(-1, 128), dg1_p.reshape(-1, D_MODEL), dg2_p.reshape(-1, D_MODEL), w, m, v)


def _shards(own, gathered, my_chip):
    return [jnp.where(my_chip == k, own, gathered[k]) for k in range(N_CHIPS)]


def _by_owner_rows(blocks):
    return jnp.stack([jnp.concatenate([b[k * (b.shape[0] // N_CHIPS):(k + 1) * (b.shape[0] // N_CHIPS)]
                                       for b in blocks], axis=0) for k in range(N_CHIPS)], axis=0)


def _by_owner_cols(blocks):
    w = blocks[0].shape[1] // N_CHIPS
    return jnp.stack([jnp.concatenate([b[:, k * w:(k + 1) * w] for b in blocks], axis=0) for k in range(N_CHIPS)],
                     axis=0)


def _pack_small(g1, g2, gq, gk, sinks):
    blk = jnp.zeros((SMALL_ROWS, 128), F32)
    blk = blk.at[SM_G1:SM_G1 + 8].set(g1.reshape(8, 128))
    blk = blk.at[SM_G2:SM_G2 + 8].set(g2.reshape(8, 128))
    blk = blk.at[SM_GQ, :HEAD_DIM].set(gq.reshape(-1))
    blk = blk.at[SM_GK, :HEAD_DIM].set(gk.reshape(-1))
    blk = blk.at[SM_SINK, :N_Q_HEADS].set(sinks.reshape(-1))
    return blk


def _unpack_small(blk):
    return (blk[SM_G1:SM_G1 + 8].reshape(1, D_MODEL), blk[SM_G2:SM_G2 + 8].reshape(1, D_MODEL),
            blk[SM_GQ, :HEAD_DIM].reshape(1, HEAD_DIM), blk[SM_GK, :HEAD_DIM].reshape(1, HEAD_DIM),
            blk[SM_SINK, :N_Q_HEADS].reshape(1, N_Q_HEADS))


def kernel(x, norm_mix_gain, w_in, q_norm_gain, k_norm_gain, attn_sinks, w_branch_attn, w_branch_ret, w_out, norm_ffn_gain, w_ffn_gate, w_ffn_up, w_ffn_down, loss_target, m_norm_mix_gain, m_w_in, m_q_norm_gain, m_k_norm_gain, m_attn_sinks, m_w_branch_attn, m_w_branch_ret, m_w_out, m_norm_ffn_gain, m_w_ffn_gate, m_w_ffn_up, m_w_ffn_down, v_norm_mix_gain, v_w_in, v_q_norm_gain, v_k_norm_gain, v_attn_sinks, v_w_branch_attn, v_w_branch_ret, v_w_out, v_norm_ffn_gain, v_w_ffn_gate, v_w_ffn_up, v_w_ffn_down):
    my_chip = 2 * lax.axis_index("x") + lax.axis_index("y")
    c_arr = lax.axis_index("c").astype(jnp.int32).reshape(1)
    chip_arr = my_chip.astype(jnp.int32).reshape(1)
    x_t, target = x[0], loss_target[0]
    g1, g2, gq, gk, sinks = norm_mix_gain, norm_ffn_gain, q_norm_gain, k_norm_gain, attn_sinks
    tables = _ret_tables(x_t.shape[0])

    own_w_in = _bf(w_in[0])
    own_mix = _bf(jnp.concatenate([w_branch_attn[0], w_branch_ret[0], w_out[0]], axis=0))
    own_wd = _bf(w_ffn_down[0])
    own_ffn = _bf(jnp.concatenate([w_ffn_gate[0], w_ffn_up[0]], axis=0))
    got_w_in, = _run_exchange(_gather_exchange([own_w_in], 0.0), "gather_w_in")
    w_in_full = jnp.concatenate(_shards(own_w_in, got_w_in, my_chip), axis=1)
    (h1, q_a, kv_a, q_r, k_r, v_r, g_r, z_a, z_r, got_mix, got_wd, got_ffn) = _proj_fwd(
        x_t, g1, w_in_full, _gather_exchange([own_mix, own_wd, own_ffn], 0.8))
    mix = _shards(own_mix, got_mix, my_chip)
    wba = jnp.concatenate([s[0:256] for s in mix], axis=0)
    wbr = jnp.concatenate([s[256:768] for s in mix], axis=0)
    wout = jnp.concatenate([s[768:1024] for s in mix], axis=0)
    wd = jnp.concatenate(_shards(own_wd, got_wd, my_chip), axis=0)
    ffn = _shards(own_ffn, got_ffn, my_chip)
    wg = jnp.concatenate([s[:D_MODEL] for s in ffn], axis=1)
    wu = jnp.concatenate([s[D_MODEL:] for s in ffn], axis=1)

    attn = _attn_fwd(q_a, kv_a, gq, gk, sinks)
    o_ret, ret, states = _ret_fwd(q_r, k_r, v_r, g_r, tables)
    ba, br, merged, x1, h2 = _mix_fwd(attn, ret, z_a, z_r, x_t, wba, wbr, wout, g2)
    act, dgate, dup, dyb, dx1, dx1b, loss_p, dg2_p = _ffn_fwd_bwd(h2, x1, target, wg, wu, wd, g2)

    def reduce_tail(name, g_half, g_other, row_off, tile, wmv):
        w, m, v = wmv
        return _adamw(w[0], m[0], v[0], g_half, g_other, c_arr, row_off=row_off, tile=tile, name=f"adamw_{name}")

    dw_gate, = _matmul_tn(h2, dgate, tm=1024, tn=D_FF // 2, name="dw_gate")
    dw_up, = _matmul_tn(h2, dup, tm=1024, tn=D_FF // 2, name="dw_up")
    dw_down, = _matmul_tn(act, dyb, tm=D_FF // 2, tn=1024, name="dw_down")
    f_blocks = [_by_owner_cols([dw_gate, dw_up]), dw_down.reshape(N_CHIPS, FF_SH, D_MODEL)]
    (dba, dbr, dz_a, dz_r, d_attn, d_o, dg_r, sib_ffn, sib_wd) = _mix_bwd(
        dx1b, z_a, z_r, ba, br, g_r, o_ret, wout, wba, wbr, _pair_exchange(f_blocks))
    f_sums = [_pair_sum(f_blocks[0], sib_ffn, c_arr, tile=512, name="pair_sum_ffn"),
              _pair_sum(f_blocks[1], sib_wd, c_arr, tile=FF_SH // 2, name="pair_sum_wd")]

    dw_ba, = _matmul_tn(attn, dba, tm=1024, tn=1024, name="dw_ba")
    dw_br, = _matmul_tn(ret, dbr, tm=1024, tn=1024, name="dw_br")
    dw_out, = _matmul_tn(merged, dx1b, tm=1024, tn=1024, name="dw_out")
    m_block = _by_owner_rows([dw_ba, dw_br, dw_out])
    dw_in = [None] * 8
    dw_in[5], sib_mix = _matmul_tn(h1, dg_r, tm=D_MODEL, tn=1024, name="dw_in_5", exchange=_pair_exchange([m_block]))
    dw_in[6], = _matmul_tn(h1, dz_a, tm=D_MODEL, tn=1024, name="dw_in_6")
    dw_in[7], = _matmul_tn(h1, dz_r, tm=D_MODEL, tn=1024, name="dw_in_7")
    m_sum = _pair_sum(m_block, sib_mix, c_arr, tile=256, name="pair_sum_mix")

    dq_r, dk_r, dv_r, got_ffn_sums, got_wd_sums = _ret_bwd(q_r, k_r, v_r, d_o, states, tables, _scatter_to_owners(f_sums))
    ffn_half = _sum_chips(f_sums[0], got_ffn_sums, chip_arr, tile=512, name="sum_chips_ffn")
    wd_half = _sum_chips(f_sums[1], got_wd_sums, chip_arr, tile=FF_SH // 2, name="sum_chips_wd")
    dw_in[2], = _matmul_tn(h1, dq_r, tm=D_MODEL, tn=1024, name="dw_in_2")
    dw_in[3], = _matmul_tn(h1, dk_r, tm=D_MODEL, tn=1024, name="dw_in_3")
    dw_in[4], = _matmul_tn(h1, dv_r, tm=D_MODEL, tn=1024, name="dw_in_4")

    (dq_a, dkv_a, dgq, dgk, dsinks, got_mix_sums, ffn_other, wd_other) = _attn_bwd(
        q_a, kv_a, d_attn, gq, gk, sinks,
        _merge_exchanges(_scatter_to_owners([m_sum]), _share_halves([ffn_half, wd_half])))
    mix_half = _sum_chips(m_sum, got_mix_sums, chip_arr, tile=256, name="sum_chips_mix")
    dw_in[0], mix_other = _matmul_tn(h1, dq_a, tm=D_MODEL, tn=1024, name="dw_in_0", exchange=_share_halves([mix_half]))
    dw_in[1], = _matmul_tn(h1, dkv_a, tm=D_MODEL, tn=256, name="dw_in_1")

    w_block = _by_owner_cols([jnp.concatenate(dw_in, axis=1)])
    sib_w_in, = _run_exchange(_pair_exchange([w_block]), "pair_exchange_w_in")
    w_sum = _pair_sum(w_block, sib_w_in, c_arr, tile=256, name="pair_sum_w_in")
    d_pieces = [dq_a, dkv_a, dq_r, dk_r, dv_r, dg_r, dz_a, dz_r]
    grad_x, dg1_p, got_w_in_sums = _proj_bwd(d_pieces, x_t, dx1, w_in_full, g1, _scatter_to_owners([w_sum]))
    w_in_half = _sum_chips(w_sum, got_w_in_sums, chip_arr, tile=256, name="sum_chips_w_in")
    w_in_other, = _run_exchange(_share_halves([w_in_half]), "share_halves_w_in")

    big = dict(
        w_in=reduce_tail("w_in", w_in_half, w_in_other, 0, 256, (w_in, m_w_in, v_w_in)),
        wg=reduce_tail("wg", ffn_half, ffn_other, 0, 512, (w_ffn_gate, m_w_ffn_gate, v_w_ffn_gate)),
        wu=reduce_tail("wu", ffn_half, ffn_other, D_MODEL, 512, (w_ffn_up, m_w_ffn_up, v_w_ffn_up)),
        wd=reduce_tail("wd", wd_half, wd_other, 0, FF_SH // 2, (w_ffn_down, m_w_ffn_down, v_w_ffn_down)),
        wba=reduce_tail("wba", mix_half, mix_other, 0, 256, (w_branch_attn, m_w_branch_attn, v_w_branch_attn)),
        wbr=reduce_tail("wbr", mix_half, mix_other, 256, 256, (w_branch_ret, m_w_branch_ret, v_w_branch_ret)),
        wout=reduce_tail("wout", mix_half, mix_other, 768, 256, (w_out, m_w_out, v_w_out)))

    zeros = jnp.zeros((1, D_MODEL), F32)
    part = _pack_small(zeros, zeros, dgq, dgk, dsinks)
    sm_w = _pack_small(norm_mix_gain, norm_ffn_gain, q_norm_gain, k_norm_gain, attn_sinks)
    sm_m = _pack_small(m_norm_mix_gain, m_norm_ffn_gain, m_q_norm_gain, m_k_norm_gain, m_attn_sinks)
    sm_v = _pack_small(v_norm_mix_gain, v_norm_ffn_gain, v_q_norm_gain, v_k_norm_gain, v_attn_sinks)
    sm_g, sm_d, sm_nm, sm_nv = _small_step(part, (loss_p, dg1_p, dg2_p), sm_w, sm_m, sm_v)
    loss = sm_g[SM_LOSS, 0]

    def leaves(i, sm):
        b = [big[n][i][None] for n in ("w_in", "wba", "wbr", "wout", "wg", "wu", "wd")]
        s1, s2, sq, sk, ss = _unpack_small(sm)
        return [s1, b[0], sq, sk, ss, b[1], b[2], b[3], s2, b[4], b[5], b[6]]

    return (loss, grad_x[None], *leaves(0, sm_g), *leaves(1, sm_d), *leaves(2, sm_nm), *leaves(3, sm_nv))
```

```python
import jax
import jax.numpy as jnp
from jax import lax
from jax.experimental import pallas as pl
from jax.experimental.pallas import tpu as pltpu

F32 = jnp.float32
BF16 = jnp.bfloat16
MESH = pl.DeviceIdType.MESH

D_MODEL = 1024
EPS = 1e-6
HEAD_DIM = 64
N_Q_HEADS = 16
N_KV_HEADS = 2
GROUP = 8
BLOCK = 128
RET_HEADS = 4
RET_QK_DIM = 256
RET_V_DIM = 512
RET_CHUNK = 128
RET_ROT_BASE = 10000.0
D_FF = 2816
ATT_Q = N_Q_HEADS * HEAD_DIM
ATT_KV = N_KV_HEADS * HEAD_DIM
RET_QK = RET_HEADS * RET_QK_DIM
RET_V = RET_HEADS * RET_V_DIM
D_IN = 9472
ADAM_LR = 0.001
ADAM_B1 = 0.9
ADAM_B2 = 0.999
ADAM_EPS = 1e-08
ADAM_WD = 0.01
ADAM_STEP = 10

N_CHIPS = 4
N_DEV = 8
VMEM_LIMIT_BYTES = 60 * 1024 * 1024

P_QA = (0, 1024)
P_KVA = (1024, 256)
P_QR = (1280, 1024)
P_KR = (2304, 1024)
P_VR = (3328, 2048)
P_GR = (5376, 2048)
P_ZA = (7424, 1024)
P_ZR = (8448, 1024)

W_IN_SH = D_IN // N_CHIPS
FF_SH = D_FF // N_CHIPS

SMALL_ROWS = 24
SM_G1, SM_G2, SM_GQ, SM_GK, SM_SINK, SM_LOSS = 0, 8, 16, 17, 18, 19


def _dot(a, b):
    return jnp.dot(a, b, preferred_element_type=F32)


def _dot_nt(a, b):
    return lax.dot_general(a, b, (((1,), (1,)), ((), ())), preferred_element_type=F32)


def _dot_tn(a, b):
    return lax.dot_general(a, b, (((0,), (0,)), ((), ())), preferred_element_type=F32)


def _bf(x):
    return x.astype(BF16)


def _rms_stats(x):
    r = lax.rsqrt(jnp.mean(x * x, axis=-1, keepdims=True) + EPS)
    return r, x * r


def _rms_bwd(dy, xhat, r, gain):
    u = dy * gain
    dx = r * (u - xhat * jnp.mean(u * xhat, axis=-1, keepdims=True))
    return dx, dy * xhat


def _params(sem):
    return pltpu.CompilerParams(dimension_semantics=sem, vmem_limit_bytes=VMEM_LIMIT_BYTES)


_ANY = pl.BlockSpec(memory_space=pl.ANY)


class _Exchange:
    def __init__(self, ins, outs, n_sems, phases):
        self.ins, self.outs, self.n_sems, self.phases = list(ins), list(outs), n_sems, list(phases)


def _merge_exchanges(a, b):
    na_i, na_o, shift = len(a.ins), len(a.outs), a.n_sems

    def first(fn):
        return lambda i, o, s, r, base: fn(i[:na_i], o[:na_o], s, r, base)

    def second(fn):
        return lambda i, o, s, r, base: fn(i[na_i:], o[na_o:], s, r, base + shift)

    phases = [(f, first(fn)) for f, fn in a.phases] + [(f, second(fn)) for f, fn in b.phases]
    return _Exchange(a.ins + b.ins, a.outs + b.outs, a.n_sems + b.n_sems, sorted(phases, key=lambda p: p[0]))


def _pallas(kern, *, grid, in_specs, out_specs, out_shape, args, name, scratch=(), exchange=None):
    if exchange is None:
        return pl.pallas_call(
            kern, grid=grid, in_specs=in_specs, out_specs=out_specs, out_shape=out_shape, name=name,
            scratch_shapes=list(scratch), compiler_params=_params(("arbitrary",) * len(grid)))(*args)
    n_in, n_out, n_sc = len(in_specs), len(out_specs), len(scratch)
    n_xi, n_xo = len(exchange.ins), len(exchange.outs)
    n_steps = 1
    for g in grid:
        n_steps *= g

    def wrapped(*refs):
        ins, refs = refs[:n_in], refs[n_in:]
        x_ins, refs = refs[:n_xi], refs[n_xi:]
        outs, refs = refs[:n_out], refs[n_out:]
        x_outs, refs = refs[:n_xo], refs[n_xo:]
        scr, (send_sems, recv_sems) = refs[:n_sc], refs[n_sc:]
        step = pl.program_id(0)
        for d in range(1, len(grid)):
            step = step * grid[d] + pl.program_id(d)
        for frac, fn in exchange.phases:
            at = min(int(frac * n_steps), n_steps - 1)

            @pl.when(step == at)
            def _(fn=fn):
                fn(x_ins, x_outs, send_sems, recv_sems, 0)

        kern(*ins, *outs, *scr)

    sems = [pltpu.SemaphoreType.DMA((exchange.n_sems,)), pltpu.SemaphoreType.DMA((exchange.n_sems,))]
    return pl.pallas_call(
        wrapped, grid=grid, in_specs=list(in_specs) + [_ANY] * n_xi, out_specs=list(out_specs) + [_ANY] * n_xo,
        out_shape=list(out_shape) + exchange.outs, name=name, scratch_shapes=list(scratch) + sems,
        compiler_params=_params(("arbitrary",) * len(grid)))(*args, *exchange.ins)


def _run_exchange(exchange, name):
    def body(*refs):
        n_i, n_o = len(exchange.ins), len(exchange.outs)
        for _, fn in exchange.phases:
            fn(refs[:n_i], refs[n_i:n_i + n_o], refs[n_i + n_o], refs[n_i + n_o + 1], 0)

    sems = [pltpu.SemaphoreType.DMA((exchange.n_sems,)), pltpu.SemaphoreType.DMA((exchange.n_sems,))]
    return pl.pallas_call(body, in_specs=[_ANY] * len(exchange.ins), out_specs=[_ANY] * len(exchange.outs),
                          out_shape=exchange.outs, scratch_shapes=sems, name=name)(*exchange.ins)


def _row_call(body, *, tm, row_ins, res_ins, row_outs, part_outs=(), name, exchange=None):
    t = row_ins[0].shape[0]
    n_tiles = t // tm
    in_specs = [pl.BlockSpec((tm, a.shape[1]), lambda i: (i, 0)) for a in row_ins]
    in_specs += [pl.BlockSpec(a.shape, lambda i: (0, 0), pipeline_mode=pl.Buffered(1)) for a in res_ins]
    out_shape = [jax.ShapeDtypeStruct((t, w), dt) for (w, dt) in row_outs]
    out_shape += [jax.ShapeDtypeStruct((n_tiles, 1, w), F32) for w in part_outs]
    out_specs = [pl.BlockSpec((tm, w), lambda i: (i, 0)) for (w, _) in row_outs]
    out_specs += [pl.BlockSpec((1, 1, w), lambda i: (i, 0, 0)) for w in part_outs]
    n_ri, n_re, n_ro = len(row_ins), len(res_ins), len(row_outs)

    def kern(*refs):
        body(refs[:n_ri], refs[n_ri:n_ri + n_re], refs[n_ri + n_re:n_ri + n_re + n_ro], refs[n_ri + n_re + n_ro:])

    return _pallas(kern, grid=(n_tiles,), in_specs=in_specs, out_specs=out_specs, out_shape=out_shape,
                   args=[*row_ins, *res_ins], name=name, exchange=exchange)


def _proj_fwd(x, g1, w_in, exchange):
    pieces = ((P_QA, F32), (P_KVA, F32), (P_QR, F32), (P_KR, F32), (P_VR, BF16), (P_GR, F32), (P_ZA, F32), (P_ZR, F32))

    def body(ri, re, ro, po):
        x_t = ri[0][...]
        r, xhat = _rms_stats(x_t)
        hb = _bf(xhat * re[0][...])
        ro[0][...] = hb
        for k, ((off, w), dt) in enumerate(pieces):
            ro[1 + k][...] = _dot(hb, re[1][:, off:off + w]).astype(dt)

    outs = [(D_MODEL, BF16)] + [(w, dt) for ((_, w), dt) in pieces]
    return _row_call(body, tm=256, row_ins=[x], res_ins=[g1, w_in], row_outs=outs, name="proj_fwd", exchange=exchange)


def _mix_fwd(attn, ret, z_a, z_r, x, wba, wbr, wout, g2):
    def body(ri, re, ro, po):
        ba = _dot(ri[0][...], re[0][...])
        br = _dot(ri[1][...], re[1][...])
        m = jax.nn.sigmoid(ri[2][...]) * ba + jax.nn.sigmoid(ri[3][...]) * br
        mb = _bf(m)
        x1 = ri[4][...] + _dot(mb, re[2][...])
        r, xhat = _rms_stats(x1)
        ro[0][...] = ba
        ro[1][...] = br
        ro[2][...] = mb
        ro[3][...] = x1
        ro[4][...] = _bf(xhat * re[3][...])

    outs = [(D_MODEL, F32), (D_MODEL, F32), (D_MODEL, BF16), (D_MODEL, F32), (D_MODEL, BF16)]
    return _row_call(body, tm=256, row_ins=[attn, ret, z_a, z_r, x], res_ins=[wba, wbr, wout, g2], row_outs=outs,
                     name="mix_fwd")


def _ffn_fwd_bwd(h2, x1, target, wg, wu, wd, g2):
    def body(ri, re, ro, po):
        h2_t = ri[0][...]
        x1_t = ri[1][...]
        gate = _dot(h2_t, re[0][...])
        up = _dot(h2_t, re[1][...])
        sg = jax.nn.sigmoid(gate)
        sl = gate * sg
        actb = _bf(sl * up)
        ro[0][...] = actb
        y = x1_t + _dot(actb, re[2][...])
        e = y - ri[2][...]
        po[0][0] = jnp.broadcast_to(0.5 * jnp.sum(jnp.sum(e * e, axis=1, keepdims=True), axis=0, keepdims=True)
                                    * (1.0 / D_MODEL), (1, 128))
        dy = e * (1.0 / D_MODEL)
        dyb = _bf(dy)
        ro[3][...] = dyb
        dact = _dot_nt(dyb, re[2][...])
        dupb = _bf(dact * sl)
        dgateb = _bf(dact * up * (sg * (1.0 + gate * (1.0 - sg))))
        ro[1][...] = dgateb
        ro[2][...] = dupb
        dh2 = _dot_nt(dgateb, re[0][...]) + _dot_nt(dupb, re[1][...])
        r, xhat = _rms_stats(x1_t)
        dxn, dgain = _rms_bwd(dh2, xhat, r, re[3][...])
        dx1 = dy + dxn
        ro[4][...] = dx1
        ro[5][...] = _bf(dx1)
        po[1][0] = jnp.sum(dgain, axis=0, keepdims=True)

    outs = [(D_FF, BF16), (D_FF, BF16), (D_FF, BF16), (D_MODEL, BF16), (D_MODEL, F32), (D_MODEL, BF16)]
    return _row_call(body, tm=256, row_ins=[h2, x1, target], res_ins=[wg, wu, wd, g2], row_outs=outs,
                     part_outs=(128, D_MODEL), name="ffn_fwd_bwd")


def _mix_bwd(dx1b, z_a, z_r, ba, br, g_r, o_ret, wout, wba, wbr, exchange):
    def body(ri, re, ro, po):
        dm = _dot_nt(ri[0][...], re[0][...])
        sa = jax.nn.sigmoid(ri[1][...])
        sr = jax.nn.sigmoid(ri[2][...])
        dbab = _bf(sa * dm)
        dbrb = _bf(sr * dm)
        ro[0][...] = dbab
        ro[1][...] = dbrb
        ro[2][...] = _bf(dm * ri[3][...] * (sa * (1.0 - sa)))
        ro[3][...] = _bf(dm * ri[4][...] * (sr * (1.0 - sr)))
        ro[4][...] = _bf(_dot_nt(dbab, re[1][...]))
        dret = _dot_nt(dbrb, re[2][...])
        for h in range(RET_HEADS):
            cols = slice(h * RET_V_DIM, (h + 1) * RET_V_DIM)
            g = ri[5][:, cols]
            r, rn = _rms_stats(ri[6][:, cols])
            sg = jax.nn.sigmoid(g)
            dret_h = dret[:, cols]
            d_rn = dret_h * (g * sg)
            ro[6][:, cols] = _bf(dret_h * rn * (sg * (1.0 + g * (1.0 - sg))))
            ro[5][:, cols] = r * (d_rn - rn * jnp.mean(d_rn * rn, axis=-1, keepdims=True))

    outs = [(D_MODEL, BF16), (D_MODEL, BF16), (D_MODEL, BF16), (D_MODEL, BF16), (ATT_Q, BF16), (RET_V, F32),
            (RET_V, BF16)]
    return _row_call(body, tm=256, row_ins=[dx1b, z_a, z_r, ba, br, g_r, o_ret], res_ins=[wout, wba, wbr],
                     row_outs=outs, name="mix_bwd", exchange=exchange)


def _proj_bwd(d_pieces, x, dx1, w_in, g1, exchange):
    groups = (P_QA, P_KVA, P_QR, P_KR, P_VR, P_GR, P_ZA, P_ZR)
    n_p = len(groups)

    def body(ri, re, ro, po):
        dh = None
        for k, (off, w) in enumerate(groups):
            term = _dot_nt(ri[k][...], re[0][:, off:off + w])
            dh = term if dh is None else dh + term
        r, xhat = _rms_stats(ri[n_p][...])
        dxn, dgain = _rms_bwd(dh, xhat, r, re[1][...])
        ro[0][...] = ri[n_p + 1][...] + dxn
        po[0][0] = jnp.sum(dgain, axis=0, keepdims=True)

    return _row_call(body, tm=256, row_ins=[*d_pieces, x, dx1], res_ins=[w_in, g1], row_outs=[(D_MODEL, F32)],
                     part_outs=(D_MODEL,), name="proj_bwd", exchange=exchange)


def _matmul_tn(a, b, *, tm, tn, name, exchange=None):
    t, m = a.shape
    n = b.shape[1]
    tk = min(2048, t)

    def kern(a_ref, b_ref, o_ref):
        k = pl.program_id(2)

        @pl.when(k == 0)
        def _():
            o_ref[...] = jnp.zeros_like(o_ref)

        o_ref[...] += _dot_tn(a_ref[...], b_ref[...])

    return _pallas(
        kern, grid=(m // tm, n // tn, t // tk),
        in_specs=[pl.BlockSpec((tk, tm), lambda i, j, k: (k, i)), pl.BlockSpec((tk, tn), lambda i, j, k: (k, j))],
        out_specs=[pl.BlockSpec((tm, tn), lambda i, j, k: (i, j))],
        out_shape=[jax.ShapeDtypeStruct((m, n), F32)], args=[a, b], name=name, exchange=exchange)


def _heads_to_lanes(x3):
    return jnp.concatenate([x3[g] for g in range(GROUP)], axis=1)


def _lanes_to_heads(xt):
    return jnp.concatenate([xt[:, g * BLOCK:(g + 1) * BLOCK] for g in range(GROUP)], axis=0)


def _attn_group(n, kvh, q_ref, kvp_ref, kvc_ref, gq_col, gk, sink_ref):
    heads = [kvh * GROUP + g for g in range(GROUP)]
    cols = slice(kvh * GROUP * HEAD_DIM, (kvh + 1) * GROUP * HEAD_DIM)
    q3 = q_ref[:, cols].T.reshape(GROUP, HEAD_DIM, BLOCK)
    rq = lax.rsqrt(jnp.mean(q3 * q3, axis=1, keepdims=True) + EPS)
    qhat = q3 * rq
    qts = _heads_to_lanes(_bf(qhat * (gq_col * (HEAD_DIM ** -0.5))))
    kcols = slice(kvh * HEAD_DIM, (kvh + 1) * HEAD_DIM)
    vcols = slice(ATT_KV + kvh * HEAD_DIM, ATT_KV + (kvh + 1) * HEAD_DIM)
    k = jnp.concatenate([kvp_ref[:, kcols], kvc_ref[:, kcols]], axis=0)
    rk, khat = _rms_stats(k)
    knb = _bf(khat * gk)
    st = _dot(knb, qts)
    j = lax.broadcasted_iota(jnp.int32, (BLOCK, GROUP * BLOCK), 0)
    i = lax.broadcasted_iota(jnp.int32, (BLOCK, GROUP * BLOCK), 1) & (BLOCK - 1)
    from_prev = j > i
    f = jnp.where(from_prev, jnp.where(n > 0, st[0:BLOCK], -1e30), st[BLOCK:2 * BLOCK])
    sink = jnp.concatenate([jnp.broadcast_to(sink_ref[0:1, h:h + 1], (1, BLOCK)) for h in heads], axis=1)
    m = jnp.maximum(jnp.max(f, axis=0, keepdims=True), sink)
    e = jnp.exp(f - m)
    es = jnp.exp(sink - m)
    inv = 1.0 / (jnp.sum(e, axis=0, keepdims=True) + es)
    return dict(heads=heads, qhat=qhat, rq=rq, qts=qts, khat=khat, rk=rk, knb=knb, from_prev=from_prev,
                pf=e * inv, psink=es * inv)


def _unfold(from_prev, xf):
    return _bf(jnp.concatenate([jnp.where(from_prev, xf, 0.0), jnp.where(from_prev, 0.0, xf)], axis=0))


def _attn_fwd(q_a, kv_a, gq_col, gk, sinks):
    t = q_a.shape[0]
    nb = t // BLOCK

    def kern(q_ref, kvp_ref, kvc_ref, gq_ref, gk_ref, sink_ref, o_ref):
        n = pl.program_id(0)
        kvt = jnp.concatenate([kvp_ref[...].T, kvc_ref[...].T], axis=1)
        for kvh in range(N_KV_HEADS):
            a = _attn_group(n, kvh, q_ref, kvp_ref, kvc_ref, gq_ref[...], gk_ref[...], sink_ref)
            vt = _bf(kvt[ATT_KV + kvh * HEAD_DIM:ATT_KV + (kvh + 1) * HEAD_DIM, :])
            out_t = _dot(vt, _unfold(a["from_prev"], a["pf"]))
            cols = slice(kvh * GROUP * HEAD_DIM, (kvh + 1) * GROUP * HEAD_DIM)
            o_ref[:, cols] = _bf(_lanes_to_heads(out_t).T)

    small = lambda a: pl.BlockSpec(a.shape, lambda n: (0, 0))
    return pl.pallas_call(
        kern, grid=(nb,),
        in_specs=[pl.BlockSpec((BLOCK, ATT_Q), lambda n: (n, 0)),
                  pl.BlockSpec((BLOCK, 2 * ATT_KV), lambda n: (jnp.maximum(n - 1, 0), 0)),
                  pl.BlockSpec((BLOCK, 2 * ATT_KV), lambda n: (n, 0)),
                  small(gq_col), small(gk), small(sinks)],
        out_specs=pl.BlockSpec((BLOCK, ATT_Q), lambda n: (n, 0)),
        out_shape=jax.ShapeDtypeStruct((t, ATT_Q), BF16), name="attn_fwd",
        compiler_params=_params(("parallel",)),
    )(q_a, kv_a, kv_a, gq_col, gk, sinks)


def _attn_bwd(q_a, kv_a, d_attn, gq_col, gk, gk_col, sinks, exchange):
    t = q_a.shape[0]
    nb = t // BLOCK

    def kern(q_ref, kvp_ref, kvc_ref, do_ref, gq_ref, gk_ref, gkc_ref, sink_ref,
             dq_ref, dkv_ref, dgq_ref, dgk_ref, dsink_ref, band_k, band_v, carry_k, carry_v):
        n = pl.program_id(0)
        gq_v = gq_ref[...]
        gk_v = gk_ref[...]

        @pl.when(n == 0)
        def _():
            carry_k[...] = jnp.zeros_like(carry_k)
            carry_v[...] = jnp.zeros_like(carry_v)
            dgq_ref[...] = jnp.zeros_like(dgq_ref)
            dgk_ref[...] = jnp.zeros_like(dgk_ref)
            dsink_ref[...] = jnp.zeros_like(dsink_ref)

        @pl.when(n == nb)
        def _():
            band_k[...] = jnp.zeros_like(band_k)
            band_v[...] = jnp.zeros_like(band_v)

        @pl.when(n < nb)
        def _():
            lane16 = lax.broadcasted_iota(jnp.int32, (1, N_Q_HEADS), 1)
            dsink = jnp.zeros((1, N_Q_HEADS), F32)
            dgq = jnp.zeros((HEAD_DIM, 1), F32)
            gk_col = gkc_ref[...]
            kvt = jnp.concatenate([kvp_ref[...].T, kvc_ref[...].T], axis=1)
            for kvh in range(N_KV_HEADS):
                a = _attn_group(n, kvh, q_ref, kvp_ref, kvc_ref, gq_v, gk_v, sink_ref)
                from_prev, pf, qhat = a["from_prev"], a["pf"], a["qhat"]
                cols = slice(kvh * GROUP * HEAD_DIM, (kvh + 1) * GROUP * HEAD_DIM)
                vcols = slice(ATT_KV + kvh * HEAD_DIM, ATT_KV + (kvh + 1) * HEAD_DIM)
                dot = _heads_to_lanes(_bf(do_ref[:, cols].astype(F32).T.reshape(GROUP, HEAD_DIM, BLOCK)))
                vb = _bf(jnp.concatenate([kvp_ref[:, vcols], kvc_ref[:, vcols]], axis=0))
                dpt = _dot(vb, dot)
                dpf = jnp.where(from_prev, dpt[0:BLOCK], dpt[BLOCK:2 * BLOCK])
                delta = jnp.sum(pf * dpf, axis=0, keepdims=True)
                dst = _unfold(from_prev, pf * (dpf - delta))
                dsk = a["psink"] * delta
                for g, h in enumerate(a["heads"]):
                    tot = jnp.sum(dsk[:, g * BLOCK:(g + 1) * BLOCK], axis=1, keepdims=True)
                    dsink = dsink - jnp.where(lane16 == h, tot, 0.0)
                kt = kvt[kvh * HEAD_DIM:(kvh + 1) * HEAD_DIM, :]
                knt = _bf(kt * lax.rsqrt(jnp.mean(kt * kt, axis=0, keepdims=True) + EPS) * gk_col)
                dqn = (_dot(knt, dst) * (HEAD_DIM ** -0.5))
                band_k[kvh] = _dot_nt(dst, a["qts"])
                band_v[kvh] = _dot_nt(_unfold(from_prev, pf), dot)
                dqn3 = _lanes_to_heads(dqn).reshape(GROUP, HEAD_DIM, BLOCK)
                u = dqn3 * gq_v
                dq3 = a["rq"] * (u - qhat * jnp.mean(u * qhat, axis=1, keepdims=True))
                dgq = dgq + jnp.sum(jnp.sum(dqn3 * qhat, axis=0), axis=1, keepdims=True)
                dq_ref[:, cols] = _bf(dq3.reshape(GROUP * HEAD_DIM, BLOCK).T)
            dsink_ref[...] += dsink
            dgq_ref[...] += dgq

        dgk = jnp.zeros((1, HEAD_DIM), F32)
        for kvh in range(N_KV_HEADS):
            kcols = slice(kvh * HEAD_DIM, (kvh + 1) * HEAD_DIM)
            vcols = slice(ATT_KV + kvh * HEAD_DIM, ATT_KV + (kvh + 1) * HEAD_DIM)
            dkn = carry_k[kvh] + band_k[kvh, 0:BLOCK, :]
            dv = carry_v[kvh] + band_v[kvh, 0:BLOCK, :]
            rk, khat = _rms_stats(kvp_ref[:, kcols])
            dk, dgain = _rms_bwd(dkn, khat, rk, gk_v)
            dgk = dgk + jnp.sum(dgain, axis=0, keepdims=True)
            dkv_ref[:, kcols] = _bf(dk)
            dkv_ref[:, vcols] = _bf(dv)
            carry_k[kvh] = band_k[kvh, BLOCK:2 * BLOCK, :]
            carry_v[kvh] = band_v[kvh, BLOCK:2 * BLOCK, :]
        dgk_ref[...] += dgk

    small = lambda a: pl.BlockSpec(a.shape, lambda n: (0, 0))
    last = nb - 1
    return _pallas(
        kern, grid=(nb + 1,),
        in_specs=[pl.BlockSpec((BLOCK, ATT_Q), lambda n: (jnp.minimum(n, last), 0)),
                  pl.BlockSpec((BLOCK, 2 * ATT_KV), lambda n: (jnp.maximum(n - 1, 0), 0)),
                  pl.BlockSpec((BLOCK, 2 * ATT_KV), lambda n: (jnp.minimum(n, last), 0)),
                  pl.BlockSpec((BLOCK, ATT_Q), lambda n: (jnp.minimum(n, last), 0)),
                  small(gq_col), small(gk), small(gk_col), small(sinks)],
        out_specs=[pl.BlockSpec((BLOCK, ATT_Q), lambda n: (jnp.minimum(n, last), 0)),
                   pl.BlockSpec((BLOCK, 2 * ATT_KV), lambda n: (jnp.maximum(n - 1, 0), 0)),
                   pl.BlockSpec((HEAD_DIM, 1), lambda n: (0, 0)),
                   pl.BlockSpec((1, HEAD_DIM), lambda n: (0, 0)),
                   pl.BlockSpec((1, N_Q_HEADS), lambda n: (0, 0))],
        out_shape=[jax.ShapeDtypeStruct((t, ATT_Q), BF16), jax.ShapeDtypeStruct((t, 2 * ATT_KV), BF16),
                   jax.ShapeDtypeStruct((HEAD_DIM, 1), F32), jax.ShapeDtypeStruct((1, HEAD_DIM), F32),
                   jax.ShapeDtypeStruct((1, N_Q_HEADS), F32)],
        scratch=[pltpu.VMEM((N_KV_HEADS, 2 * BLOCK, HEAD_DIM), F32),
                 pltpu.VMEM((N_KV_HEADS, 2 * BLOCK, HEAD_DIM), F32),
                 pltpu.VMEM((N_KV_HEADS, BLOCK, HEAD_DIM), F32),
                 pltpu.VMEM((N_KV_HEADS, BLOCK, HEAD_DIM), F32)],
        args=[q_a, kv_a, kv_a, d_attn, gq_col, gk, gk_col, sinks], name="attn_bwd", exchange=exchange)


def _ret_tables(t):
    pos = jnp.arange(t, dtype=F32)
    theta = 1.0 / (RET_ROT_BASE ** jnp.linspace(0.0, 1.0, RET_QK_DIM // 2, dtype=F32))
    ang = pos[:, None] * theta[None, :]
    sign = jnp.tile(jnp.array([-1.0, 1.0], F32), RET_QK_DIM // 2)
    pair = (jnp.arange(RET_QK_DIM)[None, :] // 2 == jnp.arange(RET_QK_DIM // 2)[:, None]).astype(F32)
    spread = lambda a, mat: jnp.dot(a, mat, precision=lax.Precision.HIGHEST)
    log_gamma = jnp.log(1.0 - 2.0 ** (-5.0 - jnp.arange(RET_HEADS, dtype=F32)))
    i = jnp.arange(RET_CHUNK, dtype=F32)
    diff = i[:, None] - i[None, :]
    causal = diff >= 0
    decay = jnp.where(causal[None], jnp.exp(jnp.where(causal, diff, 0.0)[None] * log_gamma[:, None, None]), 0.0)
    xi = jnp.exp((i + 1.0)[None, :] * log_gamma[:, None])[:, :, None]
    zeta = jnp.exp((RET_CHUNK - 1.0 - i)[None, :] * log_gamma[:, None])[:, :, None]
    gch = jnp.broadcast_to(jnp.exp(RET_CHUNK * log_gamma)[:, None, None], (RET_HEADS, 1, 128))
    return spread(jnp.cos(ang), pair), spread(jnp.sin(ang), pair * sign[None, :]), decay, xi, zeta, gch


def _swap_pairs(x):
    lane = lax.broadcasted_iota(jnp.int32, x.shape, 1)
    return jnp.where((lane & 1) == 0, pltpu.roll(x, RET_QK_DIM - 1, 1), pltpu.roll(x, 1, 1))


def _rotate(x, cos, sin_s):
    return x * cos + _swap_pairs(x) * sin_s


def _rotate_bwd(dy, cos, sin_s):
    return dy * cos + _swap_pairs(dy * sin_s)


def _ret_specs(order):
    qk = pl.BlockSpec((RET_CHUNK, RET_QK), lambda j: (order(j), 0))
    v = pl.BlockSpec((RET_CHUNK, RET_V), lambda j: (order(j), 0))
    dec = pl.BlockSpec((RET_HEADS, RET_CHUNK, RET_CHUNK), lambda j: (0, 0, 0))
    col = pl.BlockSpec((RET_HEADS, RET_CHUNK, 1), lambda j: (0, 0, 0))
    gch = pl.BlockSpec((RET_HEADS, 1, 128), lambda j: (0, 0, 0))
    st = pl.BlockSpec((RET_HEADS, None, RET_QK_DIM, RET_V_DIM), lambda j: (0, order(j), 0, 0))
    pos = pl.BlockSpec((RET_CHUNK, RET_QK_DIM), lambda j: (order(j), 0))
    return qk, v, dec, col, gch, st, pos


def _ret_fwd(q_r, k_r, v_r, g_r, tables):
    t = q_r.shape[0]
    nc = t // RET_CHUNK
    cos, sin_s, decay, xi, zeta, gch = tables

    def kern(q_ref, k_ref, v_ref, g_ref, cos_ref, sin_ref, dec_ref, xi_ref, zeta_ref, gch_ref,
             o_ref, ret_ref, st_ref, state):
        @pl.when(pl.program_id(0) == 0)
        def _():
            state[...] = jnp.zeros_like(state)

        cos_t = cos_ref[...]
        sin_t = sin_ref[...]
        for h in range(RET_HEADS):
            qc = slice(h * RET_QK_DIM, (h + 1) * RET_QK_DIM)
            vc = slice(h * RET_V_DIM, (h + 1) * RET_V_DIM)
            qs = _bf(_rotate(q_ref[:, qc], cos_t, sin_t))
            ks = _rotate(k_ref[:, qc] * (RET_QK_DIM ** -0.5), cos_t, sin_t)
            vb = v_ref[:, vc]
            s_old = state[h]
            sb = _bf(s_old)
            st_ref[h] = sb
            inner = _dot_nt(qs, _bf(ks)) * dec_ref[h]
            out = _dot(_bf(inner), vb) + _dot(qs, sb) * xi_ref[h]
            state[h] = gch_ref[h, :, 0:1] * s_old + _dot_tn(_bf(ks * zeta_ref[h]), vb)
            o_ref[:, vc] = out
            r, rn = _rms_stats(out)
            g = g_ref[:, vc]
            ret_ref[:, vc] = _bf(g * jax.nn.sigmoid(g) * rn)

    qk, v, dec, col, gsp, st, pos = _ret_specs(lambda j: j)
    return pl.pallas_call(
        kern, grid=(nc,),
        in_specs=[qk, qk, v, v, pos, pos, dec, col, col, gsp],
        out_specs=[v, v, st],
        out_shape=[jax.ShapeDtypeStruct((t, RET_V), F32), jax.ShapeDtypeStruct((t, RET_V), BF16),
                   jax.ShapeDtypeStruct((RET_HEADS, nc, RET_QK_DIM, RET_V_DIM), BF16)],
        scratch_shapes=[pltpu.VMEM((RET_HEADS, RET_QK_DIM, RET_V_DIM), F32)],
        name="ret_fwd", compiler_params=_params(("arbitrary",)),
    )(q_r, k_r, v_r, g_r, cos, sin_s, decay, xi, zeta, gch)


def _ret_bwd(q_r, k_r, v_r, d_o, states, tables, exchange):
    t = q_r.shape[0]
    nc = t // RET_CHUNK
    cos, sin_s, decay, xi, zeta, gch = tables

    def kern(q_ref, k_ref, v_ref, do_ref, st_ref, cos_ref, sin_ref, dec_ref, xi_ref, zeta_ref, gch_ref,
             dq_ref, dk_ref, dv_ref, dstate):
        @pl.when(pl.program_id(0) == 0)
        def _():
            dstate[...] = jnp.zeros_like(dstate)

        cos_t = cos_ref[...]
        sin_t = sin_ref[...]
        scale = RET_QK_DIM ** -0.5
        for h in range(RET_HEADS):
            qc = slice(h * RET_QK_DIM, (h + 1) * RET_QK_DIM)
            vc = slice(h * RET_V_DIM, (h + 1) * RET_V_DIM)
            qs = _bf(_rotate(q_ref[:, qc], cos_t, sin_t))
            ks = _rotate(k_ref[:, qc] * scale, cos_t, sin_t)
            ksb = _bf(ks)
            vb = v_ref[:, vc]
            d_o_t = do_ref[:, vc]
            dob = _bf(d_o_t)
            doxb = _bf(d_o_t * xi_ref[h])
            dec = dec_ref[h]
            ds_old = dstate[h]
            dsb = _bf(ds_old)
            pb = _bf(_dot_nt(qs, ksb) * dec)
            dpb = _bf(_dot_nt(dob, vb) * dec)
            dqs = _dot(dpb, ksb) + _dot_nt(doxb, st_ref[h])
            dks = _dot_tn(dpb, qs) + _dot_nt(vb, dsb) * zeta_ref[h]
            dv_ref[:, vc] = _bf(_dot_tn(pb, dob) + _dot(_bf(ks * zeta_ref[h]), dsb))
            dstate[h] = gch_ref[h, :, 0:1] * ds_old + _dot_tn(qs, doxb)
            dq_ref[:, qc] = _bf(_rotate_bwd(dqs, cos_t, sin_t))
            dk_ref[:, qc] = _bf(_rotate_bwd(dks, cos_t, sin_t) * scale)

    qk, v, dec, col, gsp, st, pos = _ret_specs(lambda j: nc - 1 - j)
    return _pallas(
        kern, grid=(nc,),
        in_specs=[qk, qk, v, v, st, pos, pos, dec, col, col, gsp],
        out_specs=[qk, qk, v],
        out_shape=[jax.ShapeDtypeStruct((t, RET_QK), BF16), jax.ShapeDtypeStruct((t, RET_QK), BF16),
                   jax.ShapeDtypeStruct((t, RET_V), BF16)],
        scratch=[pltpu.VMEM((RET_HEADS, RET_QK_DIM, RET_V_DIM), F32)],
        args=[q_r, k_r, v_r, d_o, states, cos, sin_s, decay, xi, zeta, gch], name="ret_bwd", exchange=exchange)


def _position():
    return lax.axis_index("x"), lax.axis_index("y"), lax.axis_index("c")


def _gather_exchange(owns, forward_at):
    n = len(owns)

    def copies(ins, outs, send_sems, recv_sems, base):
        x, y, c = _position()
        sibling = (x, y, 1 - c)
        chips = [(1 - x, y), (x, 1 - y), (1 - x, 1 - y)]
        my_chip = 2 * x + y

        def slab(a, chip, hf):
            half = owns[a].shape[0] // 2
            return outs[a].at[chip, pl.ds(hf * half, half), :]

        def copy(k, src, dst, to):
            return pltpu.make_async_remote_copy(src_ref=src, dst_ref=dst, send_sem=send_sems.at[base + k],
                                                recv_sem=recv_sems.at[base + k], device_id=to, device_id_type=MESH)

        first, passed, from_sibling = [], [], []
        for a in range(n):
            half = owns[a].shape[0] // 2
            for k, (cx, cy) in enumerate(chips):
                first.append(copy(6 * a + k, ins[a].at[pl.ds(c * half, half), :], slab(a, my_chip, c), (cx, cy, c)))
                landed = slab(a, 2 * cx + cy, c)
                passed.append(copy(6 * a + 3 + k, landed, landed, sibling))
                theirs = slab(a, 2 * cx + cy, 1 - c)
                from_sibling.append(copy(6 * a + 3 + k, theirs, theirs, sibling))
        return first, passed, from_sibling

    def start(*args):
        first, _, _ = copies(*args)
        for cp in first:
            cp.start()

    def forward(*args):
        first, passed, _ = copies(*args)
        for arrived, cp in zip(first, passed):
            arrived.wait_recv()
            cp.start()

    def finish(*args):
        first, passed, from_sibling = copies(*args)
        for cp in from_sibling:
            cp.wait_recv()
        for cp in first + passed:
            cp.wait_send()

    outs = [jax.ShapeDtypeStruct((N_CHIPS, *a.shape), a.dtype) for a in owns]
    return _Exchange(owns, outs, 6 * n, [(0.0, start), (forward_at, forward), (1.0, finish)])


def _symmetric_exchange(ins, outs, plan):
    n_sems = len(plan([None] * len(ins), [None] * len(outs), 0, 0, 0, dry=True))

    def copies(in_refs, out_refs, send_sems, recv_sems, base):
        x, y, c = _position()
        return [pltpu.make_async_remote_copy(src_ref=src, dst_ref=dst, send_sem=send_sems.at[base + k],
                                             recv_sem=recv_sems.at[base + k], device_id=dev, device_id_type=MESH)
                for k, (src, dst, dev) in enumerate(plan(in_refs, out_refs, x, y, c, dry=False))]

    def start(*args):
        for cp in copies(*args):
            cp.start()

    def finish(*args):
        for cp in copies(*args):
            cp.wait()

    return _Exchange(ins, outs, n_sems, [(0.0, start), (1.0, finish)])


def _pair_exchange(gs):
    def plan(in_refs, out_refs, x, y, c, dry):
        out = []
        for a, g in enumerate(gs):
            half = g.shape[1] // 2
            for k in range(N_CHIPS):
                out.append(None if dry else (in_refs[a].at[k, pl.ds((1 - c) * half, half), :], out_refs[a].at[k],
                                             (x, y, 1 - c)))
        return out

    outs = [jax.ShapeDtypeStruct((g.shape[0], g.shape[1] // 2, g.shape[2]), g.dtype) for g in gs]
    return _symmetric_exchange(gs, outs, plan)


def _pair_sum(g, from_sibling, c_arr, *, tile, name):
    n, rows, width = g.shape
    tiles = (rows // 2) // tile

    def kern(c_ref, g_ref, s_ref, o_ref):
        o_ref[...] = _bf(g_ref[...] + s_ref[...])

    return pl.pallas_call(
        kern,
        grid_spec=pltpu.PrefetchScalarGridSpec(
            num_scalar_prefetch=1, grid=(n, tiles),
            in_specs=[pl.BlockSpec((None, tile, width), lambda k, i, c: (k, c[0] * tiles + i, 0)),
                      pl.BlockSpec((None, tile, width), lambda k, i, c: (k, i, 0))],
            out_specs=pl.BlockSpec((None, tile, width), lambda k, i, c: (k, i, 0))),
        out_shape=jax.ShapeDtypeStruct((n, rows // 2, width), BF16), name=name,
        compiler_params=_params(("parallel", "parallel")),
    )(c_arr, g, from_sibling)


def _scatter_to_owners(hsums):
    def plan(in_refs, out_refs, x, y, c, dry):
        out = []
        for a in range(len(hsums)):
            for k, (cx, cy) in enumerate([(1 - x, y), (x, 1 - y), (1 - x, 1 - y)]):
                out.append(None if dry else (in_refs[a].at[2 * cx + cy], out_refs[a].at[k], (cx, cy, c)))
        return out

    outs = [jax.ShapeDtypeStruct((3, *h.shape[1:]), h.dtype) for h in hsums]
    return _symmetric_exchange(hsums, outs, plan)


def _sum_chips(hsum, parts, chip_arr, *, tile, name):
    n, half, width = parts.shape

    def kern(chip_ref, h_ref, p_ref, o_ref):
        acc = h_ref[...].astype(F32)
        for k in range(n):
            acc = acc + p_ref[k].astype(F32)
        o_ref[...] = acc

    return pl.pallas_call(
        kern,
        grid_spec=pltpu.PrefetchScalarGridSpec(
            num_scalar_prefetch=1, grid=(half // tile,),
            in_specs=[pl.BlockSpec((None, tile, width), lambda i, chip: (chip[0], i, 0)),
                      pl.BlockSpec((n, tile, width), lambda i, chip: (0, i, 0))],
            out_specs=pl.BlockSpec((tile, width), lambda i, chip: (i, 0))),
        out_shape=jax.ShapeDtypeStruct((half, width), F32), name=name,
        compiler_params=_params(("parallel",)),
    )(chip_arr, hsum, parts)


def _share_halves(fhalves):
    def plan(in_refs, out_refs, x, y, c, dry):
        return [None if dry else (in_refs[a], out_refs[a], (x, y, 1 - c)) for a in range(len(fhalves))]

    return _symmetric_exchange(fhalves, [jax.ShapeDtypeStruct(f.shape, f.dtype) for f in fhalves], plan)


def _adamw_math(w, g, m, v):
    m = ADAM_B1 * m + (1.0 - ADAM_B1) * g
    v = ADAM_B2 * v + (1.0 - ADAM_B2) * (g * g)
    m_hat = m / (1.0 - ADAM_B1 ** ADAM_STEP)
    v_hat = v / (1.0 - ADAM_B2 ** ADAM_STEP)
    delta = -ADAM_LR * (m_hat / (jnp.sqrt(v_hat) + ADAM_EPS) + ADAM_WD * w)
    return delta, m, v


def _adamw(w, m, v, g_mine, g_other, c_arr, *, row_off, tile, name):
    rows, width = w.shape
    tiles_per_half = g_mine.shape[0] // tile
    first = row_off // tile

    def kern(c_ref, w_ref, gm_ref, go_ref, m_ref, v_ref, g_ref, d_ref, nm_ref, nv_ref):
        in_my_half = ((first + pl.program_id(0)) // tiles_per_half) == c_ref[0]
        g = jnp.where(in_my_half, gm_ref[...], go_ref[...])
        g_ref[...] = g
        d_ref[...], nm_ref[...], nv_ref[...] = _adamw_math(w_ref[...], g, m_ref[...], v_ref[...])

    full = pl.BlockSpec((tile, width), lambda i, c: (i, 0))
    half = pl.BlockSpec((tile, width), lambda i, c: ((first + i) % tiles_per_half, 0))
    return pl.pallas_call(
        kern,
        grid_spec=pltpu.PrefetchScalarGridSpec(
            num_scalar_prefetch=1, grid=(rows // tile,), in_specs=[full, half, half, full, full],
            out_specs=[full] * 4),
        out_shape=[jax.ShapeDtypeStruct((rows, width), F32)] * 4, name=name,
        compiler_params=_params(("parallel",)),
    )(c_arr, w, g_mine, g_other, m, v)


def _small_step(part, tile_sums, w, m, v):
    loss_p, dg1_p, dg2_p = tile_sums

    def body(part_ref, loss_ref, dg1_ref, dg2_ref, w_ref, m_ref, v_ref, g_ref, d_ref, nm_ref, nv_ref,
             mine, gathered, send_sems, recv_sems):
        x, y, c = _position()
        me = 4 * x + 2 * y + c
        mine[...] = part_ref[...]
        for r in range(8):
            lanes = slice(128 * r, 128 * (r + 1))
            mine[SM_G1 + r:SM_G1 + r + 1, :] = jnp.sum(dg1_ref[:, lanes], axis=0, keepdims=True)
            mine[SM_G2 + r:SM_G2 + r + 1, :] = jnp.sum(dg2_ref[:, lanes], axis=0, keepdims=True)
        mine[SM_LOSS:SM_LOSS + 1, :] = jnp.sum(loss_ref[...], axis=0, keepdims=True)
        copies = []
        for k in range(1, N_DEV):
            flip = (k >> 2) & 1, (k >> 1) & 1, k & 1
            to = (x ^ flip[0], y ^ flip[1], c ^ flip[2])
            cp = pltpu.make_async_remote_copy(
                src_ref=mine, dst_ref=gathered.at[me], send_sem=send_sems.at[k - 1], recv_sem=recv_sems.at[k - 1],
                device_id=to, device_id_type=MESH)
            cp.start()
            copies.append(cp)
        gathered[me] = mine[...]
        for k in range(1, N_DEV):
            flip = (k >> 2) & 1, (k >> 1) & 1, k & 1
            src = 4 * (x ^ flip[0]) + 2 * (y ^ flip[1]) + (c ^ flip[2])
            pltpu.make_async_remote_copy(
                src_ref=mine, dst_ref=gathered.at[src], send_sem=send_sems.at[k - 1], recv_sem=recv_sems.at[k - 1],
                device_id=(x, y, c), device_id_type=MESH).wait_recv()
        for cp in copies:
            cp.wait_send()
        total = gathered[0]
        for k in range(1, N_DEV):
            total = total + gathered[k]
        g_ref[...] = total
        d_ref[...], nm_ref[...], nv_ref[...] = _adamw_math(w_ref[...], total, m_ref[...], v_ref[...])

    vm = pl.BlockSpec(memory_space=pltpu.VMEM)
    blk = jax.ShapeDtypeStruct((SMALL_ROWS, 128), F32)
    return pl.pallas_call(
        body, in_specs=[vm] * 7, out_specs=[vm] * 4, out_shape=[blk] * 4,
        scratch_shapes=[pltpu.VMEM((SMALL_ROWS, 128), F32), pltpu.VMEM((N_DEV, SMALL_ROWS, 128), F32),
                        pltpu.SemaphoreType.DMA((N_DEV - 1,)), pltpu.SemaphoreType.DMA((N_DEV - 1,))],
        name="small_step",
    )(part, loss_p.reshape(-1, 128), dg1_p.reshape(-1, D_MODEL), dg2_p.reshape(-1, D_MODEL), w, m, v)


def _shards(own, gathered, my_chip):
    return [jnp.where(my_chip == k, own, gathered[k]) for k in range(N_CHIPS)]


def _by_owner_rows(blocks):
    return jnp.stack([jnp.concatenate([b[k * (b.shape[0] // N_CHIPS):(k + 1) * (b.shape[0] // N_CHIPS)]
                                       for b in blocks], axis=0) for k in range(N_CHIPS)], axis=0)


def _by_owner_cols(blocks):
    w = blocks[0].shape[1] // N_CHIPS
    return jnp.stack([jnp.concatenate([b[:, k * w:(k + 1) * w] for b in blocks], axis=0) for k in range(N_CHIPS)],
                     axis=0)


def _pack_small(g1, g2, gq, gk, sinks):
    blk = jnp.zeros((SMALL_ROWS, 128), F32)
    blk = blk.at[SM_G1:SM_G1 + 8].set(g1.reshape(8, 128))
    blk = blk.at[SM_G2:SM_G2 + 8].set(g2.reshape(8, 128))
    blk = blk.at[SM_GQ, :HEAD_DIM].set(gq.reshape(-1))
    blk = blk.at[SM_GK, :HEAD_DIM].set(gk.reshape(-1))
    blk = blk.at[SM_SINK, :N_Q_HEADS].set(sinks.reshape(-1))
    return blk


def _unpack_small(blk):
    return (blk[SM_G1:SM_G1 + 8].reshape(1, D_MODEL), blk[SM_G2:SM_G2 + 8].reshape(1, D_MODEL),
            blk[SM_GQ, :HEAD_DIM].reshape(1, HEAD_DIM), blk[SM_GK, :HEAD_DIM].reshape(1, HEAD_DIM),
            blk[SM_SINK, :N_Q_HEADS].reshape(1, N_Q_HEADS))


def kernel(x, norm_mix_gain, w_in, q_norm_gain, k_norm_gain, attn_sinks, w_branch_attn, w_branch_ret, w_out, norm_ffn_gain, w_ffn_gate, w_ffn_up, w_ffn_down, loss_target, m_norm_mix_gain, m_w_in, m_q_norm_gain, m_k_norm_gain, m_attn_sinks, m_w_branch_attn, m_w_branch_ret, m_w_out, m_norm_ffn_gain, m_w_ffn_gate, m_w_ffn_up, m_w_ffn_down, v_norm_mix_gain, v_w_in, v_q_norm_gain, v_k_norm_gain, v_attn_sinks, v_w_branch_attn, v_w_branch_ret, v_w_out, v_norm_ffn_gain, v_w_ffn_gate, v_w_ffn_up, v_w_ffn_down):
    my_chip = 2 * lax.axis_index("x") + lax.axis_index("y")
    c_arr = lax.axis_index("c").astype(jnp.int32).reshape(1)
    chip_arr = my_chip.astype(jnp.int32).reshape(1)
    x_t, target = x[0], loss_target[0]
    g1, g2, gq, gk, sinks = norm_mix_gain, norm_ffn_gain, q_norm_gain, k_norm_gain, attn_sinks
    tables = _ret_tables(x_t.shape[0])

    own_w_in = _bf(w_in[0])
    own_mix = _bf(jnp.concatenate([w_branch_attn[0], w_branch_ret[0], w_out[0]], axis=0))
    own_wd = _bf(w_ffn_down[0])
    own_ffn = _bf(jnp.concatenate([w_ffn_gate[0], w_ffn_up[0]], axis=0))
    got_w_in, = _run_exchange(_gather_exchange([own_w_in], 0.0), "gather_w_in")
    w_in_full = jnp.concatenate(_shards(own_w_in, got_w_in, my_chip), axis=1)
    (h1, q_a, kv_a, q_r, k_r, v_r, g_r, z_a, z_r, got_mix, got_wd, got_ffn) = _proj_fwd(
        x_t, g1, w_in_full, _gather_exchange([own_mix, own_wd, own_ffn], 0.8))
    mix = _shards(own_mix, got_mix, my_chip)
    wba = jnp.concatenate([s[0:256] for s in mix], axis=0)
    wbr = jnp.concatenate([s[256:768] for s in mix], axis=0)
    wout = jnp.concatenate([s[768:1024] for s in mix], axis=0)
    wd = jnp.concatenate(_shards(own_wd, got_wd, my_chip), axis=0)
    ffn = _shards(own_ffn, got_ffn, my_chip)
    wg = jnp.concatenate([s[:D_MODEL] for s in ffn], axis=1)
    wu = jnp.concatenate([s[D_MODEL:] for s in ffn], axis=1)

    gq_col, gk_col = gq.reshape(HEAD_DIM, 1), gk.reshape(HEAD_DIM, 1)
    attn = _attn_fwd(q_a, kv_a, gq_col, gk, sinks)
    o_ret, ret, states = _ret_fwd(q_r, k_r, v_r, g_r, tables)
    ba, br, merged, x1, h2 = _mix_fwd(attn, ret, z_a, z_r, x_t, wba, wbr, wout, g2)
    act, dgate, dup, dyb, dx1, dx1b, loss_p, dg2_p = _ffn_fwd_bwd(h2, x1, target, wg, wu, wd, g2)

    def reduce_tail(name, g_half, g_other, row_off, tile, wmv):
        w, m, v = wmv
        return _adamw(w[0], m[0], v[0], g_half, g_other, c_arr, row_off=row_off, tile=tile, name=f"adamw_{name}")

    dw_gate, = _matmul_tn(h2, dgate, tm=1024, tn=D_FF // 2, name="dw_gate")
    dw_up, = _matmul_tn(h2, dup, tm=1024, tn=D_FF // 2, name="dw_up")
    dw_down, = _matmul_tn(act, dyb, tm=D_FF // 2, tn=1024, name="dw_down")
    f_blocks = [_by_owner_cols([dw_gate, dw_up]), dw_down.reshape(N_CHIPS, FF_SH, D_MODEL)]
    (dba, dbr, dz_a, dz_r, d_attn, d_o, dg_r, sib_ffn, sib_wd) = _mix_bwd(
        dx1b, z_a, z_r, ba, br, g_r, o_ret, wout, wba, wbr, _pair_exchange(f_blocks))
    f_sums = [_pair_sum(f_blocks[0], sib_ffn, c_arr, tile=512, name="pair_sum_ffn"),
              _pair_sum(f_blocks[1], sib_wd, c_arr, tile=FF_SH // 2, name="pair_sum_wd")]

    dw_ba, = _matmul_tn(attn, dba, tm=1024, tn=1024, name="dw_ba")
    dw_br, = _matmul_tn(ret, dbr, tm=1024, tn=1024, name="dw_br")
    dw_out, = _matmul_tn(merged, dx1b, tm=1024, tn=1024, name="dw_out")
    m_block = _by_owner_rows([dw_ba, dw_br, dw_out])
    dw_in = [None] * 8
    dw_in[5], sib_mix = _matmul_tn(h1, dg_r, tm=D_MODEL, tn=1024, name="dw_in_5", exchange=_pair_exchange([m_block]))
    dw_in[6], = _matmul_tn(h1, dz_a, tm=D_MODEL, tn=1024, name="dw_in_6")
    dw_in[7], = _matmul_tn(h1, dz_r, tm=D_MODEL, tn=1024, name="dw_in_7")
    m_sum = _pair_sum(m_block, sib_mix, c_arr, tile=256, name="pair_sum_mix")

    dq_r, dk_r, dv_r, got_ffn_sums, got_wd_sums = _ret_bwd(q_r, k_r, v_r, d_o, states, tables, _scatter_to_owners(f_sums))
    ffn_half = _sum_chips(f_sums[0], got_ffn_sums, chip_arr, tile=512, name="sum_chips_ffn")
    wd_half = _sum_chips(f_sums[1], got_wd_sums, chip_arr, tile=FF_SH // 2, name="sum_chips_wd")
    dw_in[2], = _matmul_tn(h1, dq_r, tm=D_MODEL, tn=1024, name="dw_in_2")
    dw_in[3], = _matmul_tn(h1, dk_r, tm=D_MODEL, tn=1024, name="dw_in_3")
    dw_in[4], = _matmul_tn(h1, dv_r, tm=D_MODEL, tn=1024, name="dw_in_4")

    (dq_a, dkv_a, dgq, dgk, dsinks, got_mix_sums, ffn_other, wd_other) = _attn_bwd(
        q_a, kv_a, d_attn, gq_col, gk, gk_col, sinks,
        _merge_exchanges(_scatter_to_owners([m_sum]), _share_halves([ffn_half, wd_half])))
    dgq = dgq.reshape(1, HEAD_DIM)
    mix_half = _sum_chips(m_sum, got_mix_sums, chip_arr, tile=256, name="sum_chips_mix")
    dw_in[0], mix_other = _matmul_tn(h1, dq_a, tm=D_MODEL, tn=1024, name="dw_in_0", exchange=_share_halves([mix_half]))
    dw_in[1], = _matmul_tn(h1, dkv_a, tm=D_MODEL, tn=256, name="dw_in_1")

    w_block = _by_owner_cols([jnp.concatenate(dw_in, axis=1)])
    sib_w_in, = _run_exchange(_pair_exchange([w_block]), "pair_exchange_w_in")
    w_sum = _pair_sum(w_block, sib_w_in, c_arr, tile=256, name="pair_sum_w_in")
    d_pieces = [dq_a, dkv_a, dq_r, dk_r, dv_r, dg_r, dz_a, dz_r]
    grad_x, dg1_p, got_w_in_sums = _proj_bwd(d_pieces, x_t, dx1, w_in_full, g1, _scatter_to_owners([w_sum]))
    w_in_half = _sum_chips(w_sum, got_w_in_sums, chip_arr, tile=256, name="sum_chips_w_in")
    w_in_other, = _run_exchange(_share_halves([w_in_half]), "share_halves_w_in")

    big = dict(
        w_in=reduce_tail("w_in", w_in_half, w_in_other, 0, 256, (w_in, m_w_in, v_w_in)),
        wg=reduce_tail("wg", ffn_half, ffn_other, 0, 512, (w_ffn_gate, m_w_ffn_gate, v_w_ffn_gate)),
        wu=reduce_tail("wu", ffn_half, ffn_other, D_MODEL, 512, (w_ffn_up, m_w_ffn_up, v_w_ffn_up)),
        wd=reduce_tail("wd", wd_half, wd_other, 0, FF_SH // 2, (w_ffn_down, m_w_ffn_down, v_w_ffn_down)),
        wba=reduce_tail("wba", mix_half, mix_other, 0, 256, (w_branch_attn, m_w_branch_attn, v_w_branch_attn)),
        wbr=reduce_tail("wbr", mix_half, mix_other, 256, 256, (w_branch_ret, m_w_branch_ret, v_w_branch_ret)),
        wout=reduce_tail("wout", mix_half, mix_other, 768, 256, (w_out, m_w_out, v_w_out)))

    zeros = jnp.zeros((1, D_MODEL), F32)
    part = _pack_small(zeros, zeros, dgq, dgk, dsinks)
    sm_w = _pack_small(norm_mix_gain, norm_ffn_gain, q_norm_gain, k_norm_gain, attn_sinks)
    sm_m = _pack_small(m_norm_mix_gain, m_norm_ffn_gain, m_q_norm_gain, m_k_norm_gain, m_attn_sinks)
    sm_v = _pack_small(v_norm_mix_gain, v_norm_ffn_gain, v_q_norm_gain, v_k_norm_gain, v_attn_sinks)
    sm_g, sm_d, sm_nm, sm_nv = _small_step(part, (loss_p, dg1_p, dg2_p), sm_w, sm_m, sm_v)
    loss = sm_g[SM_LOSS, 0]

    def leaves(i, sm):
        b = [big[n][i][None] for n in ("w_in", "wba", "wbr", "wout", "wg", "wu", "wd")]
        s1, s2, sq, sk, ss = _unpack_small(sm)
        return [s1, b[0], sq, sk, ss, b[1], b[2], b[3], s2, b[4], b[5], b[6]]

    return (loss, grad_x[None], *leaves(0, sm_g), *leaves(1, sm_d), *leaves(2, sm_nm), *leaves(3, sm_nv))
```

```python
import jax
import jax.numpy as jnp
from jax import lax
from jax.experimental import pallas as pl
from jax.experimental.pallas import tpu as pltpu

F32 = jnp.float32
BF16 = jnp.bfloat16
MESH = pl.DeviceIdType.MESH

D_MODEL = 1024
EPS = 1e-6
HEAD_DIM = 64
N_Q_HEADS = 16
N_KV_HEADS = 2
GROUP = 8
BLOCK = 128
RET_HEADS = 4
RET_QK_DIM = 256
RET_V_DIM = 512
RET_CHUNK = 128
RET_ROT_BASE = 10000.0
D_FF = 2816
ATT_Q = N_Q_HEADS * HEAD_DIM
ATT_KV = N_KV_HEADS * HEAD_DIM
RET_QK = RET_HEADS * RET_QK_DIM
RET_V = RET_HEADS * RET_V_DIM
D_IN = 9472
ADAM_LR = 0.001
ADAM_B1 = 0.9
ADAM_B2 = 0.999
ADAM_EPS = 1e-08
ADAM_WD = 0.01
ADAM_STEP = 10

N_CHIPS = 4
N_DEV = 8
VMEM_LIMIT_BYTES = 60 * 1024 * 1024

P_QA = (0, 1024)
P_KVA = (1024, 256)
P_QR = (1280, 1024)
P_KR = (2304, 1024)
P_VR = (3328, 2048)
P_GR = (5376, 2048)
P_ZA = (7424, 1024)
P_ZR = (8448, 1024)

W_IN_SH = D_IN // N_CHIPS
FF_SH = D_FF // N_CHIPS

SMALL_ROWS = 24
SM_G1, SM_G2, SM_GQ, SM_GK, SM_SINK, SM_LOSS = 0, 8, 16, 17, 18, 19


def _dot(a, b):
    return jnp.dot(a, b, preferred_element_type=F32)


def _dot_nt(a, b):
    return lax.dot_general(a, b, (((1,), (1,)), ((), ())), preferred_element_type=F32)


def _dot_tn(a, b):
    return lax.dot_general(a, b, (((0,), (0,)), ((), ())), preferred_element_type=F32)


def _bf(x):
    return x.astype(BF16)


def _rms_stats(x):
    r = lax.rsqrt(jnp.mean(x * x, axis=-1, keepdims=True) + EPS)
    return r, x * r


def _rms_bwd(dy, xhat, r, gain):
    u = dy * gain
    dx = r * (u - xhat * jnp.mean(u * xhat, axis=-1, keepdims=True))
    return dx, dy * xhat


def _params(sem):
    return pltpu.CompilerParams(dimension_semantics=sem, vmem_limit_bytes=VMEM_LIMIT_BYTES)


_ANY = pl.BlockSpec(memory_space=pl.ANY)


class _Exchange:
    def __init__(self, ins, outs, n_sems, phases):
        self.ins, self.outs, self.n_sems, self.phases = list(ins), list(outs), n_sems, list(phases)


def _merge_exchanges(a, b):
    na_i, na_o, shift = len(a.ins), len(a.outs), a.n_sems

    def first(fn):
        return lambda i, o, s, r, base: fn(i[:na_i], o[:na_o], s, r, base)

    def second(fn):
        return lambda i, o, s, r, base: fn(i[na_i:], o[na_o:], s, r, base + shift)

    phases = [(f, first(fn)) for f, fn in a.phases] + [(f, second(fn)) for f, fn in b.phases]
    return _Exchange(a.ins + b.ins, a.outs + b.outs, a.n_sems + b.n_sems, sorted(phases, key=lambda p: p[0]))


def _pallas(kern, *, grid, in_specs, out_specs, out_shape, args, name, scratch=(), exchange=None, aliases=None):
    aliases = aliases or {}
    if exchange is None:
        return pl.pallas_call(
            kern, grid=grid, in_specs=in_specs, out_specs=out_specs, out_shape=out_shape, name=name,
            scratch_shapes=list(scratch), input_output_aliases=aliases,
            compiler_params=_params(("arbitrary",) * len(grid)))(*args)
    n_in, n_out, n_sc = len(in_specs), len(out_specs), len(scratch)
    n_xi, n_xo = len(exchange.ins), len(exchange.outs)
    n_steps = 1
    for g in grid:
        n_steps *= g

    def wrapped(*refs):
        ins, refs = refs[:n_in], refs[n_in:]
        x_ins, refs = refs[:n_xi], refs[n_xi:]
        outs, refs = refs[:n_out], refs[n_out:]
        x_outs, refs = refs[:n_xo], refs[n_xo:]
        scr, (send_sems, recv_sems) = refs[:n_sc], refs[n_sc:]
        step = pl.program_id(0)
        for d in range(1, len(grid)):
            step = step * grid[d] + pl.program_id(d)
        for frac, fn in exchange.phases:
            at = min(int(frac * n_steps), n_steps - 1)

            @pl.when(step == at)
            def _(fn=fn):
                fn(x_ins, x_outs, send_sems, recv_sems, 0)

        kern(*ins, *outs, *scr)

    sems = [pltpu.SemaphoreType.DMA((exchange.n_sems,)), pltpu.SemaphoreType.DMA((exchange.n_sems,))]
    return pl.pallas_call(
        wrapped, grid=grid, in_specs=list(in_specs) + [_ANY] * n_xi, out_specs=list(out_specs) + [_ANY] * n_xo,
        out_shape=list(out_shape) + exchange.outs, name=name, scratch_shapes=list(scratch) + sems,
        input_output_aliases=aliases, compiler_params=_params(("arbitrary",) * len(grid)))(*args, *exchange.ins)


def _run_exchange(exchange, name):
    def body(*refs):
        n_i, n_o = len(exchange.ins), len(exchange.outs)
        for _, fn in exchange.phases:
            fn(refs[:n_i], refs[n_i:n_i + n_o], refs[n_i + n_o], refs[n_i + n_o + 1], 0)

    sems = [pltpu.SemaphoreType.DMA((exchange.n_sems,)), pltpu.SemaphoreType.DMA((exchange.n_sems,))]
    return pl.pallas_call(body, in_specs=[_ANY] * len(exchange.ins), out_specs=[_ANY] * len(exchange.outs),
                          out_shape=exchange.outs, scratch_shapes=sems, name=name)(*exchange.ins)


def _row_call(body, *, tm, row_ins, res_ins, row_outs, part_outs=(), name, exchange=None):
    t = row_ins[0].shape[0]
    n_tiles = t // tm
    in_specs = [pl.BlockSpec((tm, a.shape[1]), lambda i: (i, 0)) for a in row_ins]
    in_specs += [pl.BlockSpec(a.shape, lambda i: (0, 0), pipeline_mode=pl.Buffered(1)) for a in res_ins]
    out_shape = [jax.ShapeDtypeStruct((t, w), dt) for (w, dt) in row_outs]
    out_shape += [jax.ShapeDtypeStruct((n_tiles, 1, w), F32) for w in part_outs]
    out_specs = [pl.BlockSpec((tm, w), lambda i: (i, 0)) for (w, _) in row_outs]
    out_specs += [pl.BlockSpec((1, 1, w), lambda i: (i, 0, 0)) for w in part_outs]
    n_ri, n_re, n_ro = len(row_ins), len(res_ins), len(row_outs)

    def kern(*refs):
        body(refs[:n_ri], refs[n_ri:n_ri + n_re], refs[n_ri + n_re:n_ri + n_re + n_ro], refs[n_ri + n_re + n_ro:])

    return _pallas(kern, grid=(n_tiles,), in_specs=in_specs, out_specs=out_specs, out_shape=out_shape,
                   args=[*row_ins, *res_ins], name=name, exchange=exchange)


def _proj_fwd(x, g1, w_in_t, exchange):
    pieces = ((P_QA, F32), (P_KVA, F32), (P_QR, F32), (P_KR, F32), (P_VR, BF16), (P_GR, F32), (P_ZA, F32), (P_ZR, F32))

    def body(ri, re, ro, po):
        x_t = ri[0][...]
        r, xhat = _rms_stats(x_t)
        hb = _bf(xhat * re[0][...])
        ro[0][...] = hb
        for k, ((off, w), dt) in enumerate(pieces):
            ro[1 + k][...] = _dot_nt(hb, re[1][off:off + w, :]).astype(dt)

    outs = [(D_MODEL, BF16)] + [(w, dt) for ((_, w), dt) in pieces]
    return _row_call(body, tm=256, row_ins=[x], res_ins=[g1, w_in_t], row_outs=outs, name="proj_fwd",
                     exchange=exchange)


def _mix_fwd(attn, ret, z_a, z_r, x, wba, wbr, wout, g2):
    def body(ri, re, ro, po):
        ba = _dot(ri[0][...], re[0][...])
        br = _dot(ri[1][...], re[1][...])
        m = jax.nn.sigmoid(ri[2][...]) * ba + jax.nn.sigmoid(ri[3][...]) * br
        mb = _bf(m)
        x1 = ri[4][...] + _dot(mb, re[2][...])
        r, xhat = _rms_stats(x1)
        ro[0][...] = ba
        ro[1][...] = br
        ro[2][...] = mb
        ro[3][...] = x1
        ro[4][...] = _bf(xhat * re[3][...])

    outs = [(D_MODEL, F32), (D_MODEL, F32), (D_MODEL, BF16), (D_MODEL, F32), (D_MODEL, BF16)]
    return _row_call(body, tm=256, row_ins=[attn, ret, z_a, z_r, x], res_ins=[wba, wbr, wout, g2], row_outs=outs,
                     name="mix_fwd")


def _ffn_fwd_bwd(h2, x1, target, wg_t, wu_t, wd, g2):
    def body(ri, re, ro, po):
        h2_t = ri[0][...]
        x1_t = ri[1][...]
        gate = _dot_nt(h2_t, re[0][...])
        up = _dot_nt(h2_t, re[1][...])
        sg = jax.nn.sigmoid(gate)
        sl = gate * sg
        actb = _bf(sl * up)
        ro[0][...] = actb
        y = x1_t + _dot(actb, re[2][...])
        e = y - ri[2][...]
        po[0][0] = jnp.broadcast_to(0.5 * jnp.sum(jnp.sum(e * e, axis=1, keepdims=True), axis=0, keepdims=True)
                                    * (1.0 / D_MODEL), (1, 128))
        dy = e * (1.0 / D_MODEL)
        dyb = _bf(dy)
        ro[3][...] = dyb
        dact = _dot_nt(dyb, re[2][...])
        dupb = _bf(dact * sl)
        dgateb = _bf(dact * up * (sg * (1.0 + gate * (1.0 - sg))))
        ro[1][...] = dgateb
        ro[2][...] = dupb
        dh2 = _dot(dgateb, re[0][...]) + _dot(dupb, re[1][...])
        r, xhat = _rms_stats(x1_t)
        dxn, dgain = _rms_bwd(dh2, xhat, r, re[3][...])
        dx1 = dy + dxn
        ro[4][...] = dx1
        ro[5][...] = _bf(dx1)
        po[1][0] = jnp.sum(dgain, axis=0, keepdims=True)

    outs = [(D_FF, BF16), (D_FF, BF16), (D_FF, BF16), (D_MODEL, BF16), (D_MODEL, F32), (D_MODEL, BF16)]
    return _row_call(body, tm=256, row_ins=[h2, x1, target], res_ins=[wg_t, wu_t, wd, g2], row_outs=outs,
                     part_outs=(128, D_MODEL), name="ffn_fwd_bwd")


def _mix_bwd(dx1b, z_a, z_r, ba, br, g_r, o_ret, wout, wba, wbr, exchange):
    def body(ri, re, ro, po):
        dm = _dot_nt(ri[0][...], re[0][...])
        sa = jax.nn.sigmoid(ri[1][...])
        sr = jax.nn.sigmoid(ri[2][...])
        dbab = _bf(sa * dm)
        dbrb = _bf(sr * dm)
        ro[0][...] = dbab
        ro[1][...] = dbrb
        ro[2][...] = _bf(dm * ri[3][...] * (sa * (1.0 - sa)))
        ro[3][...] = _bf(dm * ri[4][...] * (sr * (1.0 - sr)))
        ro[4][...] = _bf(_dot_nt(dbab, re[1][...]))
        dret = _dot_nt(dbrb, re[2][...])
        for h in range(RET_HEADS):
            cols = slice(h * RET_V_DIM, (h + 1) * RET_V_DIM)
            g = ri[5][:, cols]
            r, rn = _rms_stats(ri[6][:, cols])
            sg = jax.nn.sigmoid(g)
            dret_h = dret[:, cols]
            d_rn = dret_h * (g * sg)
            ro[6][:, cols] = _bf(dret_h * rn * (sg * (1.0 + g * (1.0 - sg))))
            ro[5][:, cols] = r * (d_rn - rn * jnp.mean(d_rn * rn, axis=-1, keepdims=True))

    outs = [(D_MODEL, BF16), (D_MODEL, BF16), (D_MODEL, BF16), (D_MODEL, BF16), (ATT_Q, BF16), (RET_V, F32),
            (RET_V, BF16)]
    return _row_call(body, tm=256, row_ins=[dx1b, z_a, z_r, ba, br, g_r, o_ret], res_ins=[wout, wba, wbr],
                     row_outs=outs, name="mix_bwd", exchange=exchange)


def _proj_bwd(d_pieces, x, dx1, w_in_t, g1, exchange):
    groups = (P_QA, P_KVA, P_QR, P_KR, P_VR, P_GR, P_ZA, P_ZR)
    n_p = len(groups)

    def body(ri, re, ro, po):
        dh = None
        for k, (off, w) in enumerate(groups):
            term = _dot(ri[k][...], re[0][off:off + w, :])
            dh = term if dh is None else dh + term
        r, xhat = _rms_stats(ri[n_p][...])
        dxn, dgain = _rms_bwd(dh, xhat, r, re[1][...])
        ro[0][...] = ri[n_p + 1][...] + dxn
        po[0][0] = jnp.sum(dgain, axis=0, keepdims=True)

    return _row_call(body, tm=256, row_ins=[*d_pieces, x, dx1], res_ins=[w_in_t, g1], row_outs=[(D_MODEL, F32)],
                     part_outs=(D_MODEL,), name="proj_bwd", exchange=exchange)


def _dw(a, b, *, tm, place, buf, name, exchange=None):
    t, m = a.shape
    n = b.shape[1]
    tk = min(2048, t)
    n_i, n_k = m // tm, t // tk
    fresh = isinstance(buf, jax.ShapeDtypeStruct)
    n_copies = len(place(0))

    def kern(a_ref, b_ref, *rest):
        out_ref, acc, sems = rest[-3:]
        i, k = pl.program_id(0), pl.program_id(1)
        part = _dot_tn(a_ref[...], b_ref[...])

        @pl.when(k == 0)
        def _():
            acc[i] = part

        @pl.when(k > 0)
        def _():
            acc[i] += part

        def copies(tile):
            return [pltpu.make_async_copy(acc.at[tile, pl.ds(r0, rows), :], out_ref.at[idx], sems.at[tile * n_copies + c])
                    for c, (r0, rows, idx) in enumerate(place(tile))]

        for tile in range(n_i):
            @pl.when((i == tile) & (k == n_k - 1))
            def _(tile=tile):
                for cp in copies(tile):
                    cp.start()

        @pl.when((i == n_i - 1) & (k == n_k - 1))
        def _():
            for tile in range(n_i):
                for cp in copies(tile):
                    cp.wait()

    in_specs = [pl.BlockSpec((tk, tm), lambda i, k: (k, i)), pl.BlockSpec((tk, n), lambda i, k: (k, 0))]
    shape = buf if fresh else jax.ShapeDtypeStruct(buf.shape, buf.dtype)
    return _pallas(
        kern, grid=(n_i, n_k), in_specs=in_specs + ([] if fresh else [_ANY]), out_specs=[_ANY], out_shape=[shape],
        scratch=[pltpu.VMEM((n_i, tm, n), F32), pltpu.SemaphoreType.DMA((n_i * n_copies,))],
        args=[a, b] + ([] if fresh else [buf]), aliases=None if fresh else {2: 0}, name=name, exchange=exchange)


def _heads_to_lanes(x3):
    return jnp.concatenate([x3[g] for g in range(GROUP)], axis=1)


def _lanes_to_heads(xt):
    return jnp.concatenate([xt[:, g * BLOCK:(g + 1) * BLOCK] for g in range(GROUP)], axis=0)


def _attn_group(n, kvh, q_ref, kvp_ref, kvc_ref, gq_col, gk, sink_ref):
    heads = [kvh * GROUP + g for g in range(GROUP)]
    cols = slice(kvh * GROUP * HEAD_DIM, (kvh + 1) * GROUP * HEAD_DIM)
    q3 = q_ref[:, cols].T.reshape(GROUP, HEAD_DIM, BLOCK)
    rq = lax.rsqrt(jnp.mean(q3 * q3, axis=1, keepdims=True) + EPS)
    qhat = q3 * rq
    qts = _heads_to_lanes(_bf(qhat * (gq_col * (HEAD_DIM ** -0.5))))
    kcols = slice(kvh * HEAD_DIM, (kvh + 1) * HEAD_DIM)
    vcols = slice(ATT_KV + kvh * HEAD_DIM, ATT_KV + (kvh + 1) * HEAD_DIM)
    k = jnp.concatenate([kvp_ref[:, kcols], kvc_ref[:, kcols]], axis=0)
    rk, khat = _rms_stats(k)
    knb = _bf(khat * gk)
    st = _dot(knb, qts)
    j = lax.broadcasted_iota(jnp.int32, (BLOCK, GROUP * BLOCK), 0)
    i = lax.broadcasted_iota(jnp.int32, (BLOCK, GROUP * BLOCK), 1) & (BLOCK - 1)
    from_prev = j > i
    f = jnp.where(from_prev, jnp.where(n > 0, st[0:BLOCK], -1e30), st[BLOCK:2 * BLOCK])
    sink = jnp.concatenate([jnp.broadcast_to(sink_ref[0:1, h:h + 1], (1, BLOCK)) for h in heads], axis=1)
    m = jnp.maximum(jnp.max(f, axis=0, keepdims=True), sink)
    e = jnp.exp(f - m)
    es = jnp.exp(sink - m)
    inv = 1.0 / (jnp.sum(e, axis=0, keepdims=True) + es)
    return dict(heads=heads, qhat=qhat, rq=rq, qts=qts, khat=khat, rk=rk, knb=knb, from_prev=from_prev,
                pf=e * inv, psink=es * inv)


def _unfold(from_prev, xf):
    return _bf(jnp.concatenate([jnp.where(from_prev, xf, 0.0), jnp.where(from_prev, 0.0, xf)], axis=0))


def _attn_fwd(q_a, kv_a, gq_col, gk, sinks):
    t = q_a.shape[0]
    nb = t // BLOCK

    def kern(q_ref, kvp_ref, kvc_ref, gq_ref, gk_ref, sink_ref, o_ref):
        n = pl.program_id(0)
        kvt = jnp.concatenate([kvp_ref[...].T, kvc_ref[...].T], axis=1)
        for kvh in range(N_KV_HEADS):
            a = _attn_group(n, kvh, q_ref, kvp_ref, kvc_ref, gq_ref[...], gk_ref[...], sink_ref)
            vt = _bf(kvt[ATT_KV + kvh * HEAD_DIM:ATT_KV + (kvh + 1) * HEAD_DIM, :])
            out_t = _dot(vt, _unfold(a["from_prev"], a["pf"]))
            cols = slice(kvh * GROUP * HEAD_DIM, (kvh + 1) * GROUP * HEAD_DIM)
            o_ref[:, cols] = _bf(_lanes_to_heads(out_t).T)

    small = lambda a: pl.BlockSpec(a.shape, lambda n: (0, 0))
    return pl.pallas_call(
        kern, grid=(nb,),
        in_specs=[pl.BlockSpec((BLOCK, ATT_Q), lambda n: (n, 0)),
                  pl.BlockSpec((BLOCK, 2 * ATT_KV), lambda n: (jnp.maximum(n - 1, 0), 0)),
                  pl.BlockSpec((BLOCK, 2 * ATT_KV), lambda n: (n, 0)),
                  small(gq_col), small(gk), small(sinks)],
        out_specs=pl.BlockSpec((BLOCK, ATT_Q), lambda n: (n, 0)),
        out_shape=jax.ShapeDtypeStruct((t, ATT_Q), BF16), name="attn_fwd",
        compiler_params=_params(("parallel",)),
    )(q_a, kv_a, kv_a, gq_col, gk, sinks)


def _attn_bwd(q_a, kv_a, d_attn, gq_col, gk, gk_col, sinks, exchange):
    t = q_a.shape[0]
    nb = t // BLOCK

    def kern(q_ref, kvp_ref, kvc_ref, do_ref, gq_ref, gk_ref, gkc_ref, sink_ref,
             dq_ref, dkv_ref, dgq_ref, dgk_ref, dsink_ref, band_k, band_v, carry_k, carry_v):
        n = pl.program_id(0)
        gq_v = gq_ref[...]
        gk_v = gk_ref[...]

        @pl.when(n == 0)
        def _():
            carry_k[...] = jnp.zeros_like(carry_k)
            carry_v[...] = jnp.zeros_like(carry_v)
            dgq_ref[...] = jnp.zeros_like(dgq_ref)
            dgk_ref[...] = jnp.zeros_like(dgk_ref)
            dsink_ref[...] = jnp.zeros_like(dsink_ref)

        @pl.when(n == nb)
        def _():
            band_k[...] = jnp.zeros_like(band_k)
            band_v[...] = jnp.zeros_like(band_v)

        @pl.when(n < nb)
        def _():
            lane16 = lax.broadcasted_iota(jnp.int32, (1, N_Q_HEADS), 1)
            dsink = jnp.zeros((1, N_Q_HEADS), F32)
            dgq = jnp.zeros((HEAD_DIM, 1), F32)
            gk_col = gkc_ref[...]
            kvt = jnp.concatenate([kvp_ref[...].T, kvc_ref[...].T], axis=1)
            for kvh in range(N_KV_HEADS):
                a = _attn_group(n, kvh, q_ref, kvp_ref, kvc_ref, gq_v, gk_v, sink_ref)
                from_prev, pf, qhat = a["from_prev"], a["pf"], a["qhat"]
                cols = slice(kvh * GROUP * HEAD_DIM, (kvh + 1) * GROUP * HEAD_DIM)
                vcols = slice(ATT_KV + kvh * HEAD_DIM, ATT_KV + (kvh + 1) * HEAD_DIM)
                dot = _heads_to_lanes(_bf(do_ref[:, cols].astype(F32).T.reshape(GROUP, HEAD_DIM, BLOCK)))
                vb = _bf(jnp.concatenate([kvp_ref[:, vcols], kvc_ref[:, vcols]], axis=0))
                dpt = _dot(vb, dot)
                dpf = jnp.where(from_prev, dpt[0:BLOCK], dpt[BLOCK:2 * BLOCK])
                delta = jnp.sum(pf * dpf, axis=0, keepdims=True)
                dst = _unfold(from_prev, pf * (dpf - delta))
                dsk = a["psink"] * delta
                for g, h in enumerate(a["heads"]):
                    tot = jnp.sum(dsk[:, g * BLOCK:(g + 1) * BLOCK], axis=1, keepdims=True)
                    dsink = dsink - jnp.where(lane16 == h, tot, 0.0)
                kt = kvt[kvh * HEAD_DIM:(kvh + 1) * HEAD_DIM, :]
                knt = _bf(kt * lax.rsqrt(jnp.mean(kt * kt, axis=0, keepdims=True) + EPS) * gk_col)
                dqn = (_dot(knt, dst) * (HEAD_DIM ** -0.5))
                band_k[kvh] = _dot_nt(dst, a["qts"])
                band_v[kvh] = _dot_nt(_unfold(from_prev, pf), dot)
                dqn3 = _lanes_to_heads(dqn).reshape(GROUP, HEAD_DIM, BLOCK)
                u = dqn3 * gq_v
                dq3 = a["rq"] * (u - qhat * jnp.mean(u * qhat, axis=1, keepdims=True))
                dgq = dgq + jnp.sum(jnp.sum(dqn3 * qhat, axis=0), axis=1, keepdims=True)
                dq_ref[:, cols] = _bf(dq3.reshape(GROUP * HEAD_DIM, BLOCK).T)
            dsink_ref[...] += dsink
            dgq_ref[...] += dgq

        dgk = jnp.zeros((1, HEAD_DIM), F32)
        for kvh in range(N_KV_HEADS):
            kcols = slice(kvh * HEAD_DIM, (kvh + 1) * HEAD_DIM)
            vcols = slice(ATT_KV + kvh * HEAD_DIM, ATT_KV + (kvh + 1) * HEAD_DIM)
            dkn = carry_k[kvh] + band_k[kvh, 0:BLOCK, :]
            dv = carry_v[kvh] + band_v[kvh, 0:BLOCK, :]
            rk, khat = _rms_stats(kvp_ref[:, kcols])
            dk, dgain = _rms_bwd(dkn, khat, rk, gk_v)
            dgk = dgk + jnp.sum(dgain, axis=0, keepdims=True)
            dkv_ref[:, kcols] = _bf(dk)
            dkv_ref[:, vcols] = _bf(dv)
            carry_k[kvh] = band_k[kvh, BLOCK:2 * BLOCK, :]
            carry_v[kvh] = band_v[kvh, BLOCK:2 * BLOCK, :]
        dgk_ref[...] += dgk

    small = lambda a: pl.BlockSpec(a.shape, lambda n: (0, 0))
    last = nb - 1
    return _pallas(
        kern, grid=(nb + 1,),
        in_specs=[pl.BlockSpec((BLOCK, ATT_Q), lambda n: (jnp.minimum(n, last), 0)),
                  pl.BlockSpec((BLOCK, 2 * ATT_KV), lambda n: (jnp.maximum(n - 1, 0), 0)),
                  pl.BlockSpec((BLOCK, 2 * ATT_KV), lambda n: (jnp.minimum(n, last), 0)),
                  pl.BlockSpec((BLOCK, ATT_Q), lambda n: (jnp.minimum(n, last), 0)),
                  small(gq_col), small(gk), small(gk_col), small(sinks)],
        out_specs=[pl.BlockSpec((BLOCK, ATT_Q), lambda n: (jnp.minimum(n, last), 0)),
                   pl.BlockSpec((BLOCK, 2 * ATT_KV), lambda n: (jnp.maximum(n - 1, 0), 0)),
                   pl.BlockSpec((HEAD_DIM, 1), lambda n: (0, 0)),
                   pl.BlockSpec((1, HEAD_DIM), lambda n: (0, 0)),
                   pl.BlockSpec((1, N_Q_HEADS), lambda n: (0, 0))],
        out_shape=[jax.ShapeDtypeStruct((t, ATT_Q), BF16), jax.ShapeDtypeStruct((t, 2 * ATT_KV), BF16),
                   jax.ShapeDtypeStruct((HEAD_DIM, 1), F32), jax.ShapeDtypeStruct((1, HEAD_DIM), F32),
                   jax.ShapeDtypeStruct((1, N_Q_HEADS), F32)],
        scratch=[pltpu.VMEM((N_KV_HEADS, 2 * BLOCK, HEAD_DIM), F32),
                 pltpu.VMEM((N_KV_HEADS, 2 * BLOCK, HEAD_DIM), F32),
                 pltpu.VMEM((N_KV_HEADS, BLOCK, HEAD_DIM), F32),
                 pltpu.VMEM((N_KV_HEADS, BLOCK, HEAD_DIM), F32)],
        args=[q_a, kv_a, kv_a, d_attn, gq_col, gk, gk_col, sinks], name="attn_bwd", exchange=exchange)


def _ret_tables(t):
    pos = jnp.arange(t, dtype=F32)
    theta = 1.0 / (RET_ROT_BASE ** jnp.linspace(0.0, 1.0, RET_QK_DIM // 2, dtype=F32))
    ang = pos[:, None] * theta[None, :]
    sign = jnp.tile(jnp.array([-1.0, 1.0], F32), RET_QK_DIM // 2)
    pair = (jnp.arange(RET_QK_DIM)[None, :] // 2 == jnp.arange(RET_QK_DIM // 2)[:, None]).astype(F32)
    spread = lambda a, mat: jnp.dot(a, mat, precision=lax.Precision.HIGHEST)
    log_gamma = jnp.log(1.0 - 2.0 ** (-5.0 - jnp.arange(RET_HEADS, dtype=F32)))
    i = jnp.arange(RET_CHUNK, dtype=F32)
    diff = i[:, None] - i[None, :]
    causal = diff >= 0
    decay = jnp.where(causal[None], jnp.exp(jnp.where(causal, diff, 0.0)[None] * log_gamma[:, None, None]), 0.0)
    xi = jnp.exp((i + 1.0)[None, :] * log_gamma[:, None])[:, :, None]
    zeta = jnp.exp((RET_CHUNK - 1.0 - i)[None, :] * log_gamma[:, None])[:, :, None]
    gch = jnp.broadcast_to(jnp.exp(RET_CHUNK * log_gamma)[:, None, None], (RET_HEADS, 1, 128))
    return spread(jnp.cos(ang), pair), spread(jnp.sin(ang), pair * sign[None, :]), decay, xi, zeta, gch


def _swap_pairs(x):
    lane = lax.broadcasted_iota(jnp.int32, x.shape, 1)
    return jnp.where((lane & 1) == 0, pltpu.roll(x, RET_QK_DIM - 1, 1), pltpu.roll(x, 1, 1))


def _rotate(x, cos, sin_s):
    return x * cos + _swap_pairs(x) * sin_s


def _rotate_bwd(dy, cos, sin_s):
    return dy * cos + _swap_pairs(dy * sin_s)


def _ret_specs(order):
    qk = pl.BlockSpec((RET_CHUNK, RET_QK), lambda j: (order(j), 0))
    v = pl.BlockSpec((RET_CHUNK, RET_V), lambda j: (order(j), 0))
    dec = pl.BlockSpec((RET_HEADS, RET_CHUNK, RET_CHUNK), lambda j: (0, 0, 0))
    col = pl.BlockSpec((RET_HEADS, RET_CHUNK, 1), lambda j: (0, 0, 0))
    gch = pl.BlockSpec((RET_HEADS, 1, 128), lambda j: (0, 0, 0))
    st = pl.BlockSpec((RET_HEADS, None, RET_QK_DIM, RET_V_DIM), lambda j: (0, order(j), 0, 0))
    pos = pl.BlockSpec((RET_CHUNK, RET_QK_DIM), lambda j: (order(j), 0))
    return qk, v, dec, col, gch, st, pos


def _ret_fwd(q_r, k_r, v_r, g_r, tables):
    t = q_r.shape[0]
    nc = t // RET_CHUNK
    cos, sin_s, decay, xi, zeta, gch = tables

    def kern(q_ref, k_ref, v_ref, g_ref, cos_ref, sin_ref, dec_ref, xi_ref, zeta_ref, gch_ref,
             o_ref, ret_ref, st_ref, state):
        @pl.when(pl.program_id(0) == 0)
        def _():
            state[...] = jnp.zeros_like(state)

        cos_t = cos_ref[...]
        sin_t = sin_ref[...]
        for h in range(RET_HEADS):
            qc = slice(h * RET_QK_DIM, (h + 1) * RET_QK_DIM)
            vc = slice(h * RET_V_DIM, (h + 1) * RET_V_DIM)
            qs = _bf(_rotate(q_ref[:, qc], cos_t, sin_t))
            ks = _rotate(k_ref[:, qc] * (RET_QK_DIM ** -0.5), cos_t, sin_t)
            vb = v_ref[:, vc]
            s_old = state[h]
            sb = _bf(s_old)
            st_ref[h] = sb
            inner = _dot_nt(qs, _bf(ks)) * dec_ref[h]
            out = _dot(_bf(inner), vb) + _dot(qs, sb) * xi_ref[h]
            state[h] = gch_ref[h, :, 0:1] * s_old + _dot_tn(_bf(ks * zeta_ref[h]), vb)
            o_ref[:, vc] = out
            r, rn = _rms_stats(out)
            g = g_ref[:, vc]
            ret_ref[:, vc] = _bf(g * jax.nn.sigmoid(g) * rn)

    qk, v, dec, col, gsp, st, pos = _ret_specs(lambda j: j)
    return pl.pallas_call(
        kern, grid=(nc,),
        in_specs=[qk, qk, v, v, pos, pos, dec, col, col, gsp],
        out_specs=[v, v, st],
        out_shape=[jax.ShapeDtypeStruct((t, RET_V), F32), jax.ShapeDtypeStruct((t, RET_V), BF16),
                   jax.ShapeDtypeStruct((RET_HEADS, nc, RET_QK_DIM, RET_V_DIM), BF16)],
        scratch_shapes=[pltpu.VMEM((RET_HEADS, RET_QK_DIM, RET_V_DIM), F32)],
        name="ret_fwd", compiler_params=_params(("arbitrary",)),
    )(q_r, k_r, v_r, g_r, cos, sin_s, decay, xi, zeta, gch)


def _ret_bwd(q_r, k_r, v_r, d_o, states, tables, exchange):
    t = q_r.shape[0]
    nc = t // RET_CHUNK
    cos, sin_s, decay, xi, zeta, gch = tables

    def kern(q_ref, k_ref, v_ref, do_ref, st_ref, cos_ref, sin_ref, dec_ref, xi_ref, zeta_ref, gch_ref,
             dq_ref, dk_ref, dv_ref, dstate):
        @pl.when(pl.program_id(0) == 0)
        def _():
            dstate[...] = jnp.zeros_like(dstate)

        cos_t = cos_ref[...]
        sin_t = sin_ref[...]
        scale = RET_QK_DIM ** -0.5
        for h in range(RET_HEADS):
            qc = slice(h * RET_QK_DIM, (h + 1) * RET_QK_DIM)
            vc = slice(h * RET_V_DIM, (h + 1) * RET_V_DIM)
            qs = _bf(_rotate(q_ref[:, qc], cos_t, sin_t))
            ks = _rotate(k_ref[:, qc] * scale, cos_t, sin_t)
            ksb = _bf(ks)
            vb = v_ref[:, vc]
            d_o_t = do_ref[:, vc]
            dob = _bf(d_o_t)
            doxb = _bf(d_o_t * xi_ref[h])
            dec = dec_ref[h]
            ds_old = dstate[h]
            dsb = _bf(ds_old)
            pb = _bf(_dot_nt(qs, ksb) * dec)
            dpb = _bf(_dot_nt(dob, vb) * dec)
            dqs = _dot(dpb, ksb) + _dot_nt(doxb, st_ref[h])
            dks = _dot_tn(dpb, qs) + _dot_nt(vb, dsb) * zeta_ref[h]
            dv_ref[:, vc] = _bf(_dot_tn(pb, dob) + _dot(_bf(ks * zeta_ref[h]), dsb))
            dstate[h] = gch_ref[h, :, 0:1] * ds_old + _dot_tn(qs, doxb)
            dq_ref[:, qc] = _bf(_rotate_bwd(dqs, cos_t, sin_t))
            dk_ref[:, qc] = _bf(_rotate_bwd(dks, cos_t, sin_t) * scale)

    qk, v, dec, col, gsp, st, pos = _ret_specs(lambda j: nc - 1 - j)
    return _pallas(
        kern, grid=(nc,),
        in_specs=[qk, qk, v, v, st, pos, pos, dec, col, col, gsp],
        out_specs=[qk, qk, v],
        out_shape=[jax.ShapeDtypeStruct((t, RET_QK), BF16), jax.ShapeDtypeStruct((t, RET_QK), BF16),
                   jax.ShapeDtypeStruct((t, RET_V), BF16)],
        scratch=[pltpu.VMEM((RET_HEADS, RET_QK_DIM, RET_V_DIM), F32)],
        args=[q_r, k_r, v_r, d_o, states, cos, sin_s, decay, xi, zeta, gch], name="ret_bwd", exchange=exchange)


def _position():
    return lax.axis_index("x"), lax.axis_index("y"), lax.axis_index("c")


def _gather_exchange(owns, forward_at):
    n = len(owns)

    def copies(ins, outs, send_sems, recv_sems, base):
        x, y, c = _position()
        sibling = (x, y, 1 - c)
        chips = [(1 - x, y), (x, 1 - y), (1 - x, 1 - y)]
        my_chip = 2 * x + y

        def slab(a, chip, hf):
            half = owns[a].shape[0] // 2
            return outs[a].at[chip, pl.ds(hf * half, half), :]

        def copy(k, src, dst, to):
            return pltpu.make_async_remote_copy(src_ref=src, dst_ref=dst, send_sem=send_sems.at[base + k],
                                                recv_sem=recv_sems.at[base + k], device_id=to, device_id_type=MESH)

        first, passed, from_sibling = [], [], []
        for a in range(n):
            half = owns[a].shape[0] // 2
            for k, (cx, cy) in enumerate(chips):
                first.append(copy(6 * a + k, ins[a].at[pl.ds(c * half, half), :], slab(a, my_chip, c), (cx, cy, c)))
                landed = slab(a, 2 * cx + cy, c)
                passed.append(copy(6 * a + 3 + k, landed, landed, sibling))
                theirs = slab(a, 2 * cx + cy, 1 - c)
                from_sibling.append(copy(6 * a + 3 + k, theirs, theirs, sibling))
        return first, passed, from_sibling

    def start(*args):
        first, _, _ = copies(*args)
        for cp in first:
            cp.start()

    def forward(*args):
        first, passed, _ = copies(*args)
        for arrived, cp in zip(first, passed):
            arrived.wait_recv()
            cp.start()

    def finish(*args):
        first, passed, from_sibling = copies(*args)
        for cp in from_sibling:
            cp.wait_recv()
        for cp in first + passed:
            cp.wait_send()

    outs = [jax.ShapeDtypeStruct((N_CHIPS, *a.shape), a.dtype) for a in owns]
    return _Exchange(owns, outs, 6 * n, [(0.0, start), (forward_at, forward), (1.0, finish)])


def _symmetric_exchange(ins, outs, plan):
    n_sems = len(plan([None] * len(ins), [None] * len(outs), 0, 0, 0, dry=True))

    def copies(in_refs, out_refs, send_sems, recv_sems, base):
        x, y, c = _position()
        return [pltpu.make_async_remote_copy(src_ref=src, dst_ref=dst, send_sem=send_sems.at[base + k],
                                             recv_sem=recv_sems.at[base + k], device_id=dev, device_id_type=MESH)
                for k, (src, dst, dev) in enumerate(plan(in_refs, out_refs, x, y, c, dry=False))]

    def start(*args):
        for cp in copies(*args):
            cp.start()

    def finish(*args):
        for cp in copies(*args):
            cp.wait()

    return _Exchange(ins, outs, n_sems, [(0.0, start), (1.0, finish)])


def _pair_exchange(gs):
    def plan(in_refs, out_refs, x, y, c, dry):
        out = []
        for a, g in enumerate(gs):
            half = g.shape[1] // 2
            for k in range(N_CHIPS):
                out.append(None if dry else (in_refs[a].at[k, pl.ds((1 - c) * half, half), :], out_refs[a].at[k],
                                             (x, y, 1 - c)))
        return out

    outs = [jax.ShapeDtypeStruct((g.shape[0], g.shape[1] // 2, g.shape[2]), g.dtype) for g in gs]
    return _symmetric_exchange(gs, outs, plan)


def _pair_sum(g, from_sibling, c_arr, *, tile, name):
    n, rows, width = g.shape
    tiles = (rows // 2) // tile

    def kern(c_ref, g_ref, s_ref, o_ref):
        o_ref[...] = _bf(g_ref[...] + s_ref[...])

    return pl.pallas_call(
        kern,
        grid_spec=pltpu.PrefetchScalarGridSpec(
            num_scalar_prefetch=1, grid=(n, tiles),
            in_specs=[pl.BlockSpec((None, tile, width), lambda k, i, c: (k, c[0] * tiles + i, 0)),
                      pl.BlockSpec((None, tile, width), lambda k, i, c: (k, i, 0))],
            out_specs=pl.BlockSpec((None, tile, width), lambda k, i, c: (k, i, 0))),
        out_shape=jax.ShapeDtypeStruct((n, rows // 2, width), BF16), name=name,
        compiler_params=_params(("parallel", "parallel")),
    )(c_arr, g, from_sibling)


def _scatter_to_owners(hsums):
    def plan(in_refs, out_refs, x, y, c, dry):
        out = []
        for a in range(len(hsums)):
            for k, (cx, cy) in enumerate([(1 - x, y), (x, 1 - y), (1 - x, 1 - y)]):
                out.append(None if dry else (in_refs[a].at[2 * cx + cy], out_refs[a].at[k], (cx, cy, c)))
        return out

    outs = [jax.ShapeDtypeStruct((3, *h.shape[1:]), h.dtype) for h in hsums]
    return _symmetric_exchange(hsums, outs, plan)


def _sum_chips(hsum, parts, chip_arr, *, tile, name):
    n, half, width = parts.shape

    def kern(chip_ref, h_ref, p_ref, o_ref):
        acc = h_ref[...].astype(F32)
        for k in range(n):
            acc = acc + p_ref[k].astype(F32)
        o_ref[...] = acc

    return pl.pallas_call(
        kern,
        grid_spec=pltpu.PrefetchScalarGridSpec(
            num_scalar_prefetch=1, grid=(half // tile,),
            in_specs=[pl.BlockSpec((None, tile, width), lambda i, chip: (chip[0], i, 0)),
                      pl.BlockSpec((n, tile, width), lambda i, chip: (0, i, 0))],
            out_specs=pl.BlockSpec((tile, width), lambda i, chip: (i, 0))),
        out_shape=jax.ShapeDtypeStruct((half, width), F32), name=name,
        compiler_params=_params(("parallel",)),
    )(chip_arr, hsum, parts)


def _share_halves(fhalves):
    def plan(in_refs, out_refs, x, y, c, dry):
        return [None if dry else (in_refs[a], out_refs[a], (x, y, 1 - c)) for a in range(len(fhalves))]

    return _symmetric_exchange(fhalves, [jax.ShapeDtypeStruct(f.shape, f.dtype) for f in fhalves], plan)


def _adamw_math(w, g, m, v):
    m = ADAM_B1 * m + (1.0 - ADAM_B1) * g
    v = ADAM_B2 * v + (1.0 - ADAM_B2) * (g * g)
    m_hat = m / (1.0 - ADAM_B1 ** ADAM_STEP)
    v_hat = v / (1.0 - ADAM_B2 ** ADAM_STEP)
    delta = -ADAM_LR * (m_hat / (jnp.sqrt(v_hat) + ADAM_EPS) + ADAM_WD * w)
    return delta, m, v


def _adamw(w, m, v, g_mine, g_other, c_arr, *, row_off, tile, name):
    rows, width = w.shape
    tiles_per_half = g_mine.shape[0] // tile
    first = row_off // tile

    def kern(c_ref, w_ref, gm_ref, go_ref, m_ref, v_ref, g_ref, d_ref, nm_ref, nv_ref):
        in_my_half = ((first + pl.program_id(0)) // tiles_per_half) == c_ref[0]
        g = jnp.where(in_my_half, gm_ref[...], go_ref[...])
        g_ref[...] = g
        d_ref[...], nm_ref[...], nv_ref[...] = _adamw_math(w_ref[...], g, m_ref[...], v_ref[...])

    full = pl.BlockSpec((tile, width), lambda i, c: (i, 0))
    half = pl.BlockSpec((tile, width), lambda i, c: ((first + i) % tiles_per_half, 0))
    return pl.pallas_call(
        kern,
        grid_spec=pltpu.PrefetchScalarGridSpec(
            num_scalar_prefetch=1, grid=(rows // tile,), in_specs=[full, half, half, full, full],
            out_specs=[full] * 4),
        out_shape=[jax.ShapeDtypeStruct((rows, width), F32)] * 4, name=name,
        compiler_params=_params(("parallel",)),
    )(c_arr, w, g_mine, g_other, m, v)


def _small_step(part, tile_sums, w, m, v):
    loss_p, dg1_p, dg2_p = tile_sums

    def body(part_ref, loss_ref, dg1_ref, dg2_ref, w_ref, m_ref, v_ref, g_ref, d_ref, nm_ref, nv_ref,
             mine, gathered, send_sems, recv_sems):
        x, y, c = _position()
        me = 4 * x + 2 * y + c
        mine[...] = part_ref[...]
        for r in range(8):
            lanes = slice(128 * r, 128 * (r + 1))
            mine[SM_G1 + r:SM_G1 + r + 1, :] = jnp.sum(dg1_ref[:, lanes], axis=0, keepdims=True)
            mine[SM_G2 + r:SM_G2 + r + 1, :] = jnp.sum(dg2_ref[:, lanes], axis=0, keepdims=True)
        mine[SM_LOSS:SM_LOSS + 1, :] = jnp.sum(loss_ref[...], axis=0, keepdims=True)
        copies = []
        for k in range(1, N_DEV):
            flip = (k >> 2) & 1, (k >> 1) & 1, k & 1
            to = (x ^ flip[0], y ^ flip[1], c ^ flip[2])
            cp = pltpu.make_async_remote_copy(
                src_ref=mine, dst_ref=gathered.at[me], send_sem=send_sems.at[k - 1], recv_sem=recv_sems.at[k - 1],
                device_id=to, device_id_type=MESH)
            cp.start()
            copies.append(cp)
        gathered[me] = mine[...]
        for k in range(1, N_DEV):
            flip = (k >> 2) & 1, (k >> 1) & 1, k & 1
            src = 4 * (x ^ flip[0]) + 2 * (y ^ flip[1]) + (c ^ flip[2])
            pltpu.make_async_remote_copy(
                src_ref=mine, dst_ref=gathered.at[src], send_sem=send_sems.at[k - 1], recv_sem=recv_sems.at[k - 1],
                device_id=(x, y, c), device_id_type=MESH).wait_recv()
        for cp in copies:
            cp.wait_send()
        total = gathered[0]
        for k in range(1, N_DEV):
            total = total + gathered[k]
        g_ref[...] = total
        d_ref[...], nm_ref[...], nv_ref[...] = _adamw_math(w_ref[...], total, m_ref[...], v_ref[...])

    vm = pl.BlockSpec(memory_space=pltpu.VMEM)
    blk = jax.ShapeDtypeStruct((SMALL_ROWS, 128), F32)
    return pl.pallas_call(
        body, in_specs=[vm] * 7, out_specs=[vm] * 4, out_shape=[blk] * 4,
        scratch_shapes=[pltpu.VMEM((SMALL_ROWS, 128), F32), pltpu.VMEM((N_DEV, SMALL_ROWS, 128), F32),
                        pltpu.SemaphoreType.DMA((N_DEV - 1,)), pltpu.SemaphoreType.DMA((N_DEV - 1,))],
        name="small_step",
    )(part, loss_p.reshape(-1, 128), dg1_p.reshape(-1, D_MODEL), dg2_p.reshape(-1, D_MODEL), w, m, v)


def _with_own(gathered, own, my_chip):
    return lax.dynamic_update_slice(gathered, own[None], (my_chip, 0, 0))


def _pack_small(g1, g2, gq, gk, sinks):
    blk = jnp.zeros((SMALL_ROWS, 128), F32)
    blk = blk.at[SM_G1:SM_G1 + 8].set(g1.reshape(8, 128))
    blk = blk.at[SM_G2:SM_G2 + 8].set(g2.reshape(8, 128))
    blk = blk.at[SM_GQ, :HEAD_DIM].set(gq.reshape(-1))
    blk = blk.at[SM_GK, :HEAD_DIM].set(gk.reshape(-1))
    blk = blk.at[SM_SINK, :N_Q_HEADS].set(sinks.reshape(-1))
    return blk


def _unpack_small(blk):
    return (blk[SM_G1:SM_G1 + 8].reshape(1, D_MODEL), blk[SM_G2:SM_G2 + 8].reshape(1, D_MODEL),
            blk[SM_GQ, :HEAD_DIM].reshape(1, HEAD_DIM), blk[SM_GK, :HEAD_DIM].reshape(1, HEAD_DIM),
            blk[SM_SINK, :N_Q_HEADS].reshape(1, N_Q_HEADS))


def kernel(x, norm_mix_gain, w_in, q_norm_gain, k_norm_gain, attn_sinks, w_branch_attn, w_branch_ret, w_out, norm_ffn_gain, w_ffn_gate, w_ffn_up, w_ffn_down, loss_target, m_norm_mix_gain, m_w_in, m_q_norm_gain, m_k_norm_gain, m_attn_sinks, m_w_branch_attn, m_w_branch_ret, m_w_out, m_norm_ffn_gain, m_w_ffn_gate, m_w_ffn_up, m_w_ffn_down, v_norm_mix_gain, v_w_in, v_q_norm_gain, v_k_norm_gain, v_attn_sinks, v_w_branch_attn, v_w_branch_ret, v_w_out, v_norm_ffn_gain, v_w_ffn_gate, v_w_ffn_up, v_w_ffn_down):
    my_chip = 2 * lax.axis_index("x") + lax.axis_index("y")
    c_arr = lax.axis_index("c").astype(jnp.int32).reshape(1)
    chip_arr = my_chip.astype(jnp.int32).reshape(1)
    x_t, target = x[0], loss_target[0]
    g1, g2, gq, gk, sinks = norm_mix_gain, norm_ffn_gain, q_norm_gain, k_norm_gain, attn_sinks
    tables = _ret_tables(x_t.shape[0])

    tr = lambda a: jnp.transpose(a[0])
    own_w_in = _bf(tr(w_in))
    own_ffn = _bf(jnp.concatenate([tr(w_ffn_gate), tr(w_ffn_up), w_ffn_down[0]], axis=0))
    own_mix = _bf(jnp.concatenate([w_branch_attn[0], w_branch_ret[0], w_out[0]], axis=0))
    got_w_in, = _run_exchange(_gather_exchange([own_w_in], 0.0), "gather_w_in")
    w_in_t = _with_own(got_w_in, own_w_in, my_chip).reshape(D_IN, D_MODEL)
    (h1, q_a, kv_a, q_r, k_r, v_r, g_r, z_a, z_r, got_ffn, got_mix) = _proj_fwd(
        x_t, g1, w_in_t, _gather_exchange([own_ffn, own_mix], 0.8))
    all_ffn = _with_own(got_ffn, own_ffn, my_chip)
    all_mix = _with_own(got_mix, own_mix, my_chip)
    wg_t = all_ffn[:, 0:FF_SH].reshape(D_FF, D_MODEL)
    wu_t = all_ffn[:, FF_SH:2 * FF_SH].reshape(D_FF, D_MODEL)
    wd = all_ffn[:, 2 * FF_SH:3 * FF_SH].reshape(D_FF, D_MODEL)
    wba = all_mix[:, 0:256].reshape(ATT_Q, D_MODEL)
    wbr = all_mix[:, 256:768].reshape(RET_V, D_MODEL)
    wout = all_mix[:, 768:1024].reshape(D_MODEL, D_MODEL)

    gq_col, gk_col = gq.reshape(HEAD_DIM, 1), gk.reshape(HEAD_DIM, 1)
    attn = _attn_fwd(q_a, kv_a, gq_col, gk, sinks)
    o_ret, ret, states = _ret_fwd(q_r, k_r, v_r, g_r, tables)
    ba, br, merged, x1, h2 = _mix_fwd(attn, ret, z_a, z_r, x_t, wba, wbr, wout, g2)
    act, dgate, dup, dyb, dx1, dx1b, loss_p, dg2_p = _ffn_fwd_bwd(h2, x1, target, wg_t, wu_t, wd, g2)

    def pairs(row0, rows):
        return lambda i: [(h * rows, rows, (2 * i + h, pl.ds(row0, rows), slice(None))) for h in range(2)]

    f_block = jax.ShapeDtypeStruct((N_CHIPS, 3 * FF_SH, D_MODEL), F32)
    f_block, = _dw(dgate, h2, tm=2 * FF_SH, place=pairs(0, FF_SH), buf=f_block, name="dw_gate")
    f_block, = _dw(dup, h2, tm=2 * FF_SH, place=pairs(FF_SH, FF_SH), buf=f_block, name="dw_up")
    f_block, = _dw(act, dyb, tm=2 * FF_SH, place=pairs(2 * FF_SH, FF_SH), buf=f_block, name="dw_down")
    (dba, dbr, dz_a, dz_r, d_attn, d_o, dg_r, sib_ffn) = _mix_bwd(
        dx1b, z_a, z_r, ba, br, g_r, o_ret, wout, wba, wbr, _pair_exchange([f_block]))
    f_sum = _pair_sum(f_block, sib_ffn, c_arr, tile=528, name="pair_sum_ffn")

    def quarters(row0, rows):
        return lambda i: [(k * rows, rows, (k, pl.ds(row0, rows), slice(None))) for k in range(N_CHIPS)]

    m_block = jax.ShapeDtypeStruct((N_CHIPS, D_MODEL, D_MODEL), F32)
    m_block, = _dw(attn, dba, tm=ATT_Q, place=quarters(0, 256), buf=m_block, name="dw_ba")
    m_block, = _dw(ret, dbr, tm=D_MODEL, place=pairs(256, 512), buf=m_block, name="dw_br")
    m_block, = _dw(merged, dx1b, tm=D_MODEL, place=quarters(768, 256), buf=m_block, name="dw_out")

    def w_in_rows(group):
        off, w = group
        tm = min(w, D_MODEL)
        return dict(tm=tm, place=lambda i: [(0, tm, (pl.ds(off + i * tm, tm), slice(None)))])

    w_block = jax.ShapeDtypeStruct((D_IN, D_MODEL), F32)
    w_block, sib_mix = _dw(dg_r, h1, buf=w_block, name="dw_in_5", exchange=_pair_exchange([m_block]), **w_in_rows(P_GR))
    w_block, = _dw(dz_a, h1, buf=w_block, name="dw_in_6", **w_in_rows(P_ZA))
    w_block, = _dw(dz_r, h1, buf=w_block, name="dw_in_7", **w_in_rows(P_ZR))
    m_sum = _pair_sum(m_block, sib_mix, c_arr, tile=256, name="pair_sum_mix")

    dq_r, dk_r, dv_r, got_ffn_sums = _ret_bwd(q_r, k_r, v_r, d_o, states, tables, _scatter_to_owners([f_sum]))
    ffn_half = _sum_chips(f_sum, got_ffn_sums, chip_arr, tile=528, name="sum_chips_ffn")
    w_block, = _dw(dq_r, h1, buf=w_block, name="dw_in_2", **w_in_rows(P_QR))
    w_block, = _dw(dk_r, h1, buf=w_block, name="dw_in_3", **w_in_rows(P_KR))
    w_block, = _dw(dv_r, h1, buf=w_block, name="dw_in_4", **w_in_rows(P_VR))

    (dq_a, dkv_a, dgq, dgk, dsinks, got_mix_sums, ffn_other) = _attn_bwd(
        q_a, kv_a, d_attn, gq_col, gk, gk_col, sinks,
        _merge_exchanges(_scatter_to_owners([m_sum]), _share_halves([ffn_half])))
    dgq = dgq.reshape(1, HEAD_DIM)
    mix_half = _sum_chips(m_sum, got_mix_sums, chip_arr, tile=256, name="sum_chips_mix")
    w_block, mix_other = _dw(dq_a, h1, buf=w_block, name="dw_in_0", exchange=_share_halves([mix_half]),
                             **w_in_rows(P_QA))
    w_block, = _dw(dkv_a, h1, buf=w_block, name="dw_in_1", **w_in_rows(P_KVA))

    w_block = w_block.reshape(N_CHIPS, W_IN_SH, D_MODEL)
    sib_w_in, = _run_exchange(_pair_exchange([w_block]), "pair_exchange_w_in")
    w_sum = _pair_sum(w_block, sib_w_in, c_arr, tile=592, name="pair_sum_w_in")
    d_pieces = [dq_a, dkv_a, dq_r, dk_r, dv_r, dg_r, dz_a, dz_r]
    grad_x, dg1_p, got_w_in_sums = _proj_bwd(d_pieces, x_t, dx1, w_in_t, g1, _scatter_to_owners([w_sum]))
    w_in_half = _sum_chips(w_sum, got_w_in_sums, chip_arr, tile=592, name="sum_chips_w_in")
    w_in_other, = _run_exchange(_share_halves([w_in_half]), "share_halves_w_in")

    def update(name, g_half, g_other, row_off, tile, wmv, transposed=False):
        w, m, v = [tr(a) if transposed else a[0] for a in wmv]
        outs = _adamw(w, m, v, g_half, g_other, c_arr, row_off=row_off, tile=tile, name=f"adamw_{name}")
        return [jnp.transpose(o) if transposed else o for o in outs]

    big = dict(
        w_in=update("w_in", w_in_half, w_in_other, 0, 592, (w_in, m_w_in, v_w_in), True),
        wg=update("wg", ffn_half, ffn_other, 0, 352, (w_ffn_gate, m_w_ffn_gate, v_w_ffn_gate), True),
        wu=update("wu", ffn_half, ffn_other, FF_SH, 352, (w_ffn_up, m_w_ffn_up, v_w_ffn_up), True),
        wd=update("wd", ffn_half, ffn_other, 2 * FF_SH, 352, (w_ffn_down, m_w_ffn_down, v_w_ffn_down)),
        wba=update("wba", mix_half, mix_other, 0, 256, (w_branch_attn, m_w_branch_attn, v_w_branch_attn)),
        wbr=update("wbr", mix_half, mix_other, 256, 256, (w_branch_ret, m_w_branch_ret, v_w_branch_ret)),
        wout=update("wout", mix_half, mix_other, 768, 256, (w_out, m_w_out, v_w_out)))

    zeros = jnp.zeros((1, D_MODEL), F32)
    part = _pack_small(zeros, zeros, dgq, dgk, dsinks)
    sm_w = _pack_small(norm_mix_gain, norm_ffn_gain, q_norm_gain, k_norm_gain, attn_sinks)
    sm_m = _pack_small(m_norm_mix_gain, m_norm_ffn_gain, m_q_norm_gain, m_k_norm_gain, m_attn_sinks)
    sm_v = _pack_small(v_norm_mix_gain, v_norm_ffn_gain, v_q_norm_gain, v_k_norm_gain, v_attn_sinks)
    sm_g, sm_d, sm_nm, sm_nv = _small_step(part, (loss_p, dg1_p, dg2_p), sm_w, sm_m, sm_v)
    loss = sm_g[SM_LOSS, 0]

    def leaves(i, sm):
        b = [big[n][i][None] for n in ("w_in", "wba", "wbr", "wout", "wg", "wu", "wd")]
        s1, s2, sq, sk, ss = _unpack_small(sm)
        return [s1, b[0], sq, sk, ss, b[1], b[2], b[3], s2, b[4], b[5], b[6]]

    return (loss, grad_x[None], *leaves(0, sm_g), *leaves(1, sm_d), *leaves(2, sm_nm), *leaves(3, sm_nv))
```

```python
import jax
import jax.numpy as jnp
from jax import lax
from jax.experimental import pallas as pl
from jax.experimental.pallas import tpu as pltpu

F32 = jnp.float32
BF16 = jnp.bfloat16
MESH = pl.DeviceIdType.MESH

D_MODEL = 1024
EPS = 1e-6
HEAD_DIM = 64
N_Q_HEADS = 16
N_KV_HEADS = 2
GROUP = 8
BLOCK = 128
RET_HEADS = 4
RET_QK_DIM = 256
RET_V_DIM = 512
RET_CHUNK = 128
RET_ROT_BASE = 10000.0
D_FF = 2816
ATT_Q = N_Q_HEADS * HEAD_DIM
ATT_KV = N_KV_HEADS * HEAD_DIM
RET_QK = RET_HEADS * RET_QK_DIM
RET_V = RET_HEADS * RET_V_DIM
D_IN = 9472
ADAM_LR = 0.001
ADAM_B1 = 0.9
ADAM_B2 = 0.999
ADAM_EPS = 1e-08
ADAM_WD = 0.01
ADAM_STEP = 10

N_CHIPS = 4
N_DEV = 8
VMEM_LIMIT_BYTES = 60 * 1024 * 1024

P_QA = (0, 1024)
P_KVA = (1024, 256)
P_QR = (1280, 1024)
P_KR = (2304, 1024)
P_VR = (3328, 2048)
P_GR = (5376, 2048)
P_ZA = (7424, 1024)
P_ZR = (8448, 1024)

W_IN_SH = D_IN // N_CHIPS
FF_SH = D_FF // N_CHIPS

SMALL_ROWS = 24
SM_G1, SM_G2, SM_GQ, SM_GK, SM_SINK, SM_LOSS = 0, 8, 16, 17, 18, 19


def _dot(a, b):
    return jnp.dot(a, b, preferred_element_type=F32)


def _dot_nt(a, b):
    return lax.dot_general(a, b, (((1,), (1,)), ((), ())), preferred_element_type=F32)


def _dot_tn(a, b):
    return lax.dot_general(a, b, (((0,), (0,)), ((), ())), preferred_element_type=F32)


def _bf(x):
    return x.astype(BF16)


def _rms_stats(x):
    r = lax.rsqrt(jnp.mean(x * x, axis=-1, keepdims=True) + EPS)
    return r, x * r


def _rms_bwd(dy, xhat, r, gain):
    u = dy * gain
    dx = r * (u - xhat * jnp.mean(u * xhat, axis=-1, keepdims=True))
    return dx, dy * xhat


def _params(sem):
    return pltpu.CompilerParams(dimension_semantics=sem, vmem_limit_bytes=VMEM_LIMIT_BYTES)


_ANY = pl.BlockSpec(memory_space=pl.ANY)


class _Exchange:
    def __init__(self, ins, outs, n_sems, phases):
        self.ins, self.outs, self.n_sems, self.phases = list(ins), list(outs), n_sems, list(phases)


def _merge_exchanges(a, b):
    na_i, na_o, shift = len(a.ins), len(a.outs), a.n_sems

    def first(fn):
        return lambda i, o, s, r, base: fn(i[:na_i], o[:na_o], s, r, base)

    def second(fn):
        return lambda i, o, s, r, base: fn(i[na_i:], o[na_o:], s, r, base + shift)

    phases = [(f, first(fn)) for f, fn in a.phases] + [(f, second(fn)) for f, fn in b.phases]
    return _Exchange(a.ins + b.ins, a.outs + b.outs, a.n_sems + b.n_sems, sorted(phases, key=lambda p: p[0]))


def _pallas(kern, *, grid, in_specs, out_specs, out_shape, args, name, scratch=(), exchange=None, aliases=None):
    aliases = aliases or {}
    if exchange is None:
        return pl.pallas_call(
            kern, grid=grid, in_specs=in_specs, out_specs=out_specs, out_shape=out_shape, name=name,
            scratch_shapes=list(scratch), input_output_aliases=aliases,
            compiler_params=_params(("arbitrary",) * len(grid)))(*args)
    n_in, n_out, n_sc = len(in_specs), len(out_specs), len(scratch)
    n_xi, n_xo = len(exchange.ins), len(exchange.outs)
    n_steps = 1
    for g in grid:
        n_steps *= g

    def wrapped(*refs):
        ins, refs = refs[:n_in], refs[n_in:]
        x_ins, refs = refs[:n_xi], refs[n_xi:]
        outs, refs = refs[:n_out], refs[n_out:]
        x_outs, refs = refs[:n_xo], refs[n_xo:]
        scr, (send_sems, recv_sems) = refs[:n_sc], refs[n_sc:]
        step = pl.program_id(0)
        for d in range(1, len(grid)):
            step = step * grid[d] + pl.program_id(d)
        for frac, fn in exchange.phases:
            at = min(int(frac * n_steps), n_steps - 1)

            @pl.when(step == at)
            def _(fn=fn):
                fn(x_ins, x_outs, send_sems, recv_sems, 0)

        kern(*ins, *outs, *scr)

    sems = [pltpu.SemaphoreType.DMA((exchange.n_sems,)), pltpu.SemaphoreType.DMA((exchange.n_sems,))]
    return pl.pallas_call(
        wrapped, grid=grid, in_specs=list(in_specs) + [_ANY] * n_xi, out_specs=list(out_specs) + [_ANY] * n_xo,
        out_shape=list(out_shape) + exchange.outs, name=name, scratch_shapes=list(scratch) + sems,
        input_output_aliases=aliases, compiler_params=_params(("arbitrary",) * len(grid)))(*args, *exchange.ins)


def _run_exchange(exchange, name):
    def body(*refs):
        n_i, n_o = len(exchange.ins), len(exchange.outs)
        for _, fn in exchange.phases:
            fn(refs[:n_i], refs[n_i:n_i + n_o], refs[n_i + n_o], refs[n_i + n_o + 1], 0)

    sems = [pltpu.SemaphoreType.DMA((exchange.n_sems,)), pltpu.SemaphoreType.DMA((exchange.n_sems,))]
    return pl.pallas_call(body, in_specs=[_ANY] * len(exchange.ins), out_specs=[_ANY] * len(exchange.outs),
                          out_shape=exchange.outs, scratch_shapes=sems, name=name)(*exchange.ins)


def _fused(parts, *, grid, name, exchange=None):
    counts = [(len(p["in_specs"]), len(p["out_specs"]), len(p["scratch"])) for p in parts]
    n_in, n_out = sum(c[0] for c in counts), sum(c[1] for c in counts)

    def kern(*refs):
        ins, outs, scr = refs[:n_in], refs[n_in:n_in + n_out], refs[n_in + n_out:]
        i0 = o0 = s0 = 0
        for p, (ni, no, ns) in zip(parts, counts):
            p["kern"](*ins[i0:i0 + ni], *outs[o0:o0 + no], *scr[s0:s0 + ns])
            i0, o0, s0 = i0 + ni, o0 + no, s0 + ns

    cat = lambda key: [a for p in parts for a in p[key]]
    return _pallas(kern, grid=grid, in_specs=cat("in_specs"), out_specs=cat("out_specs"), out_shape=cat("out_shape"),
                   scratch=cat("scratch"), args=cat("args"), name=name, exchange=exchange)


def _row_call(body, *, tm, row_ins, res_ins, row_outs, part_outs=(), name, exchange=None):
    t = row_ins[0].shape[0]
    n_tiles = t // tm
    in_specs = [pl.BlockSpec((tm, a.shape[1]), lambda i: (i, 0)) for a in row_ins]
    in_specs += [pl.BlockSpec(a.shape, lambda i: (0, 0), pipeline_mode=pl.Buffered(1)) for a in res_ins]
    out_shape = [jax.ShapeDtypeStruct((t, w), dt) for (w, dt) in row_outs]
    out_shape += [jax.ShapeDtypeStruct((n_tiles, 1, w), F32) for w in part_outs]
    out_specs = [pl.BlockSpec((tm, w), lambda i: (i, 0)) for (w, _) in row_outs]
    out_specs += [pl.BlockSpec((1, 1, w), lambda i: (i, 0, 0)) for w in part_outs]
    n_ri, n_re, n_ro = len(row_ins), len(res_ins), len(row_outs)

    def kern(*refs):
        body(refs[:n_ri], refs[n_ri:n_ri + n_re], refs[n_ri + n_re:n_ri + n_re + n_ro], refs[n_ri + n_re + n_ro:])

    return _pallas(kern, grid=(n_tiles,), in_specs=in_specs, out_specs=out_specs, out_shape=out_shape,
                   args=[*row_ins, *res_ins], name=name, exchange=exchange)


def _proj_fwd(x, g1, w_in_t, exchange):
    pieces = ((P_QA, F32), (P_KVA, F32), (P_QR, F32), (P_KR, F32), (P_VR, BF16), (P_GR, F32), (P_ZA, F32), (P_ZR, F32))

    def body(ri, re, ro, po):
        x_t = ri[0][...]
        r, xhat = _rms_stats(x_t)
        hb = _bf(xhat * re[0][...])
        ro[0][...] = hb
        for k, ((off, w), dt) in enumerate(pieces):
            ro[1 + k][...] = _dot_nt(hb, re[1][off:off + w, :]).astype(dt)

    outs = [(D_MODEL, BF16)] + [(w, dt) for ((_, w), dt) in pieces]
    return _row_call(body, tm=256, row_ins=[x], res_ins=[g1, w_in_t], row_outs=outs, name="proj_fwd",
                     exchange=exchange)


def _mix_fwd(attn, ret, z_a, z_r, x, wba, wbr, wout, g2):
    def body(ri, re, ro, po):
        ba = _dot(ri[0][...], re[0][...])
        br = _dot(ri[1][...], re[1][...])
        m = jax.nn.sigmoid(ri[2][...]) * ba + jax.nn.sigmoid(ri[3][...]) * br
        mb = _bf(m)
        x1 = ri[4][...] + _dot(mb, re[2][...])
        r, xhat = _rms_stats(x1)
        ro[0][...] = ba
        ro[1][...] = br
        ro[2][...] = mb
        ro[3][...] = x1
        ro[4][...] = _bf(xhat * re[3][...])

    outs = [(D_MODEL, F32), (D_MODEL, F32), (D_MODEL, BF16), (D_MODEL, F32), (D_MODEL, BF16)]
    return _row_call(body, tm=512, row_ins=[attn, ret, z_a, z_r, x], res_ins=[wba, wbr, wout, g2], row_outs=outs,
                     name="mix_fwd")


def _ffn_fwd_bwd(h2, x1, target, wg_t, wu_t, wd, g2):
    def body(ri, re, ro, po):
        h2_t = ri[0][...]
        x1_t = ri[1][...]
        gate = _dot_nt(h2_t, re[0][...])
        up = _dot_nt(h2_t, re[1][...])
        sg = jax.nn.sigmoid(gate)
        sl = gate * sg
        actb = _bf(sl * up)
        ro[0][...] = actb
        y = x1_t + _dot(actb, re[2][...])
        e = y - ri[2][...]
        po[0][0] = jnp.broadcast_to(0.5 * jnp.sum(jnp.sum(e * e, axis=1, keepdims=True), axis=0, keepdims=True)
                                    * (1.0 / D_MODEL), (1, 128))
        dy = e * (1.0 / D_MODEL)
        dyb = _bf(dy)
        ro[3][...] = dyb
        dact = _dot_nt(dyb, re[2][...])
        dupb = _bf(dact * sl)
        dgateb = _bf(dact * up * (sg * (1.0 + gate * (1.0 - sg))))
        ro[1][...] = dgateb
        ro[2][...] = dupb
        dh2 = _dot(dgateb, re[0][...]) + _dot(dupb, re[1][...])
        r, xhat = _rms_stats(x1_t)
        dxn, dgain = _rms_bwd(dh2, xhat, r, re[3][...])
        dx1 = dy + dxn
        ro[4][...] = dx1
        ro[5][...] = _bf(dx1)
        po[1][0] = jnp.sum(dgain, axis=0, keepdims=True)

    outs = [(D_FF, BF16), (D_FF, BF16), (D_FF, BF16), (D_MODEL, BF16), (D_MODEL, F32), (D_MODEL, BF16)]
    return _row_call(body, tm=256, row_ins=[h2, x1, target], res_ins=[wg_t, wu_t, wd, g2], row_outs=outs,
                     part_outs=(128, D_MODEL), name="ffn_fwd_bwd")


def _mix_bwd(dx1b, z_a, z_r, ba, br, g_r, o_ret, wout, wba, wbr, exchange):
    def body(ri, re, ro, po):
        dm = _dot_nt(ri[0][...], re[0][...])
        sa = jax.nn.sigmoid(ri[1][...])
        sr = jax.nn.sigmoid(ri[2][...])
        dbab = _bf(sa * dm)
        dbrb = _bf(sr * dm)
        ro[0][...] = dbab
        ro[1][...] = dbrb
        ro[2][...] = _bf(dm * ri[3][...] * (sa * (1.0 - sa)))
        ro[3][...] = _bf(dm * ri[4][...] * (sr * (1.0 - sr)))
        ro[4][...] = _bf(_dot_nt(dbab, re[1][...]))
        dret = _dot_nt(dbrb, re[2][...])
        for h in range(RET_HEADS):
            cols = slice(h * RET_V_DIM, (h + 1) * RET_V_DIM)
            g = ri[5][:, cols]
            r, rn = _rms_stats(ri[6][:, cols])
            sg = jax.nn.sigmoid(g)
            dret_h = dret[:, cols]
            d_rn = dret_h * (g * sg)
            ro[6][:, cols] = _bf(dret_h * rn * (sg * (1.0 + g * (1.0 - sg))))
            ro[5][:, cols] = r * (d_rn - rn * jnp.mean(d_rn * rn, axis=-1, keepdims=True))

    outs = [(D_MODEL, BF16), (D_MODEL, BF16), (D_MODEL, BF16), (D_MODEL, BF16), (ATT_Q, BF16), (RET_V, F32),
            (RET_V, BF16)]
    return _row_call(body, tm=256, row_ins=[dx1b, z_a, z_r, ba, br, g_r, o_ret], res_ins=[wout, wba, wbr],
                     row_outs=outs, name="mix_bwd", exchange=exchange)


def _proj_bwd(d_pieces, x, dx1, w_in_t, g1, exchange):
    groups = (P_QA, P_KVA, P_QR, P_KR, P_VR, P_GR, P_ZA, P_ZR)
    n_p = len(groups)

    def body(ri, re, ro, po):
        dh = None
        for k, (off, w) in enumerate(groups):
            term = _dot(ri[k][...], re[0][off:off + w, :])
            dh = term if dh is None else dh + term
        r, xhat = _rms_stats(ri[n_p][...])
        dxn, dgain = _rms_bwd(dh, xhat, r, re[1][...])
        ro[0][...] = ri[n_p + 1][...] + dxn
        po[0][0] = jnp.sum(dgain, axis=0, keepdims=True)

    return _row_call(body, tm=512, row_ins=[*d_pieces, x, dx1], res_ins=[w_in_t, g1], row_outs=[(D_MODEL, F32)],
                     part_outs=(D_MODEL,), name="proj_bwd", exchange=exchange)


def _dw(a, b, *, tm, place, buf, name, exchange=None):
    t, m = a.shape
    n = b.shape[1]
    tk = min(2048, t)
    n_i, n_k = m // tm, t // tk
    fresh = isinstance(buf, jax.ShapeDtypeStruct)
    n_copies = len(place(0))

    def kern(a_ref, b_ref, *rest):
        out_ref, acc, sems = rest[-3:]
        i, k = pl.program_id(0), pl.program_id(1)
        part = _dot_tn(a_ref[...], b_ref[...])

        @pl.when(k == 0)
        def _():
            acc[i] = part

        @pl.when(k > 0)
        def _():
            acc[i] += part

        def copies(tile):
            return [pltpu.make_async_copy(acc.at[tile, pl.ds(r0, rows), :], out_ref.at[idx], sems.at[tile * n_copies + c])
                    for c, (r0, rows, idx) in enumerate(place(tile))]

        for tile in range(n_i):
            @pl.when((i == tile) & (k == n_k - 1))
            def _(tile=tile):
                for cp in copies(tile):
                    cp.start()

        @pl.when((i == n_i - 1) & (k == n_k - 1))
        def _():
            for tile in range(n_i):
                for cp in copies(tile):
                    cp.wait()

    in_specs = [pl.BlockSpec((tk, tm), lambda i, k: (k, i)), pl.BlockSpec((tk, n), lambda i, k: (k, 0))]
    shape = buf if fresh else jax.ShapeDtypeStruct(buf.shape, buf.dtype)
    return _pallas(
        kern, grid=(n_i, n_k), in_specs=in_specs + ([] if fresh else [_ANY]), out_specs=[_ANY], out_shape=[shape],
        scratch=[pltpu.VMEM((n_i, tm, n), F32), pltpu.SemaphoreType.DMA((n_i * n_copies,))],
        args=[a, b] + ([] if fresh else [buf]), aliases=None if fresh else {2: 0}, name=name, exchange=exchange)


def _heads_to_lanes(x3):
    return jnp.concatenate([x3[g] for g in range(GROUP)], axis=1)


def _lanes_to_heads(xt):
    return jnp.concatenate([xt[:, g * BLOCK:(g + 1) * BLOCK] for g in range(GROUP)], axis=0)


def _attn_group(n, kvh, q_ref, kvp_ref, kvc_ref, gq_col, gk, sink_ref):
    heads = [kvh * GROUP + g for g in range(GROUP)]
    cols = slice(kvh * GROUP * HEAD_DIM, (kvh + 1) * GROUP * HEAD_DIM)
    q3 = q_ref[:, cols].T.reshape(GROUP, HEAD_DIM, BLOCK)
    rq = lax.rsqrt(jnp.mean(q3 * q3, axis=1, keepdims=True) + EPS)
    qhat = q3 * rq
    qts = _heads_to_lanes(_bf(qhat * (gq_col * (HEAD_DIM ** -0.5))))
    kcols = slice(kvh * HEAD_DIM, (kvh + 1) * HEAD_DIM)
    vcols = slice(ATT_KV + kvh * HEAD_DIM, ATT_KV + (kvh + 1) * HEAD_DIM)
    k = jnp.concatenate([kvp_ref[:, kcols], kvc_ref[:, kcols]], axis=0)
    rk, khat = _rms_stats(k)
    knb = _bf(khat * gk)
    st = _dot(knb, qts)
    j = lax.broadcasted_iota(jnp.int32, (BLOCK, GROUP * BLOCK), 0)
    i = lax.broadcasted_iota(jnp.int32, (BLOCK, GROUP * BLOCK), 1) & (BLOCK - 1)
    from_prev = j > i
    f = jnp.where(from_prev, jnp.where(n > 0, st[0:BLOCK], -1e30), st[BLOCK:2 * BLOCK])
    sink = jnp.concatenate([jnp.broadcast_to(sink_ref[0:1, h:h + 1], (1, BLOCK)) for h in heads], axis=1)
    m = jnp.maximum(jnp.max(f, axis=0, keepdims=True), sink)
    e = jnp.exp(f - m)
    es = jnp.exp(sink - m)
    inv = 1.0 / (jnp.sum(e, axis=0, keepdims=True) + es)
    return dict(heads=heads, qhat=qhat, rq=rq, qts=qts, khat=khat, rk=rk, knb=knb, from_prev=from_prev,
                pf=e * inv, psink=es * inv)


def _unfold(from_prev, xf):
    return _bf(jnp.concatenate([jnp.where(from_prev, xf, 0.0), jnp.where(from_prev, 0.0, xf)], axis=0))


def _attn_fwd(q_a, kv_a, gq_col, gk, sinks):
    t = q_a.shape[0]
    nb = t // BLOCK

    def kern(q_ref, kvp_ref, kvc_ref, gq_ref, gk_ref, sink_ref, o_ref):
        n = pl.program_id(0)
        kvt = jnp.concatenate([kvp_ref[...].T, kvc_ref[...].T], axis=1)
        for kvh in range(N_KV_HEADS):
            a = _attn_group(n, kvh, q_ref, kvp_ref, kvc_ref, gq_ref[...], gk_ref[...], sink_ref)
            vt = _bf(kvt[ATT_KV + kvh * HEAD_DIM:ATT_KV + (kvh + 1) * HEAD_DIM, :])
            out_t = _dot(vt, _unfold(a["from_prev"], a["pf"]))
            cols = slice(kvh * GROUP * HEAD_DIM, (kvh + 1) * GROUP * HEAD_DIM)
            o_ref[:, cols] = _bf(_lanes_to_heads(out_t).T)

    small = lambda a: pl.BlockSpec(a.shape, lambda n: (0, 0))
    return dict(
        kern=kern,
        in_specs=[pl.BlockSpec((BLOCK, ATT_Q), lambda n: (n, 0)),
                  pl.BlockSpec((BLOCK, 2 * ATT_KV), lambda n: (jnp.maximum(n - 1, 0), 0)),
                  pl.BlockSpec((BLOCK, 2 * ATT_KV), lambda n: (n, 0)),
                  small(gq_col), small(gk), small(sinks)],
        out_specs=[pl.BlockSpec((BLOCK, ATT_Q), lambda n: (n, 0))],
        out_shape=[jax.ShapeDtypeStruct((t, ATT_Q), BF16)], scratch=[],
        args=[q_a, kv_a, kv_a, gq_col, gk, sinks])


def _attn_bwd(q_a, kv_a, d_attn, gq_col, gk, gk_col, sinks, exchange):
    t = q_a.shape[0]
    nb = t // BLOCK

    def kern(q_ref, kvp_ref, kvc_ref, do_ref, gq_ref, gk_ref, gkc_ref, sink_ref,
             dq_ref, dkv_ref, dgq_ref, dgk_ref, dsink_ref, band_k, band_v, carry_k, carry_v):
        n = pl.program_id(0)
        gq_v = gq_ref[...]
        gk_v = gk_ref[...]

        @pl.when(n == 0)
        def _():
            carry_k[...] = jnp.zeros_like(carry_k)
            carry_v[...] = jnp.zeros_like(carry_v)
            dgq_ref[...] = jnp.zeros_like(dgq_ref)
            dgk_ref[...] = jnp.zeros_like(dgk_ref)
            dsink_ref[...] = jnp.zeros_like(dsink_ref)

        @pl.when(n == nb)
        def _():
            band_k[...] = jnp.zeros_like(band_k)
            band_v[...] = jnp.zeros_like(band_v)

        @pl.when(n < nb)
        def _():
            lane16 = lax.broadcasted_iota(jnp.int32, (1, N_Q_HEADS), 1)
            dsink = jnp.zeros((1, N_Q_HEADS), F32)
            dgq = jnp.zeros((HEAD_DIM, 1), F32)
            gk_col = gkc_ref[...]
            kvt = jnp.concatenate([kvp_ref[...].T, kvc_ref[...].T], axis=1)
            for kvh in range(N_KV_HEADS):
                a = _attn_group(n, kvh, q_ref, kvp_ref, kvc_ref, gq_v, gk_v, sink_ref)
                from_prev, pf, qhat = a["from_prev"], a["pf"], a["qhat"]
                cols = slice(kvh * GROUP * HEAD_DIM, (kvh + 1) * GROUP * HEAD_DIM)
                vcols = slice(ATT_KV + kvh * HEAD_DIM, ATT_KV + (kvh + 1) * HEAD_DIM)
                dot = _heads_to_lanes(_bf(do_ref[:, cols].astype(F32).T.reshape(GROUP, HEAD_DIM, BLOCK)))
                vb = _bf(jnp.concatenate([kvp_ref[:, vcols], kvc_ref[:, vcols]], axis=0))
                dpt = _dot(vb, dot)
                dpf = jnp.where(from_prev, dpt[0:BLOCK], dpt[BLOCK:2 * BLOCK])
                delta = jnp.sum(pf * dpf, axis=0, keepdims=True)
                dst = _unfold(from_prev, pf * (dpf - delta))
                dsk = a["psink"] * delta
                for g, h in enumerate(a["heads"]):
                    tot = jnp.sum(dsk[:, g * BLOCK:(g + 1) * BLOCK], axis=1, keepdims=True)
                    dsink = dsink - jnp.where(lane16 == h, tot, 0.0)
                kt = kvt[kvh * HEAD_DIM:(kvh + 1) * HEAD_DIM, :]
                knt = _bf(kt * lax.rsqrt(jnp.mean(kt * kt, axis=0, keepdims=True) + EPS) * gk_col)
                dqn = (_dot(knt, dst) * (HEAD_DIM ** -0.5))
                band_k[kvh] = _dot_nt(dst, a["qts"])
                band_v[kvh] = _dot_nt(_unfold(from_prev, pf), dot)
                dqn3 = _lanes_to_heads(dqn).reshape(GROUP, HEAD_DIM, BLOCK)
                u = dqn3 * gq_v
                dq3 = a["rq"] * (u - qhat * jnp.mean(u * qhat, axis=1, keepdims=True))
                dgq = dgq + jnp.sum(jnp.sum(dqn3 * qhat, axis=0), axis=1, keepdims=True)
                dq_ref[:, cols] = _bf(dq3.reshape(GROUP * HEAD_DIM, BLOCK).T)
            dsink_ref[...] += dsink
            dgq_ref[...] += dgq

        dgk = jnp.zeros((1, HEAD_DIM), F32)
        for kvh in range(N_KV_HEADS):
            kcols = slice(kvh * HEAD_DIM, (kvh + 1) * HEAD_DIM)
            vcols = slice(ATT_KV + kvh * HEAD_DIM, ATT_KV + (kvh + 1) * HEAD_DIM)
            dkn = carry_k[kvh] + band_k[kvh, 0:BLOCK, :]
            dv = carry_v[kvh] + band_v[kvh, 0:BLOCK, :]
            rk, khat = _rms_stats(kvp_ref[:, kcols])
            dk, dgain = _rms_bwd(dkn, khat, rk, gk_v)
            dgk = dgk + jnp.sum(dgain, axis=0, keepdims=True)
            dkv_ref[:, kcols] = _bf(dk)
            dkv_ref[:, vcols] = _bf(dv)
            carry_k[kvh] = band_k[kvh, BLOCK:2 * BLOCK, :]
            carry_v[kvh] = band_v[kvh, BLOCK:2 * BLOCK, :]
        dgk_ref[...] += dgk

    small = lambda a: pl.BlockSpec(a.shape, lambda n: (0, 0))
    last = nb - 1
    return _pallas(
        kern, grid=(nb + 1,),
        in_specs=[pl.BlockSpec((BLOCK, ATT_Q), lambda n: (jnp.minimum(n, last), 0)),
                  pl.BlockSpec((BLOCK, 2 * ATT_KV), lambda n: (jnp.maximum(n - 1, 0), 0)),
                  pl.BlockSpec((BLOCK, 2 * ATT_KV), lambda n: (jnp.minimum(n, last), 0)),
                  pl.BlockSpec((BLOCK, ATT_Q), lambda n: (jnp.minimum(n, last), 0)),
                  small(gq_col), small(gk), small(gk_col), small(sinks)],
        out_specs=[pl.BlockSpec((BLOCK, ATT_Q), lambda n: (jnp.minimum(n, last), 0)),
                   pl.BlockSpec((BLOCK, 2 * ATT_KV), lambda n: (jnp.maximum(n - 1, 0), 0)),
                   pl.BlockSpec((HEAD_DIM, 1), lambda n: (0, 0)),
                   pl.BlockSpec((1, HEAD_DIM), lambda n: (0, 0)),
                   pl.BlockSpec((1, N_Q_HEADS), lambda n: (0, 0))],
        out_shape=[jax.ShapeDtypeStruct((t, ATT_Q), BF16), jax.ShapeDtypeStruct((t, 2 * ATT_KV), BF16),
                   jax.ShapeDtypeStruct((HEAD_DIM, 1), F32), jax.ShapeDtypeStruct((1, HEAD_DIM), F32),
                   jax.ShapeDtypeStruct((1, N_Q_HEADS), F32)],
        scratch=[pltpu.VMEM((N_KV_HEADS, 2 * BLOCK, HEAD_DIM), F32),
                 pltpu.VMEM((N_KV_HEADS, 2 * BLOCK, HEAD_DIM), F32),
                 pltpu.VMEM((N_KV_HEADS, BLOCK, HEAD_DIM), F32),
                 pltpu.VMEM((N_KV_HEADS, BLOCK, HEAD_DIM), F32)],
        args=[q_a, kv_a, kv_a, d_attn, gq_col, gk, gk_col, sinks], name="attn_bwd", exchange=exchange)


def _ret_tables(t, exchange):
    theta = 1.0 / (RET_ROT_BASE ** jnp.linspace(0.0, 1.0, RET_QK_DIM // 2, dtype=F32))
    theta2 = jnp.repeat(theta, 2)[None, :]
    sign = jnp.tile(jnp.array([-1.0, 1.0], F32), RET_QK_DIM // 2)[None, :]

    def kern(theta_ref, sign_ref, cos_ref, sin_ref):
        first = pl.program_id(0) * RET_CHUNK
        pos = (first + lax.broadcasted_iota(jnp.int32, (RET_CHUNK, RET_QK_DIM), 0)).astype(F32)
        ang = pos * theta_ref[...]
        cos_ref[...] = jnp.cos(ang)
        sin_ref[...] = jnp.sin(ang) * sign_ref[...]

    row = pl.BlockSpec((1, RET_QK_DIM), lambda n: (0, 0))
    blk = pl.BlockSpec((RET_CHUNK, RET_QK_DIM), lambda n: (n, 0))
    cos, sin_s, *got = _pallas(kern, grid=(t // RET_CHUNK,), in_specs=[row, row], out_specs=[blk, blk],
                               out_shape=[jax.ShapeDtypeStruct((t, RET_QK_DIM), F32)] * 2, args=[theta2, sign],
                               name="position_tables", exchange=exchange)
    log_gamma = jnp.log(1.0 - 2.0 ** (-5.0 - jnp.arange(RET_HEADS, dtype=F32)))
    i = jnp.arange(RET_CHUNK, dtype=F32)
    diff = i[:, None] - i[None, :]
    causal = diff >= 0
    decay = jnp.where(causal[None], jnp.exp(jnp.where(causal, diff, 0.0)[None] * log_gamma[:, None, None]), 0.0)
    xi = jnp.exp((i + 1.0)[None, :] * log_gamma[:, None])[:, :, None]
    zeta = jnp.exp((RET_CHUNK - 1.0 - i)[None, :] * log_gamma[:, None])[:, :, None]
    gch = jnp.broadcast_to(jnp.exp(RET_CHUNK * log_gamma)[:, None, None], (RET_HEADS, 1, 128))
    return (cos, sin_s, decay, xi, zeta, gch), got


def _swap_pairs(x):
    lane = lax.broadcasted_iota(jnp.int32, x.shape, 1)
    return jnp.where((lane & 1) == 0, pltpu.roll(x, RET_QK_DIM - 1, 1), pltpu.roll(x, 1, 1))


def _rotate(x, cos, sin_s):
    return x * cos + _swap_pairs(x) * sin_s


def _rotate_bwd(dy, cos, sin_s):
    return dy * cos + _swap_pairs(dy * sin_s)


def _ret_specs(order):
    qk = pl.BlockSpec((RET_CHUNK, RET_QK), lambda j: (order(j), 0))
    v = pl.BlockSpec((RET_CHUNK, RET_V), lambda j: (order(j), 0))
    dec = pl.BlockSpec((RET_HEADS, RET_CHUNK, RET_CHUNK), lambda j: (0, 0, 0))
    col = pl.BlockSpec((RET_HEADS, RET_CHUNK, 1), lambda j: (0, 0, 0))
    gch = pl.BlockSpec((RET_HEADS, 1, 128), lambda j: (0, 0, 0))
    st = pl.BlockSpec((RET_HEADS, None, RET_QK_DIM, RET_V_DIM), lambda j: (0, order(j), 0, 0))
    pos = pl.BlockSpec((RET_CHUNK, RET_QK_DIM), lambda j: (order(j), 0))
    return qk, v, dec, col, gch, st, pos


def _ret_fwd(q_r, k_r, v_r, g_r, tables):
    t = q_r.shape[0]
    nc = t // RET_CHUNK
    cos, sin_s, decay, xi, zeta, gch = tables

    def kern(q_ref, k_ref, v_ref, g_ref, cos_ref, sin_ref, dec_ref, xi_ref, zeta_ref, gch_ref,
             o_ref, ret_ref, st_ref, state):
        @pl.when(pl.program_id(0) == 0)
        def _():
            state[...] = jnp.zeros_like(state)

        cos_t = cos_ref[...]
        sin_t = sin_ref[...]
        for h in range(RET_HEADS):
            qc = slice(h * RET_QK_DIM, (h + 1) * RET_QK_DIM)
            vc = slice(h * RET_V_DIM, (h + 1) * RET_V_DIM)
            qs = _bf(_rotate(q_ref[:, qc], cos_t, sin_t))
            ks = _rotate(k_ref[:, qc] * (RET_QK_DIM ** -0.5), cos_t, sin_t)
            vb = v_ref[:, vc]
            s_old = state[h]
            sb = _bf(s_old)
            st_ref[h] = sb
            inner = _dot_nt(qs, _bf(ks)) * dec_ref[h]
            out = _dot(_bf(inner), vb) + _dot(qs, sb) * xi_ref[h]
            state[h] = gch_ref[h, :, 0:1] * s_old + _dot_tn(_bf(ks * zeta_ref[h]), vb)
            o_ref[:, vc] = out
            r, rn = _rms_stats(out)
            g = g_ref[:, vc]
            ret_ref[:, vc] = _bf(g * jax.nn.sigmoid(g) * rn)

    qk, v, dec, col, gsp, st, pos = _ret_specs(lambda j: j)
    return dict(
        kern=kern,
        in_specs=[qk, qk, v, v, pos, pos, dec, col, col, gsp],
        out_specs=[v, v, st],
        out_shape=[jax.ShapeDtypeStruct((t, RET_V), F32), jax.ShapeDtypeStruct((t, RET_V), BF16),
                   jax.ShapeDtypeStruct((RET_HEADS, nc, RET_QK_DIM, RET_V_DIM), BF16)],
        scratch=[pltpu.VMEM((RET_HEADS, RET_QK_DIM, RET_V_DIM), F32)],
        args=[q_r, k_r, v_r, g_r, cos, sin_s, decay, xi, zeta, gch])


def _ret_bwd(q_r, k_r, v_r, d_o, states, tables, exchange):
    t = q_r.shape[0]
    nc = t // RET_CHUNK
    cos, sin_s, decay, xi, zeta, gch = tables

    def kern(q_ref, k_ref, v_ref, do_ref, st_ref, cos_ref, sin_ref, dec_ref, xi_ref, zeta_ref, gch_ref,
             dq_ref, dk_ref, dv_ref, dstate):
        @pl.when(pl.program_id(0) == 0)
        def _():
            dstate[...] = jnp.zeros_like(dstate)

        cos_t = cos_ref[...]
        sin_t = sin_ref[...]
        scale = RET_QK_DIM ** -0.5
        for h in range(RET_HEADS):
            qc = slice(h * RET_QK_DIM, (h + 1) * RET_QK_DIM)
            vc = slice(h * RET_V_DIM, (h + 1) * RET_V_DIM)
            qs = _bf(_rotate(q_ref[:, qc], cos_t, sin_t))
            ks = _rotate(k_ref[:, qc] * scale, cos_t, sin_t)
            ksb = _bf(ks)
            vb = v_ref[:, vc]
            d_o_t = do_ref[:, vc]
            dob = _bf(d_o_t)
            doxb = _bf(d_o_t * xi_ref[h])
            dec = dec_ref[h]
            ds_old = dstate[h]
            dsb = _bf(ds_old)
            pb = _bf(_dot_nt(qs, ksb) * dec)
            dpb = _bf(_dot_nt(dob, vb) * dec)
            dqs = _dot(dpb, ksb) + _dot_nt(doxb, st_ref[h])
            dks = _dot_tn(dpb, qs) + _dot_nt(vb, dsb) * zeta_ref[h]
            dv_ref[:, vc] = _bf(_dot_tn(pb, dob) + _dot(_bf(ks * zeta_ref[h]), dsb))
            dstate[h] = gch_ref[h, :, 0:1] * ds_old + _dot_tn(qs, doxb)
            dq_ref[:, qc] = _bf(_rotate_bwd(dqs, cos_t, sin_t))
            dk_ref[:, qc] = _bf(_rotate_bwd(dks, cos_t, sin_t) * scale)

    qk, v, dec, col, gsp, st, pos = _ret_specs(lambda j: nc - 1 - j)
    return _pallas(
        kern, grid=(nc,),
        in_specs=[qk, qk, v, v, st, pos, pos, dec, col, col, gsp],
        out_specs=[qk, qk, v],
        out_shape=[jax.ShapeDtypeStruct((t, RET_QK), BF16), jax.ShapeDtypeStruct((t, RET_QK), BF16),
                   jax.ShapeDtypeStruct((t, RET_V), BF16)],
        scratch=[pltpu.VMEM((RET_HEADS, RET_QK_DIM, RET_V_DIM), F32)],
        args=[q_r, k_r, v_r, d_o, states, cos, sin_s, decay, xi, zeta, gch], name="ret_bwd", exchange=exchange)


def _position():
    return lax.axis_index("x"), lax.axis_index("y"), lax.axis_index("c")


def _gather_exchange(owns, forward_at):
    n = len(owns)

    def copies(ins, outs, send_sems, recv_sems, base):
        x, y, c = _position()
        sibling = (x, y, 1 - c)
        chips = [(1 - x, y), (x, 1 - y), (1 - x, 1 - y)]
        my_chip = 2 * x + y

        def slab(a, chip, hf):
            half = owns[a].shape[0] // 2
            return outs[a].at[chip, pl.ds(hf * half, half), :]

        def copy(k, src, dst, to):
            return pltpu.make_async_remote_copy(src_ref=src, dst_ref=dst, send_sem=send_sems.at[base + k],
                                                recv_sem=recv_sems.at[base + k], device_id=to, device_id_type=MESH)

        first, passed, from_sibling = [], [], []
        for a in range(n):
            half = owns[a].shape[0] // 2
            for k, (cx, cy) in enumerate(chips):
                first.append(copy(6 * a + k, ins[a].at[pl.ds(c * half, half), :], slab(a, my_chip, c), (cx, cy, c)))
                landed = slab(a, 2 * cx + cy, c)
                passed.append(copy(6 * a + 3 + k, landed, landed, sibling))
                theirs = slab(a, 2 * cx + cy, 1 - c)
                from_sibling.append(copy(6 * a + 3 + k, theirs, theirs, sibling))
        return first, passed, from_sibling

    def start(*args):
        first, _, _ = copies(*args)
        for cp in first:
            cp.start()

    def forward(*args):
        first, passed, _ = copies(*args)
        for arrived, cp in zip(first, passed):
            arrived.wait_recv()
            cp.start()

    def finish(*args):
        first, passed, from_sibling = copies(*args)
        for cp in from_sibling:
            cp.wait_recv()
        for cp in first + passed:
            cp.wait_send()

    outs = [jax.ShapeDtypeStruct((N_CHIPS, *a.shape), a.dtype) for a in owns]
    return _Exchange(owns, outs, 6 * n, [(0.0, start), (forward_at, forward), (1.0, finish)])


def _symmetric_exchange(ins, outs, plan):
    n_sems = len(plan([None] * len(ins), [None] * len(outs), 0, 0, 0, dry=True))

    def copies(in_refs, out_refs, send_sems, recv_sems, base):
        x, y, c = _position()
        return [pltpu.make_async_remote_copy(src_ref=src, dst_ref=dst, send_sem=send_sems.at[base + k],
                                             recv_sem=recv_sems.at[base + k], device_id=dev, device_id_type=MESH)
                for k, (src, dst, dev) in enumerate(plan(in_refs, out_refs, x, y, c, dry=False))]

    def start(*args):
        for cp in copies(*args):
            cp.start()

    def finish(*args):
        for cp in copies(*args):
            cp.wait()

    return _Exchange(ins, outs, n_sems, [(0.0, start), (1.0, finish)])


def _pair_exchange(gs):
    def plan(in_refs, out_refs, x, y, c, dry):
        out = []
        for a, g in enumerate(gs):
            half = g.shape[1] // 2
            for k in range(N_CHIPS):
                out.append(None if dry else (in_refs[a].at[k, pl.ds((1 - c) * half, half), :], out_refs[a].at[k],
                                             (x, y, 1 - c)))
        return out

    outs = [jax.ShapeDtypeStruct((g.shape[0], g.shape[1] // 2, g.shape[2]), g.dtype) for g in gs]
    return _symmetric_exchange(gs, outs, plan)


def _pair_sum(g, from_sibling, c_arr, *, tile, name):
    n, rows, width = g.shape
    tiles = (rows // 2) // tile

    def kern(c_ref, g_ref, s_ref, o_ref):
        o_ref[...] = _bf(g_ref[...] + s_ref[...])

    return pl.pallas_call(
        kern,
        grid_spec=pltpu.PrefetchScalarGridSpec(
            num_scalar_prefetch=1, grid=(n, tiles),
            in_specs=[pl.BlockSpec((None, tile, width), lambda k, i, c: (k, c[0] * tiles + i, 0)),
                      pl.BlockSpec((None, tile, width), lambda k, i, c: (k, i, 0))],
            out_specs=pl.BlockSpec((None, tile, width), lambda k, i, c: (k, i, 0))),
        out_shape=jax.ShapeDtypeStruct((n, rows // 2, width), BF16), name=name,
        compiler_params=_params(("parallel", "parallel")),
    )(c_arr, g, from_sibling)


def _scatter_to_owners(hsums):
    def plan(in_refs, out_refs, x, y, c, dry):
        out = []
        for a in range(len(hsums)):
            for k, (cx, cy) in enumerate([(1 - x, y), (x, 1 - y), (1 - x, 1 - y)]):
                out.append(None if dry else (in_refs[a].at[2 * cx + cy], out_refs[a].at[k], (cx, cy, c)))
        return out

    outs = [jax.ShapeDtypeStruct((3, *h.shape[1:]), h.dtype) for h in hsums]
    return _symmetric_exchange(hsums, outs, plan)


def _sum_chips(hsum, parts, chip_arr, *, tile, name):
    n, half, width = parts.shape

    def kern(chip_ref, h_ref, p_ref, o_ref):
        acc = h_ref[...].astype(F32)
        for k in range(n):
            acc = acc + p_ref[k].astype(F32)
        o_ref[...] = acc

    return pl.pallas_call(
        kern,
        grid_spec=pltpu.PrefetchScalarGridSpec(
            num_scalar_prefetch=1, grid=(half // tile,),
            in_specs=[pl.BlockSpec((None, tile, width), lambda i, chip: (chip[0], i, 0)),
                      pl.BlockSpec((n, tile, width), lambda i, chip: (0, i, 0))],
            out_specs=pl.BlockSpec((tile, width), lambda i, chip: (i, 0))),
        out_shape=jax.ShapeDtypeStruct((half, width), F32), name=name,
        compiler_params=_params(("parallel",)),
    )(chip_arr, hsum, parts)


def _share_halves(fhalves):
    def plan(in_refs, out_refs, x, y, c, dry):
        return [None if dry else (in_refs[a], out_refs[a], (x, y, 1 - c)) for a in range(len(fhalves))]

    return _symmetric_exchange(fhalves, [jax.ShapeDtypeStruct(f.shape, f.dtype) for f in fhalves], plan)


def _adamw_math(w, g, m, v):
    m = ADAM_B1 * m + (1.0 - ADAM_B1) * g
    v = ADAM_B2 * v + (1.0 - ADAM_B2) * (g * g)
    m_hat = m / (1.0 - ADAM_B1 ** ADAM_STEP)
    v_hat = v / (1.0 - ADAM_B2 ** ADAM_STEP)
    delta = -ADAM_LR * (m_hat / (jnp.sqrt(v_hat) + ADAM_EPS) + ADAM_WD * w)
    return delta, m, v


def _adamw(w, m, v, g_mine, g_other, c_arr, *, row_off, tile, name):
    rows, width = w.shape
    tiles_per_half = g_mine.shape[0] // tile
    first = row_off // tile

    def kern(c_ref, w_ref, gm_ref, go_ref, m_ref, v_ref, g_ref, d_ref, nm_ref, nv_ref):
        in_my_half = ((first + pl.program_id(0)) // tiles_per_half) == c_ref[0]
        g = jnp.where(in_my_half, gm_ref[...], go_ref[...])
        g_ref[...] = g
        d_ref[...], nm_ref[...], nv_ref[...] = _adamw_math(w_ref[...], g, m_ref[...], v_ref[...])

    full = pl.BlockSpec((tile, width), lambda i, c: (i, 0))
    half = pl.BlockSpec((tile, width), lambda i, c: ((first + i) % tiles_per_half, 0))
    return pl.pallas_call(
        kern,
        grid_spec=pltpu.PrefetchScalarGridSpec(
            num_scalar_prefetch=1, grid=(rows // tile,), in_specs=[full, half, half, full, full],
            out_specs=[full] * 4),
        out_shape=[jax.ShapeDtypeStruct((rows, width), F32)] * 4, name=name,
        compiler_params=_params(("parallel",)),
    )(c_arr, w, g_mine, g_other, m, v)


def _small_step(part, tile_sums, w, m, v):
    loss_p, dg1_p, dg2_p = tile_sums

    def body(part_ref, loss_ref, dg1_ref, dg2_ref, w_ref, m_ref, v_ref, g_ref, d_ref, nm_ref, nv_ref,
             mine, gathered, send_sems, recv_sems):
        x, y, c = _position()
        me = 4 * x + 2 * y + c
        mine[...] = part_ref[...]
        for r in range(8):
            lanes = slice(128 * r, 128 * (r + 1))
            mine[SM_G1 + r:SM_G1 + r + 1, :] = jnp.sum(dg1_ref[:, lanes], axis=0, keepdims=True)
            mine[SM_G2 + r:SM_G2 + r + 1, :] = jnp.sum(dg2_ref[:, lanes], axis=0, keepdims=True)
        mine[SM_LOSS:SM_LOSS + 1, :] = jnp.sum(loss_ref[...], axis=0, keepdims=True)
        copies = []
        for k in range(1, N_DEV):
            flip = (k >> 2) & 1, (k >> 1) & 1, k & 1
            to = (x ^ flip[0], y ^ flip[1], c ^ flip[2])
            cp = pltpu.make_async_remote_copy(
                src_ref=mine, dst_ref=gathered.at[me], send_sem=send_sems.at[k - 1], recv_sem=recv_sems.at[k - 1],
                device_id=to, device_id_type=MESH)
            cp.start()
            copies.append(cp)
        gathered[me] = mine[...]
        for k in range(1, N_DEV):
            flip = (k >> 2) & 1, (k >> 1) & 1, k & 1
            src = 4 * (x ^ flip[0]) + 2 * (y ^ flip[1]) + (c ^ flip[2])
            pltpu.make_async_remote_copy(
                src_ref=mine, dst_ref=gathered.at[src], send_sem=send_sems.at[k - 1], recv_sem=recv_sems.at[k - 1],
                device_id=(x, y, c), device_id_type=MESH).wait_recv()
        for cp in copies:
            cp.wait_send()
        total = gathered[0]
        for k in range(1, N_DEV):
            total = total + gathered[k]
        g_ref[...] = total
        d_ref[...], nm_ref[...], nv_ref[...] = _adamw_math(w_ref[...], total, m_ref[...], v_ref[...])

    vm = pl.BlockSpec(memory_space=pltpu.VMEM)
    blk = jax.ShapeDtypeStruct((SMALL_ROWS, 128), F32)
    return pl.pallas_call(
        body, in_specs=[vm] * 7, out_specs=[vm] * 4, out_shape=[blk] * 4,
        scratch_shapes=[pltpu.VMEM((SMALL_ROWS, 128), F32), pltpu.VMEM((N_DEV, SMALL_ROWS, 128), F32),
                        pltpu.SemaphoreType.DMA((N_DEV - 1,)), pltpu.SemaphoreType.DMA((N_DEV - 1,))],
        name="small_step",
    )(part, loss_p.reshape(-1, 128), dg1_p.reshape(-1, D_MODEL), dg2_p.reshape(-1, D_MODEL), w, m, v)


def _with_own(gathered, own, my_chip):
    return lax.dynamic_update_slice(gathered, own[None], (my_chip, 0, 0))


def _pack_small(g1, g2, gq, gk, sinks):
    blk = jnp.zeros((SMALL_ROWS, 128), F32)
    blk = blk.at[SM_G1:SM_G1 + 8].set(g1.reshape(8, 128))
    blk = blk.at[SM_G2:SM_G2 + 8].set(g2.reshape(8, 128))
    blk = blk.at[SM_GQ, :HEAD_DIM].set(gq.reshape(-1))
    blk = blk.at[SM_GK, :HEAD_DIM].set(gk.reshape(-1))
    blk = blk.at[SM_SINK, :N_Q_HEADS].set(sinks.reshape(-1))
    return blk


def _unpack_small(blk):
    return (blk[SM_G1:SM_G1 + 8].reshape(1, D_MODEL), blk[SM_G2:SM_G2 + 8].reshape(1, D_MODEL),
            blk[SM_GQ, :HEAD_DIM].reshape(1, HEAD_DIM), blk[SM_GK, :HEAD_DIM].reshape(1, HEAD_DIM),
            blk[SM_SINK, :N_Q_HEADS].reshape(1, N_Q_HEADS))


def kernel(x, norm_mix_gain, w_in, q_norm_gain, k_norm_gain, attn_sinks, w_branch_attn, w_branch_ret, w_out, norm_ffn_gain, w_ffn_gate, w_ffn_up, w_ffn_down, loss_target, m_norm_mix_gain, m_w_in, m_q_norm_gain, m_k_norm_gain, m_attn_sinks, m_w_branch_attn, m_w_branch_ret, m_w_out, m_norm_ffn_gain, m_w_ffn_gate, m_w_ffn_up, m_w_ffn_down, v_norm_mix_gain, v_w_in, v_q_norm_gain, v_k_norm_gain, v_attn_sinks, v_w_branch_attn, v_w_branch_ret, v_w_out, v_norm_ffn_gain, v_w_ffn_gate, v_w_ffn_up, v_w_ffn_down):
    my_chip = 2 * lax.axis_index("x") + lax.axis_index("y")
    c_arr = lax.axis_index("c").astype(jnp.int32).reshape(1)
    chip_arr = my_chip.astype(jnp.int32).reshape(1)
    x_t, target = x[0], loss_target[0]
    g1, g2, gq, gk, sinks = norm_mix_gain, norm_ffn_gain, q_norm_gain, k_norm_gain, attn_sinks

    tr = lambda a: jnp.transpose(a[0])
    own_w_in = _bf(tr(w_in))
    own_ffn = _bf(jnp.concatenate([tr(w_ffn_gate), tr(w_ffn_up), w_ffn_down[0]], axis=0))
    own_mix = _bf(jnp.concatenate([w_branch_attn[0], w_branch_ret[0], w_out[0]], axis=0))
    tables, (got_w_in,) = _ret_tables(x_t.shape[0], _gather_exchange([own_w_in], 0.9))
    w_in_t = _with_own(got_w_in, own_w_in, my_chip).reshape(D_IN, D_MODEL)
    (h1, q_a, kv_a, q_r, k_r, v_r, g_r, z_a, z_r, got_ffn, got_mix) = _proj_fwd(
        x_t, g1, w_in_t, _gather_exchange([own_ffn, own_mix], 0.8))
    all_ffn = _with_own(got_ffn, own_ffn, my_chip)
    all_mix = _with_own(got_mix, own_mix, my_chip)
    wg_t = all_ffn[:, 0:FF_SH].reshape(D_FF, D_MODEL)
    wu_t = all_ffn[:, FF_SH:2 * FF_SH].reshape(D_FF, D_MODEL)
    wd = all_ffn[:, 2 * FF_SH:3 * FF_SH].reshape(D_FF, D_MODEL)
    wba = all_mix[:, 0:256].reshape(ATT_Q, D_MODEL)
    wbr = all_mix[:, 256:768].reshape(RET_V, D_MODEL)
    wout = all_mix[:, 768:1024].reshape(D_MODEL, D_MODEL)

    gq_col, gk_col = gq.reshape(HEAD_DIM, 1), gk.reshape(HEAD_DIM, 1)
    attn, o_ret, ret, states = _fused([_attn_fwd(q_a, kv_a, gq_col, gk, sinks), _ret_fwd(q_r, k_r, v_r, g_r, tables)],
                                      grid=(x_t.shape[0] // BLOCK,), name="mixers_fwd")
    ba, br, merged, x1, h2 = _mix_fwd(attn, ret, z_a, z_r, x_t, wba, wbr, wout, g2)
    act, dgate, dup, dyb, dx1, dx1b, loss_p, dg2_p = _ffn_fwd_bwd(h2, x1, target, wg_t, wu_t, wd, g2)

    def pairs(row0, rows):
        return lambda i: [(h * rows, rows, (2 * i + h, pl.ds(row0, rows), slice(None))) for h in range(2)]

    f_block = jax.ShapeDtypeStruct((N_CHIPS, 3 * FF_SH, D_MODEL), F32)
    f_block, = _dw(dgate, h2, tm=2 * FF_SH, place=pairs(0, FF_SH), buf=f_block, name="dw_gate")
    f_block, = _dw(dup, h2, tm=2 * FF_SH, place=pairs(FF_SH, FF_SH), buf=f_block, name="dw_up")
    f_block, = _dw(act, dyb, tm=2 * FF_SH, place=pairs(2 * FF_SH, FF_SH), buf=f_block, name="dw_down")
    (dba, dbr, dz_a, dz_r, d_attn, d_o, dg_r, sib_ffn) = _mix_bwd(
        dx1b, z_a, z_r, ba, br, g_r, o_ret, wout, wba, wbr, _pair_exchange([f_block]))
    f_sum = _pair_sum(f_block, sib_ffn, c_arr, tile=528, name="pair_sum_ffn")

    def quarters(row0, rows):
        return lambda i: [(k * rows, rows, (k, pl.ds(row0, rows), slice(None))) for k in range(N_CHIPS)]

    m_block = jax.ShapeDtypeStruct((N_CHIPS, D_MODEL, D_MODEL), F32)
    m_block, = _dw(attn, dba, tm=ATT_Q, place=quarters(0, 256), buf=m_block, name="dw_ba")
    m_block, = _dw(ret, dbr, tm=D_MODEL, place=pairs(256, 512), buf=m_block, name="dw_br")
    m_block, = _dw(merged, dx1b, tm=D_MODEL, place=quarters(768, 256), buf=m_block, name="dw_out")

    def w_in_rows(group):
        off, w = group
        tm = min(w, D_MODEL)
        return dict(tm=tm, place=lambda i: [(0, tm, (pl.ds(off + i * tm, tm), slice(None)))])

    w_block = jax.ShapeDtypeStruct((D_IN, D_MODEL), F32)
    w_block, sib_mix = _dw(dg_r, h1, buf=w_block, name="dw_in_5", exchange=_pair_exchange([m_block]), **w_in_rows(P_GR))
    w_block, = _dw(dz_a, h1, buf=w_block, name="dw_in_6", **w_in_rows(P_ZA))
    w_block, = _dw(dz_r, h1, buf=w_block, name="dw_in_7", **w_in_rows(P_ZR))
    m_sum = _pair_sum(m_block, sib_mix, c_arr, tile=256, name="pair_sum_mix")

    dq_r, dk_r, dv_r, got_ffn_sums = _ret_bwd(q_r, k_r, v_r, d_o, states, tables, _scatter_to_owners([f_sum]))
    ffn_half = _sum_chips(f_sum, got_ffn_sums, chip_arr, tile=528, name="sum_chips_ffn")
    w_block, = _dw(dq_r, h1, buf=w_block, name="dw_in_2", **w_in_rows(P_QR))
    w_block, = _dw(dk_r, h1, buf=w_block, name="dw_in_3", **w_in_rows(P_KR))
    w_block, = _dw(dv_r, h1, buf=w_block, name="dw_in_4", **w_in_rows(P_VR))

    (dq_a, dkv_a, dgq, dgk, dsinks, got_mix_sums, ffn_other) = _attn_bwd(
        q_a, kv_a, d_attn, gq_col, gk, gk_col, sinks,
        _merge_exchanges(_scatter_to_owners([m_sum]), _share_halves([ffn_half])))
    dgq = dgq.reshape(1, HEAD_DIM)
    mix_half = _sum_chips(m_sum, got_mix_sums, chip_arr, tile=256, name="sum_chips_mix")
    w_block, mix_other = _dw(dq_a, h1, buf=w_block, name="dw_in_0", exchange=_share_halves([mix_half]),
                             **w_in_rows(P_QA))
    w_block, = _dw(dkv_a, h1, buf=w_block, name="dw_in_1", **w_in_rows(P_KVA))

    w_block = w_block.reshape(N_CHIPS, W_IN_SH, D_MODEL)
    sib_w_in, = _run_exchange(_pair_exchange([w_block]), "pair_exchange_w_in")
    w_sum = _pair_sum(w_block, sib_w_in, c_arr, tile=592, name="pair_sum_w_in")
    d_pieces = [dq_a, dkv_a, dq_r, dk_r, dv_r, dg_r, dz_a, dz_r]
    grad_x, dg1_p, got_w_in_sums = _proj_bwd(d_pieces, x_t, dx1, w_in_t, g1, _scatter_to_owners([w_sum]))
    w_in_half = _sum_chips(w_sum, got_w_in_sums, chip_arr, tile=592, name="sum_chips_w_in")
    w_in_other, = _run_exchange(_share_halves([w_in_half]), "share_halves_w_in")

    def update(name, g_half, g_other, row_off, tile, wmv, transposed=False):
        w, m, v = [tr(a) if transposed else a[0] for a in wmv]
        outs = _adamw(w, m, v, g_half, g_other, c_arr, row_off=row_off, tile=tile, name=f"adamw_{name}")
        return [jnp.transpose(o) if transposed else o for o in outs]

    big = dict(
        w_in=update("w_in", w_in_half, w_in_other, 0, 592, (w_in, m_w_in, v_w_in), True),
        wg=update("wg", ffn_half, ffn_other, 0, 352, (w_ffn_gate, m_w_ffn_gate, v_w_ffn_gate), True),
        wu=update("wu", ffn_half, ffn_other, FF_SH, 352, (w_ffn_up, m_w_ffn_up, v_w_ffn_up), True),
        wd=update("wd", ffn_half, ffn_other, 2 * FF_SH, 352, (w_ffn_down, m_w_ffn_down, v_w_ffn_down)),
        wba=update("wba", mix_half, mix_other, 0, 256, (w_branch_attn, m_w_branch_attn, v_w_branch_attn)),
        wbr=update("wbr", mix_half, mix_other, 256, 256, (w_branch_ret, m_w_branch_ret, v_w_branch_ret)),
        wout=update("wout", mix_half, mix_other, 768, 256, (w_out, m_w_out, v_w_out)))

    zeros = jnp.zeros((1, D_MODEL), F32)
    part = _pack_small(zeros, zeros, dgq, dgk, dsinks)
    sm_w = _pack_small(norm_mix_gain, norm_ffn_gain, q_norm_gain, k_norm_gain, attn_sinks)
    sm_m = _pack_small(m_norm_mix_gain, m_norm_ffn_gain, m_q_norm_gain, m_k_norm_gain, m_attn_sinks)
    sm_v = _pack_small(v_norm_mix_gain, v_norm_ffn_gain, v_q_norm_gain, v_k_norm_gain, v_attn_sinks)
    sm_g, sm_d, sm_nm, sm_nv = _small_step(part, (loss_p, dg1_p, dg2_p), sm_w, sm_m, sm_v)
    loss = sm_g[SM_LOSS, 0]

    def leaves(i, sm):
        b = [big[n][i][None] for n in ("w_in", "wba", "wbr", "wout", "wg", "wu", "wd")]
        s1, s2, sq, sk, ss = _unpack_small(sm)
        return [s1, b[0], sq, sk, ss, b[1], b[2], b[3], s2, b[4], b[5], b[6]]

    return (loss, grad_x[None], *leaves(0, sm_g), *leaves(1, sm_d), *leaves(2, sm_nm), *leaves(3, sm_nv))
```

```python
import jax
import jax.numpy as jnp
from jax import lax
from jax.experimental import pallas as pl
from jax.experimental.pallas import tpu as pltpu

F32 = jnp.float32
BF16 = jnp.bfloat16
MESH = pl.DeviceIdType.MESH

D_MODEL = 1024
EPS = 1e-6
HEAD_DIM = 64
N_Q_HEADS = 16
N_KV_HEADS = 2
GROUP = 8
BLOCK = 128
RET_HEADS = 4
RET_QK_DIM = 256
RET_V_DIM = 512
RET_CHUNK = 128
RET_ROT_BASE = 10000.0
D_FF = 2816
ATT_Q = N_Q_HEADS * HEAD_DIM
ATT_KV = N_KV_HEADS * HEAD_DIM
RET_QK = RET_HEADS * RET_QK_DIM
RET_V = RET_HEADS * RET_V_DIM
D_IN = 9472
ADAM_LR = 0.001
ADAM_B1 = 0.9
ADAM_B2 = 0.999
ADAM_EPS = 1e-08
ADAM_WD = 0.01
ADAM_STEP = 10

N_CHIPS = 4
N_DEV = 8
VMEM_LIMIT_BYTES = 60 * 1024 * 1024

P_QA = (0, 1024)
P_KVA = (1024, 256)
P_QR = (1280, 1024)
P_KR = (2304, 1024)
P_VR = (3328, 2048)
P_GR = (5376, 2048)
P_ZA = (7424, 1024)
P_ZR = (8448, 1024)

W_IN_SH = D_IN // N_CHIPS
FF_SH = D_FF // N_CHIPS

SMALL_ROWS = 8


def _dot(a, b):
    return jnp.dot(a, b, preferred_element_type=F32)


def _dot_nt(a, b):
    return lax.dot_general(a, b, (((1,), (1,)), ((), ())), preferred_element_type=F32)


def _dot_tn(a, b):
    return lax.dot_general(a, b, (((0,), (0,)), ((), ())), preferred_element_type=F32)


def _bf(x):
    return x.astype(BF16)


def _rms_stats(x):
    r = lax.rsqrt(jnp.mean(x * x, axis=-1, keepdims=True) + EPS)
    return r, x * r


def _rms_bwd(dy, xhat, r, gain):
    u = dy * gain
    dx = r * (u - xhat * jnp.mean(u * xhat, axis=-1, keepdims=True))
    return dx, dy * xhat


def _params(sem):
    return pltpu.CompilerParams(dimension_semantics=sem, vmem_limit_bytes=VMEM_LIMIT_BYTES)


_ANY = pl.BlockSpec(memory_space=pl.ANY)


class _Exchange:
    def __init__(self, ins, outs, n_sems, phases):
        self.ins, self.outs, self.n_sems, self.phases = list(ins), list(outs), n_sems, list(phases)


def _merge_exchanges(a, b):
    na_i, na_o, shift = len(a.ins), len(a.outs), a.n_sems

    def first(fn):
        return lambda i, o, s, r, base: fn(i[:na_i], o[:na_o], s, r, base)

    def second(fn):
        return lambda i, o, s, r, base: fn(i[na_i:], o[na_o:], s, r, base + shift)

    phases = [(f, first(fn)) for f, fn in a.phases] + [(f, second(fn)) for f, fn in b.phases]
    return _Exchange(a.ins + b.ins, a.outs + b.outs, a.n_sems + b.n_sems, sorted(phases, key=lambda p: p[0]))


def _pallas(kern, *, grid, in_specs, out_specs, out_shape, args, name, scratch=(), exchange=None, aliases=None):
    aliases = aliases or {}
    if exchange is None:
        return pl.pallas_call(
            kern, grid=grid, in_specs=in_specs, out_specs=out_specs, out_shape=out_shape, name=name,
            scratch_shapes=list(scratch), input_output_aliases=aliases,
            compiler_params=_params(("arbitrary",) * len(grid)))(*args)
    n_in, n_out, n_sc = len(in_specs), len(out_specs), len(scratch)
    n_xi, n_xo = len(exchange.ins), len(exchange.outs)
    n_steps = 1
    for g in grid:
        n_steps *= g

    def wrapped(*refs):
        ins, refs = refs[:n_in], refs[n_in:]
        x_ins, refs = refs[:n_xi], refs[n_xi:]
        outs, refs = refs[:n_out], refs[n_out:]
        x_outs, refs = refs[:n_xo], refs[n_xo:]
        scr, (send_sems, recv_sems) = refs[:n_sc], refs[n_sc:]
        step = pl.program_id(0)
        for d in range(1, len(grid)):
            step = step * grid[d] + pl.program_id(d)
        for frac, fn in exchange.phases:
            at = min(int(frac * n_steps), n_steps - 1)

            @pl.when(step == at)
            def _(fn=fn):
                fn(x_ins, x_outs, send_sems, recv_sems, 0)

        kern(*ins, *outs, *scr)

    sems = [pltpu.SemaphoreType.DMA((exchange.n_sems,)), pltpu.SemaphoreType.DMA((exchange.n_sems,))]
    return pl.pallas_call(
        wrapped, grid=grid, in_specs=list(in_specs) + [_ANY] * n_xi, out_specs=list(out_specs) + [_ANY] * n_xo,
        out_shape=list(out_shape) + exchange.outs, name=name, scratch_shapes=list(scratch) + sems,
        input_output_aliases=aliases, compiler_params=_params(("arbitrary",) * len(grid)))(*args, *exchange.ins)


def _run_exchange(exchange, name):
    def body(*refs):
        n_i, n_o = len(exchange.ins), len(exchange.outs)
        for _, fn in exchange.phases:
            fn(refs[:n_i], refs[n_i:n_i + n_o], refs[n_i + n_o], refs[n_i + n_o + 1], 0)

    sems = [pltpu.SemaphoreType.DMA((exchange.n_sems,)), pltpu.SemaphoreType.DMA((exchange.n_sems,))]
    return pl.pallas_call(body, in_specs=[_ANY] * len(exchange.ins), out_specs=[_ANY] * len(exchange.outs),
                          out_shape=exchange.outs, scratch_shapes=sems, name=name)(*exchange.ins)


def _fused(parts, *, grid, name, exchange=None):
    counts = [(len(p["in_specs"]), len(p["out_specs"]), len(p["scratch"])) for p in parts]
    n_in, n_out = sum(c[0] for c in counts), sum(c[1] for c in counts)

    def kern(*refs):
        ins, outs, scr = refs[:n_in], refs[n_in:n_in + n_out], refs[n_in + n_out:]
        i0 = o0 = s0 = 0
        for p, (ni, no, ns) in zip(parts, counts):
            p["kern"](*ins[i0:i0 + ni], *outs[o0:o0 + no], *scr[s0:s0 + ns])
            i0, o0, s0 = i0 + ni, o0 + no, s0 + ns

    cat = lambda key: [a for p in parts for a in p[key]]
    return _pallas(kern, grid=grid, in_specs=cat("in_specs"), out_specs=cat("out_specs"), out_shape=cat("out_shape"),
                   scratch=cat("scratch"), args=cat("args"), name=name, exchange=exchange)


def _row_call(body, *, tm, row_ins, res_ins, row_outs, part_outs=(), name, exchange=None):
    t = row_ins[0].shape[0]
    n_tiles = t // tm
    in_specs = [pl.BlockSpec((tm, a.shape[1]), lambda i: (i, 0)) for a in row_ins]
    in_specs += [pl.BlockSpec(a.shape, lambda i: (0, 0), pipeline_mode=pl.Buffered(1)) for a in res_ins]
    out_shape = [jax.ShapeDtypeStruct((t, w), dt) for (w, dt) in row_outs]
    out_shape += [jax.ShapeDtypeStruct((n_tiles, 1, w), F32) for w in part_outs]
    out_specs = [pl.BlockSpec((tm, w), lambda i: (i, 0)) for (w, _) in row_outs]
    out_specs += [pl.BlockSpec((1, 1, w), lambda i: (i, 0, 0)) for w in part_outs]
    n_ri, n_re, n_ro = len(row_ins), len(res_ins), len(row_outs)

    def kern(*refs):
        body(refs[:n_ri], refs[n_ri:n_ri + n_re], refs[n_ri + n_re:n_ri + n_re + n_ro], refs[n_ri + n_re + n_ro:])

    return _pallas(kern, grid=(n_tiles,), in_specs=in_specs, out_specs=out_specs, out_shape=out_shape,
                   args=[*row_ins, *res_ins], name=name, exchange=exchange)


def _proj_fwd(x, g1, w_in_t, exchange):
    pieces = ((P_QA, F32), (P_KVA, F32), (P_QR, F32), (P_KR, F32), (P_VR, BF16), (P_GR, F32), (P_ZA, F32), (P_ZR, F32))

    def body(ri, re, ro, po):
        x_t = ri[0][...]
        r, xhat = _rms_stats(x_t)
        hb = _bf(xhat * re[0][...])
        ro[0][...] = hb
        for k, ((off, w), dt) in enumerate(pieces):
            ro[1 + k][...] = _dot_nt(hb, re[1][off:off + w, :]).astype(dt)

    outs = [(D_MODEL, BF16)] + [(w, dt) for ((_, w), dt) in pieces]
    return _row_call(body, tm=256, row_ins=[x], res_ins=[g1, w_in_t], row_outs=outs, name="proj_fwd",
                     exchange=exchange)


def _mix_fwd(attn, ret, z_a, z_r, x, wba, wbr, wout, g2):
    def body(ri, re, ro, po):
        ba = _dot(ri[0][...], re[0][...])
        br = _dot(ri[1][...], re[1][...])
        m = jax.nn.sigmoid(ri[2][...]) * ba + jax.nn.sigmoid(ri[3][...]) * br
        mb = _bf(m)
        x1 = ri[4][...] + _dot(mb, re[2][...])
        r, xhat = _rms_stats(x1)
        ro[0][...] = ba
        ro[1][...] = br
        ro[2][...] = mb
        ro[3][...] = x1
        ro[4][...] = _bf(xhat * re[3][...])

    outs = [(D_MODEL, F32), (D_MODEL, F32), (D_MODEL, BF16), (D_MODEL, F32), (D_MODEL, BF16)]
    return _row_call(body, tm=512, row_ins=[attn, ret, z_a, z_r, x], res_ins=[wba, wbr, wout, g2], row_outs=outs,
                     name="mix_fwd")


def _ffn_fwd_bwd(h2, x1, target, wg_t, wu_t, wd, g2):
    def body(ri, re, ro, po):
        h2_t = ri[0][...]
        x1_t = ri[1][...]
        gate = _dot_nt(h2_t, re[0][...])
        up = _dot_nt(h2_t, re[1][...])
        sg = jax.nn.sigmoid(gate)
        sl = gate * sg
        actb = _bf(sl * up)
        ro[0][...] = actb
        y = x1_t + _dot(actb, re[2][...])
        e = y - ri[2][...]
        po[0][0] = jnp.broadcast_to(0.5 * jnp.sum(jnp.sum(e * e, axis=1, keepdims=True), axis=0, keepdims=True)
                                    * (1.0 / D_MODEL), (1, 128))
        dy = e * (1.0 / D_MODEL)
        dyb = _bf(dy)
        ro[3][...] = dyb
        dact = _dot_nt(dyb, re[2][...])
        dupb = _bf(dact * sl)
        dgateb = _bf(dact * up * (sg * (1.0 + gate * (1.0 - sg))))
        ro[1][...] = dgateb
        ro[2][...] = dupb
        dh2 = _dot(dgateb, re[0][...]) + _dot(dupb, re[1][...])
        r, xhat = _rms_stats(x1_t)
        dxn, dgain = _rms_bwd(dh2, xhat, r, re[3][...])
        dx1 = dy + dxn
        ro[4][...] = dx1
        ro[5][...] = _bf(dx1)
        po[1][0] = jnp.sum(dgain, axis=0, keepdims=True)

    outs = [(D_FF, BF16), (D_FF, BF16), (D_FF, BF16), (D_MODEL, BF16), (D_MODEL, F32), (D_MODEL, BF16)]
    return _row_call(body, tm=256, row_ins=[h2, x1, target], res_ins=[wg_t, wu_t, wd, g2], row_outs=outs,
                     part_outs=(128, D_MODEL), name="ffn_fwd_bwd")


def _mix_bwd(dx1b, z_a, z_r, ba, br, g_r, o_ret, wout, wba, wbr, exchange):
    def body(ri, re, ro, po):
        dm = _dot_nt(ri[0][...], re[0][...])
        sa = jax.nn.sigmoid(ri[1][...])
        sr = jax.nn.sigmoid(ri[2][...])
        dbab = _bf(sa * dm)
        dbrb = _bf(sr * dm)
        ro[0][...] = dbab
        ro[1][...] = dbrb
        ro[2][...] = _bf(dm * ri[3][...] * (sa * (1.0 - sa)))
        ro[3][...] = _bf(dm * ri[4][...] * (sr * (1.0 - sr)))
        ro[4][...] = _bf(_dot_nt(dbab, re[1][...]))
        dret = _dot_nt(dbrb, re[2][...])
        for h in range(RET_HEADS):
            cols = slice(h * RET_V_DIM, (h + 1) * RET_V_DIM)
            g = ri[5][:, cols]
            r, rn = _rms_stats(ri[6][:, cols])
            sg = jax.nn.sigmoid(g)
            dret_h = dret[:, cols]
            d_rn = dret_h * (g * sg)
            ro[6][:, cols] = _bf(dret_h * rn * (sg * (1.0 + g * (1.0 - sg))))
            ro[5][:, cols] = r * (d_rn - rn * jnp.mean(d_rn * rn, axis=-1, keepdims=True))

    outs = [(D_MODEL, BF16), (D_MODEL, BF16), (D_MODEL, BF16), (D_MODEL, BF16), (ATT_Q, BF16), (RET_V, F32),
            (RET_V, BF16)]
    return _row_call(body, tm=256, row_ins=[dx1b, z_a, z_r, ba, br, g_r, o_ret], res_ins=[wout, wba, wbr],
                     row_outs=outs, name="mix_bwd", exchange=exchange)


def _proj_bwd(d_pieces, x, dx1, w_in_t, g1, exchange):
    groups = (P_QA, P_KVA, P_QR, P_KR, P_VR, P_GR, P_ZA, P_ZR)
    n_p = len(groups)

    def body(ri, re, ro, po):
        dh = None
        for k, (off, w) in enumerate(groups):
            term = _dot(ri[k][...], re[0][off:off + w, :])
            dh = term if dh is None else dh + term
        r, xhat = _rms_stats(ri[n_p][...])
        dxn, dgain = _rms_bwd(dh, xhat, r, re[1][...])
        ro[0][...] = ri[n_p + 1][...] + dxn
        po[0][0] = jnp.sum(dgain, axis=0, keepdims=True)

    return _row_call(body, tm=512, row_ins=[*d_pieces, x, dx1], res_ins=[w_in_t, g1], row_outs=[(D_MODEL, F32)],
                     part_outs=(D_MODEL,), name="proj_bwd", exchange=exchange)


def _dw(a, b, *, tm, place, buf, name, exchange=None):
    t, m = a.shape
    n = b.shape[1]
    tk = min(2048, t)
    n_i, n_k = m // tm, t // tk
    fresh = isinstance(buf, jax.ShapeDtypeStruct)
    n_copies = len(place(0))

    def kern(a_ref, b_ref, *rest):
        out_ref, acc, sems = rest[-3:]
        i, k = pl.program_id(0), pl.program_id(1)
        part = _dot_tn(a_ref[...], b_ref[...])

        @pl.when(k == 0)
        def _():
            acc[i] = part

        @pl.when(k > 0)
        def _():
            acc[i] += part

        def copies(tile):
            return [pltpu.make_async_copy(acc.at[tile, pl.ds(r0, rows), :], out_ref.at[idx], sems.at[tile * n_copies + c])
                    for c, (r0, rows, idx) in enumerate(place(tile))]

        for tile in range(n_i):
            @pl.when((i == tile) & (k == n_k - 1))
            def _(tile=tile):
                for cp in copies(tile):
                    cp.start()

        @pl.when((i == n_i - 1) & (k == n_k - 1))
        def _():
            for tile in range(n_i):
                for cp in copies(tile):
                    cp.wait()

    in_specs = [pl.BlockSpec((tk, tm), lambda i, k: (k, i)), pl.BlockSpec((tk, n), lambda i, k: (k, 0))]
    shape = buf if fresh else jax.ShapeDtypeStruct(buf.shape, buf.dtype)
    return _pallas(
        kern, grid=(n_i, n_k), in_specs=in_specs + ([] if fresh else [_ANY]), out_specs=[_ANY], out_shape=[shape],
        scratch=[pltpu.VMEM((n_i, tm, n), F32), pltpu.SemaphoreType.DMA((n_i * n_copies,))],
        args=[a, b] + ([] if fresh else [buf]), aliases=None if fresh else {2: 0}, name=name, exchange=exchange)


def _heads_to_lanes(x3):
    return jnp.concatenate([x3[g] for g in range(GROUP)], axis=1)


def _lanes_to_heads(xt):
    return jnp.concatenate([xt[:, g * BLOCK:(g + 1) * BLOCK] for g in range(GROUP)], axis=0)


def _attn_group(n, kvh, q_ref, kvp_ref, kvc_ref, gq_col, gk, sink_ref):
    heads = [kvh * GROUP + g for g in range(GROUP)]
    cols = slice(kvh * GROUP * HEAD_DIM, (kvh + 1) * GROUP * HEAD_DIM)
    q3 = q_ref[:, cols].T.reshape(GROUP, HEAD_DIM, BLOCK)
    rq = lax.rsqrt(jnp.mean(q3 * q3, axis=1, keepdims=True) + EPS)
    qhat = q3 * rq
    qts = _heads_to_lanes(_bf(qhat * (gq_col * (HEAD_DIM ** -0.5))))
    kcols = slice(kvh * HEAD_DIM, (kvh + 1) * HEAD_DIM)
    vcols = slice(ATT_KV + kvh * HEAD_DIM, ATT_KV + (kvh + 1) * HEAD_DIM)
    k = jnp.concatenate([kvp_ref[:, kcols], kvc_ref[:, kcols]], axis=0)
    rk, khat = _rms_stats(k)
    knb = _bf(khat * gk)
    st = _dot(knb, qts)
    j = lax.broadcasted_iota(jnp.int32, (BLOCK, GROUP * BLOCK), 0)
    i = lax.broadcasted_iota(jnp.int32, (BLOCK, GROUP * BLOCK), 1) & (BLOCK - 1)
    from_prev = j > i
    f = jnp.where(from_prev, jnp.where(n > 0, st[0:BLOCK], -1e30), st[BLOCK:2 * BLOCK])
    sink = jnp.concatenate([jnp.broadcast_to(sink_ref[0:1, h:h + 1], (1, BLOCK)) for h in heads], axis=1)
    m = jnp.maximum(jnp.max(f, axis=0, keepdims=True), sink)
    e = jnp.exp(f - m)
    es = jnp.exp(sink - m)
    inv = 1.0 / (jnp.sum(e, axis=0, keepdims=True) + es)
    return dict(heads=heads, qhat=qhat, rq=rq, qts=qts, khat=khat, rk=rk, knb=knb, from_prev=from_prev,
                pf=e * inv, psink=es * inv)


def _unfold(from_prev, xf):
    return _bf(jnp.concatenate([jnp.where(from_prev, xf, 0.0), jnp.where(from_prev, 0.0, xf)], axis=0))


def _attn_fwd(q_a, kv_a, gq_col, gk, sinks):
    t = q_a.shape[0]
    nb = t // BLOCK

    def kern(q_ref, kvp_ref, kvc_ref, gq_ref, gk_ref, sink_ref, o_ref):
        n = pl.program_id(0)
        kvt = jnp.concatenate([kvp_ref[...].T, kvc_ref[...].T], axis=1)
        for kvh in range(N_KV_HEADS):
            a = _attn_group(n, kvh, q_ref, kvp_ref, kvc_ref, gq_ref[...], gk_ref[...], sink_ref)
            vt = _bf(kvt[ATT_KV + kvh * HEAD_DIM:ATT_KV + (kvh + 1) * HEAD_DIM, :])
            out_t = _dot(vt, _unfold(a["from_prev"], a["pf"]))
            cols = slice(kvh * GROUP * HEAD_DIM, (kvh + 1) * GROUP * HEAD_DIM)
            o_ref[:, cols] = _bf(_lanes_to_heads(out_t).T)

    small = lambda a: pl.BlockSpec(a.shape, lambda n: (0, 0))
    return dict(
        kern=kern,
        in_specs=[pl.BlockSpec((BLOCK, ATT_Q), lambda n: (n, 0)),
                  pl.BlockSpec((BLOCK, 2 * ATT_KV), lambda n: (jnp.maximum(n - 1, 0), 0)),
                  pl.BlockSpec((BLOCK, 2 * ATT_KV), lambda n: (n, 0)),
                  small(gq_col), small(gk), small(sinks)],
        out_specs=[pl.BlockSpec((BLOCK, ATT_Q), lambda n: (n, 0))],
        out_shape=[jax.ShapeDtypeStruct((t, ATT_Q), BF16)], scratch=[],
        args=[q_a, kv_a, kv_a, gq_col, gk, sinks])


def _attn_bwd(q_a, kv_a, d_attn, gq_col, gk, gk_col, sinks, exchange):
    t = q_a.shape[0]
    nb = t // BLOCK

    def kern(q_ref, kvp_ref, kvc_ref, do_ref, gq_ref, gk_ref, gkc_ref, sink_ref,
             dq_ref, dkv_ref, dgq_ref, dgk_ref, dsink_ref, band_k, band_v, carry_k, carry_v):
        n = pl.program_id(0)
        gq_v = gq_ref[...]
        gk_v = gk_ref[...]

        @pl.when(n == 0)
        def _():
            carry_k[...] = jnp.zeros_like(carry_k)
            carry_v[...] = jnp.zeros_like(carry_v)
            dgq_ref[...] = jnp.zeros_like(dgq_ref)
            dgk_ref[...] = jnp.zeros_like(dgk_ref)
            dsink_ref[...] = jnp.zeros_like(dsink_ref)

        @pl.when(n == nb)
        def _():
            band_k[...] = jnp.zeros_like(band_k)
            band_v[...] = jnp.zeros_like(band_v)

        @pl.when(n < nb)
        def _():
            lane16 = lax.broadcasted_iota(jnp.int32, (1, N_Q_HEADS), 1)
            dsink = jnp.zeros((1, N_Q_HEADS), F32)
            dgq = jnp.zeros((HEAD_DIM, 1), F32)
            gk_col = gkc_ref[...]
            kvt = jnp.concatenate([kvp_ref[...].T, kvc_ref[...].T], axis=1)
            for kvh in range(N_KV_HEADS):
                a = _attn_group(n, kvh, q_ref, kvp_ref, kvc_ref, gq_v, gk_v, sink_ref)
                from_prev, pf, qhat = a["from_prev"], a["pf"], a["qhat"]
                cols = slice(kvh * GROUP * HEAD_DIM, (kvh + 1) * GROUP * HEAD_DIM)
                vcols = slice(ATT_KV + kvh * HEAD_DIM, ATT_KV + (kvh + 1) * HEAD_DIM)
                dot = _heads_to_lanes(_bf(do_ref[:, cols].astype(F32).T.reshape(GROUP, HEAD_DIM, BLOCK)))
                vb = _bf(jnp.concatenate([kvp_ref[:, vcols], kvc_ref[:, vcols]], axis=0))
                dpt = _dot(vb, dot)
                dpf = jnp.where(from_prev, dpt[0:BLOCK], dpt[BLOCK:2 * BLOCK])
                delta = jnp.sum(pf * dpf, axis=0, keepdims=True)
                dst = _unfold(from_prev, pf * (dpf - delta))
                dsk = a["psink"] * delta
                for g, h in enumerate(a["heads"]):
                    tot = jnp.sum(dsk[:, g * BLOCK:(g + 1) * BLOCK], axis=1, keepdims=True)
                    dsink = dsink - jnp.where(lane16 == h, tot, 0.0)
                kt = kvt[kvh * HEAD_DIM:(kvh + 1) * HEAD_DIM, :]
                knt = _bf(kt * lax.rsqrt(jnp.mean(kt * kt, axis=0, keepdims=True) + EPS) * gk_col)
                dqn = (_dot(knt, dst) * (HEAD_DIM ** -0.5))
                band_k[kvh] = _dot_nt(dst, a["qts"])
                band_v[kvh] = _dot_nt(_unfold(from_prev, pf), dot)
                dqn3 = _lanes_to_heads(dqn).reshape(GROUP, HEAD_DIM, BLOCK)
                u = dqn3 * gq_v
                dq3 = a["rq"] * (u - qhat * jnp.mean(u * qhat, axis=1, keepdims=True))
                dgq = dgq + jnp.sum(jnp.sum(dqn3 * qhat, axis=0), axis=1, keepdims=True)
                dq_ref[:, cols] = _bf(dq3.reshape(GROUP * HEAD_DIM, BLOCK).T)
            dsink_ref[...] += dsink
            dgq_ref[...] += dgq

        dgk = jnp.zeros((1, HEAD_DIM), F32)
        for kvh in range(N_KV_HEADS):
            kcols = slice(kvh * HEAD_DIM, (kvh + 1) * HEAD_DIM)
            vcols = slice(ATT_KV + kvh * HEAD_DIM, ATT_KV + (kvh + 1) * HEAD_DIM)
            dkn = carry_k[kvh] + band_k[kvh, 0:BLOCK, :]
            dv = carry_v[kvh] + band_v[kvh, 0:BLOCK, :]
            rk, khat = _rms_stats(kvp_ref[:, kcols])
            dk, dgain = _rms_bwd(dkn, khat, rk, gk_v)
            dgk = dgk + jnp.sum(dgain, axis=0, keepdims=True)
            dkv_ref[:, kcols] = _bf(dk)
            dkv_ref[:, vcols] = _bf(dv)
            carry_k[kvh] = band_k[kvh, BLOCK:2 * BLOCK, :]
            carry_v[kvh] = band_v[kvh, BLOCK:2 * BLOCK, :]
        dgk_ref[...] += dgk

    small = lambda a: pl.BlockSpec(a.shape, lambda n: (0, 0))
    last = nb - 1
    return _pallas(
        kern, grid=(nb + 1,),
        in_specs=[pl.BlockSpec((BLOCK, ATT_Q), lambda n: (jnp.minimum(n, last), 0)),
                  pl.BlockSpec((BLOCK, 2 * ATT_KV), lambda n: (jnp.maximum(n - 1, 0), 0)),
                  pl.BlockSpec((BLOCK, 2 * ATT_KV), lambda n: (jnp.minimum(n, last), 0)),
                  pl.BlockSpec((BLOCK, ATT_Q), lambda n: (jnp.minimum(n, last), 0)),
                  small(gq_col), small(gk), small(gk_col), small(sinks)],
        out_specs=[pl.BlockSpec((BLOCK, ATT_Q), lambda n: (jnp.minimum(n, last), 0)),
                   pl.BlockSpec((BLOCK, 2 * ATT_KV), lambda n: (jnp.maximum(n - 1, 0), 0)),
                   pl.BlockSpec((HEAD_DIM, 1), lambda n: (0, 0)),
                   pl.BlockSpec((1, HEAD_DIM), lambda n: (0, 0)),
                   pl.BlockSpec((1, N_Q_HEADS), lambda n: (0, 0))],
        out_shape=[jax.ShapeDtypeStruct((t, ATT_Q), BF16), jax.ShapeDtypeStruct((t, 2 * ATT_KV), BF16),
                   jax.ShapeDtypeStruct((HEAD_DIM, 1), F32), jax.ShapeDtypeStruct((1, HEAD_DIM), F32),
                   jax.ShapeDtypeStruct((1, N_Q_HEADS), F32)],
        scratch=[pltpu.VMEM((N_KV_HEADS, 2 * BLOCK, HEAD_DIM), F32),
                 pltpu.VMEM((N_KV_HEADS, 2 * BLOCK, HEAD_DIM), F32),
                 pltpu.VMEM((N_KV_HEADS, BLOCK, HEAD_DIM), F32),
                 pltpu.VMEM((N_KV_HEADS, BLOCK, HEAD_DIM), F32)],
        args=[q_a, kv_a, kv_a, d_attn, gq_col, gk, gk_col, sinks], name="attn_bwd", exchange=exchange)


def _ret_tables(t, exchange):
    theta = 1.0 / (RET_ROT_BASE ** jnp.linspace(0.0, 1.0, RET_QK_DIM // 2, dtype=F32))
    theta2 = jnp.repeat(theta, 2)[None, :]
    sign = jnp.tile(jnp.array([-1.0, 1.0], F32), RET_QK_DIM // 2)[None, :]

    def kern(theta_ref, sign_ref, cos_ref, sin_ref):
        first = pl.program_id(0) * RET_CHUNK
        pos = (first + lax.broadcasted_iota(jnp.int32, (RET_CHUNK, RET_QK_DIM), 0)).astype(F32)
        ang = pos * theta_ref[...]
        cos_ref[...] = jnp.cos(ang)
        sin_ref[...] = jnp.sin(ang) * sign_ref[...]

    row = pl.BlockSpec((1, RET_QK_DIM), lambda n: (0, 0))
    blk = pl.BlockSpec((RET_CHUNK, RET_QK_DIM), lambda n: (n, 0))
    cos, sin_s, *got = _pallas(kern, grid=(t // RET_CHUNK,), in_specs=[row, row], out_specs=[blk, blk],
                               out_shape=[jax.ShapeDtypeStruct((t, RET_QK_DIM), F32)] * 2, args=[theta2, sign],
                               name="position_tables", exchange=exchange)
    log_gamma = jnp.log(1.0 - 2.0 ** (-5.0 - jnp.arange(RET_HEADS, dtype=F32)))
    i = jnp.arange(RET_CHUNK, dtype=F32)
    diff = i[:, None] - i[None, :]
    causal = diff >= 0
    decay = jnp.where(causal[None], jnp.exp(jnp.where(causal, diff, 0.0)[None] * log_gamma[:, None, None]), 0.0)
    xi = jnp.exp((i + 1.0)[None, :] * log_gamma[:, None])[:, :, None]
    zeta = jnp.exp((RET_CHUNK - 1.0 - i)[None, :] * log_gamma[:, None])[:, :, None]
    gch = jnp.broadcast_to(jnp.exp(RET_CHUNK * log_gamma)[:, None, None], (RET_HEADS, 1, 128))
    return (cos, sin_s, decay, xi, zeta, gch), got


def _swap_pairs(x):
    lane = lax.broadcasted_iota(jnp.int32, x.shape, 1)
    return jnp.where((lane & 1) == 0, pltpu.roll(x, RET_QK_DIM - 1, 1), pltpu.roll(x, 1, 1))


def _rotate(x, cos, sin_s):
    return x * cos + _swap_pairs(x) * sin_s


def _rotate_bwd(dy, cos, sin_s):
    return dy * cos + _swap_pairs(dy * sin_s)


def _ret_specs(order):
    qk = pl.BlockSpec((RET_CHUNK, RET_QK), lambda j: (order(j), 0))
    v = pl.BlockSpec((RET_CHUNK, RET_V), lambda j: (order(j), 0))
    dec = pl.BlockSpec((RET_HEADS, RET_CHUNK, RET_CHUNK), lambda j: (0, 0, 0))
    col = pl.BlockSpec((RET_HEADS, RET_CHUNK, 1), lambda j: (0, 0, 0))
    gch = pl.BlockSpec((RET_HEADS, 1, 128), lambda j: (0, 0, 0))
    st = pl.BlockSpec((RET_HEADS, None, RET_QK_DIM, RET_V_DIM), lambda j: (0, order(j), 0, 0))
    pos = pl.BlockSpec((RET_CHUNK, RET_QK_DIM), lambda j: (order(j), 0))
    return qk, v, dec, col, gch, st, pos


def _ret_fwd(q_r, k_r, v_r, g_r, tables):
    t = q_r.shape[0]
    nc = t // RET_CHUNK
    cos, sin_s, decay, xi, zeta, gch = tables

    def kern(q_ref, k_ref, v_ref, g_ref, cos_ref, sin_ref, dec_ref, xi_ref, zeta_ref, gch_ref,
             o_ref, ret_ref, st_ref, state):
        @pl.when(pl.program_id(0) == 0)
        def _():
            state[...] = jnp.zeros_like(state)

        cos_t = cos_ref[...]
        sin_t = sin_ref[...]
        for h in range(RET_HEADS):
            qc = slice(h * RET_QK_DIM, (h + 1) * RET_QK_DIM)
            vc = slice(h * RET_V_DIM, (h + 1) * RET_V_DIM)
            qs = _bf(_rotate(q_ref[:, qc], cos_t, sin_t))
            ks = _rotate(k_ref[:, qc] * (RET_QK_DIM ** -0.5), cos_t, sin_t)
            vb = v_ref[:, vc]
            s_old = state[h]
            sb = _bf(s_old)
            st_ref[h] = sb
            inner = _dot_nt(qs, _bf(ks)) * dec_ref[h]
            out = _dot(_bf(inner), vb) + _dot(qs, sb) * xi_ref[h]
            state[h] = gch_ref[h, :, 0:1] * s_old + _dot_tn(_bf(ks * zeta_ref[h]), vb)
            o_ref[:, vc] = out
            r, rn = _rms_stats(out)
            g = g_ref[:, vc]
            ret_ref[:, vc] = _bf(g * jax.nn.sigmoid(g) * rn)

    qk, v, dec, col, gsp, st, pos = _ret_specs(lambda j: j)
    return dict(
        kern=kern,
        in_specs=[qk, qk, v, v, pos, pos, dec, col, col, gsp],
        out_specs=[v, v, st],
        out_shape=[jax.ShapeDtypeStruct((t, RET_V), F32), jax.ShapeDtypeStruct((t, RET_V), BF16),
                   jax.ShapeDtypeStruct((RET_HEADS, nc, RET_QK_DIM, RET_V_DIM), BF16)],
        scratch=[pltpu.VMEM((RET_HEADS, RET_QK_DIM, RET_V_DIM), F32)],
        args=[q_r, k_r, v_r, g_r, cos, sin_s, decay, xi, zeta, gch])


def _ret_bwd(q_r, k_r, v_r, d_o, states, tables, exchange):
    t = q_r.shape[0]
    nc = t // RET_CHUNK
    cos, sin_s, decay, xi, zeta, gch = tables

    def kern(q_ref, k_ref, v_ref, do_ref, st_ref, cos_ref, sin_ref, dec_ref, xi_ref, zeta_ref, gch_ref,
             dq_ref, dk_ref, dv_ref, dstate):
        @pl.when(pl.program_id(0) == 0)
        def _():
            dstate[...] = jnp.zeros_like(dstate)

        cos_t = cos_ref[...]
        sin_t = sin_ref[...]
        scale = RET_QK_DIM ** -0.5
        for h in range(RET_HEADS):
            qc = slice(h * RET_QK_DIM, (h + 1) * RET_QK_DIM)
            vc = slice(h * RET_V_DIM, (h + 1) * RET_V_DIM)
            qs = _bf(_rotate(q_ref[:, qc], cos_t, sin_t))
            ks = _rotate(k_ref[:, qc] * scale, cos_t, sin_t)
            ksb = _bf(ks)
            vb = v_ref[:, vc]
            d_o_t = do_ref[:, vc]
            dob = _bf(d_o_t)
            doxb = _bf(d_o_t * xi_ref[h])
            dec = dec_ref[h]
            ds_old = dstate[h]
            dsb = _bf(ds_old)
            pb = _bf(_dot_nt(qs, ksb) * dec)
            dpb = _bf(_dot_nt(dob, vb) * dec)
            dqs = _dot(dpb, ksb) + _dot_nt(doxb, st_ref[h])
            dks = _dot_tn(dpb, qs) + _dot_nt(vb, dsb) * zeta_ref[h]
            dv_ref[:, vc] = _bf(_dot_tn(pb, dob) + _dot(_bf(ks * zeta_ref[h]), dsb))
            dstate[h] = gch_ref[h, :, 0:1] * ds_old + _dot_tn(qs, doxb)
            dq_ref[:, qc] = _bf(_rotate_bwd(dqs, cos_t, sin_t))
            dk_ref[:, qc] = _bf(_rotate_bwd(dks, cos_t, sin_t) * scale)

    qk, v, dec, col, gsp, st, pos = _ret_specs(lambda j: nc - 1 - j)
    return _pallas(
        kern, grid=(nc,),
        in_specs=[qk, qk, v, v, st, pos, pos, dec, col, col, gsp],
        out_specs=[qk, qk, v],
        out_shape=[jax.ShapeDtypeStruct((t, RET_QK), BF16), jax.ShapeDtypeStruct((t, RET_QK), BF16),
                   jax.ShapeDtypeStruct((t, RET_V), BF16)],
        scratch=[pltpu.VMEM((RET_HEADS, RET_QK_DIM, RET_V_DIM), F32)],
        args=[q_r, k_r, v_r, d_o, states, cos, sin_s, decay, xi, zeta, gch], name="ret_bwd", exchange=exchange)


def _position():
    return lax.axis_index("x"), lax.axis_index("y"), lax.axis_index("c")


def _gather_exchange(owns, forward_at):
    n = len(owns)

    def copies(ins, outs, send_sems, recv_sems, base):
        x, y, c = _position()
        sibling = (x, y, 1 - c)
        chips = [(1 - x, y), (x, 1 - y), (1 - x, 1 - y)]
        my_chip = 2 * x + y

        def slab(a, chip, hf):
            half = owns[a].shape[0] // 2
            return outs[a].at[chip, pl.ds(hf * half, half), :]

        def copy(k, src, dst, to):
            return pltpu.make_async_remote_copy(src_ref=src, dst_ref=dst, send_sem=send_sems.at[base + k],
                                                recv_sem=recv_sems.at[base + k], device_id=to, device_id_type=MESH)

        first, passed, from_sibling = [], [], []
        for a in range(n):
            half = owns[a].shape[0] // 2
            for k, (cx, cy) in enumerate(chips):
                first.append(copy(6 * a + k, ins[a].at[pl.ds(c * half, half), :], slab(a, my_chip, c), (cx, cy, c)))
                landed = slab(a, 2 * cx + cy, c)
                passed.append(copy(6 * a + 3 + k, landed, landed, sibling))
                theirs = slab(a, 2 * cx + cy, 1 - c)
                from_sibling.append(copy(6 * a + 3 + k, theirs, theirs, sibling))
        return first, passed, from_sibling

    def start(*args):
        first, _, _ = copies(*args)
        for cp in first:
            cp.start()

    def forward(*args):
        first, passed, _ = copies(*args)
        for arrived, cp in zip(first, passed):
            arrived.wait_recv()
            cp.start()

    def finish(*args):
        first, passed, from_sibling = copies(*args)
        for cp in from_sibling:
            cp.wait_recv()
        for cp in first + passed:
            cp.wait_send()

    outs = [jax.ShapeDtypeStruct((N_CHIPS, *a.shape), a.dtype) for a in owns]
    return _Exchange(owns, outs, 6 * n, [(0.0, start), (forward_at, forward), (1.0, finish)])


def _symmetric_exchange(ins, outs, plan):
    n_sems = len(plan([None] * len(ins), [None] * len(outs), 0, 0, 0, dry=True))

    def copies(in_refs, out_refs, send_sems, recv_sems, base):
        x, y, c = _position()
        return [pltpu.make_async_remote_copy(src_ref=src, dst_ref=dst, send_sem=send_sems.at[base + k],
                                             recv_sem=recv_sems.at[base + k], device_id=dev, device_id_type=MESH)
                for k, (src, dst, dev) in enumerate(plan(in_refs, out_refs, x, y, c, dry=False))]

    def start(*args):
        for cp in copies(*args):
            cp.start()

    def finish(*args):
        for cp in copies(*args):
            cp.wait()

    return _Exchange(ins, outs, n_sems, [(0.0, start), (1.0, finish)])


def _pair_exchange(gs):
    def plan(in_refs, out_refs, x, y, c, dry):
        out = []
        for a, g in enumerate(gs):
            half = g.shape[1] // 2
            for k in range(N_CHIPS):
                out.append(None if dry else (in_refs[a].at[k, pl.ds((1 - c) * half, half), :], out_refs[a].at[k],
                                             (x, y, 1 - c)))
        return out

    outs = [jax.ShapeDtypeStruct((g.shape[0], g.shape[1] // 2, g.shape[2]), g.dtype) for g in gs]
    return _symmetric_exchange(gs, outs, plan)


def _pair_sum(g, from_sibling, c_arr, *, tile, name):
    n, rows, width = g.shape
    tiles = (rows // 2) // tile

    def kern(c_ref, g_ref, s_ref, o_ref):
        o_ref[...] = _bf(g_ref[...] + s_ref[...])

    return pl.pallas_call(
        kern,
        grid_spec=pltpu.PrefetchScalarGridSpec(
            num_scalar_prefetch=1, grid=(n, tiles),
            in_specs=[pl.BlockSpec((None, tile, width), lambda k, i, c: (k, c[0] * tiles + i, 0)),
                      pl.BlockSpec((None, tile, width), lambda k, i, c: (k, i, 0))],
            out_specs=pl.BlockSpec((None, tile, width), lambda k, i, c: (k, i, 0))),
        out_shape=jax.ShapeDtypeStruct((n, rows // 2, width), BF16), name=name,
        compiler_params=_params(("parallel", "parallel")),
    )(c_arr, g, from_sibling)


def _scatter_to_owners(hsums):
    def plan(in_refs, out_refs, x, y, c, dry):
        out = []
        for a in range(len(hsums)):
            for k, (cx, cy) in enumerate([(1 - x, y), (x, 1 - y), (1 - x, 1 - y)]):
                out.append(None if dry else (in_refs[a].at[2 * cx + cy], out_refs[a].at[k], (cx, cy, c)))
        return out

    outs = [jax.ShapeDtypeStruct((3, *h.shape[1:]), h.dtype) for h in hsums]
    return _symmetric_exchange(hsums, outs, plan)


def _sum_chips(hsum, parts, chip_arr, *, tile, name):
    n, half, width = parts.shape

    def kern(chip_ref, h_ref, p_ref, o_ref):
        acc = h_ref[...].astype(F32)
        for k in range(n):
            acc = acc + p_ref[k].astype(F32)
        o_ref[...] = acc

    return pl.pallas_call(
        kern,
        grid_spec=pltpu.PrefetchScalarGridSpec(
            num_scalar_prefetch=1, grid=(half // tile,),
            in_specs=[pl.BlockSpec((None, tile, width), lambda i, chip: (chip[0], i, 0)),
                      pl.BlockSpec((n, tile, width), lambda i, chip: (0, i, 0))],
            out_specs=pl.BlockSpec((tile, width), lambda i, chip: (i, 0))),
        out_shape=jax.ShapeDtypeStruct((half, width), F32), name=name,
        compiler_params=_params(("parallel",)),
    )(chip_arr, hsum, parts)


def _share_halves(fhalves):
    def plan(in_refs, out_refs, x, y, c, dry):
        return [None if dry else (in_refs[a], out_refs[a], (x, y, 1 - c)) for a in range(len(fhalves))]

    return _symmetric_exchange(fhalves, [jax.ShapeDtypeStruct(f.shape, f.dtype) for f in fhalves], plan)


def _adamw_math(w, g, m, v):
    m = ADAM_B1 * m + (1.0 - ADAM_B1) * g
    v = ADAM_B2 * v + (1.0 - ADAM_B2) * (g * g)
    m_hat = m / (1.0 - ADAM_B1 ** ADAM_STEP)
    v_hat = v / (1.0 - ADAM_B2 ** ADAM_STEP)
    delta = -ADAM_LR * (m_hat / (jnp.sqrt(v_hat) + ADAM_EPS) + ADAM_WD * w)
    return delta, m, v


def _adamw(mats, g_mine, g_other, c_arr, *, tile, name):
    width = g_mine.shape[1]
    tiles_per_half = g_mine.shape[0] // tile
    n_tiles = [w.shape[0] // tile for w, _, _, _ in mats]
    n_mats = len(mats)

    def kern(c_ref, *refs):
        ins, outs = refs[:5 * n_mats], refs[5 * n_mats:]
        for j, (_, _, _, row_off) in enumerate(mats):
            w_ref, gm_ref, go_ref, m_ref, v_ref = ins[5 * j:5 * j + 5]
            i = jnp.minimum(pl.program_id(0), n_tiles[j] - 1)
            in_my_half = ((row_off // tile + i) // tiles_per_half) == c_ref[0]
            g = jnp.where(in_my_half, gm_ref[...], go_ref[...])
            d, nm, nv = _adamw_math(w_ref[...], g, m_ref[...], v_ref[...])
            for out_ref, val in zip(outs[4 * j:4 * j + 4], (g, d, nm, nv)):
                out_ref[...] = val

    in_specs, out_specs, out_shape, args = [], [], [], []
    for (w, m, v, row_off), nt in zip(mats, n_tiles):
        full = pl.BlockSpec((tile, width), lambda i, c, nt=nt: (jnp.minimum(i, nt - 1), 0))
        half = pl.BlockSpec((tile, width), lambda i, c, nt=nt, first=row_off // tile:
                            ((first + jnp.minimum(i, nt - 1)) % tiles_per_half, 0))
        in_specs += [full, half, half, full, full]
        out_specs += [full] * 4
        out_shape += [jax.ShapeDtypeStruct(w.shape, F32)] * 4
        args += [w, g_mine, g_other, m, v]
    outs = pl.pallas_call(
        kern,
        grid_spec=pltpu.PrefetchScalarGridSpec(num_scalar_prefetch=1, grid=(max(n_tiles),), in_specs=in_specs,
                                               out_specs=out_specs),
        out_shape=out_shape, name=name, compiler_params=_params(("arbitrary",)),
    )(c_arr, *args)
    return [outs[4 * j:4 * j + 4] for j in range(n_mats)]


def _small_step(partials, params):
    slots = ((0, 0, D_MODEL), (1, 0, D_MODEL), (2, 0, HEAD_DIM), (2, 128, HEAD_DIM), (2, 256, N_Q_HEADS))
    loss_slot = (2, 384, 128)

    def body(*refs):
        loss_ref, dg1_ref, dg2_ref, dgq_ref, dgk_ref, dsink_ref = refs[:6]
        p_refs, out_refs = refs[6:21], refs[21:42]
        mine, gathered, send_sems, recv_sems = refs[42:]
        x, y, c = _position()
        me = 4 * x + 2 * y + c
        mine[...] = jnp.zeros_like(mine)
        for (row, lane, n), val in zip(slots + (loss_slot,), (
                jnp.sum(dg1_ref[...], axis=0, keepdims=True), jnp.sum(dg2_ref[...], axis=0, keepdims=True),
                dgq_ref[...], dgk_ref[...], dsink_ref[...], jnp.sum(loss_ref[...], axis=0, keepdims=True))):
            mine[row:row + 1, lane:lane + n] = val
        copies = []
        for k in range(1, N_DEV):
            flip = (k >> 2) & 1, (k >> 1) & 1, k & 1
            to = (x ^ flip[0], y ^ flip[1], c ^ flip[2])
            cp = pltpu.make_async_remote_copy(
                src_ref=mine, dst_ref=gathered.at[me], send_sem=send_sems.at[k - 1], recv_sem=recv_sems.at[k - 1],
                device_id=to, device_id_type=MESH)
            cp.start()
            copies.append(cp)
        gathered[me] = mine[...]
        for k in range(1, N_DEV):
            flip = (k >> 2) & 1, (k >> 1) & 1, k & 1
            src = 4 * (x ^ flip[0]) + 2 * (y ^ flip[1]) + (c ^ flip[2])
            pltpu.make_async_remote_copy(
                src_ref=mine, dst_ref=gathered.at[src], send_sem=send_sems.at[k - 1], recv_sem=recv_sems.at[k - 1],
                device_id=(x, y, c), device_id_type=MESH).wait_recv()
        for cp in copies:
            cp.wait_send()
        total = gathered[0]
        for k in range(1, N_DEV):
            total = total + gathered[k]
        row, lane, n = loss_slot
        out_refs[0][...] = total[row:row + 1, lane:lane + n]
        for i, (row, lane, n) in enumerate(slots):
            g = total[row:row + 1, lane:lane + n]
            d, nm, nv = _adamw_math(p_refs[i][...], g, p_refs[5 + i][...], p_refs[10 + i][...])
            for kind, val in enumerate((g, d, nm, nv)):
                out_refs[1 + 5 * kind + i][...] = val

    vm = pl.BlockSpec(memory_space=pltpu.VMEM)
    shapes = [jax.ShapeDtypeStruct((1, 128), F32)] + [jax.ShapeDtypeStruct((1, n), F32) for _, _, n in slots] * 4
    return pl.pallas_call(
        body, in_specs=[vm] * 21, out_specs=[vm] * 21, out_shape=shapes,
        scratch_shapes=[pltpu.VMEM((SMALL_ROWS, D_MODEL), F32), pltpu.VMEM((N_DEV, SMALL_ROWS, D_MODEL), F32),
                        pltpu.SemaphoreType.DMA((N_DEV - 1,)), pltpu.SemaphoreType.DMA((N_DEV - 1,))],
        name="small_step",
    )(*partials, *params)


def _with_own(gathered, own, my_chip):
    return lax.dynamic_update_slice(gathered, own[None], (my_chip, 0, 0))


def kernel(x, norm_mix_gain, w_in, q_norm_gain, k_norm_gain, attn_sinks, w_branch_attn, w_branch_ret, w_out, norm_ffn_gain, w_ffn_gate, w_ffn_up, w_ffn_down, loss_target, m_norm_mix_gain, m_w_in, m_q_norm_gain, m_k_norm_gain, m_attn_sinks, m_w_branch_attn, m_w_branch_ret, m_w_out, m_norm_ffn_gain, m_w_ffn_gate, m_w_ffn_up, m_w_ffn_down, v_norm_mix_gain, v_w_in, v_q_norm_gain, v_k_norm_gain, v_attn_sinks, v_w_branch_attn, v_w_branch_ret, v_w_out, v_norm_ffn_gain, v_w_ffn_gate, v_w_ffn_up, v_w_ffn_down):
    my_chip = 2 * lax.axis_index("x") + lax.axis_index("y")
    c_arr = lax.axis_index("c").astype(jnp.int32).reshape(1)
    chip_arr = my_chip.astype(jnp.int32).reshape(1)
    x_t, target = x[0], loss_target[0]
    g1, g2, gq, gk, sinks = norm_mix_gain, norm_ffn_gain, q_norm_gain, k_norm_gain, attn_sinks

    tr = lambda a: jnp.transpose(a[0])
    own_w_in = _bf(tr(w_in))
    own_ffn = _bf(jnp.concatenate([tr(w_ffn_gate), tr(w_ffn_up), w_ffn_down[0]], axis=0))
    own_mix = _bf(jnp.concatenate([w_branch_attn[0], w_branch_ret[0], w_out[0]], axis=0))
    tables, (got_w_in,) = _ret_tables(x_t.shape[0], _gather_exchange([own_w_in], 0.9))
    w_in_t = _with_own(got_w_in, own_w_in, my_chip).reshape(D_IN, D_MODEL)
    (h1, q_a, kv_a, q_r, k_r, v_r, g_r, z_a, z_r, got_ffn, got_mix) = _proj_fwd(
        x_t, g1, w_in_t, _gather_exchange([own_ffn, own_mix], 0.8))
    all_ffn = _with_own(got_ffn, own_ffn, my_chip)
    all_mix = _with_own(got_mix, own_mix, my_chip)
    wg_t = all_ffn[:, 0:FF_SH].reshape(D_FF, D_MODEL)
    wu_t = all_ffn[:, FF_SH:2 * FF_SH].reshape(D_FF, D_MODEL)
    wd = all_ffn[:, 2 * FF_SH:3 * FF_SH].reshape(D_FF, D_MODEL)
    wba = all_mix[:, 0:256].reshape(ATT_Q, D_MODEL)
    wbr = all_mix[:, 256:768].reshape(RET_V, D_MODEL)
    wout = all_mix[:, 768:1024].reshape(D_MODEL, D_MODEL)

    gq_col, gk_col = gq.reshape(HEAD_DIM, 1), gk.reshape(HEAD_DIM, 1)
    attn, o_ret, ret, states = _fused([_attn_fwd(q_a, kv_a, gq_col, gk, sinks), _ret_fwd(q_r, k_r, v_r, g_r, tables)],
                                      grid=(x_t.shape[0] // BLOCK,), name="mixers_fwd")
    ba, br, merged, x1, h2 = _mix_fwd(attn, ret, z_a, z_r, x_t, wba, wbr, wout, g2)
    act, dgate, dup, dyb, dx1, dx1b, loss_p, dg2_p = _ffn_fwd_bwd(h2, x1, target, wg_t, wu_t, wd, g2)

    def pairs(row0, rows):
        return lambda i: [(h * rows, rows, (2 * i + h, pl.ds(row0, rows), slice(None))) for h in range(2)]

    f_block = jax.ShapeDtypeStruct((N_CHIPS, 3 * FF_SH, D_MODEL), F32)
    f_block, = _dw(dgate, h2, tm=2 * FF_SH, place=pairs(0, FF_SH), buf=f_block, name="dw_gate")
    f_block, = _dw(dup, h2, tm=2 * FF_SH, place=pairs(FF_SH, FF_SH), buf=f_block, name="dw_up")
    f_block, = _dw(act, dyb, tm=2 * FF_SH, place=pairs(2 * FF_SH, FF_SH), buf=f_block, name="dw_down")
    (dba, dbr, dz_a, dz_r, d_attn, d_o, dg_r, sib_ffn) = _mix_bwd(
        dx1b, z_a, z_r, ba, br, g_r, o_ret, wout, wba, wbr, _pair_exchange([f_block]))
    f_sum = _pair_sum(f_block, sib_ffn, c_arr, tile=528, name="pair_sum_ffn")

    def quarters(row0, rows):
        return lambda i: [(k * rows, rows, (k, pl.ds(row0, rows), slice(None))) for k in range(N_CHIPS)]

    m_block = jax.ShapeDtypeStruct((N_CHIPS, D_MODEL, D_MODEL), F32)
    m_block, = _dw(attn, dba, tm=ATT_Q, place=quarters(0, 256), buf=m_block, name="dw_ba")
    m_block, = _dw(ret, dbr, tm=D_MODEL, place=pairs(256, 512), buf=m_block, name="dw_br")
    m_block, = _dw(merged, dx1b, tm=D_MODEL, place=quarters(768, 256), buf=m_block, name="dw_out")

    def w_in_rows(group):
        off, w = group
        tm = min(w, D_MODEL)
        return dict(tm=tm, place=lambda i: [(0, tm, (pl.ds(off + i * tm, tm), slice(None)))])

    w_block = jax.ShapeDtypeStruct((D_IN, D_MODEL), F32)
    w_block, sib_mix = _dw(dg_r, h1, buf=w_block, name="dw_in_5", exchange=_pair_exchange([m_block]), **w_in_rows(P_GR))
    w_block, = _dw(dz_a, h1, buf=w_block, name="dw_in_6", **w_in_rows(P_ZA))
    w_block, = _dw(dz_r, h1, buf=w_block, name="dw_in_7", **w_in_rows(P_ZR))
    m_sum = _pair_sum(m_block, sib_mix, c_arr, tile=256, name="pair_sum_mix")

    dq_r, dk_r, dv_r, got_ffn_sums = _ret_bwd(q_r, k_r, v_r, d_o, states, tables, _scatter_to_owners([f_sum]))
    ffn_half = _sum_chips(f_sum, got_ffn_sums, chip_arr, tile=528, name="sum_chips_ffn")
    w_block, = _dw(dq_r, h1, buf=w_block, name="dw_in_2", **w_in_rows(P_QR))
    w_block, = _dw(dk_r, h1, buf=w_block, name="dw_in_3", **w_in_rows(P_KR))
    w_block, = _dw(dv_r, h1, buf=w_block, name="dw_in_4", **w_in_rows(P_VR))

    (dq_a, dkv_a, dgq, dgk, dsinks, got_mix_sums, ffn_other) = _attn_bwd(
        q_a, kv_a, d_attn, gq_col, gk, gk_col, sinks,
        _merge_exchanges(_scatter_to_owners([m_sum]), _share_halves([ffn_half])))
    dgq = dgq.reshape(1, HEAD_DIM)
    mix_half = _sum_chips(m_sum, got_mix_sums, chip_arr, tile=256, name="sum_chips_mix")
    w_block, mix_other = _dw(dq_a, h1, buf=w_block, name="dw_in_0", exchange=_share_halves([mix_half]),
                             **w_in_rows(P_QA))
    w_block, = _dw(dkv_a, h1, buf=w_block, name="dw_in_1", **w_in_rows(P_KVA))

    w_block = w_block.reshape(N_CHIPS, W_IN_SH, D_MODEL)
    sib_w_in, = _run_exchange(_pair_exchange([w_block]), "pair_exchange_w_in")
    w_sum = _pair_sum(w_block, sib_w_in, c_arr, tile=592, name="pair_sum_w_in")
    d_pieces = [dq_a, dkv_a, dq_r, dk_r, dv_r, dg_r, dz_a, dz_r]
    grad_x, dg1_p, got_w_in_sums = _proj_bwd(d_pieces, x_t, dx1, w_in_t, g1, _scatter_to_owners([w_sum]))
    w_in_half = _sum_chips(w_sum, got_w_in_sums, chip_arr, tile=592, name="sum_chips_w_in")
    w_in_other, = _run_exchange(_share_halves([w_in_half]), "share_halves_w_in")

    def update(name, g_half, g_other, tile, mats):
        outs = _adamw([tuple(tr(a) if t else a[0] for a in wmv) + (off,) for _, *wmv, off, t in mats],
                      g_half, g_other, c_arr, tile=tile, name=f"adamw_{name}")
        return {key: [jnp.transpose(o) if t else o for o in res] for (key, _, _, _, _, t), res in zip(mats, outs)}

    big = {
        **update("w_in", w_in_half, w_in_other, 592, [("w_in", w_in, m_w_in, v_w_in, 0, True)]),
        **update("ffn", ffn_half, ffn_other, 176, [
            ("wg", w_ffn_gate, m_w_ffn_gate, v_w_ffn_gate, 0, True),
            ("wu", w_ffn_up, m_w_ffn_up, v_w_ffn_up, FF_SH, True),
            ("wd", w_ffn_down, m_w_ffn_down, v_w_ffn_down, 2 * FF_SH, False)]),
        **update("mix", mix_half, mix_other, 128, [
            ("wba", w_branch_attn, m_w_branch_attn, v_w_branch_attn, 0, False),
            ("wbr", w_branch_ret, m_w_branch_ret, v_w_branch_ret, 256, False),
            ("wout", w_out, m_w_out, v_w_out, 768, False)])}

    loss_row, *small = _small_step(
        [loss_p.reshape(-1, 128), dg1_p.reshape(-1, D_MODEL), dg2_p.reshape(-1, D_MODEL), dgq, dgk, dsinks],
        [norm_mix_gain, norm_ffn_gain, q_norm_gain, k_norm_gain, attn_sinks,
         m_norm_mix_gain, m_norm_ffn_gain, m_q_norm_gain, m_k_norm_gain, m_attn_sinks,
         v_norm_mix_gain, v_norm_ffn_gain, v_q_norm_gain, v_k_norm_gain, v_attn_sinks])
    loss = loss_row[0, 0]

    def leaves(i):
        b = [big[n][i][None] for n in ("w_in", "wba", "wbr", "wout", "wg", "wu", "wd")]
        s1, s2, sq, sk, ss = small[5 * i:5 * i + 5]
        return [s1, b[0], sq, sk, ss, b[1], b[2], b[3], s2, b[4], b[5], b[6]]

    return (loss, grad_x[None], *leaves(0), *leaves(1), *leaves(2), *leaves(3))
```

```python
import jax
import jax.numpy as jnp
from jax import lax
from jax.experimental import pallas as pl
from jax.experimental.pallas import tpu as pltpu

F32 = jnp.float32
BF16 = jnp.bfloat16
MESH = pl.DeviceIdType.MESH

D_MODEL = 1024
EPS = 1e-6
HEAD_DIM = 64
N_Q_HEADS = 16
N_KV_HEADS = 2
GROUP = 8
BLOCK = 128
RET_HEADS = 4
RET_QK_DIM = 256
RET_V_DIM = 512
RET_CHUNK = 128
RET_ROT_BASE = 10000.0
D_FF = 2816
ATT_Q = N_Q_HEADS * HEAD_DIM
ATT_KV = N_KV_HEADS * HEAD_DIM
RET_QK = RET_HEADS * RET_QK_DIM
RET_V = RET_HEADS * RET_V_DIM
D_IN = 9472
ADAM_LR = 0.001
ADAM_B1 = 0.9
ADAM_B2 = 0.999
ADAM_EPS = 1e-08
ADAM_WD = 0.01
ADAM_STEP = 10

N_CHIPS = 4
N_DEV = 8
VMEM_LIMIT_BYTES = 60 * 1024 * 1024

P_QA = (0, 1024)
P_KVA = (1024, 256)
P_QR = (1280, 1024)
P_KR = (2304, 1024)
P_VR = (3328, 2048)
P_GR = (5376, 2048)
P_ZA = (7424, 1024)
P_ZR = (8448, 1024)

PROJ_BWD_TILE = 512
W_IN_SH = D_IN // N_CHIPS
FF_SH = D_FF // N_CHIPS

SMALL_ROWS = 8


def _dot(a, b):
    return jnp.dot(a, b, preferred_element_type=F32)


def _dot_nt(a, b):
    return lax.dot_general(a, b, (((1,), (1,)), ((), ())), preferred_element_type=F32)


def _dot_tn(a, b):
    return lax.dot_general(a, b, (((0,), (0,)), ((), ())), preferred_element_type=F32)


def _bf(x):
    return x.astype(BF16)


def _rms_stats(x):
    r = lax.rsqrt(jnp.mean(x * x, axis=-1, keepdims=True) + EPS)
    return r, x * r


def _rms_bwd(dy, xhat, r, gain):
    u = dy * gain
    dx = r * (u - xhat * jnp.mean(u * xhat, axis=-1, keepdims=True))
    return dx, dy * xhat


def _params(sem):
    return pltpu.CompilerParams(dimension_semantics=sem, vmem_limit_bytes=VMEM_LIMIT_BYTES)


_ANY = pl.BlockSpec(memory_space=pl.ANY)


class _Exchange:
    def __init__(self, ins, outs, n_sems, phases):
        self.ins, self.outs, self.n_sems, self.phases = list(ins), list(outs), n_sems, list(phases)


def _merge_exchanges(a, b):
    na_i, na_o, shift = len(a.ins), len(a.outs), a.n_sems

    def first(fn):
        return lambda i, o, s, r, base: fn(i[:na_i], o[:na_o], s, r, base)

    def second(fn):
        return lambda i, o, s, r, base: fn(i[na_i:], o[na_o:], s, r, base + shift)

    phases = [(f, first(fn)) for f, fn in a.phases] + [(f, second(fn)) for f, fn in b.phases]
    return _Exchange(a.ins + b.ins, a.outs + b.outs, a.n_sems + b.n_sems, sorted(phases, key=lambda p: p[0]))


def _pallas(kern, *, grid, in_specs, out_specs, out_shape, args, name, scratch=(), exchange=None, aliases=None):
    aliases = aliases or {}
    if exchange is None:
        return pl.pallas_call(
            kern, grid=grid, in_specs=in_specs, out_specs=out_specs, out_shape=out_shape, name=name,
            scratch_shapes=list(scratch), input_output_aliases=aliases,
            compiler_params=_params(("arbitrary",) * len(grid)))(*args)
    n_in, n_out, n_sc = len(in_specs), len(out_specs), len(scratch)
    n_xi, n_xo = len(exchange.ins), len(exchange.outs)
    n_steps = 1
    for g in grid:
        n_steps *= g

    def wrapped(*refs):
        ins, refs = refs[:n_in], refs[n_in:]
        x_ins, refs = refs[:n_xi], refs[n_xi:]
        outs, refs = refs[:n_out], refs[n_out:]
        x_outs, refs = refs[:n_xo], refs[n_xo:]
        scr, (send_sems, recv_sems) = refs[:n_sc], refs[n_sc:]
        step = pl.program_id(0)
        for d in range(1, len(grid)):
            step = step * grid[d] + pl.program_id(d)
        for frac, fn in exchange.phases:
            at = min(int(frac * n_steps), n_steps - 1)

            @pl.when(step == at)
            def _(fn=fn):
                fn(x_ins, x_outs, send_sems, recv_sems, 0)

        kern(*ins, *outs, *scr)

    sems = [pltpu.SemaphoreType.DMA((exchange.n_sems,)), pltpu.SemaphoreType.DMA((exchange.n_sems,))]
    return pl.pallas_call(
        wrapped, grid=grid, in_specs=list(in_specs) + [_ANY] * n_xi, out_specs=list(out_specs) + [_ANY] * n_xo,
        out_shape=list(out_shape) + exchange.outs, name=name, scratch_shapes=list(scratch) + sems,
        input_output_aliases=aliases, compiler_params=_params(("arbitrary",) * len(grid)))(*args, *exchange.ins)


def _run_exchange(exchange, name):
    def body(*refs):
        n_i, n_o = len(exchange.ins), len(exchange.outs)
        for _, fn in exchange.phases:
            fn(refs[:n_i], refs[n_i:n_i + n_o], refs[n_i + n_o], refs[n_i + n_o + 1], 0)

    sems = [pltpu.SemaphoreType.DMA((exchange.n_sems,)), pltpu.SemaphoreType.DMA((exchange.n_sems,))]
    return pl.pallas_call(body, in_specs=[_ANY] * len(exchange.ins), out_specs=[_ANY] * len(exchange.outs),
                          out_shape=exchange.outs, scratch_shapes=sems, name=name)(*exchange.ins)


def _fused(parts, *, grid, name, exchange=None):
    counts = [(len(p["in_specs"]), len(p["out_specs"]), len(p["scratch"])) for p in parts]
    n_in, n_out = sum(c[0] for c in counts), sum(c[1] for c in counts)

    def kern(*refs):
        ins, outs, scr = refs[:n_in], refs[n_in:n_in + n_out], refs[n_in + n_out:]
        i0 = o0 = s0 = 0
        for p, (ni, no, ns) in zip(parts, counts):
            p["kern"](*ins[i0:i0 + ni], *outs[o0:o0 + no], *scr[s0:s0 + ns])
            i0, o0, s0 = i0 + ni, o0 + no, s0 + ns

    cat = lambda key: [a for p in parts for a in p[key]]
    return _pallas(kern, grid=grid, in_specs=cat("in_specs"), out_specs=cat("out_specs"), out_shape=cat("out_shape"),
                   scratch=cat("scratch"), args=cat("args"), name=name, exchange=exchange)


def _row_call(body, *, tm, row_ins, res_ins, row_outs, part_outs=(), name, exchange=None, tiles=None, into=None):
    t = row_ins[0].shape[0]
    n_all = t // tm
    tile0, n_tiles = tiles or (0, n_all)
    in_specs = [pl.BlockSpec((tm, a.shape[1]), lambda i: (i + tile0, 0)) for a in row_ins]
    in_specs += [pl.BlockSpec(a.shape, lambda i: (0, 0), pipeline_mode=pl.Buffered(1)) for a in res_ins]
    out_shape = [jax.ShapeDtypeStruct((t, w), dt) for (w, dt) in row_outs]
    out_shape += [jax.ShapeDtypeStruct((n_all, 1, w), F32) for w in part_outs]
    out_specs = [pl.BlockSpec((tm, w), lambda i: (i + tile0, 0)) for (w, _) in row_outs]
    out_specs += [pl.BlockSpec((1, 1, w), lambda i: (i + tile0, 0, 0)) for w in part_outs]
    n_ri, n_re, n_ro = len(row_ins), len(res_ins), len(row_outs)
    into = list(into or [])
    first_out = n_ri + n_re + len(into)

    def kern(*refs):
        body(refs[:n_ri], refs[n_ri:n_ri + n_re], refs[first_out:first_out + n_ro], refs[first_out + n_ro:])

    return _pallas(kern, grid=(n_tiles,), in_specs=in_specs + [_ANY] * len(into), out_specs=out_specs,
                   out_shape=out_shape, args=[*row_ins, *res_ins, *into], name=name, exchange=exchange,
                   aliases={n_ri + n_re + j: j for j in range(len(into))})


def _proj_fwd(x, g1, w_in_t, exchange):
    pieces = ((P_QA, F32), (P_KVA, F32), (P_QR, F32), (P_KR, F32), (P_VR, BF16), (P_GR, F32), (P_ZA, F32), (P_ZR, F32))

    def body(ri, re, ro, po):
        x_t = ri[0][...]
        r, xhat = _rms_stats(x_t)
        hb = _bf(xhat * re[0][...])
        ro[0][...] = hb
        for k, ((off, w), dt) in enumerate(pieces):
            ro[1 + k][...] = _dot_nt(hb, re[1][off:off + w, :]).astype(dt)

    outs = [(D_MODEL, BF16)] + [(w, dt) for ((_, w), dt) in pieces]
    return _row_call(body, tm=256, row_ins=[x], res_ins=[g1, w_in_t], row_outs=outs, name="proj_fwd",
                     exchange=exchange)


def _mix_fwd(attn, ret, z_a, z_r, x, wba, wbr, wout, g2):
    def body(ri, re, ro, po):
        ba = _dot(ri[0][...], re[0][...])
        br = _dot(ri[1][...], re[1][...])
        m = jax.nn.sigmoid(ri[2][...]) * ba + jax.nn.sigmoid(ri[3][...]) * br
        mb = _bf(m)
        x1 = ri[4][...] + _dot(mb, re[2][...])
        r, xhat = _rms_stats(x1)
        ro[0][...] = ba
        ro[1][...] = br
        ro[2][...] = mb
        ro[3][...] = x1
        ro[4][...] = _bf(xhat * re[3][...])

    outs = [(D_MODEL, F32), (D_MODEL, F32), (D_MODEL, BF16), (D_MODEL, F32), (D_MODEL, BF16)]
    return _row_call(body, tm=512, row_ins=[attn, ret, z_a, z_r, x], res_ins=[wba, wbr, wout, g2], row_outs=outs,
                     name="mix_fwd")


def _ffn_fwd_bwd(h2, x1, target, wg_t, wu_t, wd, g2):
    def body(ri, re, ro, po):
        h2_t = ri[0][...]
        x1_t = ri[1][...]
        gate = _dot_nt(h2_t, re[0][...])
        up = _dot_nt(h2_t, re[1][...])
        sg = jax.nn.sigmoid(gate)
        sl = gate * sg
        actb = _bf(sl * up)
        ro[0][...] = actb
        y = x1_t + _dot(actb, re[2][...])
        e = y - ri[2][...]
        po[0][0] = jnp.broadcast_to(0.5 * jnp.sum(jnp.sum(e * e, axis=1, keepdims=True), axis=0, keepdims=True)
                                    * (1.0 / D_MODEL), (1, 128))
        dy = e * (1.0 / D_MODEL)
        dyb = _bf(dy)
        ro[3][...] = dyb
        dact = _dot_nt(dyb, re[2][...])
        dupb = _bf(dact * sl)
        dgateb = _bf(dact * up * (sg * (1.0 + gate * (1.0 - sg))))
        ro[1][...] = dgateb
        ro[2][...] = dupb
        dh2 = _dot(dgateb, re[0][...]) + _dot(dupb, re[1][...])
        r, xhat = _rms_stats(x1_t)
        dxn, dgain = _rms_bwd(dh2, xhat, r, re[3][...])
        dx1 = dy + dxn
        ro[4][...] = dx1
        ro[5][...] = _bf(dx1)
        po[1][0] = jnp.sum(dgain, axis=0, keepdims=True)

    outs = [(D_FF, BF16), (D_FF, BF16), (D_FF, BF16), (D_MODEL, BF16), (D_MODEL, F32), (D_MODEL, BF16)]
    return _row_call(body, tm=256, row_ins=[h2, x1, target], res_ins=[wg_t, wu_t, wd, g2], row_outs=outs,
                     part_outs=(128, D_MODEL), name="ffn_fwd_bwd")


def _mix_bwd(dx1b, z_a, z_r, ba, br, g_r, o_ret, wout, wba, wbr, exchange):
    def body(ri, re, ro, po):
        dm = _dot_nt(ri[0][...], re[0][...])
        sa = jax.nn.sigmoid(ri[1][...])
        sr = jax.nn.sigmoid(ri[2][...])
        dbab = _bf(sa * dm)
        dbrb = _bf(sr * dm)
        ro[0][...] = dbab
        ro[1][...] = dbrb
        ro[2][...] = _bf(dm * ri[3][...] * (sa * (1.0 - sa)))
        ro[3][...] = _bf(dm * ri[4][...] * (sr * (1.0 - sr)))
        ro[4][...] = _bf(_dot_nt(dbab, re[1][...]))
        dret = _dot_nt(dbrb, re[2][...])
        for h in range(RET_HEADS):
            cols = slice(h * RET_V_DIM, (h + 1) * RET_V_DIM)
            g = ri[5][:, cols]
            r, rn = _rms_stats(ri[6][:, cols])
            sg = jax.nn.sigmoid(g)
            dret_h = dret[:, cols]
            d_rn = dret_h * (g * sg)
            ro[6][:, cols] = _bf(dret_h * rn * (sg * (1.0 + g * (1.0 - sg))))
            ro[5][:, cols] = r * (d_rn - rn * jnp.mean(d_rn * rn, axis=-1, keepdims=True))

    outs = [(D_MODEL, BF16), (D_MODEL, BF16), (D_MODEL, BF16), (D_MODEL, BF16), (ATT_Q, BF16), (RET_V, F32),
            (RET_V, BF16)]
    return _row_call(body, tm=256, row_ins=[dx1b, z_a, z_r, ba, br, g_r, o_ret], res_ins=[wout, wba, wbr],
                     row_outs=outs, name="mix_bwd", exchange=exchange)


def _proj_bwd(d_pieces, x, dx1, w_in_t, g1, exchange, tiles, into, name):
    groups = (P_QA, P_KVA, P_QR, P_KR, P_VR, P_GR, P_ZA, P_ZR)
    n_p = len(groups)

    def body(ri, re, ro, po):
        dh = None
        for k, (off, w) in enumerate(groups):
            term = _dot(ri[k][...], re[0][off:off + w, :])
            dh = term if dh is None else dh + term
        r, xhat = _rms_stats(ri[n_p][...])
        dxn, dgain = _rms_bwd(dh, xhat, r, re[1][...])
        ro[0][...] = ri[n_p + 1][...] + dxn
        po[0][0] = jnp.sum(dgain, axis=0, keepdims=True)

    return _row_call(body, tm=PROJ_BWD_TILE, row_ins=[*d_pieces, x, dx1], res_ins=[w_in_t, g1],
                     row_outs=[(D_MODEL, F32)], part_outs=(D_MODEL,), name=name, exchange=exchange, tiles=tiles,
                     into=into)


def _dw(a, b, *, tm, place, buf, name, exchange=None):
    t, m = a.shape
    n = b.shape[1]
    tk = min(2048, t)
    n_i, n_k = m // tm, t // tk
    fresh = isinstance(buf, jax.ShapeDtypeStruct)
    n_copies = len(place(0))

    def kern(a_ref, b_ref, *rest):
        out_ref, acc, sems = rest[-3:]
        i, k = pl.program_id(0), pl.program_id(1)
        part = _dot_tn(a_ref[...], b_ref[...])

        @pl.when(k == 0)
        def _():
            acc[i] = part

        @pl.when(k > 0)
        def _():
            acc[i] += part

        def copies(tile):
            return [pltpu.make_async_copy(acc.at[tile, pl.ds(r0, rows), :], out_ref.at[idx], sems.at[tile * n_copies + c])
                    for c, (r0, rows, idx) in enumerate(place(tile))]

        for tile in range(n_i):
            @pl.when((i == tile) & (k == n_k - 1))
            def _(tile=tile):
                for cp in copies(tile):
                    cp.start()

        @pl.when((i == n_i - 1) & (k == n_k - 1))
        def _():
            for tile in range(n_i):
                for cp in copies(tile):
                    cp.wait()

    in_specs = [pl.BlockSpec((tk, tm), lambda i, k: (k, i)), pl.BlockSpec((tk, n), lambda i, k: (k, 0))]
    shape = buf if fresh else jax.ShapeDtypeStruct(buf.shape, buf.dtype)
    return _pallas(
        kern, grid=(n_i, n_k), in_specs=in_specs + ([] if fresh else [_ANY]), out_specs=[_ANY], out_shape=[shape],
        scratch=[pltpu.VMEM((n_i, tm, n), F32), pltpu.SemaphoreType.DMA((n_i * n_copies,))],
        args=[a, b] + ([] if fresh else [buf]), aliases=None if fresh else {2: 0}, name=name, exchange=exchange)


def _heads_to_lanes(x3):
    return jnp.concatenate([x3[g] for g in range(GROUP)], axis=1)


def _lanes_to_heads(xt):
    return jnp.concatenate([xt[:, g * BLOCK:(g + 1) * BLOCK] for g in range(GROUP)], axis=0)


def _attn_group(n, kvh, q_ref, kvp_ref, kvc_ref, gq_col, gk, sink_ref):
    heads = [kvh * GROUP + g for g in range(GROUP)]
    cols = slice(kvh * GROUP * HEAD_DIM, (kvh + 1) * GROUP * HEAD_DIM)
    q3 = q_ref[:, cols].T.reshape(GROUP, HEAD_DIM, BLOCK)
    rq = lax.rsqrt(jnp.mean(q3 * q3, axis=1, keepdims=True) + EPS)
    qhat = q3 * rq
    qts = _heads_to_lanes(_bf(qhat * (gq_col * (HEAD_DIM ** -0.5))))
    kcols = slice(kvh * HEAD_DIM, (kvh + 1) * HEAD_DIM)
    vcols = slice(ATT_KV + kvh * HEAD_DIM, ATT_KV + (kvh + 1) * HEAD_DIM)
    k = jnp.concatenate([kvp_ref[:, kcols], kvc_ref[:, kcols]], axis=0)
    rk, khat = _rms_stats(k)
    knb = _bf(khat * gk)
    st = _dot(knb, qts)
    j = lax.broadcasted_iota(jnp.int32, (BLOCK, GROUP * BLOCK), 0)
    i = lax.broadcasted_iota(jnp.int32, (BLOCK, GROUP * BLOCK), 1) & (BLOCK - 1)
    from_prev = j > i
    f = jnp.where(from_prev, jnp.where(n > 0, st[0:BLOCK], -1e30), st[BLOCK:2 * BLOCK])
    sink = jnp.concatenate([jnp.broadcast_to(sink_ref[0:1, h:h + 1], (1, BLOCK)) for h in heads], axis=1)
    m = jnp.maximum(jnp.max(f, axis=0, keepdims=True), sink)
    e = jnp.exp(f - m)
    es = jnp.exp(sink - m)
    inv = 1.0 / (jnp.sum(e, axis=0, keepdims=True) + es)
    return dict(heads=heads, qhat=qhat, rq=rq, qts=qts, khat=khat, rk=rk, knb=knb, from_prev=from_prev,
                pf=e * inv, psink=es * inv)


def _unfold(from_prev, xf):
    return _bf(jnp.concatenate([jnp.where(from_prev, xf, 0.0), jnp.where(from_prev, 0.0, xf)], axis=0))


def _attn_fwd(q_a, kv_a, gq_col, gk, sinks):
    t = q_a.shape[0]
    nb = t // BLOCK

    def kern(q_ref, kvp_ref, kvc_ref, gq_ref, gk_ref, sink_ref, o_ref):
        n = pl.program_id(0)
        kvt = jnp.concatenate([kvp_ref[...].T, kvc_ref[...].T], axis=1)
        for kvh in range(N_KV_HEADS):
            a = _attn_group(n, kvh, q_ref, kvp_ref, kvc_ref, gq_ref[...], gk_ref[...], sink_ref)
            vt = _bf(kvt[ATT_KV + kvh * HEAD_DIM:ATT_KV + (kvh + 1) * HEAD_DIM, :])
            out_t = _dot(vt, _unfold(a["from_prev"], a["pf"]))
            cols = slice(kvh * GROUP * HEAD_DIM, (kvh + 1) * GROUP * HEAD_DIM)
            o_ref[:, cols] = _bf(_lanes_to_heads(out_t).T)

    small = lambda a: pl.BlockSpec(a.shape, lambda n: (0, 0))
    return dict(
        kern=kern,
        in_specs=[pl.BlockSpec((BLOCK, ATT_Q), lambda n: (n, 0)),
                  pl.BlockSpec((BLOCK, 2 * ATT_KV), lambda n: (jnp.maximum(n - 1, 0), 0)),
                  pl.BlockSpec((BLOCK, 2 * ATT_KV), lambda n: (n, 0)),
                  small(gq_col), small(gk), small(sinks)],
        out_specs=[pl.BlockSpec((BLOCK, ATT_Q), lambda n: (n, 0))],
        out_shape=[jax.ShapeDtypeStruct((t, ATT_Q), BF16)], scratch=[],
        args=[q_a, kv_a, kv_a, gq_col, gk, sinks])


def _attn_bwd(q_a, kv_a, d_attn, gq_col, gk, gk_col, sinks, exchange):
    t = q_a.shape[0]
    nb = t // BLOCK

    def kern(q_ref, kvp_ref, kvc_ref, do_ref, gq_ref, gk_ref, gkc_ref, sink_ref,
             dq_ref, dkv_ref, dgq_ref, dgk_ref, dsink_ref, band_k, band_v, carry_k, carry_v):
        n = pl.program_id(0)
        gq_v = gq_ref[...]
        gk_v = gk_ref[...]

        @pl.when(n == 0)
        def _():
            carry_k[...] = jnp.zeros_like(carry_k)
            carry_v[...] = jnp.zeros_like(carry_v)
            dgq_ref[...] = jnp.zeros_like(dgq_ref)
            dgk_ref[...] = jnp.zeros_like(dgk_ref)
            dsink_ref[...] = jnp.zeros_like(dsink_ref)

        @pl.when(n == nb)
        def _():
            band_k[...] = jnp.zeros_like(band_k)
            band_v[...] = jnp.zeros_like(band_v)

        @pl.when(n < nb)
        def _():
            lane16 = lax.broadcasted_iota(jnp.int32, (1, N_Q_HEADS), 1)
            dsink = jnp.zeros((1, N_Q_HEADS), F32)
            dgq = jnp.zeros((HEAD_DIM, 1), F32)
            gk_col = gkc_ref[...]
            kvt = jnp.concatenate([kvp_ref[...].T, kvc_ref[...].T], axis=1)
            for kvh in range(N_KV_HEADS):
                a = _attn_group(n, kvh, q_ref, kvp_ref, kvc_ref, gq_v, gk_v, sink_ref)
                from_prev, pf, qhat = a["from_prev"], a["pf"], a["qhat"]
                cols = slice(kvh * GROUP * HEAD_DIM, (kvh + 1) * GROUP * HEAD_DIM)
                vcols = slice(ATT_KV + kvh * HEAD_DIM, ATT_KV + (kvh + 1) * HEAD_DIM)
                dot = _heads_to_lanes(_bf(do_ref[:, cols].astype(F32).T.reshape(GROUP, HEAD_DIM, BLOCK)))
                vb = _bf(jnp.concatenate([kvp_ref[:, vcols], kvc_ref[:, vcols]], axis=0))
                dpt = _dot(vb, dot)
                dpf = jnp.where(from_prev, dpt[0:BLOCK], dpt[BLOCK:2 * BLOCK])
                delta = jnp.sum(pf * dpf, axis=0, keepdims=True)
                dst = _unfold(from_prev, pf * (dpf - delta))
                dsk = a["psink"] * delta
                for g, h in enumerate(a["heads"]):
                    tot = jnp.sum(dsk[:, g * BLOCK:(g + 1) * BLOCK], axis=1, keepdims=True)
                    dsink = dsink - jnp.where(lane16 == h, tot, 0.0)
                kt = kvt[kvh * HEAD_DIM:(kvh + 1) * HEAD_DIM, :]
                knt = _bf(kt * lax.rsqrt(jnp.mean(kt * kt, axis=0, keepdims=True) + EPS) * gk_col)
                dqn = (_dot(knt, dst) * (HEAD_DIM ** -0.5))
                band_k[kvh] = _dot_nt(dst, a["qts"])
                band_v[kvh] = _dot_nt(_unfold(from_prev, pf), dot)
                dqn3 = _lanes_to_heads(dqn).reshape(GROUP, HEAD_DIM, BLOCK)
                u = dqn3 * gq_v
                dq3 = a["rq"] * (u - qhat * jnp.mean(u * qhat, axis=1, keepdims=True))
                dgq = dgq + jnp.sum(jnp.sum(dqn3 * qhat, axis=0), axis=1, keepdims=True)
                dq_ref[:, cols] = _bf(dq3.reshape(GROUP * HEAD_DIM, BLOCK).T)
            dsink_ref[...] += dsink
            dgq_ref[...] += dgq

        dgk = jnp.zeros((1, HEAD_DIM), F32)
        for kvh in range(N_KV_HEADS):
            kcols = slice(kvh * HEAD_DIM, (kvh + 1) * HEAD_DIM)
            vcols = slice(ATT_KV + kvh * HEAD_DIM, ATT_KV + (kvh + 1) * HEAD_DIM)
            dkn = carry_k[kvh] + band_k[kvh, 0:BLOCK, :]
            dv = carry_v[kvh] + band_v[kvh, 0:BLOCK, :]
            rk, khat = _rms_stats(kvp_ref[:, kcols])
            dk, dgain = _rms_bwd(dkn, khat, rk, gk_v)
            dgk = dgk + jnp.sum(dgain, axis=0, keepdims=True)
            dkv_ref[:, kcols] = _bf(dk)
            dkv_ref[:, vcols] = _bf(dv)
            carry_k[kvh] = band_k[kvh, BLOCK:2 * BLOCK, :]
            carry_v[kvh] = band_v[kvh, BLOCK:2 * BLOCK, :]
        dgk_ref[...] += dgk

    small = lambda a: pl.BlockSpec(a.shape, lambda n: (0, 0))
    last = nb - 1
    return _pallas(
        kern, grid=(nb + 1,),
        in_specs=[pl.BlockSpec((BLOCK, ATT_Q), lambda n: (jnp.minimum(n, last), 0)),
                  pl.BlockSpec((BLOCK, 2 * ATT_KV), lambda n: (jnp.maximum(n - 1, 0), 0)),
                  pl.BlockSpec((BLOCK, 2 * ATT_KV), lambda n: (jnp.minimum(n, last), 0)),
                  pl.BlockSpec((BLOCK, ATT_Q), lambda n: (jnp.minimum(n, last), 0)),
                  small(gq_col), small(gk), small(gk_col), small(sinks)],
        out_specs=[pl.BlockSpec((BLOCK, ATT_Q), lambda n: (jnp.minimum(n, last), 0)),
                   pl.BlockSpec((BLOCK, 2 * ATT_KV), lambda n: (jnp.maximum(n - 1, 0), 0)),
                   pl.BlockSpec((HEAD_DIM, 1), lambda n: (0, 0)),
                   pl.BlockSpec((1, HEAD_DIM), lambda n: (0, 0)),
                   pl.BlockSpec((1, N_Q_HEADS), lambda n: (0, 0))],
        out_shape=[jax.ShapeDtypeStruct((t, ATT_Q), BF16), jax.ShapeDtypeStruct((t, 2 * ATT_KV), BF16),
                   jax.ShapeDtypeStruct((HEAD_DIM, 1), F32), jax.ShapeDtypeStruct((1, HEAD_DIM), F32),
                   jax.ShapeDtypeStruct((1, N_Q_HEADS), F32)],
        scratch=[pltpu.VMEM((N_KV_HEADS, 2 * BLOCK, HEAD_DIM), F32),
                 pltpu.VMEM((N_KV_HEADS, 2 * BLOCK, HEAD_DIM), F32),
                 pltpu.VMEM((N_KV_HEADS, BLOCK, HEAD_DIM), F32),
                 pltpu.VMEM((N_KV_HEADS, BLOCK, HEAD_DIM), F32)],
        args=[q_a, kv_a, kv_a, d_attn, gq_col, gk, gk_col, sinks], name="attn_bwd", exchange=exchange)


def _ret_tables(t, exchange):
    theta = 1.0 / (RET_ROT_BASE ** jnp.linspace(0.0, 1.0, RET_QK_DIM // 2, dtype=F32))
    theta2 = jnp.repeat(theta, 2)[None, :]
    sign = jnp.tile(jnp.array([-1.0, 1.0], F32), RET_QK_DIM // 2)[None, :]

    def kern(theta_ref, sign_ref, cos_ref, sin_ref):
        first = pl.program_id(0) * RET_CHUNK
        pos = (first + lax.broadcasted_iota(jnp.int32, (RET_CHUNK, RET_QK_DIM), 0)).astype(F32)
        ang = pos * theta_ref[...]
        cos_ref[...] = jnp.cos(ang)
        sin_ref[...] = jnp.sin(ang) * sign_ref[...]

    row = pl.BlockSpec((1, RET_QK_DIM), lambda n: (0, 0))
    blk = pl.BlockSpec((RET_CHUNK, RET_QK_DIM), lambda n: (n, 0))
    cos, sin_s, *got = _pallas(kern, grid=(t // RET_CHUNK,), in_specs=[row, row], out_specs=[blk, blk],
                               out_shape=[jax.ShapeDtypeStruct((t, RET_QK_DIM), F32)] * 2, args=[theta2, sign],
                               name="position_tables", exchange=exchange)
    log_gamma = jnp.log(1.0 - 2.0 ** (-5.0 - jnp.arange(RET_HEADS, dtype=F32)))
    i = jnp.arange(RET_CHUNK, dtype=F32)
    diff = i[:, None] - i[None, :]
    causal = diff >= 0
    decay = jnp.where(causal[None], jnp.exp(jnp.where(causal, diff, 0.0)[None] * log_gamma[:, None, None]), 0.0)
    xi = jnp.exp((i + 1.0)[None, :] * log_gamma[:, None])[:, :, None]
    zeta = jnp.exp((RET_CHUNK - 1.0 - i)[None, :] * log_gamma[:, None])[:, :, None]
    gch = jnp.broadcast_to(jnp.exp(RET_CHUNK * log_gamma)[:, None, None], (RET_HEADS, 1, 128))
    return (cos, sin_s, decay, xi, zeta, gch), got


def _swap_pairs(x):
    lane = lax.broadcasted_iota(jnp.int32, x.shape, 1)
    return jnp.where((lane & 1) == 0, pltpu.roll(x, RET_QK_DIM - 1, 1), pltpu.roll(x, 1, 1))


def _rotate(x, cos, sin_s):
    return x * cos + _swap_pairs(x) * sin_s


def _rotate_bwd(dy, cos, sin_s):
    return dy * cos + _swap_pairs(dy * sin_s)


def _ret_specs(order):
    qk = pl.BlockSpec((RET_CHUNK, RET_QK), lambda j: (order(j), 0))
    v = pl.BlockSpec((RET_CHUNK, RET_V), lambda j: (order(j), 0))
    dec = pl.BlockSpec((RET_HEADS, RET_CHUNK, RET_CHUNK), lambda j: (0, 0, 0))
    col = pl.BlockSpec((RET_HEADS, RET_CHUNK, 1), lambda j: (0, 0, 0))
    gch = pl.BlockSpec((RET_HEADS, 1, 128), lambda j: (0, 0, 0))
    st = pl.BlockSpec((RET_HEADS, None, RET_QK_DIM, RET_V_DIM), lambda j: (0, order(j), 0, 0))
    pos = pl.BlockSpec((RET_CHUNK, RET_QK_DIM), lambda j: (order(j), 0))
    return qk, v, dec, col, gch, st, pos


def _ret_fwd(q_r, k_r, v_r, g_r, tables):
    t = q_r.shape[0]
    nc = t // RET_CHUNK
    cos, sin_s, decay, xi, zeta, gch = tables

    def kern(q_ref, k_ref, v_ref, g_ref, cos_ref, sin_ref, dec_ref, xi_ref, zeta_ref, gch_ref,
             o_ref, ret_ref, st_ref, state):
        @pl.when(pl.program_id(0) == 0)
        def _():
            state[...] = jnp.zeros_like(state)

        cos_t = cos_ref[...]
        sin_t = sin_ref[...]
        for h in range(RET_HEADS):
            qc = slice(h * RET_QK_DIM, (h + 1) * RET_QK_DIM)
            vc = slice(h * RET_V_DIM, (h + 1) * RET_V_DIM)
            qs = _bf(_rotate(q_ref[:, qc], cos_t, sin_t))
            ks = _rotate(k_ref[:, qc] * (RET_QK_DIM ** -0.5), cos_t, sin_t)
            vb = v_ref[:, vc]
            s_old = state[h]
            sb = _bf(s_old)
            st_ref[h] = sb
            inner = _dot_nt(qs, _bf(ks)) * dec_ref[h]
            out = _dot(_bf(inner), vb) + _dot(qs, sb) * xi_ref[h]
            state[h] = gch_ref[h, :, 0:1] * s_old + _dot_tn(_bf(ks * zeta_ref[h]), vb)
            o_ref[:, vc] = out
            r, rn = _rms_stats(out)
            g = g_ref[:, vc]
            ret_ref[:, vc] = _bf(g * jax.nn.sigmoid(g) * rn)

    qk, v, dec, col, gsp, st, pos = _ret_specs(lambda j: j)
    return dict(
        kern=kern,
        in_specs=[qk, qk, v, v, pos, pos, dec, col, col, gsp],
        out_specs=[v, v, st],
        out_shape=[jax.ShapeDtypeStruct((t, RET_V), F32), jax.ShapeDtypeStruct((t, RET_V), BF16),
                   jax.ShapeDtypeStruct((RET_HEADS, nc, RET_QK_DIM, RET_V_DIM), BF16)],
        scratch=[pltpu.VMEM((RET_HEADS, RET_QK_DIM, RET_V_DIM), F32)],
        args=[q_r, k_r, v_r, g_r, cos, sin_s, decay, xi, zeta, gch])


def _ret_bwd(q_r, k_r, v_r, d_o, states, tables, exchange):
    t = q_r.shape[0]
    nc = t // RET_CHUNK
    cos, sin_s, decay, xi, zeta, gch = tables

    def kern(q_ref, k_ref, v_ref, do_ref, st_ref, cos_ref, sin_ref, dec_ref, xi_ref, zeta_ref, gch_ref,
             dq_ref, dk_ref, dv_ref, dstate):
        @pl.when(pl.program_id(0) == 0)
        def _():
            dstate[...] = jnp.zeros_like(dstate)

        cos_t = cos_ref[...]
        sin_t = sin_ref[...]
        scale = RET_QK_DIM ** -0.5
        for h in range(RET_HEADS):
            qc = slice(h * RET_QK_DIM, (h + 1) * RET_QK_DIM)
            vc = slice(h * RET_V_DIM, (h + 1) * RET_V_DIM)
            qs = _bf(_rotate(q_ref[:, qc], cos_t, sin_t))
            ks = _rotate(k_ref[:, qc] * scale, cos_t, sin_t)
            ksb = _bf(ks)
            vb = v_ref[:, vc]
            d_o_t = do_ref[:, vc]
            dob = _bf(d_o_t)
            doxb = _bf(d_o_t * xi_ref[h])
            dec = dec_ref[h]
            ds_old = dstate[h]
            dsb = _bf(ds_old)
            pb = _bf(_dot_nt(qs, ksb) * dec)
            dpb = _bf(_dot_nt(dob, vb) * dec)
            dqs = _dot(dpb, ksb) + _dot_nt(doxb, st_ref[h])
            dks = _dot_tn(dpb, qs) + _dot_nt(vb, dsb) * zeta_ref[h]
            dv_ref[:, vc] = _bf(_dot_tn(pb, dob) + _dot(_bf(ks * zeta_ref[h]), dsb))
            dstate[h] = gch_ref[h, :, 0:1] * ds_old + _dot_tn(qs, doxb)
            dq_ref[:, qc] = _bf(_rotate_bwd(dqs, cos_t, sin_t))
            dk_ref[:, qc] = _bf(_rotate_bwd(dks, cos_t, sin_t) * scale)

    qk, v, dec, col, gsp, st, pos = _ret_specs(lambda j: nc - 1 - j)
    return _pallas(
        kern, grid=(nc,),
        in_specs=[qk, qk, v, v, st, pos, pos, dec, col, col, gsp],
        out_specs=[qk, qk, v],
        out_shape=[jax.ShapeDtypeStruct((t, RET_QK), BF16), jax.ShapeDtypeStruct((t, RET_QK), BF16),
                   jax.ShapeDtypeStruct((t, RET_V), BF16)],
        scratch=[pltpu.VMEM((RET_HEADS, RET_QK_DIM, RET_V_DIM), F32)],
        args=[q_r, k_r, v_r, d_o, states, cos, sin_s, decay, xi, zeta, gch], name="ret_bwd", exchange=exchange)


def _position():
    return lax.axis_index("x"), lax.axis_index("y"), lax.axis_index("c")


def _gather_exchange(owns, forward_at):
    n = len(owns)

    def copies(ins, outs, send_sems, recv_sems, base):
        x, y, c = _position()
        sibling = (x, y, 1 - c)
        chips = [(1 - x, y), (x, 1 - y), (1 - x, 1 - y)]
        my_chip = 2 * x + y

        def slab(a, chip, hf):
            half = owns[a].shape[0] // 2
            return outs[a].at[chip, pl.ds(hf * half, half), :]

        def copy(k, src, dst, to):
            return pltpu.make_async_remote_copy(src_ref=src, dst_ref=dst, send_sem=send_sems.at[base + k],
                                                recv_sem=recv_sems.at[base + k], device_id=to, device_id_type=MESH)

        first, passed, from_sibling = [], [], []
        for a in range(n):
            half = owns[a].shape[0] // 2
            for k, (cx, cy) in enumerate(chips):
                first.append(copy(6 * a + k, ins[a].at[pl.ds(c * half, half), :], slab(a, my_chip, c), (cx, cy, c)))
                landed = slab(a, 2 * cx + cy, c)
                passed.append(copy(6 * a + 3 + k, landed, landed, sibling))
                theirs = slab(a, 2 * cx + cy, 1 - c)
                from_sibling.append(copy(6 * a + 3 + k, theirs, theirs, sibling))
        return first, passed, from_sibling

    def start(*args):
        first, _, _ = copies(*args)
        for cp in first:
            cp.start()

    def forward(*args):
        first, passed, _ = copies(*args)
        for arrived, cp in zip(first, passed):
            arrived.wait_recv()
            cp.start()

    def finish(*args):
        first, passed, from_sibling = copies(*args)
        for cp in from_sibling:
            cp.wait_recv()
        for cp in first + passed:
            cp.wait_send()

    outs = [jax.ShapeDtypeStruct((N_CHIPS, *a.shape), a.dtype) for a in owns]
    return _Exchange(owns, outs, 6 * n, [(0.0, start), (forward_at, forward), (1.0, finish)])


def _symmetric_exchange(ins, outs, plan):
    n_sems = len(plan([None] * len(ins), [None] * len(outs), 0, 0, 0, dry=True))

    def copies(in_refs, out_refs, send_sems, recv_sems, base):
        x, y, c = _position()
        return [pltpu.make_async_remote_copy(src_ref=src, dst_ref=dst, send_sem=send_sems.at[base + k],
                                             recv_sem=recv_sems.at[base + k], device_id=dev, device_id_type=MESH)
                for k, (src, dst, dev) in enumerate(plan(in_refs, out_refs, x, y, c, dry=False))]

    def start(*args):
        for cp in copies(*args):
            cp.start()

    def finish(*args):
        for cp in copies(*args):
            cp.wait()

    return _Exchange(ins, outs, n_sems, [(0.0, start), (1.0, finish)])


def _pair_exchange(gs):
    def plan(in_refs, out_refs, x, y, c, dry):
        out = []
        for a, g in enumerate(gs):
            half = g.shape[1] // 2
            for k in range(N_CHIPS):
                out.append(None if dry else (in_refs[a].at[k, pl.ds((1 - c) * half, half), :], out_refs[a].at[k],
                                             (x, y, 1 - c)))
        return out

    outs = [jax.ShapeDtypeStruct((g.shape[0], g.shape[1] // 2, g.shape[2]), g.dtype) for g in gs]
    return _symmetric_exchange(gs, outs, plan)


def _pair_sum(g, from_sibling, c_arr, *, tile, name):
    n, rows, width = g.shape
    tiles = (rows // 2) // tile

    def kern(c_ref, g_ref, s_ref, o_ref):
        o_ref[...] = _bf(g_ref[...] + s_ref[...])

    return pl.pallas_call(
        kern,
        grid_spec=pltpu.PrefetchScalarGridSpec(
            num_scalar_prefetch=1, grid=(n, tiles),
            in_specs=[pl.BlockSpec((None, tile, width), lambda k, i, c: (k, c[0] * tiles + i, 0)),
                      pl.BlockSpec((None, tile, width), lambda k, i, c: (k, i, 0))],
            out_specs=pl.BlockSpec((None, tile, width), lambda k, i, c: (k, i, 0))),
        out_shape=jax.ShapeDtypeStruct((n, rows // 2, width), BF16), name=name,
        compiler_params=_params(("parallel", "parallel")),
    )(c_arr, g, from_sibling)


def _scatter_to_owners(hsums):
    def plan(in_refs, out_refs, x, y, c, dry):
        out = []
        for a in range(len(hsums)):
            for k, (cx, cy) in enumerate([(1 - x, y), (x, 1 - y), (1 - x, 1 - y)]):
                out.append(None if dry else (in_refs[a].at[2 * cx + cy], out_refs[a].at[k], (cx, cy, c)))
        return out

    outs = [jax.ShapeDtypeStruct((3, *h.shape[1:]), h.dtype) for h in hsums]
    return _symmetric_exchange(hsums, outs, plan)


def _sum_chips(hsum, parts, chip_arr, *, tile, name):
    n, half, width = parts.shape

    def kern(chip_ref, h_ref, p_ref, o_ref):
        acc = h_ref[...].astype(F32)
        for k in range(n):
            acc = acc + p_ref[k].astype(F32)
        o_ref[...] = acc

    return pl.pallas_call(
        kern,
        grid_spec=pltpu.PrefetchScalarGridSpec(
            num_scalar_prefetch=1, grid=(half // tile,),
            in_specs=[pl.BlockSpec((None, tile, width), lambda i, chip: (chip[0], i, 0)),
                      pl.BlockSpec((n, tile, width), lambda i, chip: (0, i, 0))],
            out_specs=pl.BlockSpec((tile, width), lambda i, chip: (i, 0))),
        out_shape=jax.ShapeDtypeStruct((half, width), F32), name=name,
        compiler_params=_params(("parallel",)),
    )(chip_arr, hsum, parts)


def _share_halves(fhalves):
    def plan(in_refs, out_refs, x, y, c, dry):
        return [None if dry else (in_refs[a], out_refs[a], (x, y, 1 - c)) for a in range(len(fhalves))]

    return _symmetric_exchange(fhalves, [jax.ShapeDtypeStruct(f.shape, f.dtype) for f in fhalves], plan)


def _adamw_math(w, g, m, v):
    m = ADAM_B1 * m + (1.0 - ADAM_B1) * g
    v = ADAM_B2 * v + (1.0 - ADAM_B2) * (g * g)
    m_hat = m / (1.0 - ADAM_B1 ** ADAM_STEP)
    v_hat = v / (1.0 - ADAM_B2 ** ADAM_STEP)
    delta = -ADAM_LR * (m_hat / (jnp.sqrt(v_hat) + ADAM_EPS) + ADAM_WD * w)
    return delta, m, v


def _adamw(mats, g_mine, g_other, c_arr, *, tile, name):
    width = g_mine.shape[1]
    tiles_per_half = g_mine.shape[0] // tile
    n_tiles = [w.shape[0] // tile for w, _, _, _ in mats]
    n_mats = len(mats)

    def kern(c_ref, *refs):
        ins, outs = refs[:5 * n_mats], refs[5 * n_mats:]
        for j, (_, _, _, row_off) in enumerate(mats):
            w_ref, gm_ref, go_ref, m_ref, v_ref = ins[5 * j:5 * j + 5]
            i = jnp.minimum(pl.program_id(0), n_tiles[j] - 1)
            in_my_half = ((row_off // tile + i) // tiles_per_half) == c_ref[0]
            g = jnp.where(in_my_half, gm_ref[...], go_ref[...])
            d, nm, nv = _adamw_math(w_ref[...], g, m_ref[...], v_ref[...])
            for out_ref, val in zip(outs[4 * j:4 * j + 4], (g, d, nm, nv)):
                out_ref[...] = val

    in_specs, out_specs, out_shape, args = [], [], [], []
    for (w, m, v, row_off), nt in zip(mats, n_tiles):
        full = pl.BlockSpec((tile, width), lambda i, c, nt=nt: (jnp.minimum(i, nt - 1), 0))
        half = pl.BlockSpec((tile, width), lambda i, c, nt=nt, first=row_off // tile:
                            ((first + jnp.minimum(i, nt - 1)) % tiles_per_half, 0))
        in_specs += [full, half, half, full, full]
        out_specs += [full] * 4
        out_shape += [jax.ShapeDtypeStruct(w.shape, F32)] * 4
        args += [w, g_mine, g_other, m, v]
    outs = pl.pallas_call(
        kern,
        grid_spec=pltpu.PrefetchScalarGridSpec(num_scalar_prefetch=1, grid=(max(n_tiles),), in_specs=in_specs,
                                               out_specs=out_specs),
        out_shape=out_shape, name=name, compiler_params=_params(("arbitrary",)),
    )(c_arr, *args)
    return [outs[4 * j:4 * j + 4] for j in range(n_mats)]


def _small_step(partials, params):
    slots = ((0, 0, D_MODEL), (1, 0, D_MODEL), (2, 0, HEAD_DIM), (2, 128, HEAD_DIM), (2, 256, N_Q_HEADS))
    loss_slot = (2, 384, 128)

    def body(*refs):
        loss_ref, dg1_ref, dg2_ref, dgq_ref, dgk_ref, dsink_ref = refs[:6]
        p_refs, out_refs = refs[6:21], refs[21:42]
        mine, gathered, send_sems, recv_sems = refs[42:]
        x, y, c = _position()
        me = 4 * x + 2 * y + c
        mine[...] = jnp.zeros_like(mine)
        for (row, lane, n), val in zip(slots + (loss_slot,), (
                jnp.sum(dg1_ref[...], axis=0, keepdims=True), jnp.sum(dg2_ref[...], axis=0, keepdims=True),
                dgq_ref[...], dgk_ref[...], dsink_ref[...], jnp.sum(loss_ref[...], axis=0, keepdims=True))):
            mine[row:row + 1, lane:lane + n] = val
        copies = []
        for k in range(1, N_DEV):
            flip = (k >> 2) & 1, (k >> 1) & 1, k & 1
            to = (x ^ flip[0], y ^ flip[1], c ^ flip[2])
            cp = pltpu.make_async_remote_copy(
                src_ref=mine, dst_ref=gathered.at[me], send_sem=send_sems.at[k - 1], recv_sem=recv_sems.at[k - 1],
                device_id=to, device_id_type=MESH)
            cp.start()
            copies.append(cp)
        gathered[me] = mine[...]
        for k in range(1, N_DEV):
            flip = (k >> 2) & 1, (k >> 1) & 1, k & 1
            src = 4 * (x ^ flip[0]) + 2 * (y ^ flip[1]) + (c ^ flip[2])
            pltpu.make_async_remote_copy(
                src_ref=mine, dst_ref=gathered.at[src], send_sem=send_sems.at[k - 1], recv_sem=recv_sems.at[k - 1],
                device_id=(x, y, c), device_id_type=MESH).wait_recv()
        for cp in copies:
            cp.wait_send()
        total = gathered[0]
        for k in range(1, N_DEV):
            total = total + gathered[k]
        row, lane, n = loss_slot
        out_refs[0][...] = total[row:row + 1, lane:lane + n]
        for i, (row, lane, n) in enumerate(slots):
            g = total[row:row + 1, lane:lane + n]
            d, nm, nv = _adamw_math(p_refs[i][...], g, p_refs[5 + i][...], p_refs[10 + i][...])
            for kind, val in enumerate((g, d, nm, nv)):
                out_refs[1 + 5 * kind + i][...] = val

    vm = pl.BlockSpec(memory_space=pltpu.VMEM)
    shapes = [jax.ShapeDtypeStruct((1, 128), F32)] + [jax.ShapeDtypeStruct((1, n), F32) for _, _, n in slots] * 4
    return pl.pallas_call(
        body, in_specs=[vm] * 21, out_specs=[vm] * 21, out_shape=shapes,
        scratch_shapes=[pltpu.VMEM((SMALL_ROWS, D_MODEL), F32), pltpu.VMEM((N_DEV, SMALL_ROWS, D_MODEL), F32),
                        pltpu.SemaphoreType.DMA((N_DEV - 1,)), pltpu.SemaphoreType.DMA((N_DEV - 1,))],
        name="small_step",
    )(*partials, *params)


def _with_own(gathered, own, my_chip):
    return lax.dynamic_update_slice(gathered, own[None], (my_chip, 0, 0))


def kernel(x, norm_mix_gain, w_in, q_norm_gain, k_norm_gain, attn_sinks, w_branch_attn, w_branch_ret, w_out, norm_ffn_gain, w_ffn_gate, w_ffn_up, w_ffn_down, loss_target, m_norm_mix_gain, m_w_in, m_q_norm_gain, m_k_norm_gain, m_attn_sinks, m_w_branch_attn, m_w_branch_ret, m_w_out, m_norm_ffn_gain, m_w_ffn_gate, m_w_ffn_up, m_w_ffn_down, v_norm_mix_gain, v_w_in, v_q_norm_gain, v_k_norm_gain, v_attn_sinks, v_w_branch_attn, v_w_branch_ret, v_w_out, v_norm_ffn_gain, v_w_ffn_gate, v_w_ffn_up, v_w_ffn_down):
    my_chip = 2 * lax.axis_index("x") + lax.axis_index("y")
    c_arr = lax.axis_index("c").astype(jnp.int32).reshape(1)
    chip_arr = my_chip.astype(jnp.int32).reshape(1)
    x_t, target = x[0], loss_target[0]
    g1, g2, gq, gk, sinks = norm_mix_gain, norm_ffn_gain, q_norm_gain, k_norm_gain, attn_sinks

    tr = lambda a: jnp.transpose(a[0])
    own_w_in = _bf(tr(w_in))
    own_ffn = _bf(jnp.concatenate([tr(w_ffn_gate), tr(w_ffn_up), w_ffn_down[0]], axis=0))
    own_mix = _bf(jnp.concatenate([w_branch_attn[0], w_branch_ret[0], w_out[0]], axis=0))
    tables, (got_w_in,) = _ret_tables(x_t.shape[0], _gather_exchange([own_w_in], 0.9))
    w_in_t = _with_own(got_w_in, own_w_in, my_chip).reshape(D_IN, D_MODEL)
    (h1, q_a, kv_a, q_r, k_r, v_r, g_r, z_a, z_r, got_ffn, got_mix) = _proj_fwd(
        x_t, g1, w_in_t, _gather_exchange([own_ffn, own_mix], 0.8))
    all_ffn = _with_own(got_ffn, own_ffn, my_chip)
    all_mix = _with_own(got_mix, own_mix, my_chip)
    wg_t = all_ffn[:, 0:FF_SH].reshape(D_FF, D_MODEL)
    wu_t = all_ffn[:, FF_SH:2 * FF_SH].reshape(D_FF, D_MODEL)
    wd = all_ffn[:, 2 * FF_SH:3 * FF_SH].reshape(D_FF, D_MODEL)
    wba = all_mix[:, 0:256].reshape(ATT_Q, D_MODEL)
    wbr = all_mix[:, 256:768].reshape(RET_V, D_MODEL)
    wout = all_mix[:, 768:1024].reshape(D_MODEL, D_MODEL)

    gq_col, gk_col = gq.reshape(HEAD_DIM, 1), gk.reshape(HEAD_DIM, 1)
    attn, o_ret, ret, states = _fused([_attn_fwd(q_a, kv_a, gq_col, gk, sinks), _ret_fwd(q_r, k_r, v_r, g_r, tables)],
                                      grid=(x_t.shape[0] // BLOCK,), name="mixers_fwd")
    ba, br, merged, x1, h2 = _mix_fwd(attn, ret, z_a, z_r, x_t, wba, wbr, wout, g2)
    act, dgate, dup, dyb, dx1, dx1b, loss_p, dg2_p = _ffn_fwd_bwd(h2, x1, target, wg_t, wu_t, wd, g2)

    def pairs(row0, rows):
        return lambda i: [(h * rows, rows, (2 * i + h, pl.ds(row0, rows), slice(None))) for h in range(2)]

    f_block = jax.ShapeDtypeStruct((N_CHIPS, 3 * FF_SH, D_MODEL), F32)
    f_block, = _dw(dgate, h2, tm=2 * FF_SH, place=pairs(0, FF_SH), buf=f_block, name="dw_gate")
    f_block, = _dw(dup, h2, tm=2 * FF_SH, place=pairs(FF_SH, FF_SH), buf=f_block, name="dw_up")
    f_block, = _dw(act, dyb, tm=2 * FF_SH, place=pairs(2 * FF_SH, FF_SH), buf=f_block, name="dw_down")
    (dba, dbr, dz_a, dz_r, d_attn, d_o, dg_r, sib_ffn) = _mix_bwd(
        dx1b, z_a, z_r, ba, br, g_r, o_ret, wout, wba, wbr, _pair_exchange([f_block]))
    f_sum = _pair_sum(f_block, sib_ffn, c_arr, tile=528, name="pair_sum_ffn")

    def quarters(row0, rows):
        return lambda i: [(k * rows, rows, (k, pl.ds(row0, rows), slice(None))) for k in range(N_CHIPS)]

    m_block = jax.ShapeDtypeStruct((N_CHIPS, D_MODEL, D_MODEL), F32)
    m_block, = _dw(attn, dba, tm=ATT_Q, place=quarters(0, 256), buf=m_block, name="dw_ba")
    m_block, = _dw(ret, dbr, tm=D_MODEL, place=pairs(256, 512), buf=m_block, name="dw_br")
    m_block, = _dw(merged, dx1b, tm=D_MODEL, place=quarters(768, 256), buf=m_block, name="dw_out")

    def w_in_rows(group):
        off, w = group
        tm = min(w, D_MODEL)
        return dict(tm=tm, place=lambda i: [(0, tm, (pl.ds(off + i * tm, tm), slice(None)))])

    w_block = jax.ShapeDtypeStruct((D_IN, D_MODEL), F32)
    w_block, sib_mix = _dw(dg_r, h1, buf=w_block, name="dw_in_5", exchange=_pair_exchange([m_block]), **w_in_rows(P_GR))
    w_block, = _dw(dz_a, h1, buf=w_block, name="dw_in_6", **w_in_rows(P_ZA))
    w_block, = _dw(dz_r, h1, buf=w_block, name="dw_in_7", **w_in_rows(P_ZR))
    m_sum = _pair_sum(m_block, sib_mix, c_arr, tile=256, name="pair_sum_mix")

    dq_r, dk_r, dv_r, got_ffn_sums = _ret_bwd(q_r, k_r, v_r, d_o, states, tables, _scatter_to_owners([f_sum]))
    ffn_half = _sum_chips(f_sum, got_ffn_sums, chip_arr, tile=528, name="sum_chips_ffn")
    w_block, = _dw(dq_r, h1, buf=w_block, name="dw_in_2", **w_in_rows(P_QR))
    w_block, = _dw(dk_r, h1, buf=w_block, name="dw_in_3", **w_in_rows(P_KR))
    w_block, = _dw(dv_r, h1, buf=w_block, name="dw_in_4", **w_in_rows(P_VR))

    (dq_a, dkv_a, dgq, dgk, dsinks, got_mix_sums, ffn_other) = _attn_bwd(
        q_a, kv_a, d_attn, gq_col, gk, gk_col, sinks,
        _merge_exchanges(_scatter_to_owners([m_sum]), _share_halves([ffn_half])))
    dgq = dgq.reshape(1, HEAD_DIM)
    mix_half = _sum_chips(m_sum, got_mix_sums, chip_arr, tile=256, name="sum_chips_mix")
    w_block, mix_other = _dw(dq_a, h1, buf=w_block, name="dw_in_0", exchange=_share_halves([mix_half]),
                             **w_in_rows(P_QA))
    w_block, = _dw(dkv_a, h1, buf=w_block, name="dw_in_1", **w_in_rows(P_KVA))

    w_block = w_block.reshape(N_CHIPS, W_IN_SH, D_MODEL)
    d_pieces = [dq_a, dkv_a, dq_r, dk_r, dv_r, dg_r, dz_a, dz_r]
    n_pb = x_t.shape[0] // PROJ_BWD_TILE
    n_a, n_b = n_pb // 4, (9 * n_pb) // 16
    done = None

    def proj_bwd_run(first, count, exchange, name):
        nonlocal done
        if count == 0:
            return _run_exchange(exchange, name + "_alone")
        res = _proj_bwd(d_pieces, x_t, dx1, w_in_t, g1, exchange, (first, count), done, name)
        done = res[:2]
        return res[2:]

    sib_w_in, = proj_bwd_run(0, n_a, _pair_exchange([w_block]), "proj_bwd_a")
    w_sum = _pair_sum(w_block, sib_w_in, c_arr, tile=592, name="pair_sum_w_in")
    got_w_in_sums, = proj_bwd_run(n_a, n_b, _scatter_to_owners([w_sum]), "proj_bwd_b")
    w_in_half = _sum_chips(w_sum, got_w_in_sums, chip_arr, tile=592, name="sum_chips_w_in")
    n_c = min(1, n_pb - n_a - n_b - 1)
    w_in_other, = proj_bwd_run(n_a + n_b, n_c, _share_halves([w_in_half]), "proj_bwd_c")
    proj_bwd_run(n_a + n_b + n_c, n_pb - n_a - n_b - n_c, None, "proj_bwd_d")
    grad_x, dg1_p = done
    def update(name, g_half, g_other, tile, mats):
        outs = _adamw([tuple(tr(a) if t else a[0] for a in wmv) + (off,) for _, *wmv, off, t in mats],
                      g_half, g_other, c_arr, tile=tile, name=f"adamw_{name}")
        return {key: [jnp.transpose(o) if t else o for o in res] for (key, _, _, _, _, t), res in zip(mats, outs)}

    big = {
        **update("w_in", w_in_half, w_in_other, 592, [("w_in", w_in, m_w_in, v_w_in, 0, True)]),
        **update("ffn", ffn_half, ffn_other, 176, [
            ("wg", w_ffn_gate, m_w_ffn_gate, v_w_ffn_gate, 0, True),
            ("wu", w_ffn_up, m_w_ffn_up, v_w_ffn_up, FF_SH, True),
            ("wd", w_ffn_down, m_w_ffn_down, v_w_ffn_down, 2 * FF_SH, False)]),
        **update("mix", mix_half, mix_other, 128, [
            ("wba", w_branch_attn, m_w_branch_attn, v_w_branch_attn, 0, False),
            ("wbr", w_branch_ret, m_w_branch_ret, v_w_branch_ret, 256, False),
            ("wout", w_out, m_w_out, v_w_out, 768, False)])}

    loss_row, *small = _small_step(
        [loss_p.reshape(-1, 128), dg1_p.reshape(-1, D_MODEL), dg2_p.reshape(-1, D_MODEL), dgq, dgk, dsinks],
        [norm_mix_gain, norm_ffn_gain, q_norm_gain, k_norm_gain, attn_sinks,
         m_norm_mix_gain, m_norm_ffn_gain, m_q_norm_gain, m_k_norm_gain, m_attn_sinks,
         v_norm_mix_gain, v_norm_ffn_gain, v_q_norm_gain, v_k_norm_gain, v_attn_sinks])
    loss = loss_row[0, 0]

    def leaves(i):
        b = [big[n][i][None] for n in ("w_in", "wba", "wbr", "wout", "wg", "wu", "wd")]
        s1, s2, sq, sk, ss = small[5 * i:5 * i + 5]
        return [s1, b[0], sq, sk, ss, b[1], b[2], b[3], s2, b[4], b[5], b[6]]

    return (loss, grad_x[None], *leaves(0), *leaves(1), *leaves(2), *leaves(3))
```

```python
import jax
import jax.numpy as jnp
from jax import lax
from jax.experimental import pallas as pl
from jax.experimental.pallas import tpu as pltpu

F32 = jnp.float32
BF16 = jnp.bfloat16
MESH = pl.DeviceIdType.MESH

D_MODEL = 1024
EPS = 1e-6
HEAD_DIM = 64
N_Q_HEADS = 16
N_KV_HEADS = 2
GROUP = 8
BLOCK = 128
RET_HEADS = 4
RET_QK_DIM = 256
RET_V_DIM = 512
RET_CHUNK = 128
RET_ROT_BASE = 10000.0
D_FF = 2816
ATT_Q = N_Q_HEADS * HEAD_DIM
ATT_KV = N_KV_HEADS * HEAD_DIM
RET_QK = RET_HEADS * RET_QK_DIM
RET_V = RET_HEADS * RET_V_DIM
D_IN = 9472
ADAM_LR = 0.001
ADAM_B1 = 0.9
ADAM_B2 = 0.999
ADAM_EPS = 1e-08
ADAM_WD = 0.01
ADAM_STEP = 10

N_CHIPS = 4
N_DEV = 8
VMEM_LIMIT_BYTES = 60 * 1024 * 1024

P_QA = (0, 1024)
P_KVA = (1024, 256)
P_QR = (1280, 1024)
P_KR = (2304, 1024)
P_VR = (3328, 2048)
P_GR = (5376, 2048)
P_ZA = (7424, 1024)
P_ZR = (8448, 1024)

W_IN_SH = D_IN // N_CHIPS
FF_SH = D_FF // N_CHIPS

SMALL_ROWS = 8


def _dot(a, b):
    return jnp.dot(a, b, preferred_element_type=F32)


def _dot_nt(a, b):
    return lax.dot_general(a, b, (((1,), (1,)), ((), ())), preferred_element_type=F32)


def _dot_tn(a, b):
    return lax.dot_general(a, b, (((0,), (0,)), ((), ())), preferred_element_type=F32)


def _bf(x):
    return x.astype(BF16)


def _rms_stats(x):
    r = lax.rsqrt(jnp.mean(x * x, axis=-1, keepdims=True) + EPS)
    return r, x * r


def _rms_bwd(dy, xhat, r, gain):
    u = dy * gain
    dx = r * (u - xhat * jnp.mean(u * xhat, axis=-1, keepdims=True))
    return dx, dy * xhat


def _params(sem):
    return pltpu.CompilerParams(dimension_semantics=sem, vmem_limit_bytes=VMEM_LIMIT_BYTES)


_ANY = pl.BlockSpec(memory_space=pl.ANY)


class _Exchange:
    def __init__(self, ins, outs, n_sems, phases):
        self.ins, self.outs, self.n_sems, self.phases = list(ins), list(outs), n_sems, list(phases)


def _merge_exchanges(a, b):
    na_i, na_o, shift = len(a.ins), len(a.outs), a.n_sems

    def first(fn):
        return lambda i, o, s, r, base: fn(i[:na_i], o[:na_o], s, r, base)

    def second(fn):
        return lambda i, o, s, r, base: fn(i[na_i:], o[na_o:], s, r, base + shift)

    phases = [(f, first(fn)) for f, fn in a.phases] + [(f, second(fn)) for f, fn in b.phases]
    return _Exchange(a.ins + b.ins, a.outs + b.outs, a.n_sems + b.n_sems, sorted(phases, key=lambda p: p[0]))


def _pallas(kern, *, grid, in_specs, out_specs, out_shape, args, name, scratch=(), exchange=None, aliases=None):
    aliases = aliases or {}
    if exchange is None:
        return pl.pallas_call(
            kern, grid=grid, in_specs=in_specs, out_specs=out_specs, out_shape=out_shape, name=name,
            scratch_shapes=list(scratch), input_output_aliases=aliases,
            compiler_params=_params(("arbitrary",) * len(grid)))(*args)
    n_in, n_out, n_sc = len(in_specs), len(out_specs), len(scratch)
    n_xi, n_xo = len(exchange.ins), len(exchange.outs)
    n_steps = 1
    for g in grid:
        n_steps *= g

    def wrapped(*refs):
        ins, refs = refs[:n_in], refs[n_in:]
        x_ins, refs = refs[:n_xi], refs[n_xi:]
        outs, refs = refs[:n_out], refs[n_out:]
        x_outs, refs = refs[:n_xo], refs[n_xo:]
        scr, (send_sems, recv_sems) = refs[:n_sc], refs[n_sc:]
        step = pl.program_id(0)
        for d in range(1, len(grid)):
            step = step * grid[d] + pl.program_id(d)
        for frac, fn in exchange.phases:
            at = min(int(frac * n_steps), n_steps - 1)

            @pl.when(step == at)
            def _(fn=fn):
                fn(x_ins, x_outs, send_sems, recv_sems, 0)

        kern(*ins, *outs, *scr)

    sems = [pltpu.SemaphoreType.DMA((exchange.n_sems,)), pltpu.SemaphoreType.DMA((exchange.n_sems,))]
    return pl.pallas_call(
        wrapped, grid=grid, in_specs=list(in_specs) + [_ANY] * n_xi, out_specs=list(out_specs) + [_ANY] * n_xo,
        out_shape=list(out_shape) + exchange.outs, name=name, scratch_shapes=list(scratch) + sems,
        input_output_aliases=aliases, compiler_params=_params(("arbitrary",) * len(grid)))(*args, *exchange.ins)


def _run_exchange(exchange, name):
    def body(*refs):
        n_i, n_o = len(exchange.ins), len(exchange.outs)
        for _, fn in exchange.phases:
            fn(refs[:n_i], refs[n_i:n_i + n_o], refs[n_i + n_o], refs[n_i + n_o + 1], 0)

    sems = [pltpu.SemaphoreType.DMA((exchange.n_sems,)), pltpu.SemaphoreType.DMA((exchange.n_sems,))]
    return pl.pallas_call(body, in_specs=[_ANY] * len(exchange.ins), out_specs=[_ANY] * len(exchange.outs),
                          out_shape=exchange.outs, scratch_shapes=sems, name=name)(*exchange.ins)


def _fused(parts, *, grid, name, exchange=None):
    counts = [(len(p["in_specs"]), len(p["out_specs"]), len(p["scratch"])) for p in parts]
    n_in, n_out = sum(c[0] for c in counts), sum(c[1] for c in counts)

    def kern(*refs):
        ins, outs, scr = refs[:n_in], refs[n_in:n_in + n_out], refs[n_in + n_out:]
        i0 = o0 = s0 = 0
        for p, (ni, no, ns) in zip(parts, counts):
            p["kern"](*ins[i0:i0 + ni], *outs[o0:o0 + no], *scr[s0:s0 + ns])
            i0, o0, s0 = i0 + ni, o0 + no, s0 + ns

    cat = lambda key: [a for p in parts for a in p[key]]
    return _pallas(kern, grid=grid, in_specs=cat("in_specs"), out_specs=cat("out_specs"), out_shape=cat("out_shape"),
                   scratch=cat("scratch"), args=cat("args"), name=name, exchange=exchange)


def _row_call(body, *, tm, row_ins, res_ins, row_outs, part_outs=(), name, exchange=None):
    t = row_ins[0].shape[0]
    n_tiles = t // tm
    in_specs = [pl.BlockSpec((tm, a.shape[1]), lambda i: (i, 0)) for a in row_ins]
    in_specs += [pl.BlockSpec(a.shape, lambda i: (0, 0), pipeline_mode=pl.Buffered(1)) for a in res_ins]
    out_shape = [jax.ShapeDtypeStruct((t, w), dt) for (w, dt) in row_outs]
    out_shape += [jax.ShapeDtypeStruct((n_tiles, 1, w), F32) for w in part_outs]
    out_specs = [pl.BlockSpec((tm, w), lambda i: (i, 0)) for (w, _) in row_outs]
    out_specs += [pl.BlockSpec((1, 1, w), lambda i: (i, 0, 0)) for w in part_outs]
    n_ri, n_re, n_ro = len(row_ins), len(res_ins), len(row_outs)

    def kern(*refs):
        body(refs[:n_ri], refs[n_ri:n_ri + n_re], refs[n_ri + n_re:n_ri + n_re + n_ro], refs[n_ri + n_re + n_ro:])

    return _pallas(kern, grid=(n_tiles,), in_specs=in_specs, out_specs=out_specs, out_shape=out_shape,
                   args=[*row_ins, *res_ins], name=name, exchange=exchange)


def _proj_fwd(x, g1, w_in_t, exchange):
    pieces = ((P_QA, F32), (P_KVA, F32), (P_QR, F32), (P_KR, F32), (P_VR, BF16), (P_GR, F32), (P_ZA, F32), (P_ZR, F32))

    def body(ri, re, ro, po):
        x_t = ri[0][...]
        r, xhat = _rms_stats(x_t)
        hb = _bf(xhat * re[0][...])
        ro[0][...] = hb
        for k, ((off, w), dt) in enumerate(pieces):
            ro[1 + k][...] = _dot_nt(hb, re[1][off:off + w, :]).astype(dt)

    outs = [(D_MODEL, BF16)] + [(w, dt) for ((_, w), dt) in pieces]
    return _row_call(body, tm=256, row_ins=[x], res_ins=[g1, w_in_t], row_outs=outs, name="proj_fwd",
                     exchange=exchange)


def _mix_fwd(attn, ret, z_a, z_r, x, wba, wbr, wout, g2):
    def body(ri, re, ro, po):
        ba = _dot(ri[0][...], re[0][...])
        br = _dot(ri[1][...], re[1][...])
        m = jax.nn.sigmoid(ri[2][...]) * ba + jax.nn.sigmoid(ri[3][...]) * br
        mb = _bf(m)
        x1 = ri[4][...] + _dot(mb, re[2][...])
        r, xhat = _rms_stats(x1)
        ro[0][...] = ba
        ro[1][...] = br
        ro[2][...] = mb
        ro[3][...] = x1
        ro[4][...] = _bf(xhat * re[3][...])

    outs = [(D_MODEL, F32), (D_MODEL, F32), (D_MODEL, BF16), (D_MODEL, F32), (D_MODEL, BF16)]
    return _row_call(body, tm=512, row_ins=[attn, ret, z_a, z_r, x], res_ins=[wba, wbr, wout, g2], row_outs=outs,
                     name="mix_fwd")


def _ffn_fwd_bwd(h2, x1, target, wg_t, wu_t, wd, g2):
    def body(ri, re, ro, po):
        h2_t = ri[0][...]
        x1_t = ri[1][...]
        gate = _dot_nt(h2_t, re[0][...])
        up = _dot_nt(h2_t, re[1][...])
        sg = jax.nn.sigmoid(gate)
        sl = gate * sg
        actb = _bf(sl * up)
        ro[0][...] = actb
        y = x1_t + _dot(actb, re[2][...])
        e = y - ri[2][...]
        po[0][0] = jnp.broadcast_to(0.5 * jnp.sum(jnp.sum(e * e, axis=1, keepdims=True), axis=0, keepdims=True)
                                    * (1.0 / D_MODEL), (1, 128))
        dy = e * (1.0 / D_MODEL)
        dyb = _bf(dy)
        ro[3][...] = dyb
        dact = _dot_nt(dyb, re[2][...])
        dupb = _bf(dact * sl)
        dgateb = _bf(dact * up * (sg * (1.0 + gate * (1.0 - sg))))
        ro[1][...] = dgateb
        ro[2][...] = dupb
        dh2 = _dot(dgateb, re[0][...]) + _dot(dupb, re[1][...])
        r, xhat = _rms_stats(x1_t)
        dxn, dgain = _rms_bwd(dh2, xhat, r, re[3][...])
        dx1 = dy + dxn
        ro[4][...] = dx1
        ro[5][...] = _bf(dx1)
        po[1][0] = jnp.sum(dgain, axis=0, keepdims=True)

    outs = [(D_FF, BF16), (D_FF, BF16), (D_FF, BF16), (D_MODEL, BF16), (D_MODEL, F32), (D_MODEL, BF16)]
    return _row_call(body, tm=256, row_ins=[h2, x1, target], res_ins=[wg_t, wu_t, wd, g2], row_outs=outs,
                     part_outs=(128, D_MODEL), name="ffn_fwd_bwd")


def _mix_bwd(dx1b, z_a, z_r, ba, br, g_r, o_ret, wout, wba, wbr, exchange):
    def body(ri, re, ro, po):
        dm = _dot_nt(ri[0][...], re[0][...])
        sa = jax.nn.sigmoid(ri[1][...])
        sr = jax.nn.sigmoid(ri[2][...])
        dbab = _bf(sa * dm)
        dbrb = _bf(sr * dm)
        ro[0][...] = dbab
        ro[1][...] = dbrb
        ro[4][:, RET_V:RET_V + D_MODEL] = _bf(dm * ri[3][...] * (sa * (1.0 - sa)))
        ro[4][:, RET_V + D_MODEL:RET_V + 2 * D_MODEL] = _bf(dm * ri[4][...] * (sr * (1.0 - sr)))
        ro[2][...] = _bf(_dot_nt(dbab, re[1][...]))
        dret = _dot_nt(dbrb, re[2][...])
        for h in range(RET_HEADS):
            cols = slice(h * RET_V_DIM, (h + 1) * RET_V_DIM)
            g = ri[5][:, cols]
            r, rn = _rms_stats(ri[6][:, cols])
            sg = jax.nn.sigmoid(g)
            dret_h = dret[:, cols]
            d_rn = dret_h * (g * sg)
            ro[4][:, cols] = _bf(dret_h * rn * (sg * (1.0 + g * (1.0 - sg))))
            ro[3][:, cols] = r * (d_rn - rn * jnp.mean(d_rn * rn, axis=-1, keepdims=True))

    outs = [(D_MODEL, BF16), (D_MODEL, BF16), (ATT_Q, BF16), (RET_V, F32), (RET_V + 2 * D_MODEL, BF16)]
    return _row_call(body, tm=256, row_ins=[dx1b, z_a, z_r, ba, br, g_r, o_ret], res_ins=[wout, wba, wbr],
                     row_outs=outs, name="mix_bwd", exchange=exchange)


def _proj_bwd(d_pieces, x, dx1, w_in_t, g1, exchange):
    widths = [p.shape[1] for p in d_pieces]
    groups = [(sum(widths[:k]), w) for k, w in enumerate(widths)]
    n_p = len(groups)

    def body(ri, re, ro, po):
        dh = None
        for k, (off, w) in enumerate(groups):
            term = _dot(ri[k][...], re[0][off:off + w, :])
            dh = term if dh is None else dh + term
        r, xhat = _rms_stats(ri[n_p][...])
        dxn, dgain = _rms_bwd(dh, xhat, r, re[1][...])
        ro[0][...] = ri[n_p + 1][...] + dxn
        po[0][0] = jnp.sum(dgain, axis=0, keepdims=True)

    return _row_call(body, tm=512, row_ins=[*d_pieces, x, dx1], res_ins=[w_in_t, g1], row_outs=[(D_MODEL, F32)],
                     part_outs=(D_MODEL,), name="proj_bwd", exchange=exchange)


def _dw(a, b, *, tm, place, buf, name, exchange=None):
    t, m = a.shape
    n = b.shape[1]
    tk = min(2048, t)
    n_i, n_k = m // tm, t // tk
    fresh = isinstance(buf, jax.ShapeDtypeStruct)
    n_copies = len(place(0))

    def kern(a_ref, b_ref, *rest):
        out_ref, acc, sems = rest[-3:]
        i, k = pl.program_id(0), pl.program_id(1)
        part = _dot_tn(a_ref[...], b_ref[...])

        @pl.when(k == 0)
        def _():
            acc[i] = part

        @pl.when(k > 0)
        def _():
            acc[i] += part

        def copies(tile):
            return [pltpu.make_async_copy(acc.at[tile, pl.ds(r0, rows), :], out_ref.at[idx], sems.at[tile * n_copies + c])
                    for c, (r0, rows, idx) in enumerate(place(tile))]

        for tile in range(n_i):
            @pl.when((i == tile) & (k == n_k - 1))
            def _(tile=tile):
                for cp in copies(tile):
                    cp.start()

        @pl.when((i == n_i - 1) & (k == n_k - 1))
        def _():
            for tile in range(n_i):
                for cp in copies(tile):
                    cp.wait()

    in_specs = [pl.BlockSpec((tk, tm), lambda i, k: (k, i)), pl.BlockSpec((tk, n), lambda i, k: (k, 0))]
    shape = buf if fresh else jax.ShapeDtypeStruct(buf.shape, buf.dtype)
    return _pallas(
        kern, grid=(n_i, n_k), in_specs=in_specs + ([] if fresh else [_ANY]), out_specs=[_ANY], out_shape=[shape],
        scratch=[pltpu.VMEM((n_i, tm, n), F32), pltpu.SemaphoreType.DMA((n_i * n_copies,))],
        args=[a, b] + ([] if fresh else [buf]), aliases=None if fresh else {2: 0}, name=name, exchange=exchange)


def _heads_to_lanes(x3):
    return jnp.concatenate([x3[g] for g in range(GROUP)], axis=1)


def _lanes_to_heads(xt):
    return jnp.concatenate([xt[:, g * BLOCK:(g + 1) * BLOCK] for g in range(GROUP)], axis=0)


def _attn_group(n, kvh, q_ref, kvp_ref, kvc_ref, gq_col, gk, sink_ref):
    heads = [kvh * GROUP + g for g in range(GROUP)]
    cols = slice(kvh * GROUP * HEAD_DIM, (kvh + 1) * GROUP * HEAD_DIM)
    q3 = q_ref[:, cols].T.reshape(GROUP, HEAD_DIM, BLOCK)
    rq = lax.rsqrt(jnp.mean(q3 * q3, axis=1, keepdims=True) + EPS)
    qhat = q3 * rq
    qts = _heads_to_lanes(_bf(qhat * (gq_col * (HEAD_DIM ** -0.5))))
    kcols = slice(kvh * HEAD_DIM, (kvh + 1) * HEAD_DIM)
    vcols = slice(ATT_KV + kvh * HEAD_DIM, ATT_KV + (kvh + 1) * HEAD_DIM)
    k = jnp.concatenate([kvp_ref[:, kcols], kvc_ref[:, kcols]], axis=0)
    rk, khat = _rms_stats(k)
    knb = _bf(khat * gk)
    st = _dot(knb, qts)
    j = lax.broadcasted_iota(jnp.int32, (BLOCK, GROUP * BLOCK), 0)
    i = lax.broadcasted_iota(jnp.int32, (BLOCK, GROUP * BLOCK), 1) & (BLOCK - 1)
    from_prev = j > i
    f = jnp.where(from_prev, jnp.where(n > 0, st[0:BLOCK], -1e30), st[BLOCK:2 * BLOCK])
    sink = jnp.concatenate([jnp.broadcast_to(sink_ref[0:1, h:h + 1], (1, BLOCK)) for h in heads], axis=1)
    m = jnp.maximum(jnp.max(f, axis=0, keepdims=True), sink)
    e = jnp.exp(f - m)
    es = jnp.exp(sink - m)
    inv = 1.0 / (jnp.sum(e, axis=0, keepdims=True) + es)
    return dict(heads=heads, qhat=qhat, rq=rq, qts=qts, khat=khat, rk=rk, knb=knb, from_prev=from_prev,
                pf=e * inv, psink=es * inv)


def _unfold(from_prev, xf):
    return _bf(jnp.concatenate([jnp.where(from_prev, xf, 0.0), jnp.where(from_prev, 0.0, xf)], axis=0))


def _attn_fwd(q_a, kv_a, gq_col, gk, sinks):
    t = q_a.shape[0]
    nb = t // BLOCK

    def kern(q_ref, kvp_ref, kvc_ref, gq_ref, gk_ref, sink_ref, o_ref):
        n = pl.program_id(0)
        kvt = jnp.concatenate([kvp_ref[...].T, kvc_ref[...].T], axis=1)
        for kvh in range(N_KV_HEADS):
            a = _attn_group(n, kvh, q_ref, kvp_ref, kvc_ref, gq_ref[...], gk_ref[...], sink_ref)
            vt = _bf(kvt[ATT_KV + kvh * HEAD_DIM:ATT_KV + (kvh + 1) * HEAD_DIM, :])
            out_t = _dot(vt, _unfold(a["from_prev"], a["pf"]))
            cols = slice(kvh * GROUP * HEAD_DIM, (kvh + 1) * GROUP * HEAD_DIM)
            o_ref[:, cols] = _bf(_lanes_to_heads(out_t).T)

    small = lambda a: pl.BlockSpec(a.shape, lambda n: (0, 0))
    return dict(
        kern=kern,
        in_specs=[pl.BlockSpec((BLOCK, ATT_Q), lambda n: (n, 0)),
                  pl.BlockSpec((BLOCK, 2 * ATT_KV), lambda n: (jnp.maximum(n - 1, 0), 0)),
                  pl.BlockSpec((BLOCK, 2 * ATT_KV), lambda n: (n, 0)),
                  small(gq_col), small(gk), small(sinks)],
        out_specs=[pl.BlockSpec((BLOCK, ATT_Q), lambda n: (n, 0))],
        out_shape=[jax.ShapeDtypeStruct((t, ATT_Q), BF16)], scratch=[],
        args=[q_a, kv_a, kv_a, gq_col, gk, sinks])


def _attn_bwd(q_a, kv_a, d_attn, gq_col, gk, gk_col, sinks, exchange):
    t = q_a.shape[0]
    nb = t // BLOCK

    def kern(q_ref, kvp_ref, kvc_ref, do_ref, gq_ref, gk_ref, gkc_ref, sink_ref,
             dq_ref, dkv_ref, dgq_ref, dgk_ref, dsink_ref, band_k, band_v, carry_k, carry_v):
        n = pl.program_id(0)
        gq_v = gq_ref[...]
        gk_v = gk_ref[...]

        @pl.when(n == 0)
        def _():
            carry_k[...] = jnp.zeros_like(carry_k)
            carry_v[...] = jnp.zeros_like(carry_v)
            dgq_ref[...] = jnp.zeros_like(dgq_ref)
            dgk_ref[...] = jnp.zeros_like(dgk_ref)
            dsink_ref[...] = jnp.zeros_like(dsink_ref)

        @pl.when(n == nb)
        def _():
            band_k[...] = jnp.zeros_like(band_k)
            band_v[...] = jnp.zeros_like(band_v)

        @pl.when(n < nb)
        def _():
            lane16 = lax.broadcasted_iota(jnp.int32, (1, N_Q_HEADS), 1)
            dsink = jnp.zeros((1, N_Q_HEADS), F32)
            dgq = jnp.zeros((HEAD_DIM, 1), F32)
            gk_col = gkc_ref[...]
            kvt = jnp.concatenate([kvp_ref[...].T, kvc_ref[...].T], axis=1)
            for kvh in range(N_KV_HEADS):
                a = _attn_group(n, kvh, q_ref, kvp_ref, kvc_ref, gq_v, gk_v, sink_ref)
                from_prev, pf, qhat = a["from_prev"], a["pf"], a["qhat"]
                cols = slice(kvh * GROUP * HEAD_DIM, (kvh + 1) * GROUP * HEAD_DIM)
                vcols = slice(ATT_KV + kvh * HEAD_DIM, ATT_KV + (kvh + 1) * HEAD_DIM)
                dot = _heads_to_lanes(_bf(do_ref[:, cols].astype(F32).T.reshape(GROUP, HEAD_DIM, BLOCK)))
                vb = _bf(jnp.concatenate([kvp_ref[:, vcols], kvc_ref[:, vcols]], axis=0))
                dpt = _dot(vb, dot)
                dpf = jnp.where(from_prev, dpt[0:BLOCK], dpt[BLOCK:2 * BLOCK])
                delta = jnp.sum(pf * dpf, axis=0, keepdims=True)
                dst = _unfold(from_prev, pf * (dpf - delta))
                dsk = a["psink"] * delta
                for g, h in enumerate(a["heads"]):
                    tot = jnp.sum(dsk[:, g * BLOCK:(g + 1) * BLOCK], axis=1, keepdims=True)
                    dsink = dsink - jnp.where(lane16 == h, tot, 0.0)
                kt = kvt[kvh * HEAD_DIM:(kvh + 1) * HEAD_DIM, :]
                knt = _bf(kt * lax.rsqrt(jnp.mean(kt * kt, axis=0, keepdims=True) + EPS) * gk_col)
                dqn = (_dot(knt, dst) * (HEAD_DIM ** -0.5))
                band_k[kvh] = _dot_nt(dst, a["qts"])
                band_v[kvh] = _dot_nt(_unfold(from_prev, pf), dot)
                dqn3 = _lanes_to_heads(dqn).reshape(GROUP, HEAD_DIM, BLOCK)
                u = dqn3 * gq_v
                dq3 = a["rq"] * (u - qhat * jnp.mean(u * qhat, axis=1, keepdims=True))
                dgq = dgq + jnp.sum(jnp.sum(dqn3 * qhat, axis=0), axis=1, keepdims=True)
                dq_ref[:, cols] = _bf(dq3.reshape(GROUP * HEAD_DIM, BLOCK).T)
            dsink_ref[...] += dsink
            dgq_ref[...] += dgq

        dgk = jnp.zeros((1, HEAD_DIM), F32)
        for kvh in range(N_KV_HEADS):
            kcols = slice(kvh * HEAD_DIM, (kvh + 1) * HEAD_DIM)
            vcols = slice(ATT_KV + kvh * HEAD_DIM, ATT_KV + (kvh + 1) * HEAD_DIM)
            dkn = carry_k[kvh] + band_k[kvh, 0:BLOCK, :]
            dv = carry_v[kvh] + band_v[kvh, 0:BLOCK, :]
            rk, khat = _rms_stats(kvp_ref[:, kcols])
            dk, dgain = _rms_bwd(dkn, khat, rk, gk_v)
            dgk = dgk + jnp.sum(dgain, axis=0, keepdims=True)
            dkv_ref[:, kcols] = _bf(dk)
            dkv_ref[:, vcols] = _bf(dv)
            carry_k[kvh] = band_k[kvh, BLOCK:2 * BLOCK, :]
            carry_v[kvh] = band_v[kvh, BLOCK:2 * BLOCK, :]
        dgk_ref[...] += dgk

    small = lambda a: pl.BlockSpec(a.shape, lambda n: (0, 0))
    last = nb - 1
    return _pallas(
        kern, grid=(nb + 1,),
        in_specs=[pl.BlockSpec((BLOCK, ATT_Q), lambda n: (jnp.minimum(n, last), 0)),
                  pl.BlockSpec((BLOCK, 2 * ATT_KV), lambda n: (jnp.maximum(n - 1, 0), 0)),
                  pl.BlockSpec((BLOCK, 2 * ATT_KV), lambda n: (jnp.minimum(n, last), 0)),
                  pl.BlockSpec((BLOCK, ATT_Q), lambda n: (jnp.minimum(n, last), 0)),
                  small(gq_col), small(gk), small(gk_col), small(sinks)],
        out_specs=[pl.BlockSpec((BLOCK, ATT_Q), lambda n: (jnp.minimum(n, last), 0)),
                   pl.BlockSpec((BLOCK, 2 * ATT_KV), lambda n: (jnp.maximum(n - 1, 0), 0)),
                   pl.BlockSpec((HEAD_DIM, 1), lambda n: (0, 0)),
                   pl.BlockSpec((1, HEAD_DIM), lambda n: (0, 0)),
                   pl.BlockSpec((1, N_Q_HEADS), lambda n: (0, 0))],
        out_shape=[jax.ShapeDtypeStruct((t, ATT_Q), BF16), jax.ShapeDtypeStruct((t, 2 * ATT_KV), BF16),
                   jax.ShapeDtypeStruct((HEAD_DIM, 1), F32), jax.ShapeDtypeStruct((1, HEAD_DIM), F32),
                   jax.ShapeDtypeStruct((1, N_Q_HEADS), F32)],
        scratch=[pltpu.VMEM((N_KV_HEADS, 2 * BLOCK, HEAD_DIM), F32),
                 pltpu.VMEM((N_KV_HEADS, 2 * BLOCK, HEAD_DIM), F32),
                 pltpu.VMEM((N_KV_HEADS, BLOCK, HEAD_DIM), F32),
                 pltpu.VMEM((N_KV_HEADS, BLOCK, HEAD_DIM), F32)],
        args=[q_a, kv_a, kv_a, d_attn, gq_col, gk, gk_col, sinks], name="attn_bwd", exchange=exchange)


def _ret_tables(t, exchange):
    theta = 1.0 / (RET_ROT_BASE ** jnp.linspace(0.0, 1.0, RET_QK_DIM // 2, dtype=F32))
    theta2 = jnp.repeat(theta, 2)[None, :]
    sign = jnp.tile(jnp.array([-1.0, 1.0], F32), RET_QK_DIM // 2)[None, :]

    def kern(theta_ref, sign_ref, cos_ref, sin_ref):
        first = pl.program_id(0) * RET_CHUNK
        pos = (first + lax.broadcasted_iota(jnp.int32, (RET_CHUNK, RET_QK_DIM), 0)).astype(F32)
        ang = pos * theta_ref[...]
        cos_ref[...] = jnp.cos(ang)
        sin_ref[...] = jnp.sin(ang) * sign_ref[...]

    row = pl.BlockSpec((1, RET_QK_DIM), lambda n: (0, 0))
    blk = pl.BlockSpec((RET_CHUNK, RET_QK_DIM), lambda n: (n, 0))
    cos, sin_s, *got = _pallas(kern, grid=(t // RET_CHUNK,), in_specs=[row, row], out_specs=[blk, blk],
                               out_shape=[jax.ShapeDtypeStruct((t, RET_QK_DIM), F32)] * 2, args=[theta2, sign],
                               name="position_tables", exchange=exchange)
    log_gamma = jnp.log(1.0 - 2.0 ** (-5.0 - jnp.arange(RET_HEADS, dtype=F32)))
    i = jnp.arange(RET_CHUNK, dtype=F32)
    diff = i[:, None] - i[None, :]
    causal = diff >= 0
    decay = jnp.where(causal[None], jnp.exp(jnp.where(causal, diff, 0.0)[None] * log_gamma[:, None, None]), 0.0)
    xi = jnp.exp((i + 1.0)[None, :] * log_gamma[:, None])[:, :, None]
    zeta = jnp.exp((RET_CHUNK - 1.0 - i)[None, :] * log_gamma[:, None])[:, :, None]
    gch = jnp.broadcast_to(jnp.exp(RET_CHUNK * log_gamma)[:, None, None], (RET_HEADS, 1, 128))
    return (cos, sin_s, decay, xi, zeta, gch), got


def _swap_pairs(x):
    lane = lax.broadcasted_iota(jnp.int32, x.shape, 1)
    return jnp.where((lane & 1) == 0, pltpu.roll(x, RET_QK_DIM - 1, 1), pltpu.roll(x, 1, 1))


def _rotate(x, cos, sin_s):
    return x * cos + _swap_pairs(x) * sin_s


def _rotate_bwd(dy, cos, sin_s):
    return dy * cos + _swap_pairs(dy * sin_s)


def _ret_specs(order):
    qk = pl.BlockSpec((RET_CHUNK, RET_QK), lambda j: (order(j), 0))
    v = pl.BlockSpec((RET_CHUNK, RET_V), lambda j: (order(j), 0))
    dec = pl.BlockSpec((RET_HEADS, RET_CHUNK, RET_CHUNK), lambda j: (0, 0, 0))
    col = pl.BlockSpec((RET_HEADS, RET_CHUNK, 1), lambda j: (0, 0, 0))
    gch = pl.BlockSpec((RET_HEADS, 1, 128), lambda j: (0, 0, 0))
    st = pl.BlockSpec((RET_HEADS, None, RET_QK_DIM, RET_V_DIM), lambda j: (0, order(j), 0, 0))
    pos = pl.BlockSpec((RET_CHUNK, RET_QK_DIM), lambda j: (order(j), 0))
    return qk, v, dec, col, gch, st, pos


def _ret_fwd(q_r, k_r, v_r, g_r, tables):
    t = q_r.shape[0]
    nc = t // RET_CHUNK
    cos, sin_s, decay, xi, zeta, gch = tables

    def kern(q_ref, k_ref, v_ref, g_ref, cos_ref, sin_ref, dec_ref, xi_ref, zeta_ref, gch_ref,
             o_ref, ret_ref, st_ref, state):
        @pl.when(pl.program_id(0) == 0)
        def _():
            state[...] = jnp.zeros_like(state)

        cos_t = cos_ref[...]
        sin_t = sin_ref[...]
        for h in range(RET_HEADS):
            qc = slice(h * RET_QK_DIM, (h + 1) * RET_QK_DIM)
            vc = slice(h * RET_V_DIM, (h + 1) * RET_V_DIM)
            qs = _bf(_rotate(q_ref[:, qc], cos_t, sin_t))
            ks = _rotate(k_ref[:, qc] * (RET_QK_DIM ** -0.5), cos_t, sin_t)
            vb = v_ref[:, vc]
            s_old = state[h]
            sb = _bf(s_old)
            st_ref[h] = sb
            inner = _dot_nt(qs, _bf(ks)) * dec_ref[h]
            out = _dot(_bf(inner), vb) + _dot(qs, sb) * xi_ref[h]
            state[h] = gch_ref[h, :, 0:1] * s_old + _dot_tn(_bf(ks * zeta_ref[h]), vb)
            o_ref[:, vc] = out
            r, rn = _rms_stats(out)
            g = g_ref[:, vc]
            ret_ref[:, vc] = _bf(g * jax.nn.sigmoid(g) * rn)

    qk, v, dec, col, gsp, st, pos = _ret_specs(lambda j: j)
    return dict(
        kern=kern,
        in_specs=[qk, qk, v, v, pos, pos, dec, col, col, gsp],
        out_specs=[v, v, st],
        out_shape=[jax.ShapeDtypeStruct((t, RET_V), F32), jax.ShapeDtypeStruct((t, RET_V), BF16),
                   jax.ShapeDtypeStruct((RET_HEADS, nc, RET_QK_DIM, RET_V_DIM), BF16)],
        scratch=[pltpu.VMEM((RET_HEADS, RET_QK_DIM, RET_V_DIM), F32)],
        args=[q_r, k_r, v_r, g_r, cos, sin_s, decay, xi, zeta, gch])


def _ret_bwd(q_r, k_r, v_r, d_o, states, tables, exchange):
    t = q_r.shape[0]
    nc = t // RET_CHUNK
    cos, sin_s, decay, xi, zeta, gch = tables

    def kern(q_ref, k_ref, v_ref, do_ref, st_ref, cos_ref, sin_ref, dec_ref, xi_ref, zeta_ref, gch_ref,
             d_ref, dstate):
        dq_ref, dk_ref = d_ref.at[:, 0:RET_QK], d_ref.at[:, RET_QK:2 * RET_QK]
        dv_ref = d_ref.at[:, 2 * RET_QK:2 * RET_QK + RET_V]

        @pl.when(pl.program_id(0) == 0)
        def _():
            dstate[...] = jnp.zeros_like(dstate)

        cos_t = cos_ref[...]
        sin_t = sin_ref[...]
        scale = RET_QK_DIM ** -0.5
        for h in range(RET_HEADS):
            qc = slice(h * RET_QK_DIM, (h + 1) * RET_QK_DIM)
            vc = slice(h * RET_V_DIM, (h + 1) * RET_V_DIM)
            qs = _bf(_rotate(q_ref[:, qc], cos_t, sin_t))
            ks = _rotate(k_ref[:, qc] * scale, cos_t, sin_t)
            ksb = _bf(ks)
            vb = v_ref[:, vc]
            d_o_t = do_ref[:, vc]
            dob = _bf(d_o_t)
            doxb = _bf(d_o_t * xi_ref[h])
            dec = dec_ref[h]
            ds_old = dstate[h]
            dsb = _bf(ds_old)
            pb = _bf(_dot_nt(qs, ksb) * dec)
            dpb = _bf(_dot_nt(dob, vb) * dec)
            dqs = _dot(dpb, ksb) + _dot_nt(doxb, st_ref[h])
            dks = _dot_tn(dpb, qs) + _dot_nt(vb, dsb) * zeta_ref[h]
            dv_ref[:, vc] = _bf(_dot_tn(pb, dob) + _dot(_bf(ks * zeta_ref[h]), dsb))
            dstate[h] = gch_ref[h, :, 0:1] * ds_old + _dot_tn(qs, doxb)
            dq_ref[:, qc] = _bf(_rotate_bwd(dqs, cos_t, sin_t))
            dk_ref[:, qc] = _bf(_rotate_bwd(dks, cos_t, sin_t) * scale)

    qk, v, dec, col, gsp, st, pos = _ret_specs(lambda j: nc - 1 - j)
    return _pallas(
        kern, grid=(nc,),
        in_specs=[qk, qk, v, v, st, pos, pos, dec, col, col, gsp],
        out_specs=[pl.BlockSpec((RET_CHUNK, 2 * RET_QK + RET_V), lambda j: (nc - 1 - j, 0))],
        out_shape=[jax.ShapeDtypeStruct((t, 2 * RET_QK + RET_V), BF16)],
        scratch=[pltpu.VMEM((RET_HEADS, RET_QK_DIM, RET_V_DIM), F32)],
        args=[q_r, k_r, v_r, d_o, states, cos, sin_s, decay, xi, zeta, gch], name="ret_bwd", exchange=exchange)


def _position():
    return lax.axis_index("x"), lax.axis_index("y"), lax.axis_index("c")


def _gather_exchange(owns, forward_at):
    n = len(owns)

    def copies(ins, outs, send_sems, recv_sems, base):
        x, y, c = _position()
        sibling = (x, y, 1 - c)
        chips = [(1 - x, y), (x, 1 - y), (1 - x, 1 - y)]
        my_chip = 2 * x + y

        def slab(a, chip, hf):
            half = owns[a].shape[0] // 2
            return outs[a].at[chip, pl.ds(hf * half, half), :]

        def copy(k, src, dst, to):
            return pltpu.make_async_remote_copy(src_ref=src, dst_ref=dst, send_sem=send_sems.at[base + k],
                                                recv_sem=recv_sems.at[base + k], device_id=to, device_id_type=MESH)

        first, passed, from_sibling = [], [], []
        for a in range(n):
            half = owns[a].shape[0] // 2
            for k, (cx, cy) in enumerate(chips):
                first.append(copy(6 * a + k, ins[a].at[pl.ds(c * half, half), :], slab(a, my_chip, c), (cx, cy, c)))
                landed = slab(a, 2 * cx + cy, c)
                passed.append(copy(6 * a + 3 + k, landed, landed, sibling))
                theirs = slab(a, 2 * cx + cy, 1 - c)
                from_sibling.append(copy(6 * a + 3 + k, theirs, theirs, sibling))
        return first, passed, from_sibling

    def start(*args):
        first, _, _ = copies(*args)
        for cp in first:
            cp.start()

    def forward(*args):
        first, passed, _ = copies(*args)
        for arrived, cp in zip(first, passed):
            arrived.wait_recv()
            cp.start()

    def finish(*args):
        first, passed, from_sibling = copies(*args)
        for cp in from_sibling:
            cp.wait_recv()
        for cp in first + passed:
            cp.wait_send()

    outs = [jax.ShapeDtypeStruct((N_CHIPS, *a.shape), a.dtype) for a in owns]
    return _Exchange(owns, outs, 6 * n, [(0.0, start), (forward_at, forward), (1.0, finish)])


def _symmetric_exchange(ins, outs, plan):
    n_sems = len(plan([None] * len(ins), [None] * len(outs), 0, 0, 0, dry=True))

    def copies(in_refs, out_refs, send_sems, recv_sems, base):
        x, y, c = _position()
        return [pltpu.make_async_remote_copy(src_ref=src, dst_ref=dst, send_sem=send_sems.at[base + k],
                                             recv_sem=recv_sems.at[base + k], device_id=dev, device_id_type=MESH)
                for k, (src, dst, dev) in enumerate(plan(in_refs, out_refs, x, y, c, dry=False))]

    def start(*args):
        for cp in copies(*args):
            cp.start()

    def finish(*args):
        for cp in copies(*args):
            cp.wait()

    return _Exchange(ins, outs, n_sems, [(0.0, start), (1.0, finish)])


def _pair_exchange(gs):
    def plan(in_refs, out_refs, x, y, c, dry):
        out = []
        for a, g in enumerate(gs):
            half = g.shape[1] // 2
            for k in range(N_CHIPS):
                out.append(None if dry else (in_refs[a].at[k, pl.ds((1 - c) * half, half), :], out_refs[a].at[k],
                                             (x, y, 1 - c)))
        return out

    outs = [jax.ShapeDtypeStruct((g.shape[0], g.shape[1] // 2, g.shape[2]), g.dtype) for g in gs]
    return _symmetric_exchange(gs, outs, plan)


def _pair_sum(g, from_sibling, c_arr, *, tile, name):
    n, rows, width = g.shape
    tiles = (rows // 2) // tile

    def kern(c_ref, g_ref, s_ref, o_ref):
        o_ref[...] = _bf(g_ref[...] + s_ref[...])

    return pl.pallas_call(
        kern,
        grid_spec=pltpu.PrefetchScalarGridSpec(
            num_scalar_prefetch=1, grid=(n, tiles),
            in_specs=[pl.BlockSpec((None, tile, width), lambda k, i, c: (k, c[0] * tiles + i, 0)),
                      pl.BlockSpec((None, tile, width), lambda k, i, c: (k, i, 0))],
            out_specs=pl.BlockSpec((None, tile, width), lambda k, i, c: (k, i, 0))),
        out_shape=jax.ShapeDtypeStruct((n, rows // 2, width), BF16), name=name,
        compiler_params=_params(("parallel", "parallel")),
    )(c_arr, g, from_sibling)


def _scatter_to_owners(hsums):
    def plan(in_refs, out_refs, x, y, c, dry):
        out = []
        for a in range(len(hsums)):
            for k, (cx, cy) in enumerate([(1 - x, y), (x, 1 - y), (1 - x, 1 - y)]):
                out.append(None if dry else (in_refs[a].at[2 * cx + cy], out_refs[a].at[k], (cx, cy, c)))
        return out

    outs = [jax.ShapeDtypeStruct((3, *h.shape[1:]), h.dtype) for h in hsums]
    return _symmetric_exchange(hsums, outs, plan)


def _sum_chips(hsum, parts, chip_arr, *, tile, name):
    n, half, width = parts.shape

    def kern(chip_ref, h_ref, p_ref, o_ref):
        acc = h_ref[...].astype(F32)
        for k in range(n):
            acc = acc + p_ref[k].astype(F32)
        o_ref[...] = acc

    return pl.pallas_call(
        kern,
        grid_spec=pltpu.PrefetchScalarGridSpec(
            num_scalar_prefetch=1, grid=(half // tile,),
            in_specs=[pl.BlockSpec((None, tile, width), lambda i, chip: (chip[0], i, 0)),
                      pl.BlockSpec((n, tile, width), lambda i, chip: (0, i, 0))],
            out_specs=pl.BlockSpec((tile, width), lambda i, chip: (i, 0))),
        out_shape=jax.ShapeDtypeStruct((half, width), F32), name=name,
        compiler_params=_params(("parallel",)),
    )(chip_arr, hsum, parts)


def _share_halves(fhalves):
    def plan(in_refs, out_refs, x, y, c, dry):
        return [None if dry else (in_refs[a], out_refs[a], (x, y, 1 - c)) for a in range(len(fhalves))]

    return _symmetric_exchange(fhalves, [jax.ShapeDtypeStruct(f.shape, f.dtype) for f in fhalves], plan)


def _adamw_math(w, g, m, v):
    m = ADAM_B1 * m + (1.0 - ADAM_B1) * g
    v = ADAM_B2 * v + (1.0 - ADAM_B2) * (g * g)
    m_hat = m / (1.0 - ADAM_B1 ** ADAM_STEP)
    v_hat = v / (1.0 - ADAM_B2 ** ADAM_STEP)
    delta = -ADAM_LR * (m_hat / (jnp.sqrt(v_hat) + ADAM_EPS) + ADAM_WD * w)
    return delta, m, v


def _adamw(mats, g_mine, g_other, c_arr, *, tile, name):
    width = g_mine.shape[1]
    tiles_per_half = g_mine.shape[0] // tile
    n_tiles = [w.shape[0] // tile for w, _, _, _ in mats]
    n_mats = len(mats)

    def kern(c_ref, *refs):
        ins, outs = refs[:5 * n_mats], refs[5 * n_mats:]
        for j, (_, _, _, row_off) in enumerate(mats):
            w_ref, gm_ref, go_ref, m_ref, v_ref = ins[5 * j:5 * j + 5]
            i = jnp.minimum(pl.program_id(0), n_tiles[j] - 1)
            in_my_half = ((row_off // tile + i) // tiles_per_half) == c_ref[0]
            g = jnp.where(in_my_half, gm_ref[...], go_ref[...])
            d, nm, nv = _adamw_math(w_ref[...], g, m_ref[...], v_ref[...])
            for out_ref, val in zip(outs[4 * j:4 * j + 4], (g, d, nm, nv)):
                out_ref[...] = val

    in_specs, out_specs, out_shape, args = [], [], [], []
    for (w, m, v, row_off), nt in zip(mats, n_tiles):
        full = pl.BlockSpec((tile, width), lambda i, c, nt=nt: (jnp.minimum(i, nt - 1), 0))
        half = pl.BlockSpec((tile, width), lambda i, c, nt=nt, first=row_off // tile:
                            ((first + jnp.minimum(i, nt - 1)) % tiles_per_half, 0))
        in_specs += [full, half, half, full, full]
        out_specs += [full] * 4
        out_shape += [jax.ShapeDtypeStruct(w.shape, F32)] * 4
        args += [w, g_mine, g_other, m, v]
    outs = pl.pallas_call(
        kern,
        grid_spec=pltpu.PrefetchScalarGridSpec(num_scalar_prefetch=1, grid=(max(n_tiles),), in_specs=in_specs,
                                               out_specs=out_specs),
        out_shape=out_shape, name=name, compiler_params=_params(("arbitrary",)),
    )(c_arr, *args)
    return [outs[4 * j:4 * j + 4] for j in range(n_mats)]


def _small_step(partials, params):
    slots = ((0, 0, D_MODEL), (1, 0, D_MODEL), (2, 0, HEAD_DIM), (2, 128, HEAD_DIM), (2, 256, N_Q_HEADS))
    loss_slot = (2, 384, 128)

    def body(*refs):
        loss_ref, dg1_ref, dg2_ref, dgq_ref, dgk_ref, dsink_ref = refs[:6]
        p_refs, out_refs = refs[6:21], refs[21:42]
        mine, gathered, send_sems, recv_sems = refs[42:]
        x, y, c = _position()
        me = 4 * x + 2 * y + c
        mine[...] = jnp.zeros_like(mine)
        for (row, lane, n), val in zip(slots + (loss_slot,), (
                jnp.sum(dg1_ref[...], axis=0, keepdims=True), jnp.sum(dg2_ref[...], axis=0, keepdims=True),
                dgq_ref[...], dgk_ref[...], dsink_ref[...], jnp.sum(loss_ref[...], axis=0, keepdims=True))):
            mine[row:row + 1, lane:lane + n] = val
        copies = []
        for k in range(1, N_DEV):
            flip = (k >> 2) & 1, (k >> 1) & 1, k & 1
            to = (x ^ flip[0], y ^ flip[1], c ^ flip[2])
            cp = pltpu.make_async_remote_copy(
                src_ref=mine, dst_ref=gathered.at[me], send_sem=send_sems.at[k - 1], recv_sem=recv_sems.at[k - 1],
                device_id=to, device_id_type=MESH)
            cp.start()
            copies.append(cp)
        gathered[me] = mine[...]
        for k in range(1, N_DEV):
            flip = (k >> 2) & 1, (k >> 1) & 1, k & 1
            src = 4 * (x ^ flip[0]) + 2 * (y ^ flip[1]) + (c ^ flip[2])
            pltpu.make_async_remote_copy(
                src_ref=mine, dst_ref=gathered.at[src], send_sem=send_sems.at[k - 1], recv_sem=recv_sems.at[k - 1],
                device_id=(x, y, c), device_id_type=MESH).wait_recv()
        for cp in copies:
            cp.wait_send()
        total = gathered[0]
        for k in range(1, N_DEV):
            total = total + gathered[k]
        row, lane, n = loss_slot
        out_refs[0][...] = total[row:row + 1, lane:lane + n]
        for i, (row, lane, n) in enumerate(slots):
            g = total[row:row + 1, lane:lane + n]
            d, nm, nv = _adamw_math(p_refs[i][...], g, p_refs[5 + i][...], p_refs[10 + i][...])
            for kind, val in enumerate((g, d, nm, nv)):
                out_refs[1 + 5 * kind + i][...] = val

    vm = pl.BlockSpec(memory_space=pltpu.VMEM)
    shapes = [jax.ShapeDtypeStruct((1, 128), F32)] + [jax.ShapeDtypeStruct((1, n), F32) for _, _, n in slots] * 4
    return pl.pallas_call(
        body, in_specs=[vm] * 21, out_specs=[vm] * 21, out_shape=shapes,
        scratch_shapes=[pltpu.VMEM((SMALL_ROWS, D_MODEL), F32), pltpu.VMEM((N_DEV, SMALL_ROWS, D_MODEL), F32),
                        pltpu.SemaphoreType.DMA((N_DEV - 1,)), pltpu.SemaphoreType.DMA((N_DEV - 1,))],
        name="small_step",
    )(*partials, *params)


def _with_own(gathered, own, my_chip):
    return lax.dynamic_update_slice(gathered, own[None], (my_chip, 0, 0))


def kernel(x, norm_mix_gain, w_in, q_norm_gain, k_norm_gain, attn_sinks, w_branch_attn, w_branch_ret, w_out, norm_ffn_gain, w_ffn_gate, w_ffn_up, w_ffn_down, loss_target, m_norm_mix_gain, m_w_in, m_q_norm_gain, m_k_norm_gain, m_attn_sinks, m_w_branch_attn, m_w_branch_ret, m_w_out, m_norm_ffn_gain, m_w_ffn_gate, m_w_ffn_up, m_w_ffn_down, v_norm_mix_gain, v_w_in, v_q_norm_gain, v_k_norm_gain, v_attn_sinks, v_w_branch_attn, v_w_branch_ret, v_w_out, v_norm_ffn_gain, v_w_ffn_gate, v_w_ffn_up, v_w_ffn_down):
    my_chip = 2 * lax.axis_index("x") + lax.axis_index("y")
    c_arr = lax.axis_index("c").astype(jnp.int32).reshape(1)
    chip_arr = my_chip.astype(jnp.int32).reshape(1)
    x_t, target = x[0], loss_target[0]
    g1, g2, gq, gk, sinks = norm_mix_gain, norm_ffn_gain, q_norm_gain, k_norm_gain, attn_sinks

    tr = lambda a: jnp.transpose(a[0])
    own_w_in = _bf(tr(w_in))
    own_rest = [_bf(a) for a in (tr(w_ffn_gate), tr(w_ffn_up), w_ffn_down[0], w_branch_attn[0], w_branch_ret[0],
                                 w_out[0])]
    tables, (got_w_in,) = _ret_tables(x_t.shape[0], _gather_exchange([own_w_in], 0.9))
    w_in_t = _with_own(got_w_in, own_w_in, my_chip).reshape(D_IN, D_MODEL)
    h1, q_a, kv_a, q_r, k_r, v_r, g_r, z_a, z_r, *got_rest = _proj_fwd(x_t, g1, w_in_t, _gather_exchange(own_rest, 0.8))
    wg_t, wu_t, wd, wba, wbr, wout = [_with_own(got, own, my_chip).reshape(N_CHIPS * own.shape[0], D_MODEL)
                                      for got, own in zip(got_rest, own_rest)]

    gq_col, gk_col = gq.reshape(HEAD_DIM, 1), gk.reshape(HEAD_DIM, 1)
    attn, o_ret, ret, states = _fused([_attn_fwd(q_a, kv_a, gq_col, gk, sinks), _ret_fwd(q_r, k_r, v_r, g_r, tables)],
                                      grid=(x_t.shape[0] // BLOCK,), name="mixers_fwd")
    ba, br, merged, x1, h2 = _mix_fwd(attn, ret, z_a, z_r, x_t, wba, wbr, wout, g2)
    act, dgate, dup, dyb, dx1, dx1b, loss_p, dg2_p = _ffn_fwd_bwd(h2, x1, target, wg_t, wu_t, wd, g2)

    def pairs(row0, rows):
        return lambda i: [(h * rows, rows, (2 * i + h, pl.ds(row0, rows), slice(None))) for h in range(2)]

    f_block = jax.ShapeDtypeStruct((N_CHIPS, 3 * FF_SH, D_MODEL), F32)
    f_block, = _dw(dgate, h2, tm=2 * FF_SH, place=pairs(0, FF_SH), buf=f_block, name="dw_gate")
    f_block, = _dw(dup, h2, tm=2 * FF_SH, place=pairs(FF_SH, FF_SH), buf=f_block, name="dw_up")
    f_block, = _dw(act, dyb, tm=2 * FF_SH, place=pairs(2 * FF_SH, FF_SH), buf=f_block, name="dw_down")
    (dba, dbr, d_attn, d_o, d_gz, sib_ffn) = _mix_bwd(
        dx1b, z_a, z_r, ba, br, g_r, o_ret, wout, wba, wbr, _pair_exchange([f_block]))
    f_sum = _pair_sum(f_block, sib_ffn, c_arr, tile=528, name="pair_sum_ffn")

    def quarters(row0, rows):
        return lambda i: [(k * rows, rows, (k, pl.ds(row0, rows), slice(None))) for k in range(N_CHIPS)]

    m_block = jax.ShapeDtypeStruct((N_CHIPS, D_MODEL, D_MODEL), F32)
    m_block, = _dw(attn, dba, tm=ATT_Q, place=quarters(0, 256), buf=m_block, name="dw_ba")
    m_block, = _dw(ret, dbr, tm=D_MODEL, place=pairs(256, 512), buf=m_block, name="dw_br")
    m_block, = _dw(merged, dx1b, tm=D_MODEL, place=quarters(768, 256), buf=m_block, name="dw_out")

    def w_in_rows(off, w):
        tm = min(w, D_MODEL)
        return dict(tm=tm, place=lambda i: [(0, tm, (pl.ds(off + i * tm, tm), slice(None)))])

    w_block = jax.ShapeDtypeStruct((D_IN, D_MODEL), F32)
    w_block, sib_mix = _dw(d_gz, h1, buf=w_block, name="dw_in_gz", exchange=_pair_exchange([m_block]),
                           **w_in_rows(P_GR[0], d_gz.shape[1]))
    m_sum = _pair_sum(m_block, sib_mix, c_arr, tile=256, name="pair_sum_mix")

    d_ret, got_ffn_sums = _ret_bwd(q_r, k_r, v_r, d_o, states, tables, _scatter_to_owners([f_sum]))
    ffn_half = _sum_chips(f_sum, got_ffn_sums, chip_arr, tile=528, name="sum_chips_ffn")
    w_block, = _dw(d_ret, h1, buf=w_block, name="dw_in_ret", **w_in_rows(P_QR[0], d_ret.shape[1]))

    (dq_a, dkv_a, dgq, dgk, dsinks, got_mix_sums, ffn_other) = _attn_bwd(
        q_a, kv_a, d_attn, gq_col, gk, gk_col, sinks,
        _merge_exchanges(_scatter_to_owners([m_sum]), _share_halves([ffn_half])))
    dgq = dgq.reshape(1, HEAD_DIM)
    mix_half = _sum_chips(m_sum, got_mix_sums, chip_arr, tile=256, name="sum_chips_mix")
    w_block, mix_other = _dw(dq_a, h1, buf=w_block, name="dw_in_q", exchange=_share_halves([mix_half]),
                             **w_in_rows(*P_QA))
    w_block, = _dw(dkv_a, h1, buf=w_block, name="dw_in_kv", **w_in_rows(*P_KVA))

    w_block = w_block.reshape(N_CHIPS, W_IN_SH, D_MODEL)
    sib_w_in, = _run_exchange(_pair_exchange([w_block]), "pair_exchange_w_in")
    w_sum = _pair_sum(w_block, sib_w_in, c_arr, tile=592, name="pair_sum_w_in")
    d_pieces = [dq_a, dkv_a, d_ret, d_gz]
    grad_x, dg1_p, got_w_in_sums = _proj_bwd(d_pieces, x_t, dx1, w_in_t, g1, _scatter_to_owners([w_sum]))
    w_in_half = _sum_chips(w_sum, got_w_in_sums, chip_arr, tile=592, name="sum_chips_w_in")
    w_in_other, = _run_exchange(_share_halves([w_in_half]), "share_halves_w_in")

    def update(name, g_half, g_other, tile, mats):
        outs = _adamw([tuple(tr(a) if t else a[0] for a in wmv) + (off,) for _, *wmv, off, t in mats],
                      g_half, g_other, c_arr, tile=tile, name=f"adamw_{name}")
        return {key: [jnp.transpose(o) if t else o for o in res] for (key, _, _, _, _, t), res in zip(mats, outs)}

    big = {
        **update("w_in", w_in_half, w_in_other, 592, [("w_in", w_in, m_w_in, v_w_in, 0, True)]),
        **update("ffn", ffn_half, ffn_other, 176, [
            ("wg", w_ffn_gate, m_w_ffn_gate, v_w_ffn_gate, 0, True),
            ("wu", w_ffn_up, m_w_ffn_up, v_w_ffn_up, FF_SH, True),
            ("wd", w_ffn_down, m_w_ffn_down, v_w_ffn_down, 2 * FF_SH, False)]),
        **update("mix", mix_half, mix_other, 128, [
            ("wba", w_branch_attn, m_w_branch_attn, v_w_branch_attn, 0, False),
            ("wbr", w_branch_ret, m_w_branch_ret, v_w_branch_ret, 256, False),
            ("wout", w_out, m_w_out, v_w_out, 768, False)])}

    loss_row, *small = _small_step(
        [loss_p.reshape(-1, 128), dg1_p.reshape(-1, D_MODEL), dg2_p.reshape(-1, D_MODEL), dgq, dgk, dsinks],
        [norm_mix_gain, norm_ffn_gain, q_norm_gain, k_norm_gain, attn_sinks,
         m_norm_mix_gain, m_norm_ffn_gain, m_q_norm_gain, m_k_norm_gain, m_attn_sinks,
         v_norm_mix_gain, v_norm_ffn_gain, v_q_norm_gain, v_k_norm_gain, v_attn_sinks])
    loss = loss_row[0, 0]

    def leaves(i):
        b = [big[n][i][None] for n in ("w_in", "wba", "wbr", "wout", "wg", "wu", "wd")]
        s1, s2, sq, sk, ss = small[5 * i:5 * i + 5]
        return [s1, b[0], sq, sk, ss, b[1], b[2], b[3], s2, b[4], b[5], b[6]]

    return (loss, grad_x[None], *leaves(0), *leaves(1), *leaves(2), *leaves(3))
```

```python
import jax
import jax.numpy as jnp
from jax import lax
from jax.experimental import pallas as pl
from jax.experimental.pallas import tpu as pltpu

F32 = jnp.float32
BF16 = jnp.bfloat16
MESH = pl.DeviceIdType.MESH

D_MODEL = 1024
EPS = 1e-6
HEAD_DIM = 64
N_Q_HEADS = 16
N_KV_HEADS = 2
GROUP = 8
BLOCK = 128
RET_HEADS = 4
RET_QK_DIM = 256
RET_V_DIM = 512
RET_CHUNK = 128
RET_ROT_BASE = 10000.0
D_FF = 2816
ATT_Q = N_Q_HEADS * HEAD_DIM
ATT_KV = N_KV_HEADS * HEAD_DIM
RET_QK = RET_HEADS * RET_QK_DIM
RET_V = RET_HEADS * RET_V_DIM
D_IN = 9472
ADAM_LR = 0.001
ADAM_B1 = 0.9
ADAM_B2 = 0.999
ADAM_EPS = 1e-08
ADAM_WD = 0.01
ADAM_STEP = 10

N_CHIPS = 4
N_DEV = 8
VMEM_LIMIT_BYTES = 60 * 1024 * 1024

P_QA = (0, 1024)
P_KVA = (1024, 256)
P_QR = (1280, 1024)
P_KR = (2304, 1024)
P_VR = (3328, 2048)
P_GR = (5376, 2048)
P_ZA = (7424, 1024)
P_ZR = (8448, 1024)

W_IN_SH = D_IN // N_CHIPS
FF_SH = D_FF // N_CHIPS

SMALL_ROWS = 8


def _dot(a, b):
    return jnp.dot(a, b, preferred_element_type=F32)


def _dot_nt(a, b):
    return lax.dot_general(a, b, (((1,), (1,)), ((), ())), preferred_element_type=F32)


def _dot_tn(a, b):
    return lax.dot_general(a, b, (((0,), (0,)), ((), ())), preferred_element_type=F32)


def _bf(x):
    return x.astype(BF16)


def _rms_stats(x):
    r = lax.rsqrt(jnp.mean(x * x, axis=-1, keepdims=True) + EPS)
    return r, x * r


def _rms_bwd(dy, xhat, r, gain):
    u = dy * gain
    dx = r * (u - xhat * jnp.mean(u * xhat, axis=-1, keepdims=True))
    return dx, dy * xhat


def _params(sem):
    return pltpu.CompilerParams(dimension_semantics=sem, vmem_limit_bytes=VMEM_LIMIT_BYTES)


_ANY = pl.BlockSpec(memory_space=pl.ANY)


class _Exchange:
    def __init__(self, ins, outs, n_sems, phases):
        self.ins, self.outs, self.n_sems, self.phases = list(ins), list(outs), n_sems, list(phases)


def _merge_exchanges(a, b):
    na_i, na_o, shift = len(a.ins), len(a.outs), a.n_sems

    def first(fn):
        return lambda i, o, s, r, base: fn(i[:na_i], o[:na_o], s, r, base)

    def second(fn):
        return lambda i, o, s, r, base: fn(i[na_i:], o[na_o:], s, r, base + shift)

    phases = [(f, first(fn)) for f, fn in a.phases] + [(f, second(fn)) for f, fn in b.phases]
    return _Exchange(a.ins + b.ins, a.outs + b.outs, a.n_sems + b.n_sems, sorted(phases, key=lambda p: p[0]))


def _pallas(kern, *, grid, in_specs, out_specs, out_shape, args, name, scratch=(), exchange=None, aliases=None):
    aliases = aliases or {}
    if exchange is None:
        return pl.pallas_call(
            kern, grid=grid, in_specs=in_specs, out_specs=out_specs, out_shape=out_shape, name=name,
            scratch_shapes=list(scratch), input_output_aliases=aliases,
            compiler_params=_params(("arbitrary",) * len(grid)))(*args)
    n_in, n_out, n_sc = len(in_specs), len(out_specs), len(scratch)
    n_xi, n_xo = len(exchange.ins), len(exchange.outs)
    n_steps = 1
    for g in grid:
        n_steps *= g

    def wrapped(*refs):
        ins, refs = refs[:n_in], refs[n_in:]
        x_ins, refs = refs[:n_xi], refs[n_xi:]
        outs, refs = refs[:n_out], refs[n_out:]
        x_outs, refs = refs[:n_xo], refs[n_xo:]
        scr, (send_sems, recv_sems) = refs[:n_sc], refs[n_sc:]
        step = pl.program_id(0)
        for d in range(1, len(grid)):
            step = step * grid[d] + pl.program_id(d)
        for frac, fn in exchange.phases:
            at = min(int(frac * n_steps), n_steps - 1)

            @pl.when(step == at)
            def _(fn=fn):
                fn(x_ins, x_outs, send_sems, recv_sems, 0)

        kern(*ins, *outs, *scr)

    sems = [pltpu.SemaphoreType.DMA((exchange.n_sems,)), pltpu.SemaphoreType.DMA((exchange.n_sems,))]
    return pl.pallas_call(
        wrapped, grid=grid, in_specs=list(in_specs) + [_ANY] * n_xi, out_specs=list(out_specs) + [_ANY] * n_xo,
        out_shape=list(out_shape) + exchange.outs, name=name, scratch_shapes=list(scratch) + sems,
        input_output_aliases=aliases, compiler_params=_params(("arbitrary",) * len(grid)))(*args, *exchange.ins)


def _run_exchange(exchange, name):
    def body(*refs):
        n_i, n_o = len(exchange.ins), len(exchange.outs)
        for _, fn in exchange.phases:
            fn(refs[:n_i], refs[n_i:n_i + n_o], refs[n_i + n_o], refs[n_i + n_o + 1], 0)

    sems = [pltpu.SemaphoreType.DMA((exchange.n_sems,)), pltpu.SemaphoreType.DMA((exchange.n_sems,))]
    return pl.pallas_call(body, in_specs=[_ANY] * len(exchange.ins), out_specs=[_ANY] * len(exchange.outs),
                          out_shape=exchange.outs, scratch_shapes=sems, name=name)(*exchange.ins)


def _fused(parts, *, grid, name, exchange=None):
    counts = [(len(p["in_specs"]), len(p["out_specs"]), len(p["scratch"])) for p in parts]
    n_in, n_out = sum(c[0] for c in counts), sum(c[1] for c in counts)

    def kern(*refs):
        ins, outs, scr = refs[:n_in], refs[n_in:n_in + n_out], refs[n_in + n_out:]
        i0 = o0 = s0 = 0
        for p, (ni, no, ns) in zip(parts, counts):
            p["kern"](*ins[i0:i0 + ni], *outs[o0:o0 + no], *scr[s0:s0 + ns])
            i0, o0, s0 = i0 + ni, o0 + no, s0 + ns

    cat = lambda key: [a for p in parts for a in p[key]]
    return _pallas(kern, grid=grid, in_specs=cat("in_specs"), out_specs=cat("out_specs"), out_shape=cat("out_shape"),
                   scratch=cat("scratch"), args=cat("args"), name=name, exchange=exchange)


def _row_call(body, *, tm, row_ins, res_ins, row_outs, part_outs=(), name, exchange=None):
    t = row_ins[0].shape[0]
    n_tiles = t // tm
    in_specs = [pl.BlockSpec((tm, a.shape[1]), lambda i: (i, 0)) for a in row_ins]
    in_specs += [pl.BlockSpec(a.shape, lambda i: (0, 0), pipeline_mode=pl.Buffered(1)) for a in res_ins]
    out_shape = [jax.ShapeDtypeStruct((t, w), dt) for (w, dt) in row_outs]
    out_shape += [jax.ShapeDtypeStruct((n_tiles, 1, w), F32) for w in part_outs]
    out_specs = [pl.BlockSpec((tm, w), lambda i: (i, 0)) for (w, _) in row_outs]
    out_specs += [pl.BlockSpec((1, 1, w), lambda i: (i, 0, 0)) for w in part_outs]
    n_ri, n_re, n_ro = len(row_ins), len(res_ins), len(row_outs)

    def kern(*refs):
        body(refs[:n_ri], refs[n_ri:n_ri + n_re], refs[n_ri + n_re:n_ri + n_re + n_ro], refs[n_ri + n_re + n_ro:])

    return _pallas(kern, grid=(n_tiles,), in_specs=in_specs, out_specs=out_specs, out_shape=out_shape,
                   args=[*row_ins, *res_ins], name=name, exchange=exchange)


def _proj_fwd(x, g1, w_in_t, exchange):
    pieces = ((P_QA, F32), (P_KVA, F32), (P_QR, F32), (P_KR, F32), (P_VR, BF16), (P_GR, F32), (P_ZA, F32), (P_ZR, F32))

    def body(ri, re, ro, po):
        x_t = ri[0][...]
        r, xhat = _rms_stats(x_t)
        hb = _bf(xhat * re[0][...])
        ro[0][...] = hb
        for k, ((off, w), dt) in enumerate(pieces):
            ro[1 + k][...] = _dot_nt(hb, re[1][off:off + w, :]).astype(dt)

    outs = [(D_MODEL, BF16)] + [(w, dt) for ((_, w), dt) in pieces]
    return _row_call(body, tm=256, row_ins=[x], res_ins=[g1, w_in_t], row_outs=outs, name="proj_fwd",
                     exchange=exchange)


def _mix_fwd(attn, ret, z_a, z_r, x, wba, wbr, wout, g2):
    def body(ri, re, ro, po):
        ba = _dot(ri[0][...], re[0][...])
        br = _dot(ri[1][...], re[1][...])
        m = jax.nn.sigmoid(ri[2][...]) * ba + jax.nn.sigmoid(ri[3][...]) * br
        mb = _bf(m)
        x1 = ri[4][...] + _dot(mb, re[2][...])
        r, xhat = _rms_stats(x1)
        ro[0][...] = ba
        ro[1][...] = br
        ro[2][...] = mb
        ro[3][...] = x1
        ro[4][...] = _bf(xhat * re[3][...])

    outs = [(D_MODEL, F32), (D_MODEL, F32), (D_MODEL, BF16), (D_MODEL, F32), (D_MODEL, BF16)]
    return _row_call(body, tm=512, row_ins=[attn, ret, z_a, z_r, x], res_ins=[wba, wbr, wout, g2], row_outs=outs,
                     name="mix_fwd")


def _ffn_fwd_bwd(h2, x1, target, wg_t, wu_t, wd, g2):
    def body(ri, re, ro, po):
        h2_t = ri[0][...]
        x1_t = ri[1][...]
        gate = _dot_nt(h2_t, re[0][...])
        up = _dot_nt(h2_t, re[1][...])
        sg = jax.nn.sigmoid(gate)
        sl = gate * sg
        actb = _bf(sl * up)
        ro[0][...] = actb
        y = x1_t + _dot(actb, re[2][...])
        e = y - ri[2][...]
        po[0][0] = jnp.broadcast_to(0.5 * jnp.sum(jnp.sum(e * e, axis=1, keepdims=True), axis=0, keepdims=True)
                                    * (1.0 / D_MODEL), (1, 128))
        dy = e * (1.0 / D_MODEL)
        dyb = _bf(dy)
        ro[3][...] = dyb
        dact = _dot_nt(dyb, re[2][...])
        dupb = _bf(dact * sl)
        dgateb = _bf(dact * up * (sg * (1.0 + gate * (1.0 - sg))))
        ro[1][...] = dgateb
        ro[2][...] = dupb
        dh2 = _dot(dgateb, re[0][...]) + _dot(dupb, re[1][...])
        r, xhat = _rms_stats(x1_t)
        dxn, dgain = _rms_bwd(dh2, xhat, r, re[3][...])
        dx1 = dy + dxn
        ro[4][...] = dx1
        ro[5][...] = _bf(dx1)
        po[1][0] = jnp.sum(dgain, axis=0, keepdims=True)

    outs = [(D_FF, BF16), (D_FF, BF16), (D_FF, BF16), (D_MODEL, BF16), (D_MODEL, F32), (D_MODEL, BF16)]
    return _row_call(body, tm=256, row_ins=[h2, x1, target], res_ins=[wg_t, wu_t, wd, g2], row_outs=outs,
                     part_outs=(128, D_MODEL), name="ffn_fwd_bwd")


def _mix_bwd(dx1b, z_a, z_r, ba, br, g_r, o_ret, wout, wba, wbr, exchange):
    def body(ri, re, ro, po):
        dm = _dot_nt(ri[0][...], re[0][...])
        sa = jax.nn.sigmoid(ri[1][...])
        sr = jax.nn.sigmoid(ri[2][...])
        dbab = _bf(sa * dm)
        dbrb = _bf(sr * dm)
        ro[0][...] = dbab
        ro[1][...] = dbrb
        ro[4][:, RET_V:RET_V + D_MODEL] = _bf(dm * ri[3][...] * (sa * (1.0 - sa)))
        ro[4][:, RET_V + D_MODEL:RET_V + 2 * D_MODEL] = _bf(dm * ri[4][...] * (sr * (1.0 - sr)))
        ro[2][...] = _bf(_dot_nt(dbab, re[1][...]))
        dret = _dot_nt(dbrb, re[2][...])
        for h in range(RET_HEADS):
            cols = slice(h * RET_V_DIM, (h + 1) * RET_V_DIM)
            g = ri[5][:, cols]
            r, rn = _rms_stats(ri[6][:, cols])
            sg = jax.nn.sigmoid(g)
            dret_h = dret[:, cols]
            d_rn = dret_h * (g * sg)
            ro[4][:, cols] = _bf(dret_h * rn * (sg * (1.0 + g * (1.0 - sg))))
            ro[3][:, cols] = r * (d_rn - rn * jnp.mean(d_rn * rn, axis=-1, keepdims=True))

    outs = [(D_MODEL, BF16), (D_MODEL, BF16), (ATT_Q, BF16), (RET_V, F32), (RET_V + 2 * D_MODEL, BF16)]
    return _row_call(body, tm=256, row_ins=[dx1b, z_a, z_r, ba, br, g_r, o_ret], res_ins=[wout, wba, wbr],
                     row_outs=outs, name="mix_bwd", exchange=exchange)


def _proj_bwd(d_pieces, x, dx1, w_in_t, g1, exchange):
    widths = [p.shape[1] for p in d_pieces]
    groups = [(sum(widths[:k]), w) for k, w in enumerate(widths)]
    n_p = len(groups)

    def body(ri, re, ro, po):
        dh = None
        for k, (off, w) in enumerate(groups):
            term = _dot(ri[k][...], re[0][off:off + w, :])
            dh = term if dh is None else dh + term
        r, xhat = _rms_stats(ri[n_p][...])
        dxn, dgain = _rms_bwd(dh, xhat, r, re[1][...])
        ro[0][...] = ri[n_p + 1][...] + dxn
        po[0][0] = jnp.sum(dgain, axis=0, keepdims=True)

    return _row_call(body, tm=512, row_ins=[*d_pieces, x, dx1], res_ins=[w_in_t, g1], row_outs=[(D_MODEL, F32)],
                     part_outs=(D_MODEL,), name="proj_bwd", exchange=exchange)


def _dw(a, b, *, tm, place, buf, name, exchange=None):
    t, m = a.shape
    n = b.shape[1]
    tk = min(2048, t)
    n_i, n_k = m // tm, t // tk
    fresh = isinstance(buf, jax.ShapeDtypeStruct)
    n_copies = len(place(0))

    def kern(a_ref, b_ref, *rest):
        out_ref, acc, sems = rest[-3:]
        i, k = pl.program_id(0), pl.program_id(1)
        part = _dot_tn(a_ref[...], b_ref[...])

        @pl.when(k == 0)
        def _():
            acc[i] = part

        @pl.when(k > 0)
        def _():
            acc[i] += part

        def copies(tile):
            return [pltpu.make_async_copy(acc.at[tile, pl.ds(r0, rows), :], out_ref.at[idx], sems.at[tile * n_copies + c])
                    for c, (r0, rows, idx) in enumerate(place(tile))]

        for tile in range(n_i):
            @pl.when((i == tile) & (k == n_k - 1))
            def _(tile=tile):
                for cp in copies(tile):
                    cp.start()

        @pl.when((i == n_i - 1) & (k == n_k - 1))
        def _():
            for tile in range(n_i):
                for cp in copies(tile):
                    cp.wait()

    in_specs = [pl.BlockSpec((tk, tm), lambda i, k: (k, i)), pl.BlockSpec((tk, n), lambda i, k: (k, 0))]
    shape = buf if fresh else jax.ShapeDtypeStruct(buf.shape, buf.dtype)
    return _pallas(
        kern, grid=(n_i, n_k), in_specs=in_specs + ([] if fresh else [_ANY]), out_specs=[_ANY], out_shape=[shape],
        scratch=[pltpu.VMEM((n_i, tm, n), F32), pltpu.SemaphoreType.DMA((n_i * n_copies,))],
        args=[a, b] + ([] if fresh else [buf]), aliases=None if fresh else {2: 0}, name=name, exchange=exchange)


def _heads_to_lanes(x3):
    return jnp.concatenate([x3[g] for g in range(GROUP)], axis=1)


def _lanes_to_heads(xt):
    return jnp.concatenate([xt[:, g * BLOCK:(g + 1) * BLOCK] for g in range(GROUP)], axis=0)


def _attn_group(n, kvh, q_ref, kvp_ref, kvc_ref, gq_col, gk, sink_ref):
    heads = [kvh * GROUP + g for g in range(GROUP)]
    cols = slice(kvh * GROUP * HEAD_DIM, (kvh + 1) * GROUP * HEAD_DIM)
    q3 = q_ref[:, cols].T.reshape(GROUP, HEAD_DIM, BLOCK)
    rq = lax.rsqrt(jnp.mean(q3 * q3, axis=1, keepdims=True) + EPS)
    qhat = q3 * rq
    qts = _heads_to_lanes(_bf(qhat * (gq_col * (HEAD_DIM ** -0.5))))
    kcols = slice(kvh * HEAD_DIM, (kvh + 1) * HEAD_DIM)
    vcols = slice(ATT_KV + kvh * HEAD_DIM, ATT_KV + (kvh + 1) * HEAD_DIM)
    k = jnp.concatenate([kvp_ref[:, kcols], kvc_ref[:, kcols]], axis=0)
    rk, khat = _rms_stats(k)
    knb = _bf(khat * gk)
    st = _dot(knb, qts)
    j = lax.broadcasted_iota(jnp.int32, (BLOCK, GROUP * BLOCK), 0)
    i = lax.broadcasted_iota(jnp.int32, (BLOCK, GROUP * BLOCK), 1) & (BLOCK - 1)
    from_prev = j > i
    f = jnp.where(from_prev, jnp.where(n > 0, st[0:BLOCK], -1e30), st[BLOCK:2 * BLOCK])
    sink = jnp.concatenate([jnp.broadcast_to(sink_ref[0:1, h:h + 1], (1, BLOCK)) for h in heads], axis=1)
    m = jnp.maximum(jnp.max(f, axis=0, keepdims=True), sink)
    e = jnp.exp(f - m)
    es = jnp.exp(sink - m)
    inv = 1.0 / (jnp.sum(e, axis=0, keepdims=True) + es)
    return dict(heads=heads, qhat=qhat, rq=rq, qts=qts, khat=khat, rk=rk, knb=knb, from_prev=from_prev,
                pf=e * inv, psink=es * inv)


def _unfold(from_prev, xf):
    return _bf(jnp.concatenate([jnp.where(from_prev, xf, 0.0), jnp.where(from_prev, 0.0, xf)], axis=0))


def _attn_fwd(q_a, kv_a, gq_col, gk, sinks):
    t = q_a.shape[0]
    nb = t // BLOCK

    def kern(q_ref, kvp_ref, kvc_ref, gq_ref, gk_ref, sink_ref, o_ref):
        n = pl.program_id(0)
        kvt = jnp.concatenate([kvp_ref[...].T, kvc_ref[...].T], axis=1)
        for kvh in range(N_KV_HEADS):
            a = _attn_group(n, kvh, q_ref, kvp_ref, kvc_ref, gq_ref[...], gk_ref[...], sink_ref)
            vt = _bf(kvt[ATT_KV + kvh * HEAD_DIM:ATT_KV + (kvh + 1) * HEAD_DIM, :])
            out_t = _dot(vt, _unfold(a["from_prev"], a["pf"]))
            cols = slice(kvh * GROUP * HEAD_DIM, (kvh + 1) * GROUP * HEAD_DIM)
            o_ref[:, cols] = _bf(_lanes_to_heads(out_t).T)

    small = lambda a: pl.BlockSpec(a.shape, lambda n: (0, 0))
    return dict(
        kern=kern,
        in_specs=[pl.BlockSpec((BLOCK, ATT_Q), lambda n: (n, 0)),
                  pl.BlockSpec((BLOCK, 2 * ATT_KV), lambda n: (jnp.maximum(n - 1, 0), 0)),
                  pl.BlockSpec((BLOCK, 2 * ATT_KV), lambda n: (n, 0)),
                  small(gq_col), small(gk), small(sinks)],
        out_specs=[pl.BlockSpec((BLOCK, ATT_Q), lambda n: (n, 0))],
        out_shape=[jax.ShapeDtypeStruct((t, ATT_Q), BF16)], scratch=[],
        args=[q_a, kv_a, kv_a, gq_col, gk, sinks])


def _attn_bwd(q_a, kv_a, d_attn, gq_col, gk, gk_col, sinks, exchange):
    t = q_a.shape[0]
    nb = t // BLOCK

    def kern(q_ref, kvp_ref, kvc_ref, do_ref, gq_ref, gk_ref, gkc_ref, sink_ref,
             dq_ref, dkv_ref, dgq_ref, dgk_ref, dsink_ref, band_k, band_v, carry_k, carry_v):
        n = pl.program_id(0)
        gq_v = gq_ref[...]
        gk_v = gk_ref[...]

        @pl.when(n == 0)
        def _():
            carry_k[...] = jnp.zeros_like(carry_k)
            carry_v[...] = jnp.zeros_like(carry_v)
            dgq_ref[...] = jnp.zeros_like(dgq_ref)
            dgk_ref[...] = jnp.zeros_like(dgk_ref)
            dsink_ref[...] = jnp.zeros_like(dsink_ref)

        @pl.when(n == nb)
        def _():
            band_k[...] = jnp.zeros_like(band_k)
            band_v[...] = jnp.zeros_like(band_v)

        @pl.when(n < nb)
        def _():
            lane16 = lax.broadcasted_iota(jnp.int32, (1, N_Q_HEADS), 1)
            dsink = jnp.zeros((1, N_Q_HEADS), F32)
            dgq = jnp.zeros((HEAD_DIM, 1), F32)
            gk_col = gkc_ref[...]
            kvt = jnp.concatenate([kvp_ref[...].T, kvc_ref[...].T], axis=1)
            for kvh in range(N_KV_HEADS):
                a = _attn_group(n, kvh, q_ref, kvp_ref, kvc_ref, gq_v, gk_v, sink_ref)
                from_prev, pf, qhat = a["from_prev"], a["pf"], a["qhat"]
                cols = slice(kvh * GROUP * HEAD_DIM, (kvh + 1) * GROUP * HEAD_DIM)
                vcols = slice(ATT_KV + kvh * HEAD_DIM, ATT_KV + (kvh + 1) * HEAD_DIM)
                dot = _heads_to_lanes(_bf(do_ref[:, cols].astype(F32).T.reshape(GROUP, HEAD_DIM, BLOCK)))
                vb = _bf(jnp.concatenate([kvp_ref[:, vcols], kvc_ref[:, vcols]], axis=0))
                dpt = _dot(vb, dot)
                dpf = jnp.where(from_prev, dpt[0:BLOCK], dpt[BLOCK:2 * BLOCK])
                delta = jnp.sum(pf * dpf, axis=0, keepdims=True)
                dst = _unfold(from_prev, pf * (dpf - delta))
                dsk = a["psink"] * delta
                for g, h in enumerate(a["heads"]):
                    tot = jnp.sum(dsk[:, g * BLOCK:(g + 1) * BLOCK], axis=1, keepdims=True)
                    dsink = dsink - jnp.where(lane16 == h, tot, 0.0)
                kt = kvt[kvh * HEAD_DIM:(kvh + 1) * HEAD_DIM, :]
                knt = _bf(kt * lax.rsqrt(jnp.mean(kt * kt, axis=0, keepdims=True) + EPS) * gk_col)
                dqn = (_dot(knt, dst) * (HEAD_DIM ** -0.5))
                band_k[kvh] = _dot_nt(dst, a["qts"])
                band_v[kvh] = _dot_nt(_unfold(from_prev, pf), dot)
                dqn3 = _lanes_to_heads(dqn).reshape(GROUP, HEAD_DIM, BLOCK)
                u = dqn3 * gq_v
                dq3 = a["rq"] * (u - qhat * jnp.mean(u * qhat, axis=1, keepdims=True))
                dgq = dgq + jnp.sum(jnp.sum(dqn3 * qhat, axis=0), axis=1, keepdims=True)
                dq_ref[:, cols] = _bf(dq3.reshape(GROUP * HEAD_DIM, BLOCK).T)
            dsink_ref[...] += dsink
            dgq_ref[...] += dgq

        dgk = jnp.zeros((1, HEAD_DIM), F32)
        for kvh in range(N_KV_HEADS):
            kcols = slice(kvh * HEAD_DIM, (kvh + 1) * HEAD_DIM)
            vcols = slice(ATT_KV + kvh * HEAD_DIM, ATT_KV + (kvh + 1) * HEAD_DIM)
            dkn = carry_k[kvh] + band_k[kvh, 0:BLOCK, :]
            dv = carry_v[kvh] + band_v[kvh, 0:BLOCK, :]
            rk, khat = _rms_stats(kvp_ref[:, kcols])
            dk, dgain = _rms_bwd(dkn, khat, rk, gk_v)
            dgk = dgk + jnp.sum(dgain, axis=0, keepdims=True)
            dkv_ref[:, kcols] = _bf(dk)
            dkv_ref[:, vcols] = _bf(dv)
            carry_k[kvh] = band_k[kvh, BLOCK:2 * BLOCK, :]
            carry_v[kvh] = band_v[kvh, BLOCK:2 * BLOCK, :]
        dgk_ref[...] += dgk

    small = lambda a: pl.BlockSpec(a.shape, lambda n: (0, 0))
    last = nb - 1
    return _pallas(
        kern, grid=(nb + 1,),
        in_specs=[pl.BlockSpec((BLOCK, ATT_Q), lambda n: (jnp.minimum(n, last), 0)),
                  pl.BlockSpec((BLOCK, 2 * ATT_KV), lambda n: (jnp.maximum(n - 1, 0), 0)),
                  pl.BlockSpec((BLOCK, 2 * ATT_KV), lambda n: (jnp.minimum(n, last), 0)),
                  pl.BlockSpec((BLOCK, ATT_Q), lambda n: (jnp.minimum(n, last), 0)),
                  small(gq_col), small(gk), small(gk_col), small(sinks)],
        out_specs=[pl.BlockSpec((BLOCK, ATT_Q), lambda n: (jnp.minimum(n, last), 0)),
                   pl.BlockSpec((BLOCK, 2 * ATT_KV), lambda n: (jnp.maximum(n - 1, 0), 0)),
                   pl.BlockSpec((HEAD_DIM, 1), lambda n: (0, 0)),
                   pl.BlockSpec((1, HEAD_DIM), lambda n: (0, 0)),
                   pl.BlockSpec((1, N_Q_HEADS), lambda n: (0, 0))],
        out_shape=[jax.ShapeDtypeStruct((t, ATT_Q), BF16), jax.ShapeDtypeStruct((t, 2 * ATT_KV), BF16),
                   jax.ShapeDtypeStruct((HEAD_DIM, 1), F32), jax.ShapeDtypeStruct((1, HEAD_DIM), F32),
                   jax.ShapeDtypeStruct((1, N_Q_HEADS), F32)],
        scratch=[pltpu.VMEM((N_KV_HEADS, 2 * BLOCK, HEAD_DIM), F32),
                 pltpu.VMEM((N_KV_HEADS, 2 * BLOCK, HEAD_DIM), F32),
                 pltpu.VMEM((N_KV_HEADS, BLOCK, HEAD_DIM), F32),
                 pltpu.VMEM((N_KV_HEADS, BLOCK, HEAD_DIM), F32)],
        args=[q_a, kv_a, kv_a, d_attn, gq_col, gk, gk_col, sinks], name="attn_bwd", exchange=exchange)


def _ret_tables(t, exchange):
    theta = 1.0 / (RET_ROT_BASE ** jnp.linspace(0.0, 1.0, RET_QK_DIM // 2, dtype=F32))
    theta2 = jnp.concatenate([theta, theta])[None, :]
    sign = jnp.concatenate([-jnp.ones_like(theta), jnp.ones_like(theta)])[None, :]

    def kern(theta_ref, sign_ref, cos_ref, sin_ref):
        first = pl.program_id(0) * RET_CHUNK
        pos = (first + lax.broadcasted_iota(jnp.int32, (RET_CHUNK, RET_QK_DIM), 0)).astype(F32)
        ang = pos * theta_ref[...]
        cos_ref[...] = jnp.cos(ang)
        sin_ref[...] = jnp.sin(ang) * sign_ref[...]

    row = pl.BlockSpec((1, RET_QK_DIM), lambda n: (0, 0))
    blk = pl.BlockSpec((RET_CHUNK, RET_QK_DIM), lambda n: (n, 0))
    cos, sin_s, *got = _pallas(kern, grid=(t // RET_CHUNK,), in_specs=[row, row], out_specs=[blk, blk],
                               out_shape=[jax.ShapeDtypeStruct((t, RET_QK_DIM), F32)] * 2, args=[theta2, sign],
                               name="position_tables", exchange=exchange)
    log_gamma = jnp.log(1.0 - 2.0 ** (-5.0 - jnp.arange(RET_HEADS, dtype=F32)))
    i = jnp.arange(RET_CHUNK, dtype=F32)
    diff = i[:, None] - i[None, :]
    causal = diff >= 0
    decay = jnp.where(causal[None], jnp.exp(jnp.where(causal, diff, 0.0)[None] * log_gamma[:, None, None]), 0.0)
    xi = jnp.exp((i + 1.0)[None, :] * log_gamma[:, None])[:, :, None]
    zeta = jnp.exp((RET_CHUNK - 1.0 - i)[None, :] * log_gamma[:, None])[:, :, None]
    gch = jnp.broadcast_to(jnp.exp(RET_CHUNK * log_gamma)[:, None, None], (RET_HEADS, 1, 128))
    return (cos, sin_s, decay, xi, zeta, gch), got


def _swap_pairs(x):
    half = RET_QK_DIM // 2
    return jnp.concatenate([x[:, half:], x[:, :half]], axis=1)


def _pairs_apart(rows):
    return rows.reshape(-1, RET_QK_DIM // 2, 2, rows.shape[1]).transpose(0, 2, 1, 3).reshape(rows.shape)


def _pairs_together(rows):
    return rows.reshape(-1, 2, RET_QK_DIM // 2, rows.shape[1]).transpose(0, 2, 1, 3).reshape(rows.shape)


def _rotate(x, cos, sin_s):
    return x * cos + _swap_pairs(x) * sin_s


def _rotate_bwd(dy, cos, sin_s):
    return dy * cos + _swap_pairs(dy * sin_s)


def _ret_specs(order):
    qk = pl.BlockSpec((RET_CHUNK, RET_QK), lambda j: (order(j), 0))
    v = pl.BlockSpec((RET_CHUNK, RET_V), lambda j: (order(j), 0))
    dec = pl.BlockSpec((RET_HEADS, RET_CHUNK, RET_CHUNK), lambda j: (0, 0, 0))
    col = pl.BlockSpec((RET_HEADS, RET_CHUNK, 1), lambda j: (0, 0, 0))
    gch = pl.BlockSpec((RET_HEADS, 1, 128), lambda j: (0, 0, 0))
    st = pl.BlockSpec((RET_HEADS, None, RET_QK_DIM, RET_V_DIM), lambda j: (0, order(j), 0, 0))
    pos = pl.BlockSpec((RET_CHUNK, RET_QK_DIM), lambda j: (order(j), 0))
    return qk, v, dec, col, gch, st, pos


def _ret_fwd(q_r, k_r, v_r, g_r, tables):
    t = q_r.shape[0]
    nc = t // RET_CHUNK
    cos, sin_s, decay, xi, zeta, gch = tables

    def kern(q_ref, k_ref, v_ref, g_ref, cos_ref, sin_ref, dec_ref, xi_ref, zeta_ref, gch_ref,
             o_ref, ret_ref, st_ref, state):
        @pl.when(pl.program_id(0) == 0)
        def _():
            state[...] = jnp.zeros_like(state)

        cos_t = cos_ref[...]
        sin_t = sin_ref[...]
        for h in range(RET_HEADS):
            qc = slice(h * RET_QK_DIM, (h + 1) * RET_QK_DIM)
            vc = slice(h * RET_V_DIM, (h + 1) * RET_V_DIM)
            qs = _bf(_rotate(q_ref[:, qc], cos_t, sin_t))
            ks = _rotate(k_ref[:, qc] * (RET_QK_DIM ** -0.5), cos_t, sin_t)
            vb = v_ref[:, vc]
            s_old = state[h]
            sb = _bf(s_old)
            st_ref[h] = sb
            inner = _dot_nt(qs, _bf(ks)) * dec_ref[h]
            out = _dot(_bf(inner), vb) + _dot(qs, sb) * xi_ref[h]
            state[h] = gch_ref[h, :, 0:1] * s_old + _dot_tn(_bf(ks * zeta_ref[h]), vb)
            o_ref[:, vc] = out
            r, rn = _rms_stats(out)
            g = g_ref[:, vc]
            ret_ref[:, vc] = _bf(g * jax.nn.sigmoid(g) * rn)

    qk, v, dec, col, gsp, st, pos = _ret_specs(lambda j: j)
    return dict(
        kern=kern,
        in_specs=[qk, qk, v, v, pos, pos, dec, col, col, gsp],
        out_specs=[v, v, st],
        out_shape=[jax.ShapeDtypeStruct((t, RET_V), F32), jax.ShapeDtypeStruct((t, RET_V), BF16),
                   jax.ShapeDtypeStruct((RET_HEADS, nc, RET_QK_DIM, RET_V_DIM), BF16)],
        scratch=[pltpu.VMEM((RET_HEADS, RET_QK_DIM, RET_V_DIM), F32)],
        args=[q_r, k_r, v_r, g_r, cos, sin_s, decay, xi, zeta, gch])


def _ret_bwd(q_r, k_r, v_r, d_o, states, tables, exchange):
    t = q_r.shape[0]
    nc = t // RET_CHUNK
    cos, sin_s, decay, xi, zeta, gch = tables

    def kern(q_ref, k_ref, v_ref, do_ref, st_ref, cos_ref, sin_ref, dec_ref, xi_ref, zeta_ref, gch_ref,
             d_ref, dstate):
        dq_ref, dk_ref = d_ref.at[:, 0:RET_QK], d_ref.at[:, RET_QK:2 * RET_QK]
        dv_ref = d_ref.at[:, 2 * RET_QK:2 * RET_QK + RET_V]

        @pl.when(pl.program_id(0) == 0)
        def _():
            dstate[...] = jnp.zeros_like(dstate)

        cos_t = cos_ref[...]
        sin_t = sin_ref[...]
        scale = RET_QK_DIM ** -0.5
        for h in range(RET_HEADS):
            qc = slice(h * RET_QK_DIM, (h + 1) * RET_QK_DIM)
            vc = slice(h * RET_V_DIM, (h + 1) * RET_V_DIM)
            qs = _bf(_rotate(q_ref[:, qc], cos_t, sin_t))
            ks = _rotate(k_ref[:, qc] * scale, cos_t, sin_t)
            ksb = _bf(ks)
            vb = v_ref[:, vc]
            d_o_t = do_ref[:, vc]
            dob = _bf(d_o_t)
            doxb = _bf(d_o_t * xi_ref[h])
            dec = dec_ref[h]
            ds_old = dstate[h]
            dsb = _bf(ds_old)
            pb = _bf(_dot_nt(qs, ksb) * dec)
            dpb = _bf(_dot_nt(dob, vb) * dec)
            dqs = _dot(dpb, ksb) + _dot_nt(doxb, st_ref[h])
            dks = _dot_tn(dpb, qs) + _dot_nt(vb, dsb) * zeta_ref[h]
            dv_ref[:, vc] = _bf(_dot_tn(pb, dob) + _dot(_bf(ks * zeta_ref[h]), dsb))
            dstate[h] = gch_ref[h, :, 0:1] * ds_old + _dot_tn(qs, doxb)
            dq_ref[:, qc] = _bf(_rotate_bwd(dqs, cos_t, sin_t))
            dk_ref[:, qc] = _bf(_rotate_bwd(dks, cos_t, sin_t) * scale)

    qk, v, dec, col, gsp, st, pos = _ret_specs(lambda j: nc - 1 - j)
    return _pallas(
        kern, grid=(nc,),
        in_specs=[qk, qk, v, v, st, pos, pos, dec, col, col, gsp],
        out_specs=[pl.BlockSpec((RET_CHUNK, 2 * RET_QK + RET_V), lambda j: (nc - 1 - j, 0))],
        out_shape=[jax.ShapeDtypeStruct((t, 2 * RET_QK + RET_V), BF16)],
        scratch=[pltpu.VMEM((RET_HEADS, RET_QK_DIM, RET_V_DIM), F32)],
        args=[q_r, k_r, v_r, d_o, states, cos, sin_s, decay, xi, zeta, gch], name="ret_bwd", exchange=exchange)


def _position():
    return lax.axis_index("x"), lax.axis_index("y"), lax.axis_index("c")


def _gather_exchange(owns, forward_at):
    n = len(owns)

    def copies(ins, outs, send_sems, recv_sems, base):
        x, y, c = _position()
        sibling = (x, y, 1 - c)
        chips = [(1 - x, y), (x, 1 - y), (1 - x, 1 - y)]
        my_chip = 2 * x + y

        def slab(a, chip, hf):
            half = owns[a].shape[0] // 2
            return outs[a].at[chip, pl.ds(hf * half, half), :]

        def copy(k, src, dst, to):
            return pltpu.make_async_remote_copy(src_ref=src, dst_ref=dst, send_sem=send_sems.at[base + k],
                                                recv_sem=recv_sems.at[base + k], device_id=to, device_id_type=MESH)

        first, passed, from_sibling = [], [], []
        for a in range(n):
            half = owns[a].shape[0] // 2
            for k, (cx, cy) in enumerate(chips):
                first.append(copy(6 * a + k, ins[a].at[pl.ds(c * half, half), :], slab(a, my_chip, c), (cx, cy, c)))
                landed = slab(a, 2 * cx + cy, c)
                passed.append(copy(6 * a + 3 + k, landed, landed, sibling))
                theirs = slab(a, 2 * cx + cy, 1 - c)
                from_sibling.append(copy(6 * a + 3 + k, theirs, theirs, sibling))
        return first, passed, from_sibling

    def start(*args):
        first, _, _ = copies(*args)
        for cp in first:
            cp.start()

    def forward(*args):
        first, passed, _ = copies(*args)
        for arrived, cp in zip(first, passed):
            arrived.wait_recv()
            cp.start()

    def finish(*args):
        first, passed, from_sibling = copies(*args)
        for cp in from_sibling:
            cp.wait_recv()
        for cp in first + passed:
            cp.wait_send()

    outs = [jax.ShapeDtypeStruct((N_CHIPS, *a.shape), a.dtype) for a in owns]
    return _Exchange(owns, outs, 6 * n, [(0.0, start), (forward_at, forward), (1.0, finish)])


def _symmetric_exchange(ins, outs, plan):
    n_sems = len(plan([None] * len(ins), [None] * len(outs), 0, 0, 0, dry=True))

    def copies(in_refs, out_refs, send_sems, recv_sems, base):
        x, y, c = _position()
        return [pltpu.make_async_remote_copy(src_ref=src, dst_ref=dst, send_sem=send_sems.at[base + k],
                                             recv_sem=recv_sems.at[base + k], device_id=dev, device_id_type=MESH)
                for k, (src, dst, dev) in enumerate(plan(in_refs, out_refs, x, y, c, dry=False))]

    def start(*args):
        for cp in copies(*args):
            cp.start()

    def finish(*args):
        for cp in copies(*args):
            cp.wait()

    return _Exchange(ins, outs, n_sems, [(0.0, start), (1.0, finish)])


def _pair_exchange(gs):
    def plan(in_refs, out_refs, x, y, c, dry):
        out = []
        for a, g in enumerate(gs):
            half = g.shape[1] // 2
            for k in range(N_CHIPS):
                out.append(None if dry else (in_refs[a].at[k, pl.ds((1 - c) * half, half), :], out_refs[a].at[k],
                                             (x, y, 1 - c)))
        return out

    outs = [jax.ShapeDtypeStruct((g.shape[0], g.shape[1] // 2, g.shape[2]), g.dtype) for g in gs]
    return _symmetric_exchange(gs, outs, plan)


def _pair_sum(g, from_sibling, c_arr, *, tile, name):
    n, rows, width = g.shape
    tiles = (rows // 2) // tile

    def kern(c_ref, g_ref, s_ref, o_ref):
        o_ref[...] = _bf(g_ref[...] + s_ref[...])

    return pl.pallas_call(
        kern,
        grid_spec=pltpu.PrefetchScalarGridSpec(
            num_scalar_prefetch=1, grid=(n, tiles),
            in_specs=[pl.BlockSpec((None, tile, width), lambda k, i, c: (k, c[0] * tiles + i, 0)),
                      pl.BlockSpec((None, tile, width), lambda k, i, c: (k, i, 0))],
            out_specs=pl.BlockSpec((None, tile, width), lambda k, i, c: (k, i, 0))),
        out_shape=jax.ShapeDtypeStruct((n, rows // 2, width), BF16), name=name,
        compiler_params=_params(("parallel", "parallel")),
    )(c_arr, g, from_sibling)


def _scatter_to_owners(hsums):
    def plan(in_refs, out_refs, x, y, c, dry):
        out = []
        for a in range(len(hsums)):
            for k, (cx, cy) in enumerate([(1 - x, y), (x, 1 - y), (1 - x, 1 - y)]):
                out.append(None if dry else (in_refs[a].at[2 * cx + cy], out_refs[a].at[k], (cx, cy, c)))
        return out

    outs = [jax.ShapeDtypeStruct((3, *h.shape[1:]), h.dtype) for h in hsums]
    return _symmetric_exchange(hsums, outs, plan)


def _sum_chips(hsum, parts, chip_arr, *, tile, name):
    n, half, width = parts.shape

    def kern(chip_ref, h_ref, p_ref, o_ref):
        acc = h_ref[...].astype(F32)
        for k in range(n):
            acc = acc + p_ref[k].astype(F32)
        o_ref[...] = acc

    return pl.pallas_call(
        kern,
        grid_spec=pltpu.PrefetchScalarGridSpec(
            num_scalar_prefetch=1, grid=(half // tile,),
            in_specs=[pl.BlockSpec((None, tile, width), lambda i, chip: (chip[0], i, 0)),
                      pl.BlockSpec((n, tile, width), lambda i, chip: (0, i, 0))],
            out_specs=pl.BlockSpec((tile, width), lambda i, chip: (i, 0))),
        out_shape=jax.ShapeDtypeStruct((half, width), F32), name=name,
        compiler_params=_params(("parallel",)),
    )(chip_arr, hsum, parts)


def _share_halves(fhalves):
    def plan(in_refs, out_refs, x, y, c, dry):
        return [None if dry else (in_refs[a], out_refs[a], (x, y, 1 - c)) for a in range(len(fhalves))]

    return _symmetric_exchange(fhalves, [jax.ShapeDtypeStruct(f.shape, f.dtype) for f in fhalves], plan)


def _adamw_math(w, g, m, v):
    m = ADAM_B1 * m + (1.0 - ADAM_B1) * g
    v = ADAM_B2 * v + (1.0 - ADAM_B2) * (g * g)
    m_hat = m / (1.0 - ADAM_B1 ** ADAM_STEP)
    v_hat = v / (1.0 - ADAM_B2 ** ADAM_STEP)
    delta = -ADAM_LR * (m_hat / (jnp.sqrt(v_hat) + ADAM_EPS) + ADAM_WD * w)
    return delta, m, v


def _adamw(mats, g_mine, g_other, c_arr, *, tile, name):
    width = g_mine.shape[1]
    tiles_per_half = g_mine.shape[0] // tile
    n_tiles = [w.shape[0] // tile for w, _, _, _ in mats]
    n_mats = len(mats)

    def kern(c_ref, *refs):
        ins, outs = refs[:5 * n_mats], refs[5 * n_mats:]
        for j, (_, _, _, row_off) in enumerate(mats):
            w_ref, gm_ref, go_ref, m_ref, v_ref = ins[5 * j:5 * j + 5]
            i = jnp.minimum(pl.program_id(0), n_tiles[j] - 1)
            in_my_half = ((row_off // tile + i) // tiles_per_half) == c_ref[0]
            g = jnp.where(in_my_half, gm_ref[...], go_ref[...])
            d, nm, nv = _adamw_math(w_ref[...], g, m_ref[...], v_ref[...])
            for out_ref, val in zip(outs[4 * j:4 * j + 4], (g, d, nm, nv)):
                out_ref[...] = val

    in_specs, out_specs, out_shape, args = [], [], [], []
    for (w, m, v, row_off), nt in zip(mats, n_tiles):
        full = pl.BlockSpec((tile, width), lambda i, c, nt=nt: (jnp.minimum(i, nt - 1), 0))
        half = pl.BlockSpec((tile, width), lambda i, c, nt=nt, first=row_off // tile:
                            ((first + jnp.minimum(i, nt - 1)) % tiles_per_half, 0))
        in_specs += [full, half, half, full, full]
        out_specs += [full] * 4
        out_shape += [jax.ShapeDtypeStruct(w.shape, F32)] * 4
        args += [w, g_mine, g_other, m, v]
    outs = pl.pallas_call(
        kern,
        grid_spec=pltpu.PrefetchScalarGridSpec(num_scalar_prefetch=1, grid=(max(n_tiles),), in_specs=in_specs,
                                               out_specs=out_specs),
        out_shape=out_shape, name=name, compiler_params=_params(("arbitrary",)),
    )(c_arr, *args)
    return [outs[4 * j:4 * j + 4] for j in range(n_mats)]


def _small_step(partials, params):
    slots = ((0, 0, D_MODEL), (1, 0, D_MODEL), (2, 0, HEAD_DIM), (2, 128, HEAD_DIM), (2, 256, N_Q_HEADS))
    loss_slot = (2, 384, 128)

    def body(*refs):
        loss_ref, dg1_ref, dg2_ref, dgq_ref, dgk_ref, dsink_ref = refs[:6]
        p_refs, out_refs = refs[6:21], refs[21:42]
        mine, gathered, send_sems, recv_sems = refs[42:]
        x, y, c = _position()
        me = 4 * x + 2 * y + c
        mine[...] = jnp.zeros_like(mine)
        for (row, lane, n), val in zip(slots + (loss_slot,), (
                jnp.sum(dg1_ref[...], axis=0, keepdims=True), jnp.sum(dg2_ref[...], axis=0, keepdims=True),
                dgq_ref[...], dgk_ref[...], dsink_ref[...], jnp.sum(loss_ref[...], axis=0, keepdims=True))):
            mine[row:row + 1, lane:lane + n] = val
        copies = []
        for k in range(1, N_DEV):
            flip = (k >> 2) & 1, (k >> 1) & 1, k & 1
            to = (x ^ flip[0], y ^ flip[1], c ^ flip[2])
            cp = pltpu.make_async_remote_copy(
                src_ref=mine, dst_ref=gathered.at[me], send_sem=send_sems.at[k - 1], recv_sem=recv_sems.at[k - 1],
                device_id=to, device_id_type=MESH)
            cp.start()
            copies.append(cp)
        gathered[me] = mine[...]
        for k in range(1, N_DEV):
            flip = (k >> 2) & 1, (k >> 1) & 1, k & 1
            src = 4 * (x ^ flip[0]) + 2 * (y ^ flip[1]) + (c ^ flip[2])
            pltpu.make_async_remote_copy(
                src_ref=mine, dst_ref=gathered.at[src], send_sem=send_sems.at[k - 1], recv_sem=recv_sems.at[k - 1],
                device_id=(x, y, c), device_id_type=MESH).wait_recv()
        for cp in copies:
            cp.wait_send()
        total = gathered[0]
        for k in range(1, N_DEV):
            total = total + gathered[k]
        row, lane, n = loss_slot
        out_refs[0][...] = total[row:row + 1, lane:lane + n]
        for i, (row, lane, n) in enumerate(slots):
            g = total[row:row + 1, lane:lane + n]
            d, nm, nv = _adamw_math(p_refs[i][...], g, p_refs[5 + i][...], p_refs[10 + i][...])
            for kind, val in enumerate((g, d, nm, nv)):
                out_refs[1 + 5 * kind + i][...] = val

    vm = pl.BlockSpec(memory_space=pltpu.VMEM)
    shapes = [jax.ShapeDtypeStruct((1, 128), F32)] + [jax.ShapeDtypeStruct((1, n), F32) for _, _, n in slots] * 4
    return pl.pallas_call(
        body, in_specs=[vm] * 21, out_specs=[vm] * 21, out_shape=shapes,
        scratch_shapes=[pltpu.VMEM((SMALL_ROWS, D_MODEL), F32), pltpu.VMEM((N_DEV, SMALL_ROWS, D_MODEL), F32),
                        pltpu.SemaphoreType.DMA((N_DEV - 1,)), pltpu.SemaphoreType.DMA((N_DEV - 1,))],
        name="small_step",
    )(*partials, *params)


def _with_own(gathered, own, my_chip):
    return lax.dynamic_update_slice(gathered, own[None], (my_chip, 0, 0))


def kernel(x, norm_mix_gain, w_in, q_norm_gain, k_norm_gain, attn_sinks, w_branch_attn, w_branch_ret, w_out, norm_ffn_gain, w_ffn_gate, w_ffn_up, w_ffn_down, loss_target, m_norm_mix_gain, m_w_in, m_q_norm_gain, m_k_norm_gain, m_attn_sinks, m_w_branch_attn, m_w_branch_ret, m_w_out, m_norm_ffn_gain, m_w_ffn_gate, m_w_ffn_up, m_w_ffn_down, v_norm_mix_gain, v_w_in, v_q_norm_gain, v_k_norm_gain, v_attn_sinks, v_w_branch_attn, v_w_branch_ret, v_w_out, v_norm_ffn_gain, v_w_ffn_gate, v_w_ffn_up, v_w_ffn_down):
    my_chip = 2 * lax.axis_index("x") + lax.axis_index("y")
    c_arr = lax.axis_index("c").astype(jnp.int32).reshape(1)
    chip_arr = my_chip.astype(jnp.int32).reshape(1)
    x_t, target = x[0], loss_target[0]
    g1, g2, gq, gk, sinks = norm_mix_gain, norm_ffn_gain, q_norm_gain, k_norm_gain, attn_sinks

    tr = lambda a: jnp.transpose(a[0])
    own_w_in = _bf(tr(w_in))
    own_rest = [_bf(a) for a in (tr(w_ffn_gate), tr(w_ffn_up), w_ffn_down[0], w_branch_attn[0], w_branch_ret[0],
                                 w_out[0])]
    tables, (got_w_in,) = _ret_tables(x_t.shape[0], _gather_exchange([own_w_in], 0.9))
    w_in_t = _with_own(got_w_in, own_w_in, my_chip).reshape(D_IN, D_MODEL)
    rot0, rot1 = P_QR[0], P_KR[0] + P_KR[1]
    w_in_t = lax.dynamic_update_slice(w_in_t, _pairs_apart(w_in_t[rot0:rot1]), (rot0, 0))
    h1, q_a, kv_a, q_r, k_r, v_r, g_r, z_a, z_r, *got_rest = _proj_fwd(x_t, g1, w_in_t, _gather_exchange(own_rest, 0.8))
    wg_t, wu_t, wd, wba, wbr, wout = [_with_own(got, own, my_chip).reshape(N_CHIPS * own.shape[0], D_MODEL)
                                      for got, own in zip(got_rest, own_rest)]

    gq_col, gk_col = gq.reshape(HEAD_DIM, 1), gk.reshape(HEAD_DIM, 1)
    attn, o_ret, ret, states = _fused([_attn_fwd(q_a, kv_a, gq_col, gk, sinks), _ret_fwd(q_r, k_r, v_r, g_r, tables)],
                                      grid=(x_t.shape[0] // BLOCK,), name="mixers_fwd")
    ba, br, merged, x1, h2 = _mix_fwd(attn, ret, z_a, z_r, x_t, wba, wbr, wout, g2)
    act, dgate, dup, dyb, dx1, dx1b, loss_p, dg2_p = _ffn_fwd_bwd(h2, x1, target, wg_t, wu_t, wd, g2)

    def pairs(row0, rows):
        return lambda i: [(h * rows, rows, (2 * i + h, pl.ds(row0, rows), slice(None))) for h in range(2)]

    f_block = jax.ShapeDtypeStruct((N_CHIPS, 3 * FF_SH, D_MODEL), F32)
    f_block, = _dw(dgate, h2, tm=2 * FF_SH, place=pairs(0, FF_SH), buf=f_block, name="dw_gate")
    f_block, = _dw(dup, h2, tm=2 * FF_SH, place=pairs(FF_SH, FF_SH), buf=f_block, name="dw_up")
    f_block, = _dw(act, dyb, tm=2 * FF_SH, place=pairs(2 * FF_SH, FF_SH), buf=f_block, name="dw_down")
    (dba, dbr, d_attn, d_o, d_gz, sib_ffn) = _mix_bwd(
        dx1b, z_a, z_r, ba, br, g_r, o_ret, wout, wba, wbr, _pair_exchange([f_block]))
    f_sum = _pair_sum(f_block, sib_ffn, c_arr, tile=528, name="pair_sum_ffn")

    def quarters(row0, rows):
        return lambda i: [(k * rows, rows, (k, pl.ds(row0, rows), slice(None))) for k in range(N_CHIPS)]

    m_block = jax.ShapeDtypeStruct((N_CHIPS, D_MODEL, D_MODEL), F32)
    m_block, = _dw(attn, dba, tm=ATT_Q, place=quarters(0, 256), buf=m_block, name="dw_ba")
    m_block, = _dw(ret, dbr, tm=D_MODEL, place=pairs(256, 512), buf=m_block, name="dw_br")
    m_block, = _dw(merged, dx1b, tm=D_MODEL, place=quarters(768, 256), buf=m_block, name="dw_out")

    def w_in_rows(off, w):
        tm = min(w, D_MODEL)
        return dict(tm=tm, place=lambda i: [(0, tm, (pl.ds(off + i * tm, tm), slice(None)))])

    w_block = jax.ShapeDtypeStruct((D_IN, D_MODEL), F32)
    w_block, sib_mix = _dw(d_gz, h1, buf=w_block, name="dw_in_gz", exchange=_pair_exchange([m_block]),
                           **w_in_rows(P_GR[0], d_gz.shape[1]))
    m_sum = _pair_sum(m_block, sib_mix, c_arr, tile=256, name="pair_sum_mix")

    d_ret, got_ffn_sums = _ret_bwd(q_r, k_r, v_r, d_o, states, tables, _scatter_to_owners([f_sum]))
    ffn_half = _sum_chips(f_sum, got_ffn_sums, chip_arr, tile=528, name="sum_chips_ffn")
    w_block, = _dw(d_ret, h1, buf=w_block, name="dw_in_ret", **w_in_rows(P_QR[0], d_ret.shape[1]))

    (dq_a, dkv_a, dgq, dgk, dsinks, got_mix_sums, ffn_other) = _attn_bwd(
        q_a, kv_a, d_attn, gq_col, gk, gk_col, sinks,
        _merge_exchanges(_scatter_to_owners([m_sum]), _share_halves([ffn_half])))
    dgq = dgq.reshape(1, HEAD_DIM)
    mix_half = _sum_chips(m_sum, got_mix_sums, chip_arr, tile=256, name="sum_chips_mix")
    w_block, mix_other = _dw(dq_a, h1, buf=w_block, name="dw_in_q", exchange=_share_halves([mix_half]),
                             **w_in_rows(*P_QA))
    w_block, = _dw(dkv_a, h1, buf=w_block, name="dw_in_kv", **w_in_rows(*P_KVA))

    w_block = lax.dynamic_update_slice(w_block, _pairs_together(w_block[rot0:rot1]), (rot0, 0))
    w_block = w_block.reshape(N_CHIPS, W_IN_SH, D_MODEL)
    sib_w_in, = _run_exchange(_pair_exchange([w_block]), "pair_exchange_w_in")
    w_sum = _pair_sum(w_block, sib_w_in, c_arr, tile=592, name="pair_sum_w_in")
    d_pieces = [dq_a, dkv_a, d_ret, d_gz]
    grad_x, dg1_p, got_w_in_sums = _proj_bwd(d_pieces, x_t, dx1, w_in_t, g1, _scatter_to_owners([w_sum]))
    w_in_half = _sum_chips(w_sum, got_w_in_sums, chip_arr, tile=592, name="sum_chips_w_in")
    w_in_other, = _run_exchange(_share_halves([w_in_half]), "share_halves_w_in")

    def update(name, g_half, g_other, tile, mats):
        outs = _adamw([tuple(tr(a) if t else a[0] for a in wmv) + (off,) for _, *wmv, off, t in mats],
                      g_half, g_other, c_arr, tile=tile, name=f"adamw_{name}")
        return {key: [jnp.transpose(o) if t else o for o in res] for (key, _, _, _, _, t), res in zip(mats, outs)}

    big = {
        **update("w_in", w_in_half, w_in_other, 592, [("w_in", w_in, m_w_in, v_w_in, 0, True)]),
        **update("ffn", ffn_half, ffn_other, 176, [
            ("wg", w_ffn_gate, m_w_ffn_gate, v_w_ffn_gate, 0, True),
            ("wu", w_ffn_up, m_w_ffn_up, v_w_ffn_up, FF_SH, True),
            ("wd", w_ffn_down, m_w_ffn_down, v_w_ffn_down, 2 * FF_SH, False)]),
        **update("mix", mix_half, mix_other, 128, [
            ("wba", w_branch_attn, m_w_branch_attn, v_w_branch_attn, 0, False),
            ("wbr", w_branch_ret, m_w_branch_ret, v_w_branch_ret, 256, False),
            ("wout", w_out, m_w_out, v_w_out, 768, False)])}

    loss_row, *small = _small_step(
        [loss_p.reshape(-1, 128), dg1_p.reshape(-1, D_MODEL), dg2_p.reshape(-1, D_MODEL), dgq, dgk, dsinks],
        [norm_mix_gain, norm_ffn_gain, q_norm_gain, k_norm_gain, attn_sinks,
         m_norm_mix_gain, m_norm_ffn_gain, m_q_norm_gain, m_k_norm_gain, m_attn_sinks,
         v_norm_mix_gain, v_norm_ffn_gain, v_q_norm_gain, v_k_norm_gain, v_attn_sinks])
    loss = loss_row[0, 0]

    def leaves(i):
        b = [big[n][i][None] for n in ("w_in", "wba", "wbr", "wout", "wg", "wu", "wd")]
        s1, s2, sq, sk, ss = small[5 * i:5 * i + 5]
        return [s1, b[0], sq, sk, ss, b[1], b[2], b[3], s2, b[4], b[5], b[6]]

    return (loss, grad_x[None], *leaves(0), *leaves(1), *leaves(2), *leaves(3))
```

```python
import jax
import jax.numpy as jnp
from jax import lax
from jax.experimental import pallas as pl
from jax.experimental.pallas import tpu as pltpu

F32 = jnp.float32
BF16 = jnp.bfloat16
MESH = pl.DeviceIdType.MESH

D_MODEL = 1024
EPS = 1e-6
HEAD_DIM = 64
N_Q_HEADS = 16
N_KV_HEADS = 2
GROUP = 8
BLOCK = 128
RET_HEADS = 4
RET_QK_DIM = 256
RET_V_DIM = 512
RET_CHUNK = 128
RET_ROT_BASE = 10000.0
D_FF = 2816
ATT_Q = N_Q_HEADS * HEAD_DIM
ATT_KV = N_KV_HEADS * HEAD_DIM
RET_QK = RET_HEADS * RET_QK_DIM
RET_V = RET_HEADS * RET_V_DIM
D_IN = 9472
ADAM_LR = 0.001
ADAM_B1 = 0.9
ADAM_B2 = 0.999
ADAM_EPS = 1e-08
ADAM_WD = 0.01
ADAM_STEP = 10

N_CHIPS = 4
N_DEV = 8
VMEM_LIMIT_BYTES = 60 * 1024 * 1024

P_QA = (0, 1024)
P_KVA = (1024, 256)
P_QR = (1280, 1024)
P_KR = (2304, 1024)
P_VR = (3328, 2048)
P_GR = (5376, 2048)
P_ZA = (7424, 1024)
P_ZR = (8448, 1024)

W_IN_SH = D_IN // N_CHIPS
FF_SH = D_FF // N_CHIPS

SMALL_ROWS = 8


def _dot(a, b):
    return jnp.dot(a, b, preferred_element_type=F32)


def _dot_nt(a, b):
    return lax.dot_general(a, b, (((1,), (1,)), ((), ())), preferred_element_type=F32)


def _dot_tn(a, b):
    return lax.dot_general(a, b, (((0,), (0,)), ((), ())), preferred_element_type=F32)


def _bf(x):
    return x.astype(BF16)


def _rms_stats(x):
    r = lax.rsqrt(jnp.mean(x * x, axis=-1, keepdims=True) + EPS)
    return r, x * r


def _rms_bwd(dy, xhat, r, gain):
    u = dy * gain
    dx = r * (u - xhat * jnp.mean(u * xhat, axis=-1, keepdims=True))
    return dx, dy * xhat


def _params(sem):
    return pltpu.CompilerParams(dimension_semantics=sem, vmem_limit_bytes=VMEM_LIMIT_BYTES)


_ANY = pl.BlockSpec(memory_space=pl.ANY)


class _Exchange:
    def __init__(self, ins, outs, n_sems, phases):
        self.ins, self.outs, self.n_sems, self.phases = list(ins), list(outs), n_sems, list(phases)


def _merge_exchanges(a, b):
    na_i, na_o, shift = len(a.ins), len(a.outs), a.n_sems

    def first(fn):
        return lambda i, o, s, r, base: fn(i[:na_i], o[:na_o], s, r, base)

    def second(fn):
        return lambda i, o, s, r, base: fn(i[na_i:], o[na_o:], s, r, base + shift)

    phases = [(f, first(fn)) for f, fn in a.phases] + [(f, second(fn)) for f, fn in b.phases]
    return _Exchange(a.ins + b.ins, a.outs + b.outs, a.n_sems + b.n_sems, sorted(phases, key=lambda p: p[0]))


def _pallas(kern, *, grid, in_specs, out_specs, out_shape, args, name, scratch=(), exchange=None, aliases=None):
    aliases = aliases or {}
    if exchange is None:
        return pl.pallas_call(
            kern, grid=grid, in_specs=in_specs, out_specs=out_specs, out_shape=out_shape, name=name,
            scratch_shapes=list(scratch), input_output_aliases=aliases,
            compiler_params=_params(("arbitrary",) * len(grid)))(*args)
    n_in, n_out, n_sc = len(in_specs), len(out_specs), len(scratch)
    n_xi, n_xo = len(exchange.ins), len(exchange.outs)
    n_steps = 1
    for g in grid:
        n_steps *= g

    def wrapped(*refs):
        ins, refs = refs[:n_in], refs[n_in:]
        x_ins, refs = refs[:n_xi], refs[n_xi:]
        outs, refs = refs[:n_out], refs[n_out:]
        x_outs, refs = refs[:n_xo], refs[n_xo:]
        scr, (send_sems, recv_sems) = refs[:n_sc], refs[n_sc:]
        step = pl.program_id(0)
        for d in range(1, len(grid)):
            step = step * grid[d] + pl.program_id(d)
        for frac, fn in exchange.phases:
            at = min(int(frac * n_steps), n_steps - 1)

            @pl.when(step == at)
            def _(fn=fn):
                fn(x_ins, x_outs, send_sems, recv_sems, 0)

        kern(*ins, *outs, *scr)

    sems = [pltpu.SemaphoreType.DMA((exchange.n_sems,)), pltpu.SemaphoreType.DMA((exchange.n_sems,))]
    return pl.pallas_call(
        wrapped, grid=grid, in_specs=list(in_specs) + [_ANY] * n_xi, out_specs=list(out_specs) + [_ANY] * n_xo,
        out_shape=list(out_shape) + exchange.outs, name=name, scratch_shapes=list(scratch) + sems,
        input_output_aliases=aliases, compiler_params=_params(("arbitrary",) * len(grid)))(*args, *exchange.ins)


def _run_exchange(exchange, name):
    def body(*refs):
        n_i, n_o = len(exchange.ins), len(exchange.outs)
        for _, fn in exchange.phases:
            fn(refs[:n_i], refs[n_i:n_i + n_o], refs[n_i + n_o], refs[n_i + n_o + 1], 0)

    sems = [pltpu.SemaphoreType.DMA((exchange.n_sems,)), pltpu.SemaphoreType.DMA((exchange.n_sems,))]
    return pl.pallas_call(body, in_specs=[_ANY] * len(exchange.ins), out_specs=[_ANY] * len(exchange.outs),
                          out_shape=exchange.outs, scratch_shapes=sems, name=name)(*exchange.ins)


def _fused(parts, *, grid, name, exchange=None):
    counts = [(len(p["in_specs"]), len(p["out_specs"]), len(p["scratch"])) for p in parts]
    n_in, n_out = sum(c[0] for c in counts), sum(c[1] for c in counts)

    def kern(*refs):
        ins, outs, scr = refs[:n_in], refs[n_in:n_in + n_out], refs[n_in + n_out:]
        i0 = o0 = s0 = 0
        for p, (ni, no, ns) in zip(parts, counts):
            p["kern"](*ins[i0:i0 + ni], *outs[o0:o0 + no], *scr[s0:s0 + ns])
            i0, o0, s0 = i0 + ni, o0 + no, s0 + ns

    cat = lambda key: [a for p in parts for a in p[key]]
    return _pallas(kern, grid=grid, in_specs=cat("in_specs"), out_specs=cat("out_specs"), out_shape=cat("out_shape"),
                   scratch=cat("scratch"), args=cat("args"), name=name, exchange=exchange)


def _row_call(body, *, tm, row_ins, res_ins, row_outs, part_outs=(), name, exchange=None):
    t = row_ins[0].shape[0]
    n_tiles = t // tm
    in_specs = [pl.BlockSpec((tm, a.shape[1]), lambda i: (i, 0)) for a in row_ins]
    in_specs += [pl.BlockSpec(a.shape, lambda i: (0, 0), pipeline_mode=pl.Buffered(1)) for a in res_ins]
    out_shape = [jax.ShapeDtypeStruct((t, w), dt) for (w, dt) in row_outs]
    out_shape += [jax.ShapeDtypeStruct((n_tiles, 1, w), F32) for w in part_outs]
    out_specs = [pl.BlockSpec((tm, w), lambda i: (i, 0)) for (w, _) in row_outs]
    out_specs += [pl.BlockSpec((1, 1, w), lambda i: (i, 0, 0)) for w in part_outs]
    n_ri, n_re, n_ro = len(row_ins), len(res_ins), len(row_outs)

    def kern(*refs):
        body(refs[:n_ri], refs[n_ri:n_ri + n_re], refs[n_ri + n_re:n_ri + n_re + n_ro], refs[n_ri + n_re + n_ro:])

    return _pallas(kern, grid=(n_tiles,), in_specs=in_specs, out_specs=out_specs, out_shape=out_shape,
                   args=[*row_ins, *res_ins], name=name, exchange=exchange)


def _proj_fwd(x, g1, w_in_t, exchange):
    pieces = ((P_QA, F32), (P_KVA, F32), (P_QR, F32), (P_KR, F32), (P_VR, BF16), (P_GR, F32), (P_ZA, F32), (P_ZR, F32))

    def body(ri, re, ro, po):
        x_t = ri[0][...]
        r, xhat = _rms_stats(x_t)
        hb = _bf(xhat * re[0][...])
        ro[0][...] = hb
        for k, ((off, w), dt) in enumerate(pieces):
            ro[1 + k][...] = _dot_nt(hb, re[1][off:off + w, :]).astype(dt)

    outs = [(D_MODEL, BF16)] + [(w, dt) for ((_, w), dt) in pieces]
    return _row_call(body, tm=256, row_ins=[x], res_ins=[g1, w_in_t], row_outs=outs, name="proj_fwd",
                     exchange=exchange)


def _mix_fwd(attn, ret, z_a, z_r, x, wba, wbr, wout, g2):
    def body(ri, re, ro, po):
        ba = _dot(ri[0][...], re[0][...])
        br = _dot(ri[1][...], re[1][...])
        m = jax.nn.sigmoid(ri[2][...]) * ba + jax.nn.sigmoid(ri[3][...]) * br
        mb = _bf(m)
        x1 = ri[4][...] + _dot(mb, re[2][...])
        r, xhat = _rms_stats(x1)
        ro[0][...] = ba
        ro[1][...] = br
        ro[2][...] = mb
        ro[3][...] = x1
        ro[4][...] = _bf(xhat * re[3][...])

    outs = [(D_MODEL, F32), (D_MODEL, F32), (D_MODEL, BF16), (D_MODEL, F32), (D_MODEL, BF16)]
    return _row_call(body, tm=512, row_ins=[attn, ret, z_a, z_r, x], res_ins=[wba, wbr, wout, g2], row_outs=outs,
                     name="mix_fwd")


def _ffn_fwd_bwd(h2, x1, target, wg_t, wu_t, wd, g2):
    def body(ri, re, ro, po):
        h2_t = ri[0][...]
        x1_t = ri[1][...]
        gate = _dot_nt(h2_t, re[0][...])
        up = _dot_nt(h2_t, re[1][...])
        sg = jax.nn.sigmoid(gate)
        sl = gate * sg
        actb = _bf(sl * up)
        ro[0][...] = actb
        y = x1_t + _dot(actb, re[2][...])
        e = y - ri[2][...]
        po[0][0] = jnp.broadcast_to(0.5 * jnp.sum(jnp.sum(e * e, axis=1, keepdims=True), axis=0, keepdims=True)
                                    * (1.0 / D_MODEL), (1, 128))
        dy = e * (1.0 / D_MODEL)
        dyb = _bf(dy)
        ro[3][...] = dyb
        dact = _dot_nt(dyb, re[2][...])
        dupb = _bf(dact * sl)
        dgateb = _bf(dact * up * (sg * (1.0 + gate * (1.0 - sg))))
        ro[1][...] = dgateb
        ro[2][...] = dupb
        dh2 = _dot(dgateb, re[0][...]) + _dot(dupb, re[1][...])
        r, xhat = _rms_stats(x1_t)
        dxn, dgain = _rms_bwd(dh2, xhat, r, re[3][...])
        dx1 = dy + dxn
        ro[4][...] = dx1
        ro[5][...] = _bf(dx1)
        po[1][0] = jnp.sum(dgain, axis=0, keepdims=True)

    outs = [(D_FF, BF16), (D_FF, BF16), (D_FF, BF16), (D_MODEL, BF16), (D_MODEL, F32), (D_MODEL, BF16)]
    return _row_call(body, tm=256, row_ins=[h2, x1, target], res_ins=[wg_t, wu_t, wd, g2], row_outs=outs,
                     part_outs=(128, D_MODEL), name="ffn_fwd_bwd")


def _mix_bwd(dx1b, z_a, z_r, ba, br, g_r, o_ret, wout, wba, wbr, exchange):
    def body(ri, re, ro, po):
        dm = _dot_nt(ri[0][...], re[0][...])
        sa = jax.nn.sigmoid(ri[1][...])
        sr = jax.nn.sigmoid(ri[2][...])
        dbab = _bf(sa * dm)
        dbrb = _bf(sr * dm)
        ro[0][...] = dbab
        ro[1][...] = dbrb
        ro[4][:, RET_V:RET_V + D_MODEL] = _bf(dm * ri[3][...] * (sa * (1.0 - sa)))
        ro[4][:, RET_V + D_MODEL:RET_V + 2 * D_MODEL] = _bf(dm * ri[4][...] * (sr * (1.0 - sr)))
        ro[2][...] = _bf(_dot_nt(dbab, re[1][...]))
        dret = _dot_nt(dbrb, re[2][...])
        for h in range(RET_HEADS):
            cols = slice(h * RET_V_DIM, (h + 1) * RET_V_DIM)
            g = ri[5][:, cols]
            r, rn = _rms_stats(ri[6][:, cols])
            sg = jax.nn.sigmoid(g)
            dret_h = dret[:, cols]
            d_rn = dret_h * (g * sg)
            ro[4][:, cols] = _bf(dret_h * rn * (sg * (1.0 + g * (1.0 - sg))))
            ro[3][:, cols] = r * (d_rn - rn * jnp.mean(d_rn * rn, axis=-1, keepdims=True))

    outs = [(D_MODEL, BF16), (D_MODEL, BF16), (ATT_Q, BF16), (RET_V, F32), (RET_V + 2 * D_MODEL, BF16)]
    return _row_call(body, tm=256, row_ins=[dx1b, z_a, z_r, ba, br, g_r, o_ret], res_ins=[wout, wba, wbr],
                     row_outs=outs, name="mix_bwd", exchange=exchange)


def _proj_bwd(d_pieces, x, dx1, w_in_t, g1, exchange):
    widths = [p.shape[1] for p in d_pieces]
    groups = [(sum(widths[:k]), w) for k, w in enumerate(widths)]
    n_p = len(groups)

    def body(ri, re, ro, po):
        dh = None
        for k, (off, w) in enumerate(groups):
            term = _dot(ri[k][...], re[0][off:off + w, :])
            dh = term if dh is None else dh + term
        r, xhat = _rms_stats(ri[n_p][...])
        dxn, dgain = _rms_bwd(dh, xhat, r, re[1][...])
        ro[0][...] = ri[n_p + 1][...] + dxn
        po[0][0] = jnp.sum(dgain, axis=0, keepdims=True)

    return _row_call(body, tm=512, row_ins=[*d_pieces, x, dx1], res_ins=[w_in_t, g1], row_outs=[(D_MODEL, F32)],
                     part_outs=(D_MODEL,), name="proj_bwd", exchange=exchange)


def _dw(a, b, *, tm, place, buf, name, exchange=None):
    t, m = a.shape
    n = b.shape[1]
    tk = min(2048, t)
    n_i, n_k = m // tm, t // tk
    fresh = isinstance(buf, jax.ShapeDtypeStruct)
    n_copies = len(place(0))

    def kern(a_ref, b_ref, *rest):
        out_ref, acc, sems = rest[-3:]
        i, k = pl.program_id(0), pl.program_id(1)
        part = _dot_tn(a_ref[...], b_ref[...])

        @pl.when(k == 0)
        def _():
            acc[i] = part

        @pl.when(k > 0)
        def _():
            acc[i] += part

        def copies(tile):
            return [pltpu.make_async_copy(acc.at[tile, pl.ds(r0, rows), :], out_ref.at[idx], sems.at[tile * n_copies + c])
                    for c, (r0, rows, idx) in enumerate(place(tile))]

        for tile in range(n_i):
            @pl.when((i == tile) & (k == n_k - 1))
            def _(tile=tile):
                for cp in copies(tile):
                    cp.start()

        @pl.when((i == n_i - 1) & (k == n_k - 1))
        def _():
            for tile in range(n_i):
                for cp in copies(tile):
                    cp.wait()

    in_specs = [pl.BlockSpec((tk, tm), lambda i, k: (k, i)), pl.BlockSpec((tk, n), lambda i, k: (k, 0))]
    shape = buf if fresh else jax.ShapeDtypeStruct(buf.shape, buf.dtype)
    return _pallas(
        kern, grid=(n_i, n_k), in_specs=in_specs + ([] if fresh else [_ANY]), out_specs=[_ANY], out_shape=[shape],
        scratch=[pltpu.VMEM((n_i, tm, n), F32), pltpu.SemaphoreType.DMA((n_i * n_copies,))],
        args=[a, b] + ([] if fresh else [buf]), aliases=None if fresh else {2: 0}, name=name, exchange=exchange)


def _heads_to_lanes(x3):
    return jnp.concatenate([x3[g] for g in range(GROUP)], axis=1)


def _lanes_to_heads(xt):
    return jnp.concatenate([xt[:, g * BLOCK:(g + 1) * BLOCK] for g in range(GROUP)], axis=0)


def _attn_queries(kvh, q_ref, gq_col):
    cols = slice(kvh * GROUP * HEAD_DIM, (kvh + 1) * GROUP * HEAD_DIM)
    q3 = q_ref[:, cols].T.reshape(GROUP, HEAD_DIM, BLOCK)
    rq = lax.rsqrt(jnp.mean(q3 * q3, axis=1, keepdims=True) + EPS)
    qhat = q3 * rq
    return qhat, rq, _heads_to_lanes(_bf(qhat * (gq_col * (HEAD_DIM ** -0.5))))


def _from_prev():
    j = lax.broadcasted_iota(jnp.int32, (BLOCK, GROUP * BLOCK), 0)
    i = lax.broadcasted_iota(jnp.int32, (BLOCK, GROUP * BLOCK), 1) & (BLOCK - 1)
    return j > i


def _attn_probs(n, kvh, qts, kvp_ref, kvc_ref, gk, sink_ref):
    kcols = slice(kvh * HEAD_DIM, (kvh + 1) * HEAD_DIM)
    k = jnp.concatenate([kvp_ref[:, kcols], kvc_ref[:, kcols]], axis=0)
    rk, khat = _rms_stats(k)
    st = _dot(_bf(khat * gk), qts)
    f = jnp.where(_from_prev(), jnp.where(n > 0, st[0:BLOCK], -1e30), st[BLOCK:2 * BLOCK])
    sink = jnp.concatenate([jnp.broadcast_to(sink_ref[0:1, kvh * GROUP + g:kvh * GROUP + g + 1], (1, BLOCK))
                            for g in range(GROUP)], axis=1)
    m = jnp.maximum(jnp.max(f, axis=0, keepdims=True), sink)
    e = jnp.exp(f - m)
    es = jnp.exp(sink - m)
    inv = 1.0 / (jnp.sum(e, axis=0, keepdims=True) + es)
    return e * inv, es * inv


def _unfold(from_prev, xf):
    return _bf(jnp.concatenate([jnp.where(from_prev, xf, 0.0), jnp.where(from_prev, 0.0, xf)], axis=0))


def _attn_fwd(q_a, kv_a, gq_col, gk, sinks):
    t = q_a.shape[0]
    nb = t // BLOCK

    def kern(q_ref, kvp_ref, kvc_ref, gq_ref, gk_ref, sink_ref, o_ref, pf_ref, ps_ref):
        n = pl.program_id(0)
        kvt = jnp.concatenate([kvp_ref[...].T, kvc_ref[...].T], axis=1)
        for kvh in range(N_KV_HEADS):
            _, _, qts = _attn_queries(kvh, q_ref, gq_ref[...])
            pf, psink = _attn_probs(n, kvh, qts, kvp_ref, kvc_ref, gk_ref[...], sink_ref)
            lanes = slice(kvh * GROUP * BLOCK, (kvh + 1) * GROUP * BLOCK)
            pf_ref[:, lanes] = pf
            ps_ref[:, lanes] = psink
            vt = _bf(kvt[ATT_KV + kvh * HEAD_DIM:ATT_KV + (kvh + 1) * HEAD_DIM, :])
            out_t = _dot(vt, _unfold(_from_prev(), pf))
            cols = slice(kvh * GROUP * HEAD_DIM, (kvh + 1) * GROUP * HEAD_DIM)
            o_ref[:, cols] = _bf(_lanes_to_heads(out_t).T)

    small = lambda a: pl.BlockSpec(a.shape, lambda n: (0, 0))
    folded = N_KV_HEADS * GROUP * BLOCK
    return dict(
        kern=kern,
        in_specs=[pl.BlockSpec((BLOCK, ATT_Q), lambda n: (n, 0)),
                  pl.BlockSpec((BLOCK, 2 * ATT_KV), lambda n: (jnp.maximum(n - 1, 0), 0)),
                  pl.BlockSpec((BLOCK, 2 * ATT_KV), lambda n: (n, 0)),
                  small(gq_col), small(gk), small(sinks)],
        out_specs=[pl.BlockSpec((BLOCK, ATT_Q), lambda n: (n, 0)), pl.BlockSpec((BLOCK, folded), lambda n: (n, 0)),
                   pl.BlockSpec((None, 1, folded), lambda n: (n, 0, 0))],
        out_shape=[jax.ShapeDtypeStruct((t, ATT_Q), BF16), jax.ShapeDtypeStruct((t, folded), F32),
                   jax.ShapeDtypeStruct((nb, 1, folded), F32)],
        scratch=[], args=[q_a, kv_a, kv_a, gq_col, gk, sinks])


def _attn_bwd(q_a, kv_a, d_attn, probs, sink_probs, gq_col, gk, gk_col, exchange):
    t = q_a.shape[0]
    nb = t // BLOCK

    def kern(q_ref, kvp_ref, kvc_ref, do_ref, pf_ref, ps_ref, gq_ref, gk_ref, gkc_ref,
             dq_ref, dkv_ref, dgq_ref, dgk_ref, dsink_ref, band_k, band_v, carry_k, carry_v):
        n = pl.program_id(0)
        gq_v = gq_ref[...]
        gk_v = gk_ref[...]

        @pl.when(n == 0)
        def _():
            carry_k[...] = jnp.zeros_like(carry_k)
            carry_v[...] = jnp.zeros_like(carry_v)
            dgq_ref[...] = jnp.zeros_like(dgq_ref)
            dgk_ref[...] = jnp.zeros_like(dgk_ref)
            dsink_ref[...] = jnp.zeros_like(dsink_ref)

        @pl.when(n == nb)
        def _():
            band_k[...] = jnp.zeros_like(band_k)
            band_v[...] = jnp.zeros_like(band_v)

        @pl.when(n < nb)
        def _():
            lane16 = lax.broadcasted_iota(jnp.int32, (1, N_Q_HEADS), 1)
            dsink = jnp.zeros((1, N_Q_HEADS), F32)
            dgq = jnp.zeros((HEAD_DIM, 1), F32)
            gk_col = gkc_ref[...]
            kvt = jnp.concatenate([kvp_ref[...].T, kvc_ref[...].T], axis=1)
            from_prev = _from_prev()
            for kvh in range(N_KV_HEADS):
                qhat, rq, qts = _attn_queries(kvh, q_ref, gq_v)
                lanes = slice(kvh * GROUP * BLOCK, (kvh + 1) * GROUP * BLOCK)
                pf = pf_ref[:, lanes]
                cols = slice(kvh * GROUP * HEAD_DIM, (kvh + 1) * GROUP * HEAD_DIM)
                vcols = slice(ATT_KV + kvh * HEAD_DIM, ATT_KV + (kvh + 1) * HEAD_DIM)
                dot = _heads_to_lanes(_bf(do_ref[:, cols].astype(F32).T.reshape(GROUP, HEAD_DIM, BLOCK)))
                vb = _bf(jnp.concatenate([kvp_ref[:, vcols], kvc_ref[:, vcols]], axis=0))
                dpt = _dot(vb, dot)
                dpf = jnp.where(from_prev, dpt[0:BLOCK], dpt[BLOCK:2 * BLOCK])
                delta = jnp.sum(pf * dpf, axis=0, keepdims=True)
                dst = _unfold(from_prev, pf * (dpf - delta))
                dsk = ps_ref[:, lanes] * delta
                for g in range(GROUP):
                    tot = jnp.sum(dsk[:, g * BLOCK:(g + 1) * BLOCK], axis=1, keepdims=True)
                    dsink = dsink - jnp.where(lane16 == kvh * GROUP + g, tot, 0.0)
                kt = kvt[kvh * HEAD_DIM:(kvh + 1) * HEAD_DIM, :]
                knt = _bf(kt * lax.rsqrt(jnp.mean(kt * kt, axis=0, keepdims=True) + EPS) * gk_col)
                dqn = (_dot(knt, dst) * (HEAD_DIM ** -0.5))
                band_k[kvh] = _dot_nt(dst, qts)
                band_v[kvh] = _dot_nt(_unfold(from_prev, pf), dot)
                dqn3 = _lanes_to_heads(dqn).reshape(GROUP, HEAD_DIM, BLOCK)
                u = dqn3 * gq_v
                dq3 = rq * (u - qhat * jnp.mean(u * qhat, axis=1, keepdims=True))
                dgq = dgq + jnp.sum(jnp.sum(dqn3 * qhat, axis=0), axis=1, keepdims=True)
                dq_ref[:, cols] = _bf(dq3.reshape(GROUP * HEAD_DIM, BLOCK).T)
            dsink_ref[...] += dsink
            dgq_ref[...] += dgq

        dgk = jnp.zeros((1, HEAD_DIM), F32)
        for kvh in range(N_KV_HEADS):
            kcols = slice(kvh * HEAD_DIM, (kvh + 1) * HEAD_DIM)
            vcols = slice(ATT_KV + kvh * HEAD_DIM, ATT_KV + (kvh + 1) * HEAD_DIM)
            dkn = carry_k[kvh] + band_k[kvh, 0:BLOCK, :]
            dv = carry_v[kvh] + band_v[kvh, 0:BLOCK, :]
            rk, khat = _rms_stats(kvp_ref[:, kcols])
            dk, dgain = _rms_bwd(dkn, khat, rk, gk_v)
            dgk = dgk + jnp.sum(dgain, axis=0, keepdims=True)
            dkv_ref[:, kcols] = _bf(dk)
            dkv_ref[:, vcols] = _bf(dv)
            carry_k[kvh] = band_k[kvh, BLOCK:2 * BLOCK, :]
            carry_v[kvh] = band_v[kvh, BLOCK:2 * BLOCK, :]
        dgk_ref[...] += dgk

    small = lambda a: pl.BlockSpec(a.shape, lambda n: (0, 0))
    last = nb - 1
    return _pallas(
        kern, grid=(nb + 1,),
        in_specs=[pl.BlockSpec((BLOCK, ATT_Q), lambda n: (jnp.minimum(n, last), 0)),
                  pl.BlockSpec((BLOCK, 2 * ATT_KV), lambda n: (jnp.maximum(n - 1, 0), 0)),
                  pl.BlockSpec((BLOCK, 2 * ATT_KV), lambda n: (jnp.minimum(n, last), 0)),
                  pl.BlockSpec((BLOCK, ATT_Q), lambda n: (jnp.minimum(n, last), 0)),
                  pl.BlockSpec((BLOCK, probs.shape[1]), lambda n: (jnp.minimum(n, last), 0)),
                  pl.BlockSpec((None, 1, probs.shape[1]), lambda n: (jnp.minimum(n, last), 0, 0)),
                  small(gq_col), small(gk), small(gk_col)],
        out_specs=[pl.BlockSpec((BLOCK, ATT_Q), lambda n: (jnp.minimum(n, last), 0)),
                   pl.BlockSpec((BLOCK, 2 * ATT_KV), lambda n: (jnp.maximum(n - 1, 0), 0)),
                   pl.BlockSpec((HEAD_DIM, 1), lambda n: (0, 0)),
                   pl.BlockSpec((1, HEAD_DIM), lambda n: (0, 0)),
                   pl.BlockSpec((1, N_Q_HEADS), lambda n: (0, 0))],
        out_shape=[jax.ShapeDtypeStruct((t, ATT_Q), BF16), jax.ShapeDtypeStruct((t, 2 * ATT_KV), BF16),
                   jax.ShapeDtypeStruct((HEAD_DIM, 1), F32), jax.ShapeDtypeStruct((1, HEAD_DIM), F32),
                   jax.ShapeDtypeStruct((1, N_Q_HEADS), F32)],
        scratch=[pltpu.VMEM((N_KV_HEADS, 2 * BLOCK, HEAD_DIM), F32),
                 pltpu.VMEM((N_KV_HEADS, 2 * BLOCK, HEAD_DIM), F32),
                 pltpu.VMEM((N_KV_HEADS, BLOCK, HEAD_DIM), F32),
                 pltpu.VMEM((N_KV_HEADS, BLOCK, HEAD_DIM), F32)],
        args=[q_a, kv_a, kv_a, d_attn, probs, sink_probs, gq_col, gk, gk_col], name="attn_bwd", exchange=exchange)


def _ret_tables(t, exchange):
    theta = 1.0 / (RET_ROT_BASE ** jnp.linspace(0.0, 1.0, RET_QK_DIM // 2, dtype=F32))
    theta2 = jnp.repeat(theta, 2)[None, :]
    sign = jnp.tile(jnp.array([-1.0, 1.0], F32), RET_QK_DIM // 2)[None, :]

    def kern(theta_ref, sign_ref, cos_ref, sin_ref):
        first = pl.program_id(0) * RET_CHUNK
        pos = (first + lax.broadcasted_iota(jnp.int32, (RET_CHUNK, RET_QK_DIM), 0)).astype(F32)
        ang = pos * theta_ref[...]
        cos_ref[...] = jnp.cos(ang)
        sin_ref[...] = jnp.sin(ang) * sign_ref[...]

    row = pl.BlockSpec((1, RET_QK_DIM), lambda n: (0, 0))
    blk = pl.BlockSpec((RET_CHUNK, RET_QK_DIM), lambda n: (n, 0))
    cos, sin_s, *got = _pallas(kern, grid=(t // RET_CHUNK,), in_specs=[row, row], out_specs=[blk, blk],
                               out_shape=[jax.ShapeDtypeStruct((t, RET_QK_DIM), F32)] * 2, args=[theta2, sign],
                               name="position_tables", exchange=exchange)
    log_gamma = jnp.log(1.0 - 2.0 ** (-5.0 - jnp.arange(RET_HEADS, dtype=F32)))
    i = jnp.arange(RET_CHUNK, dtype=F32)
    diff = i[:, None] - i[None, :]
    causal = diff >= 0
    decay = jnp.where(causal[None], jnp.exp(jnp.where(causal, diff, 0.0)[None] * log_gamma[:, None, None]), 0.0)
    xi = jnp.exp((i + 1.0)[None, :] * log_gamma[:, None])[:, :, None]
    zeta = jnp.exp((RET_CHUNK - 1.0 - i)[None, :] * log_gamma[:, None])[:, :, None]
    gch = jnp.broadcast_to(jnp.exp(RET_CHUNK * log_gamma)[:, None, None], (RET_HEADS, 1, 128))
    return (cos, sin_s, decay, xi, zeta, gch), got


def _swap_pairs(x):
    lane = lax.broadcasted_iota(jnp.int32, x.shape, 1)
    return jnp.where((lane & 1) == 0, pltpu.roll(x, RET_QK_DIM - 1, 1), pltpu.roll(x, 1, 1))


def _rotate(x, cos, sin_s):
    return x * cos + _swap_pairs(x) * sin_s


def _rotate_bwd(dy, cos, sin_s):
    return dy * cos + _swap_pairs(dy * sin_s)


def _ret_specs(order):
    qk = pl.BlockSpec((RET_CHUNK, RET_QK), lambda j: (order(j), 0))
    v = pl.BlockSpec((RET_CHUNK, RET_V), lambda j: (order(j), 0))
    dec = pl.BlockSpec((RET_HEADS, RET_CHUNK, RET_CHUNK), lambda j: (0, 0, 0))
    col = pl.BlockSpec((RET_HEADS, RET_CHUNK, 1), lambda j: (0, 0, 0))
    gch = pl.BlockSpec((RET_HEADS, 1, 128), lambda j: (0, 0, 0))
    st = pl.BlockSpec((RET_HEADS, None, RET_QK_DIM, RET_V_DIM), lambda j: (0, order(j), 0, 0))
    pos = pl.BlockSpec((RET_CHUNK, RET_QK_DIM), lambda j: (order(j), 0))
    return qk, v, dec, col, gch, st, pos


def _ret_fwd(q_r, k_r, v_r, g_r, tables):
    t = q_r.shape[0]
    nc = t // RET_CHUNK
    cos, sin_s, decay, xi, zeta, gch = tables

    def kern(q_ref, k_ref, v_ref, g_ref, cos_ref, sin_ref, dec_ref, xi_ref, zeta_ref, gch_ref,
             o_ref, ret_ref, st_ref, state):
        @pl.when(pl.program_id(0) == 0)
        def _():
            state[...] = jnp.zeros_like(state)

        cos_t = cos_ref[...]
        sin_t = sin_ref[...]
        for h in range(RET_HEADS):
            qc = slice(h * RET_QK_DIM, (h + 1) * RET_QK_DIM)
            vc = slice(h * RET_V_DIM, (h + 1) * RET_V_DIM)
            qs = _bf(_rotate(q_ref[:, qc], cos_t, sin_t))
            ks = _rotate(k_ref[:, qc] * (RET_QK_DIM ** -0.5), cos_t, sin_t)
            vb = v_ref[:, vc]
            s_old = state[h]
            sb = _bf(s_old)
            st_ref[h] = sb
            inner = _dot_nt(qs, _bf(ks)) * dec_ref[h]
            out = _dot(_bf(inner), vb) + _dot(qs, sb) * xi_ref[h]
            state[h] = gch_ref[h, :, 0:1] * s_old + _dot_tn(_bf(ks * zeta_ref[h]), vb)
            o_ref[:, vc] = out
            r, rn = _rms_stats(out)
            g = g_ref[:, vc]
            ret_ref[:, vc] = _bf(g * jax.nn.sigmoid(g) * rn)

    qk, v, dec, col, gsp, st, pos = _ret_specs(lambda j: j)
    return dict(
        kern=kern,
        in_specs=[qk, qk, v, v, pos, pos, dec, col, col, gsp],
        out_specs=[v, v, st],
        out_shape=[jax.ShapeDtypeStruct((t, RET_V), F32), jax.ShapeDtypeStruct((t, RET_V), BF16),
                   jax.ShapeDtypeStruct((RET_HEADS, nc, RET_QK_DIM, RET_V_DIM), BF16)],
        scratch=[pltpu.VMEM((RET_HEADS, RET_QK_DIM, RET_V_DIM), F32)],
        args=[q_r, k_r, v_r, g_r, cos, sin_s, decay, xi, zeta, gch])


def _ret_bwd(q_r, k_r, v_r, d_o, states, tables, exchange):
    t = q_r.shape[0]
    nc = t // RET_CHUNK
    cos, sin_s, decay, xi, zeta, gch = tables

    def kern(q_ref, k_ref, v_ref, do_ref, st_ref, cos_ref, sin_ref, dec_ref, xi_ref, zeta_ref, gch_ref,
             d_ref, dstate):
        dq_ref, dk_ref = d_ref.at[:, 0:RET_QK], d_ref.at[:, RET_QK:2 * RET_QK]
        dv_ref = d_ref.at[:, 2 * RET_QK:2 * RET_QK + RET_V]

        @pl.when(pl.program_id(0) == 0)
        def _():
            dstate[...] = jnp.zeros_like(dstate)

        cos_t = cos_ref[...]
        sin_t = sin_ref[...]
        scale = RET_QK_DIM ** -0.5
        for h in range(RET_HEADS):
            qc = slice(h * RET_QK_DIM, (h + 1) * RET_QK_DIM)
            vc = slice(h * RET_V_DIM, (h + 1) * RET_V_DIM)
            qs = _bf(_rotate(q_ref[:, qc], cos_t, sin_t))
            ks = _rotate(k_ref[:, qc] * scale, cos_t, sin_t)
            ksb = _bf(ks)
            vb = v_ref[:, vc]
            d_o_t = do_ref[:, vc]
            dob = _bf(d_o_t)
            doxb = _bf(d_o_t * xi_ref[h])
            dec = dec_ref[h]
            ds_old = dstate[h]
            dsb = _bf(ds_old)
            pb = _bf(_dot_nt(qs, ksb) * dec)
            dpb = _bf(_dot_nt(dob, vb) * dec)
            dqs = _dot(dpb, ksb) + _dot_nt(doxb, st_ref[h])
            dks = _dot_tn(dpb, qs) + _dot_nt(vb, dsb) * zeta_ref[h]
            dv_ref[:, vc] = _bf(_dot_tn(pb, dob) + _dot(_bf(ks * zeta_ref[h]), dsb))
            dstate[h] = gch_ref[h, :, 0:1] * ds_old + _dot_tn(qs, doxb)
            dq_ref[:, qc] = _bf(_rotate_bwd(dqs, cos_t, sin_t))
            dk_ref[:, qc] = _bf(_rotate_bwd(dks, cos_t, sin_t) * scale)

    qk, v, dec, col, gsp, st, pos = _ret_specs(lambda j: nc - 1 - j)
    return _pallas(
        kern, grid=(nc,),
        in_specs=[qk, qk, v, v, st, pos, pos, dec, col, col, gsp],
        out_specs=[pl.BlockSpec((RET_CHUNK, 2 * RET_QK + RET_V), lambda j: (nc - 1 - j, 0))],
        out_shape=[jax.ShapeDtypeStruct((t, 2 * RET_QK + RET_V), BF16)],
        scratch=[pltpu.VMEM((RET_HEADS, RET_QK_DIM, RET_V_DIM), F32)],
        args=[q_r, k_r, v_r, d_o, states, cos, sin_s, decay, xi, zeta, gch], name="ret_bwd", exchange=exchange)


def _position():
    return lax.axis_index("x"), lax.axis_index("y"), lax.axis_index("c")


def _gather_exchange(owns, forward_at):
    n = len(owns)

    def copies(ins, outs, send_sems, recv_sems, base):
        x, y, c = _position()
        sibling = (x, y, 1 - c)
        chips = [(1 - x, y), (x, 1 - y), (1 - x, 1 - y)]
        my_chip = 2 * x + y

        def slab(a, chip, hf):
            half = owns[a].shape[0] // 2
            return outs[a].at[chip, pl.ds(hf * half, half), :]

        def copy(k, src, dst, to):
            return pltpu.make_async_remote_copy(src_ref=src, dst_ref=dst, send_sem=send_sems.at[base + k],
                                                recv_sem=recv_sems.at[base + k], device_id=to, device_id_type=MESH)

        first, passed, from_sibling = [], [], []
        for a in range(n):
            half = owns[a].shape[0] // 2
            for k, (cx, cy) in enumerate(chips):
                first.append(copy(6 * a + k, ins[a].at[pl.ds(c * half, half), :], slab(a, my_chip, c), (cx, cy, c)))
                landed = slab(a, 2 * cx + cy, c)
                passed.append(copy(6 * a + 3 + k, landed, landed, sibling))
                theirs = slab(a, 2 * cx + cy, 1 - c)
                from_sibling.append(copy(6 * a + 3 + k, theirs, theirs, sibling))
        return first, passed, from_sibling

    def start(*args):
        first, _, _ = copies(*args)
        for cp in first:
            cp.start()

    def forward(*args):
        first, passed, _ = copies(*args)
        for arrived, cp in zip(first, passed):
            arrived.wait_recv()
            cp.start()

    def finish(*args):
        first, passed, from_sibling = copies(*args)
        for cp in from_sibling:
            cp.wait_recv()
        for cp in first + passed:
            cp.wait_send()

    outs = [jax.ShapeDtypeStruct((N_CHIPS, *a.shape), a.dtype) for a in owns]
    return _Exchange(owns, outs, 6 * n, [(0.0, start), (forward_at, forward), (1.0, finish)])


def _symmetric_exchange(ins, outs, plan):
    n_sems = len(plan([None] * len(ins), [None] * len(outs), 0, 0, 0, dry=True))

    def copies(in_refs, out_refs, send_sems, recv_sems, base):
        x, y, c = _position()
        return [pltpu.make_async_remote_copy(src_ref=src, dst_ref=dst, send_sem=send_sems.at[base + k],
                                             recv_sem=recv_sems.at[base + k], device_id=dev, device_id_type=MESH)
                for k, (src, dst, dev) in enumerate(plan(in_refs, out_refs, x, y, c, dry=False))]

    def start(*args):
        for cp in copies(*args):
            cp.start()

    def finish(*args):
        for cp in copies(*args):
            cp.wait()

    return _Exchange(ins, outs, n_sems, [(0.0, start), (1.0, finish)])


def _pair_exchange(gs):
    def plan(in_refs, out_refs, x, y, c, dry):
        out = []
        for a, g in enumerate(gs):
            half = g.shape[1] // 2
            for k in range(N_CHIPS):
                out.append(None if dry else (in_refs[a].at[k, pl.ds((1 - c) * half, half), :], out_refs[a].at[k],
                                             (x, y, 1 - c)))
        return out

    outs = [jax.ShapeDtypeStruct((g.shape[0], g.shape[1] // 2, g.shape[2]), g.dtype) for g in gs]
    return _symmetric_exchange(gs, outs, plan)


def _pair_sum(g, from_sibling, c_arr, *, tile, name):
    n, rows, width = g.shape
    tiles = (rows // 2) // tile

    def kern(c_ref, g_ref, s_ref, o_ref):
        o_ref[...] = _bf(g_ref[...] + s_ref[...])

    return pl.pallas_call(
        kern,
        grid_spec=pltpu.PrefetchScalarGridSpec(
            num_scalar_prefetch=1, grid=(n, tiles),
            in_specs=[pl.BlockSpec((None, tile, width), lambda k, i, c: (k, c[0] * tiles + i, 0)),
                      pl.BlockSpec((None, tile, width), lambda k, i, c: (k, i, 0))],
            out_specs=pl.BlockSpec((None, tile, width), lambda k, i, c: (k, i, 0))),
        out_shape=jax.ShapeDtypeStruct((n, rows // 2, width), BF16), name=name,
        compiler_params=_params(("parallel", "parallel")),
    )(c_arr, g, from_sibling)


def _scatter_to_owners(hsums):
    def plan(in_refs, out_refs, x, y, c, dry):
        out = []
        for a in range(len(hsums)):
            for k, (cx, cy) in enumerate([(1 - x, y), (x, 1 - y), (1 - x, 1 - y)]):
                out.append(None if dry else (in_refs[a].at[2 * cx + cy], out_refs[a].at[k], (cx, cy, c)))
        return out

    outs = [jax.ShapeDtypeStruct((3, *h.shape[1:]), h.dtype) for h in hsums]
    return _symmetric_exchange(hsums, outs, plan)


def _sum_chips(hsum, parts, chip_arr, *, tile, name):
    n, half, width = parts.shape

    def kern(chip_ref, h_ref, p_ref, o_ref):
        acc = h_ref[...].astype(F32)
        for k in range(n):
            acc = acc + p_ref[k].astype(F32)
        o_ref[...] = acc

    return pl.pallas_call(
        kern,
        grid_spec=pltpu.PrefetchScalarGridSpec(
            num_scalar_prefetch=1, grid=(half // tile,),
            in_specs=[pl.BlockSpec((None, tile, width), lambda i, chip: (chip[0], i, 0)),
                      pl.BlockSpec((n, tile, width), lambda i, chip: (0, i, 0))],
            out_specs=pl.BlockSpec((tile, width), lambda i, chip: (i, 0))),
        out_shape=jax.ShapeDtypeStruct((half, width), F32), name=name,
        compiler_params=_params(("parallel",)),
    )(chip_arr, hsum, parts)


def _share_halves(fhalves):
    def plan(in_refs, out_refs, x, y, c, dry):
        return [None if dry else (in_refs[a], out_refs[a], (x, y, 1 - c)) for a in range(len(fhalves))]

    return _symmetric_exchange(fhalves, [jax.ShapeDtypeStruct(f.shape, f.dtype) for f in fhalves], plan)


def _adamw_math(w, g, m, v):
    m = ADAM_B1 * m + (1.0 - ADAM_B1) * g
    v = ADAM_B2 * v + (1.0 - ADAM_B2) * (g * g)
    m_hat = m / (1.0 - ADAM_B1 ** ADAM_STEP)
    v_hat = v / (1.0 - ADAM_B2 ** ADAM_STEP)
    delta = -ADAM_LR * (m_hat / (jnp.sqrt(v_hat) + ADAM_EPS) + ADAM_WD * w)
    return delta, m, v


def _adamw(mats, g_mine, g_other, c_arr, *, tile, name):
    width = g_mine.shape[1]
    tiles_per_half = g_mine.shape[0] // tile
    n_tiles = [w.shape[0] // tile for w, _, _, _ in mats]
    n_mats = len(mats)

    def kern(c_ref, *refs):
        ins, outs = refs[:5 * n_mats], refs[5 * n_mats:]
        for j, (_, _, _, row_off) in enumerate(mats):
            w_ref, gm_ref, go_ref, m_ref, v_ref = ins[5 * j:5 * j + 5]
            i = jnp.minimum(pl.program_id(0), n_tiles[j] - 1)
            in_my_half = ((row_off // tile + i) // tiles_per_half) == c_ref[0]
            g = jnp.where(in_my_half, gm_ref[...], go_ref[...])
            d, nm, nv = _adamw_math(w_ref[...], g, m_ref[...], v_ref[...])
            for out_ref, val in zip(outs[4 * j:4 * j + 4], (g, d, nm, nv)):
                out_ref[...] = val

    in_specs, out_specs, out_shape, args = [], [], [], []
    for (w, m, v, row_off), nt in zip(mats, n_tiles):
        full = pl.BlockSpec((tile, width), lambda i, c, nt=nt: (jnp.minimum(i, nt - 1), 0))
        half = pl.BlockSpec((tile, width), lambda i, c, nt=nt, first=row_off // tile:
                            ((first + jnp.minimum(i, nt - 1)) % tiles_per_half, 0))
        in_specs += [full, half, half, full, full]
        out_specs += [full] * 4
        out_shape += [jax.ShapeDtypeStruct(w.shape, F32)] * 4
        args += [w, g_mine, g_other, m, v]
    outs = pl.pallas_call(
        kern,
        grid_spec=pltpu.PrefetchScalarGridSpec(num_scalar_prefetch=1, grid=(max(n_tiles),), in_specs=in_specs,
                                               out_specs=out_specs),
        out_shape=out_shape, name=name, compiler_params=_params(("arbitrary",)),
    )(c_arr, *args)
    return [outs[4 * j:4 * j + 4] for j in range(n_mats)]


def _small_step(partials, params):
    slots = ((0, 0, D_MODEL), (1, 0, D_MODEL), (2, 0, HEAD_DIM), (2, 128, HEAD_DIM), (2, 256, N_Q_HEADS))
    loss_slot = (2, 384, 128)

    def body(*refs):
        loss_ref, dg1_ref, dg2_ref, dgq_ref, dgk_ref, dsink_ref = refs[:6]
        p_refs, out_refs = refs[6:21], refs[21:42]
        mine, gathered, send_sems, recv_sems = refs[42:]
        x, y, c = _position()
        me = 4 * x + 2 * y + c
        mine[...] = jnp.zeros_like(mine)
        for (row, lane, n), val in zip(slots + (loss_slot,), (
                jnp.sum(dg1_ref[...], axis=0, keepdims=True), jnp.sum(dg2_ref[...], axis=0, keepdims=True),
                dgq_ref[...], dgk_ref[...], dsink_ref[...], jnp.sum(loss_ref[...], axis=0, keepdims=True))):
            mine[row:row + 1, lane:lane + n] = val
        copies = []
        for k in range(1, N_DEV):
            flip = (k >> 2) & 1, (k >> 1) & 1, k & 1
            to = (x ^ flip[0], y ^ flip[1], c ^ flip[2])
            cp = pltpu.make_async_remote_copy(
                src_ref=mine, dst_ref=gathered.at[me], send_sem=send_sems.at[k - 1], recv_sem=recv_sems.at[k - 1],
                device_id=to, device_id_type=MESH)
            cp.start()
            copies.append(cp)
        gathered[me] = mine[...]
        for k in range(1, N_DEV):
            flip = (k >> 2) & 1, (k >> 1) & 1, k & 1
            src = 4 * (x ^ flip[0]) + 2 * (y ^ flip[1]) + (c ^ flip[2])
            pltpu.make_async_remote_copy(
                src_ref=mine, dst_ref=gathered.at[src], send_sem=send_sems.at[k - 1], recv_sem=recv_sems.at[k - 1],
                device_id=(x, y, c), device_id_type=MESH).wait_recv()
        for cp in copies:
            cp.wait_send()
        total = gathered[0]
        for k in range(1, N_DEV):
            total = total + gathered[k]
        row, lane, n = loss_slot
        out_refs[0][...] = total[row:row + 1, lane:lane + n]
        for i, (row, lane, n) in enumerate(slots):
            g = total[row:row + 1, lane:lane + n]
            d, nm, nv = _adamw_math(p_refs[i][...], g, p_refs[5 + i][...], p_refs[10 + i][...])
            for kind, val in enumerate((g, d, nm, nv)):
                out_refs[1 + 5 * kind + i][...] = val

    vm = pl.BlockSpec(memory_space=pltpu.VMEM)
    shapes = [jax.ShapeDtypeStruct((1, 128), F32)] + [jax.ShapeDtypeStruct((1, n), F32) for _, _, n in slots] * 4
    return pl.pallas_call(
        body, in_specs=[vm] * 21, out_specs=[vm] * 21, out_shape=shapes,
        scratch_shapes=[pltpu.VMEM((SMALL_ROWS, D_MODEL), F32), pltpu.VMEM((N_DEV, SMALL_ROWS, D_MODEL), F32),
                        pltpu.SemaphoreType.DMA((N_DEV - 1,)), pltpu.SemaphoreType.DMA((N_DEV - 1,))],
        name="small_step",
    )(*partials, *params)


def _with_own(gathered, own, my_chip):
    return lax.dynamic_update_slice(gathered, own[None], (my_chip, 0, 0))


def kernel(x, norm_mix_gain, w_in, q_norm_gain, k_norm_gain, attn_sinks, w_branch_attn, w_branch_ret, w_out, norm_ffn_gain, w_ffn_gate, w_ffn_up, w_ffn_down, loss_target, m_norm_mix_gain, m_w_in, m_q_norm_gain, m_k_norm_gain, m_attn_sinks, m_w_branch_attn, m_w_branch_ret, m_w_out, m_norm_ffn_gain, m_w_ffn_gate, m_w_ffn_up, m_w_ffn_down, v_norm_mix_gain, v_w_in, v_q_norm_gain, v_k_norm_gain, v_attn_sinks, v_w_branch_attn, v_w_branch_ret, v_w_out, v_norm_ffn_gain, v_w_ffn_gate, v_w_ffn_up, v_w_ffn_down):
    my_chip = 2 * lax.axis_index("x") + lax.axis_index("y")
    c_arr = lax.axis_index("c").astype(jnp.int32).reshape(1)
    chip_arr = my_chip.astype(jnp.int32).reshape(1)
    x_t, target = x[0], loss_target[0]
    g1, g2, gq, gk, sinks = norm_mix_gain, norm_ffn_gain, q_norm_gain, k_norm_gain, attn_sinks

    tr = lambda a: jnp.transpose(a[0])
    own_w_in = _bf(tr(w_in))
    own_rest = [_bf(a) for a in (tr(w_ffn_gate), tr(w_ffn_up), w_ffn_down[0], w_branch_attn[0], w_branch_ret[0],
                                 w_out[0])]
    tables, (got_w_in,) = _ret_tables(x_t.shape[0], _gather_exchange([own_w_in], 0.9))
    w_in_t = _with_own(got_w_in, own_w_in, my_chip).reshape(D_IN, D_MODEL)
    h1, q_a, kv_a, q_r, k_r, v_r, g_r, z_a, z_r, *got_rest = _proj_fwd(x_t, g1, w_in_t, _gather_exchange(own_rest, 0.8))
    wg_t, wu_t, wd, wba, wbr, wout = [_with_own(got, own, my_chip).reshape(N_CHIPS * own.shape[0], D_MODEL)
                                      for got, own in zip(got_rest, own_rest)]

    gq_col, gk_col = gq.reshape(HEAD_DIM, 1), gk.reshape(HEAD_DIM, 1)
    attn, probs, sink_probs, o_ret, ret, states = _fused(
        [_attn_fwd(q_a, kv_a, gq_col, gk, sinks), _ret_fwd(q_r, k_r, v_r, g_r, tables)],
        grid=(x_t.shape[0] // BLOCK,), name="mixers_fwd")
    ba, br, merged, x1, h2 = _mix_fwd(attn, ret, z_a, z_r, x_t, wba, wbr, wout, g2)
    act, dgate, dup, dyb, dx1, dx1b, loss_p, dg2_p = _ffn_fwd_bwd(h2, x1, target, wg_t, wu_t, wd, g2)

    def pairs(row0, rows):
        return lambda i: [(h * rows, rows, (2 * i + h, pl.ds(row0, rows), slice(None))) for h in range(2)]

    f_block = jax.ShapeDtypeStruct((N_CHIPS, 3 * FF_SH, D_MODEL), F32)
    f_block, = _dw(dgate, h2, tm=2 * FF_SH, place=pairs(0, FF_SH), buf=f_block, name="dw_gate")
    f_block, = _dw(dup, h2, tm=2 * FF_SH, place=pairs(FF_SH, FF_SH), buf=f_block, name="dw_up")
    f_block, = _dw(act, dyb, tm=2 * FF_SH, place=pairs(2 * FF_SH, FF_SH), buf=f_block, name="dw_down")
    (dba, dbr, d_attn, d_o, d_gz, sib_ffn) = _mix_bwd(
        dx1b, z_a, z_r, ba, br, g_r, o_ret, wout, wba, wbr, _pair_exchange([f_block]))
    f_sum = _pair_sum(f_block, sib_ffn, c_arr, tile=528, name="pair_sum_ffn")

    def quarters(row0, rows):
        return lambda i: [(k * rows, rows, (k, pl.ds(row0, rows), slice(None))) for k in range(N_CHIPS)]

    m_block = jax.ShapeDtypeStruct((N_CHIPS, D_MODEL, D_MODEL), F32)
    m_block, = _dw(attn, dba, tm=ATT_Q, place=quarters(0, 256), buf=m_block, name="dw_ba")
    m_block, = _dw(ret, dbr, tm=D_MODEL, place=pairs(256, 512), buf=m_block, name="dw_br")
    m_block, = _dw(merged, dx1b, tm=D_MODEL, place=quarters(768, 256), buf=m_block, name="dw_out")

    def w_in_rows(off, w):
        tm = min(w, D_MODEL)
        return dict(tm=tm, place=lambda i: [(0, tm, (pl.ds(off + i * tm, tm), slice(None)))])

    w_block = jax.ShapeDtypeStruct((D_IN, D_MODEL), F32)
    w_block, sib_mix = _dw(d_gz, h1, buf=w_block, name="dw_in_gz", exchange=_pair_exchange([m_block]),
                           **w_in_rows(P_GR[0], d_gz.shape[1]))
    m_sum = _pair_sum(m_block, sib_mix, c_arr, tile=256, name="pair_sum_mix")

    d_ret, got_ffn_sums = _ret_bwd(q_r, k_r, v_r, d_o, states, tables, _scatter_to_owners([f_sum]))
    ffn_half = _sum_chips(f_sum, got_ffn_sums, chip_arr, tile=528, name="sum_chips_ffn")
    w_block, = _dw(d_ret, h1, buf=w_block, name="dw_in_ret", **w_in_rows(P_QR[0], d_ret.shape[1]))

    (dq_a, dkv_a, dgq, dgk, dsinks, got_mix_sums, ffn_other) = _attn_bwd(
        q_a, kv_a, d_attn, probs, sink_probs, gq_col, gk, gk_col,
        _merge_exchanges(_scatter_to_owners([m_sum]), _share_halves([ffn_half])))
    dgq = dgq.reshape(1, HEAD_DIM)
    mix_half = _sum_chips(m_sum, got_mix_sums, chip_arr, tile=256, name="sum_chips_mix")
    w_block, mix_other = _dw(dq_a, h1, buf=w_block, name="dw_in_q", exchange=_share_halves([mix_half]),
                             **w_in_rows(*P_QA))
    w_block, = _dw(dkv_a, h1, buf=w_block, name="dw_in_kv", **w_in_rows(*P_KVA))

    w_block = w_block.reshape(N_CHIPS, W_IN_SH, D_MODEL)
    sib_w_in, = _run_exchange(_pair_exchange([w_block]), "pair_exchange_w_in")
    w_sum = _pair_sum(w_block, sib_w_in, c_arr, tile=592, name="pair_sum_w_in")
    d_pieces = [dq_a, dkv_a, d_ret, d_gz]
    grad_x, dg1_p, got_w_in_sums = _proj_bwd(d_pieces, x_t, dx1, w_in_t, g1, _scatter_to_owners([w_sum]))
    w_in_half = _sum_chips(w_sum, got_w_in_sums, chip_arr, tile=592, name="sum_chips_w_in")
    w_in_other, = _run_exchange(_share_halves([w_in_half]), "share_halves_w_in")

    def update(name, g_half, g_other, tile, mats):
        outs = _adamw([tuple(tr(a) if t else a[0] for a in wmv) + (off,) for _, *wmv, off, t in mats],
                      g_half, g_other, c_arr, tile=tile, name=f"adamw_{name}")
        return {key: [jnp.transpose(o) if t else o for o in res] for (key, _, _, _, _, t), res in zip(mats, outs)}

    big = {
        **update("w_in", w_in_half, w_in_other, 592, [("w_in", w_in, m_w_in, v_w_in, 0, True)]),
        **update("ffn", ffn_half, ffn_other, 176, [
            ("wg", w_ffn_gate, m_w_ffn_gate, v_w_ffn_gate, 0, True),
            ("wu", w_ffn_up, m_w_ffn_up, v_w_ffn_up, FF_SH, True),
            ("wd", w_ffn_down, m_w_ffn_down, v_w_ffn_down, 2 * FF_SH, False)]),
        **update("mix", mix_half, mix_other, 128, [
            ("wba", w_branch_attn, m_w_branch_attn, v_w_branch_attn, 0, False),
            ("wbr", w_branch_ret, m_w_branch_ret, v_w_branch_ret, 256, False),
            ("wout", w_out, m_w_out, v_w_out, 768, False)])}

    loss_row, *small = _small_step(
        [loss_p.reshape(-1, 128), dg1_p.reshape(-1, D_MODEL), dg2_p.reshape(-1, D_MODEL), dgq, dgk, dsinks],
        [norm_mix_gain, norm_ffn_gain, q_norm_gain, k_norm_gain, attn_sinks,
         m_norm_mix_gain, m_norm_ffn_gain, m_q_norm_gain, m_k_norm_gain, m_attn_sinks,
         v_norm_mix_gain, v_norm_ffn_gain, v_q_norm_gain, v_k_norm_gain, v_attn_sinks])
    loss = loss_row[0, 0]

    def leaves(i):
        b = [big[n][i][None] for n in ("w_in", "wba", "wbr", "wout", "wg", "wu", "wd")]
        s1, s2, sq, sk, ss = small[5 * i:5 * i + 5]
        return [s1, b[0], sq, sk, ss, b[1], b[2], b[3], s2, b[4], b[5], b[6]]

    return (loss, grad_x[None], *leaves(0), *leaves(1), *leaves(2), *leaves(3))
```

```python
import jax
import jax.numpy as jnp
from jax import lax
from jax.experimental import pallas as pl
from jax.experimental.pallas import tpu as pltpu

F32 = jnp.float32
BF16 = jnp.bfloat16
MESH = pl.DeviceIdType.MESH

D_MODEL = 1024
EPS = 1e-6
HEAD_DIM = 64
N_Q_HEADS = 16
N_KV_HEADS = 2
GROUP = 8
BLOCK = 128
RET_HEADS = 4
RET_QK_DIM = 256
RET_V_DIM = 512
RET_CHUNK = 128
RET_ROT_BASE = 10000.0
D_FF = 2816
ATT_Q = N_Q_HEADS * HEAD_DIM
ATT_KV = N_KV_HEADS * HEAD_DIM
RET_QK = RET_HEADS * RET_QK_DIM
RET_V = RET_HEADS * RET_V_DIM
D_IN = 9472
ADAM_LR = 0.001
ADAM_B1 = 0.9
ADAM_B2 = 0.999
ADAM_EPS = 1e-08
ADAM_WD = 0.01
ADAM_STEP = 10

N_CHIPS = 4
N_DEV = 8
VMEM_LIMIT_BYTES = 60 * 1024 * 1024

P_QA = (0, 1024)
P_KVA = (1024, 256)
P_QR = (1280, 1024)
P_KR = (2304, 1024)
P_VR = (3328, 2048)
P_GR = (5376, 2048)
P_ZA = (7424, 1024)
P_ZR = (8448, 1024)

W_IN_SH = D_IN // N_CHIPS
FF_SH = D_FF // N_CHIPS

SMALL_ROWS = 8


def _dot(a, b):
    return jnp.dot(a, b, preferred_element_type=F32)


def _dot_nt(a, b):
    return lax.dot_general(a, b, (((1,), (1,)), ((), ())), preferred_element_type=F32)


def _dot_tn(a, b):
    return lax.dot_general(a, b, (((0,), (0,)), ((), ())), preferred_element_type=F32)


def _bf(x):
    return x.astype(BF16)


def _rms_stats(x):
    r = lax.rsqrt(jnp.mean(x * x, axis=-1, keepdims=True) + EPS)
    return r, x * r


def _rms_bwd(dy, xhat, r, gain):
    u = dy * gain
    dx = r * (u - xhat * jnp.mean(u * xhat, axis=-1, keepdims=True))
    return dx, dy * xhat


def _params(sem):
    return pltpu.CompilerParams(dimension_semantics=sem, vmem_limit_bytes=VMEM_LIMIT_BYTES)


_ANY = pl.BlockSpec(memory_space=pl.ANY)


class _Exchange:
    def __init__(self, ins, outs, n_sems, phases):
        self.ins, self.outs, self.n_sems, self.phases = list(ins), list(outs), n_sems, list(phases)


def _merge_exchanges(a, b):
    na_i, na_o, shift = len(a.ins), len(a.outs), a.n_sems

    def first(fn):
        return lambda i, o, s, r, base: fn(i[:na_i], o[:na_o], s, r, base)

    def second(fn):
        return lambda i, o, s, r, base: fn(i[na_i:], o[na_o:], s, r, base + shift)

    phases = [(f, first(fn)) for f, fn in a.phases] + [(f, second(fn)) for f, fn in b.phases]
    return _Exchange(a.ins + b.ins, a.outs + b.outs, a.n_sems + b.n_sems, sorted(phases, key=lambda p: p[0]))


def _pallas(kern, *, grid, in_specs, out_specs, out_shape, args, name, scratch=(), exchange=None, aliases=None):
    aliases = aliases or {}
    if exchange is None:
        return pl.pallas_call(
            kern, grid=grid, in_specs=in_specs, out_specs=out_specs, out_shape=out_shape, name=name,
            scratch_shapes=list(scratch), input_output_aliases=aliases,
            compiler_params=_params(("arbitrary",) * len(grid)))(*args)
    n_in, n_out, n_sc = len(in_specs), len(out_specs), len(scratch)
    n_xi, n_xo = len(exchange.ins), len(exchange.outs)
    n_steps = 1
    for g in grid:
        n_steps *= g

    def wrapped(*refs):
        ins, refs = refs[:n_in], refs[n_in:]
        x_ins, refs = refs[:n_xi], refs[n_xi:]
        outs, refs = refs[:n_out], refs[n_out:]
        x_outs, refs = refs[:n_xo], refs[n_xo:]
        scr, (send_sems, recv_sems) = refs[:n_sc], refs[n_sc:]
        step = pl.program_id(0)
        for d in range(1, len(grid)):
            step = step * grid[d] + pl.program_id(d)
        for frac, fn in exchange.phases:
            at = min(int(frac * n_steps), n_steps - 1)

            @pl.when(step == at)
            def _(fn=fn):
                fn(x_ins, x_outs, send_sems, recv_sems, 0)

        kern(*ins, *outs, *scr)

    sems = [pltpu.SemaphoreType.DMA((exchange.n_sems,)), pltpu.SemaphoreType.DMA((exchange.n_sems,))]
    return pl.pallas_call(
        wrapped, grid=grid, in_specs=list(in_specs) + [_ANY] * n_xi, out_specs=list(out_specs) + [_ANY] * n_xo,
        out_shape=list(out_shape) + exchange.outs, name=name, scratch_shapes=list(scratch) + sems,
        input_output_aliases=aliases, compiler_params=_params(("arbitrary",) * len(grid)))(*args, *exchange.ins)


def _run_exchange(exchange, name):
    def body(*refs):
        n_i, n_o = len(exchange.ins), len(exchange.outs)
        for _, fn in exchange.phases:
            fn(refs[:n_i], refs[n_i:n_i + n_o], refs[n_i + n_o], refs[n_i + n_o + 1], 0)

    sems = [pltpu.SemaphoreType.DMA((exchange.n_sems,)), pltpu.SemaphoreType.DMA((exchange.n_sems,))]
    return pl.pallas_call(body, in_specs=[_ANY] * len(exchange.ins), out_specs=[_ANY] * len(exchange.outs),
                          out_shape=exchange.outs, scratch_shapes=sems, name=name)(*exchange.ins)


def _fused(parts, *, grid, name, exchange=None):
    counts = [(len(p["in_specs"]), len(p["out_specs"]), len(p["scratch"])) for p in parts]
    n_in, n_out = sum(c[0] for c in counts), sum(c[1] for c in counts)

    def kern(*refs):
        ins, outs, scr = refs[:n_in], refs[n_in:n_in + n_out], refs[n_in + n_out:]
        i0 = o0 = s0 = 0
        for p, (ni, no, ns) in zip(parts, counts):
            p["kern"](*ins[i0:i0 + ni], *outs[o0:o0 + no], *scr[s0:s0 + ns])
            i0, o0, s0 = i0 + ni, o0 + no, s0 + ns

    cat = lambda key: [a for p in parts for a in p[key]]
    return _pallas(kern, grid=grid, in_specs=cat("in_specs"), out_specs=cat("out_specs"), out_shape=cat("out_shape"),
                   scratch=cat("scratch"), args=cat("args"), name=name, exchange=exchange)


def _row_call(body, *, tm, row_ins, res_ins, row_outs, part_outs=(), name, exchange=None):
    t = row_ins[0].shape[0]
    n_tiles = t // tm
    in_specs = [pl.BlockSpec((tm, a.shape[1]), lambda i: (i, 0)) for a in row_ins]
    in_specs += [pl.BlockSpec(a.shape, lambda i: (0, 0), pipeline_mode=pl.Buffered(1)) for a in res_ins]
    out_shape = [jax.ShapeDtypeStruct((t, w), dt) for (w, dt) in row_outs]
    out_shape += [jax.ShapeDtypeStruct((n_tiles, 1, w), F32) for w in part_outs]
    out_specs = [pl.BlockSpec((tm, w), lambda i: (i, 0)) for (w, _) in row_outs]
    out_specs += [pl.BlockSpec((1, 1, w), lambda i: (i, 0, 0)) for w in part_outs]
    n_ri, n_re, n_ro = len(row_ins), len(res_ins), len(row_outs)

    def kern(*refs):
        body(refs[:n_ri], refs[n_ri:n_ri + n_re], refs[n_ri + n_re:n_ri + n_re + n_ro], refs[n_ri + n_re + n_ro:])

    return _pallas(kern, grid=(n_tiles,), in_specs=in_specs, out_specs=out_specs, out_shape=out_shape,
                   args=[*row_ins, *res_ins], name=name, exchange=exchange)


def _proj_fwd(x, g1, w_in_t, exchange):
    pieces = ((P_QA, F32), (P_KVA, F32), (P_QR, F32), (P_KR, F32), (P_VR, BF16), (P_GR, F32), (P_ZA, F32), (P_ZR, F32))

    def body(ri, re, ro, po):
        x_t = ri[0][...]
        r, xhat = _rms_stats(x_t)
        hb = _bf(xhat * re[0][...])
        ro[0][...] = hb
        for k, ((off, w), dt) in enumerate(pieces):
            ro[1 + k][...] = _dot_nt(hb, re[1][off:off + w, :]).astype(dt)

    outs = [(D_MODEL, BF16)] + [(w, dt) for ((_, w), dt) in pieces]
    return _row_call(body, tm=256, row_ins=[x], res_ins=[g1, w_in_t], row_outs=outs, name="proj_fwd",
                     exchange=exchange)


def _mix_fwd(attn, ret, z_a, z_r, x, wba, wbr, wout, g2):
    def body(ri, re, ro, po):
        ba = _dot(ri[0][...], re[0][...])
        br = _dot(ri[1][...], re[1][...])
        sa = jax.nn.sigmoid(ri[2][...])
        sr = jax.nn.sigmoid(ri[3][...])
        mb = _bf(sa * ba + sr * br)
        x1 = ri[4][...] + _dot(mb, re[2][...])
        r, xhat = _rms_stats(x1)
        ro[0][...] = sa
        ro[1][...] = sr
        ro[2][...] = ba * (sa * (1.0 - sa))
        ro[3][...] = br * (sr * (1.0 - sr))
        ro[4][...] = mb
        ro[5][...] = x1
        ro[6][...] = _bf(xhat * re[3][...])

    outs = [(D_MODEL, F32)] * 4 + [(D_MODEL, BF16), (D_MODEL, F32), (D_MODEL, BF16)]
    return _row_call(body, tm=512, row_ins=[attn, ret, z_a, z_r, x], res_ins=[wba, wbr, wout, g2], row_outs=outs,
                     name="mix_fwd")


def _ffn_fwd_bwd(h2, x1, target, wg_t, wu_t, wd, g2):
    def body(ri, re, ro, po):
        h2_t = ri[0][...]
        x1_t = ri[1][...]
        gate = _dot_nt(h2_t, re[0][...])
        up = _dot_nt(h2_t, re[1][...])
        sg = jax.nn.sigmoid(gate)
        sl = gate * sg
        actb = _bf(sl * up)
        ro[0][...] = actb
        y = x1_t + _dot(actb, re[2][...])
        e = y - ri[2][...]
        po[0][0] = jnp.broadcast_to(0.5 * jnp.sum(jnp.sum(e * e, axis=1, keepdims=True), axis=0, keepdims=True)
                                    * (1.0 / D_MODEL), (1, 128))
        dy = e * (1.0 / D_MODEL)
        dyb = _bf(dy)
        ro[3][...] = dyb
        dact = _dot_nt(dyb, re[2][...])
        dupb = _bf(dact * sl)
        dgateb = _bf(dact * up * (sg * (1.0 + gate * (1.0 - sg))))
        ro[1][...] = dgateb
        ro[2][...] = dupb
        dh2 = _dot(dgateb, re[0][...]) + _dot(dupb, re[1][...])
        r, xhat = _rms_stats(x1_t)
        dxn, dgain = _rms_bwd(dh2, xhat, r, re[3][...])
        dx1 = dy + dxn
        ro[4][...] = dx1
        ro[5][...] = _bf(dx1)
        po[1][0] = jnp.sum(dgain, axis=0, keepdims=True)

    outs = [(D_FF, BF16), (D_FF, BF16), (D_FF, BF16), (D_MODEL, BF16), (D_MODEL, F32), (D_MODEL, BF16)]
    return _row_call(body, tm=256, row_ins=[h2, x1, target], res_ins=[wg_t, wu_t, wd, g2], row_outs=outs,
                     part_outs=(128, D_MODEL), name="ffn_fwd_bwd")


def _mix_bwd(dx1b, s_a, s_r, dz_a_factor, dz_r_factor, g_r, ret_normed, ret_rrms, wout, wba, wbr, exchange):
    def body(ri, re, ro, po):
        dm = _dot_nt(ri[0][...], re[0][...])
        dbab = _bf(ri[1][...] * dm)
        dbrb = _bf(ri[2][...] * dm)
        ro[0][...] = dbab
        ro[1][...] = dbrb
        ro[4][:, RET_V:RET_V + D_MODEL] = _bf(dm * ri[3][...])
        ro[4][:, RET_V + D_MODEL:RET_V + 2 * D_MODEL] = _bf(dm * ri[4][...])
        ro[2][...] = _bf(_dot_nt(dbab, re[1][...]))
        dret = _dot_nt(dbrb, re[2][...])
        for h in range(RET_HEADS):
            cols = slice(h * RET_V_DIM, (h + 1) * RET_V_DIM)
            g = ri[5][:, cols]
            rn = ri[6][:, cols]
            r = ri[7][:, h:h + 1]
            sg = jax.nn.sigmoid(g)
            dret_h = dret[:, cols]
            d_rn = dret_h * (g * sg)
            ro[4][:, cols] = _bf(dret_h * rn * (sg * (1.0 + g * (1.0 - sg))))
            ro[3][:, cols] = r * (d_rn - rn * jnp.mean(d_rn * rn, axis=-1, keepdims=True))

    outs = [(D_MODEL, BF16), (D_MODEL, BF16), (ATT_Q, BF16), (RET_V, F32), (RET_V + 2 * D_MODEL, BF16)]
    return _row_call(body, tm=256, row_ins=[dx1b, s_a, s_r, dz_a_factor, dz_r_factor, g_r, ret_normed, ret_rrms],
                     res_ins=[wout, wba, wbr], row_outs=outs, name="mix_bwd", exchange=exchange)


def _proj_bwd(d_pieces, x, dx1, w_in_t, g1, exchange):
    widths = [p.shape[1] for p in d_pieces]
    groups = [(sum(widths[:k]), w) for k, w in enumerate(widths)]
    n_p = len(groups)

    def body(ri, re, ro, po):
        dh = None
        for k, (off, w) in enumerate(groups):
            term = _dot(ri[k][...], re[0][off:off + w, :])
            dh = term if dh is None else dh + term
        r, xhat = _rms_stats(ri[n_p][...])
        dxn, dgain = _rms_bwd(dh, xhat, r, re[1][...])
        ro[0][...] = ri[n_p + 1][...] + dxn
        po[0][0] = jnp.sum(dgain, axis=0, keepdims=True)

    return _row_call(body, tm=512, row_ins=[*d_pieces, x, dx1], res_ins=[w_in_t, g1], row_outs=[(D_MODEL, F32)],
                     part_outs=(D_MODEL,), name="proj_bwd", exchange=exchange)


def _dw(a, b, *, tm, place, buf, name, exchange=None):
    t, m = a.shape
    n = b.shape[1]
    tk = min(2048, t)
    n_i, n_k = m // tm, t // tk
    fresh = isinstance(buf, jax.ShapeDtypeStruct)
    n_copies = len(place(0))

    def kern(a_ref, b_ref, *rest):
        out_ref, acc, sems = rest[-3:]
        i, k = pl.program_id(0), pl.program_id(1)
        part = _dot_tn(a_ref[...], b_ref[...])

        @pl.when(k == 0)
        def _():
            acc[i] = part

        @pl.when(k > 0)
        def _():
            acc[i] += part

        def copies(tile):
            return [pltpu.make_async_copy(acc.at[tile, pl.ds(r0, rows), :], out_ref.at[idx], sems.at[tile * n_copies + c])
                    for c, (r0, rows, idx) in enumerate(place(tile))]

        for tile in range(n_i):
            @pl.when((i == tile) & (k == n_k - 1))
            def _(tile=tile):
                for cp in copies(tile):
                    cp.start()

        @pl.when((i == n_i - 1) & (k == n_k - 1))
        def _():
            for tile in range(n_i):
                for cp in copies(tile):
                    cp.wait()

    in_specs = [pl.BlockSpec((tk, tm), lambda i, k: (k, i)), pl.BlockSpec((tk, n), lambda i, k: (k, 0))]
    shape = buf if fresh else jax.ShapeDtypeStruct(buf.shape, buf.dtype)
    return _pallas(
        kern, grid=(n_i, n_k), in_specs=in_specs + ([] if fresh else [_ANY]), out_specs=[_ANY], out_shape=[shape],
        scratch=[pltpu.VMEM((n_i, tm, n), F32), pltpu.SemaphoreType.DMA((n_i * n_copies,))],
        args=[a, b] + ([] if fresh else [buf]), aliases=None if fresh else {2: 0}, name=name, exchange=exchange)


def _heads_to_lanes(x3):
    return jnp.concatenate([x3[g] for g in range(GROUP)], axis=1)


def _lanes_to_heads(xt):
    return jnp.concatenate([xt[:, g * BLOCK:(g + 1) * BLOCK] for g in range(GROUP)], axis=0)


def _attn_queries(kvh, q_ref, gq_col):
    cols = slice(kvh * GROUP * HEAD_DIM, (kvh + 1) * GROUP * HEAD_DIM)
    q3 = q_ref[:, cols].T.reshape(GROUP, HEAD_DIM, BLOCK)
    rq = lax.rsqrt(jnp.mean(q3 * q3, axis=1, keepdims=True) + EPS)
    qhat = q3 * rq
    return qhat, rq, _heads_to_lanes(_bf(qhat * (gq_col * (HEAD_DIM ** -0.5))))


def _from_prev():
    j = lax.broadcasted_iota(jnp.int32, (BLOCK, GROUP * BLOCK), 0)
    i = lax.broadcasted_iota(jnp.int32, (BLOCK, GROUP * BLOCK), 1) & (BLOCK - 1)
    return j > i


def _attn_probs(n, kvh, qts, kvp_ref, kvc_ref, gk, sink_ref):
    kcols = slice(kvh * HEAD_DIM, (kvh + 1) * HEAD_DIM)
    k = jnp.concatenate([kvp_ref[:, kcols], kvc_ref[:, kcols]], axis=0)
    rk, khat = _rms_stats(k)
    st = _dot(_bf(khat * gk), qts)
    f = jnp.where(_from_prev(), jnp.where(n > 0, st[0:BLOCK], -1e30), st[BLOCK:2 * BLOCK])
    sink = jnp.concatenate([jnp.broadcast_to(sink_ref[0:1, kvh * GROUP + g:kvh * GROUP + g + 1], (1, BLOCK))
                            for g in range(GROUP)], axis=1)
    m = jnp.maximum(jnp.max(f, axis=0, keepdims=True), sink)
    e = jnp.exp(f - m)
    es = jnp.exp(sink - m)
    inv = 1.0 / (jnp.sum(e, axis=0, keepdims=True) + es)
    return e * inv, es * inv


def _unfold(from_prev, xf):
    return _bf(jnp.concatenate([jnp.where(from_prev, xf, 0.0), jnp.where(from_prev, 0.0, xf)], axis=0))


def _attn_fwd(q_a, kv_a, gq_col, gk, sinks):
    t = q_a.shape[0]
    nb = t // BLOCK

    def kern(q_ref, kvp_ref, kvc_ref, gq_ref, gk_ref, sink_ref, o_ref, pf_ref, ps_ref):
        n = pl.program_id(0)
        kvt = jnp.concatenate([kvp_ref[...].T, kvc_ref[...].T], axis=1)
        for kvh in range(N_KV_HEADS):
            _, _, qts = _attn_queries(kvh, q_ref, gq_ref[...])
            pf, psink = _attn_probs(n, kvh, qts, kvp_ref, kvc_ref, gk_ref[...], sink_ref)
            lanes = slice(kvh * GROUP * BLOCK, (kvh + 1) * GROUP * BLOCK)
            pf_ref[:, lanes] = pf
            ps_ref[:, lanes] = psink
            vt = _bf(kvt[ATT_KV + kvh * HEAD_DIM:ATT_KV + (kvh + 1) * HEAD_DIM, :])
            out_t = _dot(vt, _unfold(_from_prev(), pf))
            cols = slice(kvh * GROUP * HEAD_DIM, (kvh + 1) * GROUP * HEAD_DIM)
            o_ref[:, cols] = _bf(_lanes_to_heads(out_t).T)

    small = lambda a: pl.BlockSpec(a.shape, lambda n: (0, 0))
    folded = N_KV_HEADS * GROUP * BLOCK
    return dict(
        kern=kern,
        in_specs=[pl.BlockSpec((BLOCK, ATT_Q), lambda n: (n, 0)),
                  pl.BlockSpec((BLOCK, 2 * ATT_KV), lambda n: (jnp.maximum(n - 1, 0), 0)),
                  pl.BlockSpec((BLOCK, 2 * ATT_KV), lambda n: (n, 0)),
                  small(gq_col), small(gk), small(sinks)],
        out_specs=[pl.BlockSpec((BLOCK, ATT_Q), lambda n: (n, 0)), pl.BlockSpec((BLOCK, folded), lambda n: (n, 0)),
                   pl.BlockSpec((None, 1, folded), lambda n: (n, 0, 0))],
        out_shape=[jax.ShapeDtypeStruct((t, ATT_Q), BF16), jax.ShapeDtypeStruct((t, folded), F32),
                   jax.ShapeDtypeStruct((nb, 1, folded), F32)],
        scratch=[], args=[q_a, kv_a, kv_a, gq_col, gk, sinks])


def _attn_bwd(q_a, kv_a, d_attn, probs, sink_probs, gq_col, gk, gk_col, exchange):
    t = q_a.shape[0]
    nb = t // BLOCK

    def kern(q_ref, kvp_ref, kvc_ref, do_ref, pf_ref, ps_ref, gq_ref, gk_ref, gkc_ref,
             dq_ref, dkv_ref, dgq_ref, dgk_ref, dsink_ref, band_k, band_v, carry_k, carry_v):
        n = pl.program_id(0)
        gq_v = gq_ref[...]
        gk_v = gk_ref[...]

        @pl.when(n == 0)
        def _():
            carry_k[...] = jnp.zeros_like(carry_k)
            carry_v[...] = jnp.zeros_like(carry_v)
            dgq_ref[...] = jnp.zeros_like(dgq_ref)
            dgk_ref[...] = jnp.zeros_like(dgk_ref)
            dsink_ref[...] = jnp.zeros_like(dsink_ref)

        @pl.when(n == nb)
        def _():
            band_k[...] = jnp.zeros_like(band_k)
            band_v[...] = jnp.zeros_like(band_v)

        @pl.when(n < nb)
        def _():
            lane16 = lax.broadcasted_iota(jnp.int32, (1, N_Q_HEADS), 1)
            dsink = jnp.zeros((1, N_Q_HEADS), F32)
            dgq = jnp.zeros((HEAD_DIM, 1), F32)
            gk_col = gkc_ref[...]
            kvt = jnp.concatenate([kvp_ref[...].T, kvc_ref[...].T], axis=1)
            from_prev = _from_prev()
            for kvh in range(N_KV_HEADS):
                qhat, rq, qts = _attn_queries(kvh, q_ref, gq_v)
                lanes = slice(kvh * GROUP * BLOCK, (kvh + 1) * GROUP * BLOCK)
                pf = pf_ref[:, lanes]
                cols = slice(kvh * GROUP * HEAD_DIM, (kvh + 1) * GROUP * HEAD_DIM)
                vcols = slice(ATT_KV + kvh * HEAD_DIM, ATT_KV + (kvh + 1) * HEAD_DIM)
                dot = _heads_to_lanes(_bf(do_ref[:, cols].astype(F32).T.reshape(GROUP, HEAD_DIM, BLOCK)))
                vb = _bf(jnp.concatenate([kvp_ref[:, vcols], kvc_ref[:, vcols]], axis=0))
                dpt = _dot(vb, dot)
                dpf = jnp.where(from_prev, dpt[0:BLOCK], dpt[BLOCK:2 * BLOCK])
                delta = jnp.sum(pf * dpf, axis=0, keepdims=True)
                dst = _unfold(from_prev, pf * (dpf - delta))
                dsk = ps_ref[:, lanes] * delta
                for g in range(GROUP):
                    tot = jnp.sum(dsk[:, g * BLOCK:(g + 1) * BLOCK], axis=1, keepdims=True)
                    dsink = dsink - jnp.where(lane16 == kvh * GROUP + g, tot, 0.0)
                kt = kvt[kvh * HEAD_DIM:(kvh + 1) * HEAD_DIM, :]
                knt = _bf(kt * lax.rsqrt(jnp.mean(kt * kt, axis=0, keepdims=True) + EPS) * gk_col)
                dqn = (_dot(knt, dst) * (HEAD_DIM ** -0.5))
                band_k[kvh] = _dot_nt(dst, qts)
                band_v[kvh] = _dot_nt(_unfold(from_prev, pf), dot)
                dqn3 = _lanes_to_heads(dqn).reshape(GROUP, HEAD_DIM, BLOCK)
                u = dqn3 * gq_v
                dq3 = rq * (u - qhat * jnp.mean(u * qhat, axis=1, keepdims=True))
                dgq = dgq + jnp.sum(jnp.sum(dqn3 * qhat, axis=0), axis=1, keepdims=True)
                dq_ref[:, cols] = _bf(dq3.reshape(GROUP * HEAD_DIM, BLOCK).T)
            dsink_ref[...] += dsink
            dgq_ref[...] += dgq

        dgk = jnp.zeros((1, HEAD_DIM), F32)
        for kvh in range(N_KV_HEADS):
            kcols = slice(kvh * HEAD_DIM, (kvh + 1) * HEAD_DIM)
            vcols = slice(ATT_KV + kvh * HEAD_DIM, ATT_KV + (kvh + 1) * HEAD_DIM)
            dkn = carry_k[kvh] + band_k[kvh, 0:BLOCK, :]
            dv = carry_v[kvh] + band_v[kvh, 0:BLOCK, :]
            rk, khat = _rms_stats(kvp_ref[:, kcols])
            dk, dgain = _rms_bwd(dkn, khat, rk, gk_v)
            dgk = dgk + jnp.sum(dgain, axis=0, keepdims=True)
            dkv_ref[:, kcols] = _bf(dk)
            dkv_ref[:, vcols] = _bf(dv)
            carry_k[kvh] = band_k[kvh, BLOCK:2 * BLOCK, :]
            carry_v[kvh] = band_v[kvh, BLOCK:2 * BLOCK, :]
        dgk_ref[...] += dgk

    small = lambda a: pl.BlockSpec(a.shape, lambda n: (0, 0))
    last = nb - 1
    return _pallas(
        kern, grid=(nb + 1,),
        in_specs=[pl.BlockSpec((BLOCK, ATT_Q), lambda n: (jnp.minimum(n, last), 0)),
                  pl.BlockSpec((BLOCK, 2 * ATT_KV), lambda n: (jnp.maximum(n - 1, 0), 0)),
                  pl.BlockSpec((BLOCK, 2 * ATT_KV), lambda n: (jnp.minimum(n, last), 0)),
                  pl.BlockSpec((BLOCK, ATT_Q), lambda n: (jnp.minimum(n, last), 0)),
                  pl.BlockSpec((BLOCK, probs.shape[1]), lambda n: (jnp.minimum(n, last), 0)),
                  pl.BlockSpec((None, 1, probs.shape[1]), lambda n: (jnp.minimum(n, last), 0, 0)),
                  small(gq_col), small(gk), small(gk_col)],
        out_specs=[pl.BlockSpec((BLOCK, ATT_Q), lambda n: (jnp.minimum(n, last), 0)),
                   pl.BlockSpec((BLOCK, 2 * ATT_KV), lambda n: (jnp.maximum(n - 1, 0), 0)),
                   pl.BlockSpec((HEAD_DIM, 1), lambda n: (0, 0)),
                   pl.BlockSpec((1, HEAD_DIM), lambda n: (0, 0)),
                   pl.BlockSpec((1, N_Q_HEADS), lambda n: (0, 0))],
        out_shape=[jax.ShapeDtypeStruct((t, ATT_Q), BF16), jax.ShapeDtypeStruct((t, 2 * ATT_KV), BF16),
                   jax.ShapeDtypeStruct((HEAD_DIM, 1), F32), jax.ShapeDtypeStruct((1, HEAD_DIM), F32),
                   jax.ShapeDtypeStruct((1, N_Q_HEADS), F32)],
        scratch=[pltpu.VMEM((N_KV_HEADS, 2 * BLOCK, HEAD_DIM), F32),
                 pltpu.VMEM((N_KV_HEADS, 2 * BLOCK, HEAD_DIM), F32),
                 pltpu.VMEM((N_KV_HEADS, BLOCK, HEAD_DIM), F32),
                 pltpu.VMEM((N_KV_HEADS, BLOCK, HEAD_DIM), F32)],
        args=[q_a, kv_a, kv_a, d_attn, probs, sink_probs, gq_col, gk, gk_col], name="attn_bwd", exchange=exchange)


def _ret_tables(t, exchange):
    theta = 1.0 / (RET_ROT_BASE ** jnp.linspace(0.0, 1.0, RET_QK_DIM // 2, dtype=F32))
    theta2 = jnp.repeat(theta, 2)[None, :]
    sign = jnp.tile(jnp.array([-1.0, 1.0], F32), RET_QK_DIM // 2)[None, :]

    def kern(theta_ref, sign_ref, cos_ref, sin_ref):
        first = pl.program_id(0) * RET_CHUNK
        pos = (first + lax.broadcasted_iota(jnp.int32, (RET_CHUNK, RET_QK_DIM), 0)).astype(F32)
        ang = pos * theta_ref[...]
        cos_ref[...] = jnp.cos(ang)
        sin_ref[...] = jnp.sin(ang) * sign_ref[...]

    row = pl.BlockSpec((1, RET_QK_DIM), lambda n: (0, 0))
    blk = pl.BlockSpec((RET_CHUNK, RET_QK_DIM), lambda n: (n, 0))
    cos, sin_s, *got = _pallas(kern, grid=(t // RET_CHUNK,), in_specs=[row, row], out_specs=[blk, blk],
                               out_shape=[jax.ShapeDtypeStruct((t, RET_QK_DIM), F32)] * 2, args=[theta2, sign],
                               name="position_tables", exchange=exchange)
    log_gamma = jnp.log(1.0 - 2.0 ** (-5.0 - jnp.arange(RET_HEADS, dtype=F32)))
    i = jnp.arange(RET_CHUNK, dtype=F32)
    diff = i[:, None] - i[None, :]
    causal = diff >= 0
    decay = jnp.where(causal[None], jnp.exp(jnp.where(causal, diff, 0.0)[None] * log_gamma[:, None, None]), 0.0)
    xi = jnp.exp((i + 1.0)[None, :] * log_gamma[:, None])[:, :, None]
    zeta = jnp.exp((RET_CHUNK - 1.0 - i)[None, :] * log_gamma[:, None])[:, :, None]
    gch = jnp.broadcast_to(jnp.exp(RET_CHUNK * log_gamma)[:, None, None], (RET_HEADS, 1, 128))
    return (cos, sin_s, decay, xi, zeta, gch), got


def _swap_pairs(x):
    lane = lax.broadcasted_iota(jnp.int32, x.shape, 1)
    return jnp.where((lane & 1) == 0, pltpu.roll(x, RET_QK_DIM - 1, 1), pltpu.roll(x, 1, 1))


def _rotate(x, cos, sin_s):
    return x * cos + _swap_pairs(x) * sin_s


def _rotate_bwd(dy, cos, sin_s):
    return dy * cos + _swap_pairs(dy * sin_s)


def _ret_specs(order):
    qk = pl.BlockSpec((RET_CHUNK, RET_QK), lambda j: (order(j), 0))
    v = pl.BlockSpec((RET_CHUNK, RET_V), lambda j: (order(j), 0))
    dec = pl.BlockSpec((RET_HEADS, RET_CHUNK, RET_CHUNK), lambda j: (0, 0, 0))
    col = pl.BlockSpec((RET_HEADS, RET_CHUNK, 1), lambda j: (0, 0, 0))
    gch = pl.BlockSpec((RET_HEADS, 1, 128), lambda j: (0, 0, 0))
    st = pl.BlockSpec((RET_HEADS, None, RET_QK_DIM, RET_V_DIM), lambda j: (0, order(j), 0, 0))
    pos = pl.BlockSpec((RET_CHUNK, RET_QK_DIM), lambda j: (order(j), 0))
    return qk, v, dec, col, gch, st, pos


def _ret_fwd(q_r, k_r, v_r, g_r, tables):
    t = q_r.shape[0]
    nc = t // RET_CHUNK
    cos, sin_s, decay, xi, zeta, gch = tables

    def kern(q_ref, k_ref, v_ref, g_ref, cos_ref, sin_ref, dec_ref, xi_ref, zeta_ref, gch_ref,
             rn_ref, r_ref, ret_ref, st_ref, state):
        @pl.when(pl.program_id(0) == 0)
        def _():
            state[...] = jnp.zeros_like(state)

        cos_t = cos_ref[...]
        sin_t = sin_ref[...]
        for h in range(RET_HEADS):
            qc = slice(h * RET_QK_DIM, (h + 1) * RET_QK_DIM)
            vc = slice(h * RET_V_DIM, (h + 1) * RET_V_DIM)
            qs = _bf(_rotate(q_ref[:, qc], cos_t, sin_t))
            ks = _rotate(k_ref[:, qc] * (RET_QK_DIM ** -0.5), cos_t, sin_t)
            vb = v_ref[:, vc]
            s_old = state[h]
            sb = _bf(s_old)
            st_ref[h] = sb
            inner = _dot_nt(qs, _bf(ks)) * dec_ref[h]
            out = _dot(_bf(inner), vb) + _dot(qs, sb) * xi_ref[h]
            state[h] = gch_ref[h, :, 0:1] * s_old + _dot_tn(_bf(ks * zeta_ref[h]), vb)
            r, rn = _rms_stats(out)
            rn_ref[:, vc] = rn
            r_ref[:, h:h + 1] = r
            g = g_ref[:, vc]
            ret_ref[:, vc] = _bf(g * jax.nn.sigmoid(g) * rn)

    qk, v, dec, col, gsp, st, pos = _ret_specs(lambda j: j)
    return dict(
        kern=kern,
        in_specs=[qk, qk, v, v, pos, pos, dec, col, col, gsp],
        out_specs=[v, pl.BlockSpec((RET_CHUNK, RET_HEADS), lambda j: (j, 0)), v, st],
        out_shape=[jax.ShapeDtypeStruct((t, RET_V), F32), jax.ShapeDtypeStruct((t, RET_HEADS), F32),
                   jax.ShapeDtypeStruct((t, RET_V), BF16),
                   jax.ShapeDtypeStruct((RET_HEADS, nc, RET_QK_DIM, RET_V_DIM), BF16)],
        scratch=[pltpu.VMEM((RET_HEADS, RET_QK_DIM, RET_V_DIM), F32)],
        args=[q_r, k_r, v_r, g_r, cos, sin_s, decay, xi, zeta, gch])


def _ret_bwd(q_r, k_r, v_r, d_o, states, tables, exchange):
    t = q_r.shape[0]
    nc = t // RET_CHUNK
    cos, sin_s, decay, xi, zeta, gch = tables

    def kern(q_ref, k_ref, v_ref, do_ref, st_ref, cos_ref, sin_ref, dec_ref, xi_ref, zeta_ref, gch_ref,
             d_ref, dstate):
        dq_ref, dk_ref = d_ref.at[:, 0:RET_QK], d_ref.at[:, RET_QK:2 * RET_QK]
        dv_ref = d_ref.at[:, 2 * RET_QK:2 * RET_QK + RET_V]

        @pl.when(pl.program_id(0) == 0)
        def _():
            dstate[...] = jnp.zeros_like(dstate)

        cos_t = cos_ref[...]
        sin_t = sin_ref[...]
        scale = RET_QK_DIM ** -0.5
        for h in range(RET_HEADS):
            qc = slice(h * RET_QK_DIM, (h + 1) * RET_QK_DIM)
            vc = slice(h * RET_V_DIM, (h + 1) * RET_V_DIM)
            qs = _bf(_rotate(q_ref[:, qc], cos_t, sin_t))
            ks = _rotate(k_ref[:, qc] * scale, cos_t, sin_t)
            ksb = _bf(ks)
            vb = v_ref[:, vc]
            d_o_t = do_ref[:, vc]
            dob = _bf(d_o_t)
            doxb = _bf(d_o_t * xi_ref[h])
            dec = dec_ref[h]
            ds_old = dstate[h]
            dsb = _bf(ds_old)
            pb = _bf(_dot_nt(qs, ksb) * dec)
            dpb = _bf(_dot_nt(dob, vb) * dec)
            dqs = _dot(dpb, ksb) + _dot_nt(doxb, st_ref[h])
            dks = _dot_tn(dpb, qs) + _dot_nt(vb, dsb) * zeta_ref[h]
            dv_ref[:, vc] = _bf(_dot_tn(pb, dob) + _dot(_bf(ks * zeta_ref[h]), dsb))
            dstate[h] = gch_ref[h, :, 0:1] * ds_old + _dot_tn(qs, doxb)
            dq_ref[:, qc] = _bf(_rotate_bwd(dqs, cos_t, sin_t))
            dk_ref[:, qc] = _bf(_rotate_bwd(dks, cos_t, sin_t) * scale)

    qk, v, dec, col, gsp, st, pos = _ret_specs(lambda j: nc - 1 - j)
    return _pallas(
        kern, grid=(nc,),
        in_specs=[qk, qk, v, v, st, pos, pos, dec, col, col, gsp],
        out_specs=[pl.BlockSpec((RET_CHUNK, 2 * RET_QK + RET_V), lambda j: (nc - 1 - j, 0))],
        out_shape=[jax.ShapeDtypeStruct((t, 2 * RET_QK + RET_V), BF16)],
        scratch=[pltpu.VMEM((RET_HEADS, RET_QK_DIM, RET_V_DIM), F32)],
        args=[q_r, k_r, v_r, d_o, states, cos, sin_s, decay, xi, zeta, gch], name="ret_bwd", exchange=exchange)


def _position():
    return lax.axis_index("x"), lax.axis_index("y"), lax.axis_index("c")


def _gather_exchange(owns, forward_at):
    n = len(owns)

    def copies(ins, outs, send_sems, recv_sems, base):
        x, y, c = _position()
        sibling = (x, y, 1 - c)
        chips = [(1 - x, y), (x, 1 - y), (1 - x, 1 - y)]
        my_chip = 2 * x + y

        def slab(a, chip, hf):
            half = owns[a].shape[0] // 2
            return outs[a].at[chip, pl.ds(hf * half, half), :]

        def copy(k, src, dst, to):
            return pltpu.make_async_remote_copy(src_ref=src, dst_ref=dst, send_sem=send_sems.at[base + k],
                                                recv_sem=recv_sems.at[base + k], device_id=to, device_id_type=MESH)

        first, passed, from_sibling = [], [], []
        for a in range(n):
            half = owns[a].shape[0] // 2
            for k, (cx, cy) in enumerate(chips):
                first.append(copy(6 * a + k, ins[a].at[pl.ds(c * half, half), :], slab(a, my_chip, c), (cx, cy, c)))
                landed = slab(a, 2 * cx + cy, c)
                passed.append(copy(6 * a + 3 + k, landed, landed, sibling))
                theirs = slab(a, 2 * cx + cy, 1 - c)
                from_sibling.append(copy(6 * a + 3 + k, theirs, theirs, sibling))
        return first, passed, from_sibling

    def start(*args):
        first, _, _ = copies(*args)
        for cp in first:
            cp.start()

    def forward(*args):
        first, passed, _ = copies(*args)
        for arrived, cp in zip(first, passed):
            arrived.wait_recv()
            cp.start()

    def finish(*args):
        first, passed, from_sibling = copies(*args)
        for cp in from_sibling:
            cp.wait_recv()
        for cp in first + passed:
            cp.wait_send()

    outs = [jax.ShapeDtypeStruct((N_CHIPS, *a.shape), a.dtype) for a in owns]
    return _Exchange(owns, outs, 6 * n, [(0.0, start), (forward_at, forward), (1.0, finish)])


def _symmetric_exchange(ins, outs, plan):
    n_sems = len(plan([None] * len(ins), [None] * len(outs), 0, 0, 0, dry=True))

    def copies(in_refs, out_refs, send_sems, recv_sems, base):
        x, y, c = _position()
        return [pltpu.make_async_remote_copy(src_ref=src, dst_ref=dst, send_sem=send_sems.at[base + k],
                                             recv_sem=recv_sems.at[base + k], device_id=dev, device_id_type=MESH)
                for k, (src, dst, dev) in enumerate(plan(in_refs, out_refs, x, y, c, dry=False))]

    def start(*args):
        for cp in copies(*args):
            cp.start()

    def finish(*args):
        for cp in copies(*args):
            cp.wait()

    return _Exchange(ins, outs, n_sems, [(0.0, start), (1.0, finish)])


def _pair_exchange(gs):
    def plan(in_refs, out_refs, x, y, c, dry):
        out = []
        for a, g in enumerate(gs):
            half = g.shape[1] // 2
            for k in range(N_CHIPS):
                out.append(None if dry else (in_refs[a].at[k, pl.ds((1 - c) * half, half), :], out_refs[a].at[k],
                                             (x, y, 1 - c)))
        return out

    outs = [jax.ShapeDtypeStruct((g.shape[0], g.shape[1] // 2, g.shape[2]), g.dtype) for g in gs]
    return _symmetric_exchange(gs, outs, plan)


def _pair_sum(g, from_sibling, c_arr, *, tile, name):
    n, rows, width = g.shape
    tiles = (rows // 2) // tile

    def kern(c_ref, g_ref, s_ref, o_ref):
        o_ref[...] = _bf(g_ref[...] + s_ref[...])

    return pl.pallas_call(
        kern,
        grid_spec=pltpu.PrefetchScalarGridSpec(
            num_scalar_prefetch=1, grid=(n, tiles),
            in_specs=[pl.BlockSpec((None, tile, width), lambda k, i, c: (k, c[0] * tiles + i, 0)),
                      pl.BlockSpec((None, tile, width), lambda k, i, c: (k, i, 0))],
            out_specs=pl.BlockSpec((None, tile, width), lambda k, i, c: (k, i, 0))),
        out_shape=jax.ShapeDtypeStruct((n, rows // 2, width), BF16), name=name,
        compiler_params=_params(("parallel", "parallel")),
    )(c_arr, g, from_sibling)


def _scatter_to_owners(hsums):
    def plan(in_refs, out_refs, x, y, c, dry):
        out = []
        for a in range(len(hsums)):
            for k, (cx, cy) in enumerate([(1 - x, y), (x, 1 - y), (1 - x, 1 - y)]):
                out.append(None if dry else (in_refs[a].at[2 * cx + cy], out_refs[a].at[k], (cx, cy, c)))
        return out

    outs = [jax.ShapeDtypeStruct((3, *h.shape[1:]), h.dtype) for h in hsums]
    return _symmetric_exchange(hsums, outs, plan)


def _sum_chips(hsum, parts, chip_arr, *, tile, name):
    n, half, width = parts.shape

    def kern(chip_ref, h_ref, p_ref, o_ref):
        acc = h_ref[...].astype(F32)
        for k in range(n):
            acc = acc + p_ref[k].astype(F32)
        o_ref[...] = acc

    return pl.pallas_call(
        kern,
        grid_spec=pltpu.PrefetchScalarGridSpec(
            num_scalar_prefetch=1, grid=(half // tile,),
            in_specs=[pl.BlockSpec((None, tile, width), lambda i, chip: (chip[0], i, 0)),
                      pl.BlockSpec((n, tile, width), lambda i, chip: (0, i, 0))],
            out_specs=pl.BlockSpec((tile, width), lambda i, chip: (i, 0))),
        out_shape=jax.ShapeDtypeStruct((half, width), F32), name=name,
        compiler_params=_params(("parallel",)),
    )(chip_arr, hsum, parts)


def _share_halves(fhalves):
    def plan(in_refs, out_refs, x, y, c, dry):
        return [None if dry else (in_refs[a], out_refs[a], (x, y, 1 - c)) for a in range(len(fhalves))]

    return _symmetric_exchange(fhalves, [jax.ShapeDtypeStruct(f.shape, f.dtype) for f in fhalves], plan)


def _adamw_math(w, g, m, v):
    m = ADAM_B1 * m + (1.0 - ADAM_B1) * g
    v = ADAM_B2 * v + (1.0 - ADAM_B2) * (g * g)
    m_hat = m / (1.0 - ADAM_B1 ** ADAM_STEP)
    v_hat = v / (1.0 - ADAM_B2 ** ADAM_STEP)
    delta = -ADAM_LR * (m_hat / (jnp.sqrt(v_hat) + ADAM_EPS) + ADAM_WD * w)
    return delta, m, v


def _adamw(mats, g_mine, g_other, c_arr, *, tile, name):
    width = g_mine.shape[1]
    tiles_per_half = g_mine.shape[0] // tile
    n_tiles = [w.shape[0] // tile for w, _, _, _ in mats]
    n_mats = len(mats)

    def kern(c_ref, *refs):
        ins, outs = refs[:5 * n_mats], refs[5 * n_mats:]
        for j, (_, _, _, row_off) in enumerate(mats):
            w_ref, gm_ref, go_ref, m_ref, v_ref = ins[5 * j:5 * j + 5]
            i = jnp.minimum(pl.program_id(0), n_tiles[j] - 1)
            in_my_half = ((row_off // tile + i) // tiles_per_half) == c_ref[0]
            g = jnp.where(in_my_half, gm_ref[...], go_ref[...])
            d, nm, nv = _adamw_math(w_ref[...], g, m_ref[...], v_ref[...])
            for out_ref, val in zip(outs[4 * j:4 * j + 4], (g, d, nm, nv)):
                out_ref[...] = val

    in_specs, out_specs, out_shape, args = [], [], [], []
    for (w, m, v, row_off), nt in zip(mats, n_tiles):
        full = pl.BlockSpec((tile, width), lambda i, c, nt=nt: (jnp.minimum(i, nt - 1), 0))
        half = pl.BlockSpec((tile, width), lambda i, c, nt=nt, first=row_off // tile:
                            ((first + jnp.minimum(i, nt - 1)) % tiles_per_half, 0))
        in_specs += [full, half, half, full, full]
        out_specs += [full] * 4
        out_shape += [jax.ShapeDtypeStruct(w.shape, F32)] * 4
        args += [w, g_mine, g_other, m, v]
    outs = pl.pallas_call(
        kern,
        grid_spec=pltpu.PrefetchScalarGridSpec(num_scalar_prefetch=1, grid=(max(n_tiles),), in_specs=in_specs,
                                               out_specs=out_specs),
        out_shape=out_shape, name=name, compiler_params=_params(("arbitrary",)),
    )(c_arr, *args)
    return [outs[4 * j:4 * j + 4] for j in range(n_mats)]


def _small_step(partials, params):
    slots = ((0, 0, D_MODEL), (1, 0, D_MODEL), (2, 0, HEAD_DIM), (2, 128, HEAD_DIM), (2, 256, N_Q_HEADS))
    loss_slot = (2, 384, 128)

    def body(*refs):
        loss_ref, dg1_ref, dg2_ref, dgq_ref, dgk_ref, dsink_ref = refs[:6]
        p_refs, out_refs = refs[6:21], refs[21:42]
        mine, gathered, send_sems, recv_sems = refs[42:]
        x, y, c = _position()
        me = 4 * x + 2 * y + c
        mine[...] = jnp.zeros_like(mine)
        for (row, lane, n), val in zip(slots + (loss_slot,), (
                jnp.sum(dg1_ref[...], axis=0, keepdims=True), jnp.sum(dg2_ref[...], axis=0, keepdims=True),
                dgq_ref[...], dgk_ref[...], dsink_ref[...], jnp.sum(loss_ref[...], axis=0, keepdims=True))):
            mine[row:row + 1, lane:lane + n] = val
        copies = []
        for k in range(1, N_DEV):
            flip = (k >> 2) & 1, (k >> 1) & 1, k & 1
            to = (x ^ flip[0], y ^ flip[1], c ^ flip[2])
            cp = pltpu.make_async_remote_copy(
                src_ref=mine, dst_ref=gathered.at[me], send_sem=send_sems.at[k - 1], recv_sem=recv_sems.at[k - 1],
                device_id=to, device_id_type=MESH)
            cp.start()
            copies.append(cp)
        gathered[me] = mine[...]
        for k in range(1, N_DEV):
            flip = (k >> 2) & 1, (k >> 1) & 1, k & 1
            src = 4 * (x ^ flip[0]) + 2 * (y ^ flip[1]) + (c ^ flip[2])
            pltpu.make_async_remote_copy(
                src_ref=mine, dst_ref=gathered.at[src], send_sem=send_sems.at[k - 1], recv_sem=recv_sems.at[k - 1],
                device_id=(x, y, c), device_id_type=MESH).wait_recv()
        for cp in copies:
            cp.wait_send()
        total = gathered[0]
        for k in range(1, N_DEV):
            total = total + gathered[k]
        row, lane, n = loss_slot
        out_refs[0][...] = total[row:row + 1, lane:lane + n]
        for i, (row, lane, n) in enumerate(slots):
            g = total[row:row + 1, lane:lane + n]
            d, nm, nv = _adamw_math(p_refs[i][...], g, p_refs[5 + i][...], p_refs[10 + i][...])
            for kind, val in enumerate((g, d, nm, nv)):
                out_refs[1 + 5 * kind + i][...] = val

    vm = pl.BlockSpec(memory_space=pltpu.VMEM)
    shapes = [jax.ShapeDtypeStruct((1, 128), F32)] + [jax.ShapeDtypeStruct((1, n), F32) for _, _, n in slots] * 4
    return pl.pallas_call(
        body, in_specs=[vm] * 21, out_specs=[vm] * 21, out_shape=shapes,
        scratch_shapes=[pltpu.VMEM((SMALL_ROWS, D_MODEL), F32), pltpu.VMEM((N_DEV, SMALL_ROWS, D_MODEL), F32),
                        pltpu.SemaphoreType.DMA((N_DEV - 1,)), pltpu.SemaphoreType.DMA((N_DEV - 1,))],
        name="small_step",
    )(*partials, *params)


def _with_own(gathered, own, my_chip):
    return lax.dynamic_update_slice(gathered, own[None], (my_chip, 0, 0))


def kernel(x, norm_mix_gain, w_in, q_norm_gain, k_norm_gain, attn_sinks, w_branch_attn, w_branch_ret, w_out, norm_ffn_gain, w_ffn_gate, w_ffn_up, w_ffn_down, loss_target, m_norm_mix_gain, m_w_in, m_q_norm_gain, m_k_norm_gain, m_attn_sinks, m_w_branch_attn, m_w_branch_ret, m_w_out, m_norm_ffn_gain, m_w_ffn_gate, m_w_ffn_up, m_w_ffn_down, v_norm_mix_gain, v_w_in, v_q_norm_gain, v_k_norm_gain, v_attn_sinks, v_w_branch_attn, v_w_branch_ret, v_w_out, v_norm_ffn_gain, v_w_ffn_gate, v_w_ffn_up, v_w_ffn_down):
    my_chip = 2 * lax.axis_index("x") + lax.axis_index("y")
    c_arr = lax.axis_index("c").astype(jnp.int32).reshape(1)
    chip_arr = my_chip.astype(jnp.int32).reshape(1)
    x_t, target = x[0], loss_target[0]
    g1, g2, gq, gk, sinks = norm_mix_gain, norm_ffn_gain, q_norm_gain, k_norm_gain, attn_sinks

    tr = lambda a: jnp.transpose(a[0])
    own_w_in = _bf(tr(w_in))
    own_rest = [_bf(a) for a in (tr(w_ffn_gate), tr(w_ffn_up), w_ffn_down[0], w_branch_attn[0], w_branch_ret[0],
                                 w_out[0])]
    tables, (got_w_in,) = _ret_tables(x_t.shape[0], _gather_exchange([own_w_in], 0.9))
    w_in_t = _with_own(got_w_in, own_w_in, my_chip).reshape(D_IN, D_MODEL)
    h1, q_a, kv_a, q_r, k_r, v_r, g_r, z_a, z_r, *got_rest = _proj_fwd(x_t, g1, w_in_t, _gather_exchange(own_rest, 0.8))
    wg_t, wu_t, wd, wba, wbr, wout = [_with_own(got, own, my_chip).reshape(N_CHIPS * own.shape[0], D_MODEL)
                                      for got, own in zip(got_rest, own_rest)]

    gq_col, gk_col = gq.reshape(HEAD_DIM, 1), gk.reshape(HEAD_DIM, 1)
    attn, probs, sink_probs, ret_normed, ret_rrms, ret, states = _fused(
        [_attn_fwd(q_a, kv_a, gq_col, gk, sinks), _ret_fwd(q_r, k_r, v_r, g_r, tables)],
        grid=(x_t.shape[0] // BLOCK,), name="mixers_fwd")
    s_a, s_r, dz_a_factor, dz_r_factor, merged, x1, h2 = _mix_fwd(attn, ret, z_a, z_r, x_t, wba, wbr, wout, g2)
    act, dgate, dup, dyb, dx1, dx1b, loss_p, dg2_p = _ffn_fwd_bwd(h2, x1, target, wg_t, wu_t, wd, g2)

    def pairs(row0, rows):
        return lambda i: [(h * rows, rows, (2 * i + h, pl.ds(row0, rows), slice(None))) for h in range(2)]

    f_block = jax.ShapeDtypeStruct((N_CHIPS, 3 * FF_SH, D_MODEL), F32)
    f_block, = _dw(dgate, h2, tm=2 * FF_SH, place=pairs(0, FF_SH), buf=f_block, name="dw_gate")
    f_block, = _dw(dup, h2, tm=2 * FF_SH, place=pairs(FF_SH, FF_SH), buf=f_block, name="dw_up")
    f_block, = _dw(act, dyb, tm=2 * FF_SH, place=pairs(2 * FF_SH, FF_SH), buf=f_block, name="dw_down")
    (dba, dbr, d_attn, d_o, d_gz, sib_ffn) = _mix_bwd(
        dx1b, s_a, s_r, dz_a_factor, dz_r_factor, g_r, ret_normed, ret_rrms, wout, wba, wbr,
        _pair_exchange([f_block]))
    f_sum = _pair_sum(f_block, sib_ffn, c_arr, tile=528, name="pair_sum_ffn")

    def quarters(row0, rows):
        return lambda i: [(k * rows, rows, (k, pl.ds(row0, rows), slice(None))) for k in range(N_CHIPS)]

    m_block = jax.ShapeDtypeStruct((N_CHIPS, D_MODEL, D_MODEL), F32)
    m_block, = _dw(attn, dba, tm=ATT_Q, place=quarters(0, 256), buf=m_block, name="dw_ba")
    m_block, = _dw(ret, dbr, tm=D_MODEL, place=pairs(256, 512), buf=m_block, name="dw_br")
    m_block, = _dw(merged, dx1b, tm=D_MODEL, place=quarters(768, 256), buf=m_block, name="dw_out")

    def w_in_rows(off, w):
        tm = min(w, D_MODEL)
        return dict(tm=tm, place=lambda i: [(0, tm, (pl.ds(off + i * tm, tm), slice(None)))])

    w_block = jax.ShapeDtypeStruct((D_IN, D_MODEL), F32)
    w_block, sib_mix = _dw(d_gz, h1, buf=w_block, name="dw_in_gz", exchange=_pair_exchange([m_block]),
                           **w_in_rows(P_GR[0], d_gz.shape[1]))
    m_sum = _pair_sum(m_block, sib_mix, c_arr, tile=256, name="pair_sum_mix")

    d_ret, got_ffn_sums = _ret_bwd(q_r, k_r, v_r, d_o, states, tables, _scatter_to_owners([f_sum]))
    ffn_half = _sum_chips(f_sum, got_ffn_sums, chip_arr, tile=528, name="sum_chips_ffn")
    w_block, = _dw(d_ret, h1, buf=w_block, name="dw_in_ret", **w_in_rows(P_QR[0], d_ret.shape[1]))

    (dq_a, dkv_a, dgq, dgk, dsinks, got_mix_sums, ffn_other) = _attn_bwd(
        q_a, kv_a, d_attn, probs, sink_probs, gq_col, gk, gk_col,
        _merge_exchanges(_scatter_to_owners([m_sum]), _share_halves([ffn_half])))
    dgq = dgq.reshape(1, HEAD_DIM)
    mix_half = _sum_chips(m_sum, got_mix_sums, chip_arr, tile=256, name="sum_chips_mix")
    w_block, mix_other = _dw(dq_a, h1, buf=w_block, name="dw_in_q", exchange=_share_halves([mix_half]),
                             **w_in_rows(*P_QA))
    w_block, = _dw(dkv_a, h1, buf=w_block, name="dw_in_kv", **w_in_rows(*P_KVA))

    w_block = w_block.reshape(N_CHIPS, W_IN_SH, D_MODEL)
    sib_w_in, = _run_exchange(_pair_exchange([w_block]), "pair_exchange_w_in")
    w_sum = _pair_sum(w_block, sib_w_in, c_arr, tile=592, name="pair_sum_w_in")
    d_pieces = [dq_a, dkv_a, d_ret, d_gz]
    grad_x, dg1_p, got_w_in_sums = _proj_bwd(d_pieces, x_t, dx1, w_in_t, g1, _scatter_to_owners([w_sum]))
    w_in_half = _sum_chips(w_sum, got_w_in_sums, chip_arr, tile=592, name="sum_chips_w_in")
    w_in_other, = _run_exchange(_share_halves([w_in_half]), "share_halves_w_in")

    def update(name, g_half, g_other, tile, mats):
        outs = _adamw([tuple(tr(a) if t else a[0] for a in wmv) + (off,) for _, *wmv, off, t in mats],
                      g_half, g_other, c_arr, tile=tile, name=f"adamw_{name}")
        return {key: [jnp.transpose(o) if t else o for o in res] for (key, _, _, _, _, t), res in zip(mats, outs)}

    big = {
        **update("w_in", w_in_half, w_in_other, 592, [("w_in", w_in, m_w_in, v_w_in, 0, True)]),
        **update("ffn", ffn_half, ffn_other, 176, [
            ("wg", w_ffn_gate, m_w_ffn_gate, v_w_ffn_gate, 0, True),
            ("wu", w_ffn_up, m_w_ffn_up, v_w_ffn_up, FF_SH, True),
            ("wd", w_ffn_down, m_w_ffn_down, v_w_ffn_down, 2 * FF_SH, False)]),
        **update("mix", mix_half, mix_other, 128, [
            ("wba", w_branch_attn, m_w_branch_attn, v_w_branch_attn, 0, False),
            ("wbr", w_branch_ret, m_w_branch_ret, v_w_branch_ret, 256, False),
            ("wout", w_out, m_w_out, v_w_out, 768, False)])}

    loss_row, *small = _small_step(
        [loss_p.reshape(-1, 128), dg1_p.reshape(-1, D_MODEL), dg2_p.reshape(-1, D_MODEL), dgq, dgk, dsinks],
        [norm_mix_gain, norm_ffn_gain, q_norm_gain, k_norm_gain, attn_sinks,
         m_norm_mix_gain, m_norm_ffn_gain, m_q_norm_gain, m_k_norm_gain, m_attn_sinks,
         v_norm_mix_gain, v_norm_ffn_gain, v_q_norm_gain, v_k_norm_gain, v_attn_sinks])
    loss = loss_row[0, 0]

    def leaves(i):
        b = [big[n][i][None] for n in ("w_in", "wba", "wbr", "wout", "wg", "wu", "wd")]
        s1, s2, sq, sk, ss = small[5 * i:5 * i + 5]
        return [s1, b[0], sq, sk, ss, b[1], b[2], b[3], s2, b[4], b[5], b[6]]

    return (loss, grad_x[None], *leaves(0), *leaves(1), *leaves(2), *leaves(3))
```

```python
import jax
import jax.numpy as jnp
from jax import lax
from jax.experimental import pallas as pl
from jax.experimental.pallas import tpu as pltpu

F32 = jnp.float32
BF16 = jnp.bfloat16
MESH = pl.DeviceIdType.MESH

D_MODEL = 1024
EPS = 1e-6
HEAD_DIM = 64
N_Q_HEADS = 16
N_KV_HEADS = 2
GROUP = 8
BLOCK = 128
RET_HEADS = 4
RET_QK_DIM = 256
RET_V_DIM = 512
RET_CHUNK = 128
RET_ROT_BASE = 10000.0
D_FF = 2816
ATT_Q = N_Q_HEADS * HEAD_DIM
ATT_KV = N_KV_HEADS * HEAD_DIM
RET_QK = RET_HEADS * RET_QK_DIM
RET_V = RET_HEADS * RET_V_DIM
D_IN = 9472
ADAM_LR = 0.001
ADAM_B1 = 0.9
ADAM_B2 = 0.999
ADAM_EPS = 1e-08
ADAM_WD = 0.01
ADAM_STEP = 10

N_CHIPS = 4
N_DEV = 8
VMEM_LIMIT_BYTES = 60 * 1024 * 1024

P_QA = (0, 1024)
P_KVA = (1024, 256)
P_QR = (1280, 1024)
P_KR = (2304, 1024)
P_VR = (3328, 2048)
P_GR = (5376, 2048)
P_ZA = (7424, 1024)
P_ZR = (8448, 1024)

W_IN_SH = D_IN // N_CHIPS
FF_SH = D_FF // N_CHIPS

SMALL_ROWS = 8


def _dot(a, b):
    return jnp.dot(a, b, preferred_element_type=F32)


def _dot_nt(a, b):
    return lax.dot_general(a, b, (((1,), (1,)), ((), ())), preferred_element_type=F32)


def _dot_tn(a, b):
    return lax.dot_general(a, b, (((0,), (0,)), ((), ())), preferred_element_type=F32)


def _bf(x):
    return x.astype(BF16)


def _rms_stats(x):
    r = lax.rsqrt(jnp.mean(x * x, axis=-1, keepdims=True) + EPS)
    return r, x * r


def _rms_bwd(dy, xhat, r, gain):
    u = dy * gain
    dx = r * (u - xhat * jnp.mean(u * xhat, axis=-1, keepdims=True))
    return dx, dy * xhat


def _params(sem):
    return pltpu.CompilerParams(dimension_semantics=sem, vmem_limit_bytes=VMEM_LIMIT_BYTES)


_ANY = pl.BlockSpec(memory_space=pl.ANY)


class _Exchange:
    def __init__(self, ins, outs, n_sems, phases):
        self.ins, self.outs, self.n_sems, self.phases = list(ins), list(outs), n_sems, list(phases)


def _merge_exchanges(a, b):
    na_i, na_o, shift = len(a.ins), len(a.outs), a.n_sems

    def first(fn):
        return lambda i, o, s, r, base: fn(i[:na_i], o[:na_o], s, r, base)

    def second(fn):
        return lambda i, o, s, r, base: fn(i[na_i:], o[na_o:], s, r, base + shift)

    phases = [(f, first(fn)) for f, fn in a.phases] + [(f, second(fn)) for f, fn in b.phases]
    return _Exchange(a.ins + b.ins, a.outs + b.outs, a.n_sems + b.n_sems, sorted(phases, key=lambda p: p[0]))


def _pallas(kern, *, grid, in_specs, out_specs, out_shape, args, name, scratch=(), exchange=None, aliases=None):
    aliases = aliases or {}
    if exchange is None:
        return pl.pallas_call(
            kern, grid=grid, in_specs=in_specs, out_specs=out_specs, out_shape=out_shape, name=name,
            scratch_shapes=list(scratch), input_output_aliases=aliases,
            compiler_params=_params(("arbitrary",) * len(grid)))(*args)
    n_in, n_out, n_sc = len(in_specs), len(out_specs), len(scratch)
    n_xi, n_xo = len(exchange.ins), len(exchange.outs)
    n_steps = 1
    for g in grid:
        n_steps *= g

    def wrapped(*refs):
        ins, refs = refs[:n_in], refs[n_in:]
        x_ins, refs = refs[:n_xi], refs[n_xi:]
        outs, refs = refs[:n_out], refs[n_out:]
        x_outs, refs = refs[:n_xo], refs[n_xo:]
        scr, (send_sems, recv_sems) = refs[:n_sc], refs[n_sc:]
        step = pl.program_id(0)
        for d in range(1, len(grid)):
            step = step * grid[d] + pl.program_id(d)
        for frac, fn in exchange.phases:
            at = min(int(frac * n_steps), n_steps - 1)

            @pl.when(step == at)
            def _(fn=fn):
                fn(x_ins, x_outs, send_sems, recv_sems, 0)

        kern(*ins, *outs, *scr)

    sems = [pltpu.SemaphoreType.DMA((exchange.n_sems,)), pltpu.SemaphoreType.DMA((exchange.n_sems,))]
    return pl.pallas_call(
        wrapped, grid=grid, in_specs=list(in_specs) + [_ANY] * n_xi, out_specs=list(out_specs) + [_ANY] * n_xo,
        out_shape=list(out_shape) + exchange.outs, name=name, scratch_shapes=list(scratch) + sems,
        input_output_aliases=aliases, compiler_params=_params(("arbitrary",) * len(grid)))(*args, *exchange.ins)


def _run_exchange(exchange, name):
    def body(*refs):
        n_i, n_o = len(exchange.ins), len(exchange.outs)
        for _, fn in exchange.phases:
            fn(refs[:n_i], refs[n_i:n_i + n_o], refs[n_i + n_o], refs[n_i + n_o + 1], 0)

    sems = [pltpu.SemaphoreType.DMA((exchange.n_sems,)), pltpu.SemaphoreType.DMA((exchange.n_sems,))]
    return pl.pallas_call(body, in_specs=[_ANY] * len(exchange.ins), out_specs=[_ANY] * len(exchange.outs),
                          out_shape=exchange.outs, scratch_shapes=sems, name=name)(*exchange.ins)


def _fused(parts, *, grid, name, exchange=None):
    counts = [(len(p["in_specs"]), len(p["out_specs"]), len(p["scratch"])) for p in parts]
    n_in, n_out = sum(c[0] for c in counts), sum(c[1] for c in counts)

    def kern(*refs):
        ins, outs, scr = refs[:n_in], refs[n_in:n_in + n_out], refs[n_in + n_out:]
        i0 = o0 = s0 = 0
        for p, (ni, no, ns) in zip(parts, counts):
            p["kern"](*ins[i0:i0 + ni], *outs[o0:o0 + no], *scr[s0:s0 + ns])
            i0, o0, s0 = i0 + ni, o0 + no, s0 + ns

    cat = lambda key: [a for p in parts for a in p[key]]
    return _pallas(kern, grid=grid, in_specs=cat("in_specs"), out_specs=cat("out_specs"), out_shape=cat("out_shape"),
                   scratch=cat("scratch"), args=cat("args"), name=name, exchange=exchange)


def _row_call(body, *, tm, row_ins, res_ins, row_outs, part_outs=(), name, exchange=None):
    t = row_ins[0].shape[0]
    n_tiles = t // tm
    in_specs = [pl.BlockSpec((tm, a.shape[1]), lambda i: (i, 0)) for a in row_ins]
    in_specs += [pl.BlockSpec(a.shape, lambda i: (0, 0), pipeline_mode=pl.Buffered(1)) for a in res_ins]
    out_shape = [jax.ShapeDtypeStruct((t, w), dt) for (w, dt) in row_outs]
    out_shape += [jax.ShapeDtypeStruct((n_tiles, 1, w), F32) for w in part_outs]
    out_specs = [pl.BlockSpec((tm, w), lambda i: (i, 0)) for (w, _) in row_outs]
    out_specs += [pl.BlockSpec((1, 1, w), lambda i: (i, 0, 0)) for w in part_outs]
    n_ri, n_re, n_ro = len(row_ins), len(res_ins), len(row_outs)

    def kern(*refs):
        body(refs[:n_ri], refs[n_ri:n_ri + n_re], refs[n_ri + n_re:n_ri + n_re + n_ro], refs[n_ri + n_re + n_ro:])

    return _pallas(kern, grid=(n_tiles,), in_specs=in_specs, out_specs=out_specs, out_shape=out_shape,
                   args=[*row_ins, *res_ins], name=name, exchange=exchange)


def _proj_fwd(x, g1, w_in_t, exchange):
    pieces = ((P_QA, F32), (P_KVA, F32), (P_QR, F32), (P_KR, F32), (P_VR, BF16), (P_GR, F32), (P_ZA, F32), (P_ZR, F32))

    def body(ri, re, ro, po):
        x_t = ri[0][...]
        r, xhat = _rms_stats(x_t)
        hb = _bf(xhat * re[0][...])
        ro[0][...] = hb
        for k, ((off, w), dt) in enumerate(pieces):
            ro[1 + k][...] = _dot_nt(hb, re[1][off:off + w, :]).astype(dt)

    outs = [(D_MODEL, BF16)] + [(w, dt) for ((_, w), dt) in pieces]
    return _row_call(body, tm=256, row_ins=[x], res_ins=[g1, w_in_t], row_outs=outs, name="proj_fwd",
                     exchange=exchange)


def _mix_fwd(attn, ret, z_a, z_r, x, wba, wbr, wout, g2):
    def body(ri, re, ro, po):
        ba = _dot(ri[0][...], re[0][...])
        br = _dot(ri[1][...], re[1][...])
        m = jax.nn.sigmoid(ri[2][...]) * ba + jax.nn.sigmoid(ri[3][...]) * br
        mb = _bf(m)
        x1 = ri[4][...] + _dot(mb, re[2][...])
        r, xhat = _rms_stats(x1)
        ro[0][...] = ba
        ro[1][...] = br
        ro[2][...] = mb
        ro[3][...] = x1
        ro[4][...] = _bf(xhat * re[3][...])

    outs = [(D_MODEL, F32), (D_MODEL, F32), (D_MODEL, BF16), (D_MODEL, F32), (D_MODEL, BF16)]
    return _row_call(body, tm=512, row_ins=[attn, ret, z_a, z_r, x], res_ins=[wba, wbr, wout, g2], row_outs=outs,
                     name="mix_fwd")


def _ffn_fwd_bwd(h2, x1, target, wg_t, wu_t, wd, g2):
    def body(ri, re, ro, po):
        h2_t = ri[0][...]
        x1_t = ri[1][...]
        gate = _dot_nt(h2_t, re[0][...])
        up = _dot_nt(h2_t, re[1][...])
        sg = jax.nn.sigmoid(gate)
        sl = gate * sg
        actb = _bf(sl * up)
        ro[0][...] = actb
        y = x1_t + _dot(actb, re[2][...])
        e = y - ri[2][...]
        po[0][0] = jnp.broadcast_to(0.5 * jnp.sum(jnp.sum(e * e, axis=1, keepdims=True), axis=0, keepdims=True)
                                    * (1.0 / D_MODEL), (1, 128))
        dy = e * (1.0 / D_MODEL)
        dyb = _bf(dy)
        ro[3][...] = dyb
        dact = _dot_nt(dyb, re[2][...])
        dupb = _bf(dact * sl)
        dgateb = _bf(dact * up * (sg * (1.0 + gate * (1.0 - sg))))
        ro[1][...] = dgateb
        ro[2][...] = dupb
        dh2 = _dot(dgateb, re[0][...]) + _dot(dupb, re[1][...])
        r, xhat = _rms_stats(x1_t)
        dxn, dgain = _rms_bwd(dh2, xhat, r, re[3][...])
        dx1 = dy + dxn
        ro[4][...] = dx1
        ro[5][...] = _bf(dx1)
        po[1][0] = jnp.sum(dgain, axis=0, keepdims=True)

    outs = [(D_FF, BF16), (D_FF, BF16), (D_FF, BF16), (D_MODEL, BF16), (D_MODEL, F32), (D_MODEL, BF16)]
    return _row_call(body, tm=256, row_ins=[h2, x1, target], res_ins=[wg_t, wu_t, wd, g2], row_outs=outs,
                     part_outs=(128, D_MODEL), name="ffn_fwd_bwd")


def _mix_bwd(dx1b, z_a, z_r, ba, br, g_r, ret_normed, ret_rrms, wout, wba, wbr, exchange):
    def body(ri, re, ro, po):
        dm = _dot_nt(ri[0][...], re[0][...])
        sa = jax.nn.sigmoid(ri[1][...])
        sr = jax.nn.sigmoid(ri[2][...])
        dbab = _bf(sa * dm)
        dbrb = _bf(sr * dm)
        ro[0][...] = dbab
        ro[1][...] = dbrb
        ro[4][:, RET_V:RET_V + D_MODEL] = _bf(dm * ri[3][...] * (sa * (1.0 - sa)))
        ro[4][:, RET_V + D_MODEL:RET_V + 2 * D_MODEL] = _bf(dm * ri[4][...] * (sr * (1.0 - sr)))
        ro[2][...] = _bf(_dot_nt(dbab, re[1][...]))
        dret = _dot_nt(dbrb, re[2][...])
        for h in range(RET_HEADS):
            cols = slice(h * RET_V_DIM, (h + 1) * RET_V_DIM)
            g = ri[5][:, cols]
            rn = ri[6][:, cols]
            r = ri[7][:, h:h + 1]
            sg = jax.nn.sigmoid(g)
            dret_h = dret[:, cols]
            d_rn = dret_h * (g * sg)
            ro[4][:, cols] = _bf(dret_h * rn * (sg * (1.0 + g * (1.0 - sg))))
            ro[3][:, cols] = r * (d_rn - rn * jnp.mean(d_rn * rn, axis=-1, keepdims=True))

    outs = [(D_MODEL, BF16), (D_MODEL, BF16), (ATT_Q, BF16), (RET_V, F32), (RET_V + 2 * D_MODEL, BF16)]
    return _row_call(body, tm=256, row_ins=[dx1b, z_a, z_r, ba, br, g_r, ret_normed, ret_rrms],
                     res_ins=[wout, wba, wbr], row_outs=outs, name="mix_bwd", exchange=exchange)


def _proj_bwd(d_pieces, x, dx1, w_in_t, g1, exchange):
    widths = [p.shape[1] for p in d_pieces]
    groups = [(sum(widths[:k]), w) for k, w in enumerate(widths)]
    n_p = len(groups)

    def body(ri, re, ro, po):
        dh = None
        for k, (off, w) in enumerate(groups):
            term = _dot(ri[k][...], re[0][off:off + w, :])
            dh = term if dh is None else dh + term
        r, xhat = _rms_stats(ri[n_p][...])
        dxn, dgain = _rms_bwd(dh, xhat, r, re[1][...])
        ro[0][...] = ri[n_p + 1][...] + dxn
        po[0][0] = jnp.sum(dgain, axis=0, keepdims=True)

    return _row_call(body, tm=512, row_ins=[*d_pieces, x, dx1], res_ins=[w_in_t, g1], row_outs=[(D_MODEL, F32)],
                     part_outs=(D_MODEL,), name="proj_bwd", exchange=exchange)


def _dw(a, b, *, tm, place, buf, name, exchange=None):
    t, m = a.shape
    n = b.shape[1]
    tk = min(2048, t)
    n_i, n_k = m // tm, t // tk
    fresh = isinstance(buf, jax.ShapeDtypeStruct)
    n_copies = len(place(0))

    def kern(a_ref, b_ref, *rest):
        out_ref, acc, sems = rest[-3:]
        i, k = pl.program_id(0), pl.program_id(1)
        part = _dot_tn(a_ref[...], b_ref[...])

        @pl.when(k == 0)
        def _():
            acc[i] = part

        @pl.when(k > 0)
        def _():
            acc[i] += part

        def copies(tile):
            return [pltpu.make_async_copy(acc.at[tile, pl.ds(r0, rows), :], out_ref.at[idx], sems.at[tile * n_copies + c])
                    for c, (r0, rows, idx) in enumerate(place(tile))]

        for tile in range(n_i):
            @pl.when((i == tile) & (k == n_k - 1))
            def _(tile=tile):
                for cp in copies(tile):
                    cp.start()

        @pl.when((i == n_i - 1) & (k == n_k - 1))
        def _():
            for tile in range(n_i):
                for cp in copies(tile):
                    cp.wait()

    in_specs = [pl.BlockSpec((tk, tm), lambda i, k: (k, i)), pl.BlockSpec((tk, n), lambda i, k: (k, 0))]
    shape = buf if fresh else jax.ShapeDtypeStruct(buf.shape, buf.dtype)
    return _pallas(
        kern, grid=(n_i, n_k), in_specs=in_specs + ([] if fresh else [_ANY]), out_specs=[_ANY], out_shape=[shape],
        scratch=[pltpu.VMEM((n_i, tm, n), F32), pltpu.SemaphoreType.DMA((n_i * n_copies,))],
        args=[a, b] + ([] if fresh else [buf]), aliases=None if fresh else {2: 0}, name=name, exchange=exchange)


def _heads_to_lanes(x3):
    return jnp.concatenate([x3[g] for g in range(GROUP)], axis=1)


def _lanes_to_heads(xt):
    return jnp.concatenate([xt[:, g * BLOCK:(g + 1) * BLOCK] for g in range(GROUP)], axis=0)


def _attn_queries(kvh, q_ref, gq_col):
    cols = slice(kvh * GROUP * HEAD_DIM, (kvh + 1) * GROUP * HEAD_DIM)
    q3 = q_ref[:, cols].T.reshape(GROUP, HEAD_DIM, BLOCK)
    rq = lax.rsqrt(jnp.mean(q3 * q3, axis=1, keepdims=True) + EPS)
    qhat = q3 * rq
    return qhat, rq, _heads_to_lanes(_bf(qhat * (gq_col * (HEAD_DIM ** -0.5))))


def _from_prev():
    j = lax.broadcasted_iota(jnp.int32, (BLOCK, GROUP * BLOCK), 0)
    i = lax.broadcasted_iota(jnp.int32, (BLOCK, GROUP * BLOCK), 1) & (BLOCK - 1)
    return j > i


def _attn_probs(n, kvh, qts, kvp_ref, kvc_ref, gk, sink_ref):
    kcols = slice(kvh * HEAD_DIM, (kvh + 1) * HEAD_DIM)
    k = jnp.concatenate([kvp_ref[:, kcols], kvc_ref[:, kcols]], axis=0)
    rk, khat = _rms_stats(k)
    st = _dot(_bf(khat * gk), qts)
    f = jnp.where(_from_prev(), jnp.where(n > 0, st[0:BLOCK], -1e30), st[BLOCK:2 * BLOCK])
    sink = jnp.concatenate([jnp.broadcast_to(sink_ref[0:1, kvh * GROUP + g:kvh * GROUP + g + 1], (1, BLOCK))
                            for g in range(GROUP)], axis=1)
    m = jnp.maximum(jnp.max(f, axis=0, keepdims=True), sink)
    e = jnp.exp(f - m)
    es = jnp.exp(sink - m)
    inv = 1.0 / (jnp.sum(e, axis=0, keepdims=True) + es)
    return e * inv, es * inv


def _unfold(from_prev, xf):
    return _bf(jnp.concatenate([jnp.where(from_prev, xf, 0.0), jnp.where(from_prev, 0.0, xf)], axis=0))


def _attn_fwd(q_a, kv_a, gq_col, gk, sinks):
    t = q_a.shape[0]
    nb = t // BLOCK

    def kern(q_ref, kvp_ref, kvc_ref, gq_ref, gk_ref, sink_ref, o_ref, pf_ref, ps_ref):
        n = pl.program_id(0)
        kvt = jnp.concatenate([kvp_ref[...].T, kvc_ref[...].T], axis=1)
        for kvh in range(N_KV_HEADS):
            _, _, qts = _attn_queries(kvh, q_ref, gq_ref[...])
            pf, psink = _attn_probs(n, kvh, qts, kvp_ref, kvc_ref, gk_ref[...], sink_ref)
            lanes = slice(kvh * GROUP * BLOCK, (kvh + 1) * GROUP * BLOCK)
            pf_ref[:, lanes] = pf
            ps_ref[:, lanes] = psink
            vt = _bf(kvt[ATT_KV + kvh * HEAD_DIM:ATT_KV + (kvh + 1) * HEAD_DIM, :])
            out_t = _dot(vt, _unfold(_from_prev(), pf))
            cols = slice(kvh * GROUP * HEAD_DIM, (kvh + 1) * GROUP * HEAD_DIM)
            o_ref[:, cols] = _bf(_lanes_to_heads(out_t).T)

    small = lambda a: pl.BlockSpec(a.shape, lambda n: (0, 0))
    folded = N_KV_HEADS * GROUP * BLOCK
    return dict(
        kern=kern,
        in_specs=[pl.BlockSpec((BLOCK, ATT_Q), lambda n: (n, 0)),
                  pl.BlockSpec((BLOCK, 2 * ATT_KV), lambda n: (jnp.maximum(n - 1, 0), 0)),
                  pl.BlockSpec((BLOCK, 2 * ATT_KV), lambda n: (n, 0)),
                  small(gq_col), small(gk), small(sinks)],
        out_specs=[pl.BlockSpec((BLOCK, ATT_Q), lambda n: (n, 0)), pl.BlockSpec((BLOCK, folded), lambda n: (n, 0)),
                   pl.BlockSpec((None, 1, folded), lambda n: (n, 0, 0))],
        out_shape=[jax.ShapeDtypeStruct((t, ATT_Q), BF16), jax.ShapeDtypeStruct((t, folded), F32),
                   jax.ShapeDtypeStruct((nb, 1, folded), F32)],
        scratch=[], args=[q_a, kv_a, kv_a, gq_col, gk, sinks])


def _attn_bwd(q_a, kv_a, d_attn, probs, sink_probs, gq_col, gk, gk_col, exchange):
    t = q_a.shape[0]
    nb = t // BLOCK

    def kern(q_ref, kvp_ref, kvc_ref, do_ref, pf_ref, ps_ref, gq_ref, gk_ref, gkc_ref,
             dq_ref, dkv_ref, dgq_ref, dgk_ref, dsink_ref, band_k, band_v, carry_k, carry_v):
        n = pl.program_id(0)
        gq_v = gq_ref[...]
        gk_v = gk_ref[...]

        @pl.when(n == 0)
        def _():
            carry_k[...] = jnp.zeros_like(carry_k)
            carry_v[...] = jnp.zeros_like(carry_v)
            dgq_ref[...] = jnp.zeros_like(dgq_ref)
            dgk_ref[...] = jnp.zeros_like(dgk_ref)
            dsink_ref[...] = jnp.zeros_like(dsink_ref)

        @pl.when(n == nb)
        def _():
            band_k[...] = jnp.zeros_like(band_k)
            band_v[...] = jnp.zeros_like(band_v)

        @pl.when(n < nb)
        def _():
            lane16 = lax.broadcasted_iota(jnp.int32, (1, N_Q_HEADS), 1)
            dsink = jnp.zeros((1, N_Q_HEADS), F32)
            dgq = jnp.zeros((HEAD_DIM, 1), F32)
            gk_col = gkc_ref[...]
            kvt = jnp.concatenate([kvp_ref[...].T, kvc_ref[...].T], axis=1)
            from_prev = _from_prev()
            for kvh in range(N_KV_HEADS):
                qhat, rq, qts = _attn_queries(kvh, q_ref, gq_v)
                lanes = slice(kvh * GROUP * BLOCK, (kvh + 1) * GROUP * BLOCK)
                pf = pf_ref[:, lanes]
                cols = slice(kvh * GROUP * HEAD_DIM, (kvh + 1) * GROUP * HEAD_DIM)
                vcols = slice(ATT_KV + kvh * HEAD_DIM, ATT_KV + (kvh + 1) * HEAD_DIM)
                dot = _heads_to_lanes(_bf(do_ref[:, cols].astype(F32).T.reshape(GROUP, HEAD_DIM, BLOCK)))
                vb = _bf(jnp.concatenate([kvp_ref[:, vcols], kvc_ref[:, vcols]], axis=0))
                dpt = _dot(vb, dot)
                dpf = jnp.where(from_prev, dpt[0:BLOCK], dpt[BLOCK:2 * BLOCK])
                delta = jnp.sum(pf * dpf, axis=0, keepdims=True)
                dst = _unfold(from_prev, pf * (dpf - delta))
                dsk = ps_ref[:, lanes] * delta
                for g in range(GROUP):
                    tot = jnp.sum(dsk[:, g * BLOCK:(g + 1) * BLOCK], axis=1, keepdims=True)
                    dsink = dsink - jnp.where(lane16 == kvh * GROUP + g, tot, 0.0)
                kt = kvt[kvh * HEAD_DIM:(kvh + 1) * HEAD_DIM, :]
                knt = _bf(kt * lax.rsqrt(jnp.mean(kt * kt, axis=0, keepdims=True) + EPS) * gk_col)
                dqn = (_dot(knt, dst) * (HEAD_DIM ** -0.5))
                band_k[kvh] = _dot_nt(dst, qts)
                band_v[kvh] = _dot_nt(_unfold(from_prev, pf), dot)
                dqn3 = _lanes_to_heads(dqn).reshape(GROUP, HEAD_DIM, BLOCK)
                u = dqn3 * gq_v
                dq3 = rq * (u - qhat * jnp.mean(u * qhat, axis=1, keepdims=True))
                dgq = dgq + jnp.sum(jnp.sum(dqn3 * qhat, axis=0), axis=1, keepdims=True)
                dq_ref[:, cols] = _bf(dq3.reshape(GROUP * HEAD_DIM, BLOCK).T)
            dsink_ref[...] += dsink
            dgq_ref[...] += dgq

        dgk = jnp.zeros((1, HEAD_DIM), F32)
        for kvh in range(N_KV_HEADS):
            kcols = slice(kvh * HEAD_DIM, (kvh + 1) * HEAD_DIM)
            vcols = slice(ATT_KV + kvh * HEAD_DIM, ATT_KV + (kvh + 1) * HEAD_DIM)
            dkn = carry_k[kvh] + band_k[kvh, 0:BLOCK, :]
            dv = carry_v[kvh] + band_v[kvh, 0:BLOCK, :]
            rk, khat = _rms_stats(kvp_ref[:, kcols])
            dk, dgain = _rms_bwd(dkn, khat, rk, gk_v)
            dgk = dgk + jnp.sum(dgain, axis=0, keepdims=True)
            dkv_ref[:, kcols] = _bf(dk)
            dkv_ref[:, vcols] = _bf(dv)
            carry_k[kvh] = band_k[kvh, BLOCK:2 * BLOCK, :]
            carry_v[kvh] = band_v[kvh, BLOCK:2 * BLOCK, :]
        dgk_ref[...] += dgk

    small = lambda a: pl.BlockSpec(a.shape, lambda n: (0, 0))
    last = nb - 1
    return _pallas(
        kern, grid=(nb + 1,),
        in_specs=[pl.BlockSpec((BLOCK, ATT_Q), lambda n: (jnp.minimum(n, last), 0)),
                  pl.BlockSpec((BLOCK, 2 * ATT_KV), lambda n: (jnp.maximum(n - 1, 0), 0)),
                  pl.BlockSpec((BLOCK, 2 * ATT_KV), lambda n: (jnp.minimum(n, last), 0)),
                  pl.BlockSpec((BLOCK, ATT_Q), lambda n: (jnp.minimum(n, last), 0)),
                  pl.BlockSpec((BLOCK, probs.shape[1]), lambda n: (jnp.minimum(n, last), 0)),
                  pl.BlockSpec((None, 1, probs.shape[1]), lambda n: (jnp.minimum(n, last), 0, 0)),
                  small(gq_col), small(gk), small(gk_col)],
        out_specs=[pl.BlockSpec((BLOCK, ATT_Q), lambda n: (jnp.minimum(n, last), 0)),
                   pl.BlockSpec((BLOCK, 2 * ATT_KV), lambda n: (jnp.maximum(n - 1, 0), 0)),
                   pl.BlockSpec((HEAD_DIM, 1), lambda n: (0, 0)),
                   pl.BlockSpec((1, HEAD_DIM), lambda n: (0, 0)),
                   pl.BlockSpec((1, N_Q_HEADS), lambda n: (0, 0))],
        out_shape=[jax.ShapeDtypeStruct((t, ATT_Q), BF16), jax.ShapeDtypeStruct((t, 2 * ATT_KV), BF16),
                   jax.ShapeDtypeStruct((HEAD_DIM, 1), F32), jax.ShapeDtypeStruct((1, HEAD_DIM), F32),
                   jax.ShapeDtypeStruct((1, N_Q_HEADS), F32)],
        scratch=[pltpu.VMEM((N_KV_HEADS, 2 * BLOCK, HEAD_DIM), F32),
                 pltpu.VMEM((N_KV_HEADS, 2 * BLOCK, HEAD_DIM), F32),
                 pltpu.VMEM((N_KV_HEADS, BLOCK, HEAD_DIM), F32),
                 pltpu.VMEM((N_KV_HEADS, BLOCK, HEAD_DIM), F32)],
        args=[q_a, kv_a, kv_a, d_attn, probs, sink_probs, gq_col, gk, gk_col], name="attn_bwd", exchange=exchange)


def _ret_tables(t, exchange):
    theta = 1.0 / (RET_ROT_BASE ** jnp.linspace(0.0, 1.0, RET_QK_DIM // 2, dtype=F32))
    theta2 = jnp.repeat(theta, 2)[None, :]
    sign = jnp.tile(jnp.array([-1.0, 1.0], F32), RET_QK_DIM // 2)[None, :]

    def kern(theta_ref, sign_ref, cos_ref, sin_ref):
        first = pl.program_id(0) * RET_CHUNK
        pos = (first + lax.broadcasted_iota(jnp.int32, (RET_CHUNK, RET_QK_DIM), 0)).astype(F32)
        ang = pos * theta_ref[...]
        cos_ref[...] = jnp.cos(ang)
        sin_ref[...] = jnp.sin(ang) * sign_ref[...]

    row = pl.BlockSpec((1, RET_QK_DIM), lambda n: (0, 0))
    blk = pl.BlockSpec((RET_CHUNK, RET_QK_DIM), lambda n: (n, 0))
    cos, sin_s, *got = _pallas(kern, grid=(t // RET_CHUNK,), in_specs=[row, row], out_specs=[blk, blk],
                               out_shape=[jax.ShapeDtypeStruct((t, RET_QK_DIM), F32)] * 2, args=[theta2, sign],
                               name="position_tables", exchange=exchange)
    log_gamma = jnp.log(1.0 - 2.0 ** (-5.0 - jnp.arange(RET_HEADS, dtype=F32)))
    i = jnp.arange(RET_CHUNK, dtype=F32)
    diff = i[:, None] - i[None, :]
    causal = diff >= 0
    decay = jnp.where(causal[None], jnp.exp(jnp.where(causal, diff, 0.0)[None] * log_gamma[:, None, None]), 0.0)
    xi = jnp.exp((i + 1.0)[None, :] * log_gamma[:, None])[:, :, None]
    zeta = jnp.exp((RET_CHUNK - 1.0 - i)[None, :] * log_gamma[:, None])[:, :, None]
    gch = jnp.broadcast_to(jnp.exp(RET_CHUNK * log_gamma)[:, None, None], (RET_HEADS, 1, 128))
    return (cos, sin_s, decay, xi, zeta, gch), got


def _swap_pairs(x):
    lane = lax.broadcasted_iota(jnp.int32, x.shape, 1)
    return jnp.where((lane & 1) == 0, pltpu.roll(x, RET_QK_DIM - 1, 1), pltpu.roll(x, 1, 1))


def _rotate(x, cos, sin_s):
    return x * cos + _swap_pairs(x) * sin_s


def _rotate_bwd(dy, cos, sin_s):
    return dy * cos + _swap_pairs(dy * sin_s)


def _ret_specs(order):
    qk = pl.BlockSpec((RET_CHUNK, RET_QK), lambda j: (order(j), 0))
    v = pl.BlockSpec((RET_CHUNK, RET_V), lambda j: (order(j), 0))
    dec = pl.BlockSpec((RET_HEADS, RET_CHUNK, RET_CHUNK), lambda j: (0, 0, 0))
    col = pl.BlockSpec((RET_HEADS, RET_CHUNK, 1), lambda j: (0, 0, 0))
    gch = pl.BlockSpec((RET_HEADS, 1, 128), lambda j: (0, 0, 0))
    st = pl.BlockSpec((RET_HEADS, None, RET_QK_DIM, RET_V_DIM), lambda j: (0, order(j), 0, 0))
    pos = pl.BlockSpec((RET_CHUNK, RET_QK_DIM), lambda j: (order(j), 0))
    return qk, v, dec, col, gch, st, pos


def _ret_fwd(q_r, k_r, v_r, g_r, tables):
    t = q_r.shape[0]
    nc = t // RET_CHUNK
    cos, sin_s, decay, xi, zeta, gch = tables

    def kern(q_ref, k_ref, v_ref, g_ref, cos_ref, sin_ref, dec_ref, xi_ref, zeta_ref, gch_ref,
             rn_ref, r_ref, ret_ref, st_ref, state):
        @pl.when(pl.program_id(0) == 0)
        def _():
            state[...] = jnp.zeros_like(state)

        cos_t = cos_ref[...]
        sin_t = sin_ref[...]
        for h in range(RET_HEADS):
            qc = slice(h * RET_QK_DIM, (h + 1) * RET_QK_DIM)
            vc = slice(h * RET_V_DIM, (h + 1) * RET_V_DIM)
            qs = _bf(_rotate(q_ref[:, qc], cos_t, sin_t))
            ks = _rotate(k_ref[:, qc] * (RET_QK_DIM ** -0.5), cos_t, sin_t)
            vb = v_ref[:, vc]
            s_old = state[h]
            sb = _bf(s_old)
            st_ref[h] = sb
            inner = _dot_nt(qs, _bf(ks)) * dec_ref[h]
            out = _dot(_bf(inner), vb) + _dot(qs, sb) * xi_ref[h]
            state[h] = gch_ref[h, :, 0:1] * s_old + _dot_tn(_bf(ks * zeta_ref[h]), vb)
            r, rn = _rms_stats(out)
            rn_ref[:, vc] = rn
            r_ref[:, h:h + 1] = r
            g = g_ref[:, vc]
            ret_ref[:, vc] = _bf(g * jax.nn.sigmoid(g) * rn)

    qk, v, dec, col, gsp, st, pos = _ret_specs(lambda j: j)
    return dict(
        kern=kern,
        in_specs=[qk, qk, v, v, pos, pos, dec, col, col, gsp],
        out_specs=[v, pl.BlockSpec((RET_CHUNK, RET_HEADS), lambda j: (j, 0)), v, st],
        out_shape=[jax.ShapeDtypeStruct((t, RET_V), F32), jax.ShapeDtypeStruct((t, RET_HEADS), F32),
                   jax.ShapeDtypeStruct((t, RET_V), BF16),
                   jax.ShapeDtypeStruct((RET_HEADS, nc, RET_QK_DIM, RET_V_DIM), BF16)],
        scratch=[pltpu.VMEM((RET_HEADS, RET_QK_DIM, RET_V_DIM), F32)],
        args=[q_r, k_r, v_r, g_r, cos, sin_s, decay, xi, zeta, gch])


def _ret_bwd(q_r, k_r, v_r, d_o, states, tables, exchange):
    t = q_r.shape[0]
    nc = t // RET_CHUNK
    cos, sin_s, decay, xi, zeta, gch = tables

    def kern(q_ref, k_ref, v_ref, do_ref, st_ref, cos_ref, sin_ref, dec_ref, xi_ref, zeta_ref, gch_ref,
             d_ref, dstate):
        dq_ref, dk_ref = d_ref.at[:, 0:RET_QK], d_ref.at[:, RET_QK:2 * RET_QK]
        dv_ref = d_ref.at[:, 2 * RET_QK:2 * RET_QK + RET_V]

        @pl.when(pl.program_id(0) == 0)
        def _():
            dstate[...] = jnp.zeros_like(dstate)

        cos_t = cos_ref[...]
        sin_t = sin_ref[...]
        scale = RET_QK_DIM ** -0.5
        for h in range(RET_HEADS):
            qc = slice(h * RET_QK_DIM, (h + 1) * RET_QK_DIM)
            vc = slice(h * RET_V_DIM, (h + 1) * RET_V_DIM)
            qs = _bf(_rotate(q_ref[:, qc], cos_t, sin_t))
            ks = _rotate(k_ref[:, qc] * scale, cos_t, sin_t)
            ksb = _bf(ks)
            vb = v_ref[:, vc]
            d_o_t = do_ref[:, vc]
            dob = _bf(d_o_t)
            doxb = _bf(d_o_t * xi_ref[h])
            dec = dec_ref[h]
            ds_old = dstate[h]
            dsb = _bf(ds_old)
            pb = _bf(_dot_nt(qs, ksb) * dec)
            dpb = _bf(_dot_nt(dob, vb) * dec)
            dqs = _dot(dpb, ksb) + _dot_nt(doxb, st_ref[h])
            dks = _dot_tn(dpb, qs) + _dot_nt(vb, dsb) * zeta_ref[h]
            dv_ref[:, vc] = _bf(_dot_tn(pb, dob) + _dot(_bf(ks * zeta_ref[h]), dsb))
            dstate[h] = gch_ref[h, :, 0:1] * ds_old + _dot_tn(qs, doxb)
            dq_ref[:, qc] = _bf(_rotate_bwd(dqs, cos_t, sin_t))
            dk_ref[:, qc] = _bf(_rotate_bwd(dks, cos_t, sin_t) * scale)

    qk, v, dec, col, gsp, st, pos = _ret_specs(lambda j: nc - 1 - j)
    return _pallas(
        kern, grid=(nc,),
        in_specs=[qk, qk, v, v, st, pos, pos, dec, col, col, gsp],
        out_specs=[pl.BlockSpec((RET_CHUNK, 2 * RET_QK + RET_V), lambda j: (nc - 1 - j, 0))],
        out_shape=[jax.ShapeDtypeStruct((t, 2 * RET_QK + RET_V), BF16)],
        scratch=[pltpu.VMEM((RET_HEADS, RET_QK_DIM, RET_V_DIM), F32)],
        args=[q_r, k_r, v_r, d_o, states, cos, sin_s, decay, xi, zeta, gch], name="ret_bwd", exchange=exchange)


def _position():
    return lax.axis_index("x"), lax.axis_index("y"), lax.axis_index("c")


def _gather_exchange(owns, forward_at):
    n = len(owns)

    def copies(ins, outs, send_sems, recv_sems, base):
        x, y, c = _position()
        sibling = (x, y, 1 - c)
        chips = [(1 - x, y), (x, 1 - y), (1 - x, 1 - y)]
        my_chip = 2 * x + y

        def slab(a, chip, hf):
            half = owns[a].shape[0] // 2
            return outs[a].at[chip, pl.ds(hf * half, half), :]

        def copy(k, src, dst, to):
            return pltpu.make_async_remote_copy(src_ref=src, dst_ref=dst, send_sem=send_sems.at[base + k],
                                                recv_sem=recv_sems.at[base + k], device_id=to, device_id_type=MESH)

        first, passed, from_sibling = [], [], []
        for a in range(n):
            half = owns[a].shape[0] // 2
            for k, (cx, cy) in enumerate(chips):
                first.append(copy(6 * a + k, ins[a].at[pl.ds(c * half, half), :], slab(a, my_chip, c), (cx, cy, c)))
                landed = slab(a, 2 * cx + cy, c)
                passed.append(copy(6 * a + 3 + k, landed, landed, sibling))
                theirs = slab(a, 2 * cx + cy, 1 - c)
                from_sibling.append(copy(6 * a + 3 + k, theirs, theirs, sibling))
        return first, passed, from_sibling

    def start(*args):
        first, _, _ = copies(*args)
        for cp in first:
            cp.start()

    def forward(*args):
        first, passed, _ = copies(*args)
        for arrived, cp in zip(first, passed):
            arrived.wait_recv()
            cp.start()

    def finish(*args):
        first, passed, from_sibling = copies(*args)
        for cp in from_sibling:
            cp.wait_recv()
        for cp in first + passed:
            cp.wait_send()

    outs = [jax.ShapeDtypeStruct((N_CHIPS, *a.shape), a.dtype) for a in owns]
    return _Exchange(owns, outs, 6 * n, [(0.0, start), (forward_at, forward), (1.0, finish)])


def _symmetric_exchange(ins, outs, plan):
    n_sems = len(plan([None] * len(ins), [None] * len(outs), 0, 0, 0, dry=True))

    def copies(in_refs, out_refs, send_sems, recv_sems, base):
        x, y, c = _position()
        return [pltpu.make_async_remote_copy(src_ref=src, dst_ref=dst, send_sem=send_sems.at[base + k],
                                             recv_sem=recv_sems.at[base + k], device_id=dev, device_id_type=MESH)
                for k, (src, dst, dev) in enumerate(plan(in_refs, out_refs, x, y, c, dry=False))]

    def start(*args):
        for cp in copies(*args):
            cp.start()

    def finish(*args):
        for cp in copies(*args):
            cp.wait()

    return _Exchange(ins, outs, n_sems, [(0.0, start), (1.0, finish)])


def _pair_exchange(gs):
    def plan(in_refs, out_refs, x, y, c, dry):
        out = []
        for a, g in enumerate(gs):
            half = g.shape[1] // 2
            for k in range(N_CHIPS):
                out.append(None if dry else (in_refs[a].at[k, pl.ds((1 - c) * half, half), :], out_refs[a].at[k],
                                             (x, y, 1 - c)))
        return out

    outs = [jax.ShapeDtypeStruct((g.shape[0], g.shape[1] // 2, g.shape[2]), g.dtype) for g in gs]
    return _symmetric_exchange(gs, outs, plan)


def _pair_sum(g, from_sibling, c_arr, *, tile, name):
    n, rows, width = g.shape
    tiles = (rows // 2) // tile

    def kern(c_ref, g_ref, s_ref, o_ref):
        o_ref[...] = _bf(g_ref[...] + s_ref[...])

    return pl.pallas_call(
        kern,
        grid_spec=pltpu.PrefetchScalarGridSpec(
            num_scalar_prefetch=1, grid=(n, tiles),
            in_specs=[pl.BlockSpec((None, tile, width), lambda k, i, c: (k, c[0] * tiles + i, 0)),
                      pl.BlockSpec((None, tile, width), lambda k, i, c: (k, i, 0))],
            out_specs=pl.BlockSpec((None, tile, width), lambda k, i, c: (k, i, 0))),
        out_shape=jax.ShapeDtypeStruct((n, rows // 2, width), BF16), name=name,
        compiler_params=_params(("parallel", "parallel")),
    )(c_arr, g, from_sibling)


def _scatter_to_owners(hsums):
    def plan(in_refs, out_refs, x, y, c, dry):
        out = []
        for a in range(len(hsums)):
            for k, (cx, cy) in enumerate([(1 - x, y), (x, 1 - y), (1 - x, 1 - y)]):
                out.append(None if dry else (in_refs[a].at[2 * cx + cy], out_refs[a].at[k], (cx, cy, c)))
        return out

    outs = [jax.ShapeDtypeStruct((3, *h.shape[1:]), h.dtype) for h in hsums]
    return _symmetric_exchange(hsums, outs, plan)


def _sum_chips(hsum, parts, chip_arr, *, tile, name):
    n, half, width = parts.shape

    def kern(chip_ref, h_ref, p_ref, o_ref):
        acc = h_ref[...].astype(F32)
        for k in range(n):
            acc = acc + p_ref[k].astype(F32)
        o_ref[...] = acc

    return pl.pallas_call(
        kern,
        grid_spec=pltpu.PrefetchScalarGridSpec(
            num_scalar_prefetch=1, grid=(half // tile,),
            in_specs=[pl.BlockSpec((None, tile, width), lambda i, chip: (chip[0], i, 0)),
                      pl.BlockSpec((n, tile, width), lambda i, chip: (0, i, 0))],
            out_specs=pl.BlockSpec((tile, width), lambda i, chip: (i, 0))),
        out_shape=jax.ShapeDtypeStruct((half, width), F32), name=name,
        compiler_params=_params(("parallel",)),
    )(chip_arr, hsum, parts)


def _share_halves(fhalves):
    def plan(in_refs, out_refs, x, y, c, dry):
        return [None if dry else (in_refs[a], out_refs[a], (x, y, 1 - c)) for a in range(len(fhalves))]

    return _symmetric_exchange(fhalves, [jax.ShapeDtypeStruct(f.shape, f.dtype) for f in fhalves], plan)


def _adamw_math(w, g, m, v):
    m = ADAM_B1 * m + (1.0 - ADAM_B1) * g
    v = ADAM_B2 * v + (1.0 - ADAM_B2) * (g * g)
    m_hat = m / (1.0 - ADAM_B1 ** ADAM_STEP)
    v_hat = v / (1.0 - ADAM_B2 ** ADAM_STEP)
    delta = -ADAM_LR * (m_hat / (jnp.sqrt(v_hat) + ADAM_EPS) + ADAM_WD * w)
    return delta, m, v


def _adamw(mats, g_mine, g_other, c_arr, *, tile, name):
    width = g_mine.shape[1]
    tiles_per_half = g_mine.shape[0] // tile
    n_tiles = [w.shape[0] // tile for w, _, _, _ in mats]
    n_mats = len(mats)

    def kern(c_ref, *refs):
        ins, outs = refs[:5 * n_mats], refs[5 * n_mats:]
        for j, (_, _, _, row_off) in enumerate(mats):
            w_ref, gm_ref, go_ref, m_ref, v_ref = ins[5 * j:5 * j + 5]
            i = jnp.minimum(pl.program_id(0), n_tiles[j] - 1)
            in_my_half = ((row_off // tile + i) // tiles_per_half) == c_ref[0]
            g = jnp.where(in_my_half, gm_ref[...], go_ref[...])
            d, nm, nv = _adamw_math(w_ref[...], g, m_ref[...], v_ref[...])
            for out_ref, val in zip(outs[4 * j:4 * j + 4], (g, d, nm, nv)):
                out_ref[...] = val

    in_specs, out_specs, out_shape, args = [], [], [], []
    for (w, m, v, row_off), nt in zip(mats, n_tiles):
        full = pl.BlockSpec((tile, width), lambda i, c, nt=nt: (jnp.minimum(i, nt - 1), 0))
        half = pl.BlockSpec((tile, width), lambda i, c, nt=nt, first=row_off // tile:
                            ((first + jnp.minimum(i, nt - 1)) % tiles_per_half, 0))
        in_specs += [full, half, half, full, full]
        out_specs += [full] * 4
        out_shape += [jax.ShapeDtypeStruct(w.shape, F32)] * 4
        args += [w, g_mine, g_other, m, v]
    outs = pl.pallas_call(
        kern,
        grid_spec=pltpu.PrefetchScalarGridSpec(num_scalar_prefetch=1, grid=(max(n_tiles),), in_specs=in_specs,
                                               out_specs=out_specs),
        out_shape=out_shape, name=name, compiler_params=_params(("arbitrary",)),
    )(c_arr, *args)
    return [outs[4 * j:4 * j + 4] for j in range(n_mats)]


def _small_step(partials, params):
    slots = ((0, 0, D_MODEL), (1, 0, D_MODEL), (2, 0, HEAD_DIM), (2, 128, HEAD_DIM), (2, 256, N_Q_HEADS))
    loss_slot = (2, 384, 128)

    def body(*refs):
        loss_ref, dg1_ref, dg2_ref, dgq_ref, dgk_ref, dsink_ref = refs[:6]
        p_refs, out_refs = refs[6:21], refs[21:42]
        mine, gathered, send_sems, recv_sems = refs[42:]
        x, y, c = _position()
        me = 4 * x + 2 * y + c
        mine[...] = jnp.zeros_like(mine)
        for (row, lane, n), val in zip(slots + (loss_slot,), (
                jnp.sum(dg1_ref[...], axis=0, keepdims=True), jnp.sum(dg2_ref[...], axis=0, keepdims=True),
                dgq_ref[...], dgk_ref[...], dsink_ref[...], jnp.sum(loss_ref[...], axis=0, keepdims=True))):
            mine[row:row + 1, lane:lane + n] = val
        copies = []
        for k in range(1, N_DEV):
            flip = (k >> 2) & 1, (k >> 1) & 1, k & 1
            to = (x ^ flip[0], y ^ flip[1], c ^ flip[2])
            cp = pltpu.make_async_remote_copy(
                src_ref=mine, dst_ref=gathered.at[me], send_sem=send_sems.at[k - 1], recv_sem=recv_sems.at[k - 1],
                device_id=to, device_id_type=MESH)
            cp.start()
            copies.append(cp)
        gathered[me] = mine[...]
        for k in range(1, N_DEV):
            flip = (k >> 2) & 1, (k >> 1) & 1, k & 1
            src = 4 * (x ^ flip[0]) + 2 * (y ^ flip[1]) + (c ^ flip[2])
            pltpu.make_async_remote_copy(
                src_ref=mine, dst_ref=gathered.at[src], send_sem=send_sems.at[k - 1], recv_sem=recv_sems.at[k - 1],
                device_id=(x, y, c), device_id_type=MESH).wait_recv()
        for cp in copies:
            cp.wait_send()
        total = gathered[0]
        for k in range(1, N_DEV):
            total = total + gathered[k]
        row, lane, n = loss_slot
        out_refs[0][...] = total[row:row + 1, lane:lane + n]
        for i, (row, lane, n) in enumerate(slots):
            g = total[row:row + 1, lane:lane + n]
            d, nm, nv = _adamw_math(p_refs[i][...], g, p_refs[5 + i][...], p_refs[10 + i][...])
            for kind, val in enumerate((g, d, nm, nv)):
                out_refs[1 + 5 * kind + i][...] = val

    vm = pl.BlockSpec(memory_space=pltpu.VMEM)
    shapes = [jax.ShapeDtypeStruct((1, 128), F32)] + [jax.ShapeDtypeStruct((1, n), F32) for _, _, n in slots] * 4
    return pl.pallas_call(
        body, in_specs=[vm] * 21, out_specs=[vm] * 21, out_shape=shapes,
        scratch_shapes=[pltpu.VMEM((SMALL_ROWS, D_MODEL), F32), pltpu.VMEM((N_DEV, SMALL_ROWS, D_MODEL), F32),
                        pltpu.SemaphoreType.DMA((N_DEV - 1,)), pltpu.SemaphoreType.DMA((N_DEV - 1,))],
        name="small_step",
    )(*partials, *params)


def _with_own(gathered, own, my_chip):
    return lax.dynamic_update_slice(gathered, own[None], (my_chip, 0, 0))


def kernel(x, norm_mix_gain, w_in, q_norm_gain, k_norm_gain, attn_sinks, w_branch_attn, w_branch_ret, w_out, norm_ffn_gain, w_ffn_gate, w_ffn_up, w_ffn_down, loss_target, m_norm_mix_gain, m_w_in, m_q_norm_gain, m_k_norm_gain, m_attn_sinks, m_w_branch_attn, m_w_branch_ret, m_w_out, m_norm_ffn_gain, m_w_ffn_gate, m_w_ffn_up, m_w_ffn_down, v_norm_mix_gain, v_w_in, v_q_norm_gain, v_k_norm_gain, v_attn_sinks, v_w_branch_attn, v_w_branch_ret, v_w_out, v_norm_ffn_gain, v_w_ffn_gate, v_w_ffn_up, v_w_ffn_down):
    my_chip = 2 * lax.axis_index("x") + lax.axis_index("y")
    c_arr = lax.axis_index("c").astype(jnp.int32).reshape(1)
    chip_arr = my_chip.astype(jnp.int32).reshape(1)
    x_t, target = x[0], loss_target[0]
    g1, g2, gq, gk, sinks = norm_mix_gain, norm_ffn_gain, q_norm_gain, k_norm_gain, attn_sinks

    tr = lambda a: jnp.transpose(a[0])
    own_w_in = _bf(tr(w_in))
    own_rest = [_bf(a) for a in (tr(w_ffn_gate), tr(w_ffn_up), w_ffn_down[0], w_branch_attn[0], w_branch_ret[0],
                                 w_out[0])]
    tables, (got_w_in,) = _ret_tables(x_t.shape[0], _gather_exchange([own_w_in], 0.9))
    w_in_t = _with_own(got_w_in, own_w_in, my_chip).reshape(D_IN, D_MODEL)
    h1, q_a, kv_a, q_r, k_r, v_r, g_r, z_a, z_r, *got_rest = _proj_fwd(x_t, g1, w_in_t, _gather_exchange(own_rest, 0.8))
    wg_t, wu_t, wd, wba, wbr, wout = [_with_own(got, own, my_chip).reshape(N_CHIPS * own.shape[0], D_MODEL)
                                      for got, own in zip(got_rest, own_rest)]

    gq_col, gk_col = gq.reshape(HEAD_DIM, 1), gk.reshape(HEAD_DIM, 1)
    attn, probs, sink_probs, ret_normed, ret_rrms, ret, states = _fused(
        [_attn_fwd(q_a, kv_a, gq_col, gk, sinks), _ret_fwd(q_r, k_r, v_r, g_r, tables)],
        grid=(x_t.shape[0] // BLOCK,), name="mixers_fwd")
    ba, br, merged, x1, h2 = _mix_fwd(attn, ret, z_a, z_r, x_t, wba, wbr, wout, g2)
    act, dgate, dup, dyb, dx1, dx1b, loss_p, dg2_p = _ffn_fwd_bwd(h2, x1, target, wg_t, wu_t, wd, g2)

    def pairs(row0, rows):
        return lambda i: [(h * rows, rows, (2 * i + h, pl.ds(row0, rows), slice(None))) for h in range(2)]

    f_block = jax.ShapeDtypeStruct((N_CHIPS, 3 * FF_SH, D_MODEL), F32)
    f_block, = _dw(dgate, h2, tm=2 * FF_SH, place=pairs(0, FF_SH), buf=f_block, name="dw_gate")
    f_block, = _dw(dup, h2, tm=2 * FF_SH, place=pairs(FF_SH, FF_SH), buf=f_block, name="dw_up")
    f_block, = _dw(act, dyb, tm=2 * FF_SH, place=pairs(2 * FF_SH, FF_SH), buf=f_block, name="dw_down")
    (dba, dbr, d_attn, d_o, d_gz, sib_ffn) = _mix_bwd(
        dx1b, z_a, z_r, ba, br, g_r, ret_normed, ret_rrms, wout, wba, wbr, _pair_exchange([f_block]))
    f_sum = _pair_sum(f_block, sib_ffn, c_arr, tile=528, name="pair_sum_ffn")

    def quarters(row0, rows):
        return lambda i: [(k * rows, rows, (k, pl.ds(row0, rows), slice(None))) for k in range(N_CHIPS)]

    m_block = jax.ShapeDtypeStruct((N_CHIPS, D_MODEL, D_MODEL), F32)
    m_block, = _dw(attn, dba, tm=ATT_Q, place=quarters(0, 256), buf=m_block, name="dw_ba")
    m_block, = _dw(ret, dbr, tm=D_MODEL, place=pairs(256, 512), buf=m_block, name="dw_br")
    m_block, = _dw(merged, dx1b, tm=D_MODEL, place=quarters(768, 256), buf=m_block, name="dw_out")

    def w_in_rows(off, w):
        tm = min(w, D_MODEL)
        return dict(tm=tm, place=lambda i: [(0, tm, (pl.ds(off + i * tm, tm), slice(None)))])

    w_block = jax.ShapeDtypeStruct((D_IN, D_MODEL), F32)
    w_block, sib_mix = _dw(d_gz, h1, buf=w_block, name="dw_in_gz", exchange=_pair_exchange([m_block]),
                           **w_in_rows(P_GR[0], d_gz.shape[1]))
    m_sum = _pair_sum(m_block, sib_mix, c_arr, tile=256, name="pair_sum_mix")

    d_ret, got_ffn_sums = _ret_bwd(q_r, k_r, v_r, d_o, states, tables, _scatter_to_owners([f_sum]))
    ffn_half = _sum_chips(f_sum, got_ffn_sums, chip_arr, tile=528, name="sum_chips_ffn")
    w_block, = _dw(d_ret, h1, buf=w_block, name="dw_in_ret", **w_in_rows(P_QR[0], d_ret.shape[1]))

    (dq_a, dkv_a, dgq, dgk, dsinks, got_mix_sums, ffn_other) = _attn_bwd(
        q_a, kv_a, d_attn, probs, sink_probs, gq_col, gk, gk_col,
        _merge_exchanges(_scatter_to_owners([m_sum]), _share_halves([ffn_half])))
    dgq = dgq.reshape(1, HEAD_DIM)
    mix_half = _sum_chips(m_sum, got_mix_sums, chip_arr, tile=256, name="sum_chips_mix")
    w_block, mix_other = _dw(dq_a, h1, buf=w_block, name="dw_in_q", exchange=_share_halves([mix_half]),
                             **w_in_rows(*P_QA))
    w_block, = _dw(dkv_a, h1, buf=w_block, name="dw_in_kv", **w_in_rows(*P_KVA))

    w_block = w_block.reshape(N_CHIPS, W_IN_SH, D_MODEL)
    sib_w_in, = _run_exchange(_pair_exchange([w_block]), "pair_exchange_w_in")
    w_sum = _pair_sum(w_block, sib_w_in, c_arr, tile=592, name="pair_sum_w_in")
    d_pieces = [dq_a, dkv_a, d_ret, d_gz]
    grad_x, dg1_p, got_w_in_sums = _proj_bwd(d_pieces, x_t, dx1, w_in_t, g1, _scatter_to_owners([w_sum]))
    w_in_half = _sum_chips(w_sum, got_w_in_sums, chip_arr, tile=592, name="sum_chips_w_in")
    w_in_other, = _run_exchange(_share_halves([w_in_half]), "share_halves_w_in")

    def update(name, g_half, g_other, tile, mats):
        outs = _adamw([tuple(tr(a) if t else a[0] for a in wmv) + (off,) for _, *wmv, off, t in mats],
                      g_half, g_other, c_arr, tile=tile, name=f"adamw_{name}")
        return {key: [jnp.transpose(o) if t else o for o in res] for (key, _, _, _, _, t), res in zip(mats, outs)}

    big = {
        **update("w_in", w_in_half, w_in_other, 592, [("w_in", w_in, m_w_in, v_w_in, 0, True)]),
        **update("ffn", ffn_half, ffn_other, 176, [
            ("wg", w_ffn_gate, m_w_ffn_gate, v_w_ffn_gate, 0, True),
            ("wu", w_ffn_up, m_w_ffn_up, v_w_ffn_up, FF_SH, True),
            ("wd", w_ffn_down, m_w_ffn_down, v_w_ffn_down, 2 * FF_SH, False)]),
        **update("mix", mix_half, mix_other, 128, [
            ("wba", w_branch_attn, m_w_branch_attn, v_w_branch_attn, 0, False),
            ("wbr", w_branch_ret, m_w_branch_ret, v_w_branch_ret, 256, False),
            ("wout", w_out, m_w_out, v_w_out, 768, False)])}

    loss_row, *small = _small_step(
        [loss_p.reshape(-1, 128), dg1_p.reshape(-1, D_MODEL), dg2_p.reshape(-1, D_MODEL), dgq, dgk, dsinks],
        [norm_mix_gain, norm_ffn_gain, q_norm_gain, k_norm_gain, attn_sinks,
         m_norm_mix_gain, m_norm_ffn_gain, m_q_norm_gain, m_k_norm_gain, m_attn_sinks,
         v_norm_mix_gain, v_norm_ffn_gain, v_q_norm_gain, v_k_norm_gain, v_attn_sinks])
    loss = loss_row[0, 0]

    def leaves(i):
        b = [big[n][i][None] for n in ("w_in", "wba", "wbr", "wout", "wg", "wu", "wd")]
        s1, s2, sq, sk, ss = small[5 * i:5 * i + 5]
        return [s1, b[0], sq, sk, ss, b[1], b[2], b[3], s2, b[4], b[5], b[6]]

    return (loss, grad_x[None], *leaves(0), *leaves(1), *leaves(2), *leaves(3))
```

```python
import jax
import jax.numpy as jnp
from jax import lax
from jax.experimental import pallas as pl
from jax.experimental.pallas import tpu as pltpu

F32 = jnp.float32
BF16 = jnp.bfloat16
MESH = pl.DeviceIdType.MESH

D_MODEL = 1024
EPS = 1e-6
HEAD_DIM = 64
N_Q_HEADS = 16
N_KV_HEADS = 2
GROUP = 8
BLOCK = 128
RET_HEADS = 4
RET_QK_DIM = 256
RET_V_DIM = 512
RET_CHUNK = 128
RET_ROT_BASE = 10000.0
D_FF = 2816
ATT_Q = N_Q_HEADS * HEAD_DIM
ATT_KV = N_KV_HEADS * HEAD_DIM
RET_QK = RET_HEADS * RET_QK_DIM
RET_V = RET_HEADS * RET_V_DIM
D_IN = 9472
ADAM_LR = 0.001
ADAM_B1 = 0.9
ADAM_B2 = 0.999
ADAM_EPS = 1e-08
ADAM_WD = 0.01
ADAM_STEP = 10

N_CHIPS = 4
N_DEV = 8
VMEM_LIMIT_BYTES = 60 * 1024 * 1024

P_QA = (0, 1024)
P_KVA = (1024, 256)
P_QR = (1280, 1024)
P_KR = (2304, 1024)
P_VR = (3328, 2048)
P_GR = (5376, 2048)
P_ZA = (7424, 1024)
P_ZR = (8448, 1024)

W_IN_SH = D_IN // N_CHIPS
FF_SH = D_FF // N_CHIPS

SMALL_ROWS = 8


def _dot(a, b):
    return jnp.dot(a, b, preferred_element_type=F32)


def _dot_nt(a, b):
    return lax.dot_general(a, b, (((1,), (1,)), ((), ())), preferred_element_type=F32)


def _dot_tn(a, b):
    return lax.dot_general(a, b, (((0,), (0,)), ((), ())), preferred_element_type=F32)


def _bf(x):
    return x.astype(BF16)


def _rms_stats(x):
    r = lax.rsqrt(jnp.mean(x * x, axis=-1, keepdims=True) + EPS)
    return r, x * r


def _rms_bwd(dy, xhat, r, gain):
    u = dy * gain
    dx = r * (u - xhat * jnp.mean(u * xhat, axis=-1, keepdims=True))
    return dx, dy * xhat


def _params(sem):
    return pltpu.CompilerParams(dimension_semantics=sem, vmem_limit_bytes=VMEM_LIMIT_BYTES)


_ANY = pl.BlockSpec(memory_space=pl.ANY)


class _Exchange:
    def __init__(self, ins, outs, n_sems, phases):
        self.ins, self.outs, self.n_sems, self.phases = list(ins), list(outs), n_sems, list(phases)


def _pallas(kern, *, grid, in_specs, out_specs, out_shape, args, name, scratch=(), exchange=None, aliases=None):
    aliases = aliases or {}
    if exchange is None:
        return pl.pallas_call(
            kern, grid=grid, in_specs=in_specs, out_specs=out_specs, out_shape=out_shape, name=name,
            scratch_shapes=list(scratch), input_output_aliases=aliases,
            compiler_params=_params(("arbitrary",) * len(grid)))(*args)
    n_in, n_out, n_sc = len(in_specs), len(out_specs), len(scratch)
    n_xi, n_xo = len(exchange.ins), len(exchange.outs)
    n_steps = 1
    for g in grid:
        n_steps *= g

    def wrapped(*refs):
        ins, refs = refs[:n_in], refs[n_in:]
        x_ins, refs = refs[:n_xi], refs[n_xi:]
        outs, refs = refs[:n_out], refs[n_out:]
        x_outs, refs = refs[:n_xo], refs[n_xo:]
        scr, (send_sems, recv_sems) = refs[:n_sc], refs[n_sc:]
        step = pl.program_id(0)
        for d in range(1, len(grid)):
            step = step * grid[d] + pl.program_id(d)
        for frac, fn in exchange.phases:
            at = min(int(frac * n_steps), n_steps - 1)

            @pl.when(step == at)
            def _(fn=fn):
                fn(x_ins, x_outs, send_sems, recv_sems, 0)

        kern(*ins, *outs, *scr)

    sems = [pltpu.SemaphoreType.DMA((exchange.n_sems,)), pltpu.SemaphoreType.DMA((exchange.n_sems,))]
    return pl.pallas_call(
        wrapped, grid=grid, in_specs=list(in_specs) + [_ANY] * n_xi, out_specs=list(out_specs) + [_ANY] * n_xo,
        out_shape=list(out_shape) + exchange.outs, name=name, scratch_shapes=list(scratch) + sems,
        input_output_aliases=aliases, compiler_params=_params(("arbitrary",) * len(grid)))(*args, *exchange.ins)


def _run_exchange(exchange, name):
    def body(*refs):
        n_i, n_o = len(exchange.ins), len(exchange.outs)
        for _, fn in exchange.phases:
            fn(refs[:n_i], refs[n_i:n_i + n_o], refs[n_i + n_o], refs[n_i + n_o + 1], 0)

    sems = [pltpu.SemaphoreType.DMA((exchange.n_sems,)), pltpu.SemaphoreType.DMA((exchange.n_sems,))]
    return pl.pallas_call(body, in_specs=[_ANY] * len(exchange.ins), out_specs=[_ANY] * len(exchange.outs),
                          out_shape=exchange.outs, scratch_shapes=sems, name=name)(*exchange.ins)


def _fused(parts, *, grid, name, exchange=None):
    counts = [(len(p["in_specs"]), len(p["out_specs"]), len(p["scratch"])) for p in parts]
    n_in, n_out = sum(c[0] for c in counts), sum(c[1] for c in counts)

    def kern(*refs):
        ins, outs, scr = refs[:n_in], refs[n_in:n_in + n_out], refs[n_in + n_out:]
        i0 = o0 = s0 = 0
        for p, (ni, no, ns) in zip(parts, counts):
            p["kern"](*ins[i0:i0 + ni], *outs[o0:o0 + no], *scr[s0:s0 + ns])
            i0, o0, s0 = i0 + ni, o0 + no, s0 + ns

    cat = lambda key: [a for p in parts for a in p[key]]
    return _pallas(kern, grid=grid, in_specs=cat("in_specs"), out_specs=cat("out_specs"), out_shape=cat("out_shape"),
                   scratch=cat("scratch"), args=cat("args"), name=name, exchange=exchange)


def _row_call(body, *, tm, row_ins, res_ins, row_outs, part_outs=(), name, exchange=None):
    t = row_ins[0].shape[0]
    n_tiles = t // tm
    in_specs = [pl.BlockSpec((tm, a.shape[1]), lambda i: (i, 0)) for a in row_ins]
    in_specs += [pl.BlockSpec(a.shape, lambda i: (0, 0), pipeline_mode=pl.Buffered(1)) for a in res_ins]
    out_shape = [jax.ShapeDtypeStruct((t, w), dt) for (w, dt) in row_outs]
    out_shape += [jax.ShapeDtypeStruct((n_tiles, 1, w), F32) for w in part_outs]
    out_specs = [pl.BlockSpec((tm, w), lambda i: (i, 0)) for (w, _) in row_outs]
    out_specs += [pl.BlockSpec((1, 1, w), lambda i: (i, 0, 0)) for w in part_outs]
    n_ri, n_re, n_ro = len(row_ins), len(res_ins), len(row_outs)

    def kern(*refs):
        body(refs[:n_ri], refs[n_ri:n_ri + n_re], refs[n_ri + n_re:n_ri + n_re + n_ro], refs[n_ri + n_re + n_ro:])

    return _pallas(kern, grid=(n_tiles,), in_specs=in_specs, out_specs=out_specs, out_shape=out_shape,
                   args=[*row_ins, *res_ins], name=name, exchange=exchange)


def _proj_fwd(x, g1, w_in_t, exchange):
    pieces = ((P_QA, F32), (P_KVA, F32), (P_QR, F32), (P_KR, F32), (P_VR, BF16), (P_GR, F32), (P_ZA, F32), (P_ZR, F32))

    def body(ri, re, ro, po):
        x_t = ri[0][...]
        r, xhat = _rms_stats(x_t)
        hb = _bf(xhat * re[0][...])
        ro[0][...] = hb
        for k, ((off, w), dt) in enumerate(pieces):
            ro[1 + k][...] = _dot_nt(hb, re[1][off:off + w, :]).astype(dt)

    outs = [(D_MODEL, BF16)] + [(w, dt) for ((_, w), dt) in pieces]
    return _row_call(body, tm=256, row_ins=[x], res_ins=[g1, w_in_t], row_outs=outs, name="proj_fwd",
                     exchange=exchange)


def _mix_fwd(attn, ret, z_a, z_r, x, wba, wbr, wout, g2):
    def body(ri, re, ro, po):
        ba = _dot(ri[0][...], re[0][...])
        br = _dot(ri[1][...], re[1][...])
        m = jax.nn.sigmoid(ri[2][...]) * ba + jax.nn.sigmoid(ri[3][...]) * br
        mb = _bf(m)
        x1 = ri[4][...] + _dot(mb, re[2][...])
        r, xhat = _rms_stats(x1)
        ro[0][...] = ba
        ro[1][...] = br
        ro[2][...] = mb
        ro[3][...] = x1
        ro[4][...] = _bf(xhat * re[3][...])

    outs = [(D_MODEL, F32), (D_MODEL, F32), (D_MODEL, BF16), (D_MODEL, F32), (D_MODEL, BF16)]
    return _row_call(body, tm=512, row_ins=[attn, ret, z_a, z_r, x], res_ins=[wba, wbr, wout, g2], row_outs=outs,
                     name="mix_fwd")


def _ffn_fwd_bwd(h2, x1, target, wg_t, wu_t, wd, g2):
    def body(ri, re, ro, po):
        h2_t = ri[0][...]
        x1_t = ri[1][...]
        gate = _dot_nt(h2_t, re[0][...])
        up = _dot_nt(h2_t, re[1][...])
        sg = jax.nn.sigmoid(gate)
        sl = gate * sg
        actb = _bf(sl * up)
        ro[0][...] = actb
        y = x1_t + _dot(actb, re[2][...])
        e = y - ri[2][...]
        po[0][0] = jnp.broadcast_to(0.5 * jnp.sum(jnp.sum(e * e, axis=1, keepdims=True), axis=0, keepdims=True)
                                    * (1.0 / D_MODEL), (1, 128))
        dy = e * (1.0 / D_MODEL)
        dyb = _bf(dy)
        ro[3][...] = dyb
        dact = _dot_nt(dyb, re[2][...])
        dupb = _bf(dact * sl)
        dgateb = _bf(dact * up * (sg * (1.0 + gate * (1.0 - sg))))
        ro[1][...] = dgateb
        ro[2][...] = dupb
        dh2 = _dot(dgateb, re[0][...]) + _dot(dupb, re[1][...])
        r, xhat = _rms_stats(x1_t)
        dxn, dgain = _rms_bwd(dh2, xhat, r, re[3][...])
        dx1 = dy + dxn
        ro[4][...] = dx1
        ro[5][...] = _bf(dx1)
        po[1][0] = jnp.sum(dgain, axis=0, keepdims=True)

    outs = [(D_FF, BF16), (D_FF, BF16), (D_FF, BF16), (D_MODEL, BF16), (D_MODEL, F32), (D_MODEL, BF16)]
    return _row_call(body, tm=256, row_ins=[h2, x1, target], res_ins=[wg_t, wu_t, wd, g2], row_outs=outs,
                     part_outs=(128, D_MODEL), name="ffn_fwd_bwd")


def _mix_bwd(dx1b, z_a, z_r, ba, br, g_r, o_ret, wout, wba, wbr, exchange):
    def body(ri, re, ro, po):
        dm = _dot_nt(ri[0][...], re[0][...])
        sa = jax.nn.sigmoid(ri[1][...])
        sr = jax.nn.sigmoid(ri[2][...])
        dbab = _bf(sa * dm)
        dbrb = _bf(sr * dm)
        ro[0][...] = dbab
        ro[1][...] = dbrb
        ro[4][:, RET_V:RET_V + D_MODEL] = _bf(dm * ri[3][...] * (sa * (1.0 - sa)))
        ro[4][:, RET_V + D_MODEL:RET_V + 2 * D_MODEL] = _bf(dm * ri[4][...] * (sr * (1.0 - sr)))
        ro[2][...] = _bf(_dot_nt(dbab, re[1][...]))
        dret = _dot_nt(dbrb, re[2][...])
        for h in range(RET_HEADS):
            cols = slice(h * RET_V_DIM, (h + 1) * RET_V_DIM)
            g = ri[5][:, cols]
            r, rn = _rms_stats(ri[6][:, cols])
            sg = jax.nn.sigmoid(g)
            dret_h = dret[:, cols]
            d_rn = dret_h * (g * sg)
            ro[4][:, cols] = _bf(dret_h * rn * (sg * (1.0 + g * (1.0 - sg))))
            ro[3][:, cols] = r * (d_rn - rn * jnp.mean(d_rn * rn, axis=-1, keepdims=True))

    outs = [(D_MODEL, BF16), (D_MODEL, BF16), (ATT_Q, BF16), (RET_V, F32), (RET_V + 2 * D_MODEL, BF16)]
    return _row_call(body, tm=256, row_ins=[dx1b, z_a, z_r, ba, br, g_r, o_ret], res_ins=[wout, wba, wbr],
                     row_outs=outs, name="mix_bwd", exchange=exchange)


def _proj_bwd(d_pieces, x, dx1, w_in_t, g1, exchange):
    widths = [p.shape[1] for p in d_pieces]
    groups = [(sum(widths[:k]), w) for k, w in enumerate(widths)]
    n_p = len(groups)

    def body(ri, re, ro, po):
        dh = None
        for k, (off, w) in enumerate(groups):
            term = _dot(ri[k][...], re[0][off:off + w, :])
            dh = term if dh is None else dh + term
        r, xhat = _rms_stats(ri[n_p][...])
        dxn, dgain = _rms_bwd(dh, xhat, r, re[1][...])
        ro[0][...] = ri[n_p + 1][...] + dxn
        po[0][0] = jnp.sum(dgain, axis=0, keepdims=True)

    return _row_call(body, tm=512, row_ins=[*d_pieces, x, dx1], res_ins=[w_in_t, g1], row_outs=[(D_MODEL, F32)],
                     part_outs=(D_MODEL,), name="proj_bwd", exchange=exchange)


def _dw(a, b, *, tm, place, buf, name, exchange=None):
    t, m = a.shape
    n = b.shape[1]
    tk = min(2048, t)
    n_i, n_k = m // tm, t // tk
    fresh = isinstance(buf, jax.ShapeDtypeStruct)
    n_copies = len(place(0))

    def kern(a_ref, b_ref, *rest):
        out_ref, acc, sems = rest[-3:]
        i, k = pl.program_id(0), pl.program_id(1)
        part = _dot_tn(a_ref[...], b_ref[...])

        @pl.when(k == 0)
        def _():
            acc[i] = part

        @pl.when(k > 0)
        def _():
            acc[i] += part

        def copies(tile):
            return [pltpu.make_async_copy(acc.at[tile, pl.ds(r0, rows), :], out_ref.at[idx], sems.at[tile * n_copies + c])
                    for c, (r0, rows, idx) in enumerate(place(tile))]

        for tile in range(n_i):
            @pl.when((i == tile) & (k == n_k - 1))
            def _(tile=tile):
                for cp in copies(tile):
                    cp.start()

        @pl.when((i == n_i - 1) & (k == n_k - 1))
        def _():
            for tile in range(n_i):
                for cp in copies(tile):
                    cp.wait()

    in_specs = [pl.BlockSpec((tk, tm), lambda i, k: (k, i)), pl.BlockSpec((tk, n), lambda i, k: (k, 0))]
    shape = buf if fresh else jax.ShapeDtypeStruct(buf.shape, buf.dtype)
    return _pallas(
        kern, grid=(n_i, n_k), in_specs=in_specs + ([] if fresh else [_ANY]), out_specs=[_ANY], out_shape=[shape],
        scratch=[pltpu.VMEM((n_i, tm, n), F32), pltpu.SemaphoreType.DMA((n_i * n_copies,))],
        args=[a, b] + ([] if fresh else [buf]), aliases=None if fresh else {2: 0}, name=name, exchange=exchange)


def _heads_to_lanes(x3):
    return jnp.concatenate([x3[g] for g in range(GROUP)], axis=1)


def _lanes_to_heads(xt):
    return jnp.concatenate([xt[:, g * BLOCK:(g + 1) * BLOCK] for g in range(GROUP)], axis=0)


def _attn_queries(kvh, q_ref, gq_col):
    cols = slice(kvh * GROUP * HEAD_DIM, (kvh + 1) * GROUP * HEAD_DIM)
    q3 = q_ref[:, cols].T.reshape(GROUP, HEAD_DIM, BLOCK)
    rq = lax.rsqrt(jnp.mean(q3 * q3, axis=1, keepdims=True) + EPS)
    qhat = q3 * rq
    return qhat, rq, _heads_to_lanes(_bf(qhat * (gq_col * (HEAD_DIM ** -0.5))))


def _from_prev():
    j = lax.broadcasted_iota(jnp.int32, (BLOCK, GROUP * BLOCK), 0)
    i = lax.broadcasted_iota(jnp.int32, (BLOCK, GROUP * BLOCK), 1) & (BLOCK - 1)
    return j > i


def _attn_probs(n, kvh, qts, kvp_ref, kvc_ref, gk, sink_ref):
    kcols = slice(kvh * HEAD_DIM, (kvh + 1) * HEAD_DIM)
    k = jnp.concatenate([kvp_ref[:, kcols], kvc_ref[:, kcols]], axis=0)
    rk, khat = _rms_stats(k)
    st = _dot(_bf(khat * gk), qts)
    f = jnp.where(_from_prev(), jnp.where(n > 0, st[0:BLOCK], -1e30), st[BLOCK:2 * BLOCK])
    sink = jnp.concatenate([jnp.broadcast_to(sink_ref[0:1, kvh * GROUP + g:kvh * GROUP + g + 1], (1, BLOCK))
                            for g in range(GROUP)], axis=1)
    m = jnp.maximum(jnp.max(f, axis=0, keepdims=True), sink)
    e = jnp.exp(f - m)
    es = jnp.exp(sink - m)
    inv = 1.0 / (jnp.sum(e, axis=0, keepdims=True) + es)
    return e * inv, es * inv


def _unfold(from_prev, xf):
    return _bf(jnp.concatenate([jnp.where(from_prev, xf, 0.0), jnp.where(from_prev, 0.0, xf)], axis=0))


def _attn_fwd(q_a, kv_a, gq_col, gk, sinks):
    t = q_a.shape[0]
    nb = t // BLOCK

    def kern(q_ref, kvp_ref, kvc_ref, gq_ref, gk_ref, sink_ref, o_ref, pf_ref, ps_ref):
        n = pl.program_id(0)
        kvt = jnp.concatenate([kvp_ref[...].T, kvc_ref[...].T], axis=1)
        for kvh in range(N_KV_HEADS):
            _, _, qts = _attn_queries(kvh, q_ref, gq_ref[...])
            pf, psink = _attn_probs(n, kvh, qts, kvp_ref, kvc_ref, gk_ref[...], sink_ref)
            lanes = slice(kvh * GROUP * BLOCK, (kvh + 1) * GROUP * BLOCK)
            pf_ref[:, lanes] = pf
            ps_ref[:, lanes] = psink
            vt = _bf(kvt[ATT_KV + kvh * HEAD_DIM:ATT_KV + (kvh + 1) * HEAD_DIM, :])
            out_t = _dot(vt, _unfold(_from_prev(), pf))
            cols = slice(kvh * GROUP * HEAD_DIM, (kvh + 1) * GROUP * HEAD_DIM)
            o_ref[:, cols] = _bf(_lanes_to_heads(out_t).T)

    small = lambda a: pl.BlockSpec(a.shape, lambda n: (0, 0))
    folded = N_KV_HEADS * GROUP * BLOCK
    return dict(
        kern=kern,
        in_specs=[pl.BlockSpec((BLOCK, ATT_Q), lambda n: (n, 0)),
                  pl.BlockSpec((BLOCK, 2 * ATT_KV), lambda n: (jnp.maximum(n - 1, 0), 0)),
                  pl.BlockSpec((BLOCK, 2 * ATT_KV), lambda n: (n, 0)),
                  small(gq_col), small(gk), small(sinks)],
        out_specs=[pl.BlockSpec((BLOCK, ATT_Q), lambda n: (n, 0)), pl.BlockSpec((BLOCK, folded), lambda n: (n, 0)),
                   pl.BlockSpec((None, 1, folded), lambda n: (n, 0, 0))],
        out_shape=[jax.ShapeDtypeStruct((t, ATT_Q), BF16), jax.ShapeDtypeStruct((t, folded), F32),
                   jax.ShapeDtypeStruct((nb, 1, folded), F32)],
        scratch=[], args=[q_a, kv_a, kv_a, gq_col, gk, sinks])


def _attn_bwd(q_a, kv_a, d_attn, probs, sink_probs, gq_col, gk, gk_col):
    t = q_a.shape[0]
    nb = t // BLOCK

    def kern(q_ref, kvp_ref, kvc_ref, do_ref, pf_ref, ps_ref, gq_ref, gk_ref, gkc_ref,
             dq_ref, dkv_ref, dgq_ref, dgk_ref, dsink_ref, band_k, band_v, carry_k, carry_v):
        n = pl.program_id(0)
        gq_v = gq_ref[...]
        gk_v = gk_ref[...]

        @pl.when(n == 0)
        def _():
            carry_k[...] = jnp.zeros_like(carry_k)
            carry_v[...] = jnp.zeros_like(carry_v)
            dgq_ref[...] = jnp.zeros_like(dgq_ref)
            dgk_ref[...] = jnp.zeros_like(dgk_ref)
            dsink_ref[...] = jnp.zeros_like(dsink_ref)

        @pl.when(n == nb)
        def _():
            band_k[...] = jnp.zeros_like(band_k)
            band_v[...] = jnp.zeros_like(band_v)

        @pl.when(n < nb)
        def _():
            lane16 = lax.broadcasted_iota(jnp.int32, (1, N_Q_HEADS), 1)
            dsink = jnp.zeros((1, N_Q_HEADS), F32)
            dgq = jnp.zeros((HEAD_DIM, 1), F32)
            gk_col = gkc_ref[...]
            kvt = jnp.concatenate([kvp_ref[...].T, kvc_ref[...].T], axis=1)
            from_prev = _from_prev()
            for kvh in range(N_KV_HEADS):
                qhat, rq, qts = _attn_queries(kvh, q_ref, gq_v)
                lanes = slice(kvh * GROUP * BLOCK, (kvh + 1) * GROUP * BLOCK)
                pf = pf_ref[:, lanes]
                cols = slice(kvh * GROUP * HEAD_DIM, (kvh + 1) * GROUP * HEAD_DIM)
                vcols = slice(ATT_KV + kvh * HEAD_DIM, ATT_KV + (kvh + 1) * HEAD_DIM)
                dot = _heads_to_lanes(_bf(do_ref[:, cols].astype(F32).T.reshape(GROUP, HEAD_DIM, BLOCK)))
                vb = _bf(jnp.concatenate([kvp_ref[:, vcols], kvc_ref[:, vcols]], axis=0))
                dpt = _dot(vb, dot)
                dpf = jnp.where(from_prev, dpt[0:BLOCK], dpt[BLOCK:2 * BLOCK])
                delta = jnp.sum(pf * dpf, axis=0, keepdims=True)
                dst = _unfold(from_prev, pf * (dpf - delta))
                dsk = ps_ref[:, lanes] * delta
                for g in range(GROUP):
                    tot = jnp.sum(dsk[:, g * BLOCK:(g + 1) * BLOCK], axis=1, keepdims=True)
                    dsink = dsink - jnp.where(lane16 == kvh * GROUP + g, tot, 0.0)
                kt = kvt[kvh * HEAD_DIM:(kvh + 1) * HEAD_DIM, :]
                knt = _bf(kt * lax.rsqrt(jnp.mean(kt * kt, axis=0, keepdims=True) + EPS) * gk_col)
                dqn = (_dot(knt, dst) * (HEAD_DIM ** -0.5))
                band_k[kvh] = _dot_nt(dst, qts)
                band_v[kvh] = _dot_nt(_unfold(from_prev, pf), dot)
                dqn3 = _lanes_to_heads(dqn).reshape(GROUP, HEAD_DIM, BLOCK)
                u = dqn3 * gq_v
                dq3 = rq * (u - qhat * jnp.mean(u * qhat, axis=1, keepdims=True))
                dgq = dgq + jnp.sum(jnp.sum(dqn3 * qhat, axis=0), axis=1, keepdims=True)
                dq_ref[:, cols] = _bf(dq3.reshape(GROUP * HEAD_DIM, BLOCK).T)
            dsink_ref[...] += dsink
            dgq_ref[...] += dgq

        dgk = jnp.zeros((1, HEAD_DIM), F32)
        for kvh in range(N_KV_HEADS):
            kcols = slice(kvh * HEAD_DIM, (kvh + 1) * HEAD_DIM)
            vcols = slice(ATT_KV + kvh * HEAD_DIM, ATT_KV + (kvh + 1) * HEAD_DIM)
            dkn = carry_k[kvh] + band_k[kvh, 0:BLOCK, :]
            dv = carry_v[kvh] + band_v[kvh, 0:BLOCK, :]
            rk, khat = _rms_stats(kvp_ref[:, kcols])
            dk, dgain = _rms_bwd(dkn, khat, rk, gk_v)
            dgk = dgk + jnp.sum(dgain, axis=0, keepdims=True)
            dkv_ref[:, kcols] = _bf(dk)
            dkv_ref[:, vcols] = _bf(dv)
            carry_k[kvh] = band_k[kvh, BLOCK:2 * BLOCK, :]
            carry_v[kvh] = band_v[kvh, BLOCK:2 * BLOCK, :]
        dgk_ref[...] += dgk

    small = lambda a: pl.BlockSpec(a.shape, lambda n: (0, 0))
    last = nb - 1
    return dict(
        kern=kern,
        in_specs=[pl.BlockSpec((BLOCK, ATT_Q), lambda n: (jnp.minimum(n, last), 0)),
                  pl.BlockSpec((BLOCK, 2 * ATT_KV), lambda n: (jnp.maximum(n - 1, 0), 0)),
                  pl.BlockSpec((BLOCK, 2 * ATT_KV), lambda n: (jnp.minimum(n, last), 0)),
                  pl.BlockSpec((BLOCK, ATT_Q), lambda n: (jnp.minimum(n, last), 0)),
                  pl.BlockSpec((BLOCK, probs.shape[1]), lambda n: (jnp.minimum(n, last), 0)),
                  pl.BlockSpec((None, 1, probs.shape[1]), lambda n: (jnp.minimum(n, last), 0, 0)),
                  small(gq_col), small(gk), small(gk_col)],
        out_specs=[pl.BlockSpec((BLOCK, ATT_Q), lambda n: (jnp.minimum(n, last), 0)),
                   pl.BlockSpec((BLOCK, 2 * ATT_KV), lambda n: (jnp.maximum(n - 1, 0), 0)),
                   pl.BlockSpec((HEAD_DIM, 1), lambda n: (0, 0)),
                   pl.BlockSpec((1, HEAD_DIM), lambda n: (0, 0)),
                   pl.BlockSpec((1, N_Q_HEADS), lambda n: (0, 0))],
        out_shape=[jax.ShapeDtypeStruct((t, ATT_Q), BF16), jax.ShapeDtypeStruct((t, 2 * ATT_KV), BF16),
                   jax.ShapeDtypeStruct((HEAD_DIM, 1), F32), jax.ShapeDtypeStruct((1, HEAD_DIM), F32),
                   jax.ShapeDtypeStruct((1, N_Q_HEADS), F32)],
        scratch=[pltpu.VMEM((N_KV_HEADS, 2 * BLOCK, HEAD_DIM), F32),
                 pltpu.VMEM((N_KV_HEADS, 2 * BLOCK, HEAD_DIM), F32),
                 pltpu.VMEM((N_KV_HEADS, BLOCK, HEAD_DIM), F32),
                 pltpu.VMEM((N_KV_HEADS, BLOCK, HEAD_DIM), F32)],
        args=[q_a, kv_a, kv_a, d_attn, probs, sink_probs, gq_col, gk, gk_col])


def _ret_tables(t, exchange):
    theta = 1.0 / (RET_ROT_BASE ** jnp.linspace(0.0, 1.0, RET_QK_DIM // 2, dtype=F32))
    theta2 = jnp.repeat(theta, 2)[None, :]
    sign = jnp.tile(jnp.array([-1.0, 1.0], F32), RET_QK_DIM // 2)[None, :]

    def kern(theta_ref, sign_ref, cos_ref, sin_ref):
        first = pl.program_id(0) * RET_CHUNK
        pos = (first + lax.broadcasted_iota(jnp.int32, (RET_CHUNK, RET_QK_DIM), 0)).astype(F32)
        ang = pos * theta_ref[...]
        cos_ref[...] = jnp.cos(ang)
        sin_ref[...] = jnp.sin(ang) * sign_ref[...]

    row = pl.BlockSpec((1, RET_QK_DIM), lambda n: (0, 0))
    blk = pl.BlockSpec((RET_CHUNK, RET_QK_DIM), lambda n: (n, 0))
    cos, sin_s, *got = _pallas(kern, grid=(t // RET_CHUNK,), in_specs=[row, row], out_specs=[blk, blk],
                               out_shape=[jax.ShapeDtypeStruct((t, RET_QK_DIM), F32)] * 2, args=[theta2, sign],
                               name="position_tables", exchange=exchange)
    log_gamma = jnp.log(1.0 - 2.0 ** (-5.0 - jnp.arange(RET_HEADS, dtype=F32)))
    i = jnp.arange(RET_CHUNK, dtype=F32)
    diff = i[:, None] - i[None, :]
    causal = diff >= 0
    decay = jnp.where(causal[None], jnp.exp(jnp.where(causal, diff, 0.0)[None] * log_gamma[:, None, None]), 0.0)
    xi = jnp.exp((i + 1.0)[None, :] * log_gamma[:, None])[:, :, None]
    zeta = jnp.exp((RET_CHUNK - 1.0 - i)[None, :] * log_gamma[:, None])[:, :, None]
    gch = jnp.broadcast_to(jnp.exp(RET_CHUNK * log_gamma)[:, None, None], (RET_HEADS, 1, 128))
    return (cos, sin_s, decay, xi, zeta, gch), got


def _swap_pairs(x):
    lane = lax.broadcasted_iota(jnp.int32, x.shape, 1)
    return jnp.where((lane & 1) == 0, pltpu.roll(x, RET_QK_DIM - 1, 1), pltpu.roll(x, 1, 1))


def _rotate(x, cos, sin_s):
    return x * cos + _swap_pairs(x) * sin_s


def _rotate_bwd(dy, cos, sin_s):
    return dy * cos + _swap_pairs(dy * sin_s)


def _ret_specs(order):
    qk = pl.BlockSpec((RET_CHUNK, RET_QK), lambda j: (order(j), 0))
    v = pl.BlockSpec((RET_CHUNK, RET_V), lambda j: (order(j), 0))
    dec = pl.BlockSpec((RET_HEADS, RET_CHUNK, RET_CHUNK), lambda j: (0, 0, 0))
    col = pl.BlockSpec((RET_HEADS, RET_CHUNK, 1), lambda j: (0, 0, 0))
    gch = pl.BlockSpec((RET_HEADS, 1, 128), lambda j: (0, 0, 0))
    st = pl.BlockSpec((RET_HEADS, None, RET_QK_DIM, RET_V_DIM), lambda j: (0, order(j), 0, 0))
    pos = pl.BlockSpec((RET_CHUNK, RET_QK_DIM), lambda j: (order(j), 0))
    return qk, v, dec, col, gch, st, pos


def _ret_fwd(q_r, k_r, v_r, g_r, tables):
    t = q_r.shape[0]
    nc = t // RET_CHUNK
    cos, sin_s, decay, xi, zeta, gch = tables

    def kern(q_ref, k_ref, v_ref, g_ref, cos_ref, sin_ref, dec_ref, xi_ref, zeta_ref, gch_ref,
             o_ref, ret_ref, st_ref, state):
        @pl.when(pl.program_id(0) == 0)
        def _():
            state[...] = jnp.zeros_like(state)

        cos_t = cos_ref[...]
        sin_t = sin_ref[...]
        for h in range(RET_HEADS):
            qc = slice(h * RET_QK_DIM, (h + 1) * RET_QK_DIM)
            vc = slice(h * RET_V_DIM, (h + 1) * RET_V_DIM)
            qs = _bf(_rotate(q_ref[:, qc], cos_t, sin_t))
            ks = _rotate(k_ref[:, qc] * (RET_QK_DIM ** -0.5), cos_t, sin_t)
            vb = v_ref[:, vc]
            s_old = state[h]
            sb = _bf(s_old)
            st_ref[h] = sb
            inner = _dot_nt(qs, _bf(ks)) * dec_ref[h]
            out = _dot(_bf(inner), vb) + _dot(qs, sb) * xi_ref[h]
            state[h] = gch_ref[h, :, 0:1] * s_old + _dot_tn(_bf(ks * zeta_ref[h]), vb)
            o_ref[:, vc] = out
            r, rn = _rms_stats(out)
            g = g_ref[:, vc]
            ret_ref[:, vc] = _bf(g * jax.nn.sigmoid(g) * rn)

    qk, v, dec, col, gsp, st, pos = _ret_specs(lambda j: j)
    return dict(
        kern=kern,
        in_specs=[qk, qk, v, v, pos, pos, dec, col, col, gsp],
        out_specs=[v, v, st],
        out_shape=[jax.ShapeDtypeStruct((t, RET_V), F32), jax.ShapeDtypeStruct((t, RET_V), BF16),
                   jax.ShapeDtypeStruct((RET_HEADS, nc, RET_QK_DIM, RET_V_DIM), BF16)],
        scratch=[pltpu.VMEM((RET_HEADS, RET_QK_DIM, RET_V_DIM), F32)],
        args=[q_r, k_r, v_r, g_r, cos, sin_s, decay, xi, zeta, gch])


def _ret_bwd(q_r, k_r, v_r, d_o, states, tables):
    t = q_r.shape[0]
    nc = t // RET_CHUNK
    cos, sin_s, decay, xi, zeta, gch = tables

    def kern(q_ref, k_ref, v_ref, do_ref, st_ref, cos_ref, sin_ref, dec_ref, xi_ref, zeta_ref, gch_ref,
             d_ref, dstate):
        dq_ref, dk_ref = d_ref.at[:, 0:RET_QK], d_ref.at[:, RET_QK:2 * RET_QK]
        dv_ref = d_ref.at[:, 2 * RET_QK:2 * RET_QK + RET_V]

        @pl.when(pl.program_id(0) == 0)
        def _():
            dstate[...] = jnp.zeros_like(dstate)

        @pl.when(pl.program_id(0) < nc)
        def _():
            cos_t = cos_ref[...]
            sin_t = sin_ref[...]
            scale = RET_QK_DIM ** -0.5
            for h in range(RET_HEADS):
                qc = slice(h * RET_QK_DIM, (h + 1) * RET_QK_DIM)
                vc = slice(h * RET_V_DIM, (h + 1) * RET_V_DIM)
                qs = _bf(_rotate(q_ref[:, qc], cos_t, sin_t))
                ks = _rotate(k_ref[:, qc] * scale, cos_t, sin_t)
                ksb = _bf(ks)
                vb = v_ref[:, vc]
                d_o_t = do_ref[:, vc]
                dob = _bf(d_o_t)
                doxb = _bf(d_o_t * xi_ref[h])
                dec = dec_ref[h]
                ds_old = dstate[h]
                dsb = _bf(ds_old)
                pb = _bf(_dot_nt(qs, ksb) * dec)
                dpb = _bf(_dot_nt(dob, vb) * dec)
                dqs = _dot(dpb, ksb) + _dot_nt(doxb, st_ref[h])
                dks = _dot_tn(dpb, qs) + _dot_nt(vb, dsb) * zeta_ref[h]
                dv_ref[:, vc] = _bf(_dot_tn(pb, dob) + _dot(_bf(ks * zeta_ref[h]), dsb))
                dstate[h] = gch_ref[h, :, 0:1] * ds_old + _dot_tn(qs, doxb)
                dq_ref[:, qc] = _bf(_rotate_bwd(dqs, cos_t, sin_t))
                dk_ref[:, qc] = _bf(_rotate_bwd(dks, cos_t, sin_t) * scale)

    backwards = lambda j: jnp.maximum(nc - 1 - j, 0)
    qk, v, dec, col, gsp, st, pos = _ret_specs(backwards)
    return dict(
        kern=kern,
        in_specs=[qk, qk, v, v, st, pos, pos, dec, col, col, gsp],
        out_specs=[pl.BlockSpec((RET_CHUNK, 2 * RET_QK + RET_V), lambda j: (backwards(j), 0))],
        out_shape=[jax.ShapeDtypeStruct((t, 2 * RET_QK + RET_V), BF16)],
        scratch=[pltpu.VMEM((RET_HEADS, RET_QK_DIM, RET_V_DIM), F32)],
        args=[q_r, k_r, v_r, d_o, states, cos, sin_s, decay, xi, zeta, gch])


def _position():
    return lax.axis_index("x"), lax.axis_index("y"), lax.axis_index("c")


def _gather_exchange(owns, forward_at):
    n = len(owns)

    def copies(ins, outs, send_sems, recv_sems, base):
        x, y, c = _position()
        sibling = (x, y, 1 - c)
        chips = [(1 - x, y), (x, 1 - y), (1 - x, 1 - y)]
        my_chip = 2 * x + y

        def slab(a, chip, hf):
            half = owns[a].shape[0] // 2
            return outs[a].at[chip, pl.ds(hf * half, half), :]

        def copy(k, src, dst, to):
            return pltpu.make_async_remote_copy(src_ref=src, dst_ref=dst, send_sem=send_sems.at[base + k],
                                                recv_sem=recv_sems.at[base + k], device_id=to, device_id_type=MESH)

        first, passed, from_sibling = [], [], []
        for a in range(n):
            half = owns[a].shape[0] // 2
            for k, (cx, cy) in enumerate(chips):
                first.append(copy(6 * a + k, ins[a].at[pl.ds(c * half, half), :], slab(a, my_chip, c), (cx, cy, c)))
                landed = slab(a, 2 * cx + cy, c)
                passed.append(copy(6 * a + 3 + k, landed, landed, sibling))
                theirs = slab(a, 2 * cx + cy, 1 - c)
                from_sibling.append(copy(6 * a + 3 + k, theirs, theirs, sibling))
        return first, passed, from_sibling

    def start(*args):
        first, _, _ = copies(*args)
        for cp in first:
            cp.start()

    def forward(*args):
        first, passed, _ = copies(*args)
        for arrived, cp in zip(first, passed):
            arrived.wait_recv()
            cp.start()

    def finish(*args):
        first, passed, from_sibling = copies(*args)
        for cp in from_sibling:
            cp.wait_recv()
        for cp in first + passed:
            cp.wait_send()

    outs = [jax.ShapeDtypeStruct((N_CHIPS, *a.shape), a.dtype) for a in owns]
    return _Exchange(owns, outs, 6 * n, [(0.0, start), (forward_at, forward), (1.0, finish)])


def _symmetric_exchange(ins, outs, plan):
    n_sems = len(plan([None] * len(ins), [None] * len(outs), 0, 0, 0, dry=True))

    def copies(in_refs, out_refs, send_sems, recv_sems, base):
        x, y, c = _position()
        return [pltpu.make_async_remote_copy(src_ref=src, dst_ref=dst, send_sem=send_sems.at[base + k],
                                             recv_sem=recv_sems.at[base + k], device_id=dev, device_id_type=MESH)
                for k, (src, dst, dev) in enumerate(plan(in_refs, out_refs, x, y, c, dry=False))]

    def start(*args):
        for cp in copies(*args):
            cp.start()

    def finish(*args):
        for cp in copies(*args):
            cp.wait()

    return _Exchange(ins, outs, n_sems, [(0.0, start), (1.0, finish)])


def _pair_exchange(gs):
    def plan(in_refs, out_refs, x, y, c, dry):
        out = []
        for a, g in enumerate(gs):
            half = g.shape[1] // 2
            for k in range(N_CHIPS):
                out.append(None if dry else (in_refs[a].at[k, pl.ds((1 - c) * half, half), :], out_refs[a].at[k],
                                             (x, y, 1 - c)))
        return out

    outs = [jax.ShapeDtypeStruct((g.shape[0], g.shape[1] // 2, g.shape[2]), g.dtype) for g in gs]
    return _symmetric_exchange(gs, outs, plan)


def _pair_sum(g, from_sibling, c_arr, *, tile, name):
    n, rows, width = g.shape
    tiles = (rows // 2) // tile

    def kern(c_ref, g_ref, s_ref, o_ref):
        o_ref[...] = _bf(g_ref[...] + s_ref[...])

    return pl.pallas_call(
        kern,
        grid_spec=pltpu.PrefetchScalarGridSpec(
            num_scalar_prefetch=1, grid=(n, tiles),
            in_specs=[pl.BlockSpec((None, tile, width), lambda k, i, c: (k, c[0] * tiles + i, 0)),
                      pl.BlockSpec((None, tile, width), lambda k, i, c: (k, i, 0))],
            out_specs=pl.BlockSpec((None, tile, width), lambda k, i, c: (k, i, 0))),
        out_shape=jax.ShapeDtypeStruct((n, rows // 2, width), BF16), name=name,
        compiler_params=_params(("parallel", "parallel")),
    )(c_arr, g, from_sibling)


def _scatter_to_owners(hsums):
    def plan(in_refs, out_refs, x, y, c, dry):
        out = []
        for a in range(len(hsums)):
            for k, (cx, cy) in enumerate([(1 - x, y), (x, 1 - y), (1 - x, 1 - y)]):
                out.append(None if dry else (in_refs[a].at[2 * cx + cy], out_refs[a].at[k], (cx, cy, c)))
        return out

    outs = [jax.ShapeDtypeStruct((3, *h.shape[1:]), h.dtype) for h in hsums]
    return _symmetric_exchange(hsums, outs, plan)


def _sum_chips(hsum, parts, chip_arr, *, tile, name):
    n, half, width = parts.shape

    def kern(chip_ref, h_ref, p_ref, o_ref):
        acc = h_ref[...].astype(F32)
        for k in range(n):
            acc = acc + p_ref[k].astype(F32)
        o_ref[...] = acc

    return pl.pallas_call(
        kern,
        grid_spec=pltpu.PrefetchScalarGridSpec(
            num_scalar_prefetch=1, grid=(half // tile,),
            in_specs=[pl.BlockSpec((None, tile, width), lambda i, chip: (chip[0], i, 0)),
                      pl.BlockSpec((n, tile, width), lambda i, chip: (0, i, 0))],
            out_specs=pl.BlockSpec((tile, width), lambda i, chip: (i, 0))),
        out_shape=jax.ShapeDtypeStruct((half, width), F32), name=name,
        compiler_params=_params(("parallel",)),
    )(chip_arr, hsum, parts)


def _share_halves(fhalves):
    def plan(in_refs, out_refs, x, y, c, dry):
        return [None if dry else (in_refs[a], out_refs[a], (x, y, 1 - c)) for a in range(len(fhalves))]

    return _symmetric_exchange(fhalves, [jax.ShapeDtypeStruct(f.shape, f.dtype) for f in fhalves], plan)


def _adamw_math(w, g, m, v):
    m = ADAM_B1 * m + (1.0 - ADAM_B1) * g
    v = ADAM_B2 * v + (1.0 - ADAM_B2) * (g * g)
    m_hat = m / (1.0 - ADAM_B1 ** ADAM_STEP)
    v_hat = v / (1.0 - ADAM_B2 ** ADAM_STEP)
    delta = -ADAM_LR * (m_hat / (jnp.sqrt(v_hat) + ADAM_EPS) + ADAM_WD * w)
    return delta, m, v


def _adamw(mats, g_mine, g_other, c_arr, *, tile, name):
    width = g_mine.shape[1]
    tiles_per_half = g_mine.shape[0] // tile
    n_tiles = [w.shape[0] // tile for w, _, _, _ in mats]
    n_mats = len(mats)

    def kern(c_ref, *refs):
        ins, outs = refs[:5 * n_mats], refs[5 * n_mats:]
        for j, (_, _, _, row_off) in enumerate(mats):
            w_ref, gm_ref, go_ref, m_ref, v_ref = ins[5 * j:5 * j + 5]
            i = jnp.minimum(pl.program_id(0), n_tiles[j] - 1)
            in_my_half = ((row_off // tile + i) // tiles_per_half) == c_ref[0]
            g = jnp.where(in_my_half, gm_ref[...], go_ref[...])
            d, nm, nv = _adamw_math(w_ref[...], g, m_ref[...], v_ref[...])
            for out_ref, val in zip(outs[4 * j:4 * j + 4], (g, d, nm, nv)):
                out_ref[...] = val

    in_specs, out_specs, out_shape, args = [], [], [], []
    for (w, m, v, row_off), nt in zip(mats, n_tiles):
        full = pl.BlockSpec((tile, width), lambda i, c, nt=nt: (jnp.minimum(i, nt - 1), 0))
        half = pl.BlockSpec((tile, width), lambda i, c, nt=nt, first=row_off // tile:
                            ((first + jnp.minimum(i, nt - 1)) % tiles_per_half, 0))
        in_specs += [full, half, half, full, full]
        out_specs += [full] * 4
        out_shape += [jax.ShapeDtypeStruct(w.shape, F32)] * 4
        args += [w, g_mine, g_other, m, v]
    outs = pl.pallas_call(
        kern,
        grid_spec=pltpu.PrefetchScalarGridSpec(num_scalar_prefetch=1, grid=(max(n_tiles),), in_specs=in_specs,
                                               out_specs=out_specs),
        out_shape=out_shape, name=name, compiler_params=_params(("arbitrary",)),
    )(c_arr, *args)
    return [outs[4 * j:4 * j + 4] for j in range(n_mats)]


def _small_step(partials, params):
    slots = ((0, 0, D_MODEL), (1, 0, D_MODEL), (2, 0, HEAD_DIM), (2, 128, HEAD_DIM), (2, 256, N_Q_HEADS))
    loss_slot = (2, 384, 128)

    def body(*refs):
        loss_ref, dg1_ref, dg2_ref, dgq_ref, dgk_ref, dsink_ref = refs[:6]
        p_refs, out_refs = refs[6:21], refs[21:42]
        mine, gathered, send_sems, recv_sems = refs[42:]
        x, y, c = _position()
        me = 4 * x + 2 * y + c
        mine[...] = jnp.zeros_like(mine)
        for (row, lane, n), val in zip(slots + (loss_slot,), (
                jnp.sum(dg1_ref[...], axis=0, keepdims=True), jnp.sum(dg2_ref[...], axis=0, keepdims=True),
                dgq_ref[...], dgk_ref[...], dsink_ref[...], jnp.sum(loss_ref[...], axis=0, keepdims=True))):
            mine[row:row + 1, lane:lane + n] = val
        copies = []
        for k in range(1, N_DEV):
            flip = (k >> 2) & 1, (k >> 1) & 1, k & 1
            to = (x ^ flip[0], y ^ flip[1], c ^ flip[2])
            cp = pltpu.make_async_remote_copy(
                src_ref=mine, dst_ref=gathered.at[me], send_sem=send_sems.at[k - 1], recv_sem=recv_sems.at[k - 1],
                device_id=to, device_id_type=MESH)
            cp.start()
            copies.append(cp)
        gathered[me] = mine[...]
        for k in range(1, N_DEV):
            flip = (k >> 2) & 1, (k >> 1) & 1, k & 1
            src = 4 * (x ^ flip[0]) + 2 * (y ^ flip[1]) + (c ^ flip[2])
            pltpu.make_async_remote_copy(
                src_ref=mine, dst_ref=gathered.at[src], send_sem=send_sems.at[k - 1], recv_sem=recv_sems.at[k - 1],
                device_id=(x, y, c), device_id_type=MESH).wait_recv()
        for cp in copies:
            cp.wait_send()
        total = gathered[0]
        for k in range(1, N_DEV):
            total = total + gathered[k]
        row, lane, n = loss_slot
        out_refs[0][...] = total[row:row + 1, lane:lane + n]
        for i, (row, lane, n) in enumerate(slots):
            g = total[row:row + 1, lane:lane + n]
            d, nm, nv = _adamw_math(p_refs[i][...], g, p_refs[5 + i][...], p_refs[10 + i][...])
            for kind, val in enumerate((g, d, nm, nv)):
                out_refs[1 + 5 * kind + i][...] = val

    vm = pl.BlockSpec(memory_space=pltpu.VMEM)
    shapes = [jax.ShapeDtypeStruct((1, 128), F32)] + [jax.ShapeDtypeStruct((1, n), F32) for _, _, n in slots] * 4
    return pl.pallas_call(
        body, in_specs=[vm] * 21, out_specs=[vm] * 21, out_shape=shapes,
        scratch_shapes=[pltpu.VMEM((SMALL_ROWS, D_MODEL), F32), pltpu.VMEM((N_DEV, SMALL_ROWS, D_MODEL), F32),
                        pltpu.SemaphoreType.DMA((N_DEV - 1,)), pltpu.SemaphoreType.DMA((N_DEV - 1,))],
        name="small_step",
    )(*partials, *params)


def _with_own(gathered, own, my_chip):
    return lax.dynamic_update_slice(gathered, own[None], (my_chip, 0, 0))


def kernel(x, norm_mix_gain, w_in, q_norm_gain, k_norm_gain, attn_sinks, w_branch_attn, w_branch_ret, w_out, norm_ffn_gain, w_ffn_gate, w_ffn_up, w_ffn_down, loss_target, m_norm_mix_gain, m_w_in, m_q_norm_gain, m_k_norm_gain, m_attn_sinks, m_w_branch_attn, m_w_branch_ret, m_w_out, m_norm_ffn_gain, m_w_ffn_gate, m_w_ffn_up, m_w_ffn_down, v_norm_mix_gain, v_w_in, v_q_norm_gain, v_k_norm_gain, v_attn_sinks, v_w_branch_attn, v_w_branch_ret, v_w_out, v_norm_ffn_gain, v_w_ffn_gate, v_w_ffn_up, v_w_ffn_down):
    my_chip = 2 * lax.axis_index("x") + lax.axis_index("y")
    c_arr = lax.axis_index("c").astype(jnp.int32).reshape(1)
    chip_arr = my_chip.astype(jnp.int32).reshape(1)
    x_t, target = x[0], loss_target[0]
    g1, g2, gq, gk, sinks = norm_mix_gain, norm_ffn_gain, q_norm_gain, k_norm_gain, attn_sinks

    tr = lambda a: jnp.transpose(a[0])
    own_w_in = _bf(tr(w_in))
    own_rest = [_bf(a) for a in (tr(w_ffn_gate), tr(w_ffn_up), w_ffn_down[0], w_branch_attn[0], w_branch_ret[0],
                                 w_out[0])]
    tables, (got_w_in,) = _ret_tables(x_t.shape[0], _gather_exchange([own_w_in], 0.9))
    w_in_t = _with_own(got_w_in, own_w_in, my_chip).reshape(D_IN, D_MODEL)
    h1, q_a, kv_a, q_r, k_r, v_r, g_r, z_a, z_r, *got_rest = _proj_fwd(x_t, g1, w_in_t, _gather_exchange(own_rest, 0.8))
    wg_t, wu_t, wd, wba, wbr, wout = [_with_own(got, own, my_chip).reshape(N_CHIPS * own.shape[0], D_MODEL)
                                      for got, own in zip(got_rest, own_rest)]

    gq_col, gk_col = gq.reshape(HEAD_DIM, 1), gk.reshape(HEAD_DIM, 1)
    attn, probs, sink_probs, o_ret, ret, states = _fused(
        [_attn_fwd(q_a, kv_a, gq_col, gk, sinks), _ret_fwd(q_r, k_r, v_r, g_r, tables)],
        grid=(x_t.shape[0] // BLOCK,), name="mixers_fwd")
    ba, br, merged, x1, h2 = _mix_fwd(attn, ret, z_a, z_r, x_t, wba, wbr, wout, g2)
    act, dgate, dup, dyb, dx1, dx1b, loss_p, dg2_p = _ffn_fwd_bwd(h2, x1, target, wg_t, wu_t, wd, g2)

    def pairs(row0, rows):
        return lambda i: [(h * rows, rows, (2 * i + h, pl.ds(row0, rows), slice(None))) for h in range(2)]

    f_block = jax.ShapeDtypeStruct((N_CHIPS, 3 * FF_SH, D_MODEL), F32)
    f_block, = _dw(dgate, h2, tm=2 * FF_SH, place=pairs(0, FF_SH), buf=f_block, name="dw_gate")
    f_block, = _dw(dup, h2, tm=2 * FF_SH, place=pairs(FF_SH, FF_SH), buf=f_block, name="dw_up")
    f_block, = _dw(act, dyb, tm=2 * FF_SH, place=pairs(2 * FF_SH, FF_SH), buf=f_block, name="dw_down")
    (dba, dbr, d_attn, d_o, d_gz, sib_ffn) = _mix_bwd(
        dx1b, z_a, z_r, ba, br, g_r, o_ret, wout, wba, wbr, _pair_exchange([f_block]))
    f_sum = _pair_sum(f_block, sib_ffn, c_arr, tile=528, name="pair_sum_ffn")

    def quarters(row0, rows):
        return lambda i: [(k * rows, rows, (k, pl.ds(row0, rows), slice(None))) for k in range(N_CHIPS)]

    m_block = jax.ShapeDtypeStruct((N_CHIPS, D_MODEL, D_MODEL), F32)
    m_block, = _dw(attn, dba, tm=ATT_Q, place=quarters(0, 256), buf=m_block, name="dw_ba")
    m_block, = _dw(ret, dbr, tm=D_MODEL, place=pairs(256, 512), buf=m_block, name="dw_br")
    m_block, = _dw(merged, dx1b, tm=D_MODEL, place=quarters(768, 256), buf=m_block, name="dw_out")

    def w_in_rows(off, w):
        tm = min(w, D_MODEL)
        return dict(tm=tm, place=lambda i: [(0, tm, (pl.ds(off + i * tm, tm), slice(None)))])

    w_block = jax.ShapeDtypeStruct((D_IN, D_MODEL), F32)
    w_block, sib_mix = _dw(d_gz, h1, buf=w_block, name="dw_in_gz", exchange=_pair_exchange([m_block]),
                           **w_in_rows(P_GR[0], d_gz.shape[1]))
    m_sum = _pair_sum(m_block, sib_mix, c_arr, tile=256, name="pair_sum_mix")

    (dq_a, dkv_a, dgq, dgk, dsinks, d_ret, got_ffn_sums, got_mix_sums) = _fused(
        [_attn_bwd(q_a, kv_a, d_attn, probs, sink_probs, gq_col, gk, gk_col),
         _ret_bwd(q_r, k_r, v_r, d_o, states, tables)],
        grid=(x_t.shape[0] // BLOCK + 1,), name="mixers_bwd", exchange=_scatter_to_owners([f_sum, m_sum]))
    dgq = dgq.reshape(1, HEAD_DIM)
    ffn_half = _sum_chips(f_sum, got_ffn_sums, chip_arr, tile=528, name="sum_chips_ffn")
    mix_half = _sum_chips(m_sum, got_mix_sums, chip_arr, tile=256, name="sum_chips_mix")
    w_block, = _dw(d_ret, h1, buf=w_block, name="dw_in_ret", **w_in_rows(P_QR[0], d_ret.shape[1]))
    w_block, ffn_other, mix_other = _dw(dq_a, h1, buf=w_block, name="dw_in_q",
                                        exchange=_share_halves([ffn_half, mix_half]), **w_in_rows(*P_QA))
    w_block, = _dw(dkv_a, h1, buf=w_block, name="dw_in_kv", **w_in_rows(*P_KVA))

    w_block = w_block.reshape(N_CHIPS, W_IN_SH, D_MODEL)
    sib_w_in, = _run_exchange(_pair_exchange([w_block]), "pair_exchange_w_in")
    w_sum = _pair_sum(w_block, sib_w_in, c_arr, tile=592, name="pair_sum_w_in")
    d_pieces = [dq_a, dkv_a, d_ret, d_gz]
    grad_x, dg1_p, got_w_in_sums = _proj_bwd(d_pieces, x_t, dx1, w_in_t, g1, _scatter_to_owners([w_sum]))
    w_in_half = _sum_chips(w_sum, got_w_in_sums, chip_arr, tile=592, name="sum_chips_w_in")
    w_in_other, = _run_exchange(_share_halves([w_in_half]), "share_halves_w_in")

    def update(name, g_half, g_other, tile, mats):
        outs = _adamw([tuple(tr(a) if t else a[0] for a in wmv) + (off,) for _, *wmv, off, t in mats],
                      g_half, g_other, c_arr, tile=tile, name=f"adamw_{name}")
        return {key: [jnp.transpose(o) if t else o for o in res] for (key, _, _, _, _, t), res in zip(mats, outs)}

    big = {
        **update("w_in", w_in_half, w_in_other, 592, [("w_in", w_in, m_w_in, v_w_in, 0, True)]),
        **update("ffn", ffn_half, ffn_other, 176, [
            ("wg", w_ffn_gate, m_w_ffn_gate, v_w_ffn_gate, 0, True),
            ("wu", w_ffn_up, m_w_ffn_up, v_w_ffn_up, FF_SH, True),
            ("wd", w_ffn_down, m_w_ffn_down, v_w_ffn_down, 2 * FF_SH, False)]),
        **update("mix", mix_half, mix_other, 128, [
            ("wba", w_branch_attn, m_w_branch_attn, v_w_branch_attn, 0, False),
            ("wbr", w_branch_ret, m_w_branch_ret, v_w_branch_ret, 256, False),
            ("wout", w_out, m_w_out, v_w_out, 768, False)])}

    loss_row, *small = _small_step(
        [loss_p.reshape(-1, 128), dg1_p.reshape(-1, D_MODEL), dg2_p.reshape(-1, D_MODEL), dgq, dgk, dsinks],
        [norm_mix_gain, norm_ffn_gain, q_norm_gain, k_norm_gain, attn_sinks,
         m_norm_mix_gain, m_norm_ffn_gain, m_q_norm_gain, m_k_norm_gain, m_attn_sinks,
         v_norm_mix_gain, v_norm_ffn_gain, v_q_norm_gain, v_k_norm_gain, v_attn_sinks])
    loss = loss_row[0, 0]

    def leaves(i):
        b = [big[n][i][None] for n in ("w_in", "wba", "wbr", "wout", "wg", "wu", "wd")]
        s1, s2, sq, sk, ss = small[5 * i:5 * i + 5]
        return [s1, b[0], sq, sk, ss, b[1], b[2], b[3], s2, b[4], b[5], b[6]]

    return (loss, grad_x[None], *leaves(0), *leaves(1), *leaves(2), *leaves(3))
```

```python
import jax
import jax.numpy as jnp
from jax import lax
from jax.experimental import pallas as pl
from jax.experimental.pallas import tpu as pltpu

F32 = jnp.float32
BF16 = jnp.bfloat16
MESH = pl.DeviceIdType.MESH

D_MODEL = 1024
EPS = 1e-6
HEAD_DIM = 64
N_Q_HEADS = 16
N_KV_HEADS = 2
GROUP = 8
BLOCK = 128
RET_HEADS = 4
RET_QK_DIM = 256
RET_V_DIM = 512
RET_CHUNK = 128
RET_ROT_BASE = 10000.0
D_FF = 2816
ATT_Q = N_Q_HEADS * HEAD_DIM
ATT_KV = N_KV_HEADS * HEAD_DIM
RET_QK = RET_HEADS * RET_QK_DIM
RET_V = RET_HEADS * RET_V_DIM
D_IN = 9472
ADAM_LR = 0.001
ADAM_B1 = 0.9
ADAM_B2 = 0.999
ADAM_EPS = 1e-08
ADAM_WD = 0.01
ADAM_STEP = 10

N_CHIPS = 4
N_DEV = 8
VMEM_LIMIT_BYTES = 60 * 1024 * 1024

P_QA = (0, 1024)
P_KVA = (1024, 256)
P_QR = (1280, 1024)
P_KR = (2304, 1024)
P_VR = (3328, 2048)
P_GR = (5376, 2048)
P_ZA = (7424, 1024)
P_ZR = (8448, 1024)

W_IN_SH = D_IN // N_CHIPS
FF_SH = D_FF // N_CHIPS

SMALL_ROWS = 8


def _dot(a, b):
    return jnp.dot(a, b, preferred_element_type=F32)


def _dot_nt(a, b):
    return lax.dot_general(a, b, (((1,), (1,)), ((), ())), preferred_element_type=F32)


def _dot_tn(a, b):
    return lax.dot_general(a, b, (((0,), (0,)), ((), ())), preferred_element_type=F32)


def _bf(x):
    return x.astype(BF16)


def _rms_stats(x):
    r = lax.rsqrt(jnp.mean(x * x, axis=-1, keepdims=True) + EPS)
    return r, x * r


def _rms_bwd(dy, xhat, r, gain):
    u = dy * gain
    dx = r * (u - xhat * jnp.mean(u * xhat, axis=-1, keepdims=True))
    return dx, dy * xhat


def _params(sem):
    return pltpu.CompilerParams(dimension_semantics=sem, vmem_limit_bytes=VMEM_LIMIT_BYTES)


_ANY = pl.BlockSpec(memory_space=pl.ANY)


class _Exchange:
    def __init__(self, ins, outs, n_sems, phases):
        self.ins, self.outs, self.n_sems, self.phases = list(ins), list(outs), n_sems, list(phases)


def _pallas(kern, *, grid, in_specs, out_specs, out_shape, args, name, scratch=(), exchange=None, aliases=None):
    aliases = aliases or {}
    if exchange is None:
        return pl.pallas_call(
            kern, grid=grid, in_specs=in_specs, out_specs=out_specs, out_shape=out_shape, name=name,
            scratch_shapes=list(scratch), input_output_aliases=aliases,
            compiler_params=_params(("arbitrary",) * len(grid)))(*args)
    n_in, n_out, n_sc = len(in_specs), len(out_specs), len(scratch)
    n_xi, n_xo = len(exchange.ins), len(exchange.outs)
    n_steps = 1
    for g in grid:
        n_steps *= g

    def wrapped(*refs):
        ins, refs = refs[:n_in], refs[n_in:]
        x_ins, refs = refs[:n_xi], refs[n_xi:]
        outs, refs = refs[:n_out], refs[n_out:]
        x_outs, refs = refs[:n_xo], refs[n_xo:]
        scr, (send_sems, recv_sems) = refs[:n_sc], refs[n_sc:]
        step = pl.program_id(0)
        for d in range(1, len(grid)):
            step = step * grid[d] + pl.program_id(d)
        for frac, fn in exchange.phases:
            at = min(int(frac * n_steps), n_steps - 1)

            @pl.when(step == at)
            def _(fn=fn):
                fn(x_ins, x_outs, send_sems, recv_sems, 0)

        kern(*ins, *outs, *scr)

    sems = [pltpu.SemaphoreType.DMA((exchange.n_sems,)), pltpu.SemaphoreType.DMA((exchange.n_sems,))]
    return pl.pallas_call(
        wrapped, grid=grid, in_specs=list(in_specs) + [_ANY] * n_xi, out_specs=list(out_specs) + [_ANY] * n_xo,
        out_shape=list(out_shape) + exchange.outs, name=name, scratch_shapes=list(scratch) + sems,
        input_output_aliases=aliases, compiler_params=_params(("arbitrary",) * len(grid)))(*args, *exchange.ins)


def _run_exchange(exchange, name):
    def body(*refs):
        n_i, n_o = len(exchange.ins), len(exchange.outs)
        for _, fn in exchange.phases:
            fn(refs[:n_i], refs[n_i:n_i + n_o], refs[n_i + n_o], refs[n_i + n_o + 1], 0)

    sems = [pltpu.SemaphoreType.DMA((exchange.n_sems,)), pltpu.SemaphoreType.DMA((exchange.n_sems,))]
    return pl.pallas_call(body, in_specs=[_ANY] * len(exchange.ins), out_specs=[_ANY] * len(exchange.outs),
                          out_shape=exchange.outs, scratch_shapes=sems, name=name)(*exchange.ins)


def _fused(parts, *, grid, name, exchange=None):
    counts = [(len(p["in_specs"]), len(p["out_specs"]), len(p["scratch"])) for p in parts]
    n_in, n_out = sum(c[0] for c in counts), sum(c[1] for c in counts)

    def kern(*refs):
        ins, outs, scr = refs[:n_in], refs[n_in:n_in + n_out], refs[n_in + n_out:]
        i0 = o0 = s0 = 0
        for p, (ni, no, ns) in zip(parts, counts):
            p["kern"](*ins[i0:i0 + ni], *outs[o0:o0 + no], *scr[s0:s0 + ns])
            i0, o0, s0 = i0 + ni, o0 + no, s0 + ns

    cat = lambda key: [a for p in parts for a in p[key]]
    return _pallas(kern, grid=grid, in_specs=cat("in_specs"), out_specs=cat("out_specs"), out_shape=cat("out_shape"),
                   scratch=cat("scratch"), args=cat("args"), name=name, exchange=exchange)


def _row_call(body, *, tm, row_ins, res_ins, row_outs, part_outs=(), name, exchange=None):
    t = row_ins[0].shape[0]
    n_tiles = t // tm
    in_specs = [pl.BlockSpec((tm, a.shape[1]), lambda i: (i, 0)) for a in row_ins]
    in_specs += [pl.BlockSpec(a.shape, lambda i: (0, 0), pipeline_mode=pl.Buffered(1)) for a in res_ins]
    out_shape = [jax.ShapeDtypeStruct((t, w), dt) for (w, dt) in row_outs]
    out_shape += [jax.ShapeDtypeStruct((n_tiles, 1, w), F32) for w in part_outs]
    out_specs = [pl.BlockSpec((tm, w), lambda i: (i, 0)) for (w, _) in row_outs]
    out_specs += [pl.BlockSpec((1, 1, w), lambda i: (i, 0, 0)) for w in part_outs]
    n_ri, n_re, n_ro = len(row_ins), len(res_ins), len(row_outs)

    def kern(*refs):
        body(refs[:n_ri], refs[n_ri:n_ri + n_re], refs[n_ri + n_re:n_ri + n_re + n_ro], refs[n_ri + n_re + n_ro:])

    return _pallas(kern, grid=(n_tiles,), in_specs=in_specs, out_specs=out_specs, out_shape=out_shape,
                   args=[*row_ins, *res_ins], name=name, exchange=exchange)


def _proj_fwd(x, g1, w_in_t, exchange):
    pieces = ((P_QA, F32), (P_KVA, F32), (P_QR, F32), (P_KR, F32), (P_VR, BF16), (P_GR, F32), (P_ZA, F32), (P_ZR, F32))

    def body(ri, re, ro, po):
        x_t = ri[0][...]
        r, xhat = _rms_stats(x_t)
        hb = _bf(xhat * re[0][...])
        ro[0][...] = hb
        for k, ((off, w), dt) in enumerate(pieces):
            ro[1 + k][...] = _dot_nt(hb, re[1][off:off + w, :]).astype(dt)

    outs = [(D_MODEL, BF16)] + [(w, dt) for ((_, w), dt) in pieces]
    return _row_call(body, tm=256, row_ins=[x], res_ins=[g1, w_in_t], row_outs=outs, name="proj_fwd",
                     exchange=exchange)


def _mix_fwd(attn, ret, z_a, z_r, x, wba, wbr, wout, g2):
    def body(ri, re, ro, po):
        ba = _dot(ri[0][...], re[0][...])
        br = _dot(ri[1][...], re[1][...])
        m = jax.nn.sigmoid(ri[2][...]) * ba + jax.nn.sigmoid(ri[3][...]) * br
        mb = _bf(m)
        x1 = ri[4][...] + _dot(mb, re[2][...])
        r, xhat = _rms_stats(x1)
        ro[0][...] = ba
        ro[1][...] = br
        ro[2][...] = mb
        ro[3][...] = x1
        ro[4][...] = _bf(xhat * re[3][...])

    outs = [(D_MODEL, F32), (D_MODEL, F32), (D_MODEL, BF16), (D_MODEL, F32), (D_MODEL, BF16)]
    return _row_call(body, tm=512, row_ins=[attn, ret, z_a, z_r, x], res_ins=[wba, wbr, wout, g2], row_outs=outs,
                     name="mix_fwd")


def _ffn_fwd_bwd(h2, x1, target, wg_t, wu_t, wd, g2):
    def body(ri, re, ro, po):
        h2_t = ri[0][...]
        x1_t = ri[1][...]
        gate = _dot_nt(h2_t, re[0][...])
        up = _dot_nt(h2_t, re[1][...])
        sg = jax.nn.sigmoid(gate)
        sl = gate * sg
        actb = _bf(sl * up)
        ro[0][...] = actb
        y = x1_t + _dot(actb, re[2][...])
        e = y - ri[2][...]
        po[0][0] = jnp.broadcast_to(0.5 * jnp.sum(jnp.sum(e * e, axis=1, keepdims=True), axis=0, keepdims=True)
                                    * (1.0 / D_MODEL), (1, 128))
        dy = e * (1.0 / D_MODEL)
        dyb = _bf(dy)
        ro[3][...] = dyb
        dact = _dot_nt(dyb, re[2][...])
        dupb = _bf(dact * sl)
        dgateb = _bf(dact * up * (sg * (1.0 + gate * (1.0 - sg))))
        ro[1][...] = dgateb
        ro[2][...] = dupb
        dh2 = _dot(dgateb, re[0][...]) + _dot(dupb, re[1][...])
        r, xhat = _rms_stats(x1_t)
        dxn, dgain = _rms_bwd(dh2, xhat, r, re[3][...])
        dx1 = dy + dxn
        ro[4][...] = dx1
        ro[5][...] = _bf(dx1)
        po[1][0] = jnp.sum(dgain, axis=0, keepdims=True)

    outs = [(D_FF, BF16), (D_FF, BF16), (D_FF, BF16), (D_MODEL, BF16), (D_MODEL, F32), (D_MODEL, BF16)]
    return _row_call(body, tm=256, row_ins=[h2, x1, target], res_ins=[wg_t, wu_t, wd, g2], row_outs=outs,
                     part_outs=(128, D_MODEL), name="ffn_fwd_bwd")


def _mix_bwd(dx1b, z_a, z_r, ba, br, g_r, o_ret, wout, wba, wbr, exchange):
    def body(ri, re, ro, po):
        dm = _dot_nt(ri[0][...], re[0][...])
        sa = jax.nn.sigmoid(ri[1][...])
        sr = jax.nn.sigmoid(ri[2][...])
        dbab = _bf(sa * dm)
        dbrb = _bf(sr * dm)
        ro[0][...] = dbab
        ro[1][...] = dbrb
        ro[4][:, RET_V:RET_V + D_MODEL] = _bf(dm * ri[3][...] * (sa * (1.0 - sa)))
        ro[4][:, RET_V + D_MODEL:RET_V + 2 * D_MODEL] = _bf(dm * ri[4][...] * (sr * (1.0 - sr)))
        ro[2][...] = _bf(_dot_nt(dbab, re[1][...]))
        dret = _dot_nt(dbrb, re[2][...])
        for h in range(RET_HEADS):
            cols = slice(h * RET_V_DIM, (h + 1) * RET_V_DIM)
            g = ri[5][:, cols]
            r, rn = _rms_stats(ri[6][:, cols])
            sg = jax.nn.sigmoid(g)
            dret_h = dret[:, cols]
            d_rn = dret_h * (g * sg)
            ro[4][:, cols] = _bf(dret_h * rn * (sg * (1.0 + g * (1.0 - sg))))
            ro[3][:, cols] = r * (d_rn - rn * jnp.mean(d_rn * rn, axis=-1, keepdims=True))

    outs = [(D_MODEL, BF16), (D_MODEL, BF16), (ATT_Q, BF16), (RET_V, F32), (RET_V + 2 * D_MODEL, BF16)]
    return _row_call(body, tm=256, row_ins=[dx1b, z_a, z_r, ba, br, g_r, o_ret], res_ins=[wout, wba, wbr],
                     row_outs=outs, name="mix_bwd", exchange=exchange)


def _proj_bwd(d_pieces, x, dx1, w_in_t, g1, exchange):
    widths = [p.shape[1] for p in d_pieces]
    groups = [(sum(widths[:k]), w) for k, w in enumerate(widths)]
    n_p = len(groups)

    def body(ri, re, ro, po):
        dh = None
        for k, (off, w) in enumerate(groups):
            term = _dot(ri[k][...], re[0][off:off + w, :])
            dh = term if dh is None else dh + term
        r, xhat = _rms_stats(ri[n_p][...])
        dxn, dgain = _rms_bwd(dh, xhat, r, re[1][...])
        ro[0][...] = ri[n_p + 1][...] + dxn
        po[0][0] = jnp.sum(dgain, axis=0, keepdims=True)

    return _row_call(body, tm=512, row_ins=[*d_pieces, x, dx1], res_ins=[w_in_t, g1], row_outs=[(D_MODEL, F32)],
                     part_outs=(D_MODEL,), name="proj_bwd", exchange=exchange)


def _dw(a, b, *, tm, place, buf, name, exchange=None, token_tile=2048):
    t, m = a.shape
    n = b.shape[1]
    tk = min(token_tile, t)
    n_i, n_k = m // tm, t // tk
    fresh = isinstance(buf, jax.ShapeDtypeStruct)
    n_copies = len(place(0))

    def kern(a_ref, b_ref, *rest):
        out_ref, acc, sems = rest[-3:]
        i, k = pl.program_id(0), pl.program_id(1)
        part = _dot_tn(a_ref[...], b_ref[...])

        @pl.when(k == 0)
        def _():
            acc[i] = part

        @pl.when(k > 0)
        def _():
            acc[i] += part

        def copies(tile):
            return [pltpu.make_async_copy(acc.at[tile, pl.ds(r0, rows), :], out_ref.at[idx], sems.at[tile * n_copies + c])
                    for c, (r0, rows, idx) in enumerate(place(tile))]

        for tile in range(n_i):
            @pl.when((i == tile) & (k == n_k - 1))
            def _(tile=tile):
                for cp in copies(tile):
                    cp.start()

        @pl.when((i == n_i - 1) & (k == n_k - 1))
        def _():
            for tile in range(n_i):
                for cp in copies(tile):
                    cp.wait()

    in_specs = [pl.BlockSpec((tk, tm), lambda i, k: (k, i)), pl.BlockSpec((tk, n), lambda i, k: (k, 0))]
    shape = buf if fresh else jax.ShapeDtypeStruct(buf.shape, buf.dtype)
    return _pallas(
        kern, grid=(n_i, n_k), in_specs=in_specs + ([] if fresh else [_ANY]), out_specs=[_ANY], out_shape=[shape],
        scratch=[pltpu.VMEM((n_i, tm, n), F32), pltpu.SemaphoreType.DMA((n_i * n_copies,))],
        args=[a, b] + ([] if fresh else [buf]), aliases=None if fresh else {2: 0}, name=name, exchange=exchange)


def _heads_to_lanes(x3):
    return jnp.concatenate([x3[g] for g in range(GROUP)], axis=1)


def _lanes_to_heads(xt):
    return jnp.concatenate([xt[:, g * BLOCK:(g + 1) * BLOCK] for g in range(GROUP)], axis=0)


def _attn_queries(kvh, q_ref, gq_col):
    cols = slice(kvh * GROUP * HEAD_DIM, (kvh + 1) * GROUP * HEAD_DIM)
    q3 = q_ref[:, cols].T.reshape(GROUP, HEAD_DIM, BLOCK)
    rq = lax.rsqrt(jnp.mean(q3 * q3, axis=1, keepdims=True) + EPS)
    qhat = q3 * rq
    return qhat, rq, _heads_to_lanes(_bf(qhat * (gq_col * (HEAD_DIM ** -0.5))))


def _from_prev():
    j = lax.broadcasted_iota(jnp.int32, (BLOCK, GROUP * BLOCK), 0)
    i = lax.broadcasted_iota(jnp.int32, (BLOCK, GROUP * BLOCK), 1) & (BLOCK - 1)
    return j > i


def _attn_probs(n, kvh, qts, kvp_ref, kvc_ref, gk, sink_ref):
    kcols = slice(kvh * HEAD_DIM, (kvh + 1) * HEAD_DIM)
    k = jnp.concatenate([kvp_ref[:, kcols], kvc_ref[:, kcols]], axis=0)
    rk, khat = _rms_stats(k)
    st = _dot(_bf(khat * gk), qts)
    f = jnp.where(_from_prev(), jnp.where(n > 0, st[0:BLOCK], -1e30), st[BLOCK:2 * BLOCK])
    sink = jnp.concatenate([jnp.broadcast_to(sink_ref[0:1, kvh * GROUP + g:kvh * GROUP + g + 1], (1, BLOCK))
                            for g in range(GROUP)], axis=1)
    m = jnp.maximum(jnp.max(f, axis=0, keepdims=True), sink)
    e = jnp.exp(f - m)
    es = jnp.exp(sink - m)
    inv = 1.0 / (jnp.sum(e, axis=0, keepdims=True) + es)
    return e * inv, es * inv


def _unfold(from_prev, xf):
    return _bf(jnp.concatenate([jnp.where(from_prev, xf, 0.0), jnp.where(from_prev, 0.0, xf)], axis=0))


def _attn_fwd(q_a, kv_a, gq_col, gk, sinks):
    t = q_a.shape[0]
    nb = t // BLOCK

    def kern(q_ref, kvp_ref, kvc_ref, gq_ref, gk_ref, sink_ref, o_ref, pf_ref, ps_ref):
        n = pl.program_id(0)
        kvt = jnp.concatenate([kvp_ref[...].T, kvc_ref[...].T], axis=1)
        for kvh in range(N_KV_HEADS):
            _, _, qts = _attn_queries(kvh, q_ref, gq_ref[...])
            pf, psink = _attn_probs(n, kvh, qts, kvp_ref, kvc_ref, gk_ref[...], sink_ref)
            lanes = slice(kvh * GROUP * BLOCK, (kvh + 1) * GROUP * BLOCK)
            pf_ref[:, lanes] = pf
            ps_ref[:, lanes] = psink
            vt = _bf(kvt[ATT_KV + kvh * HEAD_DIM:ATT_KV + (kvh + 1) * HEAD_DIM, :])
            out_t = _dot(vt, _unfold(_from_prev(), pf))
            cols = slice(kvh * GROUP * HEAD_DIM, (kvh + 1) * GROUP * HEAD_DIM)
            o_ref[:, cols] = _bf(_lanes_to_heads(out_t).T)

    small = lambda a: pl.BlockSpec(a.shape, lambda n: (0, 0))
    folded = N_KV_HEADS * GROUP * BLOCK
    return dict(
        kern=kern,
        in_specs=[pl.BlockSpec((BLOCK, ATT_Q), lambda n: (n, 0)),
                  pl.BlockSpec((BLOCK, 2 * ATT_KV), lambda n: (jnp.maximum(n - 1, 0), 0)),
                  pl.BlockSpec((BLOCK, 2 * ATT_KV), lambda n: (n, 0)),
                  small(gq_col), small(gk), small(sinks)],
        out_specs=[pl.BlockSpec((BLOCK, ATT_Q), lambda n: (n, 0)), pl.BlockSpec((BLOCK, folded), lambda n: (n, 0)),
                   pl.BlockSpec((None, 1, folded), lambda n: (n, 0, 0))],
        out_shape=[jax.ShapeDtypeStruct((t, ATT_Q), BF16), jax.ShapeDtypeStruct((t, folded), F32),
                   jax.ShapeDtypeStruct((nb, 1, folded), F32)],
        scratch=[], args=[q_a, kv_a, kv_a, gq_col, gk, sinks])


def _attn_bwd(q_a, kv_a, d_attn, probs, sink_probs, gq_col, gk, gk_col):
    t = q_a.shape[0]
    nb = t // BLOCK

    def kern(q_ref, kvp_ref, kvc_ref, do_ref, pf_ref, ps_ref, gq_ref, gk_ref, gkc_ref,
             dq_ref, dkv_ref, dgq_ref, dgk_ref, dsink_ref, band_k, band_v, carry_k, carry_v):
        n = pl.program_id(0)
        gq_v = gq_ref[...]
        gk_v = gk_ref[...]

        @pl.when(n == 0)
        def _():
            carry_k[...] = jnp.zeros_like(carry_k)
            carry_v[...] = jnp.zeros_like(carry_v)
            dgq_ref[...] = jnp.zeros_like(dgq_ref)
            dgk_ref[...] = jnp.zeros_like(dgk_ref)
            dsink_ref[...] = jnp.zeros_like(dsink_ref)

        @pl.when(n == nb)
        def _():
            band_k[...] = jnp.zeros_like(band_k)
            band_v[...] = jnp.zeros_like(band_v)

        @pl.when(n < nb)
        def _():
            lane16 = lax.broadcasted_iota(jnp.int32, (1, N_Q_HEADS), 1)
            dsink = jnp.zeros((1, N_Q_HEADS), F32)
            dgq = jnp.zeros((HEAD_DIM, 1), F32)
            gk_col = gkc_ref[...]
            kvt = jnp.concatenate([kvp_ref[...].T, kvc_ref[...].T], axis=1)
            from_prev = _from_prev()
            for kvh in range(N_KV_HEADS):
                qhat, rq, qts = _attn_queries(kvh, q_ref, gq_v)
                lanes = slice(kvh * GROUP * BLOCK, (kvh + 1) * GROUP * BLOCK)
                pf = pf_ref[:, lanes]
                cols = slice(kvh * GROUP * HEAD_DIM, (kvh + 1) * GROUP * HEAD_DIM)
                vcols = slice(ATT_KV + kvh * HEAD_DIM, ATT_KV + (kvh + 1) * HEAD_DIM)
                dot = _heads_to_lanes(_bf(do_ref[:, cols].astype(F32).T.reshape(GROUP, HEAD_DIM, BLOCK)))
                vb = _bf(jnp.concatenate([kvp_ref[:, vcols], kvc_ref[:, vcols]], axis=0))
                dpt = _dot(vb, dot)
                dpf = jnp.where(from_prev, dpt[0:BLOCK], dpt[BLOCK:2 * BLOCK])
                delta = jnp.sum(pf * dpf, axis=0, keepdims=True)
                dst = _unfold(from_prev, pf * (dpf - delta))
                dsk = ps_ref[:, lanes] * delta
                for g in range(GROUP):
                    tot = jnp.sum(dsk[:, g * BLOCK:(g + 1) * BLOCK], axis=1, keepdims=True)
                    dsink = dsink - jnp.where(lane16 == kvh * GROUP + g, tot, 0.0)
                kt = kvt[kvh * HEAD_DIM:(kvh + 1) * HEAD_DIM, :]
                knt = _bf(kt * lax.rsqrt(jnp.mean(kt * kt, axis=0, keepdims=True) + EPS) * gk_col)
                dqn = (_dot(knt, dst) * (HEAD_DIM ** -0.5))
                band_k[kvh] = _dot_nt(dst, qts)
                band_v[kvh] = _dot_nt(_unfold(from_prev, pf), dot)
                dqn3 = _lanes_to_heads(dqn).reshape(GROUP, HEAD_DIM, BLOCK)
                u = dqn3 * gq_v
                dq3 = rq * (u - qhat * jnp.mean(u * qhat, axis=1, keepdims=True))
                dgq = dgq + jnp.sum(jnp.sum(dqn3 * qhat, axis=0), axis=1, keepdims=True)
                dq_ref[:, cols] = _bf(dq3.reshape(GROUP * HEAD_DIM, BLOCK).T)
            dsink_ref[...] += dsink
            dgq_ref[...] += dgq

        dgk = jnp.zeros((1, HEAD_DIM), F32)
        for kvh in range(N_KV_HEADS):
            kcols = slice(kvh * HEAD_DIM, (kvh + 1) * HEAD_DIM)
            vcols = slice(ATT_KV + kvh * HEAD_DIM, ATT_KV + (kvh + 1) * HEAD_DIM)
            dkn = carry_k[kvh] + band_k[kvh, 0:BLOCK, :]
            dv = carry_v[kvh] + band_v[kvh, 0:BLOCK, :]
            rk, khat = _rms_stats(kvp_ref[:, kcols])
            dk, dgain = _rms_bwd(dkn, khat, rk, gk_v)
            dgk = dgk + jnp.sum(dgain, axis=0, keepdims=True)
            dkv_ref[:, kcols] = _bf(dk)
            dkv_ref[:, vcols] = _bf(dv)
            carry_k[kvh] = band_k[kvh, BLOCK:2 * BLOCK, :]
            carry_v[kvh] = band_v[kvh, BLOCK:2 * BLOCK, :]
        dgk_ref[...] += dgk

    small = lambda a: pl.BlockSpec(a.shape, lambda n: (0, 0))
    last = nb - 1
    return dict(
        kern=kern,
        in_specs=[pl.BlockSpec((BLOCK, ATT_Q), lambda n: (jnp.minimum(n, last), 0)),
                  pl.BlockSpec((BLOCK, 2 * ATT_KV), lambda n: (jnp.maximum(n - 1, 0), 0)),
                  pl.BlockSpec((BLOCK, 2 * ATT_KV), lambda n: (jnp.minimum(n, last), 0)),
                  pl.BlockSpec((BLOCK, ATT_Q), lambda n: (jnp.minimum(n, last), 0)),
                  pl.BlockSpec((BLOCK, probs.shape[1]), lambda n: (jnp.minimum(n, last), 0)),
                  pl.BlockSpec((None, 1, probs.shape[1]), lambda n: (jnp.minimum(n, last), 0, 0)),
                  small(gq_col), small(gk), small(gk_col)],
        out_specs=[pl.BlockSpec((BLOCK, ATT_Q), lambda n: (jnp.minimum(n, last), 0)),
                   pl.BlockSpec((BLOCK, 2 * ATT_KV), lambda n: (jnp.maximum(n - 1, 0), 0)),
                   pl.BlockSpec((HEAD_DIM, 1), lambda n: (0, 0)),
                   pl.BlockSpec((1, HEAD_DIM), lambda n: (0, 0)),
                   pl.BlockSpec((1, N_Q_HEADS), lambda n: (0, 0))],
        out_shape=[jax.ShapeDtypeStruct((t, ATT_Q), BF16), jax.ShapeDtypeStruct((t, 2 * ATT_KV), BF16),
                   jax.ShapeDtypeStruct((HEAD_DIM, 1), F32), jax.ShapeDtypeStruct((1, HEAD_DIM), F32),
                   jax.ShapeDtypeStruct((1, N_Q_HEADS), F32)],
        scratch=[pltpu.VMEM((N_KV_HEADS, 2 * BLOCK, HEAD_DIM), F32),
                 pltpu.VMEM((N_KV_HEADS, 2 * BLOCK, HEAD_DIM), F32),
                 pltpu.VMEM((N_KV_HEADS, BLOCK, HEAD_DIM), F32),
                 pltpu.VMEM((N_KV_HEADS, BLOCK, HEAD_DIM), F32)],
        args=[q_a, kv_a, kv_a, d_attn, probs, sink_probs, gq_col, gk, gk_col])


def _ret_tables(t, exchange):
    theta = 1.0 / (RET_ROT_BASE ** jnp.linspace(0.0, 1.0, RET_QK_DIM // 2, dtype=F32))
    theta2 = jnp.repeat(theta, 2)[None, :]
    sign = jnp.tile(jnp.array([-1.0, 1.0], F32), RET_QK_DIM // 2)[None, :]

    def kern(theta_ref, sign_ref, cos_ref, sin_ref):
        first = pl.program_id(0) * RET_CHUNK
        pos = (first + lax.broadcasted_iota(jnp.int32, (RET_CHUNK, RET_QK_DIM), 0)).astype(F32)
        ang = pos * theta_ref[...]
        cos_ref[...] = jnp.cos(ang)
        sin_ref[...] = jnp.sin(ang) * sign_ref[...]

    row = pl.BlockSpec((1, RET_QK_DIM), lambda n: (0, 0))
    blk = pl.BlockSpec((RET_CHUNK, RET_QK_DIM), lambda n: (n, 0))
    cos, sin_s, *got = _pallas(kern, grid=(t // RET_CHUNK,), in_specs=[row, row], out_specs=[blk, blk],
                               out_shape=[jax.ShapeDtypeStruct((t, RET_QK_DIM), F32)] * 2, args=[theta2, sign],
                               name="position_tables", exchange=exchange)
    log_gamma = jnp.log(1.0 - 2.0 ** (-5.0 - jnp.arange(RET_HEADS, dtype=F32)))
    i = jnp.arange(RET_CHUNK, dtype=F32)
    diff = i[:, None] - i[None, :]
    causal = diff >= 0
    decay = jnp.where(causal[None], jnp.exp(jnp.where(causal, diff, 0.0)[None] * log_gamma[:, None, None]), 0.0)
    xi = jnp.exp((i + 1.0)[None, :] * log_gamma[:, None])[:, :, None]
    zeta = jnp.exp((RET_CHUNK - 1.0 - i)[None, :] * log_gamma[:, None])[:, :, None]
    gch = jnp.broadcast_to(jnp.exp(RET_CHUNK * log_gamma)[:, None, None], (RET_HEADS, 1, 128))
    return (cos, sin_s, decay, xi, zeta, gch), got


def _swap_pairs(x):
    lane = lax.broadcasted_iota(jnp.int32, x.shape, 1)
    return jnp.where((lane & 1) == 0, pltpu.roll(x, RET_QK_DIM - 1, 1), pltpu.roll(x, 1, 1))


def _rotate(x, cos, sin_s):
    return x * cos + _swap_pairs(x) * sin_s


def _rotate_bwd(dy, cos, sin_s):
    return dy * cos + _swap_pairs(dy * sin_s)


def _ret_specs(order):
    qk = pl.BlockSpec((RET_CHUNK, RET_QK), lambda j: (order(j), 0))
    v = pl.BlockSpec((RET_CHUNK, RET_V), lambda j: (order(j), 0))
    dec = pl.BlockSpec((RET_HEADS, RET_CHUNK, RET_CHUNK), lambda j: (0, 0, 0))
    col = pl.BlockSpec((RET_HEADS, RET_CHUNK, 1), lambda j: (0, 0, 0))
    gch = pl.BlockSpec((RET_HEADS, 1, 128), lambda j: (0, 0, 0))
    st = pl.BlockSpec((RET_HEADS, None, RET_QK_DIM, RET_V_DIM), lambda j: (0, order(j), 0, 0))
    pos = pl.BlockSpec((RET_CHUNK, RET_QK_DIM), lambda j: (order(j), 0))
    return qk, v, dec, col, gch, st, pos


def _ret_fwd(q_r, k_r, v_r, g_r, tables):
    t = q_r.shape[0]
    nc = t // RET_CHUNK
    cos, sin_s, decay, xi, zeta, gch = tables

    def kern(q_ref, k_ref, v_ref, g_ref, cos_ref, sin_ref, dec_ref, xi_ref, zeta_ref, gch_ref,
             o_ref, ret_ref, st_ref, state):
        @pl.when(pl.program_id(0) == 0)
        def _():
            state[...] = jnp.zeros_like(state)

        cos_t = cos_ref[...]
        sin_t = sin_ref[...]
        for h in range(RET_HEADS):
            qc = slice(h * RET_QK_DIM, (h + 1) * RET_QK_DIM)
            vc = slice(h * RET_V_DIM, (h + 1) * RET_V_DIM)
            qs = _bf(_rotate(q_ref[:, qc], cos_t, sin_t))
            ks = _rotate(k_ref[:, qc] * (RET_QK_DIM ** -0.5), cos_t, sin_t)
            vb = v_ref[:, vc]
            s_old = state[h]
            sb = _bf(s_old)
            st_ref[h] = sb
            inner = _dot_nt(qs, _bf(ks)) * dec_ref[h]
            out = _dot(_bf(inner), vb) + _dot(qs, sb) * xi_ref[h]
            state[h] = gch_ref[h, :, 0:1] * s_old + _dot_tn(_bf(ks * zeta_ref[h]), vb)
            o_ref[:, vc] = out
            r, rn = _rms_stats(out)
            g = g_ref[:, vc]
            ret_ref[:, vc] = _bf(g * jax.nn.sigmoid(g) * rn)

    qk, v, dec, col, gsp, st, pos = _ret_specs(lambda j: j)
    return dict(
        kern=kern,
        in_specs=[qk, qk, v, v, pos, pos, dec, col, col, gsp],
        out_specs=[v, v, st],
        out_shape=[jax.ShapeDtypeStruct((t, RET_V), F32), jax.ShapeDtypeStruct((t, RET_V), BF16),
                   jax.ShapeDtypeStruct((RET_HEADS, nc, RET_QK_DIM, RET_V_DIM), BF16)],
        scratch=[pltpu.VMEM((RET_HEADS, RET_QK_DIM, RET_V_DIM), F32)],
        args=[q_r, k_r, v_r, g_r, cos, sin_s, decay, xi, zeta, gch])


def _ret_bwd(q_r, k_r, v_r, d_o, states, tables):
    t = q_r.shape[0]
    nc = t // RET_CHUNK
    cos, sin_s, decay, xi, zeta, gch = tables

    def kern(q_ref, k_ref, v_ref, do_ref, st_ref, cos_ref, sin_ref, dec_ref, xi_ref, zeta_ref, gch_ref,
             d_ref, dstate):
        dq_ref, dk_ref = d_ref.at[:, 0:RET_QK], d_ref.at[:, RET_QK:2 * RET_QK]
        dv_ref = d_ref.at[:, 2 * RET_QK:2 * RET_QK + RET_V]

        @pl.when(pl.program_id(0) == 0)
        def _():
            dstate[...] = jnp.zeros_like(dstate)

        @pl.when(pl.program_id(0) < nc)
        def _():
            cos_t = cos_ref[...]
            sin_t = sin_ref[...]
            scale = RET_QK_DIM ** -0.5
            for h in range(RET_HEADS):
                qc = slice(h * RET_QK_DIM, (h + 1) * RET_QK_DIM)
                vc = slice(h * RET_V_DIM, (h + 1) * RET_V_DIM)
                qs = _bf(_rotate(q_ref[:, qc], cos_t, sin_t))
                ks = _rotate(k_ref[:, qc] * scale, cos_t, sin_t)
                ksb = _bf(ks)
                vb = v_ref[:, vc]
                d_o_t = do_ref[:, vc]
                dob = _bf(d_o_t)
                doxb = _bf(d_o_t * xi_ref[h])
                dec = dec_ref[h]
                ds_old = dstate[h]
                dsb = _bf(ds_old)
                pb = _bf(_dot_nt(qs, ksb) * dec)
                dpb = _bf(_dot_nt(dob, vb) * dec)
                dqs = _dot(dpb, ksb) + _dot_nt(doxb, st_ref[h])
                dks = _dot_tn(dpb, qs) + _dot_nt(vb, dsb) * zeta_ref[h]
                dv_ref[:, vc] = _bf(_dot_tn(pb, dob) + _dot(_bf(ks * zeta_ref[h]), dsb))
                dstate[h] = gch_ref[h, :, 0:1] * ds_old + _dot_tn(qs, doxb)
                dq_ref[:, qc] = _bf(_rotate_bwd(dqs, cos_t, sin_t))
                dk_ref[:, qc] = _bf(_rotate_bwd(dks, cos_t, sin_t) * scale)

    backwards = lambda j: jnp.maximum(nc - 1 - j, 0)
    qk, v, dec, col, gsp, st, pos = _ret_specs(backwards)
    return dict(
        kern=kern,
        in_specs=[qk, qk, v, v, st, pos, pos, dec, col, col, gsp],
        out_specs=[pl.BlockSpec((RET_CHUNK, 2 * RET_QK + RET_V), lambda j: (backwards(j), 0))],
        out_shape=[jax.ShapeDtypeStruct((t, 2 * RET_QK + RET_V), BF16)],
        scratch=[pltpu.VMEM((RET_HEADS, RET_QK_DIM, RET_V_DIM), F32)],
        args=[q_r, k_r, v_r, d_o, states, cos, sin_s, decay, xi, zeta, gch])


def _position():
    return lax.axis_index("x"), lax.axis_index("y"), lax.axis_index("c")


def _gather_exchange(owns, forward_at):
    n = len(owns)

    def copies(ins, outs, send_sems, recv_sems, base):
        x, y, c = _position()
        sibling = (x, y, 1 - c)
        chips = [(1 - x, y), (x, 1 - y), (1 - x, 1 - y)]
        my_chip = 2 * x + y

        def slab(a, chip, hf):
            half = owns[a].shape[0] // 2
            return outs[a].at[chip, pl.ds(hf * half, half), :]

        def copy(k, src, dst, to):
            return pltpu.make_async_remote_copy(src_ref=src, dst_ref=dst, send_sem=send_sems.at[base + k],
                                                recv_sem=recv_sems.at[base + k], device_id=to, device_id_type=MESH)

        first, passed, from_sibling = [], [], []
        for a in range(n):
            half = owns[a].shape[0] // 2
            for k, (cx, cy) in enumerate(chips):
                first.append(copy(6 * a + k, ins[a].at[pl.ds(c * half, half), :], slab(a, my_chip, c), (cx, cy, c)))
                landed = slab(a, 2 * cx + cy, c)
                passed.append(copy(6 * a + 3 + k, landed, landed, sibling))
                theirs = slab(a, 2 * cx + cy, 1 - c)
                from_sibling.append(copy(6 * a + 3 + k, theirs, theirs, sibling))
        return first, passed, from_sibling

    def start(*args):
        first, _, _ = copies(*args)
        for cp in first:
            cp.start()

    def forward(*args):
        first, passed, _ = copies(*args)
        for arrived, cp in zip(first, passed):
            arrived.wait_recv()
            cp.start()

    def finish(*args):
        first, passed, from_sibling = copies(*args)
        for cp in from_sibling:
            cp.wait_recv()
        for cp in first + passed:
            cp.wait_send()

    outs = [jax.ShapeDtypeStruct((N_CHIPS, *a.shape), a.dtype) for a in owns]
    return _Exchange(owns, outs, 6 * n, [(0.0, start), (forward_at, forward), (1.0, finish)])


def _symmetric_exchange(ins, outs, plan):
    n_sems = len(plan([None] * len(ins), [None] * len(outs), 0, 0, 0, dry=True))

    def copies(in_refs, out_refs, send_sems, recv_sems, base):
        x, y, c = _position()
        return [pltpu.make_async_remote_copy(src_ref=src, dst_ref=dst, send_sem=send_sems.at[base + k],
                                             recv_sem=recv_sems.at[base + k], device_id=dev, device_id_type=MESH)
                for k, (src, dst, dev) in enumerate(plan(in_refs, out_refs, x, y, c, dry=False))]

    def start(*args):
        for cp in copies(*args):
            cp.start()

    def finish(*args):
        for cp in copies(*args):
            cp.wait()

    return _Exchange(ins, outs, n_sems, [(0.0, start), (1.0, finish)])


def _pair_exchange(gs):
    def plan(in_refs, out_refs, x, y, c, dry):
        out = []
        for a, g in enumerate(gs):
            half = g.shape[1] // 2
            for k in range(N_CHIPS):
                out.append(None if dry else (in_refs[a].at[k, pl.ds((1 - c) * half, half), :], out_refs[a].at[k],
                                             (x, y, 1 - c)))
        return out

    outs = [jax.ShapeDtypeStruct((g.shape[0], g.shape[1] // 2, g.shape[2]), g.dtype) for g in gs]
    return _symmetric_exchange(gs, outs, plan)


def _pair_sum(g, from_sibling, c_arr, *, tile, name):
    n, rows, width = g.shape
    tiles = (rows // 2) // tile

    def kern(c_ref, g_ref, s_ref, o_ref):
        o_ref[...] = _bf(g_ref[...] + s_ref[...])

    return pl.pallas_call(
        kern,
        grid_spec=pltpu.PrefetchScalarGridSpec(
            num_scalar_prefetch=1, grid=(n, tiles),
            in_specs=[pl.BlockSpec((None, tile, width), lambda k, i, c: (k, c[0] * tiles + i, 0)),
                      pl.BlockSpec((None, tile, width), lambda k, i, c: (k, i, 0))],
            out_specs=pl.BlockSpec((None, tile, width), lambda k, i, c: (k, i, 0))),
        out_shape=jax.ShapeDtypeStruct((n, rows // 2, width), BF16), name=name,
        compiler_params=_params(("parallel", "parallel")),
    )(c_arr, g, from_sibling)


def _scatter_to_owners(hsums):
    def plan(in_refs, out_refs, x, y, c, dry):
        out = []
        for a in range(len(hsums)):
            for k, (cx, cy) in enumerate([(1 - x, y), (x, 1 - y), (1 - x, 1 - y)]):
                out.append(None if dry else (in_refs[a].at[2 * cx + cy], out_refs[a].at[k], (cx, cy, c)))
        return out

    outs = [jax.ShapeDtypeStruct((3, *h.shape[1:]), h.dtype) for h in hsums]
    return _symmetric_exchange(hsums, outs, plan)


def _sum_chips(hsum, parts, chip_arr, *, tile, name):
    n, half, width = parts.shape

    def kern(chip_ref, h_ref, p_ref, o_ref):
        acc = h_ref[...].astype(F32)
        for k in range(n):
            acc = acc + p_ref[k].astype(F32)
        o_ref[...] = acc

    return pl.pallas_call(
        kern,
        grid_spec=pltpu.PrefetchScalarGridSpec(
            num_scalar_prefetch=1, grid=(half // tile,),
            in_specs=[pl.BlockSpec((None, tile, width), lambda i, chip: (chip[0], i, 0)),
                      pl.BlockSpec((n, tile, width), lambda i, chip: (0, i, 0))],
            out_specs=pl.BlockSpec((tile, width), lambda i, chip: (i, 0))),
        out_shape=jax.ShapeDtypeStruct((half, width), F32), name=name,
        compiler_params=_params(("parallel",)),
    )(chip_arr, hsum, parts)


def _share_halves(fhalves):
    def plan(in_refs, out_refs, x, y, c, dry):
        return [None if dry else (in_refs[a], out_refs[a], (x, y, 1 - c)) for a in range(len(fhalves))]

    return _symmetric_exchange(fhalves, [jax.ShapeDtypeStruct(f.shape, f.dtype) for f in fhalves], plan)


def _adamw_math(w, g, m, v):
    m = ADAM_B1 * m + (1.0 - ADAM_B1) * g
    v = ADAM_B2 * v + (1.0 - ADAM_B2) * (g * g)
    m_hat = m / (1.0 - ADAM_B1 ** ADAM_STEP)
    v_hat = v / (1.0 - ADAM_B2 ** ADAM_STEP)
    delta = -ADAM_LR * (m_hat / (jnp.sqrt(v_hat) + ADAM_EPS) + ADAM_WD * w)
    return delta, m, v


def _adamw(mats, g_mine, g_other, c_arr, *, tile, name):
    width = g_mine.shape[1]
    tiles_per_half = g_mine.shape[0] // tile
    n_tiles = [w.shape[0] // tile for w, _, _, _ in mats]
    n_mats = len(mats)

    def kern(c_ref, *refs):
        ins, outs = refs[:5 * n_mats], refs[5 * n_mats:]
        for j, (_, _, _, row_off) in enumerate(mats):
            w_ref, gm_ref, go_ref, m_ref, v_ref = ins[5 * j:5 * j + 5]
            i = jnp.minimum(pl.program_id(0), n_tiles[j] - 1)
            in_my_half = ((row_off // tile + i) // tiles_per_half) == c_ref[0]
            g = jnp.where(in_my_half, gm_ref[...], go_ref[...])
            d, nm, nv = _adamw_math(w_ref[...], g, m_ref[...], v_ref[...])
            for out_ref, val in zip(outs[4 * j:4 * j + 4], (g, d, nm, nv)):
                out_ref[...] = val

    in_specs, out_specs, out_shape, args = [], [], [], []
    for (w, m, v, row_off), nt in zip(mats, n_tiles):
        full = pl.BlockSpec((tile, width), lambda i, c, nt=nt: (jnp.minimum(i, nt - 1), 0))

        def half(mine, nt=nt, first=row_off // tile):
            def index(i, c):
                pos = first + jnp.minimum(i, nt - 1)
                used = ((pos // tiles_per_half) == c[0]) == mine
                return (jnp.where(used, pos % tiles_per_half, 0), 0)
            return pl.BlockSpec((tile, width), index)

        in_specs += [full, half(True), half(False), full, full]
        out_specs += [full] * 4
        out_shape += [jax.ShapeDtypeStruct(w.shape, F32)] * 4
        args += [w, g_mine, g_other, m, v]
    outs = pl.pallas_call(
        kern,
        grid_spec=pltpu.PrefetchScalarGridSpec(num_scalar_prefetch=1, grid=(max(n_tiles),), in_specs=in_specs,
                                               out_specs=out_specs),
        out_shape=out_shape, name=name, compiler_params=_params(("arbitrary",)),
    )(c_arr, *args)
    return [outs[4 * j:4 * j + 4] for j in range(n_mats)]


def _small_step(partials, params):
    slots = ((0, 0, D_MODEL), (1, 0, D_MODEL), (2, 0, HEAD_DIM), (2, 128, HEAD_DIM), (2, 256, N_Q_HEADS))
    loss_slot = (2, 384, 128)

    def body(*refs):
        loss_ref, dg1_ref, dg2_ref, dgq_ref, dgk_ref, dsink_ref = refs[:6]
        p_refs, out_refs = refs[6:21], refs[21:42]
        mine, gathered, send_sems, recv_sems = refs[42:]
        x, y, c = _position()
        me = 4 * x + 2 * y + c
        mine[...] = jnp.zeros_like(mine)
        for (row, lane, n), val in zip(slots + (loss_slot,), (
                jnp.sum(dg1_ref[...], axis=0, keepdims=True), jnp.sum(dg2_ref[...], axis=0, keepdims=True),
                dgq_ref[...], dgk_ref[...], dsink_ref[...], jnp.sum(loss_ref[...], axis=0, keepdims=True))):
            mine[row:row + 1, lane:lane + n] = val
        copies = []
        for k in range(1, N_DEV):
            flip = (k >> 2) & 1, (k >> 1) & 1, k & 1
            to = (x ^ flip[0], y ^ flip[1], c ^ flip[2])
            cp = pltpu.make_async_remote_copy(
                src_ref=mine, dst_ref=gathered.at[me], send_sem=send_sems.at[k - 1], recv_sem=recv_sems.at[k - 1],
                device_id=to, device_id_type=MESH)
            cp.start()
            copies.append(cp)
        gathered[me] = mine[...]
        for k in range(1, N_DEV):
            flip = (k >> 2) & 1, (k >> 1) & 1, k & 1
            src = 4 * (x ^ flip[0]) + 2 * (y ^ flip[1]) + (c ^ flip[2])
            pltpu.make_async_remote_copy(
                src_ref=mine, dst_ref=gathered.at[src], send_sem=send_sems.at[k - 1], recv_sem=recv_sems.at[k - 1],
                device_id=(x, y, c), device_id_type=MESH).wait_recv()
        for cp in copies:
            cp.wait_send()
        total = gathered[0]
        for k in range(1, N_DEV):
            total = total + gathered[k]
        row, lane, n = loss_slot
        out_refs[0][...] = total[row:row + 1, lane:lane + n]
        for i, (row, lane, n) in enumerate(slots):
            g = total[row:row + 1, lane:lane + n]
            d, nm, nv = _adamw_math(p_refs[i][...], g, p_refs[5 + i][...], p_refs[10 + i][...])
            for kind, val in enumerate((g, d, nm, nv)):
                out_refs[1 + 5 * kind + i][...] = val

    vm = pl.BlockSpec(memory_space=pltpu.VMEM)
    shapes = [jax.ShapeDtypeStruct((1, 128), F32)] + [jax.ShapeDtypeStruct((1, n), F32) for _, _, n in slots] * 4
    return pl.pallas_call(
        body, in_specs=[vm] * 21, out_specs=[vm] * 21, out_shape=shapes,
        scratch_shapes=[pltpu.VMEM((SMALL_ROWS, D_MODEL), F32), pltpu.VMEM((N_DEV, SMALL_ROWS, D_MODEL), F32),
                        pltpu.SemaphoreType.DMA((N_DEV - 1,)), pltpu.SemaphoreType.DMA((N_DEV - 1,))],
        name="small_step",
    )(*partials, *params)


def _with_own(gathered, own, my_chip):
    return lax.dynamic_update_slice(gathered, own[None], (my_chip, 0, 0))


def kernel(x, norm_mix_gain, w_in, q_norm_gain, k_norm_gain, attn_sinks, w_branch_attn, w_branch_ret, w_out, norm_ffn_gain, w_ffn_gate, w_ffn_up, w_ffn_down, loss_target, m_norm_mix_gain, m_w_in, m_q_norm_gain, m_k_norm_gain, m_attn_sinks, m_w_branch_attn, m_w_branch_ret, m_w_out, m_norm_ffn_gain, m_w_ffn_gate, m_w_ffn_up, m_w_ffn_down, v_norm_mix_gain, v_w_in, v_q_norm_gain, v_k_norm_gain, v_attn_sinks, v_w_branch_attn, v_w_branch_ret, v_w_out, v_norm_ffn_gain, v_w_ffn_gate, v_w_ffn_up, v_w_ffn_down):
    my_chip = 2 * lax.axis_index("x") + lax.axis_index("y")
    c_arr = lax.axis_index("c").astype(jnp.int32).reshape(1)
    chip_arr = my_chip.astype(jnp.int32).reshape(1)
    x_t, target = x[0], loss_target[0]
    g1, g2, gq, gk, sinks = norm_mix_gain, norm_ffn_gain, q_norm_gain, k_norm_gain, attn_sinks

    tr = lambda a: jnp.transpose(a[0])
    own_w_in = _bf(tr(w_in))
    own_rest = [_bf(a) for a in (tr(w_ffn_gate), tr(w_ffn_up), w_ffn_down[0], w_branch_attn[0], w_branch_ret[0],
                                 w_out[0])]
    tables, (got_w_in,) = _ret_tables(x_t.shape[0], _gather_exchange([own_w_in], 0.9))
    w_in_t = _with_own(got_w_in, own_w_in, my_chip).reshape(D_IN, D_MODEL)
    h1, q_a, kv_a, q_r, k_r, v_r, g_r, z_a, z_r, *got_rest = _proj_fwd(x_t, g1, w_in_t, _gather_exchange(own_rest, 0.8))
    wg_t, wu_t, wd, wba, wbr, wout = [_with_own(got, own, my_chip).reshape(N_CHIPS * own.shape[0], D_MODEL)
                                      for got, own in zip(got_rest, own_rest)]

    gq_col, gk_col = gq.reshape(HEAD_DIM, 1), gk.reshape(HEAD_DIM, 1)
    attn, probs, sink_probs, o_ret, ret, states = _fused(
        [_attn_fwd(q_a, kv_a, gq_col, gk, sinks), _ret_fwd(q_r, k_r, v_r, g_r, tables)],
        grid=(x_t.shape[0] // BLOCK,), name="mixers_fwd")
    ba, br, merged, x1, h2 = _mix_fwd(attn, ret, z_a, z_r, x_t, wba, wbr, wout, g2)
    act, dgate, dup, dyb, dx1, dx1b, loss_p, dg2_p = _ffn_fwd_bwd(h2, x1, target, wg_t, wu_t, wd, g2)

    def pairs(row0, rows):
        return lambda i: [(h * rows, rows, (2 * i + h, pl.ds(row0, rows), slice(None))) for h in range(2)]

    f_block = jax.ShapeDtypeStruct((N_CHIPS, 3 * FF_SH, D_MODEL), F32)
    f_block, = _dw(dgate, h2, tm=2 * FF_SH, place=pairs(0, FF_SH), buf=f_block, name="dw_gate")
    f_block, = _dw(dup, h2, tm=2 * FF_SH, place=pairs(FF_SH, FF_SH), buf=f_block, name="dw_up")
    f_block, = _dw(act, dyb, tm=2 * FF_SH, place=pairs(2 * FF_SH, FF_SH), buf=f_block, name="dw_down")
    (dba, dbr, d_attn, d_o, d_gz, sib_ffn) = _mix_bwd(
        dx1b, z_a, z_r, ba, br, g_r, o_ret, wout, wba, wbr, _pair_exchange([f_block]))
    f_sum = _pair_sum(f_block, sib_ffn, c_arr, tile=528, name="pair_sum_ffn")

    def quarters(row0, rows):
        return lambda i: [(k * rows, rows, (k, pl.ds(row0, rows), slice(None))) for k in range(N_CHIPS)]

    m_block = jax.ShapeDtypeStruct((N_CHIPS, D_MODEL, D_MODEL), F32)
    m_block, = _dw(attn, dba, tm=ATT_Q, place=quarters(0, 256), buf=m_block, name="dw_ba", token_tile=4096)
    m_block, = _dw(ret, dbr, tm=D_MODEL, place=pairs(256, 512), buf=m_block, name="dw_br", token_tile=4096)
    m_block, = _dw(merged, dx1b, tm=D_MODEL, place=quarters(768, 256), buf=m_block, name="dw_out", token_tile=4096)

    def w_in_rows(off, w):
        tm = min(w, D_MODEL)
        return dict(tm=tm, place=lambda i: [(0, tm, (pl.ds(off + i * tm, tm), slice(None)))])

    w_block = jax.ShapeDtypeStruct((D_IN, D_MODEL), F32)
    w_block, sib_mix = _dw(d_gz, h1, buf=w_block, name="dw_in_gz", exchange=_pair_exchange([m_block]),
                           **w_in_rows(P_GR[0], d_gz.shape[1]))
    m_sum = _pair_sum(m_block, sib_mix, c_arr, tile=256, name="pair_sum_mix")

    (dq_a, dkv_a, dgq, dgk, dsinks, d_ret, got_ffn_sums, got_mix_sums) = _fused(
        [_attn_bwd(q_a, kv_a, d_attn, probs, sink_probs, gq_col, gk, gk_col),
         _ret_bwd(q_r, k_r, v_r, d_o, states, tables)],
        grid=(x_t.shape[0] // BLOCK + 1,), name="mixers_bwd", exchange=_scatter_to_owners([f_sum, m_sum]))
    dgq = dgq.reshape(1, HEAD_DIM)
    ffn_half = _sum_chips(f_sum, got_ffn_sums, chip_arr, tile=528, name="sum_chips_ffn")
    mix_half = _sum_chips(m_sum, got_mix_sums, chip_arr, tile=256, name="sum_chips_mix")
    w_block, = _dw(d_ret, h1, buf=w_block, name="dw_in_ret", **w_in_rows(P_QR[0], d_ret.shape[1]))
    w_block, ffn_other, mix_other = _dw(dq_a, h1, buf=w_block, name="dw_in_q", token_tile=4096,
                                        exchange=_share_halves([ffn_half, mix_half]), **w_in_rows(*P_QA))
    w_block, = _dw(dkv_a, h1, buf=w_block, name="dw_in_kv", token_tile=4096, **w_in_rows(*P_KVA))

    w_block = w_block.reshape(N_CHIPS, W_IN_SH, D_MODEL)
    sib_w_in, = _run_exchange(_pair_exchange([w_block]), "pair_exchange_w_in")
    w_sum = _pair_sum(w_block, sib_w_in, c_arr, tile=592, name="pair_sum_w_in")
    d_pieces = [dq_a, dkv_a, d_ret, d_gz]
    grad_x, dg1_p, got_w_in_sums = _proj_bwd(d_pieces, x_t, dx1, w_in_t, g1, _scatter_to_owners([w_sum]))
    w_in_half = _sum_chips(w_sum, got_w_in_sums, chip_arr, tile=592, name="sum_chips_w_in")
    w_in_other, = _run_exchange(_share_halves([w_in_half]), "share_halves_w_in")

    def update(name, g_half, g_other, tile, mats):
        outs = _adamw([tuple(tr(a) if t else a[0] for a in wmv) + (off,) for _, *wmv, off, t in mats],
                      g_half, g_other, c_arr, tile=tile, name=f"adamw_{name}")
        return {key: [jnp.transpose(o) if t else o for o in res] for (key, _, _, _, _, t), res in zip(mats, outs)}

    big = {
        **update("w_in", w_in_half, w_in_other, 592, [("w_in", w_in, m_w_in, v_w_in, 0, True)]),
        **update("ffn", ffn_half, ffn_other, 176, [
            ("wg", w_ffn_gate, m_w_ffn_gate, v_w_ffn_gate, 0, True),
            ("wu", w_ffn_up, m_w_ffn_up, v_w_ffn_up, FF_SH, True),
            ("wd", w_ffn_down, m_w_ffn_down, v_w_ffn_down, 2 * FF_SH, False)]),
        **update("mix", mix_half, mix_other, 128, [
            ("wba", w_branch_attn, m_w_branch_attn, v_w_branch_attn, 0, False),
            ("wbr", w_branch_ret, m_w_branch_ret, v_w_branch_ret, 256, False),
            ("wout", w_out, m_w_out, v_w_out, 768, False)])}

    loss_row, *small = _small_step(
        [loss_p.reshape(-1, 128), dg1_p.reshape(-1, D_MODEL), dg2_p.reshape(-1, D_MODEL), dgq, dgk, dsinks],
        [norm_mix_gain, norm_ffn_gain, q_norm_gain, k_norm_gain, attn_sinks,
         m_norm_mix_gain, m_norm_ffn_gain, m_q_norm_gain, m_k_norm_gain, m_attn_sinks,
         v_norm_mix_gain, v_norm_ffn_gain, v_q_norm_gain, v_k_norm_gain, v_attn_sinks])
    loss = loss_row[0, 0]

    def leaves(i):
        b = [big[n][i][None] for n in ("w_in", "wba", "wbr", "wout", "wg", "wu", "wd")]
        s1, s2, sq, sk, ss = small[5 * i:5 * i + 5]
        return [s1, b[0], sq, sk, ss, b[1], b[2], b[3], s2, b[4], b[5], b[6]]

    return (loss, grad_x[None], *leaves(0), *leaves(1), *leaves(2), *leaves(3))
```

```python
import jax
import jax.numpy as jnp
from jax import lax
from jax.experimental import pallas as pl
from jax.experimental.pallas import tpu as pltpu

F32 = jnp.float32
BF16 = jnp.bfloat16
MESH = pl.DeviceIdType.MESH

D_MODEL = 1024
EPS = 1e-6
HEAD_DIM = 64
N_Q_HEADS = 16
N_KV_HEADS = 2
GROUP = 8
BLOCK = 128
RET_HEADS = 4
RET_QK_DIM = 256
RET_V_DIM = 512
RET_CHUNK = 128
RET_ROT_BASE = 10000.0
D_FF = 2816
ATT_Q = N_Q_HEADS * HEAD_DIM
ATT_KV = N_KV_HEADS * HEAD_DIM
RET_QK = RET_HEADS * RET_QK_DIM
RET_V = RET_HEADS * RET_V_DIM
D_IN = 9472
ADAM_LR = 0.001
ADAM_B1 = 0.9
ADAM_B2 = 0.999
ADAM_EPS = 1e-08
ADAM_WD = 0.01
ADAM_STEP = 10

N_CHIPS = 4
N_DEV = 8
VMEM_LIMIT_BYTES = 60 * 1024 * 1024

P_QA = (0, 1024)
P_KVA = (1024, 256)
P_QR = (1280, 1024)
P_KR = (2304, 1024)
P_VR = (3328, 2048)
P_GR = (5376, 2048)
P_ZA = (7424, 1024)
P_ZR = (8448, 1024)

W_IN_SH = D_IN // N_CHIPS
FF_SH = D_FF // N_CHIPS

SMALL_ROWS = 8


def _dot(a, b):
    return jnp.dot(a, b, preferred_element_type=F32)


def _dot_nt(a, b):
    return lax.dot_general(a, b, (((1,), (1,)), ((), ())), preferred_element_type=F32)


def _dot_tn(a, b):
    return lax.dot_general(a, b, (((0,), (0,)), ((), ())), preferred_element_type=F32)


def _bf(x):
    return x.astype(BF16)


def _rms_stats(x):
    r = lax.rsqrt(jnp.mean(x * x, axis=-1, keepdims=True) + EPS)
    return r, x * r


def _rms_bwd(dy, xhat, r, gain):
    u = dy * gain
    dx = r * (u - xhat * jnp.mean(u * xhat, axis=-1, keepdims=True))
    return dx, dy * xhat


def _params(sem):
    return pltpu.CompilerParams(dimension_semantics=sem, vmem_limit_bytes=VMEM_LIMIT_BYTES)


_ANY = pl.BlockSpec(memory_space=pl.ANY)


class _Exchange:
    def __init__(self, ins, outs, n_sems, phases):
        self.ins, self.outs, self.n_sems, self.phases = list(ins), list(outs), n_sems, list(phases)


def _pallas(kern, *, grid, in_specs, out_specs, out_shape, args, name, scratch=(), exchange=None, aliases=None):
    aliases = aliases or {}
    if exchange is None:
        return pl.pallas_call(
            kern, grid=grid, in_specs=in_specs, out_specs=out_specs, out_shape=out_shape, name=name,
            scratch_shapes=list(scratch), input_output_aliases=aliases,
            compiler_params=_params(("arbitrary",) * len(grid)))(*args)
    n_in, n_out, n_sc = len(in_specs), len(out_specs), len(scratch)
    n_xi, n_xo = len(exchange.ins), len(exchange.outs)
    n_steps = 1
    for g in grid:
        n_steps *= g

    def wrapped(*refs):
        ins, refs = refs[:n_in], refs[n_in:]
        x_ins, refs = refs[:n_xi], refs[n_xi:]
        outs, refs = refs[:n_out], refs[n_out:]
        x_outs, refs = refs[:n_xo], refs[n_xo:]
        scr, (send_sems, recv_sems) = refs[:n_sc], refs[n_sc:]
        step = pl.program_id(0)
        for d in range(1, len(grid)):
            step = step * grid[d] + pl.program_id(d)
        for frac, fn in exchange.phases:
            at = min(int(frac * n_steps), n_steps - 1)

            @pl.when(step == at)
            def _(fn=fn):
                fn(x_ins, x_outs, send_sems, recv_sems, 0)

        kern(*ins, *outs, *scr)

    sems = [pltpu.SemaphoreType.DMA((exchange.n_sems,)), pltpu.SemaphoreType.DMA((exchange.n_sems,))]
    return pl.pallas_call(
        wrapped, grid=grid, in_specs=list(in_specs) + [_ANY] * n_xi, out_specs=list(out_specs) + [_ANY] * n_xo,
        out_shape=list(out_shape) + exchange.outs, name=name, scratch_shapes=list(scratch) + sems,
        input_output_aliases=aliases, compiler_params=_params(("arbitrary",) * len(grid)))(*args, *exchange.ins)


def _run_exchange(exchange, name):
    def body(*refs):
        n_i, n_o = len(exchange.ins), len(exchange.outs)
        for _, fn in exchange.phases:
            fn(refs[:n_i], refs[n_i:n_i + n_o], refs[n_i + n_o], refs[n_i + n_o + 1], 0)

    sems = [pltpu.SemaphoreType.DMA((exchange.n_sems,)), pltpu.SemaphoreType.DMA((exchange.n_sems,))]
    return pl.pallas_call(body, in_specs=[_ANY] * len(exchange.ins), out_specs=[_ANY] * len(exchange.outs),
                          out_shape=exchange.outs, scratch_shapes=sems, name=name)(*exchange.ins)


def _fused(parts, *, grid, name, exchange=None):
    counts = [(len(p["in_specs"]), len(p["out_specs"]), len(p["scratch"])) for p in parts]
    n_in, n_out = sum(c[0] for c in counts), sum(c[1] for c in counts)

    def kern(*refs):
        ins, outs, scr = refs[:n_in], refs[n_in:n_in + n_out], refs[n_in + n_out:]
        i0 = o0 = s0 = 0
        for p, (ni, no, ns) in zip(parts, counts):
            p["kern"](*ins[i0:i0 + ni], *outs[o0:o0 + no], *scr[s0:s0 + ns])
            i0, o0, s0 = i0 + ni, o0 + no, s0 + ns

    cat = lambda key: [a for p in parts for a in p[key]]
    return _pallas(kern, grid=grid, in_specs=cat("in_specs"), out_specs=cat("out_specs"), out_shape=cat("out_shape"),
                   scratch=cat("scratch"), args=cat("args"), name=name, exchange=exchange)


def _row_call(body, *, tm, row_ins, res_ins, row_outs, part_outs=(), name, exchange=None):
    t = row_ins[0].shape[0]
    n_tiles = t // tm
    in_specs = [pl.BlockSpec((tm, a.shape[1]), lambda i: (i, 0)) for a in row_ins]
    in_specs += [pl.BlockSpec(a.shape, lambda i: (0, 0), pipeline_mode=pl.Buffered(1)) for a in res_ins]
    out_shape = [jax.ShapeDtypeStruct((t, w), dt) for (w, dt) in row_outs]
    out_shape += [jax.ShapeDtypeStruct((n_tiles, 1, w), F32) for w in part_outs]
    out_specs = [pl.BlockSpec((tm, w), lambda i: (i, 0)) for (w, _) in row_outs]
    out_specs += [pl.BlockSpec((1, 1, w), lambda i: (i, 0, 0)) for w in part_outs]
    n_ri, n_re, n_ro = len(row_ins), len(res_ins), len(row_outs)

    def kern(*refs):
        body(refs[:n_ri], refs[n_ri:n_ri + n_re], refs[n_ri + n_re:n_ri + n_re + n_ro], refs[n_ri + n_re + n_ro:])

    return _pallas(kern, grid=(n_tiles,), in_specs=in_specs, out_specs=out_specs, out_shape=out_shape,
                   args=[*row_ins, *res_ins], name=name, exchange=exchange)


def _proj_fwd(x, g1, w_in_t, exchange):
    pieces = ((P_QA, F32), (P_KVA, F32), (P_QR, F32), (P_KR, F32), (P_VR, BF16), (P_GR, F32), (P_ZA, F32), (P_ZR, F32))

    def body(ri, re, ro, po):
        x_t = ri[0][...]
        r, xhat = _rms_stats(x_t)
        hb = _bf(xhat * re[0][...])
        ro[0][...] = hb
        for k, ((off, w), dt) in enumerate(pieces):
            ro[1 + k][...] = _dot_nt(hb, re[1][off:off + w, :]).astype(dt)

    outs = [(D_MODEL, BF16)] + [(w, dt) for ((_, w), dt) in pieces]
    return _row_call(body, tm=256, row_ins=[x], res_ins=[g1, w_in_t], row_outs=outs, name="proj_fwd",
                     exchange=exchange)


def _mix_fwd(attn, ret, z_a, z_r, x, wba, wbr, wout, g2):
    def body(ri, re, ro, po):
        ba = _dot(ri[0][...], re[0][...])
        br = _dot(ri[1][...], re[1][...])
        m = jax.nn.sigmoid(ri[2][...]) * ba + jax.nn.sigmoid(ri[3][...]) * br
        mb = _bf(m)
        x1 = ri[4][...] + _dot(mb, re[2][...])
        r, xhat = _rms_stats(x1)
        ro[0][...] = ba
        ro[1][...] = br
        ro[2][...] = mb
        ro[3][...] = x1
        ro[4][...] = _bf(xhat * re[3][...])

    outs = [(D_MODEL, F32), (D_MODEL, F32), (D_MODEL, BF16), (D_MODEL, F32), (D_MODEL, BF16)]
    return _row_call(body, tm=512, row_ins=[attn, ret, z_a, z_r, x], res_ins=[wba, wbr, wout, g2], row_outs=outs,
                     name="mix_fwd")


def _ffn_fwd_bwd(h2, x1, target, wg_t, wu_t, wd, g2):
    def body(ri, re, ro, po):
        h2_t = ri[0][...]
        x1_t = ri[1][...]
        gate = _dot_nt(h2_t, re[0][...])
        up = _dot_nt(h2_t, re[1][...])
        sg = jax.nn.sigmoid(gate)
        sl = gate * sg
        actb = _bf(sl * up)
        ro[0][...] = actb
        y = x1_t + _dot(actb, re[2][...])
        e = y - ri[2][...]
        po[0][0] = jnp.broadcast_to(0.5 * jnp.sum(jnp.sum(e * e, axis=1, keepdims=True), axis=0, keepdims=True)
                                    * (1.0 / D_MODEL), (1, 128))
        dy = e * (1.0 / D_MODEL)
        dyb = _bf(dy)
        ro[3][...] = dyb
        dact = _dot_nt(dyb, re[2][...])
        dupb = _bf(dact * sl)
        dgateb = _bf(dact * up * (sg * (1.0 + gate * (1.0 - sg))))
        ro[1][...] = dgateb
        ro[2][...] = dupb
        dh2 = _dot(dgateb, re[0][...]) + _dot(dupb, re[1][...])
        r, xhat = _rms_stats(x1_t)
        dxn, dgain = _rms_bwd(dh2, xhat, r, re[3][...])
        dx1 = dy + dxn
        ro[4][...] = dx1
        ro[5][...] = _bf(dx1)
        po[1][0] = jnp.sum(dgain, axis=0, keepdims=True)

    outs = [(D_FF, BF16), (D_FF, BF16), (D_FF, BF16), (D_MODEL, BF16), (D_MODEL, F32), (D_MODEL, BF16)]
    return _row_call(body, tm=256, row_ins=[h2, x1, target], res_ins=[wg_t, wu_t, wd, g2], row_outs=outs,
                     part_outs=(128, D_MODEL), name="ffn_fwd_bwd")


def _mix_bwd(dx1b, z_a, z_r, ba, br, g_r, o_ret, wout, wba, wbr, exchange):
    def body(ri, re, ro, po):
        dm = _dot_nt(ri[0][...], re[0][...])
        sa = jax.nn.sigmoid(ri[1][...])
        sr = jax.nn.sigmoid(ri[2][...])
        dbab = _bf(sa * dm)
        dbrb = _bf(sr * dm)
        ro[0][...] = dbab
        ro[1][...] = dbrb
        ro[4][:, RET_V:RET_V + D_MODEL] = _bf(dm * ri[3][...] * (sa * (1.0 - sa)))
        ro[4][:, RET_V + D_MODEL:RET_V + 2 * D_MODEL] = _bf(dm * ri[4][...] * (sr * (1.0 - sr)))
        ro[2][...] = _bf(_dot_nt(dbab, re[1][...]))
        dret = _dot_nt(dbrb, re[2][...])
        for h in range(RET_HEADS):
            cols = slice(h * RET_V_DIM, (h + 1) * RET_V_DIM)
            g = ri[5][:, cols]
            r, rn = _rms_stats(ri[6][:, cols])
            sg = jax.nn.sigmoid(g)
            dret_h = dret[:, cols]
            d_rn = dret_h * (g * sg)
            ro[4][:, cols] = _bf(dret_h * rn * (sg * (1.0 + g * (1.0 - sg))))
            ro[3][:, cols] = r * (d_rn - rn * jnp.mean(d_rn * rn, axis=-1, keepdims=True))

    outs = [(D_MODEL, BF16), (D_MODEL, BF16), (ATT_Q, BF16), (RET_V, F32), (RET_V + 2 * D_MODEL, BF16)]
    return _row_call(body, tm=256, row_ins=[dx1b, z_a, z_r, ba, br, g_r, o_ret], res_ins=[wout, wba, wbr],
                     row_outs=outs, name="mix_bwd", exchange=exchange)


def _proj_bwd(d_pieces, x, dx1, w_in_t, g1, exchange):
    widths = [p.shape[1] for p in d_pieces]
    groups = [(sum(widths[:k]), w) for k, w in enumerate(widths)]
    n_p = len(groups)

    def body(ri, re, ro, po):
        dh = None
        for k, (off, w) in enumerate(groups):
            term = _dot(ri[k][...], re[0][off:off + w, :])
            dh = term if dh is None else dh + term
        r, xhat = _rms_stats(ri[n_p][...])
        dxn, dgain = _rms_bwd(dh, xhat, r, re[1][...])
        ro[0][...] = ri[n_p + 1][...] + dxn
        po[0][0] = jnp.sum(dgain, axis=0, keepdims=True)

    return _row_call(body, tm=512, row_ins=[*d_pieces, x, dx1], res_ins=[w_in_t, g1], row_outs=[(D_MODEL, F32)],
                     part_outs=(D_MODEL,), name="proj_bwd", exchange=exchange)


def _dw(a, b, *, tm, place, buf, name, exchange=None):
    t, m = a.shape
    n = b.shape[1]
    tk = min(2048, t)
    n_i, n_k = m // tm, t // tk
    fresh = isinstance(buf, jax.ShapeDtypeStruct)
    n_copies = len(place(0))

    def kern(a_ref, b_ref, *rest):
        out_ref, acc, sems = rest[-3:]
        i, k = pl.program_id(0), pl.program_id(1)
        part = _dot_tn(a_ref[...], b_ref[...])

        @pl.when(k == 0)
        def _():
            acc[i] = part

        @pl.when(k > 0)
        def _():
            acc[i] += part

        def copies(tile):
            return [pltpu.make_async_copy(acc.at[tile, pl.ds(r0, rows), :], out_ref.at[idx], sems.at[tile * n_copies + c])
                    for c, (r0, rows, idx) in enumerate(place(tile))]

        for tile in range(n_i):
            @pl.when((i == tile) & (k == n_k - 1))
            def _(tile=tile):
                for cp in copies(tile):
                    cp.start()

        @pl.when((i == n_i - 1) & (k == n_k - 1))
        def _():
            for tile in range(n_i):
                for cp in copies(tile):
                    cp.wait()

    in_specs = [pl.BlockSpec((tk, tm), lambda i, k: (k, i)), pl.BlockSpec((tk, n), lambda i, k: (k, 0))]
    shape = buf if fresh else jax.ShapeDtypeStruct(buf.shape, buf.dtype)
    return _pallas(
        kern, grid=(n_i, n_k), in_specs=in_specs + ([] if fresh else [_ANY]), out_specs=[_ANY], out_shape=[shape],
        scratch=[pltpu.VMEM((n_i, tm, n), F32), pltpu.SemaphoreType.DMA((n_i * n_copies,))],
        args=[a, b] + ([] if fresh else [buf]), aliases=None if fresh else {2: 0}, name=name, exchange=exchange)


def _heads_to_lanes(x3):
    return jnp.concatenate([x3[g] for g in range(GROUP)], axis=1)


def _lanes_to_heads(xt):
    return jnp.concatenate([xt[:, g * BLOCK:(g + 1) * BLOCK] for g in range(GROUP)], axis=0)


def _attn_queries(kvh, q_ref, gq_col):
    cols = slice(kvh * GROUP * HEAD_DIM, (kvh + 1) * GROUP * HEAD_DIM)
    q3 = q_ref[:, cols].T.reshape(GROUP, HEAD_DIM, BLOCK)
    rq = lax.rsqrt(jnp.mean(q3 * q3, axis=1, keepdims=True) + EPS)
    qhat = q3 * rq
    return qhat, rq, _heads_to_lanes(_bf(qhat * (gq_col * (HEAD_DIM ** -0.5))))


def _from_prev():
    j = lax.broadcasted_iota(jnp.int32, (BLOCK, GROUP * BLOCK), 0)
    i = lax.broadcasted_iota(jnp.int32, (BLOCK, GROUP * BLOCK), 1) & (BLOCK - 1)
    return j > i


def _attn_probs(n, kvh, qts, kvp_ref, kvc_ref, gk, sink_ref):
    kcols = slice(kvh * HEAD_DIM, (kvh + 1) * HEAD_DIM)
    k = jnp.concatenate([kvp_ref[:, kcols], kvc_ref[:, kcols]], axis=0)
    rk, khat = _rms_stats(k)
    st = _dot(_bf(khat * gk), qts)
    f = jnp.where(_from_prev(), jnp.where(n > 0, st[0:BLOCK], -1e30), st[BLOCK:2 * BLOCK])
    sink = jnp.concatenate([jnp.broadcast_to(sink_ref[0:1, kvh * GROUP + g:kvh * GROUP + g + 1], (1, BLOCK))
                            for g in range(GROUP)], axis=1)
    m = jnp.maximum(jnp.max(f, axis=0, keepdims=True), sink)
    e = jnp.exp(f - m)
    es = jnp.exp(sink - m)
    inv = 1.0 / (jnp.sum(e, axis=0, keepdims=True) + es)
    return e * inv, es * inv


def _unfold(from_prev, xf):
    return _bf(jnp.concatenate([jnp.where(from_prev, xf, 0.0), jnp.where(from_prev, 0.0, xf)], axis=0))


def _attn_fwd(q_a, kv_a, gq_col, gk, sinks):
    t = q_a.shape[0]
    nb = t // BLOCK

    def kern(q_ref, kvp_ref, kvc_ref, gq_ref, gk_ref, sink_ref, o_ref, pf_ref, ps_ref):
        n = pl.program_id(0)
        kvt = jnp.concatenate([kvp_ref[...].T, kvc_ref[...].T], axis=1)
        for kvh in range(N_KV_HEADS):
            _, _, qts = _attn_queries(kvh, q_ref, gq_ref[...])
            pf, psink = _attn_probs(n, kvh, qts, kvp_ref, kvc_ref, gk_ref[...], sink_ref)
            lanes = slice(kvh * GROUP * BLOCK, (kvh + 1) * GROUP * BLOCK)
            pf_ref[:, lanes] = pf
            ps_ref[:, lanes] = psink
            vt = _bf(kvt[ATT_KV + kvh * HEAD_DIM:ATT_KV + (kvh + 1) * HEAD_DIM, :])
            out_t = _dot(vt, _unfold(_from_prev(), pf))
            cols = slice(kvh * GROUP * HEAD_DIM, (kvh + 1) * GROUP * HEAD_DIM)
            o_ref[:, cols] = _bf(_lanes_to_heads(out_t).T)

    small = lambda a: pl.BlockSpec(a.shape, lambda n: (0, 0))
    folded = N_KV_HEADS * GROUP * BLOCK
    return dict(
        kern=kern,
        in_specs=[pl.BlockSpec((BLOCK, ATT_Q), lambda n: (n, 0)),
                  pl.BlockSpec((BLOCK, 2 * ATT_KV), lambda n: (jnp.maximum(n - 1, 0), 0)),
                  pl.BlockSpec((BLOCK, 2 * ATT_KV), lambda n: (n, 0)),
                  small(gq_col), small(gk), small(sinks)],
        out_specs=[pl.BlockSpec((BLOCK, ATT_Q), lambda n: (n, 0)), pl.BlockSpec((BLOCK, folded), lambda n: (n, 0)),
                   pl.BlockSpec((None, 1, folded), lambda n: (n, 0, 0))],
        out_shape=[jax.ShapeDtypeStruct((t, ATT_Q), BF16), jax.ShapeDtypeStruct((t, folded), F32),
                   jax.ShapeDtypeStruct((nb, 1, folded), F32)],
        scratch=[], args=[q_a, kv_a, kv_a, gq_col, gk, sinks])


def _attn_bwd(q_a, kv_a, d_attn, probs, sink_probs, gq_col, gk, gk_col):
    t = q_a.shape[0]
    nb = t // BLOCK

    def kern(q_ref, kvp_ref, kvc_ref, do_ref, pf_ref, ps_ref, gq_ref, gk_ref, gkc_ref,
             dq_ref, dkv_ref, dgq_ref, dgk_ref, dsink_ref, band_k, band_v, carry_k, carry_v):
        n = pl.program_id(0)
        gq_v = gq_ref[...]
        gk_v = gk_ref[...]

        @pl.when(n == 0)
        def _():
            carry_k[...] = jnp.zeros_like(carry_k)
            carry_v[...] = jnp.zeros_like(carry_v)
            dgq_ref[...] = jnp.zeros_like(dgq_ref)
            dgk_ref[...] = jnp.zeros_like(dgk_ref)
            dsink_ref[...] = jnp.zeros_like(dsink_ref)

        @pl.when(n == nb)
        def _():
            band_k[...] = jnp.zeros_like(band_k)
            band_v[...] = jnp.zeros_like(band_v)

        @pl.when(n < nb)
        def _():
            lane16 = lax.broadcasted_iota(jnp.int32, (1, N_Q_HEADS), 1)
            dsink = jnp.zeros((1, N_Q_HEADS), F32)
            dgq = jnp.zeros((HEAD_DIM, 1), F32)
            gk_col = gkc_ref[...]
            kvt = jnp.concatenate([kvp_ref[...].T, kvc_ref[...].T], axis=1)
            from_prev = _from_prev()
            for kvh in range(N_KV_HEADS):
                qhat, rq, qts = _attn_queries(kvh, q_ref, gq_v)
                lanes = slice(kvh * GROUP * BLOCK, (kvh + 1) * GROUP * BLOCK)
                pf = pf_ref[:, lanes]
                cols = slice(kvh * GROUP * HEAD_DIM, (kvh + 1) * GROUP * HEAD_DIM)
                vcols = slice(ATT_KV + kvh * HEAD_DIM, ATT_KV + (kvh + 1) * HEAD_DIM)
                dot = _heads_to_lanes(_bf(do_ref[:, cols].astype(F32).T.reshape(GROUP, HEAD_DIM, BLOCK)))
                vb = _bf(jnp.concatenate([kvp_ref[:, vcols], kvc_ref[:, vcols]], axis=0))
                dpt = _dot(vb, dot)
                dpf = jnp.where(from_prev, dpt[0:BLOCK], dpt[BLOCK:2 * BLOCK])
                delta = jnp.sum(pf * dpf, axis=0, keepdims=True)
                dst = _unfold(from_prev, pf * (dpf - delta))
                dsk = ps_ref[:, lanes] * delta
                for g in range(GROUP):
                    tot = jnp.sum(dsk[:, g * BLOCK:(g + 1) * BLOCK], axis=1, keepdims=True)
                    dsink = dsink - jnp.where(lane16 == kvh * GROUP + g, tot, 0.0)
                kt = kvt[kvh * HEAD_DIM:(kvh + 1) * HEAD_DIM, :]
                knt = _bf(kt * lax.rsqrt(jnp.mean(kt * kt, axis=0, keepdims=True) + EPS) * gk_col)
                dqn = (_dot(knt, dst) * (HEAD_DIM ** -0.5))
                band_k[kvh] = _dot_nt(dst, qts)
                band_v[kvh] = _dot_nt(_unfold(from_prev, pf), dot)
                dqn3 = _lanes_to_heads(dqn).reshape(GROUP, HEAD_DIM, BLOCK)
                u = dqn3 * gq_v
                dq3 = rq * (u - qhat * jnp.mean(u * qhat, axis=1, keepdims=True))
                dgq = dgq + jnp.sum(jnp.sum(dqn3 * qhat, axis=0), axis=1, keepdims=True)
                dq_ref[:, cols] = _bf(dq3.reshape(GROUP * HEAD_DIM, BLOCK).T)
            dsink_ref[...] += dsink
            dgq_ref[...] += dgq

        dgk = jnp.zeros((1, HEAD_DIM), F32)
        for kvh in range(N_KV_HEADS):
            kcols = slice(kvh * HEAD_DIM, (kvh + 1) * HEAD_DIM)
            vcols = slice(ATT_KV + kvh * HEAD_DIM, ATT_KV + (kvh + 1) * HEAD_DIM)
            dkn = carry_k[kvh] + band_k[kvh, 0:BLOCK, :]
            dv = carry_v[kvh] + band_v[kvh, 0:BLOCK, :]
            rk, khat = _rms_stats(kvp_ref[:, kcols])
            dk, dgain = _rms_bwd(dkn, khat, rk, gk_v)
            dgk = dgk + jnp.sum(dgain, axis=0, keepdims=True)
            dkv_ref[:, kcols] = _bf(dk)
            dkv_ref[:, vcols] = _bf(dv)
            carry_k[kvh] = band_k[kvh, BLOCK:2 * BLOCK, :]
            carry_v[kvh] = band_v[kvh, BLOCK:2 * BLOCK, :]
        dgk_ref[...] += dgk

    small = lambda a: pl.BlockSpec(a.shape, lambda n: (0, 0))
    last = nb - 1
    return dict(
        kern=kern,
        in_specs=[pl.BlockSpec((BLOCK, ATT_Q), lambda n: (jnp.minimum(n, last), 0)),
                  pl.BlockSpec((BLOCK, 2 * ATT_KV), lambda n: (jnp.maximum(n - 1, 0), 0)),
                  pl.BlockSpec((BLOCK, 2 * ATT_KV), lambda n: (jnp.minimum(n, last), 0)),
                  pl.BlockSpec((BLOCK, ATT_Q), lambda n: (jnp.minimum(n, last), 0)),
                  pl.BlockSpec((BLOCK, probs.shape[1]), lambda n: (jnp.minimum(n, last), 0)),
                  pl.BlockSpec((None, 1, probs.shape[1]), lambda n: (jnp.minimum(n, last), 0, 0)),
                  small(gq_col), small(gk), small(gk_col)],
        out_specs=[pl.BlockSpec((BLOCK, ATT_Q), lambda n: (jnp.minimum(n, last), 0)),
                   pl.BlockSpec((BLOCK, 2 * ATT_KV), lambda n: (jnp.maximum(n - 1, 0), 0)),
                   pl.BlockSpec((HEAD_DIM, 1), lambda n: (0, 0)),
                   pl.BlockSpec((1, HEAD_DIM), lambda n: (0, 0)),
                   pl.BlockSpec((1, N_Q_HEADS), lambda n: (0, 0))],
        out_shape=[jax.ShapeDtypeStruct((t, ATT_Q), BF16), jax.ShapeDtypeStruct((t, 2 * ATT_KV), BF16),
                   jax.ShapeDtypeStruct((HEAD_DIM, 1), F32), jax.ShapeDtypeStruct((1, HEAD_DIM), F32),
                   jax.ShapeDtypeStruct((1, N_Q_HEADS), F32)],
        scratch=[pltpu.VMEM((N_KV_HEADS, 2 * BLOCK, HEAD_DIM), F32),
                 pltpu.VMEM((N_KV_HEADS, 2 * BLOCK, HEAD_DIM), F32),
                 pltpu.VMEM((N_KV_HEADS, BLOCK, HEAD_DIM), F32),
                 pltpu.VMEM((N_KV_HEADS, BLOCK, HEAD_DIM), F32)],
        args=[q_a, kv_a, kv_a, d_attn, probs, sink_probs, gq_col, gk, gk_col])


def _ret_tables(t, exchange):
    theta = 1.0 / (RET_ROT_BASE ** jnp.linspace(0.0, 1.0, RET_QK_DIM // 2, dtype=F32))
    theta2 = jnp.repeat(theta, 2)[None, :]
    sign = jnp.tile(jnp.array([-1.0, 1.0], F32), RET_QK_DIM // 2)[None, :]

    def kern(theta_ref, sign_ref, cos_ref, sin_ref):
        first = pl.program_id(0) * RET_CHUNK
        pos = (first + lax.broadcasted_iota(jnp.int32, (RET_CHUNK, RET_QK_DIM), 0)).astype(F32)
        ang = pos * theta_ref[...]
        cos_ref[...] = jnp.cos(ang)
        sin_ref[...] = jnp.sin(ang) * sign_ref[...]

    row = pl.BlockSpec((1, RET_QK_DIM), lambda n: (0, 0))
    blk = pl.BlockSpec((RET_CHUNK, RET_QK_DIM), lambda n: (n, 0))
    cos, sin_s, *got = _pallas(kern, grid=(t // RET_CHUNK,), in_specs=[row, row], out_specs=[blk, blk],
                               out_shape=[jax.ShapeDtypeStruct((t, RET_QK_DIM), F32)] * 2, args=[theta2, sign],
                               name="position_tables", exchange=exchange)
    log_gamma = jnp.log(1.0 - 2.0 ** (-5.0 - jnp.arange(RET_HEADS, dtype=F32)))
    i = jnp.arange(RET_CHUNK, dtype=F32)
    diff = i[:, None] - i[None, :]
    causal = diff >= 0
    decay = jnp.where(causal[None], jnp.exp(jnp.where(causal, diff, 0.0)[None] * log_gamma[:, None, None]), 0.0)
    xi = jnp.exp((i + 1.0)[None, :] * log_gamma[:, None])[:, :, None]
    zeta = jnp.exp((RET_CHUNK - 1.0 - i)[None, :] * log_gamma[:, None])[:, :, None]
    gch = jnp.broadcast_to(jnp.exp(RET_CHUNK * log_gamma)[:, None, None], (RET_HEADS, 1, 128))
    return (cos, sin_s, decay, xi, zeta, gch), got


def _swap_pairs(x):
    lane = lax.broadcasted_iota(jnp.int32, x.shape, 1)
    return jnp.where((lane & 1) == 0, pltpu.roll(x, RET_QK_DIM - 1, 1), pltpu.roll(x, 1, 1))


def _rotate(x, cos, sin_s):
    return x * cos + _swap_pairs(x) * sin_s


def _rotate_bwd(dy, cos, sin_s):
    return dy * cos + _swap_pairs(dy * sin_s)


def _ret_specs(order):
    qk = pl.BlockSpec((RET_CHUNK, RET_QK), lambda j: (order(j), 0))
    v = pl.BlockSpec((RET_CHUNK, RET_V), lambda j: (order(j), 0))
    dec = pl.BlockSpec((RET_HEADS, RET_CHUNK, RET_CHUNK), lambda j: (0, 0, 0))
    col = pl.BlockSpec((RET_HEADS, RET_CHUNK, 1), lambda j: (0, 0, 0))
    gch = pl.BlockSpec((RET_HEADS, 1, 128), lambda j: (0, 0, 0))
    st = pl.BlockSpec((RET_HEADS, None, RET_QK_DIM, RET_V_DIM), lambda j: (0, order(j), 0, 0))
    pos = pl.BlockSpec((RET_CHUNK, RET_QK_DIM), lambda j: (order(j), 0))
    return qk, v, dec, col, gch, st, pos


def _ret_fwd(q_r, k_r, v_r, g_r, tables):
    t = q_r.shape[0]
    nc = t // RET_CHUNK
    cos, sin_s, decay, xi, zeta, gch = tables

    def kern(q_ref, k_ref, v_ref, g_ref, cos_ref, sin_ref, dec_ref, xi_ref, zeta_ref, gch_ref,
             o_ref, ret_ref, st_ref, state):
        @pl.when(pl.program_id(0) == 0)
        def _():
            state[...] = jnp.zeros_like(state)

        cos_t = cos_ref[...]
        sin_t = sin_ref[...]
        for h in range(RET_HEADS):
            qc = slice(h * RET_QK_DIM, (h + 1) * RET_QK_DIM)
            vc = slice(h * RET_V_DIM, (h + 1) * RET_V_DIM)
            qs = _bf(_rotate(q_ref[:, qc], cos_t, sin_t))
            ks = _rotate(k_ref[:, qc] * (RET_QK_DIM ** -0.5), cos_t, sin_t)
            vb = v_ref[:, vc]
            s_old = state[h]
            sb = _bf(s_old)
            st_ref[h] = sb
            inner = _dot_nt(qs, _bf(ks)) * dec_ref[h]
            out = _dot(_bf(inner), vb) + _dot(qs, sb) * xi_ref[h]
            state[h] = gch_ref[h, :, 0:1] * s_old + _dot_tn(_bf(ks * zeta_ref[h]), vb)
            o_ref[:, vc] = out
            r, rn = _rms_stats(out)
            g = g_ref[:, vc]
            ret_ref[:, vc] = _bf(g * jax.nn.sigmoid(g) * rn)

    qk, v, dec, col, gsp, st, pos = _ret_specs(lambda j: j)
    return dict(
        kern=kern,
        in_specs=[qk, qk, v, v, pos, pos, dec, col, col, gsp],
        out_specs=[v, v, st],
        out_shape=[jax.ShapeDtypeStruct((t, RET_V), F32), jax.ShapeDtypeStruct((t, RET_V), BF16),
                   jax.ShapeDtypeStruct((RET_HEADS, nc, RET_QK_DIM, RET_V_DIM), BF16)],
        scratch=[pltpu.VMEM((RET_HEADS, RET_QK_DIM, RET_V_DIM), F32)],
        args=[q_r, k_r, v_r, g_r, cos, sin_s, decay, xi, zeta, gch])


def _ret_bwd(q_r, k_r, v_r, d_o, states, tables):
    t = q_r.shape[0]
    nc = t // RET_CHUNK
    cos, sin_s, decay, xi, zeta, gch = tables

    def kern(q_ref, k_ref, v_ref, do_ref, st_ref, cos_ref, sin_ref, dec_ref, xi_ref, zeta_ref, gch_ref,
             d_ref, dstate):
        dq_ref, dk_ref = d_ref.at[:, 0:RET_QK], d_ref.at[:, RET_QK:2 * RET_QK]
        dv_ref = d_ref.at[:, 2 * RET_QK:2 * RET_QK + RET_V]

        @pl.when(pl.program_id(0) == 0)
        def _():
            dstate[...] = jnp.zeros_like(dstate)

        @pl.when(pl.program_id(0) < nc)
        def _():
            cos_t = cos_ref[...]
            sin_t = sin_ref[...]
            scale = RET_QK_DIM ** -0.5
            for h in range(RET_HEADS):
                qc = slice(h * RET_QK_DIM, (h + 1) * RET_QK_DIM)
                vc = slice(h * RET_V_DIM, (h + 1) * RET_V_DIM)
                qs = _bf(_rotate(q_ref[:, qc], cos_t, sin_t))
                ks = _rotate(k_ref[:, qc] * scale, cos_t, sin_t)
                ksb = _bf(ks)
                vb = v_ref[:, vc]
                d_o_t = do_ref[:, vc]
                dob = _bf(d_o_t)
                doxb = _bf(d_o_t * xi_ref[h])
                dec = dec_ref[h]
                ds_old = dstate[h]
                dsb = _bf(ds_old)
                pb = _bf(_dot_nt(qs, ksb) * dec)
                dpb = _bf(_dot_nt(dob, vb) * dec)
                dqs = _dot(dpb, ksb) + _dot_nt(doxb, st_ref[h])
                dks = _dot_tn(dpb, qs) + _dot_nt(vb, dsb) * zeta_ref[h]
                dv_ref[:, vc] = _bf(_dot_tn(pb, dob) + _dot(_bf(ks * zeta_ref[h]), dsb))
                dstate[h] = gch_ref[h, :, 0:1] * ds_old + _dot_tn(qs, doxb)
                dq_ref[:, qc] = _bf(_rotate_bwd(dqs, cos_t, sin_t))
                dk_ref[:, qc] = _bf(_rotate_bwd(dks, cos_t, sin_t) * scale)

    backwards = lambda j: jnp.maximum(nc - 1 - j, 0)
    qk, v, dec, col, gsp, st, pos = _ret_specs(backwards)
    return dict(
        kern=kern,
        in_specs=[qk, qk, v, v, st, pos, pos, dec, col, col, gsp],
        out_specs=[pl.BlockSpec((RET_CHUNK, 2 * RET_QK + RET_V), lambda j: (backwards(j), 0))],
        out_shape=[jax.ShapeDtypeStruct((t, 2 * RET_QK + RET_V), BF16)],
        scratch=[pltpu.VMEM((RET_HEADS, RET_QK_DIM, RET_V_DIM), F32)],
        args=[q_r, k_r, v_r, d_o, states, cos, sin_s, decay, xi, zeta, gch])


def _position():
    return lax.axis_index("x"), lax.axis_index("y"), lax.axis_index("c")


def _gather_exchange(owns, forward_at):
    n = len(owns)

    def copies(ins, outs, send_sems, recv_sems, base):
        x, y, c = _position()
        sibling = (x, y, 1 - c)
        chips = [(1 - x, y), (x, 1 - y), (1 - x, 1 - y)]
        my_chip = 2 * x + y

        def slab(a, chip, hf):
            half = owns[a].shape[0] // 2
            return outs[a].at[chip, pl.ds(hf * half, half), :]

        def copy(k, src, dst, to):
            return pltpu.make_async_remote_copy(src_ref=src, dst_ref=dst, send_sem=send_sems.at[base + k],
                                                recv_sem=recv_sems.at[base + k], device_id=to, device_id_type=MESH)

        first, passed, from_sibling = [], [], []
        for a in range(n):
            half = owns[a].shape[0] // 2
            for k, (cx, cy) in enumerate(chips):
                first.append(copy(6 * a + k, ins[a].at[pl.ds(c * half, half), :], slab(a, my_chip, c), (cx, cy, c)))
                landed = slab(a, 2 * cx + cy, c)
                passed.append(copy(6 * a + 3 + k, landed, landed, sibling))
                theirs = slab(a, 2 * cx + cy, 1 - c)
                from_sibling.append(copy(6 * a + 3 + k, theirs, theirs, sibling))
        return first, passed, from_sibling

    def start(*args):
        first, _, _ = copies(*args)
        for cp in first:
            cp.start()

    def forward(*args):
        first, passed, _ = copies(*args)
        for arrived, cp in zip(first, passed):
            arrived.wait_recv()
            cp.start()

    def finish(*args):
        first, passed, from_sibling = copies(*args)
        for cp in from_sibling:
            cp.wait_recv()
        for cp in first + passed:
            cp.wait_send()

    outs = [jax.ShapeDtypeStruct((N_CHIPS, *a.shape), a.dtype) for a in owns]
    return _Exchange(owns, outs, 6 * n, [(0.0, start), (forward_at, forward), (1.0, finish)])


def _symmetric_exchange(ins, outs, plan):
    n_sems = len(plan([None] * len(ins), [None] * len(outs), 0, 0, 0, dry=True))

    def copies(in_refs, out_refs, send_sems, recv_sems, base):
        x, y, c = _position()
        return [pltpu.make_async_remote_copy(src_ref=src, dst_ref=dst, send_sem=send_sems.at[base + k],
                                             recv_sem=recv_sems.at[base + k], device_id=dev, device_id_type=MESH)
                for k, (src, dst, dev) in enumerate(plan(in_refs, out_refs, x, y, c, dry=False))]

    def start(*args):
        for cp in copies(*args):
            cp.start()

    def finish(*args):
        for cp in copies(*args):
            cp.wait()

    return _Exchange(ins, outs, n_sems, [(0.0, start), (1.0, finish)])


def _pair_exchange(gs):
    def plan(in_refs, out_refs, x, y, c, dry):
        out = []
        for a, g in enumerate(gs):
            half = g.shape[1] // 2
            for k in range(N_CHIPS):
                out.append(None if dry else (in_refs[a].at[k, pl.ds((1 - c) * half, half), :], out_refs[a].at[k],
                                             (x, y, 1 - c)))
        return out

    outs = [jax.ShapeDtypeStruct((g.shape[0], g.shape[1] // 2, g.shape[2]), g.dtype) for g in gs]
    return _symmetric_exchange(gs, outs, plan)


def _pair_sum(g, from_sibling, c_arr, *, tile, name):
    n, rows, width = g.shape
    tiles = (rows // 2) // tile

    def kern(c_ref, g_ref, s_ref, o_ref):
        o_ref[...] = _bf(g_ref[...] + s_ref[...])

    return pl.pallas_call(
        kern,
        grid_spec=pltpu.PrefetchScalarGridSpec(
            num_scalar_prefetch=1, grid=(n, tiles),
            in_specs=[pl.BlockSpec((None, tile, width), lambda k, i, c: (k, c[0] * tiles + i, 0)),
                      pl.BlockSpec((None, tile, width), lambda k, i, c: (k, i, 0))],
            out_specs=pl.BlockSpec((None, tile, width), lambda k, i, c: (k, i, 0))),
        out_shape=jax.ShapeDtypeStruct((n, rows // 2, width), BF16), name=name,
        compiler_params=_params(("parallel", "parallel")),
    )(c_arr, g, from_sibling)


def _scatter_to_owners(hsums):
    def plan(in_refs, out_refs, x, y, c, dry):
        out = []
        for a in range(len(hsums)):
            for k, (cx, cy) in enumerate([(1 - x, y), (x, 1 - y), (1 - x, 1 - y)]):
                out.append(None if dry else (in_refs[a].at[2 * cx + cy], out_refs[a].at[k], (cx, cy, c)))
        return out

    outs = [jax.ShapeDtypeStruct((3, *h.shape[1:]), h.dtype) for h in hsums]
    return _symmetric_exchange(hsums, outs, plan)


def _sum_chips(hsum, parts, chip_arr, *, tile, name):
    n, half, width = parts.shape

    def kern(chip_ref, h_ref, p_ref, o_ref):
        acc = h_ref[...].astype(F32)
        for k in range(n):
            acc = acc + p_ref[k].astype(F32)
        o_ref[...] = acc

    return pl.pallas_call(
        kern,
        grid_spec=pltpu.PrefetchScalarGridSpec(
            num_scalar_prefetch=1, grid=(half // tile,),
            in_specs=[pl.BlockSpec((None, tile, width), lambda i, chip: (chip[0], i, 0)),
                      pl.BlockSpec((n, tile, width), lambda i, chip: (0, i, 0))],
            out_specs=pl.BlockSpec((tile, width), lambda i, chip: (i, 0))),
        out_shape=jax.ShapeDtypeStruct((half, width), F32), name=name,
        compiler_params=_params(("parallel",)),
    )(chip_arr, hsum, parts)


def _share_halves(fhalves):
    def plan(in_refs, out_refs, x, y, c, dry):
        return [None if dry else (in_refs[a], out_refs[a], (x, y, 1 - c)) for a in range(len(fhalves))]

    return _symmetric_exchange(fhalves, [jax.ShapeDtypeStruct(f.shape, f.dtype) for f in fhalves], plan)


def _adamw_math(w, g, m, v):
    m = ADAM_B1 * m + (1.0 - ADAM_B1) * g
    v = ADAM_B2 * v + (1.0 - ADAM_B2) * (g * g)
    m_hat = m / (1.0 - ADAM_B1 ** ADAM_STEP)
    v_hat = v / (1.0 - ADAM_B2 ** ADAM_STEP)
    delta = -ADAM_LR * (m_hat / (jnp.sqrt(v_hat) + ADAM_EPS) + ADAM_WD * w)
    return delta, m, v


def _adamw(mats, g_mine, g_other, c_arr, *, tile, name):
    width = g_mine.shape[1]
    tiles_per_half = g_mine.shape[0] // tile
    n_tiles = [w.shape[0] // tile for w, _, _, _ in mats]
    n_mats = len(mats)

    def kern(c_ref, *refs):
        ins, outs = refs[:5 * n_mats], refs[5 * n_mats:]
        for j, (_, _, _, row_off) in enumerate(mats):
            w_ref, gm_ref, go_ref, m_ref, v_ref = ins[5 * j:5 * j + 5]
            i = jnp.minimum(pl.program_id(0), n_tiles[j] - 1)
            in_my_half = ((row_off // tile + i) // tiles_per_half) == c_ref[0]
            g = jnp.where(in_my_half, gm_ref[...], go_ref[...])
            d, nm, nv = _adamw_math(w_ref[...], g, m_ref[...], v_ref[...])
            for out_ref, val in zip(outs[4 * j:4 * j + 4], (g, d, nm, nv)):
                out_ref[...] = val

    in_specs, out_specs, out_shape, args = [], [], [], []
    for (w, m, v, row_off), nt in zip(mats, n_tiles):
        full = pl.BlockSpec((tile, width), lambda i, c, nt=nt: (jnp.minimum(i, nt - 1), 0))

        def half(mine, nt=nt, first=row_off // tile):
            def index(i, c):
                pos = first + jnp.minimum(i, nt - 1)
                used = ((pos // tiles_per_half) == c[0]) == mine
                return (jnp.where(used, pos % tiles_per_half, 0), 0)
            return pl.BlockSpec((tile, width), index)

        in_specs += [full, half(True), half(False), full, full]
        out_specs += [full] * 4
        out_shape += [jax.ShapeDtypeStruct(w.shape, F32)] * 4
        args += [w, g_mine, g_other, m, v]
    outs = pl.pallas_call(
        kern,
        grid_spec=pltpu.PrefetchScalarGridSpec(num_scalar_prefetch=1, grid=(max(n_tiles),), in_specs=in_specs,
                                               out_specs=out_specs),
        out_shape=out_shape, name=name, compiler_params=_params(("arbitrary",)),
    )(c_arr, *args)
    return [outs[4 * j:4 * j + 4] for j in range(n_mats)]


def _small_step(partials, params):
    slots = ((0, 0, D_MODEL), (1, 0, D_MODEL), (2, 0, HEAD_DIM), (2, 128, HEAD_DIM), (2, 256, N_Q_HEADS))
    loss_slot = (2, 384, 128)

    def body(*refs):
        loss_ref, dg1_ref, dg2_ref, dgq_ref, dgk_ref, dsink_ref = refs[:6]
        p_refs, out_refs = refs[6:21], refs[21:42]
        mine, gathered, send_sems, recv_sems = refs[42:]
        x, y, c = _position()
        me = 4 * x + 2 * y + c
        mine[...] = jnp.zeros_like(mine)
        for (row, lane, n), val in zip(slots + (loss_slot,), (
                jnp.sum(dg1_ref[...], axis=0, keepdims=True), jnp.sum(dg2_ref[...], axis=0, keepdims=True),
                dgq_ref[...], dgk_ref[...], dsink_ref[...], jnp.sum(loss_ref[...], axis=0, keepdims=True))):
            mine[row:row + 1, lane:lane + n] = val
        copies = []
        for k in range(1, N_DEV):
            flip = (k >> 2) & 1, (k >> 1) & 1, k & 1
            to = (x ^ flip[0], y ^ flip[1], c ^ flip[2])
            cp = pltpu.make_async_remote_copy(
                src_ref=mine, dst_ref=gathered.at[me], send_sem=send_sems.at[k - 1], recv_sem=recv_sems.at[k - 1],
                device_id=to, device_id_type=MESH)
            cp.start()
            copies.append(cp)
        gathered[me] = mine[...]
        for k in range(1, N_DEV):
            flip = (k >> 2) & 1, (k >> 1) & 1, k & 1
            src = 4 * (x ^ flip[0]) + 2 * (y ^ flip[1]) + (c ^ flip[2])
            pltpu.make_async_remote_copy(
                src_ref=mine, dst_ref=gathered.at[src], send_sem=send_sems.at[k - 1], recv_sem=recv_sems.at[k - 1],
                device_id=(x, y, c), device_id_type=MESH).wait_recv()
        for cp in copies:
            cp.wait_send()
        total = gathered[0]
        for k in range(1, N_DEV):
            total = total + gathered[k]
        row, lane, n = loss_slot
        out_refs[0][...] = total[row:row + 1, lane:lane + n]
        for i, (row, lane, n) in enumerate(slots):
            g = total[row:row + 1, lane:lane + n]
            d, nm, nv = _adamw_math(p_refs[i][...], g, p_refs[5 + i][...], p_refs[10 + i][...])
            for kind, val in enumerate((g, d, nm, nv)):
                out_refs[1 + 5 * kind + i][...] = val

    vm = pl.BlockSpec(memory_space=pltpu.VMEM)
    shapes = [jax.ShapeDtypeStruct((1, 128), F32)] + [jax.ShapeDtypeStruct((1, n), F32) for _, _, n in slots] * 4
    return pl.pallas_call(
        body, in_specs=[vm] * 21, out_specs=[vm] * 21, out_shape=shapes,
        scratch_shapes=[pltpu.VMEM((SMALL_ROWS, D_MODEL), F32), pltpu.VMEM((N_DEV, SMALL_ROWS, D_MODEL), F32),
                        pltpu.SemaphoreType.DMA((N_DEV - 1,)), pltpu.SemaphoreType.DMA((N_DEV - 1,))],
        name="small_step",
    )(*partials, *params)


def _with_own(gathered, own, my_chip):
    return lax.dynamic_update_slice(gathered, own[None], (my_chip, 0, 0))


def kernel(x, norm_mix_gain, w_in, q_norm_gain, k_norm_gain, attn_sinks, w_branch_attn, w_branch_ret, w_out, norm_ffn_gain, w_ffn_gate, w_ffn_up, w_ffn_down, loss_target, m_norm_mix_gain, m_w_in, m_q_norm_gain, m_k_norm_gain, m_attn_sinks, m_w_branch_attn, m_w_branch_ret, m_w_out, m_norm_ffn_gain, m_w_ffn_gate, m_w_ffn_up, m_w_ffn_down, v_norm_mix_gain, v_w_in, v_q_norm_gain, v_k_norm_gain, v_attn_sinks, v_w_branch_attn, v_w_branch_ret, v_w_out, v_norm_ffn_gain, v_w_ffn_gate, v_w_ffn_up, v_w_ffn_down):
    my_chip = 2 * lax.axis_index("x") + lax.axis_index("y")
    c_arr = lax.axis_index("c").astype(jnp.int32).reshape(1)
    chip_arr = my_chip.astype(jnp.int32).reshape(1)
    x_t, target = x[0], loss_target[0]
    g1, g2, gq, gk, sinks = norm_mix_gain, norm_ffn_gain, q_norm_gain, k_norm_gain, attn_sinks

    tr = lambda a: jnp.transpose(a[0])
    own_w_in = _bf(tr(w_in))
    own_rest = [_bf(a) for a in (tr(w_ffn_gate), tr(w_ffn_up), w_ffn_down[0], w_branch_attn[0], w_branch_ret[0],
                                 w_out[0])]
    tables, (got_w_in,) = _ret_tables(x_t.shape[0], _gather_exchange([own_w_in], 0.9))
    w_in_t = _with_own(got_w_in, own_w_in, my_chip).reshape(D_IN, D_MODEL)
    h1, q_a, kv_a, q_r, k_r, v_r, g_r, z_a, z_r, *got_rest = _proj_fwd(x_t, g1, w_in_t, _gather_exchange(own_rest, 0.8))
    wg_t, wu_t, wd, wba, wbr, wout = [_with_own(got, own, my_chip).reshape(N_CHIPS * own.shape[0], D_MODEL)
                                      for got, own in zip(got_rest, own_rest)]

    gq_col, gk_col = gq.reshape(HEAD_DIM, 1), gk.reshape(HEAD_DIM, 1)
    attn, probs, sink_probs, o_ret, ret, states = _fused(
        [_attn_fwd(q_a, kv_a, gq_col, gk, sinks), _ret_fwd(q_r, k_r, v_r, g_r, tables)],
        grid=(x_t.shape[0] // BLOCK,), name="mixers_fwd")
    ba, br, merged, x1, h2 = _mix_fwd(attn, ret, z_a, z_r, x_t, wba, wbr, wout, g2)
    act, dgate, dup, dyb, dx1, dx1b, loss_p, dg2_p = _ffn_fwd_bwd(h2, x1, target, wg_t, wu_t, wd, g2)

    def pairs(row0, rows):
        return lambda i: [(h * rows, rows, (2 * i + h, pl.ds(row0, rows), slice(None))) for h in range(2)]

    f_block = jax.ShapeDtypeStruct((N_CHIPS, 3 * FF_SH, D_MODEL), F32)
    f_block, = _dw(dgate, h2, tm=2 * FF_SH, place=pairs(0, FF_SH), buf=f_block, name="dw_gate")
    f_block, = _dw(dup, h2, tm=2 * FF_SH, place=pairs(FF_SH, FF_SH), buf=f_block, name="dw_up")
    f_block, = _dw(act, dyb, tm=2 * FF_SH, place=pairs(2 * FF_SH, FF_SH), buf=f_block, name="dw_down")
    (dba, dbr, d_attn, d_o, d_gz, sib_ffn) = _mix_bwd(
        dx1b, z_a, z_r, ba, br, g_r, o_ret, wout, wba, wbr, _pair_exchange([f_block]))
    f_sum = _pair_sum(f_block, sib_ffn, c_arr, tile=528, name="pair_sum_ffn")

    def quarters(row0, rows):
        return lambda i: [(k * rows, rows, (k, pl.ds(row0, rows), slice(None))) for k in range(N_CHIPS)]

    m_block = jax.ShapeDtypeStruct((N_CHIPS, D_MODEL, D_MODEL), F32)
    m_block, = _dw(attn, dba, tm=ATT_Q, place=quarters(0, 256), buf=m_block, name="dw_ba")
    m_block, = _dw(ret, dbr, tm=D_MODEL, place=pairs(256, 512), buf=m_block, name="dw_br")
    m_block, = _dw(merged, dx1b, tm=D_MODEL, place=quarters(768, 256), buf=m_block, name="dw_out")

    def w_in_rows(off, w):
        tm = min(w, D_MODEL)
        return dict(tm=tm, place=lambda i: [(0, tm, (pl.ds(off + i * tm, tm), slice(None)))])

    w_block = jax.ShapeDtypeStruct((D_IN, D_MODEL), F32)
    w_block, sib_mix = _dw(d_gz, h1, buf=w_block, name="dw_in_gz", exchange=_pair_exchange([m_block]),
                           **w_in_rows(P_GR[0], d_gz.shape[1]))
    m_sum = _pair_sum(m_block, sib_mix, c_arr, tile=256, name="pair_sum_mix")

    (dq_a, dkv_a, dgq, dgk, dsinks, d_ret, got_ffn_sums, got_mix_sums) = _fused(
        [_attn_bwd(q_a, kv_a, d_attn, probs, sink_probs, gq_col, gk, gk_col),
         _ret_bwd(q_r, k_r, v_r, d_o, states, tables)],
        grid=(x_t.shape[0] // BLOCK + 1,), name="mixers_bwd", exchange=_scatter_to_owners([f_sum, m_sum]))
    dgq = dgq.reshape(1, HEAD_DIM)
    ffn_half = _sum_chips(f_sum, got_ffn_sums, chip_arr, tile=528, name="sum_chips_ffn")
    mix_half = _sum_chips(m_sum, got_mix_sums, chip_arr, tile=256, name="sum_chips_mix")
    w_block, = _dw(d_ret, h1, buf=w_block, name="dw_in_ret", **w_in_rows(P_QR[0], d_ret.shape[1]))
    w_block, ffn_other, mix_other = _dw(dq_a, h1, buf=w_block, name="dw_in_q",
                                        exchange=_share_halves([ffn_half, mix_half]), **w_in_rows(*P_QA))
    w_block, = _dw(dkv_a, h1, buf=w_block, name="dw_in_kv", **w_in_rows(*P_KVA))

    w_block = w_block.reshape(N_CHIPS, W_IN_SH, D_MODEL)
    sib_w_in, = _run_exchange(_pair_exchange([w_block]), "pair_exchange_w_in")
    w_sum = _pair_sum(w_block, sib_w_in, c_arr, tile=592, name="pair_sum_w_in")
    d_pieces = [dq_a, dkv_a, d_ret, d_gz]
    grad_x, dg1_p, got_w_in_sums = _proj_bwd(d_pieces, x_t, dx1, w_in_t, g1, _scatter_to_owners([w_sum]))
    w_in_half = _sum_chips(w_sum, got_w_in_sums, chip_arr, tile=592, name="sum_chips_w_in")
    w_in_other, = _run_exchange(_share_halves([w_in_half]), "share_halves_w_in")

    def update(name, g_half, g_other, tile, mats):
        outs = _adamw([tuple(tr(a) if t else a[0] for a in wmv) + (off,) for _, *wmv, off, t in mats],
                      g_half, g_other, c_arr, tile=tile, name=f"adamw_{name}")
        return {key: [jnp.transpose(o) if t else o for o in res] for (key, _, _, _, _, t), res in zip(mats, outs)}

    big = {
        **update("w_in", w_in_half, w_in_other, 592, [("w_in", w_in, m_w_in, v_w_in, 0, True)]),
        **update("ffn", ffn_half, ffn_other, 176, [
            ("wg", w_ffn_gate, m_w_ffn_gate, v_w_ffn_gate, 0, True),
            ("wu", w_ffn_up, m_w_ffn_up, v_w_ffn_up, FF_SH, True),
            ("wd", w_ffn_down, m_w_ffn_down, v_w_ffn_down, 2 * FF_SH, False)]),
        **update("mix", mix_half, mix_other, 128, [
            ("wba", w_branch_attn, m_w_branch_attn, v_w_branch_attn, 0, False),
            ("wbr", w_branch_ret, m_w_branch_ret, v_w_branch_ret, 256, False),
            ("wout", w_out, m_w_out, v_w_out, 768, False)])}

    loss_row, *small = _small_step(
        [loss_p.reshape(-1, 128), dg1_p.reshape(-1, D_MODEL), dg2_p.reshape(-1, D_MODEL), dgq, dgk, dsinks],
        [norm_mix_gain, norm_ffn_gain, q_norm_gain, k_norm_gain, attn_sinks,
         m_norm_mix_gain, m_norm_ffn_gain, m_q_norm_gain, m_k_norm_gain, m_attn_sinks,
         v_norm_mix_gain, v_norm_ffn_gain, v_q_norm_gain, v_k_norm_gain, v_attn_sinks])
    loss = loss_row[0, 0]

    def leaves(i):
        b = [big[n][i][None] for n in ("w_in", "wba", "wbr", "wout", "wg", "wu", "wd")]
        s1, s2, sq, sk, ss = small[5 * i:5 * i + 5]
        return [s1, b[0], sq, sk, ss, b[1], b[2], b[3], s2, b[4], b[5], b[6]]

    return (loss, grad_x[None], *leaves(0), *leaves(1), *leaves(2), *leaves(3))
```

```python
import jax
import jax.numpy as jnp
from jax import lax
from jax.experimental import pallas as pl
from jax.experimental.pallas import tpu as pltpu

F32 = jnp.float32
BF16 = jnp.bfloat16
MESH = pl.DeviceIdType.MESH

D_MODEL = 1024
EPS = 1e-6
HEAD_DIM = 64
N_Q_HEADS = 16
N_KV_HEADS = 2
GROUP = 8
BLOCK = 128
RET_HEADS = 4
RET_QK_DIM = 256
RET_V_DIM = 512
RET_CHUNK = 128
RET_ROT_BASE = 10000.0
D_FF = 2816
ATT_Q = N_Q_HEADS * HEAD_DIM
ATT_KV = N_KV_HEADS * HEAD_DIM
RET_QK = RET_HEADS * RET_QK_DIM
RET_V = RET_HEADS * RET_V_DIM
D_IN = 9472
ADAM_LR = 0.001
ADAM_B1 = 0.9
ADAM_B2 = 0.999
ADAM_EPS = 1e-08
ADAM_WD = 0.01
ADAM_STEP = 10

N_CHIPS = 4
N_DEV = 8
VMEM_LIMIT_BYTES = 60 * 1024 * 1024

P_QA = (0, 1024)
P_KVA = (1024, 256)
P_QR = (1280, 1024)
P_KR = (2304, 1024)
P_VR = (3328, 2048)
P_GR = (5376, 2048)
P_ZA = (7424, 1024)
P_ZR = (8448, 1024)

W_IN_SH = D_IN // N_CHIPS
FF_SH = D_FF // N_CHIPS

SMALL_ROWS = 8


def _dot(a, b):
    return jnp.dot(a, b, preferred_element_type=F32)


def _dot_nt(a, b):
    return lax.dot_general(a, b, (((1,), (1,)), ((), ())), preferred_element_type=F32)


def _dot_tn(a, b):
    return lax.dot_general(a, b, (((0,), (0,)), ((), ())), preferred_element_type=F32)


def _bf(x):
    return x.astype(BF16)


def _rms_stats(x):
    r = lax.rsqrt(jnp.mean(x * x, axis=-1, keepdims=True) + EPS)
    return r, x * r


def _rms_bwd(dy, xhat, r, gain):
    u = dy * gain
    dx = r * (u - xhat * jnp.mean(u * xhat, axis=-1, keepdims=True))
    return dx, dy * xhat


def _params(sem):
    return pltpu.CompilerParams(dimension_semantics=sem, vmem_limit_bytes=VMEM_LIMIT_BYTES)


_ANY = pl.BlockSpec(memory_space=pl.ANY)


class _Exchange:
    def __init__(self, ins, outs, n_sems, phases, staging=()):
        self.ins, self.outs, self.n_sems, self.phases = list(ins), list(outs), n_sems, list(phases)
        self.staging = list(staging)


def _pallas(kern, *, grid, in_specs, out_specs, out_shape, args, name, scratch=(), exchange=None, aliases=None):
    aliases = aliases or {}
    if exchange is None:
        return pl.pallas_call(
            kern, grid=grid, in_specs=in_specs, out_specs=out_specs, out_shape=out_shape, name=name,
            scratch_shapes=list(scratch), input_output_aliases=aliases,
            compiler_params=_params(("arbitrary",) * len(grid)))(*args)
    n_in, n_out, n_sc = len(in_specs), len(out_specs), len(scratch)
    n_xi, n_xo, n_xs = len(exchange.ins), len(exchange.outs), len(exchange.staging)
    n_steps = 1
    for g in grid:
        n_steps *= g

    def wrapped(*refs):
        ins, refs = refs[:n_in], refs[n_in:]
        x_ins, refs = refs[:n_xi], refs[n_xi:]
        outs, refs = refs[:n_out], refs[n_out:]
        x_outs, refs = refs[:n_xo], refs[n_xo:]
        scr, refs = refs[:n_sc], refs[n_sc:]
        staging, (send_sems, recv_sems) = refs[:n_xs], refs[n_xs:]
        step = pl.program_id(0)
        for d in range(1, len(grid)):
            step = step * grid[d] + pl.program_id(d)
        for frac, fn in exchange.phases:
            at = min(int(frac * n_steps), n_steps - 1)

            @pl.when(step == at)
            def _(fn=fn):
                fn(x_ins, x_outs, send_sems, recv_sems, staging)

        kern(*ins, *outs, *scr)

    sems = [pltpu.SemaphoreType.DMA((exchange.n_sems,)), pltpu.SemaphoreType.DMA((exchange.n_sems,))]
    return pl.pallas_call(
        wrapped, grid=grid, in_specs=list(in_specs) + [_ANY] * n_xi, out_specs=list(out_specs) + [_ANY] * n_xo,
        out_shape=list(out_shape) + exchange.outs, name=name,
        scratch_shapes=list(scratch) + exchange.staging + sems, input_output_aliases=aliases,
        compiler_params=_params(("arbitrary",) * len(grid)))(*args, *exchange.ins)


def _run_exchange(exchange, name):
    def body(*refs):
        n_i, n_o = len(exchange.ins), len(exchange.outs)
        staging, (send_sems, recv_sems) = refs[n_i + n_o:-2], refs[-2:]
        for _, fn in exchange.phases:
            fn(refs[:n_i], refs[n_i:n_i + n_o], send_sems, recv_sems, staging)

    sems = [pltpu.SemaphoreType.DMA((exchange.n_sems,)), pltpu.SemaphoreType.DMA((exchange.n_sems,))]
    return pl.pallas_call(body, in_specs=[_ANY] * len(exchange.ins), out_specs=[_ANY] * len(exchange.outs),
                          out_shape=exchange.outs, scratch_shapes=exchange.staging + sems, name=name,
                          compiler_params=pltpu.CompilerParams(vmem_limit_bytes=VMEM_LIMIT_BYTES))(*exchange.ins)


def _fused(parts, *, grid, name, exchange=None):
    counts = [(len(p["in_specs"]), len(p["out_specs"]), len(p["scratch"])) for p in parts]
    n_in, n_out = sum(c[0] for c in counts), sum(c[1] for c in counts)

    def kern(*refs):
        ins, outs, scr = refs[:n_in], refs[n_in:n_in + n_out], refs[n_in + n_out:]
        i0 = o0 = s0 = 0
        for p, (ni, no, ns) in zip(parts, counts):
            p["kern"](*ins[i0:i0 + ni], *outs[o0:o0 + no], *scr[s0:s0 + ns])
            i0, o0, s0 = i0 + ni, o0 + no, s0 + ns

    cat = lambda key: [a for p in parts for a in p[key]]
    return _pallas(kern, grid=grid, in_specs=cat("in_specs"), out_specs=cat("out_specs"), out_shape=cat("out_shape"),
                   scratch=cat("scratch"), args=cat("args"), name=name, exchange=exchange)


def _row_call(body, *, tm, row_ins, res_ins, row_outs, part_outs=(), name, exchange=None):
    t = row_ins[0].shape[0]
    n_tiles = t // tm
    in_specs = [pl.BlockSpec((tm, a.shape[1]), lambda i: (i, 0)) for a in row_ins]
    in_specs += [pl.BlockSpec(a.shape, lambda i: (0, 0), pipeline_mode=pl.Buffered(1)) for a in res_ins]
    out_shape = [jax.ShapeDtypeStruct((t, w), dt) for (w, dt) in row_outs]
    out_shape += [jax.ShapeDtypeStruct((n_tiles, 1, w), F32) for w in part_outs]
    out_specs = [pl.BlockSpec((tm, w), lambda i: (i, 0)) for (w, _) in row_outs]
    out_specs += [pl.BlockSpec((1, 1, w), lambda i: (i, 0, 0)) for w in part_outs]
    n_ri, n_re, n_ro = len(row_ins), len(res_ins), len(row_outs)

    def kern(*refs):
        body(refs[:n_ri], refs[n_ri:n_ri + n_re], refs[n_ri + n_re:n_ri + n_re + n_ro], refs[n_ri + n_re + n_ro:])

    return _pallas(kern, grid=(n_tiles,), in_specs=in_specs, out_specs=out_specs, out_shape=out_shape,
                   args=[*row_ins, *res_ins], name=name, exchange=exchange)


def _proj_fwd(x, g1, w_in_t, exchange):
    pieces = ((P_QA, F32), (P_KVA, F32), (P_QR, F32), (P_KR, F32), (P_VR, BF16), (P_GR, F32), (P_ZA, F32), (P_ZR, F32))

    def body(ri, re, ro, po):
        x_t = ri[0][...]
        r, xhat = _rms_stats(x_t)
        hb = _bf(xhat * re[0][...])
        ro[0][...] = hb
        for k, ((off, w), dt) in enumerate(pieces):
            ro[1 + k][...] = _dot_nt(hb, re[1][off:off + w, :]).astype(dt)

    outs = [(D_MODEL, BF16)] + [(w, dt) for ((_, w), dt) in pieces]
    return _row_call(body, tm=256, row_ins=[x], res_ins=[g1, w_in_t], row_outs=outs, name="proj_fwd",
                     exchange=exchange)


def _mix_fwd(attn, ret, z_a, z_r, x, wba, wbr, wout, g2):
    def body(ri, re, ro, po):
        ba = _dot(ri[0][...], re[0][...])
        br = _dot(ri[1][...], re[1][...])
        m = jax.nn.sigmoid(ri[2][...]) * ba + jax.nn.sigmoid(ri[3][...]) * br
        mb = _bf(m)
        x1 = ri[4][...] + _dot(mb, re[2][...])
        r, xhat = _rms_stats(x1)
        ro[0][...] = ba
        ro[1][...] = br
        ro[2][...] = mb
        ro[3][...] = x1
        ro[4][...] = _bf(xhat * re[3][...])

    outs = [(D_MODEL, F32), (D_MODEL, F32), (D_MODEL, BF16), (D_MODEL, F32), (D_MODEL, BF16)]
    return _row_call(body, tm=512, row_ins=[attn, ret, z_a, z_r, x], res_ins=[wba, wbr, wout, g2], row_outs=outs,
                     name="mix_fwd")


def _ffn_fwd_bwd(h2, x1, target, wg_t, wu_t, wd, g2):
    def body(ri, re, ro, po):
        h2_t = ri[0][...]
        x1_t = ri[1][...]
        gate = _dot_nt(h2_t, re[0][...])
        up = _dot_nt(h2_t, re[1][...])
        sg = jax.nn.sigmoid(gate)
        sl = gate * sg
        actb = _bf(sl * up)
        ro[0][...] = actb
        y = x1_t + _dot(actb, re[2][...])
        e = y - ri[2][...]
        po[0][0] = jnp.broadcast_to(0.5 * jnp.sum(jnp.sum(e * e, axis=1, keepdims=True), axis=0, keepdims=True)
                                    * (1.0 / D_MODEL), (1, 128))
        dy = e * (1.0 / D_MODEL)
        dyb = _bf(dy)
        ro[3][...] = dyb
        dact = _dot_nt(dyb, re[2][...])
        dupb = _bf(dact * sl)
        dgateb = _bf(dact * up * (sg * (1.0 + gate * (1.0 - sg))))
        ro[1][...] = dgateb
        ro[2][...] = dupb
        dh2 = _dot(dgateb, re[0][...]) + _dot(dupb, re[1][...])
        r, xhat = _rms_stats(x1_t)
        dxn, dgain = _rms_bwd(dh2, xhat, r, re[3][...])
        dx1 = dy + dxn
        ro[4][...] = dx1
        ro[5][...] = _bf(dx1)
        po[1][0] = jnp.sum(dgain, axis=0, keepdims=True)

    outs = [(D_FF, BF16), (D_FF, BF16), (D_FF, BF16), (D_MODEL, BF16), (D_MODEL, F32), (D_MODEL, BF16)]
    return _row_call(body, tm=256, row_ins=[h2, x1, target], res_ins=[wg_t, wu_t, wd, g2], row_outs=outs,
                     part_outs=(128, D_MODEL), name="ffn_fwd_bwd")


def _mix_bwd(dx1b, z_a, z_r, ba, br, g_r, o_ret, wout, wba, wbr, exchange):
    def body(ri, re, ro, po):
        dm = _dot_nt(ri[0][...], re[0][...])
        sa = jax.nn.sigmoid(ri[1][...])
        sr = jax.nn.sigmoid(ri[2][...])
        dbab = _bf(sa * dm)
        dbrb = _bf(sr * dm)
        ro[0][...] = dbab
        ro[1][...] = dbrb
        ro[4][:, RET_V:RET_V + D_MODEL] = _bf(dm * ri[3][...] * (sa * (1.0 - sa)))
        ro[4][:, RET_V + D_MODEL:RET_V + 2 * D_MODEL] = _bf(dm * ri[4][...] * (sr * (1.0 - sr)))
        ro[2][...] = _bf(_dot_nt(dbab, re[1][...]))
        dret = _dot_nt(dbrb, re[2][...])
        for h in range(RET_HEADS):
            cols = slice(h * RET_V_DIM, (h + 1) * RET_V_DIM)
            g = ri[5][:, cols]
            r, rn = _rms_stats(ri[6][:, cols])
            sg = jax.nn.sigmoid(g)
            dret_h = dret[:, cols]
            d_rn = dret_h * (g * sg)
            ro[4][:, cols] = _bf(dret_h * rn * (sg * (1.0 + g * (1.0 - sg))))
            ro[3][:, cols] = r * (d_rn - rn * jnp.mean(d_rn * rn, axis=-1, keepdims=True))

    outs = [(D_MODEL, BF16), (D_MODEL, BF16), (ATT_Q, BF16), (RET_V, F32), (RET_V + 2 * D_MODEL, BF16)]
    return _row_call(body, tm=256, row_ins=[dx1b, z_a, z_r, ba, br, g_r, o_ret], res_ins=[wout, wba, wbr],
                     row_outs=outs, name="mix_bwd", exchange=exchange)


def _proj_bwd(d_pieces, x, dx1, w_in_t, g1, exchange):
    widths = [p.shape[1] for p in d_pieces]
    groups = [(sum(widths[:k]), w) for k, w in enumerate(widths)]
    n_p = len(groups)

    def body(ri, re, ro, po):
        dh = None
        for k, (off, w) in enumerate(groups):
            term = _dot(ri[k][...], re[0][off:off + w, :])
            dh = term if dh is None else dh + term
        r, xhat = _rms_stats(ri[n_p][...])
        dxn, dgain = _rms_bwd(dh, xhat, r, re[1][...])
        ro[0][...] = ri[n_p + 1][...] + dxn
        po[0][0] = jnp.sum(dgain, axis=0, keepdims=True)

    return _row_call(body, tm=512, row_ins=[*d_pieces, x, dx1], res_ins=[w_in_t, g1], row_outs=[(D_MODEL, F32)],
                     part_outs=(D_MODEL,), name="proj_bwd", exchange=exchange)


def _dw(a, b, *, tm, place, buf, name, exchange=None):
    t, m = a.shape
    n = b.shape[1]
    tk = min(2048, t)
    n_i, n_k = m // tm, t // tk
    fresh = isinstance(buf, jax.ShapeDtypeStruct)
    n_copies = len(place(0))

    def kern(a_ref, b_ref, *rest):
        out_ref, acc, sems = rest[-3:]
        i, k = pl.program_id(0), pl.program_id(1)
        part = _dot_tn(a_ref[...], b_ref[...])

        @pl.when(k == 0)
        def _():
            acc[i] = part

        @pl.when(k > 0)
        def _():
            acc[i] += part

        def copies(tile):
            return [pltpu.make_async_copy(acc.at[tile, pl.ds(r0, rows), :], out_ref.at[idx], sems.at[tile * n_copies + c])
                    for c, (r0, rows, idx) in enumerate(place(tile))]

        for tile in range(n_i):
            @pl.when((i == tile) & (k == n_k - 1))
            def _(tile=tile):
                for cp in copies(tile):
                    cp.start()

        @pl.when((i == n_i - 1) & (k == n_k - 1))
        def _():
            for tile in range(n_i):
                for cp in copies(tile):
                    cp.wait()

    in_specs = [pl.BlockSpec((tk, tm), lambda i, k: (k, i)), pl.BlockSpec((tk, n), lambda i, k: (k, 0))]
    shape = buf if fresh else jax.ShapeDtypeStruct(buf.shape, buf.dtype)
    return _pallas(
        kern, grid=(n_i, n_k), in_specs=in_specs + ([] if fresh else [_ANY]), out_specs=[_ANY], out_shape=[shape],
        scratch=[pltpu.VMEM((n_i, tm, n), F32), pltpu.SemaphoreType.DMA((n_i * n_copies,))],
        args=[a, b] + ([] if fresh else [buf]), aliases=None if fresh else {2: 0}, name=name, exchange=exchange)


def _heads_to_lanes(x3):
    return jnp.concatenate([x3[g] for g in range(GROUP)], axis=1)


def _lanes_to_heads(xt):
    return jnp.concatenate([xt[:, g * BLOCK:(g + 1) * BLOCK] for g in range(GROUP)], axis=0)


def _attn_queries(kvh, q_ref, gq_col):
    cols = slice(kvh * GROUP * HEAD_DIM, (kvh + 1) * GROUP * HEAD_DIM)
    q3 = q_ref[:, cols].T.reshape(GROUP, HEAD_DIM, BLOCK)
    rq = lax.rsqrt(jnp.mean(q3 * q3, axis=1, keepdims=True) + EPS)
    qhat = q3 * rq
    return qhat, rq, _heads_to_lanes(_bf(qhat * (gq_col * (HEAD_DIM ** -0.5))))


def _from_prev():
    j = lax.broadcasted_iota(jnp.int32, (BLOCK, GROUP * BLOCK), 0)
    i = lax.broadcasted_iota(jnp.int32, (BLOCK, GROUP * BLOCK), 1) & (BLOCK - 1)
    return j > i


def _attn_probs(n, kvh, qts, kvp_ref, kvc_ref, gk, sink_ref):
    kcols = slice(kvh * HEAD_DIM, (kvh + 1) * HEAD_DIM)
    k = jnp.concatenate([kvp_ref[:, kcols], kvc_ref[:, kcols]], axis=0)
    rk, khat = _rms_stats(k)
    st = _dot(_bf(khat * gk), qts)
    f = jnp.where(_from_prev(), jnp.where(n > 0, st[0:BLOCK], -1e30), st[BLOCK:2 * BLOCK])
    sink = jnp.concatenate([jnp.broadcast_to(sink_ref[0:1, kvh * GROUP + g:kvh * GROUP + g + 1], (1, BLOCK))
                            for g in range(GROUP)], axis=1)
    m = jnp.maximum(jnp.max(f, axis=0, keepdims=True), sink)
    e = jnp.exp(f - m)
    es = jnp.exp(sink - m)
    inv = 1.0 / (jnp.sum(e, axis=0, keepdims=True) + es)
    return e * inv, es * inv


def _unfold(from_prev, xf):
    return _bf(jnp.concatenate([jnp.where(from_prev, xf, 0.0), jnp.where(from_prev, 0.0, xf)], axis=0))


def _attn_fwd(q_a, kv_a, gq_col, gk, sinks):
    t = q_a.shape[0]
    nb = t // BLOCK

    def kern(q_ref, kvp_ref, kvc_ref, gq_ref, gk_ref, sink_ref, o_ref, pf_ref, ps_ref):
        n = pl.program_id(0)
        kvt = jnp.concatenate([kvp_ref[...].T, kvc_ref[...].T], axis=1)
        for kvh in range(N_KV_HEADS):
            _, _, qts = _attn_queries(kvh, q_ref, gq_ref[...])
            pf, psink = _attn_probs(n, kvh, qts, kvp_ref, kvc_ref, gk_ref[...], sink_ref)
            lanes = slice(kvh * GROUP * BLOCK, (kvh + 1) * GROUP * BLOCK)
            pf_ref[:, lanes] = pf
            ps_ref[:, lanes] = psink
            vt = _bf(kvt[ATT_KV + kvh * HEAD_DIM:ATT_KV + (kvh + 1) * HEAD_DIM, :])
            out_t = _dot(vt, _unfold(_from_prev(), pf))
            cols = slice(kvh * GROUP * HEAD_DIM, (kvh + 1) * GROUP * HEAD_DIM)
            o_ref[:, cols] = _bf(_lanes_to_heads(out_t).T)

    small = lambda a: pl.BlockSpec(a.shape, lambda n: (0, 0))
    folded = N_KV_HEADS * GROUP * BLOCK
    return dict(
        kern=kern,
        in_specs=[pl.BlockSpec((BLOCK, ATT_Q), lambda n: (n, 0)),
                  pl.BlockSpec((BLOCK, 2 * ATT_KV), lambda n: (jnp.maximum(n - 1, 0), 0)),
                  pl.BlockSpec((BLOCK, 2 * ATT_KV), lambda n: (n, 0)),
                  small(gq_col), small(gk), small(sinks)],
        out_specs=[pl.BlockSpec((BLOCK, ATT_Q), lambda n: (n, 0)), pl.BlockSpec((BLOCK, folded), lambda n: (n, 0)),
                   pl.BlockSpec((None, 1, folded), lambda n: (n, 0, 0))],
        out_shape=[jax.ShapeDtypeStruct((t, ATT_Q), BF16), jax.ShapeDtypeStruct((t, folded), F32),
                   jax.ShapeDtypeStruct((nb, 1, folded), F32)],
        scratch=[], args=[q_a, kv_a, kv_a, gq_col, gk, sinks])


def _attn_bwd(q_a, kv_a, d_attn, probs, sink_probs, gq_col, gk, gk_col):
    t = q_a.shape[0]
    nb = t // BLOCK

    def kern(q_ref, kvp_ref, kvc_ref, do_ref, pf_ref, ps_ref, gq_ref, gk_ref, gkc_ref,
             dq_ref, dkv_ref, dgq_ref, dgk_ref, dsink_ref, band_k, band_v, carry_k, carry_v):
        n = pl.program_id(0)
        gq_v = gq_ref[...]
        gk_v = gk_ref[...]

        @pl.when(n == 0)
        def _():
            carry_k[...] = jnp.zeros_like(carry_k)
            carry_v[...] = jnp.zeros_like(carry_v)
            dgq_ref[...] = jnp.zeros_like(dgq_ref)
            dgk_ref[...] = jnp.zeros_like(dgk_ref)
            dsink_ref[...] = jnp.zeros_like(dsink_ref)

        @pl.when(n == nb)
        def _():
            band_k[...] = jnp.zeros_like(band_k)
            band_v[...] = jnp.zeros_like(band_v)

        @pl.when(n < nb)
        def _():
            lane16 = lax.broadcasted_iota(jnp.int32, (1, N_Q_HEADS), 1)
            dsink = jnp.zeros((1, N_Q_HEADS), F32)
            dgq = jnp.zeros((HEAD_DIM, 1), F32)
            gk_col = gkc_ref[...]
            kvt = jnp.concatenate([kvp_ref[...].T, kvc_ref[...].T], axis=1)
            from_prev = _from_prev()
            for kvh in range(N_KV_HEADS):
                qhat, rq, qts = _attn_queries(kvh, q_ref, gq_v)
                lanes = slice(kvh * GROUP * BLOCK, (kvh + 1) * GROUP * BLOCK)
                pf = pf_ref[:, lanes]
                cols = slice(kvh * GROUP * HEAD_DIM, (kvh + 1) * GROUP * HEAD_DIM)
                vcols = slice(ATT_KV + kvh * HEAD_DIM, ATT_KV + (kvh + 1) * HEAD_DIM)
                dot = _heads_to_lanes(_bf(do_ref[:, cols].astype(F32).T.reshape(GROUP, HEAD_DIM, BLOCK)))
                vb = _bf(jnp.concatenate([kvp_ref[:, vcols], kvc_ref[:, vcols]], axis=0))
                dpt = _dot(vb, dot)
                dpf = jnp.where(from_prev, dpt[0:BLOCK], dpt[BLOCK:2 * BLOCK])
                delta = jnp.sum(pf * dpf, axis=0, keepdims=True)
                dst = _unfold(from_prev, pf * (dpf - delta))
                dsk = ps_ref[:, lanes] * delta
                for g in range(GROUP):
                    tot = jnp.sum(dsk[:, g * BLOCK:(g + 1) * BLOCK], axis=1, keepdims=True)
                    dsink = dsink - jnp.where(lane16 == kvh * GROUP + g, tot, 0.0)
                kt = kvt[kvh * HEAD_DIM:(kvh + 1) * HEAD_DIM, :]
                knt = _bf(kt * lax.rsqrt(jnp.mean(kt * kt, axis=0, keepdims=True) + EPS) * gk_col)
                dqn = (_dot(knt, dst) * (HEAD_DIM ** -0.5))
                band_k[kvh] = _dot_nt(dst, qts)
                band_v[kvh] = _dot_nt(_unfold(from_prev, pf), dot)
                dqn3 = _lanes_to_heads(dqn).reshape(GROUP, HEAD_DIM, BLOCK)
                u = dqn3 * gq_v
                dq3 = rq * (u - qhat * jnp.mean(u * qhat, axis=1, keepdims=True))
                dgq = dgq + jnp.sum(jnp.sum(dqn3 * qhat, axis=0), axis=1, keepdims=True)
                dq_ref[:, cols] = _bf(dq3.reshape(GROUP * HEAD_DIM, BLOCK).T)
            dsink_ref[...] += dsink
            dgq_ref[...] += dgq

        dgk = jnp.zeros((1, HEAD_DIM), F32)
        for kvh in range(N_KV_HEADS):
            kcols = slice(kvh * HEAD_DIM, (kvh + 1) * HEAD_DIM)
            vcols = slice(ATT_KV + kvh * HEAD_DIM, ATT_KV + (kvh + 1) * HEAD_DIM)
            dkn = carry_k[kvh] + band_k[kvh, 0:BLOCK, :]
            dv = carry_v[kvh] + band_v[kvh, 0:BLOCK, :]
            rk, khat = _rms_stats(kvp_ref[:, kcols])
            dk, dgain = _rms_bwd(dkn, khat, rk, gk_v)
            dgk = dgk + jnp.sum(dgain, axis=0, keepdims=True)
            dkv_ref[:, kcols] = _bf(dk)
            dkv_ref[:, vcols] = _bf(dv)
            carry_k[kvh] = band_k[kvh, BLOCK:2 * BLOCK, :]
            carry_v[kvh] = band_v[kvh, BLOCK:2 * BLOCK, :]
        dgk_ref[...] += dgk

    small = lambda a: pl.BlockSpec(a.shape, lambda n: (0, 0))
    last = nb - 1
    return dict(
        kern=kern,
        in_specs=[pl.BlockSpec((BLOCK, ATT_Q), lambda n: (jnp.minimum(n, last), 0)),
                  pl.BlockSpec((BLOCK, 2 * ATT_KV), lambda n: (jnp.maximum(n - 1, 0), 0)),
                  pl.BlockSpec((BLOCK, 2 * ATT_KV), lambda n: (jnp.minimum(n, last), 0)),
                  pl.BlockSpec((BLOCK, ATT_Q), lambda n: (jnp.minimum(n, last), 0)),
                  pl.BlockSpec((BLOCK, probs.shape[1]), lambda n: (jnp.minimum(n, last), 0)),
                  pl.BlockSpec((None, 1, probs.shape[1]), lambda n: (jnp.minimum(n, last), 0, 0)),
                  small(gq_col), small(gk), small(gk_col)],
        out_specs=[pl.BlockSpec((BLOCK, ATT_Q), lambda n: (jnp.minimum(n, last), 0)),
                   pl.BlockSpec((BLOCK, 2 * ATT_KV), lambda n: (jnp.maximum(n - 1, 0), 0)),
                   pl.BlockSpec((HEAD_DIM, 1), lambda n: (0, 0)),
                   pl.BlockSpec((1, HEAD_DIM), lambda n: (0, 0)),
                   pl.BlockSpec((1, N_Q_HEADS), lambda n: (0, 0))],
        out_shape=[jax.ShapeDtypeStruct((t, ATT_Q), BF16), jax.ShapeDtypeStruct((t, 2 * ATT_KV), BF16),
                   jax.ShapeDtypeStruct((HEAD_DIM, 1), F32), jax.ShapeDtypeStruct((1, HEAD_DIM), F32),
                   jax.ShapeDtypeStruct((1, N_Q_HEADS), F32)],
        scratch=[pltpu.VMEM((N_KV_HEADS, 2 * BLOCK, HEAD_DIM), F32),
                 pltpu.VMEM((N_KV_HEADS, 2 * BLOCK, HEAD_DIM), F32),
                 pltpu.VMEM((N_KV_HEADS, BLOCK, HEAD_DIM), F32),
                 pltpu.VMEM((N_KV_HEADS, BLOCK, HEAD_DIM), F32)],
        args=[q_a, kv_a, kv_a, d_attn, probs, sink_probs, gq_col, gk, gk_col])


def _ret_tables(t, exchange):
    theta = 1.0 / (RET_ROT_BASE ** jnp.linspace(0.0, 1.0, RET_QK_DIM // 2, dtype=F32))
    theta2 = jnp.repeat(theta, 2)[None, :]
    sign = jnp.tile(jnp.array([-1.0, 1.0], F32), RET_QK_DIM // 2)[None, :]

    def kern(theta_ref, sign_ref, cos_ref, sin_ref):
        first = pl.program_id(0) * RET_CHUNK
        pos = (first + lax.broadcasted_iota(jnp.int32, (RET_CHUNK, RET_QK_DIM), 0)).astype(F32)
        ang = pos * theta_ref[...]
        cos_ref[...] = jnp.cos(ang)
        sin_ref[...] = jnp.sin(ang) * sign_ref[...]

    row = pl.BlockSpec((1, RET_QK_DIM), lambda n: (0, 0))
    blk = pl.BlockSpec((RET_CHUNK, RET_QK_DIM), lambda n: (n, 0))
    cos, sin_s, *got = _pallas(kern, grid=(t // RET_CHUNK,), in_specs=[row, row], out_specs=[blk, blk],
                               out_shape=[jax.ShapeDtypeStruct((t, RET_QK_DIM), F32)] * 2, args=[theta2, sign],
                               name="position_tables", exchange=exchange)
    log_gamma = jnp.log(1.0 - 2.0 ** (-5.0 - jnp.arange(RET_HEADS, dtype=F32)))
    i = jnp.arange(RET_CHUNK, dtype=F32)
    diff = i[:, None] - i[None, :]
    causal = diff >= 0
    decay = jnp.where(causal[None], jnp.exp(jnp.where(causal, diff, 0.0)[None] * log_gamma[:, None, None]), 0.0)
    xi = jnp.exp((i + 1.0)[None, :] * log_gamma[:, None])[:, :, None]
    zeta = jnp.exp((RET_CHUNK - 1.0 - i)[None, :] * log_gamma[:, None])[:, :, None]
    gch = jnp.broadcast_to(jnp.exp(RET_CHUNK * log_gamma)[:, None, None], (RET_HEADS, 1, 128))
    return (cos, sin_s, decay, xi, zeta, gch), got


def _swap_pairs(x):
    lane = lax.broadcasted_iota(jnp.int32, x.shape, 1)
    return jnp.where((lane & 1) == 0, pltpu.roll(x, RET_QK_DIM - 1, 1), pltpu.roll(x, 1, 1))


def _rotate(x, cos, sin_s):
    return x * cos + _swap_pairs(x) * sin_s


def _rotate_bwd(dy, cos, sin_s):
    return dy * cos + _swap_pairs(dy * sin_s)


def _ret_specs(order):
    qk = pl.BlockSpec((RET_CHUNK, RET_QK), lambda j: (order(j), 0))
    v = pl.BlockSpec((RET_CHUNK, RET_V), lambda j: (order(j), 0))
    dec = pl.BlockSpec((RET_HEADS, RET_CHUNK, RET_CHUNK), lambda j: (0, 0, 0))
    col = pl.BlockSpec((RET_HEADS, RET_CHUNK, 1), lambda j: (0, 0, 0))
    gch = pl.BlockSpec((RET_HEADS, 1, 128), lambda j: (0, 0, 0))
    st = pl.BlockSpec((RET_HEADS, None, RET_QK_DIM, RET_V_DIM), lambda j: (0, order(j), 0, 0))
    pos = pl.BlockSpec((RET_CHUNK, RET_QK_DIM), lambda j: (order(j), 0))
    return qk, v, dec, col, gch, st, pos


def _ret_fwd(q_r, k_r, v_r, g_r, tables):
    t = q_r.shape[0]
    nc = t // RET_CHUNK
    cos, sin_s, decay, xi, zeta, gch = tables

    def kern(q_ref, k_ref, v_ref, g_ref, cos_ref, sin_ref, dec_ref, xi_ref, zeta_ref, gch_ref,
             o_ref, ret_ref, st_ref, state):
        @pl.when(pl.program_id(0) == 0)
        def _():
            state[...] = jnp.zeros_like(state)

        cos_t = cos_ref[...]
        sin_t = sin_ref[...]
        for h in range(RET_HEADS):
            qc = slice(h * RET_QK_DIM, (h + 1) * RET_QK_DIM)
            vc = slice(h * RET_V_DIM, (h + 1) * RET_V_DIM)
            qs = _bf(_rotate(q_ref[:, qc], cos_t, sin_t))
            ks = _rotate(k_ref[:, qc] * (RET_QK_DIM ** -0.5), cos_t, sin_t)
            vb = v_ref[:, vc]
            s_old = state[h]
            sb = _bf(s_old)
            st_ref[h] = sb
            inner = _dot_nt(qs, _bf(ks)) * dec_ref[h]
            out = _dot(_bf(inner), vb) + _dot(qs, sb) * xi_ref[h]
            state[h] = gch_ref[h, :, 0:1] * s_old + _dot_tn(_bf(ks * zeta_ref[h]), vb)
            o_ref[:, vc] = out
            r, rn = _rms_stats(out)
            g = g_ref[:, vc]
            ret_ref[:, vc] = _bf(g * jax.nn.sigmoid(g) * rn)

    qk, v, dec, col, gsp, st, pos = _ret_specs(lambda j: j)
    return dict(
        kern=kern,
        in_specs=[qk, qk, v, v, pos, pos, dec, col, col, gsp],
        out_specs=[v, v, st],
        out_shape=[jax.ShapeDtypeStruct((t, RET_V), F32), jax.ShapeDtypeStruct((t, RET_V), BF16),
                   jax.ShapeDtypeStruct((RET_HEADS, nc, RET_QK_DIM, RET_V_DIM), BF16)],
        scratch=[pltpu.VMEM((RET_HEADS, RET_QK_DIM, RET_V_DIM), F32)],
        args=[q_r, k_r, v_r, g_r, cos, sin_s, decay, xi, zeta, gch])


def _ret_bwd(q_r, k_r, v_r, d_o, states, tables):
    t = q_r.shape[0]
    nc = t // RET_CHUNK
    cos, sin_s, decay, xi, zeta, gch = tables

    def kern(q_ref, k_ref, v_ref, do_ref, st_ref, cos_ref, sin_ref, dec_ref, xi_ref, zeta_ref, gch_ref,
             d_ref, dstate):
        dq_ref, dk_ref = d_ref.at[:, 0:RET_QK], d_ref.at[:, RET_QK:2 * RET_QK]
        dv_ref = d_ref.at[:, 2 * RET_QK:2 * RET_QK + RET_V]

        @pl.when(pl.program_id(0) == 0)
        def _():
            dstate[...] = jnp.zeros_like(dstate)

        @pl.when(pl.program_id(0) < nc)
        def _():
            cos_t = cos_ref[...]
            sin_t = sin_ref[...]
            scale = RET_QK_DIM ** -0.5
            for h in range(RET_HEADS):
                qc = slice(h * RET_QK_DIM, (h + 1) * RET_QK_DIM)
                vc = slice(h * RET_V_DIM, (h + 1) * RET_V_DIM)
                qs = _bf(_rotate(q_ref[:, qc], cos_t, sin_t))
                ks = _rotate(k_ref[:, qc] * scale, cos_t, sin_t)
                ksb = _bf(ks)
                vb = v_ref[:, vc]
                d_o_t = do_ref[:, vc]
                dob = _bf(d_o_t)
                doxb = _bf(d_o_t * xi_ref[h])
                dec = dec_ref[h]
                ds_old = dstate[h]
                dsb = _bf(ds_old)
                pb = _bf(_dot_nt(qs, ksb) * dec)
                dpb = _bf(_dot_nt(dob, vb) * dec)
                dqs = _dot(dpb, ksb) + _dot_nt(doxb, st_ref[h])
                dks = _dot_tn(dpb, qs) + _dot_nt(vb, dsb) * zeta_ref[h]
                dv_ref[:, vc] = _bf(_dot_tn(pb, dob) + _dot(_bf(ks * zeta_ref[h]), dsb))
                dstate[h] = gch_ref[h, :, 0:1] * ds_old + _dot_tn(qs, doxb)
                dq_ref[:, qc] = _bf(_rotate_bwd(dqs, cos_t, sin_t))
                dk_ref[:, qc] = _bf(_rotate_bwd(dks, cos_t, sin_t) * scale)

    backwards = lambda j: jnp.maximum(nc - 1 - j, 0)
    qk, v, dec, col, gsp, st, pos = _ret_specs(backwards)
    return dict(
        kern=kern,
        in_specs=[qk, qk, v, v, st, pos, pos, dec, col, col, gsp],
        out_specs=[pl.BlockSpec((RET_CHUNK, 2 * RET_QK + RET_V), lambda j: (backwards(j), 0))],
        out_shape=[jax.ShapeDtypeStruct((t, 2 * RET_QK + RET_V), BF16)],
        scratch=[pltpu.VMEM((RET_HEADS, RET_QK_DIM, RET_V_DIM), F32)],
        args=[q_r, k_r, v_r, d_o, states, cos, sin_s, decay, xi, zeta, gch])


def _position():
    return lax.axis_index("x"), lax.axis_index("y"), lax.axis_index("c")


def _gather_exchange(owns, forward_at):
    n = len(owns)

    def copies(ins, outs, send_sems, recv_sems, staging):
        x, y, c = _position()
        sibling = (x, y, 1 - c)
        chips = [(1 - x, y), (x, 1 - y), (1 - x, 1 - y)]
        my_chip = 2 * x + y

        def slab(a, chip, hf):
            half = owns[a].shape[0] // 2
            return outs[a].at[chip, pl.ds(hf * half, half), :]

        def copy(k, src, dst, to):
            return pltpu.make_async_remote_copy(src_ref=src, dst_ref=dst, send_sem=send_sems.at[k],
                                                recv_sem=recv_sems.at[k], device_id=to, device_id_type=MESH)

        first, passed, from_sibling, stage_in, stage_out = [], [], [], [], []
        for a in range(n):
            half = owns[a].shape[0] // 2
            for k, (cx, cy) in enumerate(chips):
                first.append(copy(6 * a + k, ins[a].at[pl.ds(c * half, half), :], slab(a, my_chip, c), (cx, cy, c)))
                landed = slab(a, 2 * cx + cy, c)
                passed.append(copy(6 * a + 3 + k, landed, landed, sibling))
                theirs = slab(a, 2 * cx + cy, 1 - c)
                from_sibling.append(copy(6 * a + 3 + k, theirs, theirs, sibling))
            stage_in.append(pltpu.make_async_copy(ins[a], staging[a], send_sems.at[6 * n + a]))
            stage_out.append(pltpu.make_async_copy(staging[a], outs[a].at[my_chip], recv_sems.at[6 * n + a]))
        return first, passed, from_sibling, stage_in, stage_out

    def start(*args):
        first, _, _, stage_in, _ = copies(*args)
        for cp in first + stage_in:
            cp.start()

    def forward(*args):
        first, passed, _, stage_in, stage_out = copies(*args)
        for staged, cp in zip(stage_in, stage_out):
            staged.wait()
            cp.start()
        for arrived, cp in zip(first, passed):
            arrived.wait_recv()
            cp.start()

    def finish(*args):
        first, passed, from_sibling, _, stage_out = copies(*args)
        for cp in from_sibling:
            cp.wait_recv()
        for cp in first + passed:
            cp.wait_send()
        for cp in stage_out:
            cp.wait()

    outs = [jax.ShapeDtypeStruct((N_CHIPS, *a.shape), a.dtype) for a in owns]
    return _Exchange(owns, outs, 7 * n, [(0.0, start), (forward_at, forward), (1.0, finish)],
                     staging=[pltpu.VMEM(a.shape, a.dtype) for a in owns])


def _symmetric_exchange(ins, outs, plan):
    n_sems = len(plan([None] * len(ins), [None] * len(outs), 0, 0, 0, dry=True))

    def copies(in_refs, out_refs, send_sems, recv_sems, staging):
        x, y, c = _position()
        return [pltpu.make_async_remote_copy(src_ref=src, dst_ref=dst, send_sem=send_sems.at[k],
                                             recv_sem=recv_sems.at[k], device_id=dev, device_id_type=MESH)
                for k, (src, dst, dev) in enumerate(plan(in_refs, out_refs, x, y, c, dry=False))]

    def start(*args):
        for cp in copies(*args):
            cp.start()

    def finish(*args):
        for cp in copies(*args):
            cp.wait()

    return _Exchange(ins, outs, n_sems, [(0.0, start), (1.0, finish)])


def _pair_exchange(gs):
    def plan(in_refs, out_refs, x, y, c, dry):
        out = []
        for a, g in enumerate(gs):
            half = g.shape[1] // 2
            for k in range(N_CHIPS):
                out.append(None if dry else (in_refs[a].at[k, pl.ds((1 - c) * half, half), :], out_refs[a].at[k],
                                             (x, y, 1 - c)))
        return out

    outs = [jax.ShapeDtypeStruct((g.shape[0], g.shape[1] // 2, g.shape[2]), g.dtype) for g in gs]
    return _symmetric_exchange(gs, outs, plan)


def _pair_sum(g, from_sibling, c_arr, *, tile, name):
    n, rows, width = g.shape
    tiles = (rows // 2) // tile

    def kern(c_ref, g_ref, s_ref, o_ref):
        o_ref[...] = _bf(g_ref[...] + s_ref[...])

    return pl.pallas_call(
        kern,
        grid_spec=pltpu.PrefetchScalarGridSpec(
            num_scalar_prefetch=1, grid=(n, tiles),
            in_specs=[pl.BlockSpec((None, tile, width), lambda k, i, c: (k, c[0] * tiles + i, 0)),
                      pl.BlockSpec((None, tile, width), lambda k, i, c: (k, i, 0))],
            out_specs=pl.BlockSpec((None, tile, width), lambda k, i, c: (k, i, 0))),
        out_shape=jax.ShapeDtypeStruct((n, rows // 2, width), BF16), name=name,
        compiler_params=_params(("parallel", "parallel")),
    )(c_arr, g, from_sibling)


def _scatter_to_owners(hsums):
    def plan(in_refs, out_refs, x, y, c, dry):
        out = []
        for a in range(len(hsums)):
            for k, (cx, cy) in enumerate([(1 - x, y), (x, 1 - y), (1 - x, 1 - y)]):
                out.append(None if dry else (in_refs[a].at[2 * cx + cy], out_refs[a].at[k], (cx, cy, c)))
        return out

    outs = [jax.ShapeDtypeStruct((3, *h.shape[1:]), h.dtype) for h in hsums]
    return _symmetric_exchange(hsums, outs, plan)


def _sum_chips(hsum, parts, chip_arr, *, tile, name):
    n, half, width = parts.shape

    def kern(chip_ref, h_ref, p_ref, o_ref):
        acc = h_ref[...].astype(F32)
        for k in range(n):
            acc = acc + p_ref[k].astype(F32)
        o_ref[...] = acc

    return pl.pallas_call(
        kern,
        grid_spec=pltpu.PrefetchScalarGridSpec(
            num_scalar_prefetch=1, grid=(half // tile,),
            in_specs=[pl.BlockSpec((None, tile, width), lambda i, chip: (chip[0], i, 0)),
                      pl.BlockSpec((n, tile, width), lambda i, chip: (0, i, 0))],
            out_specs=pl.BlockSpec((tile, width), lambda i, chip: (i, 0))),
        out_shape=jax.ShapeDtypeStruct((half, width), F32), name=name,
        compiler_params=_params(("parallel",)),
    )(chip_arr, hsum, parts)


def _share_halves(fhalves):
    def plan(in_refs, out_refs, x, y, c, dry):
        return [None if dry else (in_refs[a], out_refs[a], (x, y, 1 - c)) for a in range(len(fhalves))]

    return _symmetric_exchange(fhalves, [jax.ShapeDtypeStruct(f.shape, f.dtype) for f in fhalves], plan)


def _adamw_math(w, g, m, v):
    m = ADAM_B1 * m + (1.0 - ADAM_B1) * g
    v = ADAM_B2 * v + (1.0 - ADAM_B2) * (g * g)
    m_hat = m / (1.0 - ADAM_B1 ** ADAM_STEP)
    v_hat = v / (1.0 - ADAM_B2 ** ADAM_STEP)
    delta = -ADAM_LR * (m_hat / (jnp.sqrt(v_hat) + ADAM_EPS) + ADAM_WD * w)
    return delta, m, v


def _adamw(mats, g_mine, g_other, c_arr, *, tile, name):
    width = g_mine.shape[1]
    tiles_per_half = g_mine.shape[0] // tile
    n_tiles = [w.shape[0] // tile for w, _, _, _ in mats]
    n_mats = len(mats)

    def kern(c_ref, *refs):
        ins, outs = refs[:5 * n_mats], refs[5 * n_mats:]
        for j, (_, _, _, row_off) in enumerate(mats):
            w_ref, gm_ref, go_ref, m_ref, v_ref = ins[5 * j:5 * j + 5]
            i = jnp.minimum(pl.program_id(0), n_tiles[j] - 1)
            in_my_half = ((row_off // tile + i) // tiles_per_half) == c_ref[0]
            g = jnp.where(in_my_half, gm_ref[...], go_ref[...])
            d, nm, nv = _adamw_math(w_ref[...], g, m_ref[...], v_ref[...])
            for out_ref, val in zip(outs[4 * j:4 * j + 4], (g, d, nm, nv)):
                out_ref[...] = val

    in_specs, out_specs, out_shape, args = [], [], [], []
    for (w, m, v, row_off), nt in zip(mats, n_tiles):
        full = pl.BlockSpec((tile, width), lambda i, c, nt=nt: (jnp.minimum(i, nt - 1), 0))

        def half(mine, nt=nt, first=row_off // tile):
            def index(i, c):
                pos = first + jnp.minimum(i, nt - 1)
                used = ((pos // tiles_per_half) == c[0]) == mine
                return (jnp.where(used, pos % tiles_per_half, 0), 0)
            return pl.BlockSpec((tile, width), index)

        in_specs += [full, half(True), half(False), full, full]
        out_specs += [full] * 4
        out_shape += [jax.ShapeDtypeStruct(w.shape, F32)] * 4
        args += [w, g_mine, g_other, m, v]
    outs = pl.pallas_call(
        kern,
        grid_spec=pltpu.PrefetchScalarGridSpec(num_scalar_prefetch=1, grid=(max(n_tiles),), in_specs=in_specs,
                                               out_specs=out_specs),
        out_shape=out_shape, name=name, compiler_params=_params(("arbitrary",)),
    )(c_arr, *args)
    return [outs[4 * j:4 * j + 4] for j in range(n_mats)]


def _small_step(partials, params):
    slots = ((0, 0, D_MODEL), (1, 0, D_MODEL), (2, 0, HEAD_DIM), (2, 128, HEAD_DIM), (2, 256, N_Q_HEADS))
    loss_slot = (2, 384, 128)

    def body(*refs):
        loss_ref, dg1_ref, dg2_ref, dgq_ref, dgk_ref, dsink_ref = refs[:6]
        p_refs, out_refs = refs[6:21], refs[21:42]
        mine, gathered, send_sems, recv_sems = refs[42:]
        x, y, c = _position()
        me = 4 * x + 2 * y + c
        mine[...] = jnp.zeros_like(mine)
        for (row, lane, n), val in zip(slots + (loss_slot,), (
                jnp.sum(dg1_ref[...], axis=0, keepdims=True), jnp.sum(dg2_ref[...], axis=0, keepdims=True),
                dgq_ref[...], dgk_ref[...], dsink_ref[...], jnp.sum(loss_ref[...], axis=0, keepdims=True))):
            mine[row:row + 1, lane:lane + n] = val
        copies = []
        for k in range(1, N_DEV):
            flip = (k >> 2) & 1, (k >> 1) & 1, k & 1
            to = (x ^ flip[0], y ^ flip[1], c ^ flip[2])
            cp = pltpu.make_async_remote_copy(
                src_ref=mine, dst_ref=gathered.at[me], send_sem=send_sems.at[k - 1], recv_sem=recv_sems.at[k - 1],
                device_id=to, device_id_type=MESH)
            cp.start()
            copies.append(cp)
        gathered[me] = mine[...]
        for k in range(1, N_DEV):
            flip = (k >> 2) & 1, (k >> 1) & 1, k & 1
            src = 4 * (x ^ flip[0]) + 2 * (y ^ flip[1]) + (c ^ flip[2])
            pltpu.make_async_remote_copy(
                src_ref=mine, dst_ref=gathered.at[src], send_sem=send_sems.at[k - 1], recv_sem=recv_sems.at[k - 1],
                device_id=(x, y, c), device_id_type=MESH).wait_recv()
        for cp in copies:
            cp.wait_send()
        total = gathered[0]
        for k in range(1, N_DEV):
            total = total + gathered[k]
        row, lane, n = loss_slot
        out_refs[0][...] = total[row:row + 1, lane:lane + n]
        for i, (row, lane, n) in enumerate(slots):
            g = total[row:row + 1, lane:lane + n]
            d, nm, nv = _adamw_math(p_refs[i][...], g, p_refs[5 + i][...], p_refs[10 + i][...])
            for kind, val in enumerate((g, d, nm, nv)):
                out_refs[1 + 5 * kind + i][...] = val

    vm = pl.BlockSpec(memory_space=pltpu.VMEM)
    shapes = [jax.ShapeDtypeStruct((1, 128), F32)] + [jax.ShapeDtypeStruct((1, n), F32) for _, _, n in slots] * 4
    return pl.pallas_call(
        body, in_specs=[vm] * 21, out_specs=[vm] * 21, out_shape=shapes,
        scratch_shapes=[pltpu.VMEM((SMALL_ROWS, D_MODEL), F32), pltpu.VMEM((N_DEV, SMALL_ROWS, D_MODEL), F32),
                        pltpu.SemaphoreType.DMA((N_DEV - 1,)), pltpu.SemaphoreType.DMA((N_DEV - 1,))],
        name="small_step",
    )(*partials, *params)


def kernel(x, norm_mix_gain, w_in, q_norm_gain, k_norm_gain, attn_sinks, w_branch_attn, w_branch_ret, w_out, norm_ffn_gain, w_ffn_gate, w_ffn_up, w_ffn_down, loss_target, m_norm_mix_gain, m_w_in, m_q_norm_gain, m_k_norm_gain, m_attn_sinks, m_w_branch_attn, m_w_branch_ret, m_w_out, m_norm_ffn_gain, m_w_ffn_gate, m_w_ffn_up, m_w_ffn_down, v_norm_mix_gain, v_w_in, v_q_norm_gain, v_k_norm_gain, v_attn_sinks, v_w_branch_attn, v_w_branch_ret, v_w_out, v_norm_ffn_gain, v_w_ffn_gate, v_w_ffn_up, v_w_ffn_down):
    my_chip = 2 * lax.axis_index("x") + lax.axis_index("y")
    c_arr = lax.axis_index("c").astype(jnp.int32).reshape(1)
    chip_arr = my_chip.astype(jnp.int32).reshape(1)
    x_t, target = x[0], loss_target[0]
    g1, g2, gq, gk, sinks = norm_mix_gain, norm_ffn_gain, q_norm_gain, k_norm_gain, attn_sinks

    tr = lambda a: jnp.transpose(a[0])
    own_w_in = _bf(tr(w_in))
    own_rest = [_bf(a) for a in (tr(w_ffn_gate), tr(w_ffn_up), w_ffn_down[0], w_branch_attn[0], w_branch_ret[0],
                                 w_out[0])]
    tables, (got_w_in,) = _ret_tables(x_t.shape[0], _gather_exchange([own_w_in], 0.9))
    w_in_t = got_w_in.reshape(D_IN, D_MODEL)
    h1, q_a, kv_a, q_r, k_r, v_r, g_r, z_a, z_r, *got_rest = _proj_fwd(x_t, g1, w_in_t, _gather_exchange(own_rest, 0.8))
    wg_t, wu_t, wd, wba, wbr, wout = [got.reshape(-1, D_MODEL) for got in got_rest]

    gq_col, gk_col = gq.reshape(HEAD_DIM, 1), gk.reshape(HEAD_DIM, 1)
    attn, probs, sink_probs, o_ret, ret, states = _fused(
        [_attn_fwd(q_a, kv_a, gq_col, gk, sinks), _ret_fwd(q_r, k_r, v_r, g_r, tables)],
        grid=(x_t.shape[0] // BLOCK,), name="mixers_fwd")
    ba, br, merged, x1, h2 = _mix_fwd(attn, ret, z_a, z_r, x_t, wba, wbr, wout, g2)
    act, dgate, dup, dyb, dx1, dx1b, loss_p, dg2_p = _ffn_fwd_bwd(h2, x1, target, wg_t, wu_t, wd, g2)

    def pairs(row0, rows):
        return lambda i: [(h * rows, rows, (2 * i + h, pl.ds(row0, rows), slice(None))) for h in range(2)]

    f_block = jax.ShapeDtypeStruct((N_CHIPS, 3 * FF_SH, D_MODEL), F32)
    f_block, = _dw(dgate, h2, tm=2 * FF_SH, place=pairs(0, FF_SH), buf=f_block, name="dw_gate")
    f_block, = _dw(dup, h2, tm=2 * FF_SH, place=pairs(FF_SH, FF_SH), buf=f_block, name="dw_up")
    f_block, = _dw(act, dyb, tm=2 * FF_SH, place=pairs(2 * FF_SH, FF_SH), buf=f_block, name="dw_down")
    (dba, dbr, d_attn, d_o, d_gz, sib_ffn) = _mix_bwd(
        dx1b, z_a, z_r, ba, br, g_r, o_ret, wout, wba, wbr, _pair_exchange([f_block]))
    f_sum = _pair_sum(f_block, sib_ffn, c_arr, tile=528, name="pair_sum_ffn")

    def quarters(row0, rows):
        return lambda i: [(k * rows, rows, (k, pl.ds(row0, rows), slice(None))) for k in range(N_CHIPS)]

    m_block = jax.ShapeDtypeStruct((N_CHIPS, D_MODEL, D_MODEL), F32)
    m_block, = _dw(attn, dba, tm=ATT_Q, place=quarters(0, 256), buf=m_block, name="dw_ba")
    m_block, = _dw(ret, dbr, tm=D_MODEL, place=pairs(256, 512), buf=m_block, name="dw_br")
    m_block, = _dw(merged, dx1b, tm=D_MODEL, place=quarters(768, 256), buf=m_block, name="dw_out")

    def w_in_rows(off, w):
        tm = min(w, D_MODEL)
        return dict(tm=tm, place=lambda i: [(0, tm, (pl.ds(off + i * tm, tm), slice(None)))])

    w_block = jax.ShapeDtypeStruct((D_IN, D_MODEL), F32)
    w_block, sib_mix = _dw(d_gz, h1, buf=w_block, name="dw_in_gz", exchange=_pair_exchange([m_block]),
                           **w_in_rows(P_GR[0], d_gz.shape[1]))
    m_sum = _pair_sum(m_block, sib_mix, c_arr, tile=256, name="pair_sum_mix")

    (dq_a, dkv_a, dgq, dgk, dsinks, d_ret, got_ffn_sums, got_mix_sums) = _fused(
        [_attn_bwd(q_a, kv_a, d_attn, probs, sink_probs, gq_col, gk, gk_col),
         _ret_bwd(q_r, k_r, v_r, d_o, states, tables)],
        grid=(x_t.shape[0] // BLOCK + 1,), name="mixers_bwd", exchange=_scatter_to_owners([f_sum, m_sum]))
    dgq = dgq.reshape(1, HEAD_DIM)
    ffn_half = _sum_chips(f_sum, got_ffn_sums, chip_arr, tile=528, name="sum_chips_ffn")
    mix_half = _sum_chips(m_sum, got_mix_sums, chip_arr, tile=256, name="sum_chips_mix")
    w_block, = _dw(d_ret, h1, buf=w_block, name="dw_in_ret", **w_in_rows(P_QR[0], d_ret.shape[1]))
    w_block, ffn_other, mix_other = _dw(dq_a, h1, buf=w_block, name="dw_in_q",
                                        exchange=_share_halves([ffn_half, mix_half]), **w_in_rows(*P_QA))
    w_block, = _dw(dkv_a, h1, buf=w_block, name="dw_in_kv", **w_in_rows(*P_KVA))

    w_block = w_block.reshape(N_CHIPS, W_IN_SH, D_MODEL)
    sib_w_in, = _run_exchange(_pair_exchange([w_block]), "pair_exchange_w_in")
    w_sum = _pair_sum(w_block, sib_w_in, c_arr, tile=592, name="pair_sum_w_in")
    d_pieces = [dq_a, dkv_a, d_ret, d_gz]
    grad_x, dg1_p, got_w_in_sums = _proj_bwd(d_pieces, x_t, dx1, w_in_t, g1, _scatter_to_owners([w_sum]))
    w_in_half = _sum_chips(w_sum, got_w_in_sums, chip_arr, tile=592, name="sum_chips_w_in")
    w_in_other, = _run_exchange(_share_halves([w_in_half]), "share_halves_w_in")

    def update(name, g_half, g_other, tile, mats):
        outs = _adamw([tuple(tr(a) if t else a[0] for a in wmv) + (off,) for _, *wmv, off, t in mats],
                      g_half, g_other, c_arr, tile=tile, name=f"adamw_{name}")
        return {key: [jnp.transpose(o) if t else o for o in res] for (key, _, _, _, _, t), res in zip(mats, outs)}

    big = {
        **update("w_in", w_in_half, w_in_other, 592, [("w_in", w_in, m_w_in, v_w_in, 0, True)]),
        **update("ffn", ffn_half, ffn_other, 176, [
            ("wg", w_ffn_gate, m_w_ffn_gate, v_w_ffn_gate, 0, True),
            ("wu", w_ffn_up, m_w_ffn_up, v_w_ffn_up, FF_SH, True),
            ("wd", w_ffn_down, m_w_ffn_down, v_w_ffn_down, 2 * FF_SH, False)]),
        **update("mix", mix_half, mix_other, 128, [
            ("wba", w_branch_attn, m_w_branch_attn, v_w_branch_attn, 0, False),
            ("wbr", w_branch_ret, m_w_branch_ret, v_w_branch_ret, 256, False),
            ("wout", w_out, m_w_out, v_w_out, 768, False)])}

    loss_row, *small = _small_step(
        [loss_p.reshape(-1, 128), dg1_p.reshape(-1, D_MODEL), dg2_p.reshape(-1, D_MODEL), dgq, dgk, dsinks],
        [norm_mix_gain, norm_ffn_gain, q_norm_gain, k_norm_gain, attn_sinks,
         m_norm_mix_gain, m_norm_ffn_gain, m_q_norm_gain, m_k_norm_gain, m_attn_sinks,
         v_norm_mix_gain, v_norm_ffn_gain, v_q_norm_gain, v_k_norm_gain, v_attn_sinks])
    loss = loss_row[0, 0]

    def leaves(i):
        b = [big[n][i][None] for n in ("w_in", "wba", "wbr", "wout", "wg", "wu", "wd")]
        s1, s2, sq, sk, ss = small[5 * i:5 * i + 5]
        return [s1, b[0], sq, sk, ss, b[1], b[2], b[3], s2, b[4], b[5], b[6]]

    return (loss, grad_x[None], *leaves(0), *leaves(1), *leaves(2), *leaves(3))
```

```python
import jax
import jax.numpy as jnp
from jax import lax
from jax.experimental import pallas as pl
from jax.experimental.pallas import tpu as pltpu

F32 = jnp.float32
BF16 = jnp.bfloat16
MESH = pl.DeviceIdType.MESH

D_MODEL = 1024
EPS = 1e-6
HEAD_DIM = 64
N_Q_HEADS = 16
N_KV_HEADS = 2
GROUP = 8
BLOCK = 128
RET_HEADS = 4
RET_QK_DIM = 256
RET_V_DIM = 512
RET_CHUNK = 128
RET_ROT_BASE = 10000.0
D_FF = 2816
ATT_Q = N_Q_HEADS * HEAD_DIM
ATT_KV = N_KV_HEADS * HEAD_DIM
RET_QK = RET_HEADS * RET_QK_DIM
RET_V = RET_HEADS * RET_V_DIM
D_IN = 9472
ADAM_LR = 0.001
ADAM_B1 = 0.9
ADAM_B2 = 0.999
ADAM_EPS = 1e-08
ADAM_WD = 0.01
ADAM_STEP = 10

N_CHIPS = 4
N_DEV = 8
VMEM_LIMIT_BYTES = 60 * 1024 * 1024

P_QA = (0, 1024)
P_KVA = (1024, 256)
P_QR = (1280, 1024)
P_KR = (2304, 1024)
P_VR = (3328, 2048)
P_GR = (5376, 2048)
P_ZA = (7424, 1024)
P_ZR = (8448, 1024)

W_IN_SH = D_IN // N_CHIPS
FF_SH = D_FF // N_CHIPS

SMALL_ROWS = 8


def _dot(a, b):
    return jnp.dot(a, b, preferred_element_type=F32)


def _dot_nt(a, b):
    return lax.dot_general(a, b, (((1,), (1,)), ((), ())), preferred_element_type=F32)


def _dot_tn(a, b):
    return lax.dot_general(a, b, (((0,), (0,)), ((), ())), preferred_element_type=F32)


def _bf(x):
    return x.astype(BF16)


def _rms_stats(x):
    r = lax.rsqrt(jnp.mean(x * x, axis=-1, keepdims=True) + EPS)
    return r, x * r


def _rms_bwd(dy, xhat, r, gain):
    u = dy * gain
    dx = r * (u - xhat * jnp.mean(u * xhat, axis=-1, keepdims=True))
    return dx, dy * xhat


def _params(sem):
    return pltpu.CompilerParams(dimension_semantics=sem, vmem_limit_bytes=VMEM_LIMIT_BYTES)


_ANY = pl.BlockSpec(memory_space=pl.ANY)


class _Exchange:
    def __init__(self, ins, outs, n_sems, phases, staging=()):
        self.ins, self.outs, self.n_sems, self.phases = list(ins), list(outs), n_sems, list(phases)
        self.staging = list(staging)


def _pallas(kern, *, grid, in_specs, out_specs, out_shape, args, name, scratch=(), exchange=None, aliases=None):
    aliases = aliases or {}
    if exchange is None:
        return pl.pallas_call(
            kern, grid=grid, in_specs=in_specs, out_specs=out_specs, out_shape=out_shape, name=name,
            scratch_shapes=list(scratch), input_output_aliases=aliases,
            compiler_params=_params(("arbitrary",) * len(grid)))(*args)
    n_in, n_out, n_sc = len(in_specs), len(out_specs), len(scratch)
    n_xi, n_xo, n_xs = len(exchange.ins), len(exchange.outs), len(exchange.staging)
    n_steps = 1
    for g in grid:
        n_steps *= g

    def wrapped(*refs):
        ins, refs = refs[:n_in], refs[n_in:]
        x_ins, refs = refs[:n_xi], refs[n_xi:]
        outs, refs = refs[:n_out], refs[n_out:]
        x_outs, refs = refs[:n_xo], refs[n_xo:]
        scr, refs = refs[:n_sc], refs[n_sc:]
        staging, (send_sems, recv_sems) = refs[:n_xs], refs[n_xs:]
        step = pl.program_id(0)
        for d in range(1, len(grid)):
            step = step * grid[d] + pl.program_id(d)
        for frac, fn in exchange.phases:
            at = min(int(frac * n_steps), n_steps - 1)

            @pl.when(step == at)
            def _(fn=fn):
                fn(x_ins, x_outs, send_sems, recv_sems, staging)

        kern(*ins, *outs, *scr)

    sems = [pltpu.SemaphoreType.DMA((exchange.n_sems,)), pltpu.SemaphoreType.DMA((exchange.n_sems,))]
    return pl.pallas_call(
        wrapped, grid=grid, in_specs=list(in_specs) + [_ANY] * n_xi, out_specs=list(out_specs) + [_ANY] * n_xo,
        out_shape=list(out_shape) + exchange.outs, name=name,
        scratch_shapes=list(scratch) + exchange.staging + sems, input_output_aliases=aliases,
        compiler_params=_params(("arbitrary",) * len(grid)))(*args, *exchange.ins)


def _run_exchange(exchange, name):
    def body(*refs):
        n_i, n_o = len(exchange.ins), len(exchange.outs)
        staging, (send_sems, recv_sems) = refs[n_i + n_o:-2], refs[-2:]
        for _, fn in exchange.phases:
            fn(refs[:n_i], refs[n_i:n_i + n_o], send_sems, recv_sems, staging)

    sems = [pltpu.SemaphoreType.DMA((exchange.n_sems,)), pltpu.SemaphoreType.DMA((exchange.n_sems,))]
    return pl.pallas_call(body, in_specs=[_ANY] * len(exchange.ins), out_specs=[_ANY] * len(exchange.outs),
                          out_shape=exchange.outs, scratch_shapes=exchange.staging + sems, name=name,
                          compiler_params=pltpu.CompilerParams(vmem_limit_bytes=VMEM_LIMIT_BYTES))(*exchange.ins)


def _fused(parts, *, grid, name, exchange=None):
    counts = [(len(p["in_specs"]), len(p["out_specs"]), len(p["scratch"])) for p in parts]
    n_in, n_out = sum(c[0] for c in counts), sum(c[1] for c in counts)

    def kern(*refs):
        ins, outs, scr = refs[:n_in], refs[n_in:n_in + n_out], refs[n_in + n_out:]
        i0 = o0 = s0 = 0
        for p, (ni, no, ns) in zip(parts, counts):
            p["kern"](*ins[i0:i0 + ni], *outs[o0:o0 + no], *scr[s0:s0 + ns])
            i0, o0, s0 = i0 + ni, o0 + no, s0 + ns

    cat = lambda key: [a for p in parts for a in p[key]]
    return _pallas(kern, grid=grid, in_specs=cat("in_specs"), out_specs=cat("out_specs"), out_shape=cat("out_shape"),
                   scratch=cat("scratch"), args=cat("args"), name=name, exchange=exchange)


def _row_call(body, *, tm, row_ins, res_ins, row_outs, part_outs=(), name, exchange=None):
    t = row_ins[0].shape[0]
    n_tiles = t // tm
    in_specs = [pl.BlockSpec((tm, a.shape[1]), lambda i: (i, 0)) for a in row_ins]
    in_specs += [pl.BlockSpec(a.shape, lambda i: (0, 0), pipeline_mode=pl.Buffered(1)) for a in res_ins]
    out_shape = [jax.ShapeDtypeStruct((t, w), dt) for (w, dt) in row_outs]
    out_shape += [jax.ShapeDtypeStruct((n_tiles, 1, w), F32) for w in part_outs]
    out_specs = [pl.BlockSpec((tm, w), lambda i: (i, 0)) for (w, _) in row_outs]
    out_specs += [pl.BlockSpec((1, 1, w), lambda i: (i, 0, 0)) for w in part_outs]
    n_ri, n_re, n_ro = len(row_ins), len(res_ins), len(row_outs)

    def kern(*refs):
        body(refs[:n_ri], refs[n_ri:n_ri + n_re], refs[n_ri + n_re:n_ri + n_re + n_ro], refs[n_ri + n_re + n_ro:])

    return _pallas(kern, grid=(n_tiles,), in_specs=in_specs, out_specs=out_specs, out_shape=out_shape,
                   args=[*row_ins, *res_ins], name=name, exchange=exchange)


def _proj_fwd(x, g1, w_in_t, exchange):
    pieces = ((P_QA, F32), (P_KVA, F32), (P_QR, F32), (P_KR, F32), (P_VR, BF16), (P_GR, F32), (P_ZA, F32), (P_ZR, F32))

    def body(ri, re, ro, po):
        x_t = ri[0][...]
        r, xhat = _rms_stats(x_t)
        hb = _bf(xhat * re[0][...])
        ro[0][...] = hb
        for k, ((off, w), dt) in enumerate(pieces):
            ro[1 + k][...] = _dot_nt(hb, re[1][off:off + w, :]).astype(dt)

    outs = [(D_MODEL, BF16)] + [(w, dt) for ((_, w), dt) in pieces]
    return _row_call(body, tm=256, row_ins=[x], res_ins=[g1, w_in_t], row_outs=outs, name="proj_fwd",
                     exchange=exchange)


def _mix_fwd(attn, ret, z_a, z_r, x, wba, wbr, wout, g2):
    def body(ri, re, ro, po):
        ba = _dot(ri[0][...], re[0][...])
        br = _dot(ri[1][...], re[1][...])
        m = jax.nn.sigmoid(ri[2][...]) * ba + jax.nn.sigmoid(ri[3][...]) * br
        mb = _bf(m)
        x1 = ri[4][...] + _dot(mb, re[2][...])
        r, xhat = _rms_stats(x1)
        ro[0][...] = ba
        ro[1][...] = br
        ro[2][...] = mb
        ro[3][...] = x1
        ro[4][...] = _bf(xhat * re[3][...])

    outs = [(D_MODEL, F32), (D_MODEL, F32), (D_MODEL, BF16), (D_MODEL, F32), (D_MODEL, BF16)]
    return _row_call(body, tm=512, row_ins=[attn, ret, z_a, z_r, x], res_ins=[wba, wbr, wout, g2], row_outs=outs,
                     name="mix_fwd")


def _ffn_fwd_bwd(h2, x1, target, wg_t, wu_t, wd, g2):
    def body(ri, re, ro, po):
        h2_t = ri[0][...]
        x1_t = ri[1][...]
        gate = _dot_nt(h2_t, re[0][...])
        up = _dot_nt(h2_t, re[1][...])
        sg = jax.nn.sigmoid(gate)
        sl = gate * sg
        actb = _bf(sl * up)
        ro[0][...] = actb
        y = x1_t + _dot(actb, re[2][...])
        e = y - ri[2][...]
        po[0][0] = jnp.broadcast_to(0.5 * jnp.sum(jnp.sum(e * e, axis=1, keepdims=True), axis=0, keepdims=True)
                                    * (1.0 / D_MODEL), (1, 128))
        dy = e * (1.0 / D_MODEL)
        dyb = _bf(dy)
        ro[2][...] = dyb
        dact = _dot_nt(dyb, re[2][...])
        dupb = _bf(dact * sl)
        dgateb = _bf(dact * up * (sg * (1.0 + gate * (1.0 - sg))))
        ro[1][:, 0:D_FF] = dgateb
        ro[1][:, D_FF:2 * D_FF] = dupb
        dh2 = _dot(dgateb, re[0][...]) + _dot(dupb, re[1][...])
        r, xhat = _rms_stats(x1_t)
        dxn, dgain = _rms_bwd(dh2, xhat, r, re[3][...])
        dx1 = dy + dxn
        ro[3][...] = dx1
        ro[4][...] = _bf(dx1)
        po[1][0] = jnp.sum(dgain, axis=0, keepdims=True)

    outs = [(D_FF, BF16), (2 * D_FF, BF16), (D_MODEL, BF16), (D_MODEL, F32), (D_MODEL, BF16)]
    return _row_call(body, tm=256, row_ins=[h2, x1, target], res_ins=[wg_t, wu_t, wd, g2], row_outs=outs,
                     part_outs=(128, D_MODEL), name="ffn_fwd_bwd")


def _mix_bwd(dx1b, z_a, z_r, ba, br, g_r, o_ret, wout, wba, wbr, exchange):
    def body(ri, re, ro, po):
        dm = _dot_nt(ri[0][...], re[0][...])
        sa = jax.nn.sigmoid(ri[1][...])
        sr = jax.nn.sigmoid(ri[2][...])
        dbab = _bf(sa * dm)
        dbrb = _bf(sr * dm)
        ro[0][...] = dbab
        ro[1][...] = dbrb
        ro[4][:, RET_V:RET_V + D_MODEL] = _bf(dm * ri[3][...] * (sa * (1.0 - sa)))
        ro[4][:, RET_V + D_MODEL:RET_V + 2 * D_MODEL] = _bf(dm * ri[4][...] * (sr * (1.0 - sr)))
        ro[2][...] = _bf(_dot_nt(dbab, re[1][...]))
        dret = _dot_nt(dbrb, re[2][...])
        for h in range(RET_HEADS):
            cols = slice(h * RET_V_DIM, (h + 1) * RET_V_DIM)
            g = ri[5][:, cols]
            r, rn = _rms_stats(ri[6][:, cols])
            sg = jax.nn.sigmoid(g)
            dret_h = dret[:, cols]
            d_rn = dret_h * (g * sg)
            ro[4][:, cols] = _bf(dret_h * rn * (sg * (1.0 + g * (1.0 - sg))))
            ro[3][:, cols] = r * (d_rn - rn * jnp.mean(d_rn * rn, axis=-1, keepdims=True))

    outs = [(D_MODEL, BF16), (D_MODEL, BF16), (ATT_Q, BF16), (RET_V, F32), (RET_V + 2 * D_MODEL, BF16)]
    return _row_call(body, tm=256, row_ins=[dx1b, z_a, z_r, ba, br, g_r, o_ret], res_ins=[wout, wba, wbr],
                     row_outs=outs, name="mix_bwd", exchange=exchange)


def _proj_bwd(d_pieces, x, dx1, w_in_t, g1, exchange):
    widths = [p.shape[1] for p in d_pieces]
    groups = [(sum(widths[:k]), w) for k, w in enumerate(widths)]
    n_p = len(groups)

    def body(ri, re, ro, po):
        dh = None
        for k, (off, w) in enumerate(groups):
            term = _dot(ri[k][...], re[0][off:off + w, :])
            dh = term if dh is None else dh + term
        r, xhat = _rms_stats(ri[n_p][...])
        dxn, dgain = _rms_bwd(dh, xhat, r, re[1][...])
        ro[0][...] = ri[n_p + 1][...] + dxn
        po[0][0] = jnp.sum(dgain, axis=0, keepdims=True)

    return _row_call(body, tm=512, row_ins=[*d_pieces, x, dx1], res_ins=[w_in_t, g1], row_outs=[(D_MODEL, F32)],
                     part_outs=(D_MODEL,), name="proj_bwd", exchange=exchange)


def _dw(a, b, *, tm, place, buf, name, exchange=None):
    t, m = a.shape
    n = b.shape[1]
    tk = min(2048, t)
    n_i, n_k = m // tm, t // tk
    fresh = isinstance(buf, jax.ShapeDtypeStruct)
    n_copies = len(place(0))

    def kern(a_ref, b_ref, *rest):
        out_ref, acc, sems = rest[-3:]
        i, k = pl.program_id(0), pl.program_id(1)
        part = _dot_tn(a_ref[...], b_ref[...])

        @pl.when(k == 0)
        def _():
            acc[i] = part

        @pl.when(k > 0)
        def _():
            acc[i] += part

        def copies(tile):
            return [pltpu.make_async_copy(acc.at[tile, pl.ds(r0, rows), :], out_ref.at[idx], sems.at[tile * n_copies + c])
                    for c, (r0, rows, idx) in enumerate(place(tile))]

        for tile in range(n_i):
            @pl.when((i == tile) & (k == n_k - 1))
            def _(tile=tile):
                for cp in copies(tile):
                    cp.start()

        @pl.when((i == n_i - 1) & (k == n_k - 1))
        def _():
            for tile in range(n_i):
                for cp in copies(tile):
                    cp.wait()

    in_specs = [pl.BlockSpec((tk, tm), lambda i, k: (k, i)), pl.BlockSpec((tk, n), lambda i, k: (k, 0))]
    shape = buf if fresh else jax.ShapeDtypeStruct(buf.shape, buf.dtype)
    return _pallas(
        kern, grid=(n_i, n_k), in_specs=in_specs + ([] if fresh else [_ANY]), out_specs=[_ANY], out_shape=[shape],
        scratch=[pltpu.VMEM((n_i, tm, n), F32), pltpu.SemaphoreType.DMA((n_i * n_copies,))],
        args=[a, b] + ([] if fresh else [buf]), aliases=None if fresh else {2: 0}, name=name, exchange=exchange)


def _heads_to_lanes(x3):
    return jnp.concatenate([x3[g] for g in range(GROUP)], axis=1)


def _lanes_to_heads(xt):
    return jnp.concatenate([xt[:, g * BLOCK:(g + 1) * BLOCK] for g in range(GROUP)], axis=0)


def _attn_queries(kvh, q_ref, gq_col):
    cols = slice(kvh * GROUP * HEAD_DIM, (kvh + 1) * GROUP * HEAD_DIM)
    q3 = q_ref[:, cols].T.reshape(GROUP, HEAD_DIM, BLOCK)
    rq = lax.rsqrt(jnp.mean(q3 * q3, axis=1, keepdims=True) + EPS)
    qhat = q3 * rq
    return qhat, rq, _heads_to_lanes(_bf(qhat * (gq_col * (HEAD_DIM ** -0.5))))


def _from_prev():
    j = lax.broadcasted_iota(jnp.int32, (BLOCK, GROUP * BLOCK), 0)
    i = lax.broadcasted_iota(jnp.int32, (BLOCK, GROUP * BLOCK), 1) & (BLOCK - 1)
    return j > i


def _attn_probs(n, kvh, qts, kvp_ref, kvc_ref, gk, sink_ref):
    kcols = slice(kvh * HEAD_DIM, (kvh + 1) * HEAD_DIM)
    k = jnp.concatenate([kvp_ref[:, kcols], kvc_ref[:, kcols]], axis=0)
    rk, khat = _rms_stats(k)
    st = _dot(_bf(khat * gk), qts)
    f = jnp.where(_from_prev(), jnp.where(n > 0, st[0:BLOCK], -1e30), st[BLOCK:2 * BLOCK])
    sink = jnp.concatenate([jnp.broadcast_to(sink_ref[0:1, kvh * GROUP + g:kvh * GROUP + g + 1], (1, BLOCK))
                            for g in range(GROUP)], axis=1)
    m = jnp.maximum(jnp.max(f, axis=0, keepdims=True), sink)
    e = jnp.exp(f - m)
    es = jnp.exp(sink - m)
    inv = 1.0 / (jnp.sum(e, axis=0, keepdims=True) + es)
    return e * inv, es * inv


def _unfold(from_prev, xf):
    return _bf(jnp.concatenate([jnp.where(from_prev, xf, 0.0), jnp.where(from_prev, 0.0, xf)], axis=0))


def _attn_fwd(q_a, kv_a, gq_col, gk, sinks):
    t = q_a.shape[0]
    nb = t // BLOCK

    def kern(q_ref, kvp_ref, kvc_ref, gq_ref, gk_ref, sink_ref, o_ref, pf_ref, ps_ref):
        n = pl.program_id(0)
        kvt = jnp.concatenate([kvp_ref[...].T, kvc_ref[...].T], axis=1)
        for kvh in range(N_KV_HEADS):
            _, _, qts = _attn_queries(kvh, q_ref, gq_ref[...])
            pf, psink = _attn_probs(n, kvh, qts, kvp_ref, kvc_ref, gk_ref[...], sink_ref)
            lanes = slice(kvh * GROUP * BLOCK, (kvh + 1) * GROUP * BLOCK)
            pf_ref[:, lanes] = pf
            ps_ref[:, lanes] = psink
            vt = _bf(kvt[ATT_KV + kvh * HEAD_DIM:ATT_KV + (kvh + 1) * HEAD_DIM, :])
            out_t = _dot(vt, _unfold(_from_prev(), pf))
            cols = slice(kvh * GROUP * HEAD_DIM, (kvh + 1) * GROUP * HEAD_DIM)
            o_ref[:, cols] = _bf(_lanes_to_heads(out_t).T)

    small = lambda a: pl.BlockSpec(a.shape, lambda n: (0, 0))
    folded = N_KV_HEADS * GROUP * BLOCK
    return dict(
        kern=kern,
        in_specs=[pl.BlockSpec((BLOCK, ATT_Q), lambda n: (n, 0)),
                  pl.BlockSpec((BLOCK, 2 * ATT_KV), lambda n: (jnp.maximum(n - 1, 0), 0)),
                  pl.BlockSpec((BLOCK, 2 * ATT_KV), lambda n: (n, 0)),
                  small(gq_col), small(gk), small(sinks)],
        out_specs=[pl.BlockSpec((BLOCK, ATT_Q), lambda n: (n, 0)), pl.BlockSpec((BLOCK, folded), lambda n: (n, 0)),
                   pl.BlockSpec((None, 1, folded), lambda n: (n, 0, 0))],
        out_shape=[jax.ShapeDtypeStruct((t, ATT_Q), BF16), jax.ShapeDtypeStruct((t, folded), F32),
                   jax.ShapeDtypeStruct((nb, 1, folded), F32)],
        scratch=[], args=[q_a, kv_a, kv_a, gq_col, gk, sinks])


def _attn_bwd(q_a, kv_a, d_attn, probs, sink_probs, gq_col, gk, gk_col):
    t = q_a.shape[0]
    nb = t // BLOCK

    def kern(q_ref, kvp_ref, kvc_ref, do_ref, pf_ref, ps_ref, gq_ref, gk_ref, gkc_ref,
             dq_ref, dkv_ref, dgq_ref, dgk_ref, dsink_ref, band_k, band_v, carry_k, carry_v):
        n = pl.program_id(0)
        gq_v = gq_ref[...]
        gk_v = gk_ref[...]

        @pl.when(n == 0)
        def _():
            carry_k[...] = jnp.zeros_like(carry_k)
            carry_v[...] = jnp.zeros_like(carry_v)
            dgq_ref[...] = jnp.zeros_like(dgq_ref)
            dgk_ref[...] = jnp.zeros_like(dgk_ref)
            dsink_ref[...] = jnp.zeros_like(dsink_ref)

        @pl.when(n == nb)
        def _():
            band_k[...] = jnp.zeros_like(band_k)
            band_v[...] = jnp.zeros_like(band_v)

        @pl.when(n < nb)
        def _():
            lane16 = lax.broadcasted_iota(jnp.int32, (1, N_Q_HEADS), 1)
            dsink = jnp.zeros((1, N_Q_HEADS), F32)
            dgq = jnp.zeros((HEAD_DIM, 1), F32)
            gk_col = gkc_ref[...]
            kvt = jnp.concatenate([kvp_ref[...].T, kvc_ref[...].T], axis=1)
            from_prev = _from_prev()
            for kvh in range(N_KV_HEADS):
                qhat, rq, qts = _attn_queries(kvh, q_ref, gq_v)
                lanes = slice(kvh * GROUP * BLOCK, (kvh + 1) * GROUP * BLOCK)
                pf = pf_ref[:, lanes]
                cols = slice(kvh * GROUP * HEAD_DIM, (kvh + 1) * GROUP * HEAD_DIM)
                vcols = slice(ATT_KV + kvh * HEAD_DIM, ATT_KV + (kvh + 1) * HEAD_DIM)
                dot = _heads_to_lanes(_bf(do_ref[:, cols].astype(F32).T.reshape(GROUP, HEAD_DIM, BLOCK)))
                vb = _bf(jnp.concatenate([kvp_ref[:, vcols], kvc_ref[:, vcols]], axis=0))
                dpt = _dot(vb, dot)
                dpf = jnp.where(from_prev, dpt[0:BLOCK], dpt[BLOCK:2 * BLOCK])
                delta = jnp.sum(pf * dpf, axis=0, keepdims=True)
                dst = _unfold(from_prev, pf * (dpf - delta))
                dsk = ps_ref[:, lanes] * delta
                for g in range(GROUP):
                    tot = jnp.sum(dsk[:, g * BLOCK:(g + 1) * BLOCK], axis=1, keepdims=True)
                    dsink = dsink - jnp.where(lane16 == kvh * GROUP + g, tot, 0.0)
                kt = kvt[kvh * HEAD_DIM:(kvh + 1) * HEAD_DIM, :]
                knt = _bf(kt * lax.rsqrt(jnp.mean(kt * kt, axis=0, keepdims=True) + EPS) * gk_col)
                dqn = (_dot(knt, dst) * (HEAD_DIM ** -0.5))
                band_k[kvh] = _dot_nt(dst, qts)
                band_v[kvh] = _dot_nt(_unfold(from_prev, pf), dot)
                dqn3 = _lanes_to_heads(dqn).reshape(GROUP, HEAD_DIM, BLOCK)
                u = dqn3 * gq_v
                dq3 = rq * (u - qhat * jnp.mean(u * qhat, axis=1, keepdims=True))
                dgq = dgq + jnp.sum(jnp.sum(dqn3 * qhat, axis=0), axis=1, keepdims=True)
                dq_ref[:, cols] = _bf(dq3.reshape(GROUP * HEAD_DIM, BLOCK).T)
            dsink_ref[...] += dsink
            dgq_ref[...] += dgq

        dgk = jnp.zeros((1, HEAD_DIM), F32)
        for kvh in range(N_KV_HEADS):
            kcols = slice(kvh * HEAD_DIM, (kvh + 1) * HEAD_DIM)
            vcols = slice(ATT_KV + kvh * HEAD_DIM, ATT_KV + (kvh + 1) * HEAD_DIM)
            dkn = carry_k[kvh] + band_k[kvh, 0:BLOCK, :]
            dv = carry_v[kvh] + band_v[kvh, 0:BLOCK, :]
            rk, khat = _rms_stats(kvp_ref[:, kcols])
            dk, dgain = _rms_bwd(dkn, khat, rk, gk_v)
            dgk = dgk + jnp.sum(dgain, axis=0, keepdims=True)
            dkv_ref[:, kcols] = _bf(dk)
            dkv_ref[:, vcols] = _bf(dv)
            carry_k[kvh] = band_k[kvh, BLOCK:2 * BLOCK, :]
            carry_v[kvh] = band_v[kvh, BLOCK:2 * BLOCK, :]
        dgk_ref[...] += dgk

    small = lambda a: pl.BlockSpec(a.shape, lambda n: (0, 0))
    last = nb - 1
    return dict(
        kern=kern,
        in_specs=[pl.BlockSpec((BLOCK, ATT_Q), lambda n: (jnp.minimum(n, last), 0)),
                  pl.BlockSpec((BLOCK, 2 * ATT_KV), lambda n: (jnp.maximum(n - 1, 0), 0)),
                  pl.BlockSpec((BLOCK, 2 * ATT_KV), lambda n: (jnp.minimum(n, last), 0)),
                  pl.BlockSpec((BLOCK, ATT_Q), lambda n: (jnp.minimum(n, last), 0)),
                  pl.BlockSpec((BLOCK, probs.shape[1]), lambda n: (jnp.minimum(n, last), 0)),
                  pl.BlockSpec((None, 1, probs.shape[1]), lambda n: (jnp.minimum(n, last), 0, 0)),
                  small(gq_col), small(gk), small(gk_col)],
        out_specs=[pl.BlockSpec((BLOCK, ATT_Q), lambda n: (jnp.minimum(n, last), 0)),
                   pl.BlockSpec((BLOCK, 2 * ATT_KV), lambda n: (jnp.maximum(n - 1, 0), 0)),
                   pl.BlockSpec((HEAD_DIM, 1), lambda n: (0, 0)),
                   pl.BlockSpec((1, HEAD_DIM), lambda n: (0, 0)),
                   pl.BlockSpec((1, N_Q_HEADS), lambda n: (0, 0))],
        out_shape=[jax.ShapeDtypeStruct((t, ATT_Q), BF16), jax.ShapeDtypeStruct((t, 2 * ATT_KV), BF16),
                   jax.ShapeDtypeStruct((HEAD_DIM, 1), F32), jax.ShapeDtypeStruct((1, HEAD_DIM), F32),
                   jax.ShapeDtypeStruct((1, N_Q_HEADS), F32)],
        scratch=[pltpu.VMEM((N_KV_HEADS, 2 * BLOCK, HEAD_DIM), F32),
                 pltpu.VMEM((N_KV_HEADS, 2 * BLOCK, HEAD_DIM), F32),
                 pltpu.VMEM((N_KV_HEADS, BLOCK, HEAD_DIM), F32),
                 pltpu.VMEM((N_KV_HEADS, BLOCK, HEAD_DIM), F32)],
        args=[q_a, kv_a, kv_a, d_attn, probs, sink_probs, gq_col, gk, gk_col])


def _ret_tables(t, exchange):
    theta = 1.0 / (RET_ROT_BASE ** jnp.linspace(0.0, 1.0, RET_QK_DIM // 2, dtype=F32))
    theta2 = jnp.repeat(theta, 2)[None, :]
    sign = jnp.tile(jnp.array([-1.0, 1.0], F32), RET_QK_DIM // 2)[None, :]

    def kern(theta_ref, sign_ref, cos_ref, sin_ref):
        first = pl.program_id(0) * RET_CHUNK
        pos = (first + lax.broadcasted_iota(jnp.int32, (RET_CHUNK, RET_QK_DIM), 0)).astype(F32)
        ang = pos * theta_ref[...]
        cos_ref[...] = jnp.cos(ang)
        sin_ref[...] = jnp.sin(ang) * sign_ref[...]

    row = pl.BlockSpec((1, RET_QK_DIM), lambda n: (0, 0))
    blk = pl.BlockSpec((RET_CHUNK, RET_QK_DIM), lambda n: (n, 0))
    cos, sin_s, *got = _pallas(kern, grid=(t // RET_CHUNK,), in_specs=[row, row], out_specs=[blk, blk],
                               out_shape=[jax.ShapeDtypeStruct((t, RET_QK_DIM), F32)] * 2, args=[theta2, sign],
                               name="position_tables", exchange=exchange)
    log_gamma = jnp.log(1.0 - 2.0 ** (-5.0 - jnp.arange(RET_HEADS, dtype=F32)))
    i = jnp.arange(RET_CHUNK, dtype=F32)
    diff = i[:, None] - i[None, :]
    causal = diff >= 0
    decay = jnp.where(causal[None], jnp.exp(jnp.where(causal, diff, 0.0)[None] * log_gamma[:, None, None]), 0.0)
    xi = jnp.exp((i + 1.0)[None, :] * log_gamma[:, None])[:, :, None]
    zeta = jnp.exp((RET_CHUNK - 1.0 - i)[None, :] * log_gamma[:, None])[:, :, None]
    gch = jnp.broadcast_to(jnp.exp(RET_CHUNK * log_gamma)[:, None, None], (RET_HEADS, 1, 128))
    return (cos, sin_s, decay, xi, zeta, gch), got


def _swap_pairs(x):
    lane = lax.broadcasted_iota(jnp.int32, x.shape, 1)
    return jnp.where((lane & 1) == 0, pltpu.roll(x, RET_QK_DIM - 1, 1), pltpu.roll(x, 1, 1))


def _rotate(x, cos, sin_s):
    return x * cos + _swap_pairs(x) * sin_s


def _rotate_bwd(dy, cos, sin_s):
    return dy * cos + _swap_pairs(dy * sin_s)


def _ret_specs(order):
    qk = pl.BlockSpec((RET_CHUNK, RET_QK), lambda j: (order(j), 0))
    v = pl.BlockSpec((RET_CHUNK, RET_V), lambda j: (order(j), 0))
    dec = pl.BlockSpec((RET_HEADS, RET_CHUNK, RET_CHUNK), lambda j: (0, 0, 0))
    col = pl.BlockSpec((RET_HEADS, RET_CHUNK, 1), lambda j: (0, 0, 0))
    gch = pl.BlockSpec((RET_HEADS, 1, 128), lambda j: (0, 0, 0))
    st = pl.BlockSpec((RET_HEADS, None, RET_QK_DIM, RET_V_DIM), lambda j: (0, order(j), 0, 0))
    pos = pl.BlockSpec((RET_CHUNK, RET_QK_DIM), lambda j: (order(j), 0))
    return qk, v, dec, col, gch, st, pos


def _ret_fwd(q_r, k_r, v_r, g_r, tables):
    t = q_r.shape[0]
    nc = t // RET_CHUNK
    cos, sin_s, decay, xi, zeta, gch = tables

    def kern(q_ref, k_ref, v_ref, g_ref, cos_ref, sin_ref, dec_ref, xi_ref, zeta_ref, gch_ref,
             o_ref, ret_ref, st_ref, state):
        @pl.when(pl.program_id(0) == 0)
        def _():
            state[...] = jnp.zeros_like(state)

        cos_t = cos_ref[...]
        sin_t = sin_ref[...]
        for h in range(RET_HEADS):
            qc = slice(h * RET_QK_DIM, (h + 1) * RET_QK_DIM)
            vc = slice(h * RET_V_DIM, (h + 1) * RET_V_DIM)
            qs = _bf(_rotate(q_ref[:, qc], cos_t, sin_t))
            ks = _rotate(k_ref[:, qc] * (RET_QK_DIM ** -0.5), cos_t, sin_t)
            vb = v_ref[:, vc]
            s_old = state[h]
            sb = _bf(s_old)
            st_ref[h] = sb
            inner = _dot_nt(qs, _bf(ks)) * dec_ref[h]
            out = _dot(_bf(inner), vb) + _dot(qs, sb) * xi_ref[h]
            state[h] = gch_ref[h, :, 0:1] * s_old + _dot_tn(_bf(ks * zeta_ref[h]), vb)
            o_ref[:, vc] = out
            r, rn = _rms_stats(out)
            g = g_ref[:, vc]
            ret_ref[:, vc] = _bf(g * jax.nn.sigmoid(g) * rn)

    qk, v, dec, col, gsp, st, pos = _ret_specs(lambda j: j)
    return dict(
        kern=kern,
        in_specs=[qk, qk, v, v, pos, pos, dec, col, col, gsp],
        out_specs=[v, v, st],
        out_shape=[jax.ShapeDtypeStruct((t, RET_V), F32), jax.ShapeDtypeStruct((t, RET_V), BF16),
                   jax.ShapeDtypeStruct((RET_HEADS, nc, RET_QK_DIM, RET_V_DIM), BF16)],
        scratch=[pltpu.VMEM((RET_HEADS, RET_QK_DIM, RET_V_DIM), F32)],
        args=[q_r, k_r, v_r, g_r, cos, sin_s, decay, xi, zeta, gch])


def _ret_bwd(q_r, k_r, v_r, d_o, states, tables):
    t = q_r.shape[0]
    nc = t // RET_CHUNK
    cos, sin_s, decay, xi, zeta, gch = tables

    def kern(q_ref, k_ref, v_ref, do_ref, st_ref, cos_ref, sin_ref, dec_ref, xi_ref, zeta_ref, gch_ref,
             d_ref, dstate):
        dq_ref, dk_ref = d_ref.at[:, 0:RET_QK], d_ref.at[:, RET_QK:2 * RET_QK]
        dv_ref = d_ref.at[:, 2 * RET_QK:2 * RET_QK + RET_V]

        @pl.when(pl.program_id(0) == 0)
        def _():
            dstate[...] = jnp.zeros_like(dstate)

        @pl.when(pl.program_id(0) < nc)
        def _():
            cos_t = cos_ref[...]
            sin_t = sin_ref[...]
            scale = RET_QK_DIM ** -0.5
            for h in range(RET_HEADS):
                qc = slice(h * RET_QK_DIM, (h + 1) * RET_QK_DIM)
                vc = slice(h * RET_V_DIM, (h + 1) * RET_V_DIM)
                qs = _bf(_rotate(q_ref[:, qc], cos_t, sin_t))
                ks = _rotate(k_ref[:, qc] * scale, cos_t, sin_t)
                ksb = _bf(ks)
                vb = v_ref[:, vc]
                d_o_t = do_ref[:, vc]
                dob = _bf(d_o_t)
                doxb = _bf(d_o_t * xi_ref[h])
                dec = dec_ref[h]
                ds_old = dstate[h]
                dsb = _bf(ds_old)
                pb = _bf(_dot_nt(qs, ksb) * dec)
                dpb = _bf(_dot_nt(dob, vb) * dec)
                dqs = _dot(dpb, ksb) + _dot_nt(doxb, st_ref[h])
                dks = _dot_tn(dpb, qs) + _dot_nt(vb, dsb) * zeta_ref[h]
                dv_ref[:, vc] = _bf(_dot_tn(pb, dob) + _dot(_bf(ks * zeta_ref[h]), dsb))
                dstate[h] = gch_ref[h, :, 0:1] * ds_old + _dot_tn(qs, doxb)
                dq_ref[:, qc] = _bf(_rotate_bwd(dqs, cos_t, sin_t))
                dk_ref[:, qc] = _bf(_rotate_bwd(dks, cos_t, sin_t) * scale)

    backwards = lambda j: jnp.maximum(nc - 1 - j, 0)
    qk, v, dec, col, gsp, st, pos = _ret_specs(backwards)
    return dict(
        kern=kern,
        in_specs=[qk, qk, v, v, st, pos, pos, dec, col, col, gsp],
        out_specs=[pl.BlockSpec((RET_CHUNK, 2 * RET_QK + RET_V), lambda j: (backwards(j), 0))],
        out_shape=[jax.ShapeDtypeStruct((t, 2 * RET_QK + RET_V), BF16)],
        scratch=[pltpu.VMEM((RET_HEADS, RET_QK_DIM, RET_V_DIM), F32)],
        args=[q_r, k_r, v_r, d_o, states, cos, sin_s, decay, xi, zeta, gch])


def _position():
    return lax.axis_index("x"), lax.axis_index("y"), lax.axis_index("c")


def _gather_exchange(owns, forward_at):
    n = len(owns)

    def copies(ins, outs, send_sems, recv_sems, staging):
        x, y, c = _position()
        sibling = (x, y, 1 - c)
        chips = [(1 - x, y), (x, 1 - y), (1 - x, 1 - y)]
        my_chip = 2 * x + y

        def slab(a, chip, hf):
            half = owns[a].shape[0] // 2
            return outs[a].at[chip, pl.ds(hf * half, half), :]

        def copy(k, src, dst, to):
            return pltpu.make_async_remote_copy(src_ref=src, dst_ref=dst, send_sem=send_sems.at[k],
                                                recv_sem=recv_sems.at[k], device_id=to, device_id_type=MESH)

        first, passed, from_sibling, stage_in, stage_out = [], [], [], [], []
        for a in range(n):
            half = owns[a].shape[0] // 2
            for k, (cx, cy) in enumerate(chips):
                first.append(copy(6 * a + k, ins[a].at[pl.ds(c * half, half), :], slab(a, my_chip, c), (cx, cy, c)))
                landed = slab(a, 2 * cx + cy, c)
                passed.append(copy(6 * a + 3 + k, landed, landed, sibling))
                theirs = slab(a, 2 * cx + cy, 1 - c)
                from_sibling.append(copy(6 * a + 3 + k, theirs, theirs, sibling))
            stage_in.append(pltpu.make_async_copy(ins[a], staging[a], send_sems.at[6 * n + a]))
            stage_out.append(pltpu.make_async_copy(staging[a], outs[a].at[my_chip], recv_sems.at[6 * n + a]))
        return first, passed, from_sibling, stage_in, stage_out

    def start(*args):
        first, _, _, stage_in, _ = copies(*args)
        for cp in first + stage_in:
            cp.start()

    def forward(*args):
        first, passed, _, stage_in, stage_out = copies(*args)
        for staged, cp in zip(stage_in, stage_out):
            staged.wait()
            cp.start()
        for arrived, cp in zip(first, passed):
            arrived.wait_recv()
            cp.start()

    def finish(*args):
        first, passed, from_sibling, _, stage_out = copies(*args)
        for cp in from_sibling:
            cp.wait_recv()
        for cp in first + passed:
            cp.wait_send()
        for cp in stage_out:
            cp.wait()

    outs = [jax.ShapeDtypeStruct((N_CHIPS, *a.shape), a.dtype) for a in owns]
    return _Exchange(owns, outs, 7 * n, [(0.0, start), (forward_at, forward), (1.0, finish)],
                     staging=[pltpu.VMEM(a.shape, a.dtype) for a in owns])


def _symmetric_exchange(ins, outs, plan):
    n_sems = len(plan([None] * len(ins), [None] * len(outs), 0, 0, 0, dry=True))

    def copies(in_refs, out_refs, send_sems, recv_sems, staging):
        x, y, c = _position()
        return [pltpu.make_async_remote_copy(src_ref=src, dst_ref=dst, send_sem=send_sems.at[k],
                                             recv_sem=recv_sems.at[k], device_id=dev, device_id_type=MESH)
                for k, (src, dst, dev) in enumerate(plan(in_refs, out_refs, x, y, c, dry=False))]

    def start(*args):
        for cp in copies(*args):
            cp.start()

    def finish(*args):
        for cp in copies(*args):
            cp.wait()

    return _Exchange(ins, outs, n_sems, [(0.0, start), (1.0, finish)])


def _pair_exchange(gs):
    def plan(in_refs, out_refs, x, y, c, dry):
        out = []
        for a, g in enumerate(gs):
            half = g.shape[1] // 2
            for k in range(N_CHIPS):
                out.append(None if dry else (in_refs[a].at[k, pl.ds((1 - c) * half, half), :], out_refs[a].at[k],
                                             (x, y, 1 - c)))
        return out

    outs = [jax.ShapeDtypeStruct((g.shape[0], g.shape[1] // 2, g.shape[2]), g.dtype) for g in gs]
    return _symmetric_exchange(gs, outs, plan)


def _pair_sum(g, from_sibling, c_arr, *, tile, name):
    n, rows, width = g.shape
    tiles = (rows // 2) // tile

    def kern(c_ref, g_ref, s_ref, o_ref):
        o_ref[...] = _bf(g_ref[...] + s_ref[...])

    return pl.pallas_call(
        kern,
        grid_spec=pltpu.PrefetchScalarGridSpec(
            num_scalar_prefetch=1, grid=(n, tiles),
            in_specs=[pl.BlockSpec((None, tile, width), lambda k, i, c: (k, c[0] * tiles + i, 0)),
                      pl.BlockSpec((None, tile, width), lambda k, i, c: (k, i, 0))],
            out_specs=pl.BlockSpec((None, tile, width), lambda k, i, c: (k, i, 0))),
        out_shape=jax.ShapeDtypeStruct((n, rows // 2, width), BF16), name=name,
        compiler_params=_params(("parallel", "parallel")),
    )(c_arr, g, from_sibling)


def _scatter_to_owners(hsums):
    def plan(in_refs, out_refs, x, y, c, dry):
        out = []
        for a in range(len(hsums)):
            for k, (cx, cy) in enumerate([(1 - x, y), (x, 1 - y), (1 - x, 1 - y)]):
                out.append(None if dry else (in_refs[a].at[2 * cx + cy], out_refs[a].at[k], (cx, cy, c)))
        return out

    outs = [jax.ShapeDtypeStruct((3, *h.shape[1:]), h.dtype) for h in hsums]
    return _symmetric_exchange(hsums, outs, plan)


def _sum_chips(hsum, parts, chip_arr, *, tile, name):
    n, half, width = parts.shape

    def kern(chip_ref, h_ref, p_ref, o_ref):
        acc = h_ref[...].astype(F32)
        for k in range(n):
            acc = acc + p_ref[k].astype(F32)
        o_ref[...] = acc

    return pl.pallas_call(
        kern,
        grid_spec=pltpu.PrefetchScalarGridSpec(
            num_scalar_prefetch=1, grid=(half // tile,),
            in_specs=[pl.BlockSpec((None, tile, width), lambda i, chip: (chip[0], i, 0)),
                      pl.BlockSpec((n, tile, width), lambda i, chip: (0, i, 0))],
            out_specs=pl.BlockSpec((tile, width), lambda i, chip: (i, 0))),
        out_shape=jax.ShapeDtypeStruct((half, width), F32), name=name,
        compiler_params=_params(("parallel",)),
    )(chip_arr, hsum, parts)


def _share_halves(fhalves):
    def plan(in_refs, out_refs, x, y, c, dry):
        return [None if dry else (in_refs[a], out_refs[a], (x, y, 1 - c)) for a in range(len(fhalves))]

    return _symmetric_exchange(fhalves, [jax.ShapeDtypeStruct(f.shape, f.dtype) for f in fhalves], plan)


def _adamw_math(w, g, m, v):
    m = ADAM_B1 * m + (1.0 - ADAM_B1) * g
    v = ADAM_B2 * v + (1.0 - ADAM_B2) * (g * g)
    m_hat = m / (1.0 - ADAM_B1 ** ADAM_STEP)
    v_hat = v / (1.0 - ADAM_B2 ** ADAM_STEP)
    delta = -ADAM_LR * (m_hat / (jnp.sqrt(v_hat) + ADAM_EPS) + ADAM_WD * w)
    return delta, m, v


def _adamw(mats, g_mine, g_other, c_arr, *, tile, name):
    width = g_mine.shape[1]
    tiles_per_half = g_mine.shape[0] // tile
    n_tiles = [w.shape[0] // tile for w, _, _, _ in mats]
    n_mats = len(mats)

    def kern(c_ref, *refs):
        ins, outs = refs[:5 * n_mats], refs[5 * n_mats:]
        for j, (_, _, _, row_off) in enumerate(mats):
            w_ref, gm_ref, go_ref, m_ref, v_ref = ins[5 * j:5 * j + 5]
            i = jnp.minimum(pl.program_id(0), n_tiles[j] - 1)
            in_my_half = ((row_off // tile + i) // tiles_per_half) == c_ref[0]
            g = jnp.where(in_my_half, gm_ref[...], go_ref[...])
            d, nm, nv = _adamw_math(w_ref[...], g, m_ref[...], v_ref[...])
            for out_ref, val in zip(outs[4 * j:4 * j + 4], (g, d, nm, nv)):
                out_ref[...] = val

    in_specs, out_specs, out_shape, args = [], [], [], []
    for (w, m, v, row_off), nt in zip(mats, n_tiles):
        full = pl.BlockSpec((tile, width), lambda i, c, nt=nt: (jnp.minimum(i, nt - 1), 0))

        def half(mine, nt=nt, first=row_off // tile):
            def index(i, c):
                pos = first + jnp.minimum(i, nt - 1)
                used = ((pos // tiles_per_half) == c[0]) == mine
                return (jnp.where(used, pos % tiles_per_half, 0), 0)
            return pl.BlockSpec((tile, width), index)

        in_specs += [full, half(True), half(False), full, full]
        out_specs += [full] * 4
        out_shape += [jax.ShapeDtypeStruct(w.shape, F32)] * 4
        args += [w, g_mine, g_other, m, v]
    outs = pl.pallas_call(
        kern,
        grid_spec=pltpu.PrefetchScalarGridSpec(num_scalar_prefetch=1, grid=(max(n_tiles),), in_specs=in_specs,
                                               out_specs=out_specs),
        out_shape=out_shape, name=name, compiler_params=_params(("arbitrary",)),
    )(c_arr, *args)
    return [outs[4 * j:4 * j + 4] for j in range(n_mats)]


def _small_step(partials, params):
    slots = ((0, 0, D_MODEL), (1, 0, D_MODEL), (2, 0, HEAD_DIM), (2, 128, HEAD_DIM), (2, 256, N_Q_HEADS))
    loss_slot = (2, 384, 128)

    def body(*refs):
        loss_ref, dg1_ref, dg2_ref, dgq_ref, dgk_ref, dsink_ref = refs[:6]
        p_refs, out_refs = refs[6:21], refs[21:42]
        mine, gathered, send_sems, recv_sems = refs[42:]
        x, y, c = _position()
        me = 4 * x + 2 * y + c
        mine[...] = jnp.zeros_like(mine)
        for (row, lane, n), val in zip(slots + (loss_slot,), (
                jnp.sum(dg1_ref[...], axis=0, keepdims=True), jnp.sum(dg2_ref[...], axis=0, keepdims=True),
                dgq_ref[...], dgk_ref[...], dsink_ref[...], jnp.sum(loss_ref[...], axis=0, keepdims=True))):
            mine[row:row + 1, lane:lane + n] = val
        copies = []
        for k in range(1, N_DEV):
            flip = (k >> 2) & 1, (k >> 1) & 1, k & 1
            to = (x ^ flip[0], y ^ flip[1], c ^ flip[2])
            cp = pltpu.make_async_remote_copy(
                src_ref=mine, dst_ref=gathered.at[me], send_sem=send_sems.at[k - 1], recv_sem=recv_sems.at[k - 1],
                device_id=to, device_id_type=MESH)
            cp.start()
            copies.append(cp)
        gathered[me] = mine[...]
        for k in range(1, N_DEV):
            flip = (k >> 2) & 1, (k >> 1) & 1, k & 1
            src = 4 * (x ^ flip[0]) + 2 * (y ^ flip[1]) + (c ^ flip[2])
            pltpu.make_async_remote_copy(
                src_ref=mine, dst_ref=gathered.at[src], send_sem=send_sems.at[k - 1], recv_sem=recv_sems.at[k - 1],
                device_id=(x, y, c), device_id_type=MESH).wait_recv()
        for cp in copies:
            cp.wait_send()
        total = gathered[0]
        for k in range(1, N_DEV):
            total = total + gathered[k]
        row, lane, n = loss_slot
        out_refs[0][...] = total[row:row + 1, lane:lane + n]
        for i, (row, lane, n) in enumerate(slots):
            g = total[row:row + 1, lane:lane + n]
            d, nm, nv = _adamw_math(p_refs[i][...], g, p_refs[5 + i][...], p_refs[10 + i][...])
            for kind, val in enumerate((g, d, nm, nv)):
                out_refs[1 + 5 * kind + i][...] = val

    vm = pl.BlockSpec(memory_space=pltpu.VMEM)
    shapes = [jax.ShapeDtypeStruct((1, 128), F32)] + [jax.ShapeDtypeStruct((1, n), F32) for _, _, n in slots] * 4
    return pl.pallas_call(
        body, in_specs=[vm] * 21, out_specs=[vm] * 21, out_shape=shapes,
        scratch_shapes=[pltpu.VMEM((SMALL_ROWS, D_MODEL), F32), pltpu.VMEM((N_DEV, SMALL_ROWS, D_MODEL), F32),
                        pltpu.SemaphoreType.DMA((N_DEV - 1,)), pltpu.SemaphoreType.DMA((N_DEV - 1,))],
        name="small_step",
    )(*partials, *params)


def kernel(x, norm_mix_gain, w_in, q_norm_gain, k_norm_gain, attn_sinks, w_branch_attn, w_branch_ret, w_out, norm_ffn_gain, w_ffn_gate, w_ffn_up, w_ffn_down, loss_target, m_norm_mix_gain, m_w_in, m_q_norm_gain, m_k_norm_gain, m_attn_sinks, m_w_branch_attn, m_w_branch_ret, m_w_out, m_norm_ffn_gain, m_w_ffn_gate, m_w_ffn_up, m_w_ffn_down, v_norm_mix_gain, v_w_in, v_q_norm_gain, v_k_norm_gain, v_attn_sinks, v_w_branch_attn, v_w_branch_ret, v_w_out, v_norm_ffn_gain, v_w_ffn_gate, v_w_ffn_up, v_w_ffn_down):
    my_chip = 2 * lax.axis_index("x") + lax.axis_index("y")
    c_arr = lax.axis_index("c").astype(jnp.int32).reshape(1)
    chip_arr = my_chip.astype(jnp.int32).reshape(1)
    x_t, target = x[0], loss_target[0]
    g1, g2, gq, gk, sinks = norm_mix_gain, norm_ffn_gain, q_norm_gain, k_norm_gain, attn_sinks

    tr = lambda a: jnp.transpose(a[0])
    own_w_in = _bf(tr(w_in))
    own_rest = [_bf(a) for a in (tr(w_ffn_gate), tr(w_ffn_up), w_ffn_down[0], w_branch_attn[0], w_branch_ret[0],
                                 w_out[0])]
    tables, (got_w_in,) = _ret_tables(x_t.shape[0], _gather_exchange([own_w_in], 0.9))
    w_in_t = got_w_in.reshape(D_IN, D_MODEL)
    h1, q_a, kv_a, q_r, k_r, v_r, g_r, z_a, z_r, *got_rest = _proj_fwd(x_t, g1, w_in_t, _gather_exchange(own_rest, 0.8))
    wg_t, wu_t, wd, wba, wbr, wout = [got.reshape(-1, D_MODEL) for got in got_rest]

    gq_col, gk_col = gq.reshape(HEAD_DIM, 1), gk.reshape(HEAD_DIM, 1)
    attn, probs, sink_probs, o_ret, ret, states = _fused(
        [_attn_fwd(q_a, kv_a, gq_col, gk, sinks), _ret_fwd(q_r, k_r, v_r, g_r, tables)],
        grid=(x_t.shape[0] // BLOCK,), name="mixers_fwd")
    ba, br, merged, x1, h2 = _mix_fwd(attn, ret, z_a, z_r, x_t, wba, wbr, wout, g2)
    act, dgate_dup, dyb, dx1, dx1b, loss_p, dg2_p = _ffn_fwd_bwd(h2, x1, target, wg_t, wu_t, wd, g2)

    def pairs(row0, rows):
        return lambda i: [(h * rows, rows, (2 * i + h, pl.ds(row0, rows), slice(None))) for h in range(2)]

    f_block = jax.ShapeDtypeStruct((N_CHIPS, 3 * FF_SH, D_MODEL), F32)
    f_block, = _dw(dgate_dup, h2, tm=2 * FF_SH, buf=f_block, name="dw_gate_up",
                   place=lambda i: pairs(0, FF_SH)(i) if i < 2 else pairs(FF_SH, FF_SH)(i - 2))
    f_block, = _dw(act, dyb, tm=2 * FF_SH, place=pairs(2 * FF_SH, FF_SH), buf=f_block, name="dw_down")
    (dba, dbr, d_attn, d_o, d_gz, sib_ffn) = _mix_bwd(
        dx1b, z_a, z_r, ba, br, g_r, o_ret, wout, wba, wbr, _pair_exchange([f_block]))
    f_sum = _pair_sum(f_block, sib_ffn, c_arr, tile=528, name="pair_sum_ffn")

    def quarters(row0, rows):
        return lambda i: [(k * rows, rows, (k, pl.ds(row0, rows), slice(None))) for k in range(N_CHIPS)]

    m_block = jax.ShapeDtypeStruct((N_CHIPS, D_MODEL, D_MODEL), F32)
    m_block, = _dw(attn, dba, tm=ATT_Q, place=quarters(0, 256), buf=m_block, name="dw_ba")
    m_block, = _dw(ret, dbr, tm=D_MODEL, place=pairs(256, 512), buf=m_block, name="dw_br")
    m_block, = _dw(merged, dx1b, tm=D_MODEL, place=quarters(768, 256), buf=m_block, name="dw_out")

    def w_in_rows(off, w):
        tm = min(w, D_MODEL)
        return dict(tm=tm, place=lambda i: [(0, tm, (pl.ds(off + i * tm, tm), slice(None)))])

    w_block = jax.ShapeDtypeStruct((D_IN, D_MODEL), F32)
    w_block, sib_mix = _dw(d_gz, h1, buf=w_block, name="dw_in_gz", exchange=_pair_exchange([m_block]),
                           **w_in_rows(P_GR[0], d_gz.shape[1]))
    m_sum = _pair_sum(m_block, sib_mix, c_arr, tile=256, name="pair_sum_mix")

    (dq_a, dkv_a, dgq, dgk, dsinks, d_ret, got_ffn_sums, got_mix_sums) = _fused(
        [_attn_bwd(q_a, kv_a, d_attn, probs, sink_probs, gq_col, gk, gk_col),
         _ret_bwd(q_r, k_r, v_r, d_o, states, tables)],
        grid=(x_t.shape[0] // BLOCK + 1,), name="mixers_bwd", exchange=_scatter_to_owners([f_sum, m_sum]))
    dgq = dgq.reshape(1, HEAD_DIM)
    ffn_half = _sum_chips(f_sum, got_ffn_sums, chip_arr, tile=528, name="sum_chips_ffn")
    mix_half = _sum_chips(m_sum, got_mix_sums, chip_arr, tile=256, name="sum_chips_mix")
    w_block, = _dw(d_ret, h1, buf=w_block, name="dw_in_ret", **w_in_rows(P_QR[0], d_ret.shape[1]))
    w_block, ffn_other, mix_other = _dw(dq_a, h1, buf=w_block, name="dw_in_q",
                                        exchange=_share_halves([ffn_half, mix_half]), **w_in_rows(*P_QA))
    w_block, = _dw(dkv_a, h1, buf=w_block, name="dw_in_kv", **w_in_rows(*P_KVA))

    w_block = w_block.reshape(N_CHIPS, W_IN_SH, D_MODEL)
    sib_w_in, = _run_exchange(_pair_exchange([w_block]), "pair_exchange_w_in")
    w_sum = _pair_sum(w_block, sib_w_in, c_arr, tile=592, name="pair_sum_w_in")
    d_pieces = [dq_a, dkv_a, d_ret, d_gz]
    grad_x, dg1_p, got_w_in_sums = _proj_bwd(d_pieces, x_t, dx1, w_in_t, g1, _scatter_to_owners([w_sum]))
    w_in_half = _sum_chips(w_sum, got_w_in_sums, chip_arr, tile=592, name="sum_chips_w_in")
    w_in_other, = _run_exchange(_share_halves([w_in_half]), "share_halves_w_in")

    def update(name, g_half, g_other, tile, mats):
        outs = _adamw([tuple(tr(a) if t else a[0] for a in wmv) + (off,) for _, *wmv, off, t in mats],
                      g_half, g_other, c_arr, tile=tile, name=f"adamw_{name}")
        return {key: [jnp.transpose(o) if t else o for o in res] for (key, _, _, _, _, t), res in zip(mats, outs)}

    big = {
        **update("w_in", w_in_half, w_in_other, 592, [("w_in", w_in, m_w_in, v_w_in, 0, True)]),
        **update("ffn", ffn_half, ffn_other, 176, [
            ("wg", w_ffn_gate, m_w_ffn_gate, v_w_ffn_gate, 0, True),
            ("wu", w_ffn_up, m_w_ffn_up, v_w_ffn_up, FF_SH, True),
            ("wd", w_ffn_down, m_w_ffn_down, v_w_ffn_down, 2 * FF_SH, False)]),
        **update("mix", mix_half, mix_other, 128, [
            ("wba", w_branch_attn, m_w_branch_attn, v_w_branch_attn, 0, False),
            ("wbr", w_branch_ret, m_w_branch_ret, v_w_branch_ret, 256, False),
            ("wout", w_out, m_w_out, v_w_out, 768, False)])}

    loss_row, *small = _small_step(
        [loss_p.reshape(-1, 128), dg1_p.reshape(-1, D_MODEL), dg2_p.reshape(-1, D_MODEL), dgq, dgk, dsinks],
        [norm_mix_gain, norm_ffn_gain, q_norm_gain, k_norm_gain, attn_sinks,
         m_norm_mix_gain, m_norm_ffn_gain, m_q_norm_gain, m_k_norm_gain, m_attn_sinks,
         v_norm_mix_gain, v_norm_ffn_gain, v_q_norm_gain, v_k_norm_gain, v_attn_sinks])
    loss = loss_row[0, 0]

    def leaves(i):
        b = [big[n][i][None] for n in ("w_in", "wba", "wbr", "wout", "wg", "wu", "wd")]
        s1, s2, sq, sk, ss = small[5 * i:5 * i + 5]
        return [s1, b[0], sq, sk, ss, b[1], b[2], b[3], s2, b[4], b[5], b[6]]

    return (loss, grad_x[None], *leaves(0), *leaves(1), *leaves(2), *leaves(3))
```

```python
import jax
import jax.numpy as jnp
from jax import lax
from jax.experimental import pallas as pl
from jax.experimental.pallas import tpu as pltpu

F32 = jnp.float32
BF16 = jnp.bfloat16
MESH = pl.DeviceIdType.MESH

D_MODEL = 1024
EPS = 1e-6
HEAD_DIM = 64
N_Q_HEADS = 16
N_KV_HEADS = 2
GROUP = 8
BLOCK = 128
RET_HEADS = 4
RET_QK_DIM = 256
RET_V_DIM = 512
RET_CHUNK = 128
RET_ROT_BASE = 10000.0
D_FF = 2816
ATT_Q = N_Q_HEADS * HEAD_DIM
ATT_KV = N_KV_HEADS * HEAD_DIM
RET_QK = RET_HEADS * RET_QK_DIM
RET_V = RET_HEADS * RET_V_DIM
D_IN = 9472
ADAM_LR = 0.001
ADAM_B1 = 0.9
ADAM_B2 = 0.999
ADAM_EPS = 1e-08
ADAM_WD = 0.01
ADAM_STEP = 10

N_CHIPS = 4
N_DEV = 8
VMEM_LIMIT_BYTES = 60 * 1024 * 1024

P_QA = (0, 1024)
P_KVA = (1024, 256)
P_QR = (1280, 1024)
P_KR = (2304, 1024)
P_VR = (3328, 2048)
P_GR = (5376, 2048)
P_ZA = (7424, 1024)
P_ZR = (8448, 1024)

W_IN_SH = D_IN // N_CHIPS
FF_SH = D_FF // N_CHIPS

SMALL_ROWS = 8


def _dot(a, b):
    return jnp.dot(a, b, preferred_element_type=F32)


def _dot_nt(a, b):
    return lax.dot_general(a, b, (((1,), (1,)), ((), ())), preferred_element_type=F32)


def _dot_tn(a, b):
    return lax.dot_general(a, b, (((0,), (0,)), ((), ())), preferred_element_type=F32)


def _bf(x):
    return x.astype(BF16)


def _rms_stats(x):
    r = lax.rsqrt(jnp.mean(x * x, axis=-1, keepdims=True) + EPS)
    return r, x * r


def _rms_bwd(dy, xhat, r, gain):
    u = dy * gain
    dx = r * (u - xhat * jnp.mean(u * xhat, axis=-1, keepdims=True))
    return dx, dy * xhat


def _params(sem):
    return pltpu.CompilerParams(dimension_semantics=sem, vmem_limit_bytes=VMEM_LIMIT_BYTES)


_ANY = pl.BlockSpec(memory_space=pl.ANY)


class _Exchange:
    def __init__(self, ins, outs, n_sems, phases, staging=(), result_sources=()):
        self.ins, self.outs, self.n_sems, self.phases = list(ins), list(outs), n_sems, list(phases)
        self.staging, self.result_sources = list(staging), list(result_sources)


def _pallas(kern, *, grid, in_specs, out_specs, out_shape, args, name, scratch=(), exchange=None, aliases=None):
    aliases = aliases or {}
    if exchange is None:
        return pl.pallas_call(
            kern, grid=grid, in_specs=in_specs, out_specs=out_specs, out_shape=out_shape, name=name,
            scratch_shapes=list(scratch), input_output_aliases=aliases,
            compiler_params=_params(("arbitrary",) * len(grid)))(*args)
    n_in, n_out, n_sc = len(in_specs), len(out_specs), len(scratch)
    n_xi, n_xo, n_xs = len(exchange.ins), len(exchange.outs), len(exchange.staging)
    n_steps = 1
    for g in grid:
        n_steps *= g

    def wrapped(*refs):
        ins, refs = refs[:n_in], refs[n_in:]
        x_ins, refs = refs[:n_xi], refs[n_xi:]
        outs, refs = refs[:n_out], refs[n_out:]
        x_outs, refs = refs[:n_xo], refs[n_xo:]
        scr, refs = refs[:n_sc], refs[n_sc:]
        staging, (send_sems, recv_sems) = refs[:n_xs], refs[n_xs:]
        x_ins = list(x_ins) + [outs[j] for j in exchange.result_sources]
        step = pl.program_id(0)
        for d in range(1, len(grid)):
            step = step * grid[d] + pl.program_id(d)
        for frac, fn in exchange.phases:
            at = min(int(frac * n_steps), n_steps - 1)

            @pl.when(step == at)
            def _(fn=fn):
                fn(x_ins, x_outs, send_sems, recv_sems, staging)

        kern(*ins, *outs, *scr)

    sems = [pltpu.SemaphoreType.DMA((exchange.n_sems,)), pltpu.SemaphoreType.DMA((exchange.n_sems,))]
    return pl.pallas_call(
        wrapped, grid=grid, in_specs=list(in_specs) + [_ANY] * n_xi, out_specs=list(out_specs) + [_ANY] * n_xo,
        out_shape=list(out_shape) + exchange.outs, name=name,
        scratch_shapes=list(scratch) + exchange.staging + sems, input_output_aliases=aliases,
        compiler_params=_params(("arbitrary",) * len(grid)))(*args, *exchange.ins)


def _run_exchange(exchange, name):
    def body(*refs):
        n_i, n_o = len(exchange.ins), len(exchange.outs)
        staging, (send_sems, recv_sems) = refs[n_i + n_o:-2], refs[-2:]
        for _, fn in exchange.phases:
            fn(refs[:n_i], refs[n_i:n_i + n_o], send_sems, recv_sems, staging)

    sems = [pltpu.SemaphoreType.DMA((exchange.n_sems,)), pltpu.SemaphoreType.DMA((exchange.n_sems,))]
    return pl.pallas_call(body, in_specs=[_ANY] * len(exchange.ins), out_specs=[_ANY] * len(exchange.outs),
                          out_shape=exchange.outs, scratch_shapes=exchange.staging + sems, name=name,
                          compiler_params=pltpu.CompilerParams(vmem_limit_bytes=VMEM_LIMIT_BYTES))(*exchange.ins)


def _fused(parts, *, grid, name, exchange=None):
    counts = [(len(p["in_specs"]), len(p["out_specs"]), len(p["scratch"])) for p in parts]
    n_in, n_out = sum(c[0] for c in counts), sum(c[1] for c in counts)

    def kern(*refs):
        ins, outs, scr = refs[:n_in], refs[n_in:n_in + n_out], refs[n_in + n_out:]
        i0 = o0 = s0 = 0
        for p, (ni, no, ns) in zip(parts, counts):
            p["kern"](*ins[i0:i0 + ni], *outs[o0:o0 + no], *scr[s0:s0 + ns])
            i0, o0, s0 = i0 + ni, o0 + no, s0 + ns

    cat = lambda key: [a for p in parts for a in p[key]]
    return _pallas(kern, grid=grid, in_specs=cat("in_specs"), out_specs=cat("out_specs"), out_shape=cat("out_shape"),
                   scratch=cat("scratch"), args=cat("args"), name=name, exchange=exchange)


def _row_call(body, *, tm, row_ins, res_ins, row_outs, part_outs=(), name, exchange=None):
    t = row_ins[0].shape[0]
    n_tiles = t // tm
    in_specs = [pl.BlockSpec((tm, a.shape[1]), lambda i: (i, 0)) for a in row_ins]
    in_specs += [pl.BlockSpec(a.shape, lambda i: (0, 0), pipeline_mode=pl.Buffered(1)) for a in res_ins]
    out_shape = [jax.ShapeDtypeStruct((t, w), dt) for (w, dt) in row_outs]
    out_shape += [jax.ShapeDtypeStruct((n_tiles, 1, w), F32) for w in part_outs]
    out_specs = [pl.BlockSpec((tm, w), lambda i: (i, 0)) for (w, _) in row_outs]
    out_specs += [pl.BlockSpec((1, 1, w), lambda i: (i, 0, 0)) for w in part_outs]
    n_ri, n_re, n_ro = len(row_ins), len(res_ins), len(row_outs)

    def kern(*refs):
        body(refs[:n_ri], refs[n_ri:n_ri + n_re], refs[n_ri + n_re:n_ri + n_re + n_ro], refs[n_ri + n_re + n_ro:])

    return _pallas(kern, grid=(n_tiles,), in_specs=in_specs, out_specs=out_specs, out_shape=out_shape,
                   args=[*row_ins, *res_ins], name=name, exchange=exchange)


def _proj_fwd(x, g1, w_in_t, exchange):
    pieces = ((P_QA, F32), (P_KVA, F32), (P_QR, F32), (P_KR, F32), (P_VR, BF16), (P_GR, F32), (P_ZA, F32), (P_ZR, F32))

    def body(ri, re, ro, po):
        x_t = ri[0][...]
        r, xhat = _rms_stats(x_t)
        hb = _bf(xhat * re[0][...])
        ro[0][...] = hb
        for k, ((off, w), dt) in enumerate(pieces):
            ro[1 + k][...] = _dot_nt(hb, re[1][off:off + w, :]).astype(dt)

    outs = [(D_MODEL, BF16)] + [(w, dt) for ((_, w), dt) in pieces]
    return _row_call(body, tm=256, row_ins=[x], res_ins=[g1, w_in_t], row_outs=outs, name="proj_fwd",
                     exchange=exchange)


def _mix_fwd(attn, ret, z_a, z_r, x, wba, wbr, wout, g2):
    def body(ri, re, ro, po):
        ba = _dot(ri[0][...], re[0][...])
        br = _dot(ri[1][...], re[1][...])
        m = jax.nn.sigmoid(ri[2][...]) * ba + jax.nn.sigmoid(ri[3][...]) * br
        mb = _bf(m)
        x1 = ri[4][...] + _dot(mb, re[2][...])
        r, xhat = _rms_stats(x1)
        ro[0][...] = ba
        ro[1][...] = br
        ro[2][...] = mb
        ro[3][...] = x1
        ro[4][...] = _bf(xhat * re[3][...])

    outs = [(D_MODEL, F32), (D_MODEL, F32), (D_MODEL, BF16), (D_MODEL, F32), (D_MODEL, BF16)]
    return _row_call(body, tm=512, row_ins=[attn, ret, z_a, z_r, x], res_ins=[wba, wbr, wout, g2], row_outs=outs,
                     name="mix_fwd")


def _ffn_fwd_bwd(h2, x1, target, wg_t, wu_t, wd, g2):
    def body(ri, re, ro, po):
        h2_t = ri[0][...]
        x1_t = ri[1][...]
        gate = _dot_nt(h2_t, re[0][...])
        up = _dot_nt(h2_t, re[1][...])
        sg = jax.nn.sigmoid(gate)
        sl = gate * sg
        actb = _bf(sl * up)
        ro[0][...] = actb
        y = x1_t + _dot(actb, re[2][...])
        e = y - ri[2][...]
        po[0][0] = jnp.broadcast_to(0.5 * jnp.sum(jnp.sum(e * e, axis=1, keepdims=True), axis=0, keepdims=True)
                                    * (1.0 / D_MODEL), (1, 128))
        dy = e * (1.0 / D_MODEL)
        dyb = _bf(dy)
        ro[3][...] = dyb
        dact = _dot_nt(dyb, re[2][...])
        dupb = _bf(dact * sl)
        dgateb = _bf(dact * up * (sg * (1.0 + gate * (1.0 - sg))))
        ro[1][...] = dgateb
        ro[2][...] = dupb
        dh2 = _dot(dgateb, re[0][...]) + _dot(dupb, re[1][...])
        r, xhat = _rms_stats(x1_t)
        dxn, dgain = _rms_bwd(dh2, xhat, r, re[3][...])
        dx1 = dy + dxn
        ro[4][...] = dx1
        ro[5][...] = _bf(dx1)
        po[1][0] = jnp.sum(dgain, axis=0, keepdims=True)

    outs = [(D_FF, BF16), (D_FF, BF16), (D_FF, BF16), (D_MODEL, BF16), (D_MODEL, F32), (D_MODEL, BF16)]
    return _row_call(body, tm=256, row_ins=[h2, x1, target], res_ins=[wg_t, wu_t, wd, g2], row_outs=outs,
                     part_outs=(128, D_MODEL), name="ffn_fwd_bwd")


def _mix_bwd(dx1b, z_a, z_r, ba, br, g_r, o_ret, wout, wba, wbr, exchange):
    def body(ri, re, ro, po):
        dm = _dot_nt(ri[0][...], re[0][...])
        sa = jax.nn.sigmoid(ri[1][...])
        sr = jax.nn.sigmoid(ri[2][...])
        dbab = _bf(sa * dm)
        dbrb = _bf(sr * dm)
        ro[0][...] = dbab
        ro[1][...] = dbrb
        ro[4][:, RET_V:RET_V + D_MODEL] = _bf(dm * ri[3][...] * (sa * (1.0 - sa)))
        ro[4][:, RET_V + D_MODEL:RET_V + 2 * D_MODEL] = _bf(dm * ri[4][...] * (sr * (1.0 - sr)))
        ro[2][...] = _bf(_dot_nt(dbab, re[1][...]))
        dret = _dot_nt(dbrb, re[2][...])
        for h in range(RET_HEADS):
            cols = slice(h * RET_V_DIM, (h + 1) * RET_V_DIM)
            g = ri[5][:, cols]
            r, rn = _rms_stats(ri[6][:, cols])
            sg = jax.nn.sigmoid(g)
            dret_h = dret[:, cols]
            d_rn = dret_h * (g * sg)
            ro[4][:, cols] = _bf(dret_h * rn * (sg * (1.0 + g * (1.0 - sg))))
            ro[3][:, cols] = r * (d_rn - rn * jnp.mean(d_rn * rn, axis=-1, keepdims=True))

    outs = [(D_MODEL, BF16), (D_MODEL, BF16), (ATT_Q, BF16), (RET_V, F32), (RET_V + 2 * D_MODEL, BF16)]
    return _row_call(body, tm=256, row_ins=[dx1b, z_a, z_r, ba, br, g_r, o_ret], res_ins=[wout, wba, wbr],
                     row_outs=outs, name="mix_bwd", exchange=exchange)


def _proj_bwd(d_pieces, x, dx1, w_in_t, g1, exchange):
    widths = [p.shape[1] for p in d_pieces]
    groups = [(sum(widths[:k]), w) for k, w in enumerate(widths)]
    n_p = len(groups)

    def body(ri, re, ro, po):
        dh = None
        for k, (off, w) in enumerate(groups):
            term = _dot(ri[k][...], re[0][off:off + w, :])
            dh = term if dh is None else dh + term
        r, xhat = _rms_stats(ri[n_p][...])
        dxn, dgain = _rms_bwd(dh, xhat, r, re[1][...])
        ro[0][...] = ri[n_p + 1][...] + dxn
        po[0][0] = jnp.sum(dgain, axis=0, keepdims=True)

    return _row_call(body, tm=512, row_ins=[*d_pieces, x, dx1], res_ins=[w_in_t, g1], row_outs=[(D_MODEL, F32)],
                     part_outs=(D_MODEL,), name="proj_bwd", exchange=exchange)


def _dw(a, b, *, tm, place, buf, name, exchange=None):
    t, m = a.shape
    n = b.shape[1]
    tk = min(2048, t)
    n_i, n_k = m // tm, t // tk
    fresh = isinstance(buf, jax.ShapeDtypeStruct)
    n_copies = len(place(0))

    def kern(a_ref, b_ref, *rest):
        out_ref, acc, sems = rest[-3:]
        i, k = pl.program_id(0), pl.program_id(1)
        part = _dot_tn(a_ref[...], b_ref[...])

        @pl.when(k == 0)
        def _():
            acc[i] = part

        @pl.when(k > 0)
        def _():
            acc[i] += part

        def copies(tile):
            return [pltpu.make_async_copy(acc.at[tile, pl.ds(r0, rows), :], out_ref.at[idx], sems.at[tile * n_copies + c])
                    for c, (r0, rows, idx) in enumerate(place(tile))]

        for tile in range(n_i):
            @pl.when((i == tile) & (k == n_k - 1))
            def _(tile=tile):
                for cp in copies(tile):
                    cp.start()

        @pl.when((i == n_i - 1) & (k == n_k - 1))
        def _():
            for tile in range(n_i):
                for cp in copies(tile):
                    cp.wait()

    in_specs = [pl.BlockSpec((tk, tm), lambda i, k: (k, i)), pl.BlockSpec((tk, n), lambda i, k: (k, 0))]
    shape = buf if fresh else jax.ShapeDtypeStruct(buf.shape, buf.dtype)
    return _pallas(
        kern, grid=(n_i, n_k), in_specs=in_specs + ([] if fresh else [_ANY]), out_specs=[_ANY], out_shape=[shape],
        scratch=[pltpu.VMEM((n_i, tm, n), F32), pltpu.SemaphoreType.DMA((n_i * n_copies,))],
        args=[a, b] + ([] if fresh else [buf]), aliases=None if fresh else {2: 0}, name=name, exchange=exchange)


def _heads_to_lanes(x3):
    return jnp.concatenate([x3[g] for g in range(GROUP)], axis=1)


def _lanes_to_heads(xt):
    return jnp.concatenate([xt[:, g * BLOCK:(g + 1) * BLOCK] for g in range(GROUP)], axis=0)


def _attn_queries(kvh, q_ref, gq_col):
    cols = slice(kvh * GROUP * HEAD_DIM, (kvh + 1) * GROUP * HEAD_DIM)
    q3 = q_ref[:, cols].T.reshape(GROUP, HEAD_DIM, BLOCK)
    rq = lax.rsqrt(jnp.mean(q3 * q3, axis=1, keepdims=True) + EPS)
    qhat = q3 * rq
    return qhat, rq, _heads_to_lanes(_bf(qhat * (gq_col * (HEAD_DIM ** -0.5))))


def _from_prev():
    j = lax.broadcasted_iota(jnp.int32, (BLOCK, GROUP * BLOCK), 0)
    i = lax.broadcasted_iota(jnp.int32, (BLOCK, GROUP * BLOCK), 1) & (BLOCK - 1)
    return j > i


def _attn_probs(n, kvh, qts, kvp_ref, kvc_ref, gk, sink_ref):
    kcols = slice(kvh * HEAD_DIM, (kvh + 1) * HEAD_DIM)
    k = jnp.concatenate([kvp_ref[:, kcols], kvc_ref[:, kcols]], axis=0)
    rk, khat = _rms_stats(k)
    st = _dot(_bf(khat * gk), qts)
    f = jnp.where(_from_prev(), jnp.where(n > 0, st[0:BLOCK], -1e30), st[BLOCK:2 * BLOCK])
    sink = jnp.concatenate([jnp.broadcast_to(sink_ref[0:1, kvh * GROUP + g:kvh * GROUP + g + 1], (1, BLOCK))
                            for g in range(GROUP)], axis=1)
    m = jnp.maximum(jnp.max(f, axis=0, keepdims=True), sink)
    e = jnp.exp(f - m)
    es = jnp.exp(sink - m)
    inv = 1.0 / (jnp.sum(e, axis=0, keepdims=True) + es)
    return e * inv, es * inv


def _unfold(from_prev, xf):
    return _bf(jnp.concatenate([jnp.where(from_prev, xf, 0.0), jnp.where(from_prev, 0.0, xf)], axis=0))


def _attn_fwd(q_a, kv_a, gq_col, gk, sinks):
    t = q_a.shape[0]
    nb = t // BLOCK

    def kern(q_ref, kvp_ref, kvc_ref, gq_ref, gk_ref, sink_ref, o_ref, pf_ref, ps_ref):
        n = pl.program_id(0)
        kvt = jnp.concatenate([kvp_ref[...].T, kvc_ref[...].T], axis=1)
        for kvh in range(N_KV_HEADS):
            _, _, qts = _attn_queries(kvh, q_ref, gq_ref[...])
            pf, psink = _attn_probs(n, kvh, qts, kvp_ref, kvc_ref, gk_ref[...], sink_ref)
            lanes = slice(kvh * GROUP * BLOCK, (kvh + 1) * GROUP * BLOCK)
            pf_ref[:, lanes] = pf
            ps_ref[:, lanes] = psink
            vt = _bf(kvt[ATT_KV + kvh * HEAD_DIM:ATT_KV + (kvh + 1) * HEAD_DIM, :])
            out_t = _dot(vt, _unfold(_from_prev(), pf))
            cols = slice(kvh * GROUP * HEAD_DIM, (kvh + 1) * GROUP * HEAD_DIM)
            o_ref[:, cols] = _bf(_lanes_to_heads(out_t).T)

    small = lambda a: pl.BlockSpec(a.shape, lambda n: (0, 0))
    folded = N_KV_HEADS * GROUP * BLOCK
    return dict(
        kern=kern,
        in_specs=[pl.BlockSpec((BLOCK, ATT_Q), lambda n: (n, 0)),
                  pl.BlockSpec((BLOCK, 2 * ATT_KV), lambda n: (jnp.maximum(n - 1, 0), 0)),
                  pl.BlockSpec((BLOCK, 2 * ATT_KV), lambda n: (n, 0)),
                  small(gq_col), small(gk), small(sinks)],
        out_specs=[pl.BlockSpec((BLOCK, ATT_Q), lambda n: (n, 0)), pl.BlockSpec((BLOCK, folded), lambda n: (n, 0)),
                   pl.BlockSpec((None, 1, folded), lambda n: (n, 0, 0))],
        out_shape=[jax.ShapeDtypeStruct((t, ATT_Q), BF16), jax.ShapeDtypeStruct((t, folded), F32),
                   jax.ShapeDtypeStruct((nb, 1, folded), F32)],
        scratch=[], args=[q_a, kv_a, kv_a, gq_col, gk, sinks])


def _attn_bwd(q_a, kv_a, d_attn, probs, sink_probs, gq_col, gk, gk_col):
    t = q_a.shape[0]
    nb = t // BLOCK

    def kern(q_ref, kvp_ref, kvc_ref, do_ref, pf_ref, ps_ref, gq_ref, gk_ref, gkc_ref,
             dq_ref, dkv_ref, dgq_ref, dgk_ref, dsink_ref, band_k, band_v, carry_k, carry_v):
        n = pl.program_id(0)
        gq_v = gq_ref[...]
        gk_v = gk_ref[...]

        @pl.when(n == 0)
        def _():
            carry_k[...] = jnp.zeros_like(carry_k)
            carry_v[...] = jnp.zeros_like(carry_v)
            dgq_ref[...] = jnp.zeros_like(dgq_ref)
            dgk_ref[...] = jnp.zeros_like(dgk_ref)
            dsink_ref[...] = jnp.zeros_like(dsink_ref)

        @pl.when(n == nb)
        def _():
            band_k[...] = jnp.zeros_like(band_k)
            band_v[...] = jnp.zeros_like(band_v)

        @pl.when(n < nb)
        def _():
            lane16 = lax.broadcasted_iota(jnp.int32, (1, N_Q_HEADS), 1)
            dsink = jnp.zeros((1, N_Q_HEADS), F32)
            dgq = jnp.zeros((HEAD_DIM, 1), F32)
            gk_col = gkc_ref[...]
            kvt = jnp.concatenate([kvp_ref[...].T, kvc_ref[...].T], axis=1)
            from_prev = _from_prev()
            for kvh in range(N_KV_HEADS):
                qhat, rq, qts = _attn_queries(kvh, q_ref, gq_v)
                lanes = slice(kvh * GROUP * BLOCK, (kvh + 1) * GROUP * BLOCK)
                pf = pf_ref[:, lanes]
                cols = slice(kvh * GROUP * HEAD_DIM, (kvh + 1) * GROUP * HEAD_DIM)
                vcols = slice(ATT_KV + kvh * HEAD_DIM, ATT_KV + (kvh + 1) * HEAD_DIM)
                dot = _heads_to_lanes(_bf(do_ref[:, cols].astype(F32).T.reshape(GROUP, HEAD_DIM, BLOCK)))
                vb = _bf(jnp.concatenate([kvp_ref[:, vcols], kvc_ref[:, vcols]], axis=0))
                dpt = _dot(vb, dot)
                dpf = jnp.where(from_prev, dpt[0:BLOCK], dpt[BLOCK:2 * BLOCK])
                delta = jnp.sum(pf * dpf, axis=0, keepdims=True)
                dst = _unfold(from_prev, pf * (dpf - delta))
                dsk = ps_ref[:, lanes] * delta
                for g in range(GROUP):
                    tot = jnp.sum(dsk[:, g * BLOCK:(g + 1) * BLOCK], axis=1, keepdims=True)
                    dsink = dsink - jnp.where(lane16 == kvh * GROUP + g, tot, 0.0)
                kt = kvt[kvh * HEAD_DIM:(kvh + 1) * HEAD_DIM, :]
                knt = _bf(kt * lax.rsqrt(jnp.mean(kt * kt, axis=0, keepdims=True) + EPS) * gk_col)
                dqn = (_dot(knt, dst) * (HEAD_DIM ** -0.5))
                band_k[kvh] = _dot_nt(dst, qts)
                band_v[kvh] = _dot_nt(_unfold(from_prev, pf), dot)
                dqn3 = _lanes_to_heads(dqn).reshape(GROUP, HEAD_DIM, BLOCK)
                u = dqn3 * gq_v
                dq3 = rq * (u - qhat * jnp.mean(u * qhat, axis=1, keepdims=True))
                dgq = dgq + jnp.sum(jnp.sum(dqn3 * qhat, axis=0), axis=1, keepdims=True)
                dq_ref[:, cols] = _bf(dq3.reshape(GROUP * HEAD_DIM, BLOCK).T)
            dsink_ref[...] += dsink
            dgq_ref[...] += dgq

        dgk = jnp.zeros((1, HEAD_DIM), F32)
        for kvh in range(N_KV_HEADS):
            kcols = slice(kvh * HEAD_DIM, (kvh + 1) * HEAD_DIM)
            vcols = slice(ATT_KV + kvh * HEAD_DIM, ATT_KV + (kvh + 1) * HEAD_DIM)
            dkn = carry_k[kvh] + band_k[kvh, 0:BLOCK, :]
            dv = carry_v[kvh] + band_v[kvh, 0:BLOCK, :]
            rk, khat = _rms_stats(kvp_ref[:, kcols])
            dk, dgain = _rms_bwd(dkn, khat, rk, gk_v)
            dgk = dgk + jnp.sum(dgain, axis=0, keepdims=True)
            dkv_ref[:, kcols] = _bf(dk)
            dkv_ref[:, vcols] = _bf(dv)
            carry_k[kvh] = band_k[kvh, BLOCK:2 * BLOCK, :]
            carry_v[kvh] = band_v[kvh, BLOCK:2 * BLOCK, :]
        dgk_ref[...] += dgk

    small = lambda a: pl.BlockSpec(a.shape, lambda n: (0, 0))
    last = nb - 1
    return dict(
        kern=kern,
        in_specs=[pl.BlockSpec((BLOCK, ATT_Q), lambda n: (jnp.minimum(n, last), 0)),
                  pl.BlockSpec((BLOCK, 2 * ATT_KV), lambda n: (jnp.maximum(n - 1, 0), 0)),
                  pl.BlockSpec((BLOCK, 2 * ATT_KV), lambda n: (jnp.minimum(n, last), 0)),
                  pl.BlockSpec((BLOCK, ATT_Q), lambda n: (jnp.minimum(n, last), 0)),
                  pl.BlockSpec((BLOCK, probs.shape[1]), lambda n: (jnp.minimum(n, last), 0)),
                  pl.BlockSpec((None, 1, probs.shape[1]), lambda n: (jnp.minimum(n, last), 0, 0)),
                  small(gq_col), small(gk), small(gk_col)],
        out_specs=[pl.BlockSpec((BLOCK, ATT_Q), lambda n: (jnp.minimum(n, last), 0)),
                   pl.BlockSpec((BLOCK, 2 * ATT_KV), lambda n: (jnp.maximum(n - 1, 0), 0)),
                   pl.BlockSpec((HEAD_DIM, 1), lambda n: (0, 0)),
                   pl.BlockSpec((1, HEAD_DIM), lambda n: (0, 0)),
                   pl.BlockSpec((1, N_Q_HEADS), lambda n: (0, 0))],
        out_shape=[jax.ShapeDtypeStruct((t, ATT_Q), BF16), jax.ShapeDtypeStruct((t, 2 * ATT_KV), BF16),
                   jax.ShapeDtypeStruct((HEAD_DIM, 1), F32), jax.ShapeDtypeStruct((1, HEAD_DIM), F32),
                   jax.ShapeDtypeStruct((1, N_Q_HEADS), F32)],
        scratch=[pltpu.VMEM((N_KV_HEADS, 2 * BLOCK, HEAD_DIM), F32),
                 pltpu.VMEM((N_KV_HEADS, 2 * BLOCK, HEAD_DIM), F32),
                 pltpu.VMEM((N_KV_HEADS, BLOCK, HEAD_DIM), F32),
                 pltpu.VMEM((N_KV_HEADS, BLOCK, HEAD_DIM), F32)],
        args=[q_a, kv_a, kv_a, d_attn, probs, sink_probs, gq_col, gk, gk_col])


def _ret_tables(t, exchange):
    theta = 1.0 / (RET_ROT_BASE ** jnp.linspace(0.0, 1.0, RET_QK_DIM // 2, dtype=F32))
    theta2 = jnp.repeat(theta, 2)[None, :]
    sign = jnp.tile(jnp.array([-1.0, 1.0], F32), RET_QK_DIM // 2)[None, :]

    def kern(theta_ref, sign_ref, cos_ref, sin_ref):
        first = pl.program_id(0) * RET_CHUNK
        pos = (first + lax.broadcasted_iota(jnp.int32, (RET_CHUNK, RET_QK_DIM), 0)).astype(F32)
        ang = pos * theta_ref[...]
        cos_ref[...] = jnp.cos(ang)
        sin_ref[...] = jnp.sin(ang) * sign_ref[...]

    row = pl.BlockSpec((1, RET_QK_DIM), lambda n: (0, 0))
    blk = pl.BlockSpec((RET_CHUNK, RET_QK_DIM), lambda n: (n, 0))
    cos, sin_s, *got = _pallas(kern, grid=(t // RET_CHUNK,), in_specs=[row, row], out_specs=[blk, blk],
                               out_shape=[jax.ShapeDtypeStruct((t, RET_QK_DIM), F32)] * 2, args=[theta2, sign],
                               name="position_tables", exchange=exchange)
    log_gamma = jnp.log(1.0 - 2.0 ** (-5.0 - jnp.arange(RET_HEADS, dtype=F32)))
    i = jnp.arange(RET_CHUNK, dtype=F32)
    diff = i[:, None] - i[None, :]
    causal = diff >= 0
    decay = jnp.where(causal[None], jnp.exp(jnp.where(causal, diff, 0.0)[None] * log_gamma[:, None, None]), 0.0)
    xi = jnp.exp((i + 1.0)[None, :] * log_gamma[:, None])[:, :, None]
    zeta = jnp.exp((RET_CHUNK - 1.0 - i)[None, :] * log_gamma[:, None])[:, :, None]
    gch = jnp.broadcast_to(jnp.exp(RET_CHUNK * log_gamma)[:, None, None], (RET_HEADS, 1, 128))
    return (cos, sin_s, decay, xi, zeta, gch), got


def _swap_pairs(x):
    lane = lax.broadcasted_iota(jnp.int32, x.shape, 1)
    return jnp.where((lane & 1) == 0, pltpu.roll(x, RET_QK_DIM - 1, 1), pltpu.roll(x, 1, 1))


def _rotate(x, cos, sin_s):
    return x * cos + _swap_pairs(x) * sin_s


def _rotate_bwd(dy, cos, sin_s):
    return dy * cos + _swap_pairs(dy * sin_s)


def _ret_specs(order):
    qk = pl.BlockSpec((RET_CHUNK, RET_QK), lambda j: (order(j), 0))
    v = pl.BlockSpec((RET_CHUNK, RET_V), lambda j: (order(j), 0))
    dec = pl.BlockSpec((RET_HEADS, RET_CHUNK, RET_CHUNK), lambda j: (0, 0, 0))
    col = pl.BlockSpec((RET_HEADS, RET_CHUNK, 1), lambda j: (0, 0, 0))
    gch = pl.BlockSpec((RET_HEADS, 1, 128), lambda j: (0, 0, 0))
    st = pl.BlockSpec((RET_HEADS, None, RET_QK_DIM, RET_V_DIM), lambda j: (0, order(j), 0, 0))
    pos = pl.BlockSpec((RET_CHUNK, RET_QK_DIM), lambda j: (order(j), 0))
    return qk, v, dec, col, gch, st, pos


def _ret_fwd(q_r, k_r, v_r, g_r, tables):
    t = q_r.shape[0]
    nc = t // RET_CHUNK
    cos, sin_s, decay, xi, zeta, gch = tables

    def kern(q_ref, k_ref, v_ref, g_ref, cos_ref, sin_ref, dec_ref, xi_ref, zeta_ref, gch_ref,
             o_ref, ret_ref, st_ref, state):
        @pl.when(pl.program_id(0) == 0)
        def _():
            state[...] = jnp.zeros_like(state)

        cos_t = cos_ref[...]
        sin_t = sin_ref[...]
        for h in range(RET_HEADS):
            qc = slice(h * RET_QK_DIM, (h + 1) * RET_QK_DIM)
            vc = slice(h * RET_V_DIM, (h + 1) * RET_V_DIM)
            qs = _bf(_rotate(q_ref[:, qc], cos_t, sin_t))
            ks = _rotate(k_ref[:, qc] * (RET_QK_DIM ** -0.5), cos_t, sin_t)
            vb = v_ref[:, vc]
            s_old = state[h]
            sb = _bf(s_old)
            st_ref[h] = sb
            inner = _dot_nt(qs, _bf(ks)) * dec_ref[h]
            out = _dot(_bf(inner), vb) + _dot(qs, sb) * xi_ref[h]
            state[h] = gch_ref[h, :, 0:1] * s_old + _dot_tn(_bf(ks * zeta_ref[h]), vb)
            o_ref[:, vc] = out
            r, rn = _rms_stats(out)
            g = g_ref[:, vc]
            ret_ref[:, vc] = _bf(g * jax.nn.sigmoid(g) * rn)

    qk, v, dec, col, gsp, st, pos = _ret_specs(lambda j: j)
    return dict(
        kern=kern,
        in_specs=[qk, qk, v, v, pos, pos, dec, col, col, gsp],
        out_specs=[v, v, st],
        out_shape=[jax.ShapeDtypeStruct((t, RET_V), F32), jax.ShapeDtypeStruct((t, RET_V), BF16),
                   jax.ShapeDtypeStruct((RET_HEADS, nc, RET_QK_DIM, RET_V_DIM), BF16)],
        scratch=[pltpu.VMEM((RET_HEADS, RET_QK_DIM, RET_V_DIM), F32)],
        args=[q_r, k_r, v_r, g_r, cos, sin_s, decay, xi, zeta, gch])


def _ret_bwd(q_r, k_r, v_r, d_o, states, tables):
    t = q_r.shape[0]
    nc = t // RET_CHUNK
    cos, sin_s, decay, xi, zeta, gch = tables

    def kern(q_ref, k_ref, v_ref, do_ref, st_ref, cos_ref, sin_ref, dec_ref, xi_ref, zeta_ref, gch_ref,
             d_ref, dstate):
        dq_ref, dk_ref = d_ref.at[:, 0:RET_QK], d_ref.at[:, RET_QK:2 * RET_QK]
        dv_ref = d_ref.at[:, 2 * RET_QK:2 * RET_QK + RET_V]

        @pl.when(pl.program_id(0) == 0)
        def _():
            dstate[...] = jnp.zeros_like(dstate)

        @pl.when(pl.program_id(0) < nc)
        def _():
            cos_t = cos_ref[...]
            sin_t = sin_ref[...]
            scale = RET_QK_DIM ** -0.5
            for h in range(RET_HEADS):
                qc = slice(h * RET_QK_DIM, (h + 1) * RET_QK_DIM)
                vc = slice(h * RET_V_DIM, (h + 1) * RET_V_DIM)
                qs = _bf(_rotate(q_ref[:, qc], cos_t, sin_t))
                ks = _rotate(k_ref[:, qc] * scale, cos_t, sin_t)
                ksb = _bf(ks)
                vb = v_ref[:, vc]
                d_o_t = do_ref[:, vc]
                dob = _bf(d_o_t)
                doxb = _bf(d_o_t * xi_ref[h])
                dec = dec_ref[h]
                ds_old = dstate[h]
                dsb = _bf(ds_old)
                pb = _bf(_dot_nt(qs, ksb) * dec)
                dpb = _bf(_dot_nt(dob, vb) * dec)
                dqs = _dot(dpb, ksb) + _dot_nt(doxb, st_ref[h])
                dks = _dot_tn(dpb, qs) + _dot_nt(vb, dsb) * zeta_ref[h]
                dv_ref[:, vc] = _bf(_dot_tn(pb, dob) + _dot(_bf(ks * zeta_ref[h]), dsb))
                dstate[h] = gch_ref[h, :, 0:1] * ds_old + _dot_tn(qs, doxb)
                dq_ref[:, qc] = _bf(_rotate_bwd(dqs, cos_t, sin_t))
                dk_ref[:, qc] = _bf(_rotate_bwd(dks, cos_t, sin_t) * scale)

    backwards = lambda j: jnp.maximum(nc - 1 - j, 0)
    qk, v, dec, col, gsp, st, pos = _ret_specs(backwards)
    return dict(
        kern=kern,
        in_specs=[qk, qk, v, v, st, pos, pos, dec, col, col, gsp],
        out_specs=[pl.BlockSpec((RET_CHUNK, 2 * RET_QK + RET_V), lambda j: (backwards(j), 0))],
        out_shape=[jax.ShapeDtypeStruct((t, 2 * RET_QK + RET_V), BF16)],
        scratch=[pltpu.VMEM((RET_HEADS, RET_QK_DIM, RET_V_DIM), F32)],
        args=[q_r, k_r, v_r, d_o, states, cos, sin_s, decay, xi, zeta, gch])


def _position():
    return lax.axis_index("x"), lax.axis_index("y"), lax.axis_index("c")


def _gather_exchange(owns, forward_at):
    n = len(owns)

    def copies(ins, outs, send_sems, recv_sems, staging):
        x, y, c = _position()
        sibling = (x, y, 1 - c)
        chips = [(1 - x, y), (x, 1 - y), (1 - x, 1 - y)]
        my_chip = 2 * x + y

        def slab(a, chip, hf):
            half = owns[a].shape[0] // 2
            return outs[a].at[chip, pl.ds(hf * half, half), :]

        def copy(k, src, dst, to):
            return pltpu.make_async_remote_copy(src_ref=src, dst_ref=dst, send_sem=send_sems.at[k],
                                                recv_sem=recv_sems.at[k], device_id=to, device_id_type=MESH)

        first, passed, from_sibling, stage_in, stage_out = [], [], [], [], []
        for a in range(n):
            half = owns[a].shape[0] // 2
            for k, (cx, cy) in enumerate(chips):
                first.append(copy(6 * a + k, ins[a].at[pl.ds(c * half, half), :], slab(a, my_chip, c), (cx, cy, c)))
                landed = slab(a, 2 * cx + cy, c)
                passed.append(copy(6 * a + 3 + k, landed, landed, sibling))
                theirs = slab(a, 2 * cx + cy, 1 - c)
                from_sibling.append(copy(6 * a + 3 + k, theirs, theirs, sibling))
            stage_in.append(pltpu.make_async_copy(ins[a], staging[a], send_sems.at[6 * n + a]))
            stage_out.append(pltpu.make_async_copy(staging[a], outs[a].at[my_chip], recv_sems.at[6 * n + a]))
        return first, passed, from_sibling, stage_in, stage_out

    def start(*args):
        first, _, _, stage_in, _ = copies(*args)
        for cp in first + stage_in:
            cp.start()

    def forward(*args):
        first, passed, _, stage_in, stage_out = copies(*args)
        for staged, cp in zip(stage_in, stage_out):
            staged.wait()
            cp.start()
        for arrived, cp in zip(first, passed):
            arrived.wait_recv()
            cp.start()

    def finish(*args):
        first, passed, from_sibling, _, stage_out = copies(*args)
        for cp in from_sibling:
            cp.wait_recv()
        for cp in first + passed:
            cp.wait_send()
        for cp in stage_out:
            cp.wait()

    outs = [jax.ShapeDtypeStruct((N_CHIPS, *a.shape), a.dtype) for a in owns]
    return _Exchange(owns, outs, 7 * n, [(0.0, start), (forward_at, forward), (1.0, finish)],
                     staging=[pltpu.VMEM(a.shape, a.dtype) for a in owns])


def _symmetric_exchange(ins, outs, plan, result_sources=()):
    n_sems = len(plan([None] * (len(ins) + len(result_sources)), [None] * len(outs), 0, 0, 0, dry=True))

    def copies(in_refs, out_refs, send_sems, recv_sems, staging):
        x, y, c = _position()
        return [pltpu.make_async_remote_copy(src_ref=src, dst_ref=dst, send_sem=send_sems.at[k],
                                             recv_sem=recv_sems.at[k], device_id=dev, device_id_type=MESH)
                for k, (src, dst, dev) in enumerate(plan(in_refs, out_refs, x, y, c, dry=False))]

    def start(*args):
        for cp in copies(*args):
            cp.start()

    def finish(*args):
        for cp in copies(*args):
            cp.wait()

    return _Exchange(ins, outs, n_sems, [(0.0, start), (1.0, finish)], result_sources=result_sources)


def _pair_exchange(gs):
    def plan(in_refs, out_refs, x, y, c, dry):
        out = []
        for a, g in enumerate(gs):
            half = g.shape[1] // 2
            for k in range(N_CHIPS):
                out.append(None if dry else (in_refs[a].at[k, pl.ds((1 - c) * half, half), :], out_refs[a].at[k],
                                             (x, y, 1 - c)))
        return out

    outs = [jax.ShapeDtypeStruct((g.shape[0], g.shape[1] // 2, g.shape[2]), g.dtype) for g in gs]
    return _symmetric_exchange(gs, outs, plan)


def _pair_exchange_w_in(slabs, w_block=None):
    half = W_IN_SH // 2

    def plan(in_refs, out_refs, x, y, c, dry):
        return [None if dry else (in_refs[0].at[pl.ds(k * W_IN_SH + (1 - c) * half, half), :], out_refs[0].at[j],
                                  (x, y, 1 - c)) for j, k in enumerate(slabs)]

    outs = [jax.ShapeDtypeStruct((len(slabs), half, D_MODEL), F32)]
    if w_block is None:
        return _symmetric_exchange([], outs, plan, result_sources=[0])
    return _symmetric_exchange([w_block], outs, plan)


def _pair_sum(g, from_sibling, c_arr, *, tile, name):
    n, rows, width = g.shape
    tiles = (rows // 2) // tile
    firsts = [sum(s.shape[0] for s in from_sibling[:j]) for j in range(len(from_sibling))]

    def kern(c_ref, g_ref, *rest):
        *s_refs, o_ref = rest
        k = pl.program_id(1)
        s = s_refs[0][...]
        for first, s_ref in zip(firsts[1:], s_refs[1:]):
            s = jnp.where(k >= first, s_ref[...], s)
        o_ref[...] = _bf(g_ref[...] + s)

    def sibling_spec(first, count):
        return pl.BlockSpec((None, tile, width), lambda i, k, c: (jnp.clip(k - first, 0, count - 1), i, 0))

    return pl.pallas_call(
        kern,
        grid_spec=pltpu.PrefetchScalarGridSpec(
            num_scalar_prefetch=1, grid=(tiles, n),
            in_specs=[pl.BlockSpec((None, tile, width), lambda i, k, c: (k, c[0] * tiles + i, 0))]
            + [sibling_spec(first, s.shape[0]) for first, s in zip(firsts, from_sibling)],
            out_specs=pl.BlockSpec((None, tile, width), lambda i, k, c: (k, i, 0))),
        out_shape=jax.ShapeDtypeStruct((n, rows // 2, width), BF16), name=name,
        compiler_params=_params(("parallel", "parallel")),
    )(c_arr, g, *from_sibling)


def _scatter_to_owners(hsums):
    def plan(in_refs, out_refs, x, y, c, dry):
        out = []
        for a in range(len(hsums)):
            for k, (cx, cy) in enumerate([(1 - x, y), (x, 1 - y), (1 - x, 1 - y)]):
                out.append(None if dry else (in_refs[a].at[2 * cx + cy], out_refs[a].at[k], (cx, cy, c)))
        return out

    outs = [jax.ShapeDtypeStruct((3, *h.shape[1:]), h.dtype) for h in hsums]
    return _symmetric_exchange(hsums, outs, plan)


def _sum_chips(hsum, parts, chip_arr, *, tile, name):
    n, half, width = parts.shape

    def kern(chip_ref, h_ref, p_ref, o_ref):
        acc = h_ref[...].astype(F32)
        for k in range(n):
            acc = acc + p_ref[k].astype(F32)
        o_ref[...] = acc

    return pl.pallas_call(
        kern,
        grid_spec=pltpu.PrefetchScalarGridSpec(
            num_scalar_prefetch=1, grid=(half // tile,),
            in_specs=[pl.BlockSpec((None, tile, width), lambda i, chip: (chip[0], i, 0)),
                      pl.BlockSpec((n, tile, width), lambda i, chip: (0, i, 0))],
            out_specs=pl.BlockSpec((tile, width), lambda i, chip: (i, 0))),
        out_shape=jax.ShapeDtypeStruct((half, width), F32), name=name,
        compiler_params=_params(("parallel",)),
    )(chip_arr, hsum, parts)


def _share_halves(fhalves):
    def plan(in_refs, out_refs, x, y, c, dry):
        return [None if dry else (in_refs[a], out_refs[a], (x, y, 1 - c)) for a in range(len(fhalves))]

    return _symmetric_exchange(fhalves, [jax.ShapeDtypeStruct(f.shape, f.dtype) for f in fhalves], plan)


def _adamw_math(w, g, m, v):
    m = ADAM_B1 * m + (1.0 - ADAM_B1) * g
    v = ADAM_B2 * v + (1.0 - ADAM_B2) * (g * g)
    m_hat = m / (1.0 - ADAM_B1 ** ADAM_STEP)
    v_hat = v / (1.0 - ADAM_B2 ** ADAM_STEP)
    delta = -ADAM_LR * (m_hat / (jnp.sqrt(v_hat) + ADAM_EPS) + ADAM_WD * w)
    return delta, m, v


def _adamw(mats, g_mine, g_other, c_arr, *, tile, name):
    width = g_mine.shape[1]
    tiles_per_half = g_mine.shape[0] // tile
    n_tiles = [w.shape[0] // tile for w, _, _, _ in mats]
    n_mats = len(mats)

    def kern(c_ref, *refs):
        ins, outs = refs[:5 * n_mats], refs[5 * n_mats:]
        for j, (_, _, _, row_off) in enumerate(mats):
            w_ref, gm_ref, go_ref, m_ref, v_ref = ins[5 * j:5 * j + 5]
            i = jnp.minimum(pl.program_id(0), n_tiles[j] - 1)
            in_my_half = ((row_off // tile + i) // tiles_per_half) == c_ref[0]
            g = jnp.where(in_my_half, gm_ref[...], go_ref[...])
            d, nm, nv = _adamw_math(w_ref[...], g, m_ref[...], v_ref[...])
            for out_ref, val in zip(outs[4 * j:4 * j + 4], (g, d, nm, nv)):
                out_ref[...] = val

    in_specs, out_specs, out_shape, args = [], [], [], []
    for (w, m, v, row_off), nt in zip(mats, n_tiles):
        full = pl.BlockSpec((tile, width), lambda i, c, nt=nt: (jnp.minimum(i, nt - 1), 0))

        def half(mine, nt=nt, first=row_off // tile):
            def index(i, c):
                pos = first + jnp.minimum(i, nt - 1)
                used = ((pos // tiles_per_half) == c[0]) == mine
                return (jnp.where(used, pos % tiles_per_half, 0), 0)
            return pl.BlockSpec((tile, width), index)

        in_specs += [full, half(True), half(False), full, full]
        out_specs += [full] * 4
        out_shape += [jax.ShapeDtypeStruct(w.shape, F32)] * 4
        args += [w, g_mine, g_other, m, v]
    outs = pl.pallas_call(
        kern,
        grid_spec=pltpu.PrefetchScalarGridSpec(num_scalar_prefetch=1, grid=(max(n_tiles),), in_specs=in_specs,
                                               out_specs=out_specs),
        out_shape=out_shape, name=name, compiler_params=_params(("arbitrary",)),
    )(c_arr, *args)
    return [outs[4 * j:4 * j + 4] for j in range(n_mats)]


def _small_step(partials, params):
    slots = ((0, 0, D_MODEL), (1, 0, D_MODEL), (2, 0, HEAD_DIM), (2, 128, HEAD_DIM), (2, 256, N_Q_HEADS))
    loss_slot = (2, 384, 128)

    def body(*refs):
        loss_ref, dg1_ref, dg2_ref, dgq_ref, dgk_ref, dsink_ref = refs[:6]
        p_refs, out_refs = refs[6:21], refs[21:42]
        mine, gathered, send_sems, recv_sems = refs[42:]
        x, y, c = _position()
        me = 4 * x + 2 * y + c
        mine[...] = jnp.zeros_like(mine)
        for (row, lane, n), val in zip(slots + (loss_slot,), (
                jnp.sum(dg1_ref[...], axis=0, keepdims=True), jnp.sum(dg2_ref[...], axis=0, keepdims=True),
                dgq_ref[...], dgk_ref[...], dsink_ref[...], jnp.sum(loss_ref[...], axis=0, keepdims=True))):
            mine[row:row + 1, lane:lane + n] = val
        copies = []
        for k in range(1, N_DEV):
            flip = (k >> 2) & 1, (k >> 1) & 1, k & 1
            to = (x ^ flip[0], y ^ flip[1], c ^ flip[2])
            cp = pltpu.make_async_remote_copy(
                src_ref=mine, dst_ref=gathered.at[me], send_sem=send_sems.at[k - 1], recv_sem=recv_sems.at[k - 1],
                device_id=to, device_id_type=MESH)
            cp.start()
            copies.append(cp)
        gathered[me] = mine[...]
        for k in range(1, N_DEV):
            flip = (k >> 2) & 1, (k >> 1) & 1, k & 1
            src = 4 * (x ^ flip[0]) + 2 * (y ^ flip[1]) + (c ^ flip[2])
            pltpu.make_async_remote_copy(
                src_ref=mine, dst_ref=gathered.at[src], send_sem=send_sems.at[k - 1], recv_sem=recv_sems.at[k - 1],
                device_id=(x, y, c), device_id_type=MESH).wait_recv()
        for cp in copies:
            cp.wait_send()
        total = gathered[0]
        for k in range(1, N_DEV):
            total = total + gathered[k]
        row, lane, n = loss_slot
        out_refs[0][...] = total[row:row + 1, lane:lane + n]
        for i, (row, lane, n) in enumerate(slots):
            g = total[row:row + 1, lane:lane + n]
            d, nm, nv = _adamw_math(p_refs[i][...], g, p_refs[5 + i][...], p_refs[10 + i][...])
            for kind, val in enumerate((g, d, nm, nv)):
                out_refs[1 + 5 * kind + i][...] = val

    vm = pl.BlockSpec(memory_space=pltpu.VMEM)
    shapes = [jax.ShapeDtypeStruct((1, 128), F32)] + [jax.ShapeDtypeStruct((1, n), F32) for _, _, n in slots] * 4
    return pl.pallas_call(
        body, in_specs=[vm] * 21, out_specs=[vm] * 21, out_shape=shapes,
        scratch_shapes=[pltpu.VMEM((SMALL_ROWS, D_MODEL), F32), pltpu.VMEM((N_DEV, SMALL_ROWS, D_MODEL), F32),
                        pltpu.SemaphoreType.DMA((N_DEV - 1,)), pltpu.SemaphoreType.DMA((N_DEV - 1,))],
        name="small_step",
    )(*partials, *params)


def kernel(x, norm_mix_gain, w_in, q_norm_gain, k_norm_gain, attn_sinks, w_branch_attn, w_branch_ret, w_out, norm_ffn_gain, w_ffn_gate, w_ffn_up, w_ffn_down, loss_target, m_norm_mix_gain, m_w_in, m_q_norm_gain, m_k_norm_gain, m_attn_sinks, m_w_branch_attn, m_w_branch_ret, m_w_out, m_norm_ffn_gain, m_w_ffn_gate, m_w_ffn_up, m_w_ffn_down, v_norm_mix_gain, v_w_in, v_q_norm_gain, v_k_norm_gain, v_attn_sinks, v_w_branch_attn, v_w_branch_ret, v_w_out, v_norm_ffn_gain, v_w_ffn_gate, v_w_ffn_up, v_w_ffn_down):
    my_chip = 2 * lax.axis_index("x") + lax.axis_index("y")
    c_arr = lax.axis_index("c").astype(jnp.int32).reshape(1)
    chip_arr = my_chip.astype(jnp.int32).reshape(1)
    x_t, target = x[0], loss_target[0]
    g1, g2, gq, gk, sinks = norm_mix_gain, norm_ffn_gain, q_norm_gain, k_norm_gain, attn_sinks

    tr = lambda a: jnp.transpose(a[0])
    own_w_in = _bf(tr(w_in))
    own_rest = [_bf(a) for a in (tr(w_ffn_gate), tr(w_ffn_up), w_ffn_down[0], w_branch_attn[0], w_branch_ret[0],
                                 w_out[0])]
    tables, (got_w_in,) = _ret_tables(x_t.shape[0], _gather_exchange([own_w_in], 0.9))
    w_in_t = got_w_in.reshape(D_IN, D_MODEL)
    h1, q_a, kv_a, q_r, k_r, v_r, g_r, z_a, z_r, *got_rest = _proj_fwd(x_t, g1, w_in_t, _gather_exchange(own_rest, 0.8))
    wg_t, wu_t, wd, wba, wbr, wout = [got.reshape(-1, D_MODEL) for got in got_rest]

    gq_col, gk_col = gq.reshape(HEAD_DIM, 1), gk.reshape(HEAD_DIM, 1)
    attn, probs, sink_probs, o_ret, ret, states = _fused(
        [_attn_fwd(q_a, kv_a, gq_col, gk, sinks), _ret_fwd(q_r, k_r, v_r, g_r, tables)],
        grid=(x_t.shape[0] // BLOCK,), name="mixers_fwd")
    ba, br, merged, x1, h2 = _mix_fwd(attn, ret, z_a, z_r, x_t, wba, wbr, wout, g2)
    act, dgate, dup, dyb, dx1, dx1b, loss_p, dg2_p = _ffn_fwd_bwd(h2, x1, target, wg_t, wu_t, wd, g2)

    def pairs(row0, rows):
        return lambda i: [(h * rows, rows, (2 * i + h, pl.ds(row0, rows), slice(None))) for h in range(2)]

    f_block = jax.ShapeDtypeStruct((N_CHIPS, 3 * FF_SH, D_MODEL), F32)
    f_block, = _dw(dgate, h2, tm=2 * FF_SH, place=pairs(0, FF_SH), buf=f_block, name="dw_gate")
    f_block, = _dw(dup, h2, tm=2 * FF_SH, place=pairs(FF_SH, FF_SH), buf=f_block, name="dw_up")
    f_block, = _dw(act, dyb, tm=2 * FF_SH, place=pairs(2 * FF_SH, FF_SH), buf=f_block, name="dw_down")
    (dba, dbr, d_attn, d_o, d_gz, sib_ffn) = _mix_bwd(
        dx1b, z_a, z_r, ba, br, g_r, o_ret, wout, wba, wbr, _pair_exchange([f_block]))
    f_sum = _pair_sum(f_block, [sib_ffn], c_arr, tile=528, name="pair_sum_ffn")

    def quarters(row0, rows):
        return lambda i: [(k * rows, rows, (k, pl.ds(row0, rows), slice(None))) for k in range(N_CHIPS)]

    m_block = jax.ShapeDtypeStruct((N_CHIPS, D_MODEL, D_MODEL), F32)
    m_block, = _dw(attn, dba, tm=ATT_Q, place=quarters(0, 256), buf=m_block, name="dw_ba")
    m_block, = _dw(ret, dbr, tm=D_MODEL, place=pairs(256, 512), buf=m_block, name="dw_br")
    m_block, = _dw(merged, dx1b, tm=D_MODEL, place=quarters(768, 256), buf=m_block, name="dw_out")

    def w_in_rows(off, w):
        tm = min(w, D_MODEL)
        return dict(tm=tm, place=lambda i: [(0, tm, (pl.ds(off + i * tm, tm), slice(None)))])

    w_block = jax.ShapeDtypeStruct((D_IN, D_MODEL), F32)
    w_block, sib_mix = _dw(d_gz, h1, buf=w_block, name="dw_in_gz", exchange=_pair_exchange([m_block]),
                           **w_in_rows(P_GR[0], d_gz.shape[1]))
    m_sum = _pair_sum(m_block, [sib_mix], c_arr, tile=256, name="pair_sum_mix")

    (dq_a, dkv_a, dgq, dgk, dsinks, d_ret, got_ffn_sums, got_mix_sums) = _fused(
        [_attn_bwd(q_a, kv_a, d_attn, probs, sink_probs, gq_col, gk, gk_col),
         _ret_bwd(q_r, k_r, v_r, d_o, states, tables)],
        grid=(x_t.shape[0] // BLOCK + 1,), name="mixers_bwd", exchange=_scatter_to_owners([f_sum, m_sum]))
    dgq = dgq.reshape(1, HEAD_DIM)
    ffn_half = _sum_chips(f_sum, got_ffn_sums, chip_arr, tile=528, name="sum_chips_ffn")
    mix_half = _sum_chips(m_sum, got_mix_sums, chip_arr, tile=256, name="sum_chips_mix")
    w_block, ffn_other, mix_other = _dw(d_ret, h1, buf=w_block, name="dw_in_ret",
                                        exchange=_share_halves([ffn_half, mix_half]),
                                        **w_in_rows(P_QR[0], d_ret.shape[1]))
    w_block, sib_w_in_rest = _dw(dq_a, h1, buf=w_block, name="dw_in_q", exchange=_pair_exchange_w_in((1, 2, 3)),
                                 **w_in_rows(*P_QA))
    w_block, = _dw(dkv_a, h1, buf=w_block, name="dw_in_kv", **w_in_rows(*P_KVA))
    sib_w_in_first, = _run_exchange(_pair_exchange_w_in((0,), w_block), "pair_exchange_w_in")
    w_sum = _pair_sum(w_block.reshape(N_CHIPS, W_IN_SH, D_MODEL), [sib_w_in_first, sib_w_in_rest], c_arr, tile=592,
                      name="pair_sum_w_in")
    d_pieces = [dq_a, dkv_a, d_ret, d_gz]
    grad_x, dg1_p, got_w_in_sums = _proj_bwd(d_pieces, x_t, dx1, w_in_t, g1, _scatter_to_owners([w_sum]))
    w_in_half = _sum_chips(w_sum, got_w_in_sums, chip_arr, tile=592, name="sum_chips_w_in")
    w_in_other, = _run_exchange(_share_halves([w_in_half]), "share_halves_w_in")

    def update(name, g_half, g_other, tile, mats):
        outs = _adamw([tuple(tr(a) if t else a[0] for a in wmv) + (off,) for _, *wmv, off, t in mats],
                      g_half, g_other, c_arr, tile=tile, name=f"adamw_{name}")
        return {key: [jnp.transpose(o) if t else o for o in res] for (key, _, _, _, _, t), res in zip(mats, outs)}

    big = {
        **update("w_in", w_in_half, w_in_other, 592, [("w_in", w_in, m_w_in, v_w_in, 0, True)]),
        **update("ffn", ffn_half, ffn_other, 176, [
            ("wg", w_ffn_gate, m_w_ffn_gate, v_w_ffn_gate, 0, True),
            ("wu", w_ffn_up, m_w_ffn_up, v_w_ffn_up, FF_SH, True),
            ("wd", w_ffn_down, m_w_ffn_down, v_w_ffn_down, 2 * FF_SH, False)]),
        **update("mix", mix_half, mix_other, 128, [
            ("wba", w_branch_attn, m_w_branch_attn, v_w_branch_attn, 0, False),
            ("wbr", w_branch_ret, m_w_branch_ret, v_w_branch_ret, 256, False),
            ("wout", w_out, m_w_out, v_w_out, 768, False)])}

    loss_row, *small = _small_step(
        [loss_p.reshape(-1, 128), dg1_p.reshape(-1, D_MODEL), dg2_p.reshape(-1, D_MODEL), dgq, dgk, dsinks],
        [norm_mix_gain, norm_ffn_gain, q_norm_gain, k_norm_gain, attn_sinks,
         m_norm_mix_gain, m_norm_ffn_gain, m_q_norm_gain, m_k_norm_gain, m_attn_sinks,
         v_norm_mix_gain, v_norm_ffn_gain, v_q_norm_gain, v_k_norm_gain, v_attn_sinks])
    loss = loss_row[0, 0]

    def leaves(i):
        b = [big[n][i][None] for n in ("w_in", "wba", "wbr", "wout", "wg", "wu", "wd")]
        s1, s2, sq, sk, ss = small[5 * i:5 * i + 5]
        return [s1, b[0], sq, sk, ss, b[1], b[2], b[3], s2, b[4], b[5], b[6]]

    return (loss, grad_x[None], *leaves(0), *leaves(1), *leaves(2), *leaves(3))
```

```python
import jax
import jax.numpy as jnp
from jax import lax
from jax.experimental import pallas as pl
from jax.experimental.pallas import tpu as pltpu

F32 = jnp.float32
BF16 = jnp.bfloat16
MESH = pl.DeviceIdType.MESH

D_MODEL = 1024
EPS = 1e-6
HEAD_DIM = 64
N_Q_HEADS = 16
N_KV_HEADS = 2
GROUP = 8
BLOCK = 128
RET_HEADS = 4
RET_QK_DIM = 256
RET_V_DIM = 512
RET_CHUNK = 128
RET_ROT_BASE = 10000.0
D_FF = 2816
ATT_Q = N_Q_HEADS * HEAD_DIM
ATT_KV = N_KV_HEADS * HEAD_DIM
RET_QK = RET_HEADS * RET_QK_DIM
RET_V = RET_HEADS * RET_V_DIM
D_IN = 9472
ADAM_LR = 0.001
ADAM_B1 = 0.9
ADAM_B2 = 0.999
ADAM_EPS = 1e-08
ADAM_WD = 0.01
ADAM_STEP = 10

N_CHIPS = 4
N_DEV = 8
VMEM_LIMIT_BYTES = 60 * 1024 * 1024

P_QA = (0, 1024)
P_KVA = (1024, 256)
P_QR = (1280, 1024)
P_KR = (2304, 1024)
P_VR = (3328, 2048)
P_GR = (5376, 2048)
P_ZA = (7424, 1024)
P_ZR = (8448, 1024)

W_IN_SH = D_IN // N_CHIPS
FF_SH = D_FF // N_CHIPS

SMALL_ROWS = 8


def _dot(a, b):
    return jnp.dot(a, b, preferred_element_type=F32)


def _dot_nt(a, b):
    return lax.dot_general(a, b, (((1,), (1,)), ((), ())), preferred_element_type=F32)


def _dot_tn(a, b):
    return lax.dot_general(a, b, (((0,), (0,)), ((), ())), preferred_element_type=F32)


def _bf(x):
    return x.astype(BF16)


def _rms_stats(x):
    r = lax.rsqrt(jnp.mean(x * x, axis=-1, keepdims=True) + EPS)
    return r, x * r


def _rms_bwd(dy, xhat, r, gain):
    u = dy * gain
    dx = r * (u - xhat * jnp.mean(u * xhat, axis=-1, keepdims=True))
    return dx, dy * xhat


def _params(sem):
    return pltpu.CompilerParams(dimension_semantics=sem, vmem_limit_bytes=VMEM_LIMIT_BYTES)


_ANY = pl.BlockSpec(memory_space=pl.ANY)


class _Exchange:
    def __init__(self, ins, outs, n_sems, phases, staging=(), result_sources=()):
        self.ins, self.outs, self.n_sems, self.phases = list(ins), list(outs), n_sems, list(phases)
        self.staging, self.result_sources = list(staging), list(result_sources)


def _pallas(kern, *, grid, in_specs, out_specs, out_shape, args, name, scratch=(), exchange=None, aliases=None):
    aliases = aliases or {}
    if exchange is None:
        return pl.pallas_call(
            kern, grid=grid, in_specs=in_specs, out_specs=out_specs, out_shape=out_shape, name=name,
            scratch_shapes=list(scratch), input_output_aliases=aliases,
            compiler_params=_params(("arbitrary",) * len(grid)))(*args)
    n_in, n_out, n_sc = len(in_specs), len(out_specs), len(scratch)
    n_xi, n_xo, n_xs = len(exchange.ins), len(exchange.outs), len(exchange.staging)
    n_steps = 1
    for g in grid:
        n_steps *= g

    def wrapped(*refs):
        ins, refs = refs[:n_in], refs[n_in:]
        x_ins, refs = refs[:n_xi], refs[n_xi:]
        outs, refs = refs[:n_out], refs[n_out:]
        x_outs, refs = refs[:n_xo], refs[n_xo:]
        scr, refs = refs[:n_sc], refs[n_sc:]
        staging, (send_sems, recv_sems) = refs[:n_xs], refs[n_xs:]
        x_ins = list(x_ins) + [outs[j] for j in exchange.result_sources]
        step = pl.program_id(0)
        for d in range(1, len(grid)):
            step = step * grid[d] + pl.program_id(d)
        for frac, fn in exchange.phases:
            at = min(int(frac * n_steps), n_steps - 1)

            @pl.when(step == at)
            def _(fn=fn):
                fn(x_ins, x_outs, send_sems, recv_sems, staging)

        kern(*ins, *outs, *scr)

    sems = [pltpu.SemaphoreType.DMA((exchange.n_sems,)), pltpu.SemaphoreType.DMA((exchange.n_sems,))]
    return pl.pallas_call(
        wrapped, grid=grid, in_specs=list(in_specs) + [_ANY] * n_xi, out_specs=list(out_specs) + [_ANY] * n_xo,
        out_shape=list(out_shape) + exchange.outs, name=name,
        scratch_shapes=list(scratch) + exchange.staging + sems, input_output_aliases=aliases,
        compiler_params=_params(("arbitrary",) * len(grid)))(*args, *exchange.ins)


def _run_exchange(exchange, name):
    def body(*refs):
        n_i, n_o = len(exchange.ins), len(exchange.outs)
        staging, (send_sems, recv_sems) = refs[n_i + n_o:-2], refs[-2:]
        for _, fn in exchange.phases:
            fn(refs[:n_i], refs[n_i:n_i + n_o], send_sems, recv_sems, staging)

    sems = [pltpu.SemaphoreType.DMA((exchange.n_sems,)), pltpu.SemaphoreType.DMA((exchange.n_sems,))]
    return pl.pallas_call(body, in_specs=[_ANY] * len(exchange.ins), out_specs=[_ANY] * len(exchange.outs),
                          out_shape=exchange.outs, scratch_shapes=exchange.staging + sems, name=name,
                          compiler_params=pltpu.CompilerParams(vmem_limit_bytes=VMEM_LIMIT_BYTES))(*exchange.ins)


def _fused(parts, *, grid, name, exchange=None):
    counts = [(len(p["in_specs"]), len(p["out_specs"]), len(p["scratch"])) for p in parts]
    n_in, n_out = sum(c[0] for c in counts), sum(c[1] for c in counts)

    def kern(*refs):
        ins, outs, scr = refs[:n_in], refs[n_in:n_in + n_out], refs[n_in + n_out:]
        i0 = o0 = s0 = 0
        for p, (ni, no, ns) in zip(parts, counts):
            p["kern"](*ins[i0:i0 + ni], *outs[o0:o0 + no], *scr[s0:s0 + ns])
            i0, o0, s0 = i0 + ni, o0 + no, s0 + ns

    cat = lambda key: [a for p in parts for a in p[key]]
    return _pallas(kern, grid=grid, in_specs=cat("in_specs"), out_specs=cat("out_specs"), out_shape=cat("out_shape"),
                   scratch=cat("scratch"), args=cat("args"), name=name, exchange=exchange)


def _row_call(body, *, tm, row_ins, res_ins, row_outs, part_outs=(), name, exchange=None):
    t = row_ins[0].shape[0]
    n_tiles = t // tm
    in_specs = [pl.BlockSpec((tm, a.shape[1]), lambda i: (i, 0)) for a in row_ins]
    in_specs += [pl.BlockSpec(a.shape, lambda i: (0, 0), pipeline_mode=pl.Buffered(1)) for a in res_ins]
    out_shape = [jax.ShapeDtypeStruct((t, w), dt) for (w, dt) in row_outs]
    out_shape += [jax.ShapeDtypeStruct((n_tiles, 1, w), F32) for w in part_outs]
    out_specs = [pl.BlockSpec((tm, w), lambda i: (i, 0)) for (w, _) in row_outs]
    out_specs += [pl.BlockSpec((1, 1, w), lambda i: (i, 0, 0)) for w in part_outs]
    n_ri, n_re, n_ro = len(row_ins), len(res_ins), len(row_outs)

    def kern(*refs):
        body(refs[:n_ri], refs[n_ri:n_ri + n_re], refs[n_ri + n_re:n_ri + n_re + n_ro], refs[n_ri + n_re + n_ro:])

    return _pallas(kern, grid=(n_tiles,), in_specs=in_specs, out_specs=out_specs, out_shape=out_shape,
                   args=[*row_ins, *res_ins], name=name, exchange=exchange)


def _proj_fwd(x, g1, w_in_t, exchange):
    pieces = ((P_QA, F32), (P_KVA, F32), (P_QR, F32), (P_KR, F32), (P_VR, BF16), (P_GR, F32), (P_ZA, F32), (P_ZR, F32))

    def body(ri, re, ro, po):
        x_t = ri[0][...]
        r, xhat = _rms_stats(x_t)
        hb = _bf(xhat * re[0][...])
        ro[0][...] = hb
        for k, ((off, w), dt) in enumerate(pieces):
            ro[1 + k][...] = _dot_nt(hb, re[1][off:off + w, :]).astype(dt)

    outs = [(D_MODEL, BF16)] + [(w, dt) for ((_, w), dt) in pieces]
    return _row_call(body, tm=256, row_ins=[x], res_ins=[g1, w_in_t], row_outs=outs, name="proj_fwd",
                     exchange=exchange)


def _mix_fwd(attn, ret, z_a, z_r, x, wba, wbr, wout, g2):
    def body(ri, re, ro, po):
        ba = _dot(ri[0][...], re[0][...])
        br = _dot(ri[1][...], re[1][...])
        m = jax.nn.sigmoid(ri[2][...]) * ba + jax.nn.sigmoid(ri[3][...]) * br
        mb = _bf(m)
        x1 = ri[4][...] + _dot(mb, re[2][...])
        r, xhat = _rms_stats(x1)
        ro[0][...] = ba
        ro[1][...] = br
        ro[2][...] = mb
        ro[3][...] = x1
        ro[4][...] = _bf(xhat * re[3][...])

    outs = [(D_MODEL, F32), (D_MODEL, F32), (D_MODEL, BF16), (D_MODEL, F32), (D_MODEL, BF16)]
    return _row_call(body, tm=512, row_ins=[attn, ret, z_a, z_r, x], res_ins=[wba, wbr, wout, g2], row_outs=outs,
                     name="mix_fwd")


def _ffn_fwd_bwd(h2, x1, target, wg_t, wu_t, wd, g2):
    def body(ri, re, ro, po):
        h2_t = ri[0][...]
        x1_t = ri[1][...]
        gate = _dot_nt(h2_t, re[0][...])
        up = _dot_nt(h2_t, re[1][...])
        sg = jax.nn.sigmoid(gate)
        sl = gate * sg
        actb = _bf(sl * up)
        ro[0][...] = actb
        y = x1_t + _dot(actb, re[2][...])
        e = y - ri[2][...]
        po[0][0] = jnp.broadcast_to(0.5 * jnp.sum(jnp.sum(e * e, axis=1, keepdims=True), axis=0, keepdims=True)
                                    * (1.0 / D_MODEL), (1, 128))
        dy = e * (1.0 / D_MODEL)
        dyb = _bf(dy)
        ro[3][...] = dyb
        dact = _dot_nt(dyb, re[2][...])
        dupb = _bf(dact * sl)
        dgateb = _bf(dact * up * (sg * (1.0 + gate * (1.0 - sg))))
        ro[1][...] = dgateb
        ro[2][...] = dupb
        dh2 = _dot(dgateb, re[0][...]) + _dot(dupb, re[1][...])
        r, xhat = _rms_stats(x1_t)
        dxn, dgain = _rms_bwd(dh2, xhat, r, re[3][...])
        dx1 = dy + dxn
        ro[4][...] = dx1
        ro[5][...] = _bf(dx1)
        po[1][0] = jnp.sum(dgain, axis=0, keepdims=True)

    outs = [(D_FF, BF16), (D_FF, BF16), (D_FF, BF16), (D_MODEL, BF16), (D_MODEL, F32), (D_MODEL, BF16)]
    return _row_call(body, tm=256, row_ins=[h2, x1, target], res_ins=[wg_t, wu_t, wd, g2], row_outs=outs,
                     part_outs=(128, D_MODEL), name="ffn_fwd_bwd")


def _mix_bwd(dx1b, z_a, z_r, ba, br, g_r, o_ret, wout, wba, wbr, exchange):
    def body(ri, re, ro, po):
        dm = _dot_nt(ri[0][...], re[0][...])
        sa = jax.nn.sigmoid(ri[1][...])
        sr = jax.nn.sigmoid(ri[2][...])
        dbab = _bf(sa * dm)
        dbrb = _bf(sr * dm)
        ro[0][...] = dbab
        ro[1][...] = dbrb
        ro[4][:, RET_V:RET_V + D_MODEL] = _bf(dm * ri[3][...] * (sa * (1.0 - sa)))
        ro[4][:, RET_V + D_MODEL:RET_V + 2 * D_MODEL] = _bf(dm * ri[4][...] * (sr * (1.0 - sr)))
        ro[2][...] = _bf(_dot_nt(dbab, re[1][...]))
        dret = _dot_nt(dbrb, re[2][...])
        for h in range(RET_HEADS):
            cols = slice(h * RET_V_DIM, (h + 1) * RET_V_DIM)
            g = ri[5][:, cols]
            r, rn = _rms_stats(ri[6][:, cols])
            sg = jax.nn.sigmoid(g)
            dret_h = dret[:, cols]
            d_rn = dret_h * (g * sg)
            ro[4][:, cols] = _bf(dret_h * rn * (sg * (1.0 + g * (1.0 - sg))))
            ro[3][:, cols] = r * (d_rn - rn * jnp.mean(d_rn * rn, axis=-1, keepdims=True))

    outs = [(D_MODEL, BF16), (D_MODEL, BF16), (ATT_Q, BF16), (RET_V, F32), (RET_V + 2 * D_MODEL, BF16)]
    return _row_call(body, tm=256, row_ins=[dx1b, z_a, z_r, ba, br, g_r, o_ret], res_ins=[wout, wba, wbr],
                     row_outs=outs, name="mix_bwd", exchange=exchange)


def _proj_bwd(d_pieces, x, dx1, w_in_t, g1, exchange):
    widths = [p.shape[1] for p in d_pieces]
    groups = [(sum(widths[:k]), w) for k, w in enumerate(widths)]
    n_p = len(groups)

    def body(ri, re, ro, po):
        dh = None
        for k, (off, w) in enumerate(groups):
            term = _dot(ri[k][...], re[0][off:off + w, :])
            dh = term if dh is None else dh + term
        r, xhat = _rms_stats(ri[n_p][...])
        dxn, dgain = _rms_bwd(dh, xhat, r, re[1][...])
        ro[0][...] = ri[n_p + 1][...] + dxn
        po[0][0] = jnp.sum(dgain, axis=0, keepdims=True)

    return _row_call(body, tm=512, row_ins=[*d_pieces, x, dx1], res_ins=[w_in_t, g1], row_outs=[(D_MODEL, F32)],
                     part_outs=(D_MODEL,), name="proj_bwd", exchange=exchange)


def _dw(a, b, *, tm, place, buf, name, exchange=None):
    t, m = a.shape
    n = b.shape[1]
    tk = min(2048, t)
    n_i, n_k = m // tm, t // tk
    fresh = isinstance(buf, jax.ShapeDtypeStruct)
    n_copies = len(place(0))

    def kern(a_ref, b_ref, *rest):
        out_ref, acc, sems = rest[-3:]
        i, k = pl.program_id(0), pl.program_id(1)
        part = _dot_tn(a_ref[...], b_ref[...])

        @pl.when(k == 0)
        def _():
            acc[i] = part

        @pl.when(k > 0)
        def _():
            acc[i] += part

        def copies(tile):
            return [pltpu.make_async_copy(acc.at[tile, pl.ds(r0, rows), :], out_ref.at[idx], sems.at[tile * n_copies + c])
                    for c, (r0, rows, idx) in enumerate(place(tile))]

        for tile in range(n_i):
            @pl.when((i == tile) & (k == n_k - 1))
            def _(tile=tile):
                for cp in copies(tile):
                    cp.start()

        @pl.when((i == n_i - 1) & (k == n_k - 1))
        def _():
            for tile in range(n_i):
                for cp in copies(tile):
                    cp.wait()

    in_specs = [pl.BlockSpec((tk, tm), lambda i, k: (k, i)), pl.BlockSpec((tk, n), lambda i, k: (k, 0))]
    shape = buf if fresh else jax.ShapeDtypeStruct(buf.shape, buf.dtype)
    return _pallas(
        kern, grid=(n_i, n_k), in_specs=in_specs + ([] if fresh else [_ANY]), out_specs=[_ANY], out_shape=[shape],
        scratch=[pltpu.VMEM((n_i, tm, n), F32), pltpu.SemaphoreType.DMA((n_i * n_copies,))],
        args=[a, b] + ([] if fresh else [buf]), aliases=None if fresh else {2: 0}, name=name, exchange=exchange)


def _heads_to_lanes(x3):
    return jnp.concatenate([x3[g] for g in range(GROUP)], axis=1)


def _lanes_to_heads(xt):
    return jnp.concatenate([xt[:, g * BLOCK:(g + 1) * BLOCK] for g in range(GROUP)], axis=0)


def _attn_queries(kvh, q_ref, gq_col):
    cols = slice(kvh * GROUP * HEAD_DIM, (kvh + 1) * GROUP * HEAD_DIM)
    q3 = q_ref[:, cols].T.reshape(GROUP, HEAD_DIM, BLOCK)
    rq = lax.rsqrt(jnp.mean(q3 * q3, axis=1, keepdims=True) + EPS)
    qhat = q3 * rq
    return qhat, rq, _heads_to_lanes(_bf(qhat * (gq_col * (HEAD_DIM ** -0.5))))


def _from_prev():
    j = lax.broadcasted_iota(jnp.int32, (BLOCK, GROUP * BLOCK), 0)
    i = lax.broadcasted_iota(jnp.int32, (BLOCK, GROUP * BLOCK), 1) & (BLOCK - 1)
    return j > i


def _attn_probs(n, kvh, qts, kvp_ref, kvc_ref, gk, sink_ref):
    kcols = slice(kvh * HEAD_DIM, (kvh + 1) * HEAD_DIM)
    k = jnp.concatenate([kvp_ref[:, kcols], kvc_ref[:, kcols]], axis=0)
    rk, khat = _rms_stats(k)
    st = _dot(_bf(khat * gk), qts)
    f = jnp.where(_from_prev(), jnp.where(n > 0, st[0:BLOCK], -1e30), st[BLOCK:2 * BLOCK])
    sink = jnp.concatenate([jnp.broadcast_to(sink_ref[0:1, kvh * GROUP + g:kvh * GROUP + g + 1], (1, BLOCK))
                            for g in range(GROUP)], axis=1)
    m = jnp.maximum(jnp.max(f, axis=0, keepdims=True), sink)
    e = jnp.exp(f - m)
    es = jnp.exp(sink - m)
    inv = 1.0 / (jnp.sum(e, axis=0, keepdims=True) + es)
    return e * inv, es * inv


def _unfold(from_prev, xf):
    return _bf(jnp.concatenate([jnp.where(from_prev, xf, 0.0), jnp.where(from_prev, 0.0, xf)], axis=0))


def _attn_fwd(q_a, kv_a, gq_col, gk, sinks):
    t = q_a.shape[0]
    nb = t // BLOCK

    def kern(q_ref, kvp_ref, kvc_ref, gq_ref, gk_ref, sink_ref, o_ref, pf_ref, ps_ref):
        n = pl.program_id(0)
        kvt = jnp.concatenate([kvp_ref[...].T, kvc_ref[...].T], axis=1)
        for kvh in range(N_KV_HEADS):
            _, _, qts = _attn_queries(kvh, q_ref, gq_ref[...])
            pf, psink = _attn_probs(n, kvh, qts, kvp_ref, kvc_ref, gk_ref[...], sink_ref)
            lanes = slice(kvh * GROUP * BLOCK, (kvh + 1) * GROUP * BLOCK)
            pf_ref[:, lanes] = pf
            ps_ref[:, lanes] = psink
            vt = _bf(kvt[ATT_KV + kvh * HEAD_DIM:ATT_KV + (kvh + 1) * HEAD_DIM, :])
            out_t = _dot(vt, _unfold(_from_prev(), pf))
            cols = slice(kvh * GROUP * HEAD_DIM, (kvh + 1) * GROUP * HEAD_DIM)
            o_ref[:, cols] = _bf(_lanes_to_heads(out_t).T)

    small = lambda a: pl.BlockSpec(a.shape, lambda n: (0, 0))
    folded = N_KV_HEADS * GROUP * BLOCK
    return dict(
        kern=kern,
        in_specs=[pl.BlockSpec((BLOCK, ATT_Q), lambda n: (n, 0)),
                  pl.BlockSpec((BLOCK, 2 * ATT_KV), lambda n: (jnp.maximum(n - 1, 0), 0)),
                  pl.BlockSpec((BLOCK, 2 * ATT_KV), lambda n: (n, 0)),
                  small(gq_col), small(gk), small(sinks)],
        out_specs=[pl.BlockSpec((BLOCK, ATT_Q), lambda n: (n, 0)), pl.BlockSpec((BLOCK, folded), lambda n: (n, 0)),
                   pl.BlockSpec((None, 1, folded), lambda n: (n, 0, 0))],
        out_shape=[jax.ShapeDtypeStruct((t, ATT_Q), BF16), jax.ShapeDtypeStruct((t, folded), F32),
                   jax.ShapeDtypeStruct((nb, 1, folded), F32)],
        scratch=[], args=[q_a, kv_a, kv_a, gq_col, gk, sinks])


def _attn_bwd(q_a, kv_a, d_attn, probs, sink_probs, gq_col, gk, gk_col):
    t = q_a.shape[0]
    nb = t // BLOCK

    def kern(q_ref, kvp_ref, kvc_ref, do_ref, pf_ref, ps_ref, gq_ref, gk_ref, gkc_ref,
             dq_ref, dkv_ref, dgq_ref, dgk_ref, dsink_ref, band_k, band_v, carry_k, carry_v):
        n = pl.program_id(0)
        gq_v = gq_ref[...]
        gk_v = gk_ref[...]

        @pl.when(n == 0)
        def _():
            carry_k[...] = jnp.zeros_like(carry_k)
            carry_v[...] = jnp.zeros_like(carry_v)
            dgq_ref[...] = jnp.zeros_like(dgq_ref)
            dgk_ref[...] = jnp.zeros_like(dgk_ref)
            dsink_ref[...] = jnp.zeros_like(dsink_ref)

        @pl.when(n == nb)
        def _():
            band_k[...] = jnp.zeros_like(band_k)
            band_v[...] = jnp.zeros_like(band_v)

        @pl.when(n < nb)
        def _():
            lane16 = lax.broadcasted_iota(jnp.int32, (1, N_Q_HEADS), 1)
            dsink = jnp.zeros((1, N_Q_HEADS), F32)
            dgq = jnp.zeros((HEAD_DIM, 1), F32)
            gk_col = gkc_ref[...]
            kvt = jnp.concatenate([kvp_ref[...].T, kvc_ref[...].T], axis=1)
            from_prev = _from_prev()
            for kvh in range(N_KV_HEADS):
                qhat, rq, qts = _attn_queries(kvh, q_ref, gq_v)
                lanes = slice(kvh * GROUP * BLOCK, (kvh + 1) * GROUP * BLOCK)
                pf = pf_ref[:, lanes]
                cols = slice(kvh * GROUP * HEAD_DIM, (kvh + 1) * GROUP * HEAD_DIM)
                vcols = slice(ATT_KV + kvh * HEAD_DIM, ATT_KV + (kvh + 1) * HEAD_DIM)
                dot = _heads_to_lanes(_bf(do_ref[:, cols].astype(F32).T.reshape(GROUP, HEAD_DIM, BLOCK)))
                vb = _bf(jnp.concatenate([kvp_ref[:, vcols], kvc_ref[:, vcols]], axis=0))
                dpt = _dot(vb, dot)
                dpf = jnp.where(from_prev, dpt[0:BLOCK], dpt[BLOCK:2 * BLOCK])
                delta = jnp.sum(pf * dpf, axis=0, keepdims=True)
                dst = _unfold(from_prev, pf * (dpf - delta))
                dsk = ps_ref[:, lanes] * delta
                for g in range(GROUP):
                    tot = jnp.sum(dsk[:, g * BLOCK:(g + 1) * BLOCK], axis=1, keepdims=True)
                    dsink = dsink - jnp.where(lane16 == kvh * GROUP + g, tot, 0.0)
                kt = kvt[kvh * HEAD_DIM:(kvh + 1) * HEAD_DIM, :]
                knt = _bf(kt * lax.rsqrt(jnp.mean(kt * kt, axis=0, keepdims=True) + EPS) * gk_col)
                dqn = (_dot(knt, dst) * (HEAD_DIM ** -0.5))
                band_k[kvh] = _dot_nt(dst, qts)
                band_v[kvh] = _dot_nt(_unfold(from_prev, pf), dot)
                dqn3 = _lanes_to_heads(dqn).reshape(GROUP, HEAD_DIM, BLOCK)
                u = dqn3 * gq_v
                dq3 = rq * (u - qhat * jnp.mean(u * qhat, axis=1, keepdims=True))
                dgq = dgq + jnp.sum(jnp.sum(dqn3 * qhat, axis=0), axis=1, keepdims=True)
                dq_ref[:, cols] = _bf(dq3.reshape(GROUP * HEAD_DIM, BLOCK).T)
            dsink_ref[...] += dsink
            dgq_ref[...] += dgq

        dgk = jnp.zeros((1, HEAD_DIM), F32)
        for kvh in range(N_KV_HEADS):
            kcols = slice(kvh * HEAD_DIM, (kvh + 1) * HEAD_DIM)
            vcols = slice(ATT_KV + kvh * HEAD_DIM, ATT_KV + (kvh + 1) * HEAD_DIM)
            dkn = carry_k[kvh] + band_k[kvh, 0:BLOCK, :]
            dv = carry_v[kvh] + band_v[kvh, 0:BLOCK, :]
            rk, khat = _rms_stats(kvp_ref[:, kcols])
            dk, dgain = _rms_bwd(dkn, khat, rk, gk_v)
            dgk = dgk + jnp.sum(dgain, axis=0, keepdims=True)
            dkv_ref[:, kcols] = _bf(dk)
            dkv_ref[:, vcols] = _bf(dv)
            carry_k[kvh] = band_k[kvh, BLOCK:2 * BLOCK, :]
            carry_v[kvh] = band_v[kvh, BLOCK:2 * BLOCK, :]
        dgk_ref[...] += dgk

    small = lambda a: pl.BlockSpec(a.shape, lambda n: (0, 0))
    last = nb - 1
    return dict(
        kern=kern,
        in_specs=[pl.BlockSpec((BLOCK, ATT_Q), lambda n: (jnp.minimum(n, last), 0)),
                  pl.BlockSpec((BLOCK, 2 * ATT_KV), lambda n: (jnp.maximum(n - 1, 0), 0)),
                  pl.BlockSpec((BLOCK, 2 * ATT_KV), lambda n: (jnp.minimum(n, last), 0)),
                  pl.BlockSpec((BLOCK, ATT_Q), lambda n: (jnp.minimum(n, last), 0)),
                  pl.BlockSpec((BLOCK, probs.shape[1]), lambda n: (jnp.minimum(n, last), 0)),
                  pl.BlockSpec((None, 1, probs.shape[1]), lambda n: (jnp.minimum(n, last), 0, 0)),
                  small(gq_col), small(gk), small(gk_col)],
        out_specs=[pl.BlockSpec((BLOCK, ATT_Q), lambda n: (jnp.minimum(n, last), 0)),
                   pl.BlockSpec((BLOCK, 2 * ATT_KV), lambda n: (jnp.maximum(n - 1, 0), 0)),
                   pl.BlockSpec((HEAD_DIM, 1), lambda n: (0, 0)),
                   pl.BlockSpec((1, HEAD_DIM), lambda n: (0, 0)),
                   pl.BlockSpec((1, N_Q_HEADS), lambda n: (0, 0))],
        out_shape=[jax.ShapeDtypeStruct((t, ATT_Q), BF16), jax.ShapeDtypeStruct((t, 2 * ATT_KV), BF16),
                   jax.ShapeDtypeStruct((HEAD_DIM, 1), F32), jax.ShapeDtypeStruct((1, HEAD_DIM), F32),
                   jax.ShapeDtypeStruct((1, N_Q_HEADS), F32)],
        scratch=[pltpu.VMEM((N_KV_HEADS, 2 * BLOCK, HEAD_DIM), F32),
                 pltpu.VMEM((N_KV_HEADS, 2 * BLOCK, HEAD_DIM), F32),
                 pltpu.VMEM((N_KV_HEADS, BLOCK, HEAD_DIM), F32),
                 pltpu.VMEM((N_KV_HEADS, BLOCK, HEAD_DIM), F32)],
        args=[q_a, kv_a, kv_a, d_attn, probs, sink_probs, gq_col, gk, gk_col])


def _ret_tables(t, exchange):
    theta = 1.0 / (RET_ROT_BASE ** jnp.linspace(0.0, 1.0, RET_QK_DIM // 2, dtype=F32))
    theta2 = jnp.repeat(theta, 2)[None, :]
    sign = jnp.tile(jnp.array([-1.0, 1.0], F32), RET_QK_DIM // 2)[None, :]

    def kern(theta_ref, sign_ref, cos_ref, sin_ref):
        first = pl.program_id(0) * RET_CHUNK
        pos = (first + lax.broadcasted_iota(jnp.int32, (RET_CHUNK, RET_QK_DIM), 0)).astype(F32)
        ang = pos * theta_ref[...]
        cos_ref[...] = jnp.cos(ang)
        sin_ref[...] = jnp.sin(ang) * sign_ref[...]

    row = pl.BlockSpec((1, RET_QK_DIM), lambda n: (0, 0))
    blk = pl.BlockSpec((RET_CHUNK, RET_QK_DIM), lambda n: (n, 0))
    cos, sin_s, *got = _pallas(kern, grid=(t // RET_CHUNK,), in_specs=[row, row], out_specs=[blk, blk],
                               out_shape=[jax.ShapeDtypeStruct((t, RET_QK_DIM), F32)] * 2, args=[theta2, sign],
                               name="position_tables", exchange=exchange)
    log_gamma = jnp.log(1.0 - 2.0 ** (-5.0 - jnp.arange(RET_HEADS, dtype=F32)))
    i = jnp.arange(RET_CHUNK, dtype=F32)
    diff = i[:, None] - i[None, :]
    causal = diff >= 0
    decay = jnp.where(causal[None], jnp.exp(jnp.where(causal, diff, 0.0)[None] * log_gamma[:, None, None]), 0.0)
    xi = jnp.exp((i + 1.0)[None, :] * log_gamma[:, None])[:, :, None]
    zeta = jnp.exp((RET_CHUNK - 1.0 - i)[None, :] * log_gamma[:, None])[:, :, None]
    gch = jnp.broadcast_to(jnp.exp(RET_CHUNK * log_gamma)[:, None, None], (RET_HEADS, 1, 128))
    return (cos, sin_s, decay, xi, zeta, gch), got


def _swap_pairs(x):
    lane = lax.broadcasted_iota(jnp.int32, x.shape, 1)
    return jnp.where((lane & 1) == 0, pltpu.roll(x, RET_QK_DIM - 1, 1), pltpu.roll(x, 1, 1))


def _rotate(x, cos, sin_s):
    return x * cos + _swap_pairs(x) * sin_s


def _rotate_bwd(dy, cos, sin_s):
    return dy * cos + _swap_pairs(dy * sin_s)


def _ret_specs(order):
    qk = pl.BlockSpec((RET_CHUNK, RET_QK), lambda j: (order(j), 0))
    v = pl.BlockSpec((RET_CHUNK, RET_V), lambda j: (order(j), 0))
    dec = pl.BlockSpec((RET_HEADS, RET_CHUNK, RET_CHUNK), lambda j: (0, 0, 0))
    col = pl.BlockSpec((RET_HEADS, RET_CHUNK, 1), lambda j: (0, 0, 0))
    gch = pl.BlockSpec((RET_HEADS, 1, 128), lambda j: (0, 0, 0))
    st = pl.BlockSpec((RET_HEADS, None, RET_QK_DIM, RET_V_DIM), lambda j: (0, order(j), 0, 0))
    pos = pl.BlockSpec((RET_CHUNK, RET_QK_DIM), lambda j: (order(j), 0))
    return qk, v, dec, col, gch, st, pos


def _ret_fwd(q_r, k_r, v_r, g_r, tables):
    t = q_r.shape[0]
    nc = t // RET_CHUNK
    cos, sin_s, decay, xi, zeta, gch = tables

    def kern(q_ref, k_ref, v_ref, g_ref, cos_ref, sin_ref, dec_ref, xi_ref, zeta_ref, gch_ref,
             o_ref, ret_ref, st_ref, state):
        @pl.when(pl.program_id(0) == 0)
        def _():
            state[...] = jnp.zeros_like(state)

        cos_t = cos_ref[...]
        sin_t = sin_ref[...]
        for h in range(RET_HEADS):
            qc = slice(h * RET_QK_DIM, (h + 1) * RET_QK_DIM)
            vc = slice(h * RET_V_DIM, (h + 1) * RET_V_DIM)
            qs = _bf(_rotate(q_ref[:, qc], cos_t, sin_t))
            ks = _rotate(k_ref[:, qc] * (RET_QK_DIM ** -0.5), cos_t, sin_t)
            vb = v_ref[:, vc]
            s_old = state[h]
            sb = _bf(s_old)
            st_ref[h] = sb
            inner = _dot_nt(qs, _bf(ks)) * dec_ref[h]
            out = _dot(_bf(inner), vb) + _dot(qs, sb) * xi_ref[h]
            state[h] = gch_ref[h, :, 0:1] * s_old + _dot_tn(_bf(ks * zeta_ref[h]), vb)
            o_ref[:, vc] = out
            r, rn = _rms_stats(out)
            g = g_ref[:, vc]
            ret_ref[:, vc] = _bf(g * jax.nn.sigmoid(g) * rn)

    qk, v, dec, col, gsp, st, pos = _ret_specs(lambda j: j)
    return dict(
        kern=kern,
        in_specs=[qk, qk, v, v, pos, pos, dec, col, col, gsp],
        out_specs=[v, v, st],
        out_shape=[jax.ShapeDtypeStruct((t, RET_V), F32), jax.ShapeDtypeStruct((t, RET_V), BF16),
                   jax.ShapeDtypeStruct((RET_HEADS, nc, RET_QK_DIM, RET_V_DIM), BF16)],
        scratch=[pltpu.VMEM((RET_HEADS, RET_QK_DIM, RET_V_DIM), F32)],
        args=[q_r, k_r, v_r, g_r, cos, sin_s, decay, xi, zeta, gch])


def _ret_bwd(q_r, k_r, v_r, d_o, states, tables):
    t = q_r.shape[0]
    nc = t // RET_CHUNK
    cos, sin_s, decay, xi, zeta, gch = tables

    def kern(q_ref, k_ref, v_ref, do_ref, st_ref, cos_ref, sin_ref, dec_ref, xi_ref, zeta_ref, gch_ref,
             d_ref, dstate):
        dq_ref, dk_ref = d_ref.at[:, 0:RET_QK], d_ref.at[:, RET_QK:2 * RET_QK]
        dv_ref = d_ref.at[:, 2 * RET_QK:2 * RET_QK + RET_V]

        @pl.when(pl.program_id(0) == 0)
        def _():
            dstate[...] = jnp.zeros_like(dstate)

        @pl.when(pl.program_id(0) < nc)
        def _():
            cos_t = cos_ref[...]
            sin_t = sin_ref[...]
            scale = RET_QK_DIM ** -0.5
            for h in range(RET_HEADS):
                qc = slice(h * RET_QK_DIM, (h + 1) * RET_QK_DIM)
                vc = slice(h * RET_V_DIM, (h + 1) * RET_V_DIM)
                qs = _bf(_rotate(q_ref[:, qc], cos_t, sin_t))
                ks = _rotate(k_ref[:, qc] * scale, cos_t, sin_t)
                ksb = _bf(ks)
                vb = v_ref[:, vc]
                d_o_t = do_ref[:, vc]
                dob = _bf(d_o_t)
                doxb = _bf(d_o_t * xi_ref[h])
                dec = dec_ref[h]
                ds_old = dstate[h]
                dsb = _bf(ds_old)
                pb = _bf(_dot_nt(qs, ksb) * dec)
                dpb = _bf(_dot_nt(dob, vb) * dec)
                dqs = _dot(dpb, ksb) + _dot_nt(doxb, st_ref[h])
                dks = _dot_tn(dpb, qs) + _dot_nt(vb, dsb) * zeta_ref[h]
                dv_ref[:, vc] = _bf(_dot_tn(pb, dob) + _dot(_bf(ks * zeta_ref[h]), dsb))
                dstate[h] = gch_ref[h, :, 0:1] * ds_old + _dot_tn(qs, doxb)
                dq_ref[:, qc] = _bf(_rotate_bwd(dqs, cos_t, sin_t))
                dk_ref[:, qc] = _bf(_rotate_bwd(dks, cos_t, sin_t) * scale)

    backwards = lambda j: jnp.maximum(nc - 1 - j, 0)
    qk, v, dec, col, gsp, st, pos = _ret_specs(backwards)
    return dict(
        kern=kern,
        in_specs=[qk, qk, v, v, st, pos, pos, dec, col, col, gsp],
        out_specs=[pl.BlockSpec((RET_CHUNK, 2 * RET_QK + RET_V), lambda j: (backwards(j), 0))],
        out_shape=[jax.ShapeDtypeStruct((t, 2 * RET_QK + RET_V), BF16)],
        scratch=[pltpu.VMEM((RET_HEADS, RET_QK_DIM, RET_V_DIM), F32)],
        args=[q_r, k_r, v_r, d_o, states, cos, sin_s, decay, xi, zeta, gch])


def _position():
    return lax.axis_index("x"), lax.axis_index("y"), lax.axis_index("c")


def _gather_exchange(owns, forward_at):
    n = len(owns)

    def copies(ins, outs, send_sems, recv_sems, staging):
        x, y, c = _position()
        sibling = (x, y, 1 - c)
        chips = [(1 - x, y), (x, 1 - y), (1 - x, 1 - y)]
        my_chip = 2 * x + y

        def slab(a, chip, hf):
            half = owns[a].shape[0] // 2
            return outs[a].at[chip, pl.ds(hf * half, half), :]

        def copy(k, src, dst, to):
            return pltpu.make_async_remote_copy(src_ref=src, dst_ref=dst, send_sem=send_sems.at[k],
                                                recv_sem=recv_sems.at[k], device_id=to, device_id_type=MESH)

        first, passed, from_sibling, stage_in, stage_out = [], [], [], [], []
        for a in range(n):
            half = owns[a].shape[0] // 2
            for k, (cx, cy) in enumerate(chips):
                first.append(copy(6 * a + k, ins[a].at[pl.ds(c * half, half), :], slab(a, my_chip, c), (cx, cy, c)))
                landed = slab(a, 2 * cx + cy, c)
                passed.append(copy(6 * a + 3 + k, landed, landed, sibling))
                theirs = slab(a, 2 * cx + cy, 1 - c)
                from_sibling.append(copy(6 * a + 3 + k, theirs, theirs, sibling))
            stage_in.append(pltpu.make_async_copy(ins[a], staging[a], send_sems.at[6 * n + a]))
            stage_out.append(pltpu.make_async_copy(staging[a], outs[a].at[my_chip], recv_sems.at[6 * n + a]))
        return first, passed, from_sibling, stage_in, stage_out

    def start(*args):
        first, _, _, stage_in, _ = copies(*args)
        for cp in first + stage_in:
            cp.start()

    def forward(*args):
        first, passed, _, stage_in, stage_out = copies(*args)
        for staged, cp in zip(stage_in, stage_out):
            staged.wait()
            cp.start()
        for arrived, cp in zip(first, passed):
            arrived.wait_recv()
            cp.start()

    def finish(*args):
        first, passed, from_sibling, _, stage_out = copies(*args)
        for cp in from_sibling:
            cp.wait_recv()
        for cp in first + passed:
            cp.wait_send()
        for cp in stage_out:
            cp.wait()

    outs = [jax.ShapeDtypeStruct((N_CHIPS, *a.shape), a.dtype) for a in owns]
    return _Exchange(owns, outs, 7 * n, [(0.0, start), (forward_at, forward), (1.0, finish)],
                     staging=[pltpu.VMEM(a.shape, a.dtype) for a in owns])


def _symmetric_exchange(ins, outs, plan, result_sources=()):
    n_sems = len(plan([None] * (len(ins) + len(result_sources)), [None] * len(outs), 0, 0, 0, dry=True))

    def copies(in_refs, out_refs, send_sems, recv_sems, staging):
        x, y, c = _position()
        return [pltpu.make_async_remote_copy(src_ref=src, dst_ref=dst, send_sem=send_sems.at[k],
                                             recv_sem=recv_sems.at[k], device_id=dev, device_id_type=MESH)
                for k, (src, dst, dev) in enumerate(plan(in_refs, out_refs, x, y, c, dry=False))]

    def start(*args):
        for cp in copies(*args):
            cp.start()

    def finish(*args):
        for cp in copies(*args):
            cp.wait()

    return _Exchange(ins, outs, n_sems, [(0.0, start), (1.0, finish)], result_sources=result_sources)


def _pair_exchange(gs):
    def plan(in_refs, out_refs, x, y, c, dry):
        out = []
        for a, g in enumerate(gs):
            half = g.shape[1] // 2
            for k in range(N_CHIPS):
                out.append(None if dry else (in_refs[a].at[k, pl.ds((1 - c) * half, half), :], out_refs[a].at[k],
                                             (x, y, 1 - c)))
        return out

    outs = [jax.ShapeDtypeStruct((g.shape[0], g.shape[1] // 2, g.shape[2]), g.dtype) for g in gs]
    return _symmetric_exchange(gs, outs, plan)


def _w_in_pair_plan(slabs):
    half = W_IN_SH // 2

    def plan(in_refs, out_refs, x, y, c, dry):
        return [None if dry else (in_refs[0].at[pl.ds(k * W_IN_SH + (1 - c) * half, half), :], out_refs[0].at[j],
                                  (x, y, 1 - c)) for j, k in enumerate(slabs)]

    return plan, [jax.ShapeDtypeStruct((len(slabs), half, D_MODEL), F32)]


def _pair_exchange_w_in(slabs, w_block=None):
    plan, outs = _w_in_pair_plan(slabs)
    if w_block is None:
        return _symmetric_exchange([], outs, plan, result_sources=[0])
    return _symmetric_exchange([w_block], outs, plan)


def _pair_sum(g, from_sibling, c_arr, *, tile, name):
    n, rows, width = g.shape
    tiles = (rows // 2) // tile
    firsts = [sum(s.shape[0] for s in from_sibling[:j]) for j in range(len(from_sibling))]

    def kern(c_ref, g_ref, *rest):
        *s_refs, o_ref = rest
        k = pl.program_id(1)
        s = s_refs[0][...]
        for first, s_ref in zip(firsts[1:], s_refs[1:]):
            s = jnp.where(k >= first, s_ref[...], s)
        o_ref[...] = _bf(g_ref[...] + s)

    def sibling_spec(first, count):
        return pl.BlockSpec((None, tile, width), lambda i, k, c: (jnp.clip(k - first, 0, count - 1), i, 0))

    return pl.pallas_call(
        kern,
        grid_spec=pltpu.PrefetchScalarGridSpec(
            num_scalar_prefetch=1, grid=(tiles, n),
            in_specs=[pl.BlockSpec((None, tile, width), lambda i, k, c: (k, c[0] * tiles + i, 0))]
            + [sibling_spec(first, s.shape[0]) for first, s in zip(firsts, from_sibling)],
            out_specs=pl.BlockSpec((None, tile, width), lambda i, k, c: (k, i, 0))),
        out_shape=jax.ShapeDtypeStruct((n, rows // 2, width), BF16), name=name,
        compiler_params=_params(("parallel", "parallel")),
    )(c_arr, g, *from_sibling)


def _scatter_to_owners(hsums):
    def plan(in_refs, out_refs, x, y, c, dry):
        out = []
        for a in range(len(hsums)):
            for k, (cx, cy) in enumerate([(1 - x, y), (x, 1 - y), (1 - x, 1 - y)]):
                out.append(None if dry else (in_refs[a].at[2 * cx + cy], out_refs[a].at[k], (cx, cy, c)))
        return out

    outs = [jax.ShapeDtypeStruct((3, *h.shape[1:]), h.dtype) for h in hsums]
    return _symmetric_exchange(hsums, outs, plan)


def _sum_chips(hsum, parts, chip_arr, *, tile, name):
    n, half, width = parts.shape

    def kern(chip_ref, h_ref, p_ref, o_ref):
        acc = h_ref[...].astype(F32)
        for k in range(n):
            acc = acc + p_ref[k].astype(F32)
        o_ref[...] = acc

    return pl.pallas_call(
        kern,
        grid_spec=pltpu.PrefetchScalarGridSpec(
            num_scalar_prefetch=1, grid=(half // tile,),
            in_specs=[pl.BlockSpec((None, tile, width), lambda i, chip: (chip[0], i, 0)),
                      pl.BlockSpec((n, tile, width), lambda i, chip: (0, i, 0))],
            out_specs=pl.BlockSpec((tile, width), lambda i, chip: (i, 0))),
        out_shape=jax.ShapeDtypeStruct((half, width), F32), name=name,
        compiler_params=_params(("parallel",)),
    )(chip_arr, hsum, parts)


def _share_halves(fhalves, w_in_slabs=()):
    n = len(fhalves)
    pair_plan, pair_outs = _w_in_pair_plan(w_in_slabs)

    def plan(in_refs, out_refs, x, y, c, dry):
        share = [None if dry else (in_refs[a], out_refs[a], (x, y, 1 - c)) for a in range(n)]
        return share + (pair_plan(in_refs[n:], out_refs[n:], x, y, c, dry) if w_in_slabs else [])

    outs = [jax.ShapeDtypeStruct(f.shape, f.dtype) for f in fhalves] + (pair_outs if w_in_slabs else [])
    return _symmetric_exchange(fhalves, outs, plan, result_sources=[0] if w_in_slabs else [])


def _adamw_math(w, g, m, v):
    m = ADAM_B1 * m + (1.0 - ADAM_B1) * g
    v = ADAM_B2 * v + (1.0 - ADAM_B2) * (g * g)
    m_hat = m / (1.0 - ADAM_B1 ** ADAM_STEP)
    v_hat = v / (1.0 - ADAM_B2 ** ADAM_STEP)
    delta = -ADAM_LR * (m_hat / (jnp.sqrt(v_hat) + ADAM_EPS) + ADAM_WD * w)
    return delta, m, v


def _adamw(mats, g_mine, g_other, c_arr, *, tile, name):
    width = g_mine.shape[1]
    tiles_per_half = g_mine.shape[0] // tile
    n_tiles = [w.shape[0] // tile for w, _, _, _ in mats]
    n_mats = len(mats)

    def kern(c_ref, *refs):
        ins, outs = refs[:5 * n_mats], refs[5 * n_mats:]
        for j, (_, _, _, row_off) in enumerate(mats):
            w_ref, gm_ref, go_ref, m_ref, v_ref = ins[5 * j:5 * j + 5]
            i = jnp.minimum(pl.program_id(0), n_tiles[j] - 1)
            in_my_half = ((row_off // tile + i) // tiles_per_half) == c_ref[0]
            g = jnp.where(in_my_half, gm_ref[...], go_ref[...])
            d, nm, nv = _adamw_math(w_ref[...], g, m_ref[...], v_ref[...])
            for out_ref, val in zip(outs[4 * j:4 * j + 4], (g, d, nm, nv)):
                out_ref[...] = val

    in_specs, out_specs, out_shape, args = [], [], [], []
    for (w, m, v, row_off), nt in zip(mats, n_tiles):
        full = pl.BlockSpec((tile, width), lambda i, c, nt=nt: (jnp.minimum(i, nt - 1), 0))

        def half(mine, nt=nt, first=row_off // tile):
            def index(i, c):
                pos = first + jnp.minimum(i, nt - 1)
                used = ((pos // tiles_per_half) == c[0]) == mine
                return (jnp.where(used, pos % tiles_per_half, 0), 0)
            return pl.BlockSpec((tile, width), index)

        in_specs += [full, half(True), half(False), full, full]
        out_specs += [full] * 4
        out_shape += [jax.ShapeDtypeStruct(w.shape, F32)] * 4
        args += [w, g_mine, g_other, m, v]
    outs = pl.pallas_call(
        kern,
        grid_spec=pltpu.PrefetchScalarGridSpec(num_scalar_prefetch=1, grid=(max(n_tiles),), in_specs=in_specs,
                                               out_specs=out_specs),
        out_shape=out_shape, name=name, compiler_params=_params(("arbitrary",)),
    )(c_arr, *args)
    return [outs[4 * j:4 * j + 4] for j in range(n_mats)]


def _small_step(partials, params):
    slots = ((0, 0, D_MODEL), (1, 0, D_MODEL), (2, 0, HEAD_DIM), (2, 128, HEAD_DIM), (2, 256, N_Q_HEADS))
    loss_slot = (2, 384, 128)

    def body(*refs):
        loss_ref, dg1_ref, dg2_ref, dgq_ref, dgk_ref, dsink_ref = refs[:6]
        p_refs, out_refs = refs[6:21], refs[21:42]
        mine, gathered, send_sems, recv_sems = refs[42:]
        x, y, c = _position()
        me = 4 * x + 2 * y + c
        mine[...] = jnp.zeros_like(mine)
        for (row, lane, n), val in zip(slots + (loss_slot,), (
                jnp.sum(dg1_ref[...], axis=0, keepdims=True), jnp.sum(dg2_ref[...], axis=0, keepdims=True),
                dgq_ref[...], dgk_ref[...], dsink_ref[...], jnp.sum(loss_ref[...], axis=0, keepdims=True))):
            mine[row:row + 1, lane:lane + n] = val
        copies = []
        for k in range(1, N_DEV):
            flip = (k >> 2) & 1, (k >> 1) & 1, k & 1
            to = (x ^ flip[0], y ^ flip[1], c ^ flip[2])
            cp = pltpu.make_async_remote_copy(
                src_ref=mine, dst_ref=gathered.at[me], send_sem=send_sems.at[k - 1], recv_sem=recv_sems.at[k - 1],
                device_id=to, device_id_type=MESH)
            cp.start()
            copies.append(cp)
        gathered[me] = mine[...]
        for k in range(1, N_DEV):
            flip = (k >> 2) & 1, (k >> 1) & 1, k & 1
            src = 4 * (x ^ flip[0]) + 2 * (y ^ flip[1]) + (c ^ flip[2])
            pltpu.make_async_remote_copy(
                src_ref=mine, dst_ref=gathered.at[src], send_sem=send_sems.at[k - 1], recv_sem=recv_sems.at[k - 1],
                device_id=(x, y, c), device_id_type=MESH).wait_recv()
        for cp in copies:
            cp.wait_send()
        total = gathered[0]
        for k in range(1, N_DEV):
            total = total + gathered[k]
        row, lane, n = loss_slot
        out_refs[0][...] = total[row:row + 1, lane:lane + n]
        for i, (row, lane, n) in enumerate(slots):
            g = total[row:row + 1, lane:lane + n]
            d, nm, nv = _adamw_math(p_refs[i][...], g, p_refs[5 + i][...], p_refs[10 + i][...])
            for kind, val in enumerate((g, d, nm, nv)):
                out_refs[1 + 5 * kind + i][...] = val

    vm = pl.BlockSpec(memory_space=pltpu.VMEM)
    shapes = [jax.ShapeDtypeStruct((1, 128), F32)] + [jax.ShapeDtypeStruct((1, n), F32) for _, _, n in slots] * 4
    return pl.pallas_call(
        body, in_specs=[vm] * 21, out_specs=[vm] * 21, out_shape=shapes,
        scratch_shapes=[pltpu.VMEM((SMALL_ROWS, D_MODEL), F32), pltpu.VMEM((N_DEV, SMALL_ROWS, D_MODEL), F32),
                        pltpu.SemaphoreType.DMA((N_DEV - 1,)), pltpu.SemaphoreType.DMA((N_DEV - 1,))],
        name="small_step",
    )(*partials, *params)


def kernel(x, norm_mix_gain, w_in, q_norm_gain, k_norm_gain, attn_sinks, w_branch_attn, w_branch_ret, w_out, norm_ffn_gain, w_ffn_gate, w_ffn_up, w_ffn_down, loss_target, m_norm_mix_gain, m_w_in, m_q_norm_gain, m_k_norm_gain, m_attn_sinks, m_w_branch_attn, m_w_branch_ret, m_w_out, m_norm_ffn_gain, m_w_ffn_gate, m_w_ffn_up, m_w_ffn_down, v_norm_mix_gain, v_w_in, v_q_norm_gain, v_k_norm_gain, v_attn_sinks, v_w_branch_attn, v_w_branch_ret, v_w_out, v_norm_ffn_gain, v_w_ffn_gate, v_w_ffn_up, v_w_ffn_down):
    my_chip = 2 * lax.axis_index("x") + lax.axis_index("y")
    c_arr = lax.axis_index("c").astype(jnp.int32).reshape(1)
    chip_arr = my_chip.astype(jnp.int32).reshape(1)
    x_t, target = x[0], loss_target[0]
    g1, g2, gq, gk, sinks = norm_mix_gain, norm_ffn_gain, q_norm_gain, k_norm_gain, attn_sinks

    tr = lambda a: jnp.transpose(a[0])
    own_w_in = _bf(tr(w_in))
    own_rest = [_bf(a) for a in (tr(w_ffn_gate), tr(w_ffn_up), w_ffn_down[0], w_branch_attn[0], w_branch_ret[0],
                                 w_out[0])]
    tables, (got_w_in,) = _ret_tables(x_t.shape[0], _gather_exchange([own_w_in], 0.9))
    w_in_t = got_w_in.reshape(D_IN, D_MODEL)
    h1, q_a, kv_a, q_r, k_r, v_r, g_r, z_a, z_r, *got_rest = _proj_fwd(x_t, g1, w_in_t, _gather_exchange(own_rest, 0.8))
    wg_t, wu_t, wd, wba, wbr, wout = [got.reshape(-1, D_MODEL) for got in got_rest]

    gq_col, gk_col = gq.reshape(HEAD_DIM, 1), gk.reshape(HEAD_DIM, 1)
    attn, probs, sink_probs, o_ret, ret, states = _fused(
        [_attn_fwd(q_a, kv_a, gq_col, gk, sinks), _ret_fwd(q_r, k_r, v_r, g_r, tables)],
        grid=(x_t.shape[0] // BLOCK,), name="mixers_fwd")
    ba, br, merged, x1, h2 = _mix_fwd(attn, ret, z_a, z_r, x_t, wba, wbr, wout, g2)
    act, dgate, dup, dyb, dx1, dx1b, loss_p, dg2_p = _ffn_fwd_bwd(h2, x1, target, wg_t, wu_t, wd, g2)

    def pairs(row0, rows):
        return lambda i: [(h * rows, rows, (2 * i + h, pl.ds(row0, rows), slice(None))) for h in range(2)]

    f_block = jax.ShapeDtypeStruct((N_CHIPS, 3 * FF_SH, D_MODEL), F32)
    f_block, = _dw(dgate, h2, tm=2 * FF_SH, place=pairs(0, FF_SH), buf=f_block, name="dw_gate")
    f_block, = _dw(dup, h2, tm=2 * FF_SH, place=pairs(FF_SH, FF_SH), buf=f_block, name="dw_up")
    f_block, = _dw(act, dyb, tm=2 * FF_SH, place=pairs(2 * FF_SH, FF_SH), buf=f_block, name="dw_down")
    (dba, dbr, d_attn, d_o, d_gz, sib_ffn) = _mix_bwd(
        dx1b, z_a, z_r, ba, br, g_r, o_ret, wout, wba, wbr, _pair_exchange([f_block]))
    f_sum = _pair_sum(f_block, [sib_ffn], c_arr, tile=528, name="pair_sum_ffn")

    def quarters(row0, rows):
        return lambda i: [(k * rows, rows, (k, pl.ds(row0, rows), slice(None))) for k in range(N_CHIPS)]

    m_block = jax.ShapeDtypeStruct((N_CHIPS, D_MODEL, D_MODEL), F32)
    m_block, = _dw(attn, dba, tm=ATT_Q, place=quarters(0, 256), buf=m_block, name="dw_ba")
    m_block, = _dw(ret, dbr, tm=D_MODEL, place=pairs(256, 512), buf=m_block, name="dw_br")
    m_block, = _dw(merged, dx1b, tm=D_MODEL, place=quarters(768, 256), buf=m_block, name="dw_out")

    def w_in_rows(off, w):
        tm = min(w, D_MODEL)
        return dict(tm=tm, place=lambda i: [(0, tm, (pl.ds(off + i * tm, tm), slice(None)))])

    w_block = jax.ShapeDtypeStruct((D_IN, D_MODEL), F32)
    w_block, sib_mix = _dw(d_gz, h1, buf=w_block, name="dw_in_gz", exchange=_pair_exchange([m_block]),
                           **w_in_rows(P_GR[0], d_gz.shape[1]))
    m_sum = _pair_sum(m_block, [sib_mix], c_arr, tile=256, name="pair_sum_mix")

    (dq_a, dkv_a, dgq, dgk, dsinks, d_ret, got_ffn_sums, got_mix_sums) = _fused(
        [_attn_bwd(q_a, kv_a, d_attn, probs, sink_probs, gq_col, gk, gk_col),
         _ret_bwd(q_r, k_r, v_r, d_o, states, tables)],
        grid=(x_t.shape[0] // BLOCK + 1,), name="mixers_bwd", exchange=_scatter_to_owners([f_sum, m_sum]))
    dgq = dgq.reshape(1, HEAD_DIM)
    ffn_half = _sum_chips(f_sum, got_ffn_sums, chip_arr, tile=528, name="sum_chips_ffn")
    mix_half = _sum_chips(m_sum, got_mix_sums, chip_arr, tile=256, name="sum_chips_mix")
    w_block, ffn_other, mix_other, sib_w_in_3 = _dw(
        d_ret, h1, buf=w_block, name="dw_in_ret", exchange=_share_halves([ffn_half, mix_half], w_in_slabs=(3,)),
        **w_in_rows(P_QR[0], d_ret.shape[1]))
    w_block, sib_w_in_12 = _dw(dq_a, h1, buf=w_block, name="dw_in_q", exchange=_pair_exchange_w_in((1, 2)),
                               **w_in_rows(*P_QA))
    w_block, = _dw(dkv_a, h1, buf=w_block, name="dw_in_kv", **w_in_rows(*P_KVA))
    sib_w_in_0, = _run_exchange(_pair_exchange_w_in((0,), w_block), "pair_exchange_w_in")
    w_sum = _pair_sum(w_block.reshape(N_CHIPS, W_IN_SH, D_MODEL), [sib_w_in_0, sib_w_in_12, sib_w_in_3], c_arr,
                      tile=592, name="pair_sum_w_in")
    d_pieces = [dq_a, dkv_a, d_ret, d_gz]
    grad_x, dg1_p, got_w_in_sums = _proj_bwd(d_pieces, x_t, dx1, w_in_t, g1, _scatter_to_owners([w_sum]))
    w_in_half = _sum_chips(w_sum, got_w_in_sums, chip_arr, tile=592, name="sum_chips_w_in")
    w_in_other, = _run_exchange(_share_halves([w_in_half]), "share_halves_w_in")

    def update(name, g_half, g_other, tile, mats):
        outs = _adamw([tuple(tr(a) if t else a[0] for a in wmv) + (off,) for _, *wmv, off, t in mats],
                      g_half, g_other, c_arr, tile=tile, name=f"adamw_{name}")
        return {key: [jnp.transpose(o) if t else o for o in res] for (key, _, _, _, _, t), res in zip(mats, outs)}

    big = {
        **update("w_in", w_in_half, w_in_other, 592, [("w_in", w_in, m_w_in, v_w_in, 0, True)]),
        **update("ffn", ffn_half, ffn_other, 176, [
            ("wg", w_ffn_gate, m_w_ffn_gate, v_w_ffn_gate, 0, True),
            ("wu", w_ffn_up, m_w_ffn_up, v_w_ffn_up, FF_SH, True),
            ("wd", w_ffn_down, m_w_ffn_down, v_w_ffn_down, 2 * FF_SH, False)]),
        **update("mix", mix_half, mix_other, 128, [
            ("wba", w_branch_attn, m_w_branch_attn, v_w_branch_attn, 0, False),
            ("wbr", w_branch_ret, m_w_branch_ret, v_w_branch_ret, 256, False),
            ("wout", w_out, m_w_out, v_w_out, 768, False)])}

    loss_row, *small = _small_step(
        [loss_p.reshape(-1, 128), dg1_p.reshape(-1, D_MODEL), dg2_p.reshape(-1, D_MODEL), dgq, dgk, dsinks],
        [norm_mix_gain, norm_ffn_gain, q_norm_gain, k_norm_gain, attn_sinks,
         m_norm_mix_gain, m_norm_ffn_gain, m_q_norm_gain, m_k_norm_gain, m_attn_sinks,
         v_norm_mix_gain, v_norm_ffn_gain, v_q_norm_gain, v_k_norm_gain, v_attn_sinks])
    loss = loss_row[0, 0]

    def leaves(i):
        b = [big[n][i][None] for n in ("w_in", "wba", "wbr", "wout", "wg", "wu", "wd")]
        s1, s2, sq, sk, ss = small[5 * i:5 * i + 5]
        return [s1, b[0], sq, sk, ss, b[1], b[2], b[3], s2, b[4], b[5], b[6]]

    return (loss, grad_x[None], *leaves(0), *leaves(1), *leaves(2), *leaves(3))
```

```python
import jax
import jax.numpy as jnp
from jax import lax
from jax.experimental import pallas as pl
from jax.experimental.pallas import tpu as pltpu

F32 = jnp.float32
BF16 = jnp.bfloat16
MESH = pl.DeviceIdType.MESH

D_MODEL = 1024
EPS = 1e-6
HEAD_DIM = 64
N_Q_HEADS = 16
N_KV_HEADS = 2
GROUP = 8
BLOCK = 128
RET_HEADS = 4
RET_QK_DIM = 256
RET_V_DIM = 512
RET_CHUNK = 128
RET_ROT_BASE = 10000.0
D_FF = 2816
ATT_Q = N_Q_HEADS * HEAD_DIM
ATT_KV = N_KV_HEADS * HEAD_DIM
RET_QK = RET_HEADS * RET_QK_DIM
RET_V = RET_HEADS * RET_V_DIM
D_IN = 9472
ADAM_LR = 0.001
ADAM_B1 = 0.9
ADAM_B2 = 0.999
ADAM_EPS = 1e-08
ADAM_WD = 0.01
ADAM_STEP = 10

N_CHIPS = 4
N_DEV = 8
VMEM_LIMIT_BYTES = 60 * 1024 * 1024

P_QA = (0, 1024)
P_KVA = (1024, 256)
P_QR = (1280, 1024)
P_KR = (2304, 1024)
P_VR = (3328, 2048)
P_GR = (5376, 2048)
P_ZA = (7424, 1024)
P_ZR = (8448, 1024)

W_IN_SH = D_IN // N_CHIPS
FF_SH = D_FF // N_CHIPS

SMALL_ROWS = 8


def _dot(a, b):
    return jnp.dot(a, b, preferred_element_type=F32)


def _dot_nt(a, b):
    return lax.dot_general(a, b, (((1,), (1,)), ((), ())), preferred_element_type=F32)


def _dot_tn(a, b):
    return lax.dot_general(a, b, (((0,), (0,)), ((), ())), preferred_element_type=F32)


def _bf(x):
    return x.astype(BF16)


def _rms_stats(x):
    r = lax.rsqrt(jnp.mean(x * x, axis=-1, keepdims=True) + EPS)
    return r, x * r


def _rms_bwd(dy, xhat, r, gain):
    u = dy * gain
    dx = r * (u - xhat * jnp.mean(u * xhat, axis=-1, keepdims=True))
    return dx, dy * xhat


def _params(sem):
    return pltpu.CompilerParams(dimension_semantics=sem, vmem_limit_bytes=VMEM_LIMIT_BYTES)


_ANY = pl.BlockSpec(memory_space=pl.ANY)


class _Exchange:
    def __init__(self, ins, outs, n_sems, phases, staging=(), result_sources=()):
        self.ins, self.outs, self.n_sems, self.phases = list(ins), list(outs), n_sems, list(phases)
        self.staging, self.result_sources = list(staging), list(result_sources)


def _pallas(kern, *, grid, in_specs, out_specs, out_shape, args, name, scratch=(), exchange=None, aliases=None):
    aliases = aliases or {}
    if exchange is None:
        return pl.pallas_call(
            kern, grid=grid, in_specs=in_specs, out_specs=out_specs, out_shape=out_shape, name=name,
            scratch_shapes=list(scratch), input_output_aliases=aliases,
            compiler_params=_params(("arbitrary",) * len(grid)))(*args)
    n_in, n_out, n_sc = len(in_specs), len(out_specs), len(scratch)
    n_xi, n_xo, n_xs = len(exchange.ins), len(exchange.outs), len(exchange.staging)
    n_steps = 1
    for g in grid:
        n_steps *= g

    def wrapped(*refs):
        ins, refs = refs[:n_in], refs[n_in:]
        x_ins, refs = refs[:n_xi], refs[n_xi:]
        outs, refs = refs[:n_out], refs[n_out:]
        x_outs, refs = refs[:n_xo], refs[n_xo:]
        scr, refs = refs[:n_sc], refs[n_sc:]
        staging, (send_sems, recv_sems) = refs[:n_xs], refs[n_xs:]
        x_ins = list(x_ins) + [outs[j] for j in exchange.result_sources]
        step = pl.program_id(0)
        for d in range(1, len(grid)):
            step = step * grid[d] + pl.program_id(d)
        for frac, fn in exchange.phases:
            at = min(int(frac * n_steps), n_steps - 1)

            @pl.when(step == at)
            def _(fn=fn):
                fn(x_ins, x_outs, send_sems, recv_sems, staging)

        kern(*ins, *outs, *scr)

    sems = [pltpu.SemaphoreType.DMA((exchange.n_sems,)), pltpu.SemaphoreType.DMA((exchange.n_sems,))]
    return pl.pallas_call(
        wrapped, grid=grid, in_specs=list(in_specs) + [_ANY] * n_xi, out_specs=list(out_specs) + [_ANY] * n_xo,
        out_shape=list(out_shape) + exchange.outs, name=name,
        scratch_shapes=list(scratch) + exchange.staging + sems, input_output_aliases=aliases,
        compiler_params=_params(("arbitrary",) * len(grid)))(*args, *exchange.ins)


def _run_exchange(exchange, name):
    def body(*refs):
        n_i, n_o = len(exchange.ins), len(exchange.outs)
        staging, (send_sems, recv_sems) = refs[n_i + n_o:-2], refs[-2:]
        for _, fn in exchange.phases:
            fn(refs[:n_i], refs[n_i:n_i + n_o], send_sems, recv_sems, staging)

    sems = [pltpu.SemaphoreType.DMA((exchange.n_sems,)), pltpu.SemaphoreType.DMA((exchange.n_sems,))]
    return pl.pallas_call(body, in_specs=[_ANY] * len(exchange.ins), out_specs=[_ANY] * len(exchange.outs),
                          out_shape=exchange.outs, scratch_shapes=exchange.staging + sems, name=name,
                          compiler_params=pltpu.CompilerParams(vmem_limit_bytes=VMEM_LIMIT_BYTES))(*exchange.ins)


def _fused(parts, *, grid, name, exchange=None):
    counts = [(len(p["in_specs"]), len(p["out_specs"]), len(p["scratch"])) for p in parts]
    n_in, n_out = sum(c[0] for c in counts), sum(c[1] for c in counts)

    def kern(*refs):
        ins, outs, scr = refs[:n_in], refs[n_in:n_in + n_out], refs[n_in + n_out:]
        i0 = o0 = s0 = 0
        for p, (ni, no, ns) in zip(parts, counts):
            p["kern"](*ins[i0:i0 + ni], *outs[o0:o0 + no], *scr[s0:s0 + ns])
            i0, o0, s0 = i0 + ni, o0 + no, s0 + ns

    cat = lambda key: [a for p in parts for a in p[key]]
    return _pallas(kern, grid=grid, in_specs=cat("in_specs"), out_specs=cat("out_specs"), out_shape=cat("out_shape"),
                   scratch=cat("scratch"), args=cat("args"), name=name, exchange=exchange)


def _row_call(body, *, tm, row_ins, res_ins, row_outs, part_outs=(), name, exchange=None):
    t = row_ins[0].shape[0]
    n_tiles = t // tm
    in_specs = [pl.BlockSpec((tm, a.shape[1]), lambda i: (i, 0)) for a in row_ins]
    in_specs += [pl.BlockSpec(a.shape, lambda i: (0, 0), pipeline_mode=pl.Buffered(1)) for a in res_ins]
    out_shape = [jax.ShapeDtypeStruct((t, w), dt) for (w, dt) in row_outs]
    out_shape += [jax.ShapeDtypeStruct((n_tiles, 1, w), F32) for w in part_outs]
    out_specs = [pl.BlockSpec((tm, w), lambda i: (i, 0)) for (w, _) in row_outs]
    out_specs += [pl.BlockSpec((1, 1, w), lambda i: (i, 0, 0)) for w in part_outs]
    n_ri, n_re, n_ro = len(row_ins), len(res_ins), len(row_outs)

    def kern(*refs):
        body(refs[:n_ri], refs[n_ri:n_ri + n_re], refs[n_ri + n_re:n_ri + n_re + n_ro], refs[n_ri + n_re + n_ro:])

    return _pallas(kern, grid=(n_tiles,), in_specs=in_specs, out_specs=out_specs, out_shape=out_shape,
                   args=[*row_ins, *res_ins], name=name, exchange=exchange)


def _proj_fwd(x, g1, w_in_t, exchange):
    pieces = ((P_QA, F32), (P_KVA, F32), (P_QR, F32), (P_KR, F32), (P_VR, BF16), (P_GR, F32), (P_ZA, F32), (P_ZR, F32))

    def body(ri, re, ro, po):
        x_t = ri[0][...]
        r, xhat = _rms_stats(x_t)
        hb = _bf(xhat * re[0][...])
        ro[0][...] = hb
        for k, ((off, w), dt) in enumerate(pieces):
            ro[1 + k][...] = _dot_nt(hb, re[1][off:off + w, :]).astype(dt)

    outs = [(D_MODEL, BF16)] + [(w, dt) for ((_, w), dt) in pieces]
    return _row_call(body, tm=256, row_ins=[x], res_ins=[g1, w_in_t], row_outs=outs, name="proj_fwd",
                     exchange=exchange)


def _mix_fwd(attn, ret, z_a, z_r, x, wba, wbr, wout, g2):
    def body(ri, re, ro, po):
        ba = _dot(ri[0][...], re[0][...])
        br = _dot(ri[1][...], re[1][...])
        m = jax.nn.sigmoid(ri[2][...]) * ba + jax.nn.sigmoid(ri[3][...]) * br
        mb = _bf(m)
        x1 = ri[4][...] + _dot(mb, re[2][...])
        r, xhat = _rms_stats(x1)
        ro[0][...] = ba
        ro[1][...] = br
        ro[2][...] = mb
        ro[3][...] = x1
        ro[4][...] = _bf(xhat * re[3][...])

    outs = [(D_MODEL, F32), (D_MODEL, F32), (D_MODEL, BF16), (D_MODEL, F32), (D_MODEL, BF16)]
    return _row_call(body, tm=512, row_ins=[attn, ret, z_a, z_r, x], res_ins=[wba, wbr, wout, g2], row_outs=outs,
                     name="mix_fwd")


def _ffn_fwd_bwd(h2, x1, target, wg_t, wu_t, wd, g2):
    def body(ri, re, ro, po):
        h2_t = ri[0][...]
        x1_t = ri[1][...]
        gate = _dot_nt(h2_t, re[0][...])
        up = _dot_nt(h2_t, re[1][...])
        sg = jax.nn.sigmoid(gate)
        sl = gate * sg
        actb = _bf(sl * up)
        ro[0][...] = actb
        y = x1_t + _dot(actb, re[2][...])
        e = y - ri[2][...]
        po[0][0] = jnp.broadcast_to(0.5 * jnp.sum(jnp.sum(e * e, axis=1, keepdims=True), axis=0, keepdims=True)
                                    * (1.0 / D_MODEL), (1, 128))
        dy = e * (1.0 / D_MODEL)
        dyb = _bf(dy)
        ro[3][...] = dyb
        dact = _dot_nt(dyb, re[2][...])
        dupb = _bf(dact * sl)
        dgateb = _bf(dact * up * (sg * (1.0 + gate * (1.0 - sg))))
        ro[1][...] = dgateb
        ro[2][...] = dupb
        dh2 = _dot(dgateb, re[0][...]) + _dot(dupb, re[1][...])
        r, xhat = _rms_stats(x1_t)
        dxn, dgain = _rms_bwd(dh2, xhat, r, re[3][...])
        dx1 = dy + dxn
        ro[4][...] = dx1
        ro[5][...] = _bf(dx1)
        po[1][0] = jnp.sum(dgain, axis=0, keepdims=True)

    outs = [(D_FF, BF16), (D_FF, BF16), (D_FF, BF16), (D_MODEL, BF16), (D_MODEL, F32), (D_MODEL, BF16)]
    return _row_call(body, tm=256, row_ins=[h2, x1, target], res_ins=[wg_t, wu_t, wd, g2], row_outs=outs,
                     part_outs=(128, D_MODEL), name="ffn_fwd_bwd")


def _mix_bwd(dx1b, z_a, z_r, ba, br, g_r, o_ret, wout, wba, wbr, exchange):
    def body(ri, re, ro, po):
        dm = _dot_nt(ri[0][...], re[0][...])
        sa = jax.nn.sigmoid(ri[1][...])
        sr = jax.nn.sigmoid(ri[2][...])
        dbab = _bf(sa * dm)
        dbrb = _bf(sr * dm)
        ro[0][...] = dbab
        ro[1][...] = dbrb
        ro[4][:, RET_V:RET_V + D_MODEL] = _bf(dm * ri[3][...] * (sa * (1.0 - sa)))
        ro[4][:, RET_V + D_MODEL:RET_V + 2 * D_MODEL] = _bf(dm * ri[4][...] * (sr * (1.0 - sr)))
        ro[2][...] = _bf(_dot_nt(dbab, re[1][...]))
        dret = _dot_nt(dbrb, re[2][...])
        for h in range(RET_HEADS):
            cols = slice(h * RET_V_DIM, (h + 1) * RET_V_DIM)
            g = ri[5][:, cols]
            r, rn = _rms_stats(ri[6][:, cols])
            sg = jax.nn.sigmoid(g)
            dret_h = dret[:, cols]
            d_rn = dret_h * (g * sg)
            ro[4][:, cols] = _bf(dret_h * rn * (sg * (1.0 + g * (1.0 - sg))))
            ro[3][:, cols] = r * (d_rn - rn * jnp.mean(d_rn * rn, axis=-1, keepdims=True))

    outs = [(D_MODEL, BF16), (D_MODEL, BF16), (ATT_Q, BF16), (RET_V, F32), (RET_V + 2 * D_MODEL, BF16)]
    return _row_call(body, tm=256, row_ins=[dx1b, z_a, z_r, ba, br, g_r, o_ret], res_ins=[wout, wba, wbr],
                     row_outs=outs, name="mix_bwd", exchange=exchange)


def _proj_bwd(d_pieces, x, dx1, w_in_t, g1, exchange):
    widths = [p.shape[1] for p in d_pieces]
    groups = [(sum(widths[:k]), w) for k, w in enumerate(widths)]
    n_p = len(groups)

    def body(ri, re, ro, po):
        dh = None
        for k, (off, w) in enumerate(groups):
            term = _dot(ri[k][...], re[0][off:off + w, :])
            dh = term if dh is None else dh + term
        r, xhat = _rms_stats(ri[n_p][...])
        dxn, dgain = _rms_bwd(dh, xhat, r, re[1][...])
        ro[0][...] = ri[n_p + 1][...] + dxn
        po[0][0] = jnp.sum(dgain, axis=0, keepdims=True)

    return _row_call(body, tm=512, row_ins=[*d_pieces, x, dx1], res_ins=[w_in_t, g1], row_outs=[(D_MODEL, F32)],
                     part_outs=(D_MODEL,), name="proj_bwd", exchange=exchange)


def _dw(a, b, *, tm, place, buf, name, exchange=None):
    t, m = a.shape
    n = b.shape[1]
    tk = min(2048, t)
    n_i, n_k = m // tm, t // tk
    fresh = isinstance(buf, jax.ShapeDtypeStruct)
    n_copies = len(place(0))

    def kern(a_ref, b_ref, *rest):
        out_ref, acc, sems = rest[-3:]
        i, k = pl.program_id(0), pl.program_id(1)
        part = _dot_tn(a_ref[...], b_ref[...])

        @pl.when(k == 0)
        def _():
            acc[i] = part

        @pl.when(k > 0)
        def _():
            acc[i] += part

        def copies(tile):
            return [pltpu.make_async_copy(acc.at[tile, pl.ds(r0, rows), :], out_ref.at[idx], sems.at[tile * n_copies + c])
                    for c, (r0, rows, idx) in enumerate(place(tile))]

        for tile in range(n_i):
            @pl.when((i == tile) & (k == n_k - 1))
            def _(tile=tile):
                for cp in copies(tile):
                    cp.start()

        @pl.when((i == n_i - 1) & (k == n_k - 1))
        def _():
            for tile in range(n_i):
                for cp in copies(tile):
                    cp.wait()

    in_specs = [pl.BlockSpec((tk, tm), lambda i, k: (k, i)), pl.BlockSpec((tk, n), lambda i, k: (k, 0))]
    shape = buf if fresh else jax.ShapeDtypeStruct(buf.shape, buf.dtype)
    return _pallas(
        kern, grid=(n_i, n_k), in_specs=in_specs + ([] if fresh else [_ANY]), out_specs=[_ANY], out_shape=[shape],
        scratch=[pltpu.VMEM((n_i, tm, n), F32), pltpu.SemaphoreType.DMA((n_i * n_copies,))],
        args=[a, b] + ([] if fresh else [buf]), aliases=None if fresh else {2: 0}, name=name, exchange=exchange)


def _heads_to_lanes(x3):
    return jnp.concatenate([x3[g] for g in range(GROUP)], axis=1)


def _lanes_to_heads(xt):
    return jnp.concatenate([xt[:, g * BLOCK:(g + 1) * BLOCK] for g in range(GROUP)], axis=0)


def _attn_queries(kvh, q_ref, gq_col):
    cols = slice(kvh * GROUP * HEAD_DIM, (kvh + 1) * GROUP * HEAD_DIM)
    q3 = q_ref[:, cols].T.reshape(GROUP, HEAD_DIM, BLOCK)
    rq = lax.rsqrt(jnp.mean(q3 * q3, axis=1, keepdims=True) + EPS)
    qhat = q3 * rq
    return qhat, rq, _heads_to_lanes(_bf(qhat * (gq_col * (HEAD_DIM ** -0.5))))


def _from_prev():
    j = lax.broadcasted_iota(jnp.int32, (BLOCK, GROUP * BLOCK), 0)
    i = lax.broadcasted_iota(jnp.int32, (BLOCK, GROUP * BLOCK), 1) & (BLOCK - 1)
    return j > i


def _attn_probs(n, kvh, qts, kvp_ref, kvc_ref, gk, sink_ref):
    kcols = slice(kvh * HEAD_DIM, (kvh + 1) * HEAD_DIM)
    k = jnp.concatenate([kvp_ref[:, kcols], kvc_ref[:, kcols]], axis=0)
    rk, khat = _rms_stats(k)
    st = _dot(_bf(khat * gk), qts)
    f = jnp.where(_from_prev(), jnp.where(n > 0, st[0:BLOCK], -1e30), st[BLOCK:2 * BLOCK])
    sink = jnp.concatenate([jnp.broadcast_to(sink_ref[0:1, kvh * GROUP + g:kvh * GROUP + g + 1], (1, BLOCK))
                            for g in range(GROUP)], axis=1)
    m = jnp.maximum(jnp.max(f, axis=0, keepdims=True), sink)
    e = jnp.exp(f - m)
    es = jnp.exp(sink - m)
    inv = 1.0 / (jnp.sum(e, axis=0, keepdims=True) + es)
    return e * inv, es * inv


def _unfold(from_prev, xf):
    return _bf(jnp.concatenate([jnp.where(from_prev, xf, 0.0), jnp.where(from_prev, 0.0, xf)], axis=0))


def _attn_fwd(q_a, kv_a, gq_col, gk, sinks):
    t = q_a.shape[0]
    nb = t // BLOCK

    def kern(q_ref, kvp_ref, kvc_ref, gq_ref, gk_ref, sink_ref, o_ref, pf_ref, ps_ref):
        n = pl.program_id(0)
        kvt = jnp.concatenate([kvp_ref[...].T, kvc_ref[...].T], axis=1)
        for kvh in range(N_KV_HEADS):
            _, _, qts = _attn_queries(kvh, q_ref, gq_ref[...])
            pf, psink = _attn_probs(n, kvh, qts, kvp_ref, kvc_ref, gk_ref[...], sink_ref)
            lanes = slice(kvh * GROUP * BLOCK, (kvh + 1) * GROUP * BLOCK)
            pf_ref[:, lanes] = pf
            ps_ref[:, lanes] = psink
            vt = _bf(kvt[ATT_KV + kvh * HEAD_DIM:ATT_KV + (kvh + 1) * HEAD_DIM, :])
            out_t = _dot(vt, _unfold(_from_prev(), pf))
            cols = slice(kvh * GROUP * HEAD_DIM, (kvh + 1) * GROUP * HEAD_DIM)
            o_ref[:, cols] = _bf(_lanes_to_heads(out_t).T)

    small = lambda a: pl.BlockSpec(a.shape, lambda n: (0, 0))
    folded = N_KV_HEADS * GROUP * BLOCK
    return dict(
        kern=kern,
        in_specs=[pl.BlockSpec((BLOCK, ATT_Q), lambda n: (n, 0)),
                  pl.BlockSpec((BLOCK, 2 * ATT_KV), lambda n: (jnp.maximum(n - 1, 0), 0)),
                  pl.BlockSpec((BLOCK, 2 * ATT_KV), lambda n: (n, 0)),
                  small(gq_col), small(gk), small(sinks)],
        out_specs=[pl.BlockSpec((BLOCK, ATT_Q), lambda n: (n, 0)), pl.BlockSpec((BLOCK, folded), lambda n: (n, 0)),
                   pl.BlockSpec((None, 1, folded), lambda n: (n, 0, 0))],
        out_shape=[jax.ShapeDtypeStruct((t, ATT_Q), BF16), jax.ShapeDtypeStruct((t, folded), F32),
                   jax.ShapeDtypeStruct((nb, 1, folded), F32)],
        scratch=[], args=[q_a, kv_a, kv_a, gq_col, gk, sinks])


def _attn_bwd(q_a, kv_a, d_attn, probs, sink_probs, gq_col, gk, gk_col):
    t = q_a.shape[0]
    nb = t // BLOCK

    def kern(q_ref, kvp_ref, kvc_ref, do_ref, pf_ref, ps_ref, gq_ref, gk_ref, gkc_ref,
             dq_ref, dkv_ref, dgq_ref, dgk_ref, dsink_ref, band_k, band_v, carry_k, carry_v):
        n = pl.program_id(0)
        gq_v = gq_ref[...]
        gk_v = gk_ref[...]

        @pl.when(n == 0)
        def _():
            carry_k[...] = jnp.zeros_like(carry_k)
            carry_v[...] = jnp.zeros_like(carry_v)
            dgq_ref[...] = jnp.zeros_like(dgq_ref)
            dgk_ref[...] = jnp.zeros_like(dgk_ref)
            dsink_ref[...] = jnp.zeros_like(dsink_ref)

        @pl.when(n == nb)
        def _():
            band_k[...] = jnp.zeros_like(band_k)
            band_v[...] = jnp.zeros_like(band_v)

        @pl.when(n < nb)
        def _():
            lane16 = lax.broadcasted_iota(jnp.int32, (1, N_Q_HEADS), 1)
            dsink = jnp.zeros((1, N_Q_HEADS), F32)
            dgq = jnp.zeros((HEAD_DIM, 1), F32)
            gk_col = gkc_ref[...]
            kvt = jnp.concatenate([kvp_ref[...].T, kvc_ref[...].T], axis=1)
            from_prev = _from_prev()
            for kvh in range(N_KV_HEADS):
                qhat, rq, qts = _attn_queries(kvh, q_ref, gq_v)
                lanes = slice(kvh * GROUP * BLOCK, (kvh + 1) * GROUP * BLOCK)
                pf = pf_ref[:, lanes]
                cols = slice(kvh * GROUP * HEAD_DIM, (kvh + 1) * GROUP * HEAD_DIM)
                vcols = slice(ATT_KV + kvh * HEAD_DIM, ATT_KV + (kvh + 1) * HEAD_DIM)
                dot = _heads_to_lanes(_bf(do_ref[:, cols].astype(F32).T.reshape(GROUP, HEAD_DIM, BLOCK)))
                vb = _bf(jnp.concatenate([kvp_ref[:, vcols], kvc_ref[:, vcols]], axis=0))
                dpt = _dot(vb, dot)
                dpf = jnp.where(from_prev, dpt[0:BLOCK], dpt[BLOCK:2 * BLOCK])
                delta = jnp.sum(pf * dpf, axis=0, keepdims=True)
                dst = _unfold(from_prev, pf * (dpf - delta))
                dsk = ps_ref[:, lanes] * delta
                for g in range(GROUP):
                    tot = jnp.sum(dsk[:, g * BLOCK:(g + 1) * BLOCK], axis=1, keepdims=True)
                    dsink = dsink - jnp.where(lane16 == kvh * GROUP + g, tot, 0.0)
                kt = kvt[kvh * HEAD_DIM:(kvh + 1) * HEAD_DIM, :]
                knt = _bf(kt * lax.rsqrt(jnp.mean(kt * kt, axis=0, keepdims=True) + EPS) * gk_col)
                dqn = (_dot(knt, dst) * (HEAD_DIM ** -0.5))
                band_k[kvh] = _dot_nt(dst, qts)
                band_v[kvh] = _dot_nt(_unfold(from_prev, pf), dot)
                dqn3 = _lanes_to_heads(dqn).reshape(GROUP, HEAD_DIM, BLOCK)
                u = dqn3 * gq_v
                dq3 = rq * (u - qhat * jnp.mean(u * qhat, axis=1, keepdims=True))
                dgq = dgq + jnp.sum(jnp.sum(dqn3 * qhat, axis=0), axis=1, keepdims=True)
                dq_ref[:, cols] = _bf(dq3.reshape(GROUP * HEAD_DIM, BLOCK).T)
            dsink_ref[...] += dsink
            dgq_ref[...] += dgq

        dgk = jnp.zeros((1, HEAD_DIM), F32)
        for kvh in range(N_KV_HEADS):
            kcols = slice(kvh * HEAD_DIM, (kvh + 1) * HEAD_DIM)
            vcols = slice(ATT_KV + kvh * HEAD_DIM, ATT_KV + (kvh + 1) * HEAD_DIM)
            dkn = carry_k[kvh] + band_k[kvh, 0:BLOCK, :]
            dv = carry_v[kvh] + band_v[kvh, 0:BLOCK, :]
            rk, khat = _rms_stats(kvp_ref[:, kcols])
            dk, dgain = _rms_bwd(dkn, khat, rk, gk_v)
            dgk = dgk + jnp.sum(dgain, axis=0, keepdims=True)
            dkv_ref[:, kcols] = _bf(dk)
            dkv_ref[:, vcols] = _bf(dv)
            carry_k[kvh] = band_k[kvh, BLOCK:2 * BLOCK, :]
            carry_v[kvh] = band_v[kvh, BLOCK:2 * BLOCK, :]
        dgk_ref[...] += dgk

    small = lambda a: pl.BlockSpec(a.shape, lambda n: (0, 0))
    last = nb - 1
    return dict(
        kern=kern,
        in_specs=[pl.BlockSpec((BLOCK, ATT_Q), lambda n: (jnp.minimum(n, last), 0)),
                  pl.BlockSpec((BLOCK, 2 * ATT_KV), lambda n: (jnp.maximum(n - 1, 0), 0)),
                  pl.BlockSpec((BLOCK, 2 * ATT_KV), lambda n: (jnp.minimum(n, last), 0)),
                  pl.BlockSpec((BLOCK, ATT_Q), lambda n: (jnp.minimum(n, last), 0)),
                  pl.BlockSpec((BLOCK, probs.shape[1]), lambda n: (jnp.minimum(n, last), 0)),
                  pl.BlockSpec((None, 1, probs.shape[1]), lambda n: (jnp.minimum(n, last), 0, 0)),
                  small(gq_col), small(gk), small(gk_col)],
        out_specs=[pl.BlockSpec((BLOCK, ATT_Q), lambda n: (jnp.minimum(n, last), 0)),
                   pl.BlockSpec((BLOCK, 2 * ATT_KV), lambda n: (jnp.maximum(n - 1, 0), 0)),
                   pl.BlockSpec((HEAD_DIM, 1), lambda n: (0, 0)),
                   pl.BlockSpec((1, HEAD_DIM), lambda n: (0, 0)),
                   pl.BlockSpec((1, N_Q_HEADS), lambda n: (0, 0))],
        out_shape=[jax.ShapeDtypeStruct((t, ATT_Q), BF16), jax.ShapeDtypeStruct((t, 2 * ATT_KV), BF16),
                   jax.ShapeDtypeStruct((HEAD_DIM, 1), F32), jax.ShapeDtypeStruct((1, HEAD_DIM), F32),
                   jax.ShapeDtypeStruct((1, N_Q_HEADS), F32)],
        scratch=[pltpu.VMEM((N_KV_HEADS, 2 * BLOCK, HEAD_DIM), F32),
                 pltpu.VMEM((N_KV_HEADS, 2 * BLOCK, HEAD_DIM), F32),
                 pltpu.VMEM((N_KV_HEADS, BLOCK, HEAD_DIM), F32),
                 pltpu.VMEM((N_KV_HEADS, BLOCK, HEAD_DIM), F32)],
        args=[q_a, kv_a, kv_a, d_attn, probs, sink_probs, gq_col, gk, gk_col])


def _ret_tables(t, exchange):
    theta = 1.0 / (RET_ROT_BASE ** jnp.linspace(0.0, 1.0, RET_QK_DIM // 2, dtype=F32))
    theta2 = jnp.repeat(theta, 2)[None, :]
    sign = jnp.tile(jnp.array([-1.0, 1.0], F32), RET_QK_DIM // 2)[None, :]

    def kern(theta_ref, sign_ref, cos_ref, sin_ref):
        first = pl.program_id(0) * RET_CHUNK
        pos = (first + lax.broadcasted_iota(jnp.int32, (RET_CHUNK, RET_QK_DIM), 0)).astype(F32)
        ang = pos * theta_ref[...]
        cos_ref[...] = jnp.cos(ang)
        sin_ref[...] = jnp.sin(ang) * sign_ref[...]

    row = pl.BlockSpec((1, RET_QK_DIM), lambda n: (0, 0))
    blk = pl.BlockSpec((RET_CHUNK, RET_QK_DIM), lambda n: (n, 0))
    cos, sin_s, *got = _pallas(kern, grid=(t // RET_CHUNK,), in_specs=[row, row], out_specs=[blk, blk],
                               out_shape=[jax.ShapeDtypeStruct((t, RET_QK_DIM), F32)] * 2, args=[theta2, sign],
                               name="position_tables", exchange=exchange)
    log_gamma = jnp.log(1.0 - 2.0 ** (-5.0 - jnp.arange(RET_HEADS, dtype=F32)))
    i = jnp.arange(RET_CHUNK, dtype=F32)
    diff = i[:, None] - i[None, :]
    causal = diff >= 0
    decay = jnp.where(causal[None], jnp.exp(jnp.where(causal, diff, 0.0)[None] * log_gamma[:, None, None]), 0.0)
    xi = jnp.exp((i + 1.0)[None, :] * log_gamma[:, None])[:, :, None]
    zeta = jnp.exp((RET_CHUNK - 1.0 - i)[None, :] * log_gamma[:, None])[:, :, None]
    gch = jnp.broadcast_to(jnp.exp(RET_CHUNK * log_gamma)[:, None, None], (RET_HEADS, 1, 128))
    return (cos, sin_s, decay, xi, zeta, gch), got


def _swap_pairs(x):
    lane = lax.broadcasted_iota(jnp.int32, x.shape, 1)
    return jnp.where((lane & 1) == 0, pltpu.roll(x, RET_QK_DIM - 1, 1), pltpu.roll(x, 1, 1))


def _rotate(x, cos, sin_s):
    return x * cos + _swap_pairs(x) * sin_s


def _rotate_bwd(dy, cos, sin_s):
    return dy * cos + _swap_pairs(dy * sin_s)


def _ret_specs(order):
    qk = pl.BlockSpec((RET_CHUNK, RET_QK), lambda j: (order(j), 0))
    v = pl.BlockSpec((RET_CHUNK, RET_V), lambda j: (order(j), 0))
    dec = pl.BlockSpec((RET_HEADS, RET_CHUNK, RET_CHUNK), lambda j: (0, 0, 0))
    col = pl.BlockSpec((RET_HEADS, RET_CHUNK, 1), lambda j: (0, 0, 0))
    gch = pl.BlockSpec((RET_HEADS, 1, 128), lambda j: (0, 0, 0))
    st = pl.BlockSpec((RET_HEADS, None, RET_QK_DIM, RET_V_DIM), lambda j: (0, order(j), 0, 0))
    pos = pl.BlockSpec((RET_CHUNK, RET_QK_DIM), lambda j: (order(j), 0))
    return qk, v, dec, col, gch, st, pos


def _ret_fwd(q_r, k_r, v_r, g_r, tables):
    t = q_r.shape[0]
    nc = t // RET_CHUNK
    cos, sin_s, decay, xi, zeta, gch = tables

    def kern(q_ref, k_ref, v_ref, g_ref, cos_ref, sin_ref, dec_ref, xi_ref, zeta_ref, gch_ref,
             o_ref, ret_ref, st_ref, state):
        @pl.when(pl.program_id(0) == 0)
        def _():
            state[...] = jnp.zeros_like(state)

        cos_t = cos_ref[...]
        sin_t = sin_ref[...]
        for h in range(RET_HEADS):
            qc = slice(h * RET_QK_DIM, (h + 1) * RET_QK_DIM)
            vc = slice(h * RET_V_DIM, (h + 1) * RET_V_DIM)
            qs = _bf(_rotate(q_ref[:, qc], cos_t, sin_t))
            ks = _rotate(k_ref[:, qc] * (RET_QK_DIM ** -0.5), cos_t, sin_t)
            vb = v_ref[:, vc]
            s_old = state[h]
            sb = _bf(s_old)
            st_ref[h] = sb
            inner = _dot_nt(qs, _bf(ks)) * dec_ref[h]
            out = _dot(_bf(inner), vb) + _dot(qs, sb) * xi_ref[h]
            state[h] = gch_ref[h, :, 0:1] * s_old + _dot_tn(_bf(ks * zeta_ref[h]), vb)
            o_ref[:, vc] = out
            r, rn = _rms_stats(out)
            g = g_ref[:, vc]
            ret_ref[:, vc] = _bf(g * jax.nn.sigmoid(g) * rn)

    qk, v, dec, col, gsp, st, pos = _ret_specs(lambda j: j)
    return dict(
        kern=kern,
        in_specs=[qk, qk, v, v, pos, pos, dec, col, col, gsp],
        out_specs=[v, v, st],
        out_shape=[jax.ShapeDtypeStruct((t, RET_V), F32), jax.ShapeDtypeStruct((t, RET_V), BF16),
                   jax.ShapeDtypeStruct((RET_HEADS, nc, RET_QK_DIM, RET_V_DIM), BF16)],
        scratch=[pltpu.VMEM((RET_HEADS, RET_QK_DIM, RET_V_DIM), F32)],
        args=[q_r, k_r, v_r, g_r, cos, sin_s, decay, xi, zeta, gch])


def _ret_bwd(q_r, k_r, v_r, d_o, states, tables):
    t = q_r.shape[0]
    nc = t // RET_CHUNK
    cos, sin_s, decay, xi, zeta, gch = tables

    def kern(q_ref, k_ref, v_ref, do_ref, st_ref, cos_ref, sin_ref, dec_ref, xi_ref, zeta_ref, gch_ref,
             d_ref, dstate):
        dq_ref, dk_ref = d_ref.at[:, 0:RET_QK], d_ref.at[:, RET_QK:2 * RET_QK]
        dv_ref = d_ref.at[:, 2 * RET_QK:2 * RET_QK + RET_V]

        @pl.when(pl.program_id(0) == 0)
        def _():
            dstate[...] = jnp.zeros_like(dstate)

        @pl.when(pl.program_id(0) < nc)
        def _():
            cos_t = cos_ref[...]
            sin_t = sin_ref[...]
            scale = RET_QK_DIM ** -0.5
            for h in range(RET_HEADS):
                qc = slice(h * RET_QK_DIM, (h + 1) * RET_QK_DIM)
                vc = slice(h * RET_V_DIM, (h + 1) * RET_V_DIM)
                qs = _bf(_rotate(q_ref[:, qc], cos_t, sin_t))
                ks = _rotate(k_ref[:, qc] * scale, cos_t, sin_t)
                ksb = _bf(ks)
                vb = v_ref[:, vc]
                d_o_t = do_ref[:, vc]
                dob = _bf(d_o_t)
                doxb = _bf(d_o_t * xi_ref[h])
                dec = dec_ref[h]
                ds_old = dstate[h]
                dsb = _bf(ds_old)
                pb = _bf(_dot_nt(qs, ksb) * dec)
                dpb = _bf(_dot_nt(dob, vb) * dec)
                dqs = _dot(dpb, ksb) + _dot_nt(doxb, st_ref[h])
                dks = _dot_tn(dpb, qs) + _dot_nt(vb, dsb) * zeta_ref[h]
                dv_ref[:, vc] = _bf(_dot_tn(pb, dob) + _dot(_bf(ks * zeta_ref[h]), dsb))
                dstate[h] = gch_ref[h, :, 0:1] * ds_old + _dot_tn(qs, doxb)
                dq_ref[:, qc] = _bf(_rotate_bwd(dqs, cos_t, sin_t))
                dk_ref[:, qc] = _bf(_rotate_bwd(dks, cos_t, sin_t) * scale)

    backwards = lambda j: jnp.maximum(nc - 1 - j, 0)
    qk, v, dec, col, gsp, st, pos = _ret_specs(backwards)
    return dict(
        kern=kern,
        in_specs=[qk, qk, v, v, st, pos, pos, dec, col, col, gsp],
        out_specs=[pl.BlockSpec((RET_CHUNK, 2 * RET_QK + RET_V), lambda j: (backwards(j), 0))],
        out_shape=[jax.ShapeDtypeStruct((t, 2 * RET_QK + RET_V), BF16)],
        scratch=[pltpu.VMEM((RET_HEADS, RET_QK_DIM, RET_V_DIM), F32)],
        args=[q_r, k_r, v_r, d_o, states, cos, sin_s, decay, xi, zeta, gch])


def _position():
    return lax.axis_index("x"), lax.axis_index("y"), lax.axis_index("c")


def _gather_exchange(owns, forward_at):
    n = len(owns)

    def copies(ins, outs, send_sems, recv_sems, staging):
        x, y, c = _position()
        sibling = (x, y, 1 - c)
        chips = [(1 - x, y), (x, 1 - y), (1 - x, 1 - y)]
        my_chip = 2 * x + y

        def slab(a, chip, hf):
            half = owns[a].shape[0] // 2
            return outs[a].at[chip, pl.ds(hf * half, half), :]

        def copy(k, src, dst, to):
            return pltpu.make_async_remote_copy(src_ref=src, dst_ref=dst, send_sem=send_sems.at[k],
                                                recv_sem=recv_sems.at[k], device_id=to, device_id_type=MESH)

        first, passed, from_sibling, stage_in, stage_out = [], [], [], [], []
        for a in range(n):
            half = owns[a].shape[0] // 2
            for k, (cx, cy) in enumerate(chips):
                first.append(copy(6 * a + k, ins[a].at[pl.ds(c * half, half), :], slab(a, my_chip, c), (cx, cy, c)))
                landed = slab(a, 2 * cx + cy, c)
                passed.append(copy(6 * a + 3 + k, landed, landed, sibling))
                theirs = slab(a, 2 * cx + cy, 1 - c)
                from_sibling.append(copy(6 * a + 3 + k, theirs, theirs, sibling))
            stage_in.append(pltpu.make_async_copy(ins[a], staging[a], send_sems.at[6 * n + a]))
            stage_out.append(pltpu.make_async_copy(staging[a], outs[a].at[my_chip], recv_sems.at[6 * n + a]))
        return first, passed, from_sibling, stage_in, stage_out

    def start(*args):
        first, _, _, stage_in, _ = copies(*args)
        for cp in first + stage_in:
            cp.start()

    def forward(*args):
        first, passed, _, stage_in, stage_out = copies(*args)
        for staged, cp in zip(stage_in, stage_out):
            staged.wait()
            cp.start()
        for arrived, cp in zip(first, passed):
            arrived.wait_recv()
            cp.start()

    def finish(*args):
        first, passed, from_sibling, _, stage_out = copies(*args)
        for cp in from_sibling:
            cp.wait_recv()
        for cp in first + passed:
            cp.wait_send()
        for cp in stage_out:
            cp.wait()

    outs = [jax.ShapeDtypeStruct((N_CHIPS, *a.shape), a.dtype) for a in owns]
    return _Exchange(owns, outs, 7 * n, [(0.0, start), (forward_at, forward), (1.0, finish)],
                     staging=[pltpu.VMEM(a.shape, a.dtype) for a in owns])


def _symmetric_exchange(ins, outs, plan, result_sources=()):
    n_sems = len(plan([None] * (len(ins) + len(result_sources)), [None] * len(outs), 0, 0, 0, dry=True))

    def copies(in_refs, out_refs, send_sems, recv_sems, staging):
        x, y, c = _position()
        return [pltpu.make_async_remote_copy(src_ref=src, dst_ref=dst, send_sem=send_sems.at[k],
                                             recv_sem=recv_sems.at[k], device_id=dev, device_id_type=MESH)
                for k, (src, dst, dev) in enumerate(plan(in_refs, out_refs, x, y, c, dry=False))]

    def start(*args):
        for cp in copies(*args):
            cp.start()

    def finish(*args):
        for cp in copies(*args):
            cp.wait()

    return _Exchange(ins, outs, n_sems, [(0.0, start), (1.0, finish)], result_sources=result_sources)


def _pair_exchange(gs):
    def plan(in_refs, out_refs, x, y, c, dry):
        out = []
        for a, g in enumerate(gs):
            half = g.shape[1] // 2
            for k in range(N_CHIPS):
                out.append(None if dry else (in_refs[a].at[k, pl.ds((1 - c) * half, half), :], out_refs[a].at[k],
                                             (x, y, 1 - c)))
        return out

    outs = [jax.ShapeDtypeStruct((g.shape[0], g.shape[1] // 2, g.shape[2]), g.dtype) for g in gs]
    return _symmetric_exchange(gs, outs, plan)


def _w_in_pair_plan(slabs):
    half = W_IN_SH // 2

    def plan(in_refs, out_refs, x, y, c, dry):
        return [None if dry else (in_refs[0].at[pl.ds(k * W_IN_SH + (1 - c) * half, half), :], out_refs[0].at[j],
                                  (x, y, 1 - c)) for j, k in enumerate(slabs)]

    return plan, [jax.ShapeDtypeStruct((len(slabs), half, D_MODEL), F32)]


def _pair_exchange_w_in(slabs, w_block=None):
    plan, outs = _w_in_pair_plan(slabs)
    if w_block is None:
        return _symmetric_exchange([], outs, plan, result_sources=[0])
    return _symmetric_exchange([w_block], outs, plan)


def _pair_sum(g, from_sibling, c_arr, *, tile, name):
    n, rows, width = g.shape
    tiles = (rows // 2) // tile
    firsts = [sum(s.shape[0] for s in from_sibling[:j]) for j in range(len(from_sibling))]

    def kern(c_ref, g_ref, *rest):
        *s_refs, o_ref = rest
        k = pl.program_id(1)
        s = s_refs[0][...]
        for first, s_ref in zip(firsts[1:], s_refs[1:]):
            s = jnp.where(k >= first, s_ref[...], s)
        o_ref[...] = _bf(g_ref[...] + s)

    def sibling_spec(first, count):
        return pl.BlockSpec((None, tile, width), lambda i, k, c: (jnp.clip(k - first, 0, count - 1), i, 0))

    return pl.pallas_call(
        kern,
        grid_spec=pltpu.PrefetchScalarGridSpec(
            num_scalar_prefetch=1, grid=(tiles, n),
            in_specs=[pl.BlockSpec((None, tile, width), lambda i, k, c: (k, c[0] * tiles + i, 0))]
            + [sibling_spec(first, s.shape[0]) for first, s in zip(firsts, from_sibling)],
            out_specs=pl.BlockSpec((None, tile, width), lambda i, k, c: (k, i, 0))),
        out_shape=jax.ShapeDtypeStruct((n, rows // 2, width), BF16), name=name,
        compiler_params=_params(("parallel", "parallel")),
    )(c_arr, g, *from_sibling)


def _scatter_to_owners(hsums):
    def plan(in_refs, out_refs, x, y, c, dry):
        out = []
        for a in range(len(hsums)):
            for k, (cx, cy) in enumerate([(1 - x, y), (x, 1 - y), (1 - x, 1 - y)]):
                out.append(None if dry else (in_refs[a].at[2 * cx + cy], out_refs[a].at[k], (cx, cy, c)))
        return out

    outs = [jax.ShapeDtypeStruct((3, *h.shape[1:]), h.dtype) for h in hsums]
    return _symmetric_exchange(hsums, outs, plan)


def _sum_chips(hsum, parts, chip_arr, *, tile, name):
    n, half, width = parts.shape

    def kern(chip_ref, h_ref, p_ref, o_ref):
        acc = h_ref[...].astype(F32)
        for k in range(n):
            acc = acc + p_ref[k].astype(F32)
        o_ref[...] = acc

    return pl.pallas_call(
        kern,
        grid_spec=pltpu.PrefetchScalarGridSpec(
            num_scalar_prefetch=1, grid=(half // tile,),
            in_specs=[pl.BlockSpec((None, tile, width), lambda i, chip: (chip[0], i, 0)),
                      pl.BlockSpec((n, tile, width), lambda i, chip: (0, i, 0))],
            out_specs=pl.BlockSpec((tile, width), lambda i, chip: (i, 0))),
        out_shape=jax.ShapeDtypeStruct((half, width), F32), name=name,
        compiler_params=_params(("parallel",)),
    )(chip_arr, hsum, parts)


def _share_halves(fhalves, w_in_slabs=()):
    n = len(fhalves)
    pair_plan, pair_outs = _w_in_pair_plan(w_in_slabs)

    def plan(in_refs, out_refs, x, y, c, dry):
        share = [None if dry else (in_refs[a], out_refs[a], (x, y, 1 - c)) for a in range(n)]
        return share + (pair_plan(in_refs[n:], out_refs[n:], x, y, c, dry) if w_in_slabs else [])

    outs = [jax.ShapeDtypeStruct(f.shape, f.dtype) for f in fhalves] + (pair_outs if w_in_slabs else [])
    return _symmetric_exchange(fhalves, outs, plan, result_sources=[0] if w_in_slabs else [])


def _adamw_math(w, g, m, v):
    m = ADAM_B1 * m + (1.0 - ADAM_B1) * g
    v = ADAM_B2 * v + (1.0 - ADAM_B2) * (g * g)
    m_hat = m / (1.0 - ADAM_B1 ** ADAM_STEP)
    v_hat = v / (1.0 - ADAM_B2 ** ADAM_STEP)
    delta = -ADAM_LR * (m_hat / (jnp.sqrt(v_hat) + ADAM_EPS) + ADAM_WD * w)
    return delta, m, v


def _adamw(mats, g_mine, g_other, c_arr, *, tile, name):
    width = g_mine.shape[1]
    tiles_per_half = g_mine.shape[0] // tile
    n_tiles = [w.shape[0] // tile for w, _, _, _ in mats]
    n_mats = len(mats)

    def kern(c_ref, *refs):
        ins, outs = refs[:5 * n_mats], refs[5 * n_mats:]
        for j, (_, _, _, row_off) in enumerate(mats):
            w_ref, gm_ref, go_ref, m_ref, v_ref = ins[5 * j:5 * j + 5]
            i = jnp.minimum(pl.program_id(0), n_tiles[j] - 1)
            in_my_half = ((row_off // tile + i) // tiles_per_half) == c_ref[0]
            g = jnp.where(in_my_half, gm_ref[...], go_ref[...])
            d, nm, nv = _adamw_math(w_ref[...], g, m_ref[...], v_ref[...])
            for out_ref, val in zip(outs[4 * j:4 * j + 4], (g, d, nm, nv)):
                out_ref[...] = val

    in_specs, out_specs, out_shape, args = [], [], [], []
    for (w, m, v, row_off), nt in zip(mats, n_tiles):
        full = pl.BlockSpec((tile, width), lambda i, c, nt=nt: (jnp.minimum(i, nt - 1), 0))

        def half(mine, nt=nt, first=row_off // tile):
            def index(i, c):
                pos = first + jnp.minimum(i, nt - 1)
                used = ((pos // tiles_per_half) == c[0]) == mine
                return (jnp.where(used, pos % tiles_per_half, 0), 0)
            return pl.BlockSpec((tile, width), index)

        in_specs += [full, half(True), half(False), full, full]
        out_specs += [full] * 4
        out_shape += [jax.ShapeDtypeStruct(w.shape, F32)] * 4
        args += [w, g_mine, g_other, m, v]
    outs = pl.pallas_call(
        kern,
        grid_spec=pltpu.PrefetchScalarGridSpec(num_scalar_prefetch=1, grid=(max(n_tiles),), in_specs=in_specs,
                                               out_specs=out_specs),
        out_shape=out_shape, name=name, compiler_params=_params(("arbitrary",)),
    )(c_arr, *args)
    return [outs[4 * j:4 * j + 4] for j in range(n_mats)]


def _small_step(partials, params, w_in_half):
    slots = ((0, 0, D_MODEL), (1, 0, D_MODEL), (2, 0, HEAD_DIM), (2, 128, HEAD_DIM), (2, 256, N_Q_HEADS))
    loss_slot = (2, 384, 128)

    def body(*refs):
        loss_ref, dg1_ref, dg2_ref, dgq_ref, dgk_ref, dsink_ref = refs[:6]
        p_refs, half_ref, out_refs, other_ref = refs[6:21], refs[21], refs[22:43], refs[43]
        mine, gathered, send_sems, recv_sems = refs[44:]
        x, y, c = _position()
        me = 4 * x + 2 * y + c
        halves = pltpu.make_async_remote_copy(
            src_ref=half_ref, dst_ref=other_ref, send_sem=send_sems.at[N_DEV - 1], recv_sem=recv_sems.at[N_DEV - 1],
            device_id=(x, y, 1 - c), device_id_type=MESH)
        halves.start()
        mine[...] = jnp.zeros_like(mine)
        for (row, lane, n), val in zip(slots + (loss_slot,), (
                jnp.sum(dg1_ref[...], axis=0, keepdims=True), jnp.sum(dg2_ref[...], axis=0, keepdims=True),
                dgq_ref[...], dgk_ref[...], dsink_ref[...], jnp.sum(loss_ref[...], axis=0, keepdims=True))):
            mine[row:row + 1, lane:lane + n] = val
        copies = []
        for k in range(1, N_DEV):
            flip = (k >> 2) & 1, (k >> 1) & 1, k & 1
            to = (x ^ flip[0], y ^ flip[1], c ^ flip[2])
            cp = pltpu.make_async_remote_copy(
                src_ref=mine, dst_ref=gathered.at[me], send_sem=send_sems.at[k - 1], recv_sem=recv_sems.at[k - 1],
                device_id=to, device_id_type=MESH)
            cp.start()
            copies.append(cp)
        gathered[me] = mine[...]
        for k in range(1, N_DEV):
            flip = (k >> 2) & 1, (k >> 1) & 1, k & 1
            src = 4 * (x ^ flip[0]) + 2 * (y ^ flip[1]) + (c ^ flip[2])
            pltpu.make_async_remote_copy(
                src_ref=mine, dst_ref=gathered.at[src], send_sem=send_sems.at[k - 1], recv_sem=recv_sems.at[k - 1],
                device_id=(x, y, c), device_id_type=MESH).wait_recv()
        for cp in copies:
            cp.wait_send()
        total = gathered[0]
        for k in range(1, N_DEV):
            total = total + gathered[k]
        row, lane, n = loss_slot
        out_refs[0][...] = total[row:row + 1, lane:lane + n]
        for i, (row, lane, n) in enumerate(slots):
            g = total[row:row + 1, lane:lane + n]
            d, nm, nv = _adamw_math(p_refs[i][...], g, p_refs[5 + i][...], p_refs[10 + i][...])
            for kind, val in enumerate((g, d, nm, nv)):
                out_refs[1 + 5 * kind + i][...] = val
        halves.wait()

    vm = pl.BlockSpec(memory_space=pltpu.VMEM)
    shapes = [jax.ShapeDtypeStruct((1, 128), F32)] + [jax.ShapeDtypeStruct((1, n), F32) for _, _, n in slots] * 4
    return pl.pallas_call(
        body, in_specs=[vm] * 21 + [_ANY], out_specs=[vm] * 21 + [_ANY],
        out_shape=shapes + [jax.ShapeDtypeStruct(w_in_half.shape, w_in_half.dtype)],
        scratch_shapes=[pltpu.VMEM((SMALL_ROWS, D_MODEL), F32), pltpu.VMEM((N_DEV, SMALL_ROWS, D_MODEL), F32),
                        pltpu.SemaphoreType.DMA((N_DEV,)), pltpu.SemaphoreType.DMA((N_DEV,))],
        name="small_step",
    )(*partials, *params, w_in_half)


def kernel(x, norm_mix_gain, w_in, q_norm_gain, k_norm_gain, attn_sinks, w_branch_attn, w_branch_ret, w_out, norm_ffn_gain, w_ffn_gate, w_ffn_up, w_ffn_down, loss_target, m_norm_mix_gain, m_w_in, m_q_norm_gain, m_k_norm_gain, m_attn_sinks, m_w_branch_attn, m_w_branch_ret, m_w_out, m_norm_ffn_gain, m_w_ffn_gate, m_w_ffn_up, m_w_ffn_down, v_norm_mix_gain, v_w_in, v_q_norm_gain, v_k_norm_gain, v_attn_sinks, v_w_branch_attn, v_w_branch_ret, v_w_out, v_norm_ffn_gain, v_w_ffn_gate, v_w_ffn_up, v_w_ffn_down):
    my_chip = 2 * lax.axis_index("x") + lax.axis_index("y")
    c_arr = lax.axis_index("c").astype(jnp.int32).reshape(1)
    chip_arr = my_chip.astype(jnp.int32).reshape(1)
    x_t, target = x[0], loss_target[0]
    g1, g2, gq, gk, sinks = norm_mix_gain, norm_ffn_gain, q_norm_gain, k_norm_gain, attn_sinks

    tr = lambda a: jnp.transpose(a[0])
    own_w_in = _bf(tr(w_in))
    own_rest = [_bf(a) for a in (tr(w_ffn_gate), tr(w_ffn_up), w_ffn_down[0], w_branch_attn[0], w_branch_ret[0],
                                 w_out[0])]
    tables, (got_w_in,) = _ret_tables(x_t.shape[0], _gather_exchange([own_w_in], 0.9))
    w_in_t = got_w_in.reshape(D_IN, D_MODEL)
    h1, q_a, kv_a, q_r, k_r, v_r, g_r, z_a, z_r, *got_rest = _proj_fwd(x_t, g1, w_in_t, _gather_exchange(own_rest, 0.8))
    wg_t, wu_t, wd, wba, wbr, wout = [got.reshape(-1, D_MODEL) for got in got_rest]

    gq_col, gk_col = gq.reshape(HEAD_DIM, 1), gk.reshape(HEAD_DIM, 1)
    attn, probs, sink_probs, o_ret, ret, states = _fused(
        [_attn_fwd(q_a, kv_a, gq_col, gk, sinks), _ret_fwd(q_r, k_r, v_r, g_r, tables)],
        grid=(x_t.shape[0] // BLOCK,), name="mixers_fwd")
    ba, br, merged, x1, h2 = _mix_fwd(attn, ret, z_a, z_r, x_t, wba, wbr, wout, g2)
    act, dgate, dup, dyb, dx1, dx1b, loss_p, dg2_p = _ffn_fwd_bwd(h2, x1, target, wg_t, wu_t, wd, g2)

    def pairs(row0, rows):
        return lambda i: [(h * rows, rows, (2 * i + h, pl.ds(row0, rows), slice(None))) for h in range(2)]

    f_block = jax.ShapeDtypeStruct((N_CHIPS, 3 * FF_SH, D_MODEL), F32)
    f_block, = _dw(dgate, h2, tm=2 * FF_SH, place=pairs(0, FF_SH), buf=f_block, name="dw_gate")
    f_block, = _dw(dup, h2, tm=2 * FF_SH, place=pairs(FF_SH, FF_SH), buf=f_block, name="dw_up")
    f_block, = _dw(act, dyb, tm=2 * FF_SH, place=pairs(2 * FF_SH, FF_SH), buf=f_block, name="dw_down")
    (dba, dbr, d_attn, d_o, d_gz, sib_ffn) = _mix_bwd(
        dx1b, z_a, z_r, ba, br, g_r, o_ret, wout, wba, wbr, _pair_exchange([f_block]))
    f_sum = _pair_sum(f_block, [sib_ffn], c_arr, tile=528, name="pair_sum_ffn")

    def quarters(row0, rows):
        return lambda i: [(k * rows, rows, (k, pl.ds(row0, rows), slice(None))) for k in range(N_CHIPS)]

    m_block = jax.ShapeDtypeStruct((N_CHIPS, D_MODEL, D_MODEL), F32)
    m_block, = _dw(attn, dba, tm=ATT_Q, place=quarters(0, 256), buf=m_block, name="dw_ba")
    m_block, = _dw(ret, dbr, tm=D_MODEL, place=pairs(256, 512), buf=m_block, name="dw_br")
    m_block, = _dw(merged, dx1b, tm=D_MODEL, place=quarters(768, 256), buf=m_block, name="dw_out")

    def w_in_rows(off, w):
        tm = min(w, D_MODEL)
        return dict(tm=tm, place=lambda i: [(0, tm, (pl.ds(off + i * tm, tm), slice(None)))])

    w_block = jax.ShapeDtypeStruct((D_IN, D_MODEL), F32)
    w_block, sib_mix = _dw(d_gz, h1, buf=w_block, name="dw_in_gz", exchange=_pair_exchange([m_block]),
                           **w_in_rows(P_GR[0], d_gz.shape[1]))
    m_sum = _pair_sum(m_block, [sib_mix], c_arr, tile=256, name="pair_sum_mix")

    (dq_a, dkv_a, dgq, dgk, dsinks, d_ret, got_ffn_sums, got_mix_sums) = _fused(
        [_attn_bwd(q_a, kv_a, d_attn, probs, sink_probs, gq_col, gk, gk_col),
         _ret_bwd(q_r, k_r, v_r, d_o, states, tables)],
        grid=(x_t.shape[0] // BLOCK + 1,), name="mixers_bwd", exchange=_scatter_to_owners([f_sum, m_sum]))
    dgq = dgq.reshape(1, HEAD_DIM)
    ffn_half = _sum_chips(f_sum, got_ffn_sums, chip_arr, tile=528, name="sum_chips_ffn")
    mix_half = _sum_chips(m_sum, got_mix_sums, chip_arr, tile=256, name="sum_chips_mix")
    w_block, ffn_other, mix_other, sib_w_in_3 = _dw(
        d_ret, h1, buf=w_block, name="dw_in_ret", exchange=_share_halves([ffn_half, mix_half], w_in_slabs=(3,)),
        **w_in_rows(P_QR[0], d_ret.shape[1]))
    w_block, sib_w_in_12 = _dw(dq_a, h1, buf=w_block, name="dw_in_q", exchange=_pair_exchange_w_in((1, 2)),
                               **w_in_rows(*P_QA))
    w_block, = _dw(dkv_a, h1, buf=w_block, name="dw_in_kv", **w_in_rows(*P_KVA))
    sib_w_in_0, = _run_exchange(_pair_exchange_w_in((0,), w_block), "pair_exchange_w_in")
    w_sum = _pair_sum(w_block.reshape(N_CHIPS, W_IN_SH, D_MODEL), [sib_w_in_0, sib_w_in_12, sib_w_in_3], c_arr,
                      tile=592, name="pair_sum_w_in")
    d_pieces = [dq_a, dkv_a, d_ret, d_gz]
    grad_x, dg1_p, got_w_in_sums = _proj_bwd(d_pieces, x_t, dx1, w_in_t, g1, _scatter_to_owners([w_sum]))
    w_in_half = _sum_chips(w_sum, got_w_in_sums, chip_arr, tile=592, name="sum_chips_w_in")
    loss_row, *small, w_in_other = _small_step(
        [loss_p.reshape(-1, 128), dg1_p.reshape(-1, D_MODEL), dg2_p.reshape(-1, D_MODEL), dgq, dgk, dsinks],
        [norm_mix_gain, norm_ffn_gain, q_norm_gain, k_norm_gain, attn_sinks,
         m_norm_mix_gain, m_norm_ffn_gain, m_q_norm_gain, m_k_norm_gain, m_attn_sinks,
         v_norm_mix_gain, v_norm_ffn_gain, v_q_norm_gain, v_k_norm_gain, v_attn_sinks], w_in_half)
    loss = loss_row[0, 0]

    def update(name, g_half, g_other, tile, mats):
        outs = _adamw([tuple(tr(a) if t else a[0] for a in wmv) + (off,) for _, *wmv, off, t in mats],
                      g_half, g_other, c_arr, tile=tile, name=f"adamw_{name}")
        return {key: [jnp.transpose(o) if t else o for o in res] for (key, _, _, _, _, t), res in zip(mats, outs)}

    big = {
        **update("w_in", w_in_half, w_in_other, 592, [("w_in", w_in, m_w_in, v_w_in, 0, True)]),
        **update("ffn", ffn_half, ffn_other, 176, [
            ("wg", w_ffn_gate, m_w_ffn_gate, v_w_ffn_gate, 0, True),
            ("wu", w_ffn_up, m_w_ffn_up, v_w_ffn_up, FF_SH, True),
            ("wd", w_ffn_down, m_w_ffn_down, v_w_ffn_down, 2 * FF_SH, False)]),
        **update("mix", mix_half, mix_other, 128, [
            ("wba", w_branch_attn, m_w_branch_attn, v_w_branch_attn, 0, False),
            ("wbr", w_branch_ret, m_w_branch_ret, v_w_branch_ret, 256, False),
            ("wout", w_out, m_w_out, v_w_out, 768, False)])}

    def leaves(i):
        b = [big[n][i][None] for n in ("w_in", "wba", "wbr", "wout", "wg", "wu", "wd")]
        s1, s2, sq, sk, ss = small[5 * i:5 * i + 5]
        return [s1, b[0], sq, sk, ss, b[1], b[2], b[3], s2, b[4], b[5], b[6]]

    return (loss, grad_x[None], *leaves(0), *leaves(1), *leaves(2), *leaves(3))
```

```python
import jax
import jax.numpy as jnp
from jax import lax
from jax.experimental import pallas as pl
from jax.experimental.pallas import tpu as pltpu

F32 = jnp.float32
BF16 = jnp.bfloat16
MESH = pl.DeviceIdType.MESH

D_MODEL = 1024
EPS = 1e-6
HEAD_DIM = 64
N_Q_HEADS = 16
N_KV_HEADS = 2
GROUP = 8
BLOCK = 128
RET_HEADS = 4
RET_QK_DIM = 256
RET_V_DIM = 512
RET_CHUNK = 128
RET_ROT_BASE = 10000.0
D_FF = 2816
ATT_Q = N_Q_HEADS * HEAD_DIM
ATT_KV = N_KV_HEADS * HEAD_DIM
RET_QK = RET_HEADS * RET_QK_DIM
RET_V = RET_HEADS * RET_V_DIM
D_IN = 9472
ADAM_LR = 0.001
ADAM_B1 = 0.9
ADAM_B2 = 0.999
ADAM_EPS = 1e-08
ADAM_WD = 0.01
ADAM_STEP = 10

N_CHIPS = 4
N_DEV = 8
VMEM_LIMIT_BYTES = 60 * 1024 * 1024

P_QA = (0, 1024)
P_KVA = (1024, 256)
P_QR = (1280, 1024)
P_KR = (2304, 1024)
P_VR = (3328, 2048)
P_GR = (5376, 2048)
P_ZA = (7424, 1024)
P_ZR = (8448, 1024)

W_IN_SH = D_IN // N_CHIPS
FF_SH = D_FF // N_CHIPS

SMALL_ROWS = 8


def _dot(a, b):
    return jnp.dot(a, b, preferred_element_type=F32)


def _dot_nt(a, b):
    return lax.dot_general(a, b, (((1,), (1,)), ((), ())), preferred_element_type=F32)


def _dot_tn(a, b):
    return lax.dot_general(a, b, (((0,), (0,)), ((), ())), preferred_element_type=F32)


def _bf(x):
    return x.astype(BF16)


def _rms_stats(x):
    r = lax.rsqrt(jnp.mean(x * x, axis=-1, keepdims=True) + EPS)
    return r, x * r


def _rms_bwd(dy, xhat, r, gain):
    u = dy * gain
    dx = r * (u - xhat * jnp.mean(u * xhat, axis=-1, keepdims=True))
    return dx, dy * xhat


def _params(sem):
    return pltpu.CompilerParams(dimension_semantics=sem, vmem_limit_bytes=VMEM_LIMIT_BYTES)


_ANY = pl.BlockSpec(memory_space=pl.ANY)


class _Exchange:
    def __init__(self, ins, outs, n_sems, phases, staging=(), result_sources=()):
        self.ins, self.outs, self.n_sems, self.phases = list(ins), list(outs), n_sems, list(phases)
        self.staging, self.result_sources = list(staging), list(result_sources)


def _pallas(kern, *, grid, in_specs, out_specs, out_shape, args, name, scratch=(), exchange=None, aliases=None):
    aliases = aliases or {}
    if exchange is None:
        return pl.pallas_call(
            kern, grid=grid, in_specs=in_specs, out_specs=out_specs, out_shape=out_shape, name=name,
            scratch_shapes=list(scratch), input_output_aliases=aliases,
            compiler_params=_params(("arbitrary",) * len(grid)))(*args)
    n_in, n_out, n_sc = len(in_specs), len(out_specs), len(scratch)
    n_xi, n_xo, n_xs = len(exchange.ins), len(exchange.outs), len(exchange.staging)
    n_steps = 1
    for g in grid:
        n_steps *= g

    def wrapped(*refs):
        ins, refs = refs[:n_in], refs[n_in:]
        x_ins, refs = refs[:n_xi], refs[n_xi:]
        outs, refs = refs[:n_out], refs[n_out:]
        x_outs, refs = refs[:n_xo], refs[n_xo:]
        scr, refs = refs[:n_sc], refs[n_sc:]
        staging, (send_sems, recv_sems) = refs[:n_xs], refs[n_xs:]
        x_ins = list(x_ins) + [outs[j] for j in exchange.result_sources]
        step = pl.program_id(0)
        for d in range(1, len(grid)):
            step = step * grid[d] + pl.program_id(d)
        for frac, fn in exchange.phases:
            at = min(int(frac * n_steps), n_steps - 1)

            @pl.when(step == at)
            def _(fn=fn):
                fn(x_ins, x_outs, send_sems, recv_sems, staging)

        kern(*ins, *outs, *scr)

    sems = [pltpu.SemaphoreType.DMA((exchange.n_sems,)), pltpu.SemaphoreType.DMA((exchange.n_sems,))]
    return pl.pallas_call(
        wrapped, grid=grid, in_specs=list(in_specs) + [_ANY] * n_xi, out_specs=list(out_specs) + [_ANY] * n_xo,
        out_shape=list(out_shape) + exchange.outs, name=name,
        scratch_shapes=list(scratch) + exchange.staging + sems, input_output_aliases=aliases,
        compiler_params=_params(("arbitrary",) * len(grid)))(*args, *exchange.ins)


def _run_exchange(exchange, name):
    def body(*refs):
        n_i, n_o = len(exchange.ins), len(exchange.outs)
        staging, (send_sems, recv_sems) = refs[n_i + n_o:-2], refs[-2:]
        for _, fn in exchange.phases:
            fn(refs[:n_i], refs[n_i:n_i + n_o], send_sems, recv_sems, staging)

    sems = [pltpu.SemaphoreType.DMA((exchange.n_sems,)), pltpu.SemaphoreType.DMA((exchange.n_sems,))]
    return pl.pallas_call(body, in_specs=[_ANY] * len(exchange.ins), out_specs=[_ANY] * len(exchange.outs),
                          out_shape=exchange.outs, scratch_shapes=exchange.staging + sems, name=name,
                          compiler_params=pltpu.CompilerParams(vmem_limit_bytes=VMEM_LIMIT_BYTES))(*exchange.ins)


def _fused(parts, *, grid, name, exchange=None):
    counts = [(len(p["in_specs"]), len(p["out_specs"]), len(p["scratch"])) for p in parts]
    n_in, n_out = sum(c[0] for c in counts), sum(c[1] for c in counts)

    def kern(*refs):
        ins, outs, scr = refs[:n_in], refs[n_in:n_in + n_out], refs[n_in + n_out:]
        i0 = o0 = s0 = 0
        for p, (ni, no, ns) in zip(parts, counts):
            p["kern"](*ins[i0:i0 + ni], *outs[o0:o0 + no], *scr[s0:s0 + ns])
            i0, o0, s0 = i0 + ni, o0 + no, s0 + ns

    cat = lambda key: [a for p in parts for a in p[key]]
    return _pallas(kern, grid=grid, in_specs=cat("in_specs"), out_specs=cat("out_specs"), out_shape=cat("out_shape"),
                   scratch=cat("scratch"), args=cat("args"), name=name, exchange=exchange)


def _row_call(body, *, tm, row_ins, res_ins, row_outs, part_outs=(), name, exchange=None):
    t = row_ins[0].shape[0]
    n_tiles = t // tm
    in_specs = [pl.BlockSpec((tm, a.shape[1]), lambda i: (i, 0)) for a in row_ins]
    in_specs += [pl.BlockSpec(a.shape, lambda i: (0, 0), pipeline_mode=pl.Buffered(1)) for a in res_ins]
    out_shape = [jax.ShapeDtypeStruct((t, w), dt) for (w, dt) in row_outs]
    out_shape += [jax.ShapeDtypeStruct((n_tiles, 1, w), F32) for w in part_outs]
    out_specs = [pl.BlockSpec((tm, w), lambda i: (i, 0)) for (w, _) in row_outs]
    out_specs += [pl.BlockSpec((1, 1, w), lambda i: (i, 0, 0)) for w in part_outs]
    n_ri, n_re, n_ro = len(row_ins), len(res_ins), len(row_outs)

    def kern(*refs):
        body(refs[:n_ri], refs[n_ri:n_ri + n_re], refs[n_ri + n_re:n_ri + n_re + n_ro], refs[n_ri + n_re + n_ro:])

    return _pallas(kern, grid=(n_tiles,), in_specs=in_specs, out_specs=out_specs, out_shape=out_shape,
                   args=[*row_ins, *res_ins], name=name, exchange=exchange)


def _proj_fwd(x, g1, w_in_t, exchange):
    pieces = ((P_QA, F32), (P_KVA, F32), (P_QR, F32), (P_KR, F32), (P_VR, BF16), (P_GR, F32), (P_ZA, F32), (P_ZR, F32))

    def body(ri, re, ro, po):
        x_t = ri[0][...]
        r, xhat = _rms_stats(x_t)
        hb = _bf(xhat * re[0][...])
        ro[0][...] = hb
        for k, ((off, w), dt) in enumerate(pieces):
            ro[1 + k][...] = _dot_nt(hb, re[1][off:off + w, :]).astype(dt)

    outs = [(D_MODEL, BF16)] + [(w, dt) for ((_, w), dt) in pieces]
    return _row_call(body, tm=256, row_ins=[x], res_ins=[g1, w_in_t], row_outs=outs, name="proj_fwd",
                     exchange=exchange)


def _mix_fwd(attn, ret, z_a, z_r, x, wba, wbr, wout, g2):
    def body(ri, re, ro, po):
        ba = _dot(ri[0][...], re[0][...])
        br = _dot(ri[1][...], re[1][...])
        m = jax.nn.sigmoid(ri[2][...]) * ba + jax.nn.sigmoid(ri[3][...]) * br
        mb = _bf(m)
        x1 = ri[4][...] + _dot(mb, re[2][...])
        r, xhat = _rms_stats(x1)
        ro[0][...] = ba
        ro[1][...] = br
        ro[2][...] = mb
        ro[3][...] = x1
        ro[4][...] = _bf(xhat * re[3][...])

    outs = [(D_MODEL, F32), (D_MODEL, F32), (D_MODEL, BF16), (D_MODEL, F32), (D_MODEL, BF16)]
    return _row_call(body, tm=512, row_ins=[attn, ret, z_a, z_r, x], res_ins=[wba, wbr, wout, g2], row_outs=outs,
                     name="mix_fwd")


def _ffn_fwd_bwd(h2, x1, target, wg_t, wu_t, wd, g2):
    def body(ri, re, ro, po):
        h2_t = ri[0][...]
        x1_t = ri[1][...]
        gate = _dot_nt(h2_t, re[0][...])
        up = _dot_nt(h2_t, re[1][...])
        sg = jax.nn.sigmoid(gate)
        sl = gate * sg
        actb = _bf(sl * up)
        ro[0][...] = actb
        y = x1_t + _dot(actb, re[2][...])
        e = y - ri[2][...]
        po[0][0] = jnp.broadcast_to(0.5 * jnp.sum(jnp.sum(e * e, axis=1, keepdims=True), axis=0, keepdims=True)
                                    * (1.0 / D_MODEL), (1, 128))
        dy = e * (1.0 / D_MODEL)
        dyb = _bf(dy)
        ro[3][...] = dyb
        dact = _dot_nt(dyb, re[2][...])
        dupb = _bf(dact * sl)
        dgateb = _bf(dact * up * (sg * (1.0 + gate * (1.0 - sg))))
        ro[1][...] = dgateb
        ro[2][...] = dupb
        dh2 = _dot(dgateb, re[0][...]) + _dot(dupb, re[1][...])
        r, xhat = _rms_stats(x1_t)
        dxn, dgain = _rms_bwd(dh2, xhat, r, re[3][...])
        dx1 = dy + dxn
        ro[4][...] = dx1
        ro[5][...] = _bf(dx1)
        po[1][0] = jnp.sum(dgain, axis=0, keepdims=True)

    outs = [(D_FF, BF16), (D_FF, BF16), (D_FF, BF16), (D_MODEL, BF16), (D_MODEL, F32), (D_MODEL, BF16)]
    return _row_call(body, tm=256, row_ins=[h2, x1, target], res_ins=[wg_t, wu_t, wd, g2], row_outs=outs,
                     part_outs=(128, D_MODEL), name="ffn_fwd_bwd")


def _mix_bwd(dx1b, z_a, z_r, ba, br, g_r, o_ret, wout, wba, wbr, exchange):
    def body(ri, re, ro, po):
        dm = _dot_nt(ri[0][...], re[0][...])
        sa = jax.nn.sigmoid(ri[1][...])
        sr = jax.nn.sigmoid(ri[2][...])
        dbab = _bf(sa * dm)
        dbrb = _bf(sr * dm)
        ro[0][...] = dbab
        ro[1][...] = dbrb
        ro[4][:, RET_V:RET_V + D_MODEL] = _bf(dm * ri[3][...] * (sa * (1.0 - sa)))
        ro[4][:, RET_V + D_MODEL:RET_V + 2 * D_MODEL] = _bf(dm * ri[4][...] * (sr * (1.0 - sr)))
        ro[2][...] = _bf(_dot_nt(dbab, re[1][...]))
        dret = _dot_nt(dbrb, re[2][...])
        for h in range(RET_HEADS):
            cols = slice(h * RET_V_DIM, (h + 1) * RET_V_DIM)
            g = ri[5][:, cols]
            r, rn = _rms_stats(ri[6][:, cols])
            sg = jax.nn.sigmoid(g)
            dret_h = dret[:, cols]
            d_rn = dret_h * (g * sg)
            ro[4][:, cols] = _bf(dret_h * rn * (sg * (1.0 + g * (1.0 - sg))))
            ro[3][:, cols] = r * (d_rn - rn * jnp.mean(d_rn * rn, axis=-1, keepdims=True))

    outs = [(D_MODEL, BF16), (D_MODEL, BF16), (ATT_Q, BF16), (RET_V, F32), (RET_V + 2 * D_MODEL, BF16)]
    return _row_call(body, tm=256, row_ins=[dx1b, z_a, z_r, ba, br, g_r, o_ret], res_ins=[wout, wba, wbr],
                     row_outs=outs, name="mix_bwd", exchange=exchange)


def _proj_bwd(d_pieces, x, dx1, w_in_t, g1, exchange):
    widths = [p.shape[1] for p in d_pieces]
    groups = [(sum(widths[:k]), w) for k, w in enumerate(widths)]
    n_p = len(groups)

    def body(ri, re, ro, po):
        dh = None
        for k, (off, w) in enumerate(groups):
            term = _dot(ri[k][...], re[0][off:off + w, :])
            dh = term if dh is None else dh + term
        r, xhat = _rms_stats(ri[n_p][...])
        dxn, dgain = _rms_bwd(dh, xhat, r, re[1][...])
        ro[0][...] = ri[n_p + 1][...] + dxn
        po[0][0] = jnp.sum(dgain, axis=0, keepdims=True)

    return _row_call(body, tm=512, row_ins=[*d_pieces, x, dx1], res_ins=[w_in_t, g1], row_outs=[(D_MODEL, F32)],
                     part_outs=(D_MODEL,), name="proj_bwd", exchange=exchange)


def _dw(a, b, *, tm, place, buf, name, exchange=None):
    t, m = a.shape
    n = b.shape[1]
    tk = min(2048, t)
    n_i, n_k = m // tm, t // tk
    fresh = isinstance(buf, jax.ShapeDtypeStruct)
    n_copies = len(place(0))

    def kern(a_ref, b_ref, *rest):
        out_ref, acc, sems = rest[-3:]
        i, k = pl.program_id(0), pl.program_id(1)
        part = _dot_tn(a_ref[...], b_ref[...])

        @pl.when(k == 0)
        def _():
            acc[i] = part

        @pl.when(k > 0)
        def _():
            acc[i] += part

        def copies(tile):
            return [pltpu.make_async_copy(acc.at[tile, pl.ds(r0, rows), :], out_ref.at[idx], sems.at[tile * n_copies + c])
                    for c, (r0, rows, idx) in enumerate(place(tile))]

        for tile in range(n_i):
            @pl.when((i == tile) & (k == n_k - 1))
            def _(tile=tile):
                for cp in copies(tile):
                    cp.start()

        @pl.when((i == n_i - 1) & (k == n_k - 1))
        def _():
            for tile in range(n_i):
                for cp in copies(tile):
                    cp.wait()

    in_specs = [pl.BlockSpec((tk, tm), lambda i, k: (k, i)), pl.BlockSpec((tk, n), lambda i, k: (k, 0))]
    shape = buf if fresh else jax.ShapeDtypeStruct(buf.shape, buf.dtype)
    return _pallas(
        kern, grid=(n_i, n_k), in_specs=in_specs + ([] if fresh else [_ANY]), out_specs=[_ANY], out_shape=[shape],
        scratch=[pltpu.VMEM((n_i, tm, n), F32), pltpu.SemaphoreType.DMA((n_i * n_copies,))],
        args=[a, b] + ([] if fresh else [buf]), aliases=None if fresh else {2: 0}, name=name, exchange=exchange)


def _heads_to_lanes(x3):
    return jnp.concatenate([x3[g] for g in range(GROUP)], axis=1)


def _lanes_to_heads(xt):
    return jnp.concatenate([xt[:, g * BLOCK:(g + 1) * BLOCK] for g in range(GROUP)], axis=0)


def _attn_queries(kvh, q_ref, gq_col):
    cols = slice(kvh * GROUP * HEAD_DIM, (kvh + 1) * GROUP * HEAD_DIM)
    q3 = q_ref[:, cols].T.reshape(GROUP, HEAD_DIM, BLOCK)
    rq = lax.rsqrt(jnp.mean(q3 * q3, axis=1, keepdims=True) + EPS)
    qhat = q3 * rq
    return qhat, rq, _heads_to_lanes(_bf(qhat * (gq_col * (HEAD_DIM ** -0.5))))


def _from_prev():
    j = lax.broadcasted_iota(jnp.int32, (BLOCK, GROUP * BLOCK), 0)
    i = lax.broadcasted_iota(jnp.int32, (BLOCK, GROUP * BLOCK), 1) & (BLOCK - 1)
    return j > i


def _attn_probs(n, kvh, qts, kvp_ref, kvc_ref, gk, sink_ref):
    kcols = slice(kvh * HEAD_DIM, (kvh + 1) * HEAD_DIM)
    k = jnp.concatenate([kvp_ref[:, kcols], kvc_ref[:, kcols]], axis=0)
    rk, khat = _rms_stats(k)
    st = _dot(_bf(khat * gk), qts)
    f = jnp.where(_from_prev(), jnp.where(n > 0, st[0:BLOCK], -1e30), st[BLOCK:2 * BLOCK])
    sink = jnp.concatenate([jnp.broadcast_to(sink_ref[0:1, kvh * GROUP + g:kvh * GROUP + g + 1], (1, BLOCK))
                            for g in range(GROUP)], axis=1)
    m = jnp.maximum(jnp.max(f, axis=0, keepdims=True), sink)
    e = jnp.exp(f - m)
    es = jnp.exp(sink - m)
    inv = 1.0 / (jnp.sum(e, axis=0, keepdims=True) + es)
    return e * inv, es * inv


def _unfold(from_prev, xf):
    return _bf(jnp.concatenate([jnp.where(from_prev, xf, 0.0), jnp.where(from_prev, 0.0, xf)], axis=0))


def _attn_fwd(q_a, kv_a, gq_col, gk, sinks):
    t = q_a.shape[0]
    nb = t // BLOCK

    def kern(q_ref, kvp_ref, kvc_ref, gq_ref, gk_ref, sink_ref, o_ref, pf_ref, ps_ref):
        n = pl.program_id(0)
        kvt = jnp.concatenate([kvp_ref[...].T, kvc_ref[...].T], axis=1)
        for kvh in range(N_KV_HEADS):
            _, _, qts = _attn_queries(kvh, q_ref, gq_ref[...])
            pf, psink = _attn_probs(n, kvh, qts, kvp_ref, kvc_ref, gk_ref[...], sink_ref)
            lanes = slice(kvh * GROUP * BLOCK, (kvh + 1) * GROUP * BLOCK)
            pf_ref[:, lanes] = pf
            ps_ref[:, lanes] = psink
            vt = _bf(kvt[ATT_KV + kvh * HEAD_DIM:ATT_KV + (kvh + 1) * HEAD_DIM, :])
            out_t = _dot(vt, _unfold(_from_prev(), pf))
            cols = slice(kvh * GROUP * HEAD_DIM, (kvh + 1) * GROUP * HEAD_DIM)
            o_ref[:, cols] = _bf(_lanes_to_heads(out_t).T)

    small = lambda a: pl.BlockSpec(a.shape, lambda n: (0, 0))
    folded = N_KV_HEADS * GROUP * BLOCK
    return dict(
        kern=kern,
        in_specs=[pl.BlockSpec((BLOCK, ATT_Q), lambda n: (n, 0)),
                  pl.BlockSpec((BLOCK, 2 * ATT_KV), lambda n: (jnp.maximum(n - 1, 0), 0)),
                  pl.BlockSpec((BLOCK, 2 * ATT_KV), lambda n: (n, 0)),
                  small(gq_col), small(gk), small(sinks)],
        out_specs=[pl.BlockSpec((BLOCK, ATT_Q), lambda n: (n, 0)), pl.BlockSpec((BLOCK, folded), lambda n: (n, 0)),
                   pl.BlockSpec((None, 1, folded), lambda n: (n, 0, 0))],
        out_shape=[jax.ShapeDtypeStruct((t, ATT_Q), BF16), jax.ShapeDtypeStruct((t, folded), F32),
                   jax.ShapeDtypeStruct((nb, 1, folded), F32)],
        scratch=[], args=[q_a, kv_a, kv_a, gq_col, gk, sinks])


def _attn_bwd(q_a, kv_a, d_attn, probs, sink_probs, gq_col, gk, gk_col):
    t = q_a.shape[0]
    nb = t // BLOCK

    def kern(q_ref, kvp_ref, kvc_ref, do_ref, pf_ref, ps_ref, gq_ref, gk_ref, gkc_ref,
             dq_ref, dkv_ref, dgq_ref, dgk_ref, dsink_ref, band_k, band_v, carry_k, carry_v):
        n = pl.program_id(0)
        gq_v = gq_ref[...]
        gk_v = gk_ref[...]

        @pl.when(n == 0)
        def _():
            carry_k[...] = jnp.zeros_like(carry_k)
            carry_v[...] = jnp.zeros_like(carry_v)
            dgq_ref[...] = jnp.zeros_like(dgq_ref)
            dgk_ref[...] = jnp.zeros_like(dgk_ref)
            dsink_ref[...] = jnp.zeros_like(dsink_ref)

        @pl.when(n == nb)
        def _():
            band_k[...] = jnp.zeros_like(band_k)
            band_v[...] = jnp.zeros_like(band_v)

        @pl.when(n < nb)
        def _():
            lane16 = lax.broadcasted_iota(jnp.int32, (1, N_Q_HEADS), 1)
            dsink = jnp.zeros((1, N_Q_HEADS), F32)
            dgq = jnp.zeros((HEAD_DIM, 1), F32)
            gk_col = gkc_ref[...]
            kvt = jnp.concatenate([kvp_ref[...].T, kvc_ref[...].T], axis=1)
            from_prev = _from_prev()
            for kvh in range(N_KV_HEADS):
                qhat, rq, qts = _attn_queries(kvh, q_ref, gq_v)
                lanes = slice(kvh * GROUP * BLOCK, (kvh + 1) * GROUP * BLOCK)
                pf = pf_ref[:, lanes]
                cols = slice(kvh * GROUP * HEAD_DIM, (kvh + 1) * GROUP * HEAD_DIM)
                vcols = slice(ATT_KV + kvh * HEAD_DIM, ATT_KV + (kvh + 1) * HEAD_DIM)
                dot = _heads_to_lanes(_bf(do_ref[:, cols].astype(F32).T.reshape(GROUP, HEAD_DIM, BLOCK)))
                vb = _bf(jnp.concatenate([kvp_ref[:, vcols], kvc_ref[:, vcols]], axis=0))
                dpt = _dot(vb, dot)
                dpf = jnp.where(from_prev, dpt[0:BLOCK], dpt[BLOCK:2 * BLOCK])
                delta = jnp.sum(pf * dpf, axis=0, keepdims=True)
                dst = _unfold(from_prev, pf * (dpf - delta))
                dsk = ps_ref[:, lanes] * delta
                for g in range(GROUP):
                    tot = jnp.sum(dsk[:, g * BLOCK:(g + 1) * BLOCK], axis=1, keepdims=True)
                    dsink = dsink - jnp.where(lane16 == kvh * GROUP + g, tot, 0.0)
                kt = kvt[kvh * HEAD_DIM:(kvh + 1) * HEAD_DIM, :]
                knt = _bf(kt * lax.rsqrt(jnp.mean(kt * kt, axis=0, keepdims=True) + EPS) * gk_col)
                dqn = (_dot(knt, dst) * (HEAD_DIM ** -0.5))
                band_k[kvh] = _dot_nt(dst, qts)
                band_v[kvh] = _dot_nt(_unfold(from_prev, pf), dot)
                dqn3 = _lanes_to_heads(dqn).reshape(GROUP, HEAD_DIM, BLOCK)
                u = dqn3 * gq_v
                dq3 = rq * (u - qhat * jnp.mean(u * qhat, axis=1, keepdims=True))
                dgq = dgq + jnp.sum(jnp.sum(dqn3 * qhat, axis=0), axis=1, keepdims=True)
                dq_ref[:, cols] = _bf(dq3.reshape(GROUP * HEAD_DIM, BLOCK).T)
            dsink_ref[...] += dsink
            dgq_ref[...] += dgq

        dgk = jnp.zeros((1, HEAD_DIM), F32)
        for kvh in range(N_KV_HEADS):
            kcols = slice(kvh * HEAD_DIM, (kvh + 1) * HEAD_DIM)
            vcols = slice(ATT_KV + kvh * HEAD_DIM, ATT_KV + (kvh + 1) * HEAD_DIM)
            dkn = carry_k[kvh] + band_k[kvh, 0:BLOCK, :]
            dv = carry_v[kvh] + band_v[kvh, 0:BLOCK, :]
            rk, khat = _rms_stats(kvp_ref[:, kcols])
            dk, dgain = _rms_bwd(dkn, khat, rk, gk_v)
            dgk = dgk + jnp.sum(dgain, axis=0, keepdims=True)
            dkv_ref[:, kcols] = _bf(dk)
            dkv_ref[:, vcols] = _bf(dv)
            carry_k[kvh] = band_k[kvh, BLOCK:2 * BLOCK, :]
            carry_v[kvh] = band_v[kvh, BLOCK:2 * BLOCK, :]
        dgk_ref[...] += dgk

    small = lambda a: pl.BlockSpec(a.shape, lambda n: (0, 0))
    last = nb - 1
    return dict(
        kern=kern,
        in_specs=[pl.BlockSpec((BLOCK, ATT_Q), lambda n: (jnp.minimum(n, last), 0)),
                  pl.BlockSpec((BLOCK, 2 * ATT_KV), lambda n: (jnp.maximum(n - 1, 0), 0)),
                  pl.BlockSpec((BLOCK, 2 * ATT_KV), lambda n: (jnp.minimum(n, last), 0)),
                  pl.BlockSpec((BLOCK, ATT_Q), lambda n: (jnp.minimum(n, last), 0)),
                  pl.BlockSpec((BLOCK, probs.shape[1]), lambda n: (jnp.minimum(n, last), 0)),
                  pl.BlockSpec((None, 1, probs.shape[1]), lambda n: (jnp.minimum(n, last), 0, 0)),
                  small(gq_col), small(gk), small(gk_col)],
        out_specs=[pl.BlockSpec((BLOCK, ATT_Q), lambda n: (jnp.minimum(n, last), 0)),
                   pl.BlockSpec((BLOCK, 2 * ATT_KV), lambda n: (jnp.maximum(n - 1, 0), 0)),
                   pl.BlockSpec((HEAD_DIM, 1), lambda n: (0, 0)),
                   pl.BlockSpec((1, HEAD_DIM), lambda n: (0, 0)),
                   pl.BlockSpec((1, N_Q_HEADS), lambda n: (0, 0))],
        out_shape=[jax.ShapeDtypeStruct((t, ATT_Q), BF16), jax.ShapeDtypeStruct((t, 2 * ATT_KV), BF16),
                   jax.ShapeDtypeStruct((HEAD_DIM, 1), F32), jax.ShapeDtypeStruct((1, HEAD_DIM), F32),
                   jax.ShapeDtypeStruct((1, N_Q_HEADS), F32)],
        scratch=[pltpu.VMEM((N_KV_HEADS, 2 * BLOCK, HEAD_DIM), F32),
                 pltpu.VMEM((N_KV_HEADS, 2 * BLOCK, HEAD_DIM), F32),
                 pltpu.VMEM((N_KV_HEADS, BLOCK, HEAD_DIM), F32),
                 pltpu.VMEM((N_KV_HEADS, BLOCK, HEAD_DIM), F32)],
        args=[q_a, kv_a, kv_a, d_attn, probs, sink_probs, gq_col, gk, gk_col])


def _ret_tables(t, exchange):
    theta = 1.0 / (RET_ROT_BASE ** jnp.linspace(0.0, 1.0, RET_QK_DIM // 2, dtype=F32))
    theta2 = jnp.repeat(theta, 2)[None, :]
    sign = jnp.tile(jnp.array([-1.0, 1.0], F32), RET_QK_DIM // 2)[None, :]

    def kern(theta_ref, sign_ref, cos_ref, sin_ref):
        first = pl.program_id(0) * RET_CHUNK
        pos = (first + lax.broadcasted_iota(jnp.int32, (RET_CHUNK, RET_QK_DIM), 0)).astype(F32)
        ang = pos * theta_ref[...]
        cos_ref[...] = jnp.cos(ang)
        sin_ref[...] = jnp.sin(ang) * sign_ref[...]

    row = pl.BlockSpec((1, RET_QK_DIM), lambda n: (0, 0))
    blk = pl.BlockSpec((RET_CHUNK, RET_QK_DIM), lambda n: (n, 0))
    cos, sin_s, *got = _pallas(kern, grid=(t // RET_CHUNK,), in_specs=[row, row], out_specs=[blk, blk],
                               out_shape=[jax.ShapeDtypeStruct((t, RET_QK_DIM), F32)] * 2, args=[theta2, sign],
                               name="position_tables", exchange=exchange)
    log_gamma = jnp.log(1.0 - 2.0 ** (-5.0 - jnp.arange(RET_HEADS, dtype=F32)))
    i = jnp.arange(RET_CHUNK, dtype=F32)
    diff = i[:, None] - i[None, :]
    causal = diff >= 0
    decay = jnp.where(causal[None], jnp.exp(jnp.where(causal, diff, 0.0)[None] * log_gamma[:, None, None]), 0.0)
    xi = jnp.exp((i + 1.0)[None, :] * log_gamma[:, None])[:, :, None]
    zeta = jnp.exp((RET_CHUNK - 1.0 - i)[None, :] * log_gamma[:, None])[:, :, None]
    gch = jnp.broadcast_to(jnp.exp(RET_CHUNK * log_gamma)[:, None, None], (RET_HEADS, 1, 128))
    return (cos, sin_s, decay, xi, zeta, gch), got


def _swap_pairs(x):
    lane = lax.broadcasted_iota(jnp.int32, x.shape, 1)
    return jnp.where((lane & 1) == 0, pltpu.roll(x, RET_QK_DIM - 1, 1), pltpu.roll(x, 1, 1))


def _rotate(x, cos, sin_s):
    return x * cos + _swap_pairs(x) * sin_s


def _rotate_bwd(dy, cos, sin_s):
    return dy * cos + _swap_pairs(dy * sin_s)


def _ret_specs(order):
    qk = pl.BlockSpec((RET_CHUNK, RET_QK), lambda j: (order(j), 0))
    v = pl.BlockSpec((RET_CHUNK, RET_V), lambda j: (order(j), 0))
    dec = pl.BlockSpec((RET_HEADS, RET_CHUNK, RET_CHUNK), lambda j: (0, 0, 0))
    col = pl.BlockSpec((RET_HEADS, RET_CHUNK, 1), lambda j: (0, 0, 0))
    gch = pl.BlockSpec((RET_HEADS, 1, 128), lambda j: (0, 0, 0))
    st = pl.BlockSpec((RET_HEADS, None, RET_QK_DIM, RET_V_DIM), lambda j: (0, order(j), 0, 0))
    pos = pl.BlockSpec((RET_CHUNK, RET_QK_DIM), lambda j: (order(j), 0))
    return qk, v, dec, col, gch, st, pos


def _ret_fwd(q_r, k_r, v_r, g_r, tables):
    t = q_r.shape[0]
    nc = t // RET_CHUNK
    cos, sin_s, decay, xi, zeta, gch = tables

    def kern(q_ref, k_ref, v_ref, g_ref, cos_ref, sin_ref, dec_ref, xi_ref, zeta_ref, gch_ref,
             o_ref, ret_ref, st_ref, state):
        @pl.when(pl.program_id(0) == 0)
        def _():
            state[...] = jnp.zeros_like(state)

        cos_t = cos_ref[...]
        sin_t = sin_ref[...]
        for h in range(RET_HEADS):
            qc = slice(h * RET_QK_DIM, (h + 1) * RET_QK_DIM)
            vc = slice(h * RET_V_DIM, (h + 1) * RET_V_DIM)
            qs = _bf(_rotate(q_ref[:, qc], cos_t, sin_t))
            ks = _rotate(k_ref[:, qc] * (RET_QK_DIM ** -0.5), cos_t, sin_t)
            vb = v_ref[:, vc]
            s_old = state[h]
            sb = _bf(s_old)
            st_ref[h] = sb
            inner = _dot_nt(qs, _bf(ks)) * dec_ref[h]
            out = _dot(_bf(inner), vb) + _dot(qs, sb) * xi_ref[h]
            state[h] = gch_ref[h, :, 0:1] * s_old + _dot_tn(_bf(ks * zeta_ref[h]), vb)
            o_ref[:, vc] = out
            r, rn = _rms_stats(out)
            g = g_ref[:, vc]
            ret_ref[:, vc] = _bf(g * jax.nn.sigmoid(g) * rn)

    qk, v, dec, col, gsp, st, pos = _ret_specs(lambda j: j)
    return dict(
        kern=kern,
        in_specs=[qk, qk, v, v, pos, pos, dec, col, col, gsp],
        out_specs=[v, v, st],
        out_shape=[jax.ShapeDtypeStruct((t, RET_V), F32), jax.ShapeDtypeStruct((t, RET_V), BF16),
                   jax.ShapeDtypeStruct((RET_HEADS, nc, RET_QK_DIM, RET_V_DIM), BF16)],
        scratch=[pltpu.VMEM((RET_HEADS, RET_QK_DIM, RET_V_DIM), F32)],
        args=[q_r, k_r, v_r, g_r, cos, sin_s, decay, xi, zeta, gch])


def _ret_bwd(q_r, k_r, v_r, d_o, states, tables):
    t = q_r.shape[0]
    nc = t // RET_CHUNK
    cos, sin_s, decay, xi, zeta, gch = tables

    def kern(q_ref, k_ref, v_ref, do_ref, st_ref, cos_ref, sin_ref, dec_ref, xi_ref, zeta_ref, gch_ref,
             d_ref, dstate):
        dq_ref, dk_ref = d_ref.at[:, 0:RET_QK], d_ref.at[:, RET_QK:2 * RET_QK]
        dv_ref = d_ref.at[:, 2 * RET_QK:2 * RET_QK + RET_V]

        @pl.when(pl.program_id(0) == 0)
        def _():
            dstate[...] = jnp.zeros_like(dstate)

        @pl.when(pl.program_id(0) < nc)
        def _():
            cos_t = cos_ref[...]
            sin_t = sin_ref[...]
            scale = RET_QK_DIM ** -0.5
            for h in range(RET_HEADS):
                qc = slice(h * RET_QK_DIM, (h + 1) * RET_QK_DIM)
                vc = slice(h * RET_V_DIM, (h + 1) * RET_V_DIM)
                qs = _bf(_rotate(q_ref[:, qc], cos_t, sin_t))
                ks = _rotate(k_ref[:, qc] * scale, cos_t, sin_t)
                ksb = _bf(ks)
                vb = v_ref[:, vc]
                d_o_t = do_ref[:, vc]
                dob = _bf(d_o_t)
                doxb = _bf(d_o_t * xi_ref[h])
                dec = dec_ref[h]
                ds_old = dstate[h]
                dsb = _bf(ds_old)
                pb = _bf(_dot_nt(qs, ksb) * dec)
                dpb = _bf(_dot_nt(dob, vb) * dec)
                dqs = _dot(dpb, ksb) + _dot_nt(doxb, st_ref[h])
                dks = _dot_tn(dpb, qs) + _dot_nt(vb, dsb) * zeta_ref[h]
                dv_ref[:, vc] = _bf(_dot_tn(pb, dob) + _dot(_bf(ks * zeta_ref[h]), dsb))
                dstate[h] = gch_ref[h, :, 0:1] * ds_old + _dot_tn(qs, doxb)
                dq_ref[:, qc] = _bf(_rotate_bwd(dqs, cos_t, sin_t))
                dk_ref[:, qc] = _bf(_rotate_bwd(dks, cos_t, sin_t) * scale)

    backwards = lambda j: jnp.maximum(nc - 1 - j, 0)
    qk, v, dec, col, gsp, st, pos = _ret_specs(backwards)
    return dict(
        kern=kern,
        in_specs=[qk, qk, v, v, st, pos, pos, dec, col, col, gsp],
        out_specs=[pl.BlockSpec((RET_CHUNK, 2 * RET_QK + RET_V), lambda j: (backwards(j), 0))],
        out_shape=[jax.ShapeDtypeStruct((t, 2 * RET_QK + RET_V), BF16)],
        scratch=[pltpu.VMEM((RET_HEADS, RET_QK_DIM, RET_V_DIM), F32)],
        args=[q_r, k_r, v_r, d_o, states, cos, sin_s, decay, xi, zeta, gch])


def _position():
    return lax.axis_index("x"), lax.axis_index("y"), lax.axis_index("c")


def _gather_exchange(owns, forward_at):
    n = len(owns)

    def copies(ins, outs, send_sems, recv_sems, staging):
        x, y, c = _position()
        sibling = (x, y, 1 - c)
        chips = [(1 - x, y), (x, 1 - y), (1 - x, 1 - y)]
        my_chip = 2 * x + y

        def slab(a, chip, hf):
            half = owns[a].shape[0] // 2
            return outs[a].at[chip, pl.ds(hf * half, half), :]

        def copy(k, src, dst, to):
            return pltpu.make_async_remote_copy(src_ref=src, dst_ref=dst, send_sem=send_sems.at[k],
                                                recv_sem=recv_sems.at[k], device_id=to, device_id_type=MESH)

        first, passed, from_sibling, stage_in, stage_out = [], [], [], [], []
        for a in range(n):
            half = owns[a].shape[0] // 2
            for k, (cx, cy) in enumerate(chips):
                first.append(copy(6 * a + k, ins[a].at[pl.ds(c * half, half), :], slab(a, my_chip, c), (cx, cy, c)))
                landed = slab(a, 2 * cx + cy, c)
                passed.append(copy(6 * a + 3 + k, landed, landed, sibling))
                theirs = slab(a, 2 * cx + cy, 1 - c)
                from_sibling.append(copy(6 * a + 3 + k, theirs, theirs, sibling))
            stage_in.append(pltpu.make_async_copy(ins[a], staging[a], send_sems.at[6 * n + a]))
            stage_out.append(pltpu.make_async_copy(staging[a], outs[a].at[my_chip], recv_sems.at[6 * n + a]))
        return first, passed, from_sibling, stage_in, stage_out

    def start(*args):
        first, _, _, stage_in, _ = copies(*args)
        for cp in first + stage_in:
            cp.start()

    def forward(*args):
        first, passed, _, stage_in, stage_out = copies(*args)
        for staged, cp in zip(stage_in, stage_out):
            staged.wait()
            cp.start()
        for arrived, cp in zip(first, passed):
            arrived.wait_recv()
            cp.start()

    def finish(*args):
        first, passed, from_sibling, _, stage_out = copies(*args)
        for cp in from_sibling:
            cp.wait_recv()
        for cp in first + passed:
            cp.wait_send()
        for cp in stage_out:
            cp.wait()

    outs = [jax.ShapeDtypeStruct((N_CHIPS, *a.shape), a.dtype) for a in owns]
    return _Exchange(owns, outs, 7 * n, [(0.0, start), (forward_at, forward), (1.0, finish)],
                     staging=[pltpu.VMEM(a.shape, a.dtype) for a in owns])


def _symmetric_exchange(ins, outs, plan, result_sources=()):
    n_sems = len(plan([None] * (len(ins) + len(result_sources)), [None] * len(outs), 0, 0, 0, dry=True))

    def copies(in_refs, out_refs, send_sems, recv_sems, staging):
        x, y, c = _position()
        return [pltpu.make_async_remote_copy(src_ref=src, dst_ref=dst, send_sem=send_sems.at[k],
                                             recv_sem=recv_sems.at[k], device_id=dev, device_id_type=MESH)
                for k, (src, dst, dev) in enumerate(plan(in_refs, out_refs, x, y, c, dry=False))]

    def start(*args):
        for cp in copies(*args):
            cp.start()

    def finish(*args):
        for cp in copies(*args):
            cp.wait()

    return _Exchange(ins, outs, n_sems, [(0.0, start), (1.0, finish)], result_sources=result_sources)


def _pair_exchange(gs):
    def plan(in_refs, out_refs, x, y, c, dry):
        out = []
        for a, g in enumerate(gs):
            half = g.shape[1] // 2
            for k in range(N_CHIPS):
                out.append(None if dry else (in_refs[a].at[k, pl.ds((1 - c) * half, half), :], out_refs[a].at[k],
                                             (x, y, 1 - c)))
        return out

    outs = [jax.ShapeDtypeStruct((g.shape[0], g.shape[1] // 2, g.shape[2]), g.dtype) for g in gs]
    return _symmetric_exchange(gs, outs, plan)


def _w_in_pair_plan(slabs):
    half = W_IN_SH // 2

    def plan(in_refs, out_refs, x, y, c, dry):
        return [None if dry else (in_refs[0].at[pl.ds(k * W_IN_SH + (1 - c) * half, half), :], out_refs[0].at[j],
                                  (x, y, 1 - c)) for j, k in enumerate(slabs)]

    return plan, [jax.ShapeDtypeStruct((len(slabs), half, D_MODEL), F32)]


def _pair_exchange_w_in(slabs, w_block=None):
    plan, outs = _w_in_pair_plan(slabs)
    if w_block is None:
        return _symmetric_exchange([], outs, plan, result_sources=[0])
    return _symmetric_exchange([w_block], outs, plan)


def _pair_sum(g, from_sibling, c_arr, *, tile, name):
    n, rows, width = g.shape
    tiles = (rows // 2) // tile
    firsts = [sum(s.shape[0] for s in from_sibling[:j]) for j in range(len(from_sibling))]

    def kern(c_ref, g_ref, *rest):
        *s_refs, o_ref = rest
        k = pl.program_id(1)
        s = s_refs[0][...]
        for first, s_ref in zip(firsts[1:], s_refs[1:]):
            s = jnp.where(k >= first, s_ref[...], s)
        o_ref[...] = _bf(g_ref[...] + s)

    def sibling_spec(first, count):
        return pl.BlockSpec((None, tile, width), lambda i, k, c: (jnp.clip(k - first, 0, count - 1), i, 0))

    return pl.pallas_call(
        kern,
        grid_spec=pltpu.PrefetchScalarGridSpec(
            num_scalar_prefetch=1, grid=(tiles, n),
            in_specs=[pl.BlockSpec((None, tile, width), lambda i, k, c: (k, c[0] * tiles + i, 0))]
            + [sibling_spec(first, s.shape[0]) for first, s in zip(firsts, from_sibling)],
            out_specs=pl.BlockSpec((None, tile, width), lambda i, k, c: (k, i, 0))),
        out_shape=jax.ShapeDtypeStruct((n, rows // 2, width), BF16), name=name,
        compiler_params=_params(("parallel", "parallel")),
    )(c_arr, g, *from_sibling)


def _scatter_to_owners(hsums):
    def plan(in_refs, out_refs, x, y, c, dry):
        out = []
        for a in range(len(hsums)):
            for k, (cx, cy) in enumerate([(1 - x, y), (x, 1 - y), (1 - x, 1 - y)]):
                out.append(None if dry else (in_refs[a].at[2 * cx + cy], out_refs[a].at[k], (cx, cy, c)))
        return out

    outs = [jax.ShapeDtypeStruct((3, *h.shape[1:]), h.dtype) for h in hsums]
    return _symmetric_exchange(hsums, outs, plan)


def _sum_chips(hsum, parts, chip_arr, *, tile, name):
    n, half, width = parts.shape

    def kern(chip_ref, h_ref, p_ref, o_ref):
        acc = h_ref[...].astype(F32)
        for k in range(n):
            acc = acc + p_ref[k].astype(F32)
        o_ref[...] = acc

    return pl.pallas_call(
        kern,
        grid_spec=pltpu.PrefetchScalarGridSpec(
            num_scalar_prefetch=1, grid=(half // tile,),
            in_specs=[pl.BlockSpec((None, tile, width), lambda i, chip: (chip[0], i, 0)),
                      pl.BlockSpec((n, tile, width), lambda i, chip: (0, i, 0))],
            out_specs=pl.BlockSpec((tile, width), lambda i, chip: (i, 0))),
        out_shape=jax.ShapeDtypeStruct((half, width), F32), name=name,
        compiler_params=_params(("parallel",)),
    )(chip_arr, hsum, parts)


def _share_halves(fhalves, w_in_slabs=()):
    n = len(fhalves)
    pair_plan, pair_outs = _w_in_pair_plan(w_in_slabs)

    def plan(in_refs, out_refs, x, y, c, dry):
        share = [None if dry else (in_refs[a], out_refs[a], (x, y, 1 - c)) for a in range(n)]
        return share + (pair_plan(in_refs[n:], out_refs[n:], x, y, c, dry) if w_in_slabs else [])

    outs = [jax.ShapeDtypeStruct(f.shape, f.dtype) for f in fhalves] + (pair_outs if w_in_slabs else [])
    return _symmetric_exchange(fhalves, outs, plan, result_sources=[0] if w_in_slabs else [])


def _adamw_math(w, g, m, v):
    m = ADAM_B1 * m + (1.0 - ADAM_B1) * g
    v = ADAM_B2 * v + (1.0 - ADAM_B2) * (g * g)
    m_hat = m / (1.0 - ADAM_B1 ** ADAM_STEP)
    v_hat = v / (1.0 - ADAM_B2 ** ADAM_STEP)
    delta = -ADAM_LR * (m_hat / (jnp.sqrt(v_hat) + ADAM_EPS) + ADAM_WD * w)
    return delta, m, v


def _adamw(mats, g_mine, g_other, c_arr, *, tile, name):
    width = g_mine.shape[1]
    tiles_per_half = g_mine.shape[0] // tile
    n_tiles = [w.shape[0] // tile for w, _, _, _ in mats]
    n_mats = len(mats)

    def kern(c_ref, *refs):
        ins, outs = refs[:5 * n_mats], refs[5 * n_mats:]
        for j, (_, _, _, row_off) in enumerate(mats):
            w_ref, gm_ref, go_ref, m_ref, v_ref = ins[5 * j:5 * j + 5]
            i = jnp.minimum(pl.program_id(0), n_tiles[j] - 1)
            in_my_half = ((row_off // tile + i) // tiles_per_half) == c_ref[0]
            g = jnp.where(in_my_half, gm_ref[...], go_ref[...])
            d, nm, nv = _adamw_math(w_ref[...], g, m_ref[...], v_ref[...])
            for out_ref, val in zip(outs[4 * j:4 * j + 4], (g, d, nm, nv)):
                out_ref[...] = val

    in_specs, out_specs, out_shape, args = [], [], [], []
    for (w, m, v, row_off), nt in zip(mats, n_tiles):
        full = pl.BlockSpec((tile, width), lambda i, c, nt=nt: (jnp.minimum(i, nt - 1), 0))

        def half(mine, nt=nt, first=row_off // tile):
            def index(i, c):
                pos = first + jnp.minimum(i, nt - 1)
                used = ((pos // tiles_per_half) == c[0]) == mine
                return (jnp.where(used, pos % tiles_per_half, 0), 0)
            return pl.BlockSpec((tile, width), index)

        in_specs += [full, half(True), half(False), full, full]
        out_specs += [full] * 4
        out_shape += [jax.ShapeDtypeStruct(w.shape, F32)] * 4
        args += [w, g_mine, g_other, m, v]
    outs = pl.pallas_call(
        kern,
        grid_spec=pltpu.PrefetchScalarGridSpec(num_scalar_prefetch=1, grid=(max(n_tiles),), in_specs=in_specs,
                                               out_specs=out_specs),
        out_shape=out_shape, name=name, compiler_params=_params(("arbitrary",)),
    )(c_arr, *args)
    return [outs[4 * j:4 * j + 4] for j in range(n_mats)]


def _small_step(partials, params, w_in_half):
    slots = ((0, 0, D_MODEL), (1, 0, D_MODEL), (2, 0, HEAD_DIM), (2, 128, HEAD_DIM), (2, 256, N_Q_HEADS))
    loss_slot = (2, 384, 128)

    def body(*refs):
        loss_ref, dg1_ref, dg2_ref, dgq_ref, dgk_ref, dsink_ref = refs[:6]
        p_refs, half_ref, out_refs, other_ref = refs[6:21], refs[21], refs[22:43], refs[43]
        mine, gathered, send_sems, recv_sems = refs[44:]
        x, y, c = _position()
        me = 4 * x + 2 * y + c
        halves = pltpu.make_async_remote_copy(
            src_ref=half_ref, dst_ref=other_ref, send_sem=send_sems.at[N_DEV - 1], recv_sem=recv_sems.at[N_DEV - 1],
            device_id=(x, y, 1 - c), device_id_type=MESH)
        halves.start()
        mine[...] = jnp.zeros_like(mine)
        for (row, lane, n), val in zip(slots + (loss_slot,), (
                jnp.sum(dg1_ref[...], axis=0, keepdims=True), jnp.sum(dg2_ref[...], axis=0, keepdims=True),
                dgq_ref[...], dgk_ref[...], dsink_ref[...], jnp.sum(loss_ref[...], axis=0, keepdims=True))):
            mine[row:row + 1, lane:lane + n] = val
        copies = []
        for k in range(1, N_DEV):
            flip = (k >> 2) & 1, (k >> 1) & 1, k & 1
            to = (x ^ flip[0], y ^ flip[1], c ^ flip[2])
            cp = pltpu.make_async_remote_copy(
                src_ref=mine, dst_ref=gathered.at[me], send_sem=send_sems.at[k - 1], recv_sem=recv_sems.at[k - 1],
                device_id=to, device_id_type=MESH)
            cp.start()
            copies.append(cp)
        gathered[me] = mine[...]
        for k in range(1, N_DEV):
            flip = (k >> 2) & 1, (k >> 1) & 1, k & 1
            src = 4 * (x ^ flip[0]) + 2 * (y ^ flip[1]) + (c ^ flip[2])
            pltpu.make_async_remote_copy(
                src_ref=mine, dst_ref=gathered.at[src], send_sem=send_sems.at[k - 1], recv_sem=recv_sems.at[k - 1],
                device_id=(x, y, c), device_id_type=MESH).wait_recv()
        for cp in copies:
            cp.wait_send()
        total = gathered[0]
        for k in range(1, N_DEV):
            total = total + gathered[k]
        row, lane, n = loss_slot
        out_refs[0][...] = total[row:row + 1, lane:lane + n]
        for i, (row, lane, n) in enumerate(slots):
            g = total[row:row + 1, lane:lane + n]
            d, nm, nv = _adamw_math(p_refs[i][...], g, p_refs[5 + i][...], p_refs[10 + i][...])
            for kind, val in enumerate((g, d, nm, nv)):
                out_refs[1 + 5 * kind + i][...] = val
        halves.wait()

    vm = pl.BlockSpec(memory_space=pltpu.VMEM)
    shapes = [jax.ShapeDtypeStruct((1, 128), F32)] + [jax.ShapeDtypeStruct((1, n), F32) for _, _, n in slots] * 4
    return pl.pallas_call(
        body, in_specs=[vm] * 21 + [_ANY], out_specs=[vm] * 21 + [_ANY],
        out_shape=shapes + [jax.ShapeDtypeStruct(w_in_half.shape, w_in_half.dtype)],
        scratch_shapes=[pltpu.VMEM((SMALL_ROWS, D_MODEL), F32), pltpu.VMEM((N_DEV, SMALL_ROWS, D_MODEL), F32),
                        pltpu.SemaphoreType.DMA((N_DEV,)), pltpu.SemaphoreType.DMA((N_DEV,))],
        name="small_step",
    )(*partials, *params, w_in_half)


def kernel(x, norm_mix_gain, w_in, q_norm_gain, k_norm_gain, attn_sinks, w_branch_attn, w_branch_ret, w_out, norm_ffn_gain, w_ffn_gate, w_ffn_up, w_ffn_down, loss_target, m_norm_mix_gain, m_w_in, m_q_norm_gain, m_k_norm_gain, m_attn_sinks, m_w_branch_attn, m_w_branch_ret, m_w_out, m_norm_ffn_gain, m_w_ffn_gate, m_w_ffn_up, m_w_ffn_down, v_norm_mix_gain, v_w_in, v_q_norm_gain, v_k_norm_gain, v_attn_sinks, v_w_branch_attn, v_w_branch_ret, v_w_out, v_norm_ffn_gain, v_w_ffn_gate, v_w_ffn_up, v_w_ffn_down):
    my_chip = 2 * lax.axis_index("x") + lax.axis_index("y")
    c_arr = lax.axis_index("c").astype(jnp.int32).reshape(1)
    chip_arr = my_chip.astype(jnp.int32).reshape(1)
    x_t, target = x[0], loss_target[0]
    g1, g2, gq, gk, sinks = norm_mix_gain, norm_ffn_gain, q_norm_gain, k_norm_gain, attn_sinks

    tr = lambda a: jnp.transpose(a[0])
    own_w_in = _bf(tr(w_in))
    own_rest = [_bf(a) for a in (tr(w_ffn_gate), tr(w_ffn_up), w_ffn_down[0], w_branch_attn[0], w_branch_ret[0],
                                 w_out[0])]
    tables, (got_w_in,) = _ret_tables(x_t.shape[0], _gather_exchange([own_w_in], 0.9))
    w_in_t = got_w_in.reshape(D_IN, D_MODEL)
    h1, q_a, kv_a, q_r, k_r, v_r, g_r, z_a, z_r, *got_rest = _proj_fwd(x_t, g1, w_in_t, _gather_exchange(own_rest, 0.8))
    wg_t, wu_t, wd, wba, wbr, wout = [got.reshape(-1, D_MODEL) for got in got_rest]

    gq_col, gk_col = gq.reshape(HEAD_DIM, 1), gk.reshape(HEAD_DIM, 1)
    attn, probs, sink_probs, o_ret, ret, states = _fused(
        [_attn_fwd(q_a, kv_a, gq_col, gk, sinks), _ret_fwd(q_r, k_r, v_r, g_r, tables)],
        grid=(x_t.shape[0] // BLOCK,), name="mixers_fwd")
    ba, br, merged, x1, h2 = _mix_fwd(attn, ret, z_a, z_r, x_t, wba, wbr, wout, g2)
    act, dgate, dup, dyb, dx1, dx1b, loss_p, dg2_p = _ffn_fwd_bwd(h2, x1, target, wg_t, wu_t, wd, g2)

    def pairs(row0, rows):
        return lambda i: [(h * rows, rows, (2 * i + h, pl.ds(row0, rows), slice(None))) for h in range(2)]

    f_block = jax.ShapeDtypeStruct((N_CHIPS, 3 * FF_SH, D_MODEL), F32)
    f_block, = _dw(dgate, h2, tm=2 * FF_SH, place=pairs(0, FF_SH), buf=f_block, name="dw_gate")
    f_block, = _dw(dup, h2, tm=2 * FF_SH, place=pairs(FF_SH, FF_SH), buf=f_block, name="dw_up")
    f_block, = _dw(act, dyb, tm=2 * FF_SH, place=pairs(2 * FF_SH, FF_SH), buf=f_block, name="dw_down")
    (dba, dbr, d_attn, d_o, d_gz, sib_ffn) = _mix_bwd(
        dx1b, z_a, z_r, ba, br, g_r, o_ret, wout, wba, wbr, _pair_exchange([f_block]))
    f_sum = _pair_sum(f_block, [sib_ffn], c_arr, tile=528, name="pair_sum_ffn")

    def quarters(row0, rows):
        return lambda i: [(k * rows, rows, (k, pl.ds(row0, rows), slice(None))) for k in range(N_CHIPS)]

    m_block = jax.ShapeDtypeStruct((N_CHIPS, D_MODEL, D_MODEL), F32)
    m_block, = _dw(attn, dba, tm=ATT_Q, place=quarters(0, 256), buf=m_block, name="dw_ba")
    m_block, = _dw(ret, dbr, tm=D_MODEL, place=pairs(256, 512), buf=m_block, name="dw_br")
    m_block, = _dw(merged, dx1b, tm=D_MODEL, place=quarters(768, 256), buf=m_block, name="dw_out")

    def w_in_rows(off, w):
        tm = min(w, D_MODEL)
        return dict(tm=tm, place=lambda i: [(0, tm, (pl.ds(off + i * tm, tm), slice(None)))])

    w_block = jax.ShapeDtypeStruct((D_IN, D_MODEL), F32)
    w_block, sib_mix = _dw(d_gz, h1, buf=w_block, name="dw_in_gz", exchange=_pair_exchange([m_block]),
                           **w_in_rows(P_GR[0], d_gz.shape[1]))
    m_sum = _pair_sum(m_block, [sib_mix], c_arr, tile=256, name="pair_sum_mix")

    (dq_a, dkv_a, dgq, dgk, dsinks, d_ret, got_ffn_sums, got_mix_sums) = _fused(
        [_attn_bwd(q_a, kv_a, d_attn, probs, sink_probs, gq_col, gk, gk_col),
         _ret_bwd(q_r, k_r, v_r, d_o, states, tables)],
        grid=(x_t.shape[0] // BLOCK + 1,), name="mixers_bwd", exchange=_scatter_to_owners([f_sum, m_sum]))
    dgq = dgq.reshape(1, HEAD_DIM)
    ffn_half = _sum_chips(f_sum, got_ffn_sums, chip_arr, tile=528, name="sum_chips_ffn")
    mix_half = _sum_chips(m_sum, got_mix_sums, chip_arr, tile=256, name="sum_chips_mix")
    w_block, ffn_other, mix_other, sib_w_in_3 = _dw(
        d_ret, h1, buf=w_block, name="dw_in_ret", exchange=_share_halves([ffn_half, mix_half], w_in_slabs=(3,)),
        **w_in_rows(P_QR[0], d_ret.shape[1]))
    w_block, sib_w_in_1 = _dw(dq_a, h1, buf=w_block, name="dw_in_q", exchange=_pair_exchange_w_in((1,)),
                              **w_in_rows(*P_QA))
    w_block, sib_w_in_2 = _dw(dkv_a, h1, buf=w_block, name="dw_in_kv", exchange=_pair_exchange_w_in((2,)),
                              **w_in_rows(*P_KVA))
    sib_w_in_0, = _run_exchange(_pair_exchange_w_in((0,), w_block), "pair_exchange_w_in")
    w_sum = _pair_sum(w_block.reshape(N_CHIPS, W_IN_SH, D_MODEL), [sib_w_in_0, sib_w_in_1, sib_w_in_2, sib_w_in_3],
                      c_arr, tile=592, name="pair_sum_w_in")
    d_pieces = [dq_a, dkv_a, d_ret, d_gz]
    grad_x, dg1_p, got_w_in_sums = _proj_bwd(d_pieces, x_t, dx1, w_in_t, g1, _scatter_to_owners([w_sum]))
    w_in_half = _sum_chips(w_sum, got_w_in_sums, chip_arr, tile=592, name="sum_chips_w_in")
    loss_row, *small, w_in_other = _small_step(
        [loss_p.reshape(-1, 128), dg1_p.reshape(-1, D_MODEL), dg2_p.reshape(-1, D_MODEL), dgq, dgk, dsinks],
        [norm_mix_gain, norm_ffn_gain, q_norm_gain, k_norm_gain, attn_sinks,
         m_norm_mix_gain, m_norm_ffn_gain, m_q_norm_gain, m_k_norm_gain, m_attn_sinks,
         v_norm_mix_gain, v_norm_ffn_gain, v_q_norm_gain, v_k_norm_gain, v_attn_sinks], w_in_half)
    loss = loss_row[0, 0]

    def update(name, g_half, g_other, tile, mats):
        outs = _adamw([tuple(tr(a) if t else a[0] for a in wmv) + (off,) for _, *wmv, off, t in mats],
                      g_half, g_other, c_arr, tile=tile, name=f"adamw_{name}")
        return {key: [jnp.transpose(o) if t else o for o in res] for (key, _, _, _, _, t), res in zip(mats, outs)}

    big = {
        **update("w_in", w_in_half, w_in_other, 592, [("w_in", w_in, m_w_in, v_w_in, 0, True)]),
        **update("ffn", ffn_half, ffn_other, 176, [
            ("wg", w_ffn_gate, m_w_ffn_gate, v_w_ffn_gate, 0, True),
            ("wu", w_ffn_up, m_w_ffn_up, v_w_ffn_up, FF_SH, True),
            ("wd", w_ffn_down, m_w_ffn_down, v_w_ffn_down, 2 * FF_SH, False)]),
        **update("mix", mix_half, mix_other, 128, [
            ("wba", w_branch_attn, m_w_branch_attn, v_w_branch_attn, 0, False),
            ("wbr", w_branch_ret, m_w_branch_ret, v_w_branch_ret, 256, False),
            ("wout", w_out, m_w_out, v_w_out, 768, False)])}

    def leaves(i):
        b = [big[n][i][None] for n in ("w_in", "wba", "wbr", "wout", "wg", "wu", "wd")]
        s1, s2, sq, sk, ss = small[5 * i:5 * i + 5]
        return [s1, b[0], sq, sk, ss, b[1], b[2], b[3], s2, b[4], b[5], b[6]]

    return (loss, grad_x[None], *leaves(0), *leaves(1), *leaves(2), *leaves(3))
```

```python
import jax
import jax.numpy as jnp
from jax import lax
from jax.experimental import pallas as pl
from jax.experimental.pallas import tpu as pltpu

F32 = jnp.float32
BF16 = jnp.bfloat16
MESH = pl.DeviceIdType.MESH

D_MODEL = 1024
EPS = 1e-6
HEAD_DIM = 64
N_Q_HEADS = 16
N_KV_HEADS = 2
GROUP = 8
BLOCK = 128
RET_HEADS = 4
RET_QK_DIM = 256
RET_V_DIM = 512
RET_CHUNK = 128
RET_ROT_BASE = 10000.0
D_FF = 2816
ATT_Q = N_Q_HEADS * HEAD_DIM
ATT_KV = N_KV_HEADS * HEAD_DIM
RET_QK = RET_HEADS * RET_QK_DIM
RET_V = RET_HEADS * RET_V_DIM
D_IN = 9472
ADAM_LR = 0.001
ADAM_B1 = 0.9
ADAM_B2 = 0.999
ADAM_EPS = 1e-08
ADAM_WD = 0.01
ADAM_STEP = 10

N_CHIPS = 4
N_DEV = 8
VMEM_LIMIT_BYTES = 60 * 1024 * 1024

P_QA = (0, 1024)
P_KVA = (1024, 256)
P_QR = (1280, 1024)
P_KR = (2304, 1024)
P_VR = (3328, 2048)
P_GR = (5376, 2048)
P_ZA = (7424, 1024)
P_ZR = (8448, 1024)

W_IN_SH = D_IN // N_CHIPS
FF_SH = D_FF // N_CHIPS

SMALL_ROWS = 8


def _dot(a, b):
    return jnp.dot(a, b, preferred_element_type=F32)


def _dot_nt(a, b):
    return lax.dot_general(a, b, (((1,), (1,)), ((), ())), preferred_element_type=F32)


def _dot_tn(a, b):
    return lax.dot_general(a, b, (((0,), (0,)), ((), ())), preferred_element_type=F32)


def _bf(x):
    return x.astype(BF16)


def _rms_stats(x):
    r = lax.rsqrt(jnp.mean(x * x, axis=-1, keepdims=True) + EPS)
    return r, x * r


def _rms_bwd(dy, xhat, r, gain):
    u = dy * gain
    dx = r * (u - xhat * jnp.mean(u * xhat, axis=-1, keepdims=True))
    return dx, dy * xhat


def _params(sem):
    return pltpu.CompilerParams(dimension_semantics=sem, vmem_limit_bytes=VMEM_LIMIT_BYTES)


_ANY = pl.BlockSpec(memory_space=pl.ANY)


class _Exchange:
    def __init__(self, ins, outs, n_sems, phases, staging=(), result_sources=()):
        self.ins, self.outs, self.n_sems, self.phases = list(ins), list(outs), n_sems, list(phases)
        self.staging, self.result_sources = list(staging), list(result_sources)


def _pallas(kern, *, grid, in_specs, out_specs, out_shape, args, name, scratch=(), exchange=None, aliases=None):
    aliases = aliases or {}
    if exchange is None:
        return pl.pallas_call(
            kern, grid=grid, in_specs=in_specs, out_specs=out_specs, out_shape=out_shape, name=name,
            scratch_shapes=list(scratch), input_output_aliases=aliases,
            compiler_params=_params(("arbitrary",) * len(grid)))(*args)
    n_in, n_out, n_sc = len(in_specs), len(out_specs), len(scratch)
    n_xi, n_xo, n_xs = len(exchange.ins), len(exchange.outs), len(exchange.staging)
    n_steps = 1
    for g in grid:
        n_steps *= g

    def wrapped(*refs):
        ins, refs = refs[:n_in], refs[n_in:]
        x_ins, refs = refs[:n_xi], refs[n_xi:]
        outs, refs = refs[:n_out], refs[n_out:]
        x_outs, refs = refs[:n_xo], refs[n_xo:]
        scr, refs = refs[:n_sc], refs[n_sc:]
        staging, (send_sems, recv_sems) = refs[:n_xs], refs[n_xs:]
        x_ins = list(x_ins) + [outs[j] for j in exchange.result_sources]
        step = pl.program_id(0)
        for d in range(1, len(grid)):
            step = step * grid[d] + pl.program_id(d)
        for frac, fn in exchange.phases:
            at = min(int(frac * n_steps), n_steps - 1)

            @pl.when(step == at)
            def _(fn=fn):
                fn(x_ins, x_outs, send_sems, recv_sems, staging)

        kern(*ins, *outs, *scr)

    sems = [pltpu.SemaphoreType.DMA((exchange.n_sems,)), pltpu.SemaphoreType.DMA((exchange.n_sems,))]
    return pl.pallas_call(
        wrapped, grid=grid, in_specs=list(in_specs) + [_ANY] * n_xi, out_specs=list(out_specs) + [_ANY] * n_xo,
        out_shape=list(out_shape) + exchange.outs, name=name,
        scratch_shapes=list(scratch) + exchange.staging + sems, input_output_aliases=aliases,
        compiler_params=_params(("arbitrary",) * len(grid)))(*args, *exchange.ins)


def _run_exchange(exchange, name):
    def body(*refs):
        n_i, n_o = len(exchange.ins), len(exchange.outs)
        staging, (send_sems, recv_sems) = refs[n_i + n_o:-2], refs[-2:]
        for _, fn in exchange.phases:
            fn(refs[:n_i], refs[n_i:n_i + n_o], send_sems, recv_sems, staging)

    sems = [pltpu.SemaphoreType.DMA((exchange.n_sems,)), pltpu.SemaphoreType.DMA((exchange.n_sems,))]
    return pl.pallas_call(body, in_specs=[_ANY] * len(exchange.ins), out_specs=[_ANY] * len(exchange.outs),
                          out_shape=exchange.outs, scratch_shapes=exchange.staging + sems, name=name,
                          compiler_params=pltpu.CompilerParams(vmem_limit_bytes=VMEM_LIMIT_BYTES))(*exchange.ins)


def _fused(parts, *, grid, name, exchange=None):
    counts = [(len(p["in_specs"]), len(p["out_specs"]), len(p["scratch"])) for p in parts]
    n_in, n_out = sum(c[0] for c in counts), sum(c[1] for c in counts)

    def kern(*refs):
        ins, outs, scr = refs[:n_in], refs[n_in:n_in + n_out], refs[n_in + n_out:]
        i0 = o0 = s0 = 0
        for p, (ni, no, ns) in zip(parts, counts):
            p["kern"](*ins[i0:i0 + ni], *outs[o0:o0 + no], *scr[s0:s0 + ns])
            i0, o0, s0 = i0 + ni, o0 + no, s0 + ns

    cat = lambda key: [a for p in parts for a in p[key]]
    return _pallas(kern, grid=grid, in_specs=cat("in_specs"), out_specs=cat("out_specs"), out_shape=cat("out_shape"),
                   scratch=cat("scratch"), args=cat("args"), name=name, exchange=exchange)


def _row_call(body, *, tm, row_ins, res_ins, row_outs, part_outs=(), name, exchange=None):
    t = row_ins[0].shape[0]
    n_tiles = t // tm
    in_specs = [pl.BlockSpec((tm, a.shape[1]), lambda i: (i, 0)) for a in row_ins]
    in_specs += [pl.BlockSpec(a.shape, lambda i: (0, 0), pipeline_mode=pl.Buffered(1)) for a in res_ins]
    out_shape = [jax.ShapeDtypeStruct((t, w), dt) for (w, dt) in row_outs]
    out_shape += [jax.ShapeDtypeStruct((n_tiles, 1, w), F32) for w in part_outs]
    out_specs = [pl.BlockSpec((tm, w), lambda i: (i, 0)) for (w, _) in row_outs]
    out_specs += [pl.BlockSpec((1, 1, w), lambda i: (i, 0, 0)) for w in part_outs]
    n_ri, n_re, n_ro = len(row_ins), len(res_ins), len(row_outs)

    def kern(*refs):
        body(refs[:n_ri], refs[n_ri:n_ri + n_re], refs[n_ri + n_re:n_ri + n_re + n_ro], refs[n_ri + n_re + n_ro:])

    return _pallas(kern, grid=(n_tiles,), in_specs=in_specs, out_specs=out_specs, out_shape=out_shape,
                   args=[*row_ins, *res_ins], name=name, exchange=exchange)


def _proj_fwd(x, g1, w_in_t, exchange):
    pieces = ((P_QA, F32), (P_KVA, F32), (P_QR, F32), (P_KR, F32), (P_VR, BF16), (P_GR, F32), (P_ZA, F32), (P_ZR, F32))

    def body(ri, re, ro, po):
        x_t = ri[0][...]
        r, xhat = _rms_stats(x_t)
        hb = _bf(xhat * re[0][...])
        ro[0][...] = hb
        for k, ((off, w), dt) in enumerate(pieces):
            ro[1 + k][...] = _dot_nt(hb, re[1][off:off + w, :]).astype(dt)

    outs = [(D_MODEL, BF16)] + [(w, dt) for ((_, w), dt) in pieces]
    return _row_call(body, tm=256, row_ins=[x], res_ins=[g1, w_in_t], row_outs=outs, name="proj_fwd",
                     exchange=exchange)


def _mix_fwd(attn, ret, z_a, z_r, x, wba, wbr, wout, g2):
    def body(ri, re, ro, po):
        ba = _dot(ri[0][...], re[0][...])
        br = _dot(ri[1][...], re[1][...])
        m = jax.nn.sigmoid(ri[2][...]) * ba + jax.nn.sigmoid(ri[3][...]) * br
        mb = _bf(m)
        x1 = ri[4][...] + _dot(mb, re[2][...])
        r, xhat = _rms_stats(x1)
        ro[0][...] = ba
        ro[1][...] = br
        ro[2][...] = mb
        ro[3][...] = x1
        ro[4][...] = _bf(xhat * re[3][...])

    outs = [(D_MODEL, F32), (D_MODEL, F32), (D_MODEL, BF16), (D_MODEL, F32), (D_MODEL, BF16)]
    return _row_call(body, tm=512, row_ins=[attn, ret, z_a, z_r, x], res_ins=[wba, wbr, wout, g2], row_outs=outs,
                     name="mix_fwd")


def _ffn_fwd_bwd(h2, x1, target, wg_t, wu_t, wd, g2):
    def body(ri, re, ro, po):
        h2_t = ri[0][...]
        x1_t = ri[1][...]
        gate = _dot_nt(h2_t, re[0][...])
        up = _dot_nt(h2_t, re[1][...])
        sg = jax.nn.sigmoid(gate)
        sl = gate * sg
        actb = _bf(sl * up)
        ro[0][...] = actb
        y = x1_t + _dot(actb, re[2][...])
        e = y - ri[2][...]
        po[0][0] = jnp.broadcast_to(0.5 * jnp.sum(jnp.sum(e * e, axis=1, keepdims=True), axis=0, keepdims=True)
                                    * (1.0 / D_MODEL), (1, 128))
        dy = e * (1.0 / D_MODEL)
        dyb = _bf(dy)
        ro[3][...] = dyb
        dact = _dot_nt(dyb, re[2][...])
        dupb = _bf(dact * sl)
        dgateb = _bf(dact * up * (sg * (1.0 + gate * (1.0 - sg))))
        ro[1][...] = dgateb
        ro[2][...] = dupb
        dh2 = _dot(dgateb, re[0][...]) + _dot(dupb, re[1][...])
        r, xhat = _rms_stats(x1_t)
        dxn, dgain = _rms_bwd(dh2, xhat, r, re[3][...])
        dx1 = dy + dxn
        ro[4][...] = dx1
        ro[5][...] = _bf(dx1)
        po[1][0] = jnp.sum(dgain, axis=0, keepdims=True)

    outs = [(D_FF, BF16), (D_FF, BF16), (D_FF, BF16), (D_MODEL, BF16), (D_MODEL, F32), (D_MODEL, BF16)]
    return _row_call(body, tm=256, row_ins=[h2, x1, target], res_ins=[wg_t, wu_t, wd, g2], row_outs=outs,
                     part_outs=(128, D_MODEL), name="ffn_fwd_bwd")


def _mix_bwd(dx1b, z_a, z_r, ba, br, g_r, o_ret, wout, wba, wbr, exchange):
    def body(ri, re, ro, po):
        dm = _dot_nt(ri[0][...], re[0][...])
        sa = jax.nn.sigmoid(ri[1][...])
        sr = jax.nn.sigmoid(ri[2][...])
        dbab = _bf(sa * dm)
        dbrb = _bf(sr * dm)
        ro[0][...] = dbab
        ro[1][...] = dbrb
        ro[4][:, RET_V:RET_V + D_MODEL] = _bf(dm * ri[3][...] * (sa * (1.0 - sa)))
        ro[4][:, RET_V + D_MODEL:RET_V + 2 * D_MODEL] = _bf(dm * ri[4][...] * (sr * (1.0 - sr)))
        ro[2][...] = _bf(_dot_nt(dbab, re[1][...]))
        dret = _dot_nt(dbrb, re[2][...])
        for h in range(RET_HEADS):
            cols = slice(h * RET_V_DIM, (h + 1) * RET_V_DIM)
            g = ri[5][:, cols]
            r, rn = _rms_stats(ri[6][:, cols])
            sg = jax.nn.sigmoid(g)
            dret_h = dret[:, cols]
            d_rn = dret_h * (g * sg)
            ro[4][:, cols] = _bf(dret_h * rn * (sg * (1.0 + g * (1.0 - sg))))
            ro[3][:, cols] = r * (d_rn - rn * jnp.mean(d_rn * rn, axis=-1, keepdims=True))

    outs = [(D_MODEL, BF16), (D_MODEL, BF16), (ATT_Q, BF16), (RET_V, F32), (RET_V + 2 * D_MODEL, BF16)]
    return _row_call(body, tm=256, row_ins=[dx1b, z_a, z_r, ba, br, g_r, o_ret], res_ins=[wout, wba, wbr],
                     row_outs=outs, name="mix_bwd", exchange=exchange)


def _proj_bwd(d_pieces, x, dx1, w_in_t, g1, exchange):
    widths = [p.shape[1] for p in d_pieces]
    groups = [(sum(widths[:k]), w) for k, w in enumerate(widths)]
    n_p = len(groups)

    def body(ri, re, ro, po):
        dh = None
        for k, (off, w) in enumerate(groups):
            term = _dot(ri[k][...], re[0][off:off + w, :])
            dh = term if dh is None else dh + term
        r, xhat = _rms_stats(ri[n_p][...])
        dxn, dgain = _rms_bwd(dh, xhat, r, re[1][...])
        ro[0][...] = ri[n_p + 1][...] + dxn
        po[0][0] = jnp.sum(dgain, axis=0, keepdims=True)

    return _row_call(body, tm=512, row_ins=[*d_pieces, x, dx1], res_ins=[w_in_t, g1], row_outs=[(D_MODEL, F32)],
                     part_outs=(D_MODEL,), name="proj_bwd", exchange=exchange)


def _dw(a, b, *, tm, place, buf, name, exchange=None):
    a_list = list(a) if isinstance(a, (list, tuple)) else [a]
    n_a = len(a_list)
    t, m = a_list[0].shape[0], sum(x.shape[1] for x in a_list)
    assert n_a == 1 or m == tm
    n = b.shape[1]
    tk = min(2048, t)
    n_i, n_k = m // tm, t // tk
    fresh = isinstance(buf, jax.ShapeDtypeStruct)
    n_copies = len(place(0))

    def kern(*refs):
        a_refs, b_ref = refs[:n_a], refs[n_a]
        out_ref, acc, sems = refs[-3:]
        i, k = pl.program_id(0), pl.program_id(1)
        off = 0
        for a_ref in a_refs:
            w = a_ref.shape[1]
            part = _dot_tn(a_ref[...], b_ref[...])

            @pl.when(k == 0)
            def _(part=part, off=off, w=w):
                acc[i, off:off + w, :] = part

            @pl.when(k > 0)
            def _(part=part, off=off, w=w):
                acc[i, off:off + w, :] += part

            off += w

        def copies(tile):
            return [pltpu.make_async_copy(acc.at[tile, pl.ds(r0, rows), :], out_ref.at[idx], sems.at[tile * n_copies + c])
                    for c, (r0, rows, idx) in enumerate(place(tile))]

        for tile in range(n_i):
            @pl.when((i == tile) & (k == n_k - 1))
            def _(tile=tile):
                for cp in copies(tile):
                    cp.start()

        @pl.when((i == n_i - 1) & (k == n_k - 1))
        def _():
            for tile in range(n_i):
                for cp in copies(tile):
                    cp.wait()

    in_specs = [pl.BlockSpec((tk, tm if n_a == 1 else x.shape[1]), lambda i, k: (k, i)) for x in a_list]
    in_specs.append(pl.BlockSpec((tk, n), lambda i, k: (k, 0)))
    shape = buf if fresh else jax.ShapeDtypeStruct(buf.shape, buf.dtype)
    return _pallas(
        kern, grid=(n_i, n_k), in_specs=in_specs + ([] if fresh else [_ANY]), out_specs=[_ANY], out_shape=[shape],
        scratch=[pltpu.VMEM((n_i, tm, n), F32), pltpu.SemaphoreType.DMA((n_i * n_copies,))],
        args=a_list + [b] + ([] if fresh else [buf]), aliases=None if fresh else {n_a + 1: 0}, name=name,
        exchange=exchange)


def _heads_to_lanes(x3):
    return jnp.concatenate([x3[g] for g in range(GROUP)], axis=1)


def _lanes_to_heads(xt):
    return jnp.concatenate([xt[:, g * BLOCK:(g + 1) * BLOCK] for g in range(GROUP)], axis=0)


def _attn_queries(kvh, q_ref, gq_col):
    cols = slice(kvh * GROUP * HEAD_DIM, (kvh + 1) * GROUP * HEAD_DIM)
    q3 = q_ref[:, cols].T.reshape(GROUP, HEAD_DIM, BLOCK)
    rq = lax.rsqrt(jnp.mean(q3 * q3, axis=1, keepdims=True) + EPS)
    qhat = q3 * rq
    return qhat, rq, _heads_to_lanes(_bf(qhat * (gq_col * (HEAD_DIM ** -0.5))))


def _from_prev():
    j = lax.broadcasted_iota(jnp.int32, (BLOCK, GROUP * BLOCK), 0)
    i = lax.broadcasted_iota(jnp.int32, (BLOCK, GROUP * BLOCK), 1) & (BLOCK - 1)
    return j > i


def _attn_probs(n, kvh, qts, kvp_ref, kvc_ref, gk, sink_ref):
    kcols = slice(kvh * HEAD_DIM, (kvh + 1) * HEAD_DIM)
    k = jnp.concatenate([kvp_ref[:, kcols], kvc_ref[:, kcols]], axis=0)
    rk, khat = _rms_stats(k)
    st = _dot(_bf(khat * gk), qts)
    f = jnp.where(_from_prev(), jnp.where(n > 0, st[0:BLOCK], -1e30), st[BLOCK:2 * BLOCK])
    sink = jnp.concatenate([jnp.broadcast_to(sink_ref[0:1, kvh * GROUP + g:kvh * GROUP + g + 1], (1, BLOCK))
                            for g in range(GROUP)], axis=1)
    m = jnp.maximum(jnp.max(f, axis=0, keepdims=True), sink)
    e = jnp.exp(f - m)
    es = jnp.exp(sink - m)
    inv = 1.0 / (jnp.sum(e, axis=0, keepdims=True) + es)
    return e * inv, es * inv


def _unfold(from_prev, xf):
    return _bf(jnp.concatenate([jnp.where(from_prev, xf, 0.0), jnp.where(from_prev, 0.0, xf)], axis=0))


def _attn_fwd(q_a, kv_a, gq_col, gk, sinks):
    t = q_a.shape[0]
    nb = t // BLOCK

    def kern(q_ref, kvp_ref, kvc_ref, gq_ref, gk_ref, sink_ref, o_ref, pf_ref, ps_ref):
        n = pl.program_id(0)
        kvt = jnp.concatenate([kvp_ref[...].T, kvc_ref[...].T], axis=1)
        for kvh in range(N_KV_HEADS):
            _, _, qts = _attn_queries(kvh, q_ref, gq_ref[...])
            pf, psink = _attn_probs(n, kvh, qts, kvp_ref, kvc_ref, gk_ref[...], sink_ref)
            lanes = slice(kvh * GROUP * BLOCK, (kvh + 1) * GROUP * BLOCK)
            pf_ref[:, lanes] = pf
            ps_ref[:, lanes] = psink
            vt = _bf(kvt[ATT_KV + kvh * HEAD_DIM:ATT_KV + (kvh + 1) * HEAD_DIM, :])
            out_t = _dot(vt, _unfold(_from_prev(), pf))
            cols = slice(kvh * GROUP * HEAD_DIM, (kvh + 1) * GROUP * HEAD_DIM)
            o_ref[:, cols] = _bf(_lanes_to_heads(out_t).T)

    small = lambda a: pl.BlockSpec(a.shape, lambda n: (0, 0))
    folded = N_KV_HEADS * GROUP * BLOCK
    return dict(
        kern=kern,
        in_specs=[pl.BlockSpec((BLOCK, ATT_Q), lambda n: (n, 0)),
                  pl.BlockSpec((BLOCK, 2 * ATT_KV), lambda n: (jnp.maximum(n - 1, 0), 0)),
                  pl.BlockSpec((BLOCK, 2 * ATT_KV), lambda n: (n, 0)),
                  small(gq_col), small(gk), small(sinks)],
        out_specs=[pl.BlockSpec((BLOCK, ATT_Q), lambda n: (n, 0)), pl.BlockSpec((BLOCK, folded), lambda n: (n, 0)),
                   pl.BlockSpec((None, 1, folded), lambda n: (n, 0, 0))],
        out_shape=[jax.ShapeDtypeStruct((t, ATT_Q), BF16), jax.ShapeDtypeStruct((t, folded), F32),
                   jax.ShapeDtypeStruct((nb, 1, folded), F32)],
        scratch=[], args=[q_a, kv_a, kv_a, gq_col, gk, sinks])


def _attn_bwd(q_a, kv_a, d_attn, probs, sink_probs, gq_col, gk, gk_col):
    t = q_a.shape[0]
    nb = t // BLOCK

    def kern(q_ref, kvp_ref, kvc_ref, do_ref, pf_ref, ps_ref, gq_ref, gk_ref, gkc_ref,
             dq_ref, dkv_ref, dgq_ref, dgk_ref, dsink_ref, band_k, band_v, carry_k, carry_v):
        n = pl.program_id(0)
        gq_v = gq_ref[...]
        gk_v = gk_ref[...]

        @pl.when(n == 0)
        def _():
            carry_k[...] = jnp.zeros_like(carry_k)
            carry_v[...] = jnp.zeros_like(carry_v)
            dgq_ref[...] = jnp.zeros_like(dgq_ref)
            dgk_ref[...] = jnp.zeros_like(dgk_ref)
            dsink_ref[...] = jnp.zeros_like(dsink_ref)

        @pl.when(n == nb)
        def _():
            band_k[...] = jnp.zeros_like(band_k)
            band_v[...] = jnp.zeros_like(band_v)

        @pl.when(n < nb)
        def _():
            lane16 = lax.broadcasted_iota(jnp.int32, (1, N_Q_HEADS), 1)
            dsink = jnp.zeros((1, N_Q_HEADS), F32)
            dgq = jnp.zeros((HEAD_DIM, 1), F32)
            gk_col = gkc_ref[...]
            kvt = jnp.concatenate([kvp_ref[...].T, kvc_ref[...].T], axis=1)
            from_prev = _from_prev()
            for kvh in range(N_KV_HEADS):
                qhat, rq, qts = _attn_queries(kvh, q_ref, gq_v)
                lanes = slice(kvh * GROUP * BLOCK, (kvh + 1) * GROUP * BLOCK)
                pf = pf_ref[:, lanes]
                cols = slice(kvh * GROUP * HEAD_DIM, (kvh + 1) * GROUP * HEAD_DIM)
                vcols = slice(ATT_KV + kvh * HEAD_DIM, ATT_KV + (kvh + 1) * HEAD_DIM)
                dot = _heads_to_lanes(_bf(do_ref[:, cols].astype(F32).T.reshape(GROUP, HEAD_DIM, BLOCK)))
                vb = _bf(jnp.concatenate([kvp_ref[:, vcols], kvc_ref[:, vcols]], axis=0))
                dpt = _dot(vb, dot)
                dpf = jnp.where(from_prev, dpt[0:BLOCK], dpt[BLOCK:2 * BLOCK])
                delta = jnp.sum(pf * dpf, axis=0, keepdims=True)
                dst = _unfold(from_prev, pf * (dpf - delta))
                dsk = ps_ref[:, lanes] * delta
                for g in range(GROUP):
                    tot = jnp.sum(dsk[:, g * BLOCK:(g + 1) * BLOCK], axis=1, keepdims=True)
                    dsink = dsink - jnp.where(lane16 == kvh * GROUP + g, tot, 0.0)
                kt = kvt[kvh * HEAD_DIM:(kvh + 1) * HEAD_DIM, :]
                knt = _bf(kt * lax.rsqrt(jnp.mean(kt * kt, axis=0, keepdims=True) + EPS) * gk_col)
                dqn = (_dot(knt, dst) * (HEAD_DIM ** -0.5))
                band_k[kvh] = _dot_nt(dst, qts)
                band_v[kvh] = _dot_nt(_unfold(from_prev, pf), dot)
                dqn3 = _lanes_to_heads(dqn).reshape(GROUP, HEAD_DIM, BLOCK)
                u = dqn3 * gq_v
                dq3 = rq * (u - qhat * jnp.mean(u * qhat, axis=1, keepdims=True))
                dgq = dgq + jnp.sum(jnp.sum(dqn3 * qhat, axis=0), axis=1, keepdims=True)
                dq_ref[:, cols] = _bf(dq3.reshape(GROUP * HEAD_DIM, BLOCK).T)
            dsink_ref[...] += dsink
            dgq_ref[...] += dgq

        dgk = jnp.zeros((1, HEAD_DIM), F32)
        for kvh in range(N_KV_HEADS):
            kcols = slice(kvh * HEAD_DIM, (kvh + 1) * HEAD_DIM)
            vcols = slice(ATT_KV + kvh * HEAD_DIM, ATT_KV + (kvh + 1) * HEAD_DIM)
            dkn = carry_k[kvh] + band_k[kvh, 0:BLOCK, :]
            dv = carry_v[kvh] + band_v[kvh, 0:BLOCK, :]
            rk, khat = _rms_stats(kvp_ref[:, kcols])
            dk, dgain = _rms_bwd(dkn, khat, rk, gk_v)
            dgk = dgk + jnp.sum(dgain, axis=0, keepdims=True)
            dkv_ref[:, kcols] = _bf(dk)
            dkv_ref[:, vcols] = _bf(dv)
            carry_k[kvh] = band_k[kvh, BLOCK:2 * BLOCK, :]
            carry_v[kvh] = band_v[kvh, BLOCK:2 * BLOCK, :]
        dgk_ref[...] += dgk

    small = lambda a: pl.BlockSpec(a.shape, lambda n: (0, 0))
    last = nb - 1
    return dict(
        kern=kern,
        in_specs=[pl.BlockSpec((BLOCK, ATT_Q), lambda n: (jnp.minimum(n, last), 0)),
                  pl.BlockSpec((BLOCK, 2 * ATT_KV), lambda n: (jnp.maximum(n - 1, 0), 0)),
                  pl.BlockSpec((BLOCK, 2 * ATT_KV), lambda n: (jnp.minimum(n, last), 0)),
                  pl.BlockSpec((BLOCK, ATT_Q), lambda n: (jnp.minimum(n, last), 0)),
                  pl.BlockSpec((BLOCK, probs.shape[1]), lambda n: (jnp.minimum(n, last), 0)),
                  pl.BlockSpec((None, 1, probs.shape[1]), lambda n: (jnp.minimum(n, last), 0, 0)),
                  small(gq_col), small(gk), small(gk_col)],
        out_specs=[pl.BlockSpec((BLOCK, ATT_Q), lambda n: (jnp.minimum(n, last), 0)),
                   pl.BlockSpec((BLOCK, 2 * ATT_KV), lambda n: (jnp.maximum(n - 1, 0), 0)),
                   pl.BlockSpec((HEAD_DIM, 1), lambda n: (0, 0)),
                   pl.BlockSpec((1, HEAD_DIM), lambda n: (0, 0)),
                   pl.BlockSpec((1, N_Q_HEADS), lambda n: (0, 0))],
        out_shape=[jax.ShapeDtypeStruct((t, ATT_Q), BF16), jax.ShapeDtypeStruct((t, 2 * ATT_KV), BF16),
                   jax.ShapeDtypeStruct((HEAD_DIM, 1), F32), jax.ShapeDtypeStruct((1, HEAD_DIM), F32),
                   jax.ShapeDtypeStruct((1, N_Q_HEADS), F32)],
        scratch=[pltpu.VMEM((N_KV_HEADS, 2 * BLOCK, HEAD_DIM), F32),
                 pltpu.VMEM((N_KV_HEADS, 2 * BLOCK, HEAD_DIM), F32),
                 pltpu.VMEM((N_KV_HEADS, BLOCK, HEAD_DIM), F32),
                 pltpu.VMEM((N_KV_HEADS, BLOCK, HEAD_DIM), F32)],
        args=[q_a, kv_a, kv_a, d_attn, probs, sink_probs, gq_col, gk, gk_col])


def _ret_tables(t, exchange):
    theta = 1.0 / (RET_ROT_BASE ** jnp.linspace(0.0, 1.0, RET_QK_DIM // 2, dtype=F32))
    theta2 = jnp.repeat(theta, 2)[None, :]
    sign = jnp.tile(jnp.array([-1.0, 1.0], F32), RET_QK_DIM // 2)[None, :]

    def kern(theta_ref, sign_ref, cos_ref, sin_ref):
        first = pl.program_id(0) * RET_CHUNK
        pos = (first + lax.broadcasted_iota(jnp.int32, (RET_CHUNK, RET_QK_DIM), 0)).astype(F32)
        ang = pos * theta_ref[...]
        cos_ref[...] = jnp.cos(ang)
        sin_ref[...] = jnp.sin(ang) * sign_ref[...]

    row = pl.BlockSpec((1, RET_QK_DIM), lambda n: (0, 0))
    blk = pl.BlockSpec((RET_CHUNK, RET_QK_DIM), lambda n: (n, 0))
    cos, sin_s, *got = _pallas(kern, grid=(t // RET_CHUNK,), in_specs=[row, row], out_specs=[blk, blk],
                               out_shape=[jax.ShapeDtypeStruct((t, RET_QK_DIM), F32)] * 2, args=[theta2, sign],
                               name="position_tables", exchange=exchange)
    log_gamma = jnp.log(1.0 - 2.0 ** (-5.0 - jnp.arange(RET_HEADS, dtype=F32)))
    i = jnp.arange(RET_CHUNK, dtype=F32)
    diff = i[:, None] - i[None, :]
    causal = diff >= 0
    decay = jnp.where(causal[None], jnp.exp(jnp.where(causal, diff, 0.0)[None] * log_gamma[:, None, None]), 0.0)
    xi = jnp.exp((i + 1.0)[None, :] * log_gamma[:, None])[:, :, None]
    zeta = jnp.exp((RET_CHUNK - 1.0 - i)[None, :] * log_gamma[:, None])[:, :, None]
    gch = jnp.broadcast_to(jnp.exp(RET_CHUNK * log_gamma)[:, None, None], (RET_HEADS, 1, 128))
    return (cos, sin_s, decay, xi, zeta, gch), got


def _swap_pairs(x):
    lane = lax.broadcasted_iota(jnp.int32, x.shape, 1)
    return jnp.where((lane & 1) == 0, pltpu.roll(x, RET_QK_DIM - 1, 1), pltpu.roll(x, 1, 1))


def _rotate(x, cos, sin_s):
    return x * cos + _swap_pairs(x) * sin_s


def _rotate_bwd(dy, cos, sin_s):
    return dy * cos + _swap_pairs(dy * sin_s)


def _ret_specs(order):
    qk = pl.BlockSpec((RET_CHUNK, RET_QK), lambda j: (order(j), 0))
    v = pl.BlockSpec((RET_CHUNK, RET_V), lambda j: (order(j), 0))
    dec = pl.BlockSpec((RET_HEADS, RET_CHUNK, RET_CHUNK), lambda j: (0, 0, 0))
    col = pl.BlockSpec((RET_HEADS, RET_CHUNK, 1), lambda j: (0, 0, 0))
    gch = pl.BlockSpec((RET_HEADS, 1, 128), lambda j: (0, 0, 0))
    st = pl.BlockSpec((RET_HEADS, None, RET_QK_DIM, RET_V_DIM), lambda j: (0, order(j), 0, 0))
    pos = pl.BlockSpec((RET_CHUNK, RET_QK_DIM), lambda j: (order(j), 0))
    return qk, v, dec, col, gch, st, pos


def _ret_fwd(q_r, k_r, v_r, g_r, tables):
    t = q_r.shape[0]
    nc = t // RET_CHUNK
    cos, sin_s, decay, xi, zeta, gch = tables

    def kern(q_ref, k_ref, v_ref, g_ref, cos_ref, sin_ref, dec_ref, xi_ref, zeta_ref, gch_ref,
             o_ref, ret_ref, st_ref, state):
        @pl.when(pl.program_id(0) == 0)
        def _():
            state[...] = jnp.zeros_like(state)

        cos_t = cos_ref[...]
        sin_t = sin_ref[...]
        for h in range(RET_HEADS):
            qc = slice(h * RET_QK_DIM, (h + 1) * RET_QK_DIM)
            vc = slice(h * RET_V_DIM, (h + 1) * RET_V_DIM)
            qs = _bf(_rotate(q_ref[:, qc], cos_t, sin_t))
            ks = _rotate(k_ref[:, qc] * (RET_QK_DIM ** -0.5), cos_t, sin_t)
            vb = v_ref[:, vc]
            s_old = state[h]
            sb = _bf(s_old)
            st_ref[h] = sb
            inner = _dot_nt(qs, _bf(ks)) * dec_ref[h]
            out = _dot(_bf(inner), vb) + _dot(qs, sb) * xi_ref[h]
            state[h] = gch_ref[h, :, 0:1] * s_old + _dot_tn(_bf(ks * zeta_ref[h]), vb)
            o_ref[:, vc] = out
            r, rn = _rms_stats(out)
            g = g_ref[:, vc]
            ret_ref[:, vc] = _bf(g * jax.nn.sigmoid(g) * rn)

    qk, v, dec, col, gsp, st, pos = _ret_specs(lambda j: j)
    return dict(
        kern=kern,
        in_specs=[qk, qk, v, v, pos, pos, dec, col, col, gsp],
        out_specs=[v, v, st],
        out_shape=[jax.ShapeDtypeStruct((t, RET_V), F32), jax.ShapeDtypeStruct((t, RET_V), BF16),
                   jax.ShapeDtypeStruct((RET_HEADS, nc, RET_QK_DIM, RET_V_DIM), BF16)],
        scratch=[pltpu.VMEM((RET_HEADS, RET_QK_DIM, RET_V_DIM), F32)],
        args=[q_r, k_r, v_r, g_r, cos, sin_s, decay, xi, zeta, gch])


def _ret_bwd(q_r, k_r, v_r, d_o, states, tables):
    t = q_r.shape[0]
    nc = t // RET_CHUNK
    cos, sin_s, decay, xi, zeta, gch = tables

    def kern(q_ref, k_ref, v_ref, do_ref, st_ref, cos_ref, sin_ref, dec_ref, xi_ref, zeta_ref, gch_ref,
             d_ref, dstate):
        dq_ref, dk_ref = d_ref.at[:, 0:RET_QK], d_ref.at[:, RET_QK:2 * RET_QK]
        dv_ref = d_ref.at[:, 2 * RET_QK:2 * RET_QK + RET_V]

        @pl.when(pl.program_id(0) == 0)
        def _():
            dstate[...] = jnp.zeros_like(dstate)

        @pl.when(pl.program_id(0) < nc)
        def _():
            cos_t = cos_ref[...]
            sin_t = sin_ref[...]
            scale = RET_QK_DIM ** -0.5
            for h in range(RET_HEADS):
                qc = slice(h * RET_QK_DIM, (h + 1) * RET_QK_DIM)
                vc = slice(h * RET_V_DIM, (h + 1) * RET_V_DIM)
                qs = _bf(_rotate(q_ref[:, qc], cos_t, sin_t))
                ks = _rotate(k_ref[:, qc] * scale, cos_t, sin_t)
                ksb = _bf(ks)
                vb = v_ref[:, vc]
                d_o_t = do_ref[:, vc]
                dob = _bf(d_o_t)
                doxb = _bf(d_o_t * xi_ref[h])
                dec = dec_ref[h]
                ds_old = dstate[h]
                dsb = _bf(ds_old)
                pb = _bf(_dot_nt(qs, ksb) * dec)
                dpb = _bf(_dot_nt(dob, vb) * dec)
                dqs = _dot(dpb, ksb) + _dot_nt(doxb, st_ref[h])
                dks = _dot_tn(dpb, qs) + _dot_nt(vb, dsb) * zeta_ref[h]
                dv_ref[:, vc] = _bf(_dot_tn(pb, dob) + _dot(_bf(ks * zeta_ref[h]), dsb))
                dstate[h] = gch_ref[h, :, 0:1] * ds_old + _dot_tn(qs, doxb)
                dq_ref[:, qc] = _bf(_rotate_bwd(dqs, cos_t, sin_t))
                dk_ref[:, qc] = _bf(_rotate_bwd(dks, cos_t, sin_t) * scale)

    backwards = lambda j: jnp.maximum(nc - 1 - j, 0)
    qk, v, dec, col, gsp, st, pos = _ret_specs(backwards)
    return dict(
        kern=kern,
        in_specs=[qk, qk, v, v, st, pos, pos, dec, col, col, gsp],
        out_specs=[pl.BlockSpec((RET_CHUNK, 2 * RET_QK + RET_V), lambda j: (backwards(j), 0))],
        out_shape=[jax.ShapeDtypeStruct((t, 2 * RET_QK + RET_V), BF16)],
        scratch=[pltpu.VMEM((RET_HEADS, RET_QK_DIM, RET_V_DIM), F32)],
        args=[q_r, k_r, v_r, d_o, states, cos, sin_s, decay, xi, zeta, gch])


def _position():
    return lax.axis_index("x"), lax.axis_index("y"), lax.axis_index("c")


def _gather_exchange(owns, forward_at):
    n = len(owns)

    def copies(ins, outs, send_sems, recv_sems, staging):
        x, y, c = _position()
        sibling = (x, y, 1 - c)
        chips = [(1 - x, y), (x, 1 - y), (1 - x, 1 - y)]
        my_chip = 2 * x + y

        def slab(a, chip, hf):
            half = owns[a].shape[0] // 2
            return outs[a].at[chip, pl.ds(hf * half, half), :]

        def copy(k, src, dst, to):
            return pltpu.make_async_remote_copy(src_ref=src, dst_ref=dst, send_sem=send_sems.at[k],
                                                recv_sem=recv_sems.at[k], device_id=to, device_id_type=MESH)

        first, passed, from_sibling, stage_in, stage_out = [], [], [], [], []
        for a in range(n):
            half = owns[a].shape[0] // 2
            for k, (cx, cy) in enumerate(chips):
                first.append(copy(6 * a + k, ins[a].at[pl.ds(c * half, half), :], slab(a, my_chip, c), (cx, cy, c)))
                landed = slab(a, 2 * cx + cy, c)
                passed.append(copy(6 * a + 3 + k, landed, landed, sibling))
                theirs = slab(a, 2 * cx + cy, 1 - c)
                from_sibling.append(copy(6 * a + 3 + k, theirs, theirs, sibling))
            stage_in.append(pltpu.make_async_copy(ins[a], staging[a], send_sems.at[6 * n + a]))
            stage_out.append(pltpu.make_async_copy(staging[a], outs[a].at[my_chip], recv_sems.at[6 * n + a]))
        return first, passed, from_sibling, stage_in, stage_out

    def start(*args):
        first, _, _, stage_in, _ = copies(*args)
        for cp in first + stage_in:
            cp.start()

    def forward(*args):
        first, passed, _, stage_in, stage_out = copies(*args)
        for staged, cp in zip(stage_in, stage_out):
            staged.wait()
            cp.start()
        for arrived, cp in zip(first, passed):
            arrived.wait_recv()
            cp.start()

    def finish(*args):
        first, passed, from_sibling, _, stage_out = copies(*args)
        for cp in from_sibling:
            cp.wait_recv()
        for cp in first + passed:
            cp.wait_send()
        for cp in stage_out:
            cp.wait()

    outs = [jax.ShapeDtypeStruct((N_CHIPS, *a.shape), a.dtype) for a in owns]
    return _Exchange(owns, outs, 7 * n, [(0.0, start), (forward_at, forward), (1.0, finish)],
                     staging=[pltpu.VMEM(a.shape, a.dtype) for a in owns])


def _symmetric_exchange(ins, outs, plan, result_sources=()):
    n_sems = len(plan([None] * (len(ins) + len(result_sources)), [None] * len(outs), 0, 0, 0, dry=True))

    def copies(in_refs, out_refs, send_sems, recv_sems, staging):
        x, y, c = _position()
        return [pltpu.make_async_remote_copy(src_ref=src, dst_ref=dst, send_sem=send_sems.at[k],
                                             recv_sem=recv_sems.at[k], device_id=dev, device_id_type=MESH)
                for k, (src, dst, dev) in enumerate(plan(in_refs, out_refs, x, y, c, dry=False))]

    def start(*args):
        for cp in copies(*args):
            cp.start()

    def finish(*args):
        for cp in copies(*args):
            cp.wait()

    return _Exchange(ins, outs, n_sems, [(0.0, start), (1.0, finish)], result_sources=result_sources)


def _pair_exchange(gs):
    def plan(in_refs, out_refs, x, y, c, dry):
        out = []
        for a, g in enumerate(gs):
            half = g.shape[1] // 2
            for k in range(N_CHIPS):
                out.append(None if dry else (in_refs[a].at[k, pl.ds((1 - c) * half, half), :], out_refs[a].at[k],
                                             (x, y, 1 - c)))
        return out

    outs = [jax.ShapeDtypeStruct((g.shape[0], g.shape[1] // 2, g.shape[2]), g.dtype) for g in gs]
    return _symmetric_exchange(gs, outs, plan)


def _w_in_pair_plan(slabs):
    half = W_IN_SH // 2

    def plan(in_refs, out_refs, x, y, c, dry):
        return [None if dry else (in_refs[0].at[pl.ds(k * W_IN_SH + (1 - c) * half, half), :], out_refs[0].at[j],
                                  (x, y, 1 - c)) for j, k in enumerate(slabs)]

    return plan, [jax.ShapeDtypeStruct((len(slabs), half, D_MODEL), F32)]


def _pair_exchange_w_in(slabs, w_block=None):
    plan, outs = _w_in_pair_plan(slabs)
    if w_block is None:
        return _symmetric_exchange([], outs, plan, result_sources=[0])
    return _symmetric_exchange([w_block], outs, plan)


def _pair_sum(g, from_sibling, c_arr, *, tile, name):
    n, rows, width = g.shape
    tiles = (rows // 2) // tile
    firsts = [sum(s.shape[0] for s in from_sibling[:j]) for j in range(len(from_sibling))]

    def kern(c_ref, g_ref, *rest):
        *s_refs, o_ref = rest
        k = pl.program_id(1)
        s = s_refs[0][...]
        for first, s_ref in zip(firsts[1:], s_refs[1:]):
            s = jnp.where(k >= first, s_ref[...], s)
        o_ref[...] = _bf(g_ref[...] + s)

    def sibling_spec(first, count):
        return pl.BlockSpec((None, tile, width), lambda i, k, c: (jnp.clip(k - first, 0, count - 1), i, 0))

    return pl.pallas_call(
        kern,
        grid_spec=pltpu.PrefetchScalarGridSpec(
            num_scalar_prefetch=1, grid=(tiles, n),
            in_specs=[pl.BlockSpec((None, tile, width), lambda i, k, c: (k, c[0] * tiles + i, 0))]
            + [sibling_spec(first, s.shape[0]) for first, s in zip(firsts, from_sibling)],
            out_specs=pl.BlockSpec((None, tile, width), lambda i, k, c: (k, i, 0))),
        out_shape=jax.ShapeDtypeStruct((n, rows // 2, width), BF16), name=name,
        compiler_params=_params(("parallel", "parallel")),
    )(c_arr, g, *from_sibling)


def _scatter_to_owners(hsums):
    def plan(in_refs, out_refs, x, y, c, dry):
        out = []
        for a in range(len(hsums)):
            for k, (cx, cy) in enumerate([(1 - x, y), (x, 1 - y), (1 - x, 1 - y)]):
                out.append(None if dry else (in_refs[a].at[2 * cx + cy], out_refs[a].at[k], (cx, cy, c)))
        return out

    outs = [jax.ShapeDtypeStruct((3, *h.shape[1:]), h.dtype) for h in hsums]
    return _symmetric_exchange(hsums, outs, plan)


def _sum_chips(hsum, parts, chip_arr, *, tile, name):
    n, half, width = parts.shape

    def kern(chip_ref, h_ref, p_ref, o_ref):
        acc = h_ref[...].astype(F32)
        for k in range(n):
            acc = acc + p_ref[k].astype(F32)
        o_ref[...] = acc

    return pl.pallas_call(
        kern,
        grid_spec=pltpu.PrefetchScalarGridSpec(
            num_scalar_prefetch=1, grid=(half // tile,),
            in_specs=[pl.BlockSpec((None, tile, width), lambda i, chip: (chip[0], i, 0)),
                      pl.BlockSpec((n, tile, width), lambda i, chip: (0, i, 0))],
            out_specs=pl.BlockSpec((tile, width), lambda i, chip: (i, 0))),
        out_shape=jax.ShapeDtypeStruct((half, width), F32), name=name,
        compiler_params=_params(("parallel",)),
    )(chip_arr, hsum, parts)


def _share_halves(fhalves, w_in_slabs=()):
    n = len(fhalves)
    pair_plan, pair_outs = _w_in_pair_plan(w_in_slabs)

    def plan(in_refs, out_refs, x, y, c, dry):
        share = [None if dry else (in_refs[a], out_refs[a], (x, y, 1 - c)) for a in range(n)]
        return share + (pair_plan(in_refs[n:], out_refs[n:], x, y, c, dry) if w_in_slabs else [])

    outs = [jax.ShapeDtypeStruct(f.shape, f.dtype) for f in fhalves] + (pair_outs if w_in_slabs else [])
    return _symmetric_exchange(fhalves, outs, plan, result_sources=[0] if w_in_slabs else [])


def _adamw_math(w, g, m, v):
    m = ADAM_B1 * m + (1.0 - ADAM_B1) * g
    v = ADAM_B2 * v + (1.0 - ADAM_B2) * (g * g)
    m_hat = m / (1.0 - ADAM_B1 ** ADAM_STEP)
    v_hat = v / (1.0 - ADAM_B2 ** ADAM_STEP)
    delta = -ADAM_LR * (m_hat / (jnp.sqrt(v_hat) + ADAM_EPS) + ADAM_WD * w)
    return delta, m, v


def _adamw(mats, g_mine, g_other, c_arr, *, tile, name):
    width = g_mine.shape[1]
    tiles_per_half = g_mine.shape[0] // tile
    n_tiles = [w.shape[0] // tile for w, _, _, _ in mats]
    n_mats = len(mats)

    def kern(c_ref, *refs):
        ins, outs = refs[:5 * n_mats], refs[5 * n_mats:]
        for j, (_, _, _, row_off) in enumerate(mats):
            w_ref, gm_ref, go_ref, m_ref, v_ref = ins[5 * j:5 * j + 5]
            i = jnp.minimum(pl.program_id(0), n_tiles[j] - 1)
            in_my_half = ((row_off // tile + i) // tiles_per_half) == c_ref[0]
            g = jnp.where(in_my_half, gm_ref[...], go_ref[...])
            d, nm, nv = _adamw_math(w_ref[...], g, m_ref[...], v_ref[...])
            for out_ref, val in zip(outs[4 * j:4 * j + 4], (g, d, nm, nv)):
                out_ref[...] = val

    in_specs, out_specs, out_shape, args = [], [], [], []
    for (w, m, v, row_off), nt in zip(mats, n_tiles):
        full = pl.BlockSpec((tile, width), lambda i, c, nt=nt: (jnp.minimum(i, nt - 1), 0))

        def half(mine, nt=nt, first=row_off // tile):
            def index(i, c):
                pos = first + jnp.minimum(i, nt - 1)
                used = ((pos // tiles_per_half) == c[0]) == mine
                return (jnp.where(used, pos % tiles_per_half, 0), 0)
            return pl.BlockSpec((tile, width), index)

        in_specs += [full, half(True), half(False), full, full]
        out_specs += [full] * 4
        out_shape += [jax.ShapeDtypeStruct(w.shape, F32)] * 4
        args += [w, g_mine, g_other, m, v]
    outs = pl.pallas_call(
        kern,
        grid_spec=pltpu.PrefetchScalarGridSpec(num_scalar_prefetch=1, grid=(max(n_tiles),), in_specs=in_specs,
                                               out_specs=out_specs),
        out_shape=out_shape, name=name, compiler_params=_params(("arbitrary",)),
    )(c_arr, *args)
    return [outs[4 * j:4 * j + 4] for j in range(n_mats)]


def _small_step(partials, params, w_in_half):
    slots = ((0, 0, D_MODEL), (1, 0, D_MODEL), (2, 0, HEAD_DIM), (2, 128, HEAD_DIM), (2, 256, N_Q_HEADS))
    loss_slot = (2, 384, 128)

    def body(*refs):
        loss_ref, dg1_ref, dg2_ref, dgq_ref, dgk_ref, dsink_ref = refs[:6]
        p_refs, half_ref, out_refs, other_ref = refs[6:21], refs[21], refs[22:43], refs[43]
        mine, gathered, send_sems, recv_sems = refs[44:]
        x, y, c = _position()
        me = 4 * x + 2 * y + c
        halves = pltpu.make_async_remote_copy(
            src_ref=half_ref, dst_ref=other_ref, send_sem=send_sems.at[N_DEV - 1], recv_sem=recv_sems.at[N_DEV - 1],
            device_id=(x, y, 1 - c), device_id_type=MESH)
        halves.start()
        mine[...] = jnp.zeros_like(mine)
        for (row, lane, n), val in zip(slots + (loss_slot,), (
                jnp.sum(dg1_ref[...], axis=0, keepdims=True), jnp.sum(dg2_ref[...], axis=0, keepdims=True),
                dgq_ref[...], dgk_ref[...], dsink_ref[...], jnp.sum(loss_ref[...], axis=0, keepdims=True))):
            mine[row:row + 1, lane:lane + n] = val
        copies = []
        for k in range(1, N_DEV):
            flip = (k >> 2) & 1, (k >> 1) & 1, k & 1
            to = (x ^ flip[0], y ^ flip[1], c ^ flip[2])
            cp = pltpu.make_async_remote_copy(
                src_ref=mine, dst_ref=gathered.at[me], send_sem=send_sems.at[k - 1], recv_sem=recv_sems.at[k - 1],
                device_id=to, device_id_type=MESH)
            cp.start()
            copies.append(cp)
        gathered[me] = mine[...]
        for k in range(1, N_DEV):
            flip = (k >> 2) & 1, (k >> 1) & 1, k & 1
            src = 4 * (x ^ flip[0]) + 2 * (y ^ flip[1]) + (c ^ flip[2])
            pltpu.make_async_remote_copy(
                src_ref=mine, dst_ref=gathered.at[src], send_sem=send_sems.at[k - 1], recv_sem=recv_sems.at[k - 1],
                device_id=(x, y, c), device_id_type=MESH).wait_recv()
        for cp in copies:
            cp.wait_send()
        total = gathered[0]
        for k in range(1, N_DEV):
            total = total + gathered[k]
        row, lane, n = loss_slot
        out_refs[0][...] = total[row:row + 1, lane:lane + n]
        for i, (row, lane, n) in enumerate(slots):
            g = total[row:row + 1, lane:lane + n]
            d, nm, nv = _adamw_math(p_refs[i][...], g, p_refs[5 + i][...], p_refs[10 + i][...])
            for kind, val in enumerate((g, d, nm, nv)):
                out_refs[1 + 5 * kind + i][...] = val
        halves.wait()

    vm = pl.BlockSpec(memory_space=pltpu.VMEM)
    shapes = [jax.ShapeDtypeStruct((1, 128), F32)] + [jax.ShapeDtypeStruct((1, n), F32) for _, _, n in slots] * 4
    return pl.pallas_call(
        body, in_specs=[vm] * 21 + [_ANY], out_specs=[vm] * 21 + [_ANY],
        out_shape=shapes + [jax.ShapeDtypeStruct(w_in_half.shape, w_in_half.dtype)],
        scratch_shapes=[pltpu.VMEM((SMALL_ROWS, D_MODEL), F32), pltpu.VMEM((N_DEV, SMALL_ROWS, D_MODEL), F32),
                        pltpu.SemaphoreType.DMA((N_DEV,)), pltpu.SemaphoreType.DMA((N_DEV,))],
        name="small_step",
    )(*partials, *params, w_in_half)


def kernel(x, norm_mix_gain, w_in, q_norm_gain, k_norm_gain, attn_sinks, w_branch_attn, w_branch_ret, w_out, norm_ffn_gain, w_ffn_gate, w_ffn_up, w_ffn_down, loss_target, m_norm_mix_gain, m_w_in, m_q_norm_gain, m_k_norm_gain, m_attn_sinks, m_w_branch_attn, m_w_branch_ret, m_w_out, m_norm_ffn_gain, m_w_ffn_gate, m_w_ffn_up, m_w_ffn_down, v_norm_mix_gain, v_w_in, v_q_norm_gain, v_k_norm_gain, v_attn_sinks, v_w_branch_attn, v_w_branch_ret, v_w_out, v_norm_ffn_gain, v_w_ffn_gate, v_w_ffn_up, v_w_ffn_down):
    my_chip = 2 * lax.axis_index("x") + lax.axis_index("y")
    c_arr = lax.axis_index("c").astype(jnp.int32).reshape(1)
    chip_arr = my_chip.astype(jnp.int32).reshape(1)
    x_t, target = x[0], loss_target[0]
    g1, g2, gq, gk, sinks = norm_mix_gain, norm_ffn_gain, q_norm_gain, k_norm_gain, attn_sinks

    tr = lambda a: jnp.transpose(a[0])
    own_w_in = _bf(tr(w_in))
    own_rest = [_bf(a) for a in (tr(w_ffn_gate), tr(w_ffn_up), w_ffn_down[0], w_branch_attn[0], w_branch_ret[0],
                                 w_out[0])]
    tables, (got_w_in,) = _ret_tables(x_t.shape[0], _gather_exchange([own_w_in], 0.9))
    w_in_t = got_w_in.reshape(D_IN, D_MODEL)
    h1, q_a, kv_a, q_r, k_r, v_r, g_r, z_a, z_r, *got_rest = _proj_fwd(x_t, g1, w_in_t, _gather_exchange(own_rest, 0.8))
    wg_t, wu_t, wd, wba, wbr, wout = [got.reshape(-1, D_MODEL) for got in got_rest]

    gq_col, gk_col = gq.reshape(HEAD_DIM, 1), gk.reshape(HEAD_DIM, 1)
    attn, probs, sink_probs, o_ret, ret, states = _fused(
        [_attn_fwd(q_a, kv_a, gq_col, gk, sinks), _ret_fwd(q_r, k_r, v_r, g_r, tables)],
        grid=(x_t.shape[0] // BLOCK,), name="mixers_fwd")
    ba, br, merged, x1, h2 = _mix_fwd(attn, ret, z_a, z_r, x_t, wba, wbr, wout, g2)
    act, dgate, dup, dyb, dx1, dx1b, loss_p, dg2_p = _ffn_fwd_bwd(h2, x1, target, wg_t, wu_t, wd, g2)

    def pairs(row0, rows):
        return lambda i: [(h * rows, rows, (2 * i + h, pl.ds(row0, rows), slice(None))) for h in range(2)]

    f_block = jax.ShapeDtypeStruct((N_CHIPS, 3 * FF_SH, D_MODEL), F32)
    f_block, = _dw(dgate, h2, tm=2 * FF_SH, place=pairs(0, FF_SH), buf=f_block, name="dw_gate")
    f_block, = _dw(dup, h2, tm=2 * FF_SH, place=pairs(FF_SH, FF_SH), buf=f_block, name="dw_up")
    f_block, = _dw(act, dyb, tm=2 * FF_SH, place=pairs(2 * FF_SH, FF_SH), buf=f_block, name="dw_down")
    (dba, dbr, d_attn, d_o, d_gz, sib_ffn) = _mix_bwd(
        dx1b, z_a, z_r, ba, br, g_r, o_ret, wout, wba, wbr, _pair_exchange([f_block]))
    f_sum = _pair_sum(f_block, [sib_ffn], c_arr, tile=528, name="pair_sum_ffn")

    def quarters(row0, rows):
        return lambda i: [(k * rows, rows, (k, pl.ds(row0, rows), slice(None))) for k in range(N_CHIPS)]

    m_block = jax.ShapeDtypeStruct((N_CHIPS, D_MODEL, D_MODEL), F32)
    m_block, = _dw(attn, dba, tm=ATT_Q, place=quarters(0, 256), buf=m_block, name="dw_ba")
    m_block, = _dw(ret, dbr, tm=D_MODEL, place=pairs(256, 512), buf=m_block, name="dw_br")
    m_block, = _dw(merged, dx1b, tm=D_MODEL, place=quarters(768, 256), buf=m_block, name="dw_out")

    def w_in_rows(off, w):
        tm = min(w, D_MODEL)
        return dict(tm=tm, place=lambda i: [(0, tm, (pl.ds(off + i * tm, tm), slice(None)))])

    w_block = jax.ShapeDtypeStruct((D_IN, D_MODEL), F32)
    w_block, sib_mix = _dw(d_gz, h1, buf=w_block, name="dw_in_gz", exchange=_pair_exchange([m_block]),
                           **w_in_rows(P_GR[0], d_gz.shape[1]))
    m_sum = _pair_sum(m_block, [sib_mix], c_arr, tile=256, name="pair_sum_mix")

    (dq_a, dkv_a, dgq, dgk, dsinks, d_ret, got_ffn_sums, got_mix_sums) = _fused(
        [_attn_bwd(q_a, kv_a, d_attn, probs, sink_probs, gq_col, gk, gk_col),
         _ret_bwd(q_r, k_r, v_r, d_o, states, tables)],
        grid=(x_t.shape[0] // BLOCK + 1,), name="mixers_bwd", exchange=_scatter_to_owners([f_sum, m_sum]))
    dgq = dgq.reshape(1, HEAD_DIM)
    ffn_half = _sum_chips(f_sum, got_ffn_sums, chip_arr, tile=528, name="sum_chips_ffn")
    mix_half = _sum_chips(m_sum, got_mix_sums, chip_arr, tile=256, name="sum_chips_mix")
    w_block, ffn_other, mix_other, sib_w_in_3 = _dw(
        d_ret, h1, buf=w_block, name="dw_in_ret", exchange=_share_halves([ffn_half, mix_half], w_in_slabs=(3,)),
        **w_in_rows(P_QR[0], d_ret.shape[1]))
    qkv_rows = P_QA[1] + P_KVA[1]
    w_block, sib_w_in_12 = _dw([dq_a, dkv_a], h1, buf=w_block, name="dw_in_q_kv", tm=qkv_rows,
                               exchange=_pair_exchange_w_in((1, 2)),
                               place=lambda i: [(0, qkv_rows, (pl.ds(P_QA[0], qkv_rows), slice(None)))])
    sib_w_in_0, = _run_exchange(_pair_exchange_w_in((0,), w_block), "pair_exchange_w_in")
    w_sum = _pair_sum(w_block.reshape(N_CHIPS, W_IN_SH, D_MODEL), [sib_w_in_0, sib_w_in_12, sib_w_in_3], c_arr,
                      tile=592, name="pair_sum_w_in")
    d_pieces = [dq_a, dkv_a, d_ret, d_gz]
    grad_x, dg1_p, got_w_in_sums = _proj_bwd(d_pieces, x_t, dx1, w_in_t, g1, _scatter_to_owners([w_sum]))
    w_in_half = _sum_chips(w_sum, got_w_in_sums, chip_arr, tile=592, name="sum_chips_w_in")
    loss_row, *small, w_in_other = _small_step(
        [loss_p.reshape(-1, 128), dg1_p.reshape(-1, D_MODEL), dg2_p.reshape(-1, D_MODEL), dgq, dgk, dsinks],
        [norm_mix_gain, norm_ffn_gain, q_norm_gain, k_norm_gain, attn_sinks,
         m_norm_mix_gain, m_norm_ffn_gain, m_q_norm_gain, m_k_norm_gain, m_attn_sinks,
         v_norm_mix_gain, v_norm_ffn_gain, v_q_norm_gain, v_k_norm_gain, v_attn_sinks], w_in_half)
    loss = loss_row[0, 0]

    def update(name, g_half, g_other, tile, mats):
        outs = _adamw([tuple(tr(a) if t else a[0] for a in wmv) + (off,) for _, *wmv, off, t in mats],
                      g_half, g_other, c_arr, tile=tile, name=f"adamw_{name}")
        return {key: [jnp.transpose(o) if t else o for o in res] for (key, _, _, _, _, t), res in zip(mats, outs)}

    big = {
        **update("w_in", w_in_half, w_in_other, 592, [("w_in", w_in, m_w_in, v_w_in, 0, True)]),
        **update("ffn", ffn_half, ffn_other, 176, [
            ("wg", w_ffn_gate, m_w_ffn_gate, v_w_ffn_gate, 0, True),
            ("wu", w_ffn_up, m_w_ffn_up, v_w_ffn_up, FF_SH, True),
            ("wd", w_ffn_down, m_w_ffn_down, v_w_ffn_down, 2 * FF_SH, False)]),
        **update("mix", mix_half, mix_other, 128, [
            ("wba", w_branch_attn, m_w_branch_attn, v_w_branch_attn, 0, False),
            ("wbr", w_branch_ret, m_w_branch_ret, v_w_branch_ret, 256, False),
            ("wout", w_out, m_w_out, v_w_out, 768, False)])}

    def leaves(i):
        b = [big[n][i][None] for n in ("w_in", "wba", "wbr", "wout", "wg", "wu", "wd")]
        s1, s2, sq, sk, ss = small[5 * i:5 * i + 5]
        return [s1, b[0], sq, sk, ss, b[1], b[2], b[3], s2, b[4], b[5], b[6]]

    return (loss, grad_x[None], *leaves(0), *leaves(1), *leaves(2), *leaves(3))
```

```python
import jax
import jax.numpy as jnp
from jax import lax
from jax.experimental import pallas as pl
from jax.experimental.pallas import tpu as pltpu

F32 = jnp.float32
BF16 = jnp.bfloat16
MESH = pl.DeviceIdType.MESH

D_MODEL = 1024
EPS = 1e-6
HEAD_DIM = 64
N_Q_HEADS = 16
N_KV_HEADS = 2
GROUP = 8
BLOCK = 128
RET_HEADS = 4
RET_QK_DIM = 256
RET_V_DIM = 512
RET_CHUNK = 128
RET_ROT_BASE = 10000.0
D_FF = 2816
ATT_Q = N_Q_HEADS * HEAD_DIM
ATT_KV = N_KV_HEADS * HEAD_DIM
RET_QK = RET_HEADS * RET_QK_DIM
RET_V = RET_HEADS * RET_V_DIM
D_IN = 9472
ADAM_LR = 0.001
ADAM_B1 = 0.9
ADAM_B2 = 0.999
ADAM_EPS = 1e-08
ADAM_WD = 0.01
ADAM_STEP = 10

N_CHIPS = 4
N_DEV = 8
VMEM_LIMIT_BYTES = 60 * 1024 * 1024

P_QA = (0, 1024)
P_KVA = (1024, 256)
P_QR = (1280, 1024)
P_KR = (2304, 1024)
P_VR = (3328, 2048)
P_GR = (5376, 2048)
P_ZA = (7424, 1024)
P_ZR = (8448, 1024)

W_IN_SH = D_IN // N_CHIPS
FF_SH = D_FF // N_CHIPS

SMALL_ROWS = 8


def _dot(a, b):
    return jnp.dot(a, b, preferred_element_type=F32)


def _dot_nt(a, b):
    return lax.dot_general(a, b, (((1,), (1,)), ((), ())), preferred_element_type=F32)


def _dot_tn(a, b):
    return lax.dot_general(a, b, (((0,), (0,)), ((), ())), preferred_element_type=F32)


def _bf(x):
    return x.astype(BF16)


def _rms_stats(x):
    r = lax.rsqrt(jnp.mean(x * x, axis=-1, keepdims=True) + EPS)
    return r, x * r


def _rms_bwd(dy, xhat, r, gain):
    u = dy * gain
    dx = r * (u - xhat * jnp.mean(u * xhat, axis=-1, keepdims=True))
    return dx, dy * xhat


def _params(sem):
    return pltpu.CompilerParams(dimension_semantics=sem, vmem_limit_bytes=VMEM_LIMIT_BYTES)


_ANY = pl.BlockSpec(memory_space=pl.ANY)


class _Exchange:
    def __init__(self, ins, outs, n_sems, phases, staging=(), result_sources=()):
        self.ins, self.outs, self.n_sems, self.phases = list(ins), list(outs), n_sems, list(phases)
        self.staging, self.result_sources = list(staging), list(result_sources)


def _pallas(kern, *, grid, in_specs, out_specs, out_shape, args, name, scratch=(), exchange=None, aliases=None):
    aliases = aliases or {}
    if exchange is None:
        return pl.pallas_call(
            kern, grid=grid, in_specs=in_specs, out_specs=out_specs, out_shape=out_shape, name=name,
            scratch_shapes=list(scratch), input_output_aliases=aliases,
            compiler_params=_params(("arbitrary",) * len(grid)))(*args)
    n_in, n_out, n_sc = len(in_specs), len(out_specs), len(scratch)
    n_xi, n_xo, n_xs = len(exchange.ins), len(exchange.outs), len(exchange.staging)
    n_steps = 1
    for g in grid:
        n_steps *= g

    def wrapped(*refs):
        ins, refs = refs[:n_in], refs[n_in:]
        x_ins, refs = refs[:n_xi], refs[n_xi:]
        outs, refs = refs[:n_out], refs[n_out:]
        x_outs, refs = refs[:n_xo], refs[n_xo:]
        scr, refs = refs[:n_sc], refs[n_sc:]
        staging, (send_sems, recv_sems) = refs[:n_xs], refs[n_xs:]
        x_ins = list(x_ins) + [outs[j] for j in exchange.result_sources]
        step = pl.program_id(0)
        for d in range(1, len(grid)):
            step = step * grid[d] + pl.program_id(d)
        for frac, fn in exchange.phases:
            at = min(int(frac * n_steps), n_steps - 1)

            @pl.when(step == at)
            def _(fn=fn):
                fn(x_ins, x_outs, send_sems, recv_sems, staging)

        kern(*ins, *outs, *scr)

    sems = [pltpu.SemaphoreType.DMA((exchange.n_sems,)), pltpu.SemaphoreType.DMA((exchange.n_sems,))]
    return pl.pallas_call(
        wrapped, grid=grid, in_specs=list(in_specs) + [_ANY] * n_xi, out_specs=list(out_specs) + [_ANY] * n_xo,
        out_shape=list(out_shape) + exchange.outs, name=name,
        scratch_shapes=list(scratch) + exchange.staging + sems, input_output_aliases=aliases,
        compiler_params=_params(("arbitrary",) * len(grid)))(*args, *exchange.ins)


def _run_exchange(exchange, name):
    def body(*refs):
        n_i, n_o = len(exchange.ins), len(exchange.outs)
        staging, (send_sems, recv_sems) = refs[n_i + n_o:-2], refs[-2:]
        for _, fn in exchange.phases:
            fn(refs[:n_i], refs[n_i:n_i + n_o], send_sems, recv_sems, staging)

    sems = [pltpu.SemaphoreType.DMA((exchange.n_sems,)), pltpu.SemaphoreType.DMA((exchange.n_sems,))]
    return pl.pallas_call(body, in_specs=[_ANY] * len(exchange.ins), out_specs=[_ANY] * len(exchange.outs),
                          out_shape=exchange.outs, scratch_shapes=exchange.staging + sems, name=name,
                          compiler_params=pltpu.CompilerParams(vmem_limit_bytes=VMEM_LIMIT_BYTES))(*exchange.ins)


def _fused(parts, *, grid, name, exchange=None):
    counts = [(len(p["in_specs"]), len(p["out_specs"]), len(p["scratch"])) for p in parts]
    n_in, n_out = sum(c[0] for c in counts), sum(c[1] for c in counts)

    def kern(*refs):
        ins, outs, scr = refs[:n_in], refs[n_in:n_in + n_out], refs[n_in + n_out:]
        i0 = o0 = s0 = 0
        for p, (ni, no, ns) in zip(parts, counts):
            p["kern"](*ins[i0:i0 + ni], *outs[o0:o0 + no], *scr[s0:s0 + ns])
            i0, o0, s0 = i0 + ni, o0 + no, s0 + ns

    cat = lambda key: [a for p in parts for a in p[key]]
    return _pallas(kern, grid=grid, in_specs=cat("in_specs"), out_specs=cat("out_specs"), out_shape=cat("out_shape"),
                   scratch=cat("scratch"), args=cat("args"), name=name, exchange=exchange)


def _row_call(body, *, tm, row_ins, res_ins, row_outs, part_outs=(), name, exchange=None):
    t = row_ins[0].shape[0]
    n_tiles = t // tm
    in_specs = [pl.BlockSpec((tm, a.shape[1]), lambda i: (i, 0)) for a in row_ins]
    in_specs += [pl.BlockSpec(a.shape, lambda i: (0, 0), pipeline_mode=pl.Buffered(1)) for a in res_ins]
    out_shape = [jax.ShapeDtypeStruct((t, w), dt) for (w, dt) in row_outs]
    out_shape += [jax.ShapeDtypeStruct((n_tiles, 1, w), F32) for w in part_outs]
    out_specs = [pl.BlockSpec((tm, w), lambda i: (i, 0)) for (w, _) in row_outs]
    out_specs += [pl.BlockSpec((1, 1, w), lambda i: (i, 0, 0)) for w in part_outs]
    n_ri, n_re, n_ro = len(row_ins), len(res_ins), len(row_outs)

    def kern(*refs):
        body(refs[:n_ri], refs[n_ri:n_ri + n_re], refs[n_ri + n_re:n_ri + n_re + n_ro], refs[n_ri + n_re + n_ro:])

    return _pallas(kern, grid=(n_tiles,), in_specs=in_specs, out_specs=out_specs, out_shape=out_shape,
                   args=[*row_ins, *res_ins], name=name, exchange=exchange)


def _proj_fwd(x, g1, w_in_t, exchange):
    pieces = ((P_QA, F32), (P_KVA, F32), (P_QR, F32), (P_KR, F32), (P_VR, BF16), (P_GR, F32), (P_ZA, F32), (P_ZR, F32))

    def body(ri, re, ro, po):
        x_t = ri[0][...]
        r, xhat = _rms_stats(x_t)
        hb = _bf(xhat * re[0][...])
        ro[0][...] = hb
        for k, ((off, w), dt) in enumerate(pieces):
            ro[1 + k][...] = _dot_nt(hb, re[1][off:off + w, :]).astype(dt)

    outs = [(D_MODEL, BF16)] + [(w, dt) for ((_, w), dt) in pieces]
    return _row_call(body, tm=256, row_ins=[x], res_ins=[g1, w_in_t], row_outs=outs, name="proj_fwd",
                     exchange=exchange)


def _mix_fwd(attn, ret, z_a, z_r, x, wba, wbr, wout, g2):
    def body(ri, re, ro, po):
        ba = _dot(ri[0][...], re[0][...])
        br = _dot(ri[1][...], re[1][...])
        m = jax.nn.sigmoid(ri[2][...]) * ba + jax.nn.sigmoid(ri[3][...]) * br
        mb = _bf(m)
        x1 = ri[4][...] + _dot(mb, re[2][...])
        r, xhat = _rms_stats(x1)
        ro[0][...] = ba
        ro[1][...] = br
        ro[2][...] = mb
        ro[3][...] = x1
        ro[4][...] = _bf(xhat * re[3][...])

    outs = [(D_MODEL, F32), (D_MODEL, F32), (D_MODEL, BF16), (D_MODEL, F32), (D_MODEL, BF16)]
    return _row_call(body, tm=512, row_ins=[attn, ret, z_a, z_r, x], res_ins=[wba, wbr, wout, g2], row_outs=outs,
                     name="mix_fwd")


def _ffn_fwd_bwd(h2, x1, target, wg_t, wu_t, wd, g2):
    def body(ri, re, ro, po):
        h2_t = ri[0][...]
        x1_t = ri[1][...]
        gate = _dot_nt(h2_t, re[0][...])
        up = _dot_nt(h2_t, re[1][...])
        sg = jax.nn.sigmoid(gate)
        sl = gate * sg
        actb = _bf(sl * up)
        ro[0][...] = actb
        y = x1_t + _dot(actb, re[2][...])
        e = y - ri[2][...]
        po[0][0] = jnp.broadcast_to(0.5 * jnp.sum(jnp.sum(e * e, axis=1, keepdims=True), axis=0, keepdims=True)
                                    * (1.0 / D_MODEL), (1, 128))
        dy = e * (1.0 / D_MODEL)
        dyb = _bf(dy)
        ro[3][...] = dyb
        dact = _dot_nt(dyb, re[2][...])
        dupb = _bf(dact * sl)
        dgateb = _bf(dact * up * (sg * (1.0 + gate * (1.0 - sg))))
        ro[1][...] = dgateb
        ro[2][...] = dupb
        dh2 = _dot(dgateb, re[0][...]) + _dot(dupb, re[1][...])
        r, xhat = _rms_stats(x1_t)
        dxn, dgain = _rms_bwd(dh2, xhat, r, re[3][...])
        dx1 = dy + dxn
        ro[4][...] = dx1
        ro[5][...] = _bf(dx1)
        po[1][0] = jnp.sum(dgain, axis=0, keepdims=True)

    outs = [(D_FF, BF16), (D_FF, BF16), (D_FF, BF16), (D_MODEL, BF16), (D_MODEL, F32), (D_MODEL, BF16)]
    return _row_call(body, tm=256, row_ins=[h2, x1, target], res_ins=[wg_t, wu_t, wd, g2], row_outs=outs,
                     part_outs=(128, D_MODEL), name="ffn_fwd_bwd")


def _mix_bwd(dx1b, z_a, z_r, ba, br, g_r, o_ret, wout, wba, wbr, exchange):
    def body(ri, re, ro, po):
        dm = _dot_nt(ri[0][...], re[0][...])
        sa = jax.nn.sigmoid(ri[1][...])
        sr = jax.nn.sigmoid(ri[2][...])
        dbab = _bf(sa * dm)
        dbrb = _bf(sr * dm)
        ro[0][...] = dbab
        ro[1][...] = dbrb
        ro[4][:, RET_V:RET_V + D_MODEL] = _bf(dm * ri[3][...] * (sa * (1.0 - sa)))
        ro[4][:, RET_V + D_MODEL:RET_V + 2 * D_MODEL] = _bf(dm * ri[4][...] * (sr * (1.0 - sr)))
        ro[2][...] = _bf(_dot_nt(dbab, re[1][...]))
        dret = _dot_nt(dbrb, re[2][...])
        for h in range(RET_HEADS):
            cols = slice(h * RET_V_DIM, (h + 1) * RET_V_DIM)
            g = ri[5][:, cols]
            r, rn = _rms_stats(ri[6][:, cols])
            sg = jax.nn.sigmoid(g)
            dret_h = dret[:, cols]
            d_rn = dret_h * (g * sg)
            ro[4][:, cols] = _bf(dret_h * rn * (sg * (1.0 + g * (1.0 - sg))))
            ro[3][:, cols] = r * (d_rn - rn * jnp.mean(d_rn * rn, axis=-1, keepdims=True))

    outs = [(D_MODEL, BF16), (D_MODEL, BF16), (ATT_Q, BF16), (RET_V, F32), (RET_V + 2 * D_MODEL, BF16)]
    return _row_call(body, tm=256, row_ins=[dx1b, z_a, z_r, ba, br, g_r, o_ret], res_ins=[wout, wba, wbr],
                     row_outs=outs, name="mix_bwd", exchange=exchange)


def _proj_bwd(d_pieces, x, dx1, w_in_t, g1, exchange):
    widths = [p.shape[1] for p in d_pieces]
    groups = [(sum(widths[:k]), w) for k, w in enumerate(widths)]
    n_p = len(groups)

    def body(ri, re, ro, po):
        dh = None
        for k, (off, w) in enumerate(groups):
            term = _dot(ri[k][...], re[0][off:off + w, :])
            dh = term if dh is None else dh + term
        r, xhat = _rms_stats(ri[n_p][...])
        dxn, dgain = _rms_bwd(dh, xhat, r, re[1][...])
        ro[0][...] = ri[n_p + 1][...] + dxn
        po[0][0] = jnp.sum(dgain, axis=0, keepdims=True)

    return _row_call(body, tm=512, row_ins=[*d_pieces, x, dx1], res_ins=[w_in_t, g1], row_outs=[(D_MODEL, F32)],
                     part_outs=(D_MODEL,), name="proj_bwd", exchange=exchange)


def _dw(a, b, *, tm, place, buf, name, exchange=None, tk=2048):
    a_list = list(a) if isinstance(a, (list, tuple)) else [a]
    n_a = len(a_list)
    t, m = a_list[0].shape[0], sum(x.shape[1] for x in a_list)
    widths = [tm * x.shape[1] // m for x in a_list]
    assert sum(widths) == tm
    n = b.shape[1]
    tk = min(tk, t)
    n_i, n_k = m // tm, t // tk
    fresh = isinstance(buf, jax.ShapeDtypeStruct)
    n_copies = len(place(0))

    def kern(*refs):
        a_refs, b_ref = refs[:n_a], refs[n_a]
        out_ref, acc, sems = refs[-3:]
        i, k = pl.program_id(0), pl.program_id(1)
        off = 0
        for a_ref in a_refs:
            w = a_ref.shape[1]
            part = _dot_tn(a_ref[...], b_ref[...])

            @pl.when(k == 0)
            def _(part=part, off=off, w=w):
                acc[i, off:off + w, :] = part

            @pl.when(k > 0)
            def _(part=part, off=off, w=w):
                acc[i, off:off + w, :] += part

            off += w

        def copies(tile):
            return [pltpu.make_async_copy(acc.at[tile, pl.ds(r0, rows), :], out_ref.at[idx], sems.at[tile * n_copies + c])
                    for c, (r0, rows, idx) in enumerate(place(tile))]

        for tile in range(n_i):
            @pl.when((i == tile) & (k == n_k - 1))
            def _(tile=tile):
                for cp in copies(tile):
                    cp.start()

        @pl.when((i == n_i - 1) & (k == n_k - 1))
        def _():
            for tile in range(n_i):
                for cp in copies(tile):
                    cp.wait()

    in_specs = [pl.BlockSpec((tk, w), lambda i, k: (k, i)) for w in widths]
    in_specs.append(pl.BlockSpec((tk, n), lambda i, k: (k, 0)))
    shape = buf if fresh else jax.ShapeDtypeStruct(buf.shape, buf.dtype)
    return _pallas(
        kern, grid=(n_i, n_k), in_specs=in_specs + ([] if fresh else [_ANY]), out_specs=[_ANY], out_shape=[shape],
        scratch=[pltpu.VMEM((n_i, tm, n), F32), pltpu.SemaphoreType.DMA((n_i * n_copies,))],
        args=a_list + [b] + ([] if fresh else [buf]), aliases=None if fresh else {n_a + 1: 0}, name=name,
        exchange=exchange)


def _heads_to_lanes(x3):
    return jnp.concatenate([x3[g] for g in range(GROUP)], axis=1)


def _lanes_to_heads(xt):
    return jnp.concatenate([xt[:, g * BLOCK:(g + 1) * BLOCK] for g in range(GROUP)], axis=0)


def _attn_queries(kvh, q_ref, gq_col):
    cols = slice(kvh * GROUP * HEAD_DIM, (kvh + 1) * GROUP * HEAD_DIM)
    q3 = q_ref[:, cols].T.reshape(GROUP, HEAD_DIM, BLOCK)
    rq = lax.rsqrt(jnp.mean(q3 * q3, axis=1, keepdims=True) + EPS)
    qhat = q3 * rq
    return qhat, rq, _heads_to_lanes(_bf(qhat * (gq_col * (HEAD_DIM ** -0.5))))


def _from_prev():
    j = lax.broadcasted_iota(jnp.int32, (BLOCK, GROUP * BLOCK), 0)
    i = lax.broadcasted_iota(jnp.int32, (BLOCK, GROUP * BLOCK), 1) & (BLOCK - 1)
    return j > i


def _attn_probs(n, kvh, qts, kvp_ref, kvc_ref, gk, sink_ref):
    kcols = slice(kvh * HEAD_DIM, (kvh + 1) * HEAD_DIM)
    k = jnp.concatenate([kvp_ref[:, kcols], kvc_ref[:, kcols]], axis=0)
    rk, khat = _rms_stats(k)
    st = _dot(_bf(khat * gk), qts)
    f = jnp.where(_from_prev(), jnp.where(n > 0, st[0:BLOCK], -1e30), st[BLOCK:2 * BLOCK])
    sink = jnp.concatenate([jnp.broadcast_to(sink_ref[0:1, kvh * GROUP + g:kvh * GROUP + g + 1], (1, BLOCK))
                            for g in range(GROUP)], axis=1)
    m = jnp.maximum(jnp.max(f, axis=0, keepdims=True), sink)
    e = jnp.exp(f - m)
    es = jnp.exp(sink - m)
    inv = 1.0 / (jnp.sum(e, axis=0, keepdims=True) + es)
    return e * inv, es * inv


def _unfold(from_prev, xf):
    return _bf(jnp.concatenate([jnp.where(from_prev, xf, 0.0), jnp.where(from_prev, 0.0, xf)], axis=0))


def _attn_fwd(q_a, kv_a, gq_col, gk, sinks):
    t = q_a.shape[0]
    nb = t // BLOCK

    def kern(q_ref, kvp_ref, kvc_ref, gq_ref, gk_ref, sink_ref, o_ref, pf_ref, ps_ref):
        n = pl.program_id(0)
        kvt = jnp.concatenate([kvp_ref[...].T, kvc_ref[...].T], axis=1)
        for kvh in range(N_KV_HEADS):
            _, _, qts = _attn_queries(kvh, q_ref, gq_ref[...])
            pf, psink = _attn_probs(n, kvh, qts, kvp_ref, kvc_ref, gk_ref[...], sink_ref)
            lanes = slice(kvh * GROUP * BLOCK, (kvh + 1) * GROUP * BLOCK)
            pf_ref[:, lanes] = pf
            ps_ref[:, lanes] = psink
            vt = _bf(kvt[ATT_KV + kvh * HEAD_DIM:ATT_KV + (kvh + 1) * HEAD_DIM, :])
            out_t = _dot(vt, _unfold(_from_prev(), pf))
            cols = slice(kvh * GROUP * HEAD_DIM, (kvh + 1) * GROUP * HEAD_DIM)
            o_ref[:, cols] = _bf(_lanes_to_heads(out_t).T)

    small = lambda a: pl.BlockSpec(a.shape, lambda n: (0, 0))
    folded = N_KV_HEADS * GROUP * BLOCK
    return dict(
        kern=kern,
        in_specs=[pl.BlockSpec((BLOCK, ATT_Q), lambda n: (n, 0)),
                  pl.BlockSpec((BLOCK, 2 * ATT_KV), lambda n: (jnp.maximum(n - 1, 0), 0)),
                  pl.BlockSpec((BLOCK, 2 * ATT_KV), lambda n: (n, 0)),
                  small(gq_col), small(gk), small(sinks)],
        out_specs=[pl.BlockSpec((BLOCK, ATT_Q), lambda n: (n, 0)), pl.BlockSpec((BLOCK, folded), lambda n: (n, 0)),
                   pl.BlockSpec((None, 1, folded), lambda n: (n, 0, 0))],
        out_shape=[jax.ShapeDtypeStruct((t, ATT_Q), BF16), jax.ShapeDtypeStruct((t, folded), F32),
                   jax.ShapeDtypeStruct((nb, 1, folded), F32)],
        scratch=[], args=[q_a, kv_a, kv_a, gq_col, gk, sinks])


def _attn_bwd(q_a, kv_a, d_attn, probs, sink_probs, gq_col, gk, gk_col):
    t = q_a.shape[0]
    nb = t // BLOCK

    def kern(q_ref, kvp_ref, kvc_ref, do_ref, pf_ref, ps_ref, gq_ref, gk_ref, gkc_ref,
             dq_ref, dkv_ref, dgq_ref, dgk_ref, dsink_ref, band_k, band_v, carry_k, carry_v):
        n = pl.program_id(0)
        gq_v = gq_ref[...]
        gk_v = gk_ref[...]

        @pl.when(n == 0)
        def _():
            carry_k[...] = jnp.zeros_like(carry_k)
            carry_v[...] = jnp.zeros_like(carry_v)
            dgq_ref[...] = jnp.zeros_like(dgq_ref)
            dgk_ref[...] = jnp.zeros_like(dgk_ref)
            dsink_ref[...] = jnp.zeros_like(dsink_ref)

        @pl.when(n == nb)
        def _():
            band_k[...] = jnp.zeros_like(band_k)
            band_v[...] = jnp.zeros_like(band_v)

        @pl.when(n < nb)
        def _():
            lane16 = lax.broadcasted_iota(jnp.int32, (1, N_Q_HEADS), 1)
            dsink = jnp.zeros((1, N_Q_HEADS), F32)
            dgq = jnp.zeros((HEAD_DIM, 1), F32)
            gk_col = gkc_ref[...]
            kvt = jnp.concatenate([kvp_ref[...].T, kvc_ref[...].T], axis=1)
            from_prev = _from_prev()
            for kvh in range(N_KV_HEADS):
                qhat, rq, qts = _attn_queries(kvh, q_ref, gq_v)
                lanes = slice(kvh * GROUP * BLOCK, (kvh + 1) * GROUP * BLOCK)
                pf = pf_ref[:, lanes]
                cols = slice(kvh * GROUP * HEAD_DIM, (kvh + 1) * GROUP * HEAD_DIM)
                vcols = slice(ATT_KV + kvh * HEAD_DIM, ATT_KV + (kvh + 1) * HEAD_DIM)
                dot = _heads_to_lanes(_bf(do_ref[:, cols].astype(F32).T.reshape(GROUP, HEAD_DIM, BLOCK)))
                vb = _bf(jnp.concatenate([kvp_ref[:, vcols], kvc_ref[:, vcols]], axis=0))
                dpt = _dot(vb, dot)
                dpf = jnp.where(from_prev, dpt[0:BLOCK], dpt[BLOCK:2 * BLOCK])
                delta = jnp.sum(pf * dpf, axis=0, keepdims=True)
                dst = _unfold(from_prev, pf * (dpf - delta))
                dsk = ps_ref[:, lanes] * delta
                for g in range(GROUP):
                    tot = jnp.sum(dsk[:, g * BLOCK:(g + 1) * BLOCK], axis=1, keepdims=True)
                    dsink = dsink - jnp.where(lane16 == kvh * GROUP + g, tot, 0.0)
                kt = kvt[kvh * HEAD_DIM:(kvh + 1) * HEAD_DIM, :]
                knt = _bf(kt * lax.rsqrt(jnp.mean(kt * kt, axis=0, keepdims=True) + EPS) * gk_col)
                dqn = (_dot(knt, dst) * (HEAD_DIM ** -0.5))
                band_k[kvh] = _dot_nt(dst, qts)
                band_v[kvh] = _dot_nt(_unfold(from_prev, pf), dot)
                dqn3 = _lanes_to_heads(dqn).reshape(GROUP, HEAD_DIM, BLOCK)
                u = dqn3 * gq_v
                dq3 = rq * (u - qhat * jnp.mean(u * qhat, axis=1, keepdims=True))
                dgq = dgq + jnp.sum(jnp.sum(dqn3 * qhat, axis=0), axis=1, keepdims=True)
                dq_ref[:, cols] = _bf(dq3.reshape(GROUP * HEAD_DIM, BLOCK).T)
            dsink_ref[...] += dsink
            dgq_ref[...] += dgq

        dgk = jnp.zeros((1, HEAD_DIM), F32)
        for kvh in range(N_KV_HEADS):
            kcols = slice(kvh * HEAD_DIM, (kvh + 1) * HEAD_DIM)
            vcols = slice(ATT_KV + kvh * HEAD_DIM, ATT_KV + (kvh + 1) * HEAD_DIM)
            dkn = carry_k[kvh] + band_k[kvh, 0:BLOCK, :]
            dv = carry_v[kvh] + band_v[kvh, 0:BLOCK, :]
            rk, khat = _rms_stats(kvp_ref[:, kcols])
            dk, dgain = _rms_bwd(dkn, khat, rk, gk_v)
            dgk = dgk + jnp.sum(dgain, axis=0, keepdims=True)
            dkv_ref[:, kcols] = _bf(dk)
            dkv_ref[:, vcols] = _bf(dv)
            carry_k[kvh] = band_k[kvh, BLOCK:2 * BLOCK, :]
            carry_v[kvh] = band_v[kvh, BLOCK:2 * BLOCK, :]
        dgk_ref[...] += dgk

    small = lambda a: pl.BlockSpec(a.shape, lambda n: (0, 0))
    last = nb - 1
    return dict(
        kern=kern,
        in_specs=[pl.BlockSpec((BLOCK, ATT_Q), lambda n: (jnp.minimum(n, last), 0)),
                  pl.BlockSpec((BLOCK, 2 * ATT_KV), lambda n: (jnp.maximum(n - 1, 0), 0)),
                  pl.BlockSpec((BLOCK, 2 * ATT_KV), lambda n: (jnp.minimum(n, last), 0)),
                  pl.BlockSpec((BLOCK, ATT_Q), lambda n: (jnp.minimum(n, last), 0)),
                  pl.BlockSpec((BLOCK, probs.shape[1]), lambda n: (jnp.minimum(n, last), 0)),
                  pl.BlockSpec((None, 1, probs.shape[1]), lambda n: (jnp.minimum(n, last), 0, 0)),
                  small(gq_col), small(gk), small(gk_col)],
        out_specs=[pl.BlockSpec((BLOCK, ATT_Q), lambda n: (jnp.minimum(n, last), 0)),
                   pl.BlockSpec((BLOCK, 2 * ATT_KV), lambda n: (jnp.maximum(n - 1, 0), 0)),
                   pl.BlockSpec((HEAD_DIM, 1), lambda n: (0, 0)),
                   pl.BlockSpec((1, HEAD_DIM), lambda n: (0, 0)),
                   pl.BlockSpec((1, N_Q_HEADS), lambda n: (0, 0))],
        out_shape=[jax.ShapeDtypeStruct((t, ATT_Q), BF16), jax.ShapeDtypeStruct((t, 2 * ATT_KV), BF16),
                   jax.ShapeDtypeStruct((HEAD_DIM, 1), F32), jax.ShapeDtypeStruct((1, HEAD_DIM), F32),
                   jax.ShapeDtypeStruct((1, N_Q_HEADS), F32)],
        scratch=[pltpu.VMEM((N_KV_HEADS, 2 * BLOCK, HEAD_DIM), F32),
                 pltpu.VMEM((N_KV_HEADS, 2 * BLOCK, HEAD_DIM), F32),
                 pltpu.VMEM((N_KV_HEADS, BLOCK, HEAD_DIM), F32),
                 pltpu.VMEM((N_KV_HEADS, BLOCK, HEAD_DIM), F32)],
        args=[q_a, kv_a, kv_a, d_attn, probs, sink_probs, gq_col, gk, gk_col])


def _ret_tables(t, exchange):
    theta = 1.0 / (RET_ROT_BASE ** jnp.linspace(0.0, 1.0, RET_QK_DIM // 2, dtype=F32))
    theta2 = jnp.repeat(theta, 2)[None, :]
    sign = jnp.tile(jnp.array([-1.0, 1.0], F32), RET_QK_DIM // 2)[None, :]

    def kern(theta_ref, sign_ref, cos_ref, sin_ref):
        first = pl.program_id(0) * RET_CHUNK
        pos = (first + lax.broadcasted_iota(jnp.int32, (RET_CHUNK, RET_QK_DIM), 0)).astype(F32)
        ang = pos * theta_ref[...]
        cos_ref[...] = jnp.cos(ang)
        sin_ref[...] = jnp.sin(ang) * sign_ref[...]

    row = pl.BlockSpec((1, RET_QK_DIM), lambda n: (0, 0))
    blk = pl.BlockSpec((RET_CHUNK, RET_QK_DIM), lambda n: (n, 0))
    cos, sin_s, *got = _pallas(kern, grid=(t // RET_CHUNK,), in_specs=[row, row], out_specs=[blk, blk],
                               out_shape=[jax.ShapeDtypeStruct((t, RET_QK_DIM), F32)] * 2, args=[theta2, sign],
                               name="position_tables", exchange=exchange)
    log_gamma = jnp.log(1.0 - 2.0 ** (-5.0 - jnp.arange(RET_HEADS, dtype=F32)))
    i = jnp.arange(RET_CHUNK, dtype=F32)
    diff = i[:, None] - i[None, :]
    causal = diff >= 0
    decay = jnp.where(causal[None], jnp.exp(jnp.where(causal, diff, 0.0)[None] * log_gamma[:, None, None]), 0.0)
    xi = jnp.exp((i + 1.0)[None, :] * log_gamma[:, None])[:, :, None]
    zeta = jnp.exp((RET_CHUNK - 1.0 - i)[None, :] * log_gamma[:, None])[:, :, None]
    gch = jnp.broadcast_to(jnp.exp(RET_CHUNK * log_gamma)[:, None, None], (RET_HEADS, 1, 128))
    return (cos, sin_s, decay, xi, zeta, gch), got


def _swap_pairs(x):
    lane = lax.broadcasted_iota(jnp.int32, x.shape, 1)
    return jnp.where((lane & 1) == 0, pltpu.roll(x, RET_QK_DIM - 1, 1), pltpu.roll(x, 1, 1))


def _rotate(x, cos, sin_s):
    return x * cos + _swap_pairs(x) * sin_s


def _rotate_bwd(dy, cos, sin_s):
    return dy * cos + _swap_pairs(dy * sin_s)


def _ret_specs(order):
    qk = pl.BlockSpec((RET_CHUNK, RET_QK), lambda j: (order(j), 0))
    v = pl.BlockSpec((RET_CHUNK, RET_V), lambda j: (order(j), 0))
    dec = pl.BlockSpec((RET_HEADS, RET_CHUNK, RET_CHUNK), lambda j: (0, 0, 0))
    col = pl.BlockSpec((RET_HEADS, RET_CHUNK, 1), lambda j: (0, 0, 0))
    gch = pl.BlockSpec((RET_HEADS, 1, 128), lambda j: (0, 0, 0))
    st = pl.BlockSpec((RET_HEADS, None, RET_QK_DIM, RET_V_DIM), lambda j: (0, order(j), 0, 0))
    pos = pl.BlockSpec((RET_CHUNK, RET_QK_DIM), lambda j: (order(j), 0))
    return qk, v, dec, col, gch, st, pos


def _ret_fwd(q_r, k_r, v_r, g_r, tables):
    t = q_r.shape[0]
    nc = t // RET_CHUNK
    cos, sin_s, decay, xi, zeta, gch = tables

    def kern(q_ref, k_ref, v_ref, g_ref, cos_ref, sin_ref, dec_ref, xi_ref, zeta_ref, gch_ref,
             o_ref, ret_ref, st_ref, state):
        @pl.when(pl.program_id(0) == 0)
        def _():
            state[...] = jnp.zeros_like(state)

        cos_t = cos_ref[...]
        sin_t = sin_ref[...]
        for h in range(RET_HEADS):
            qc = slice(h * RET_QK_DIM, (h + 1) * RET_QK_DIM)
            vc = slice(h * RET_V_DIM, (h + 1) * RET_V_DIM)
            qs = _bf(_rotate(q_ref[:, qc], cos_t, sin_t))
            ks = _rotate(k_ref[:, qc] * (RET_QK_DIM ** -0.5), cos_t, sin_t)
            vb = v_ref[:, vc]
            s_old = state[h]
            sb = _bf(s_old)
            st_ref[h] = sb
            inner = _dot_nt(qs, _bf(ks)) * dec_ref[h]
            out = _dot(_bf(inner), vb) + _dot(qs, sb) * xi_ref[h]
            state[h] = gch_ref[h, :, 0:1] * s_old + _dot_tn(_bf(ks * zeta_ref[h]), vb)
            o_ref[:, vc] = out
            r, rn = _rms_stats(out)
            g = g_ref[:, vc]
            ret_ref[:, vc] = _bf(g * jax.nn.sigmoid(g) * rn)

    qk, v, dec, col, gsp, st, pos = _ret_specs(lambda j: j)
    return dict(
        kern=kern,
        in_specs=[qk, qk, v, v, pos, pos, dec, col, col, gsp],
        out_specs=[v, v, st],
        out_shape=[jax.ShapeDtypeStruct((t, RET_V), F32), jax.ShapeDtypeStruct((t, RET_V), BF16),
                   jax.ShapeDtypeStruct((RET_HEADS, nc, RET_QK_DIM, RET_V_DIM), BF16)],
        scratch=[pltpu.VMEM((RET_HEADS, RET_QK_DIM, RET_V_DIM), F32)],
        args=[q_r, k_r, v_r, g_r, cos, sin_s, decay, xi, zeta, gch])


def _ret_bwd(q_r, k_r, v_r, d_o, states, tables):
    t = q_r.shape[0]
    nc = t // RET_CHUNK
    cos, sin_s, decay, xi, zeta, gch = tables

    def kern(q_ref, k_ref, v_ref, do_ref, st_ref, cos_ref, sin_ref, dec_ref, xi_ref, zeta_ref, gch_ref,
             d_ref, dstate):
        dq_ref, dk_ref = d_ref.at[:, 0:RET_QK], d_ref.at[:, RET_QK:2 * RET_QK]
        dv_ref = d_ref.at[:, 2 * RET_QK:2 * RET_QK + RET_V]

        @pl.when(pl.program_id(0) == 0)
        def _():
            dstate[...] = jnp.zeros_like(dstate)

        @pl.when(pl.program_id(0) < nc)
        def _():
            cos_t = cos_ref[...]
            sin_t = sin_ref[...]
            scale = RET_QK_DIM ** -0.5
            for h in range(RET_HEADS):
                qc = slice(h * RET_QK_DIM, (h + 1) * RET_QK_DIM)
                vc = slice(h * RET_V_DIM, (h + 1) * RET_V_DIM)
                qs = _bf(_rotate(q_ref[:, qc], cos_t, sin_t))
                ks = _rotate(k_ref[:, qc] * scale, cos_t, sin_t)
                ksb = _bf(ks)
                vb = v_ref[:, vc]
                d_o_t = do_ref[:, vc]
                dob = _bf(d_o_t)
                doxb = _bf(d_o_t * xi_ref[h])
                dec = dec_ref[h]
                ds_old = dstate[h]
                dsb = _bf(ds_old)
                pb = _bf(_dot_nt(qs, ksb) * dec)
                dpb = _bf(_dot_nt(dob, vb) * dec)
                dqs = _dot(dpb, ksb) + _dot_nt(doxb, st_ref[h])
                dks = _dot_tn(dpb, qs) + _dot_nt(vb, dsb) * zeta_ref[h]
                dv_ref[:, vc] = _bf(_dot_tn(pb, dob) + _dot(_bf(ks * zeta_ref[h]), dsb))
                dstate[h] = gch_ref[h, :, 0:1] * ds_old + _dot_tn(qs, doxb)
                dq_ref[:, qc] = _bf(_rotate_bwd(dqs, cos_t, sin_t))
                dk_ref[:, qc] = _bf(_rotate_bwd(dks, cos_t, sin_t) * scale)

    backwards = lambda j: jnp.maximum(nc - 1 - j, 0)
    qk, v, dec, col, gsp, st, pos = _ret_specs(backwards)
    return dict(
        kern=kern,
        in_specs=[qk, qk, v, v, st, pos, pos, dec, col, col, gsp],
        out_specs=[pl.BlockSpec((RET_CHUNK, 2 * RET_QK + RET_V), lambda j: (backwards(j), 0))],
        out_shape=[jax.ShapeDtypeStruct((t, 2 * RET_QK + RET_V), BF16)],
        scratch=[pltpu.VMEM((RET_HEADS, RET_QK_DIM, RET_V_DIM), F32)],
        args=[q_r, k_r, v_r, d_o, states, cos, sin_s, decay, xi, zeta, gch])


def _position():
    return lax.axis_index("x"), lax.axis_index("y"), lax.axis_index("c")


def _gather_exchange(owns, forward_at):
    n = len(owns)

    def copies(ins, outs, send_sems, recv_sems, staging):
        x, y, c = _position()
        sibling = (x, y, 1 - c)
        chips = [(1 - x, y), (x, 1 - y), (1 - x, 1 - y)]
        my_chip = 2 * x + y

        def slab(a, chip, hf):
            half = owns[a].shape[0] // 2
            return outs[a].at[chip, pl.ds(hf * half, half), :]

        def copy(k, src, dst, to):
            return pltpu.make_async_remote_copy(src_ref=src, dst_ref=dst, send_sem=send_sems.at[k],
                                                recv_sem=recv_sems.at[k], device_id=to, device_id_type=MESH)

        first, passed, from_sibling, stage_in, stage_out = [], [], [], [], []
        for a in range(n):
            half = owns[a].shape[0] // 2
            for k, (cx, cy) in enumerate(chips):
                first.append(copy(6 * a + k, ins[a].at[pl.ds(c * half, half), :], slab(a, my_chip, c), (cx, cy, c)))
                landed = slab(a, 2 * cx + cy, c)
                passed.append(copy(6 * a + 3 + k, landed, landed, sibling))
                theirs = slab(a, 2 * cx + cy, 1 - c)
                from_sibling.append(copy(6 * a + 3 + k, theirs, theirs, sibling))
            stage_in.append(pltpu.make_async_copy(ins[a], staging[a], send_sems.at[6 * n + a]))
            stage_out.append(pltpu.make_async_copy(staging[a], outs[a].at[my_chip], recv_sems.at[6 * n + a]))
        return first, passed, from_sibling, stage_in, stage_out

    def start(*args):
        first, _, _, stage_in, _ = copies(*args)
        for cp in first + stage_in:
            cp.start()

    def forward(*args):
        first, passed, _, stage_in, stage_out = copies(*args)
        for staged, cp in zip(stage_in, stage_out):
            staged.wait()
            cp.start()
        for arrived, cp in zip(first, passed):
            arrived.wait_recv()
            cp.start()

    def finish(*args):
        first, passed, from_sibling, _, stage_out = copies(*args)
        for cp in from_sibling:
            cp.wait_recv()
        for cp in first + passed:
            cp.wait_send()
        for cp in stage_out:
            cp.wait()

    outs = [jax.ShapeDtypeStruct((N_CHIPS, *a.shape), a.dtype) for a in owns]
    return _Exchange(owns, outs, 7 * n, [(0.0, start), (forward_at, forward), (1.0, finish)],
                     staging=[pltpu.VMEM(a.shape, a.dtype) for a in owns])


def _symmetric_exchange(ins, outs, plan, result_sources=()):
    n_sems = len(plan([None] * (len(ins) + len(result_sources)), [None] * len(outs), 0, 0, 0, dry=True))

    def copies(in_refs, out_refs, send_sems, recv_sems, staging):
        x, y, c = _position()
        return [pltpu.make_async_remote_copy(src_ref=src, dst_ref=dst, send_sem=send_sems.at[k],
                                             recv_sem=recv_sems.at[k], device_id=dev, device_id_type=MESH)
                for k, (src, dst, dev) in enumerate(plan(in_refs, out_refs, x, y, c, dry=False))]

    def start(*args):
        for cp in copies(*args):
            cp.start()

    def finish(*args):
        for cp in copies(*args):
            cp.wait()

    return _Exchange(ins, outs, n_sems, [(0.0, start), (1.0, finish)], result_sources=result_sources)


def _pair_exchange(gs):
    def plan(in_refs, out_refs, x, y, c, dry):
        out = []
        for a, g in enumerate(gs):
            half = g.shape[1] // 2
            for k in range(N_CHIPS):
                out.append(None if dry else (in_refs[a].at[k, pl.ds((1 - c) * half, half), :], out_refs[a].at[k],
                                             (x, y, 1 - c)))
        return out

    outs = [jax.ShapeDtypeStruct((g.shape[0], g.shape[1] // 2, g.shape[2]), g.dtype) for g in gs]
    return _symmetric_exchange(gs, outs, plan)


def _w_in_pair_plan(slabs):
    half = W_IN_SH // 2

    def plan(in_refs, out_refs, x, y, c, dry):
        return [None if dry else (in_refs[0].at[pl.ds(k * W_IN_SH + (1 - c) * half, half), :], out_refs[0].at[j],
                                  (x, y, 1 - c)) for j, k in enumerate(slabs)]

    return plan, [jax.ShapeDtypeStruct((len(slabs), half, D_MODEL), F32)]


def _pair_exchange_w_in(slabs, w_block=None):
    plan, outs = _w_in_pair_plan(slabs)
    if w_block is None:
        return _symmetric_exchange([], outs, plan, result_sources=[0])
    return _symmetric_exchange([w_block], outs, plan)


def _pair_sum(g, from_sibling, c_arr, *, tile, name):
    n, rows, width = g.shape
    tiles = (rows // 2) // tile
    firsts = [sum(s.shape[0] for s in from_sibling[:j]) for j in range(len(from_sibling))]

    def kern(c_ref, g_ref, *rest):
        *s_refs, o_ref = rest
        k = pl.program_id(1)
        s = s_refs[0][...]
        for first, s_ref in zip(firsts[1:], s_refs[1:]):
            s = jnp.where(k >= first, s_ref[...], s)
        o_ref[...] = _bf(g_ref[...] + s)

    def sibling_spec(first, count):
        return pl.BlockSpec((None, tile, width), lambda i, k, c: (jnp.clip(k - first, 0, count - 1), i, 0))

    return pl.pallas_call(
        kern,
        grid_spec=pltpu.PrefetchScalarGridSpec(
            num_scalar_prefetch=1, grid=(tiles, n),
            in_specs=[pl.BlockSpec((None, tile, width), lambda i, k, c: (k, c[0] * tiles + i, 0))]
            + [sibling_spec(first, s.shape[0]) for first, s in zip(firsts, from_sibling)],
            out_specs=pl.BlockSpec((None, tile, width), lambda i, k, c: (k, i, 0))),
        out_shape=jax.ShapeDtypeStruct((n, rows // 2, width), BF16), name=name,
        compiler_params=_params(("parallel", "parallel")),
    )(c_arr, g, *from_sibling)


def _scatter_to_owners(hsums):
    def plan(in_refs, out_refs, x, y, c, dry):
        out = []
        for a in range(len(hsums)):
            for k, (cx, cy) in enumerate([(1 - x, y), (x, 1 - y), (1 - x, 1 - y)]):
                out.append(None if dry else (in_refs[a].at[2 * cx + cy], out_refs[a].at[k], (cx, cy, c)))
        return out

    outs = [jax.ShapeDtypeStruct((3, *h.shape[1:]), h.dtype) for h in hsums]
    return _symmetric_exchange(hsums, outs, plan)


def _sum_chips(hsum, parts, chip_arr, *, tile, name):
    n, half, width = parts.shape

    def kern(chip_ref, h_ref, p_ref, o_ref):
        acc = h_ref[...].astype(F32)
        for k in range(n):
            acc = acc + p_ref[k].astype(F32)
        o_ref[...] = acc

    return pl.pallas_call(
        kern,
        grid_spec=pltpu.PrefetchScalarGridSpec(
            num_scalar_prefetch=1, grid=(half // tile,),
            in_specs=[pl.BlockSpec((None, tile, width), lambda i, chip: (chip[0], i, 0)),
                      pl.BlockSpec((n, tile, width), lambda i, chip: (0, i, 0))],
            out_specs=pl.BlockSpec((tile, width), lambda i, chip: (i, 0))),
        out_shape=jax.ShapeDtypeStruct((half, width), F32), name=name,
        compiler_params=_params(("parallel",)),
    )(chip_arr, hsum, parts)


def _share_halves(fhalves, w_in_slabs=()):
    n = len(fhalves)
    pair_plan, pair_outs = _w_in_pair_plan(w_in_slabs)

    def plan(in_refs, out_refs, x, y, c, dry):
        share = [None if dry else (in_refs[a], out_refs[a], (x, y, 1 - c)) for a in range(n)]
        return share + (pair_plan(in_refs[n:], out_refs[n:], x, y, c, dry) if w_in_slabs else [])

    outs = [jax.ShapeDtypeStruct(f.shape, f.dtype) for f in fhalves] + (pair_outs if w_in_slabs else [])
    return _symmetric_exchange(fhalves, outs, plan, result_sources=[0] if w_in_slabs else [])


def _adamw_math(w, g, m, v):
    m = ADAM_B1 * m + (1.0 - ADAM_B1) * g
    v = ADAM_B2 * v + (1.0 - ADAM_B2) * (g * g)
    m_hat = m / (1.0 - ADAM_B1 ** ADAM_STEP)
    v_hat = v / (1.0 - ADAM_B2 ** ADAM_STEP)
    delta = -ADAM_LR * (m_hat / (jnp.sqrt(v_hat) + ADAM_EPS) + ADAM_WD * w)
    return delta, m, v


def _adamw(mats, g_mine, g_other, c_arr, *, tile, name):
    width = g_mine.shape[1]
    tiles_per_half = g_mine.shape[0] // tile
    n_tiles = [w.shape[0] // tile for w, _, _, _ in mats]
    n_mats = len(mats)

    def kern(c_ref, *refs):
        ins, outs = refs[:5 * n_mats], refs[5 * n_mats:]
        for j, (_, _, _, row_off) in enumerate(mats):
            w_ref, gm_ref, go_ref, m_ref, v_ref = ins[5 * j:5 * j + 5]
            i = jnp.minimum(pl.program_id(0), n_tiles[j] - 1)
            in_my_half = ((row_off // tile + i) // tiles_per_half) == c_ref[0]
            g = jnp.where(in_my_half, gm_ref[...], go_ref[...])
            d, nm, nv = _adamw_math(w_ref[...], g, m_ref[...], v_ref[...])
            for out_ref, val in zip(outs[4 * j:4 * j + 4], (g, d, nm, nv)):
                out_ref[...] = val

    in_specs, out_specs, out_shape, args = [], [], [], []
    for (w, m, v, row_off), nt in zip(mats, n_tiles):
        full = pl.BlockSpec((tile, width), lambda i, c, nt=nt: (jnp.minimum(i, nt - 1), 0))

        def half(mine, nt=nt, first=row_off // tile):
            def index(i, c):
                pos = first + jnp.minimum(i, nt - 1)
                used = ((pos // tiles_per_half) == c[0]) == mine
                return (jnp.where(used, pos % tiles_per_half, 0), 0)
            return pl.BlockSpec((tile, width), index)

        in_specs += [full, half(True), half(False), full, full]
        out_specs += [full] * 4
        out_shape += [jax.ShapeDtypeStruct(w.shape, F32)] * 4
        args += [w, g_mine, g_other, m, v]
    outs = pl.pallas_call(
        kern,
        grid_spec=pltpu.PrefetchScalarGridSpec(num_scalar_prefetch=1, grid=(max(n_tiles),), in_specs=in_specs,
                                               out_specs=out_specs),
        out_shape=out_shape, name=name, compiler_params=_params(("arbitrary",)),
    )(c_arr, *args)
    return [outs[4 * j:4 * j + 4] for j in range(n_mats)]


def _small_step(partials, params, w_in_half):
    slots = ((0, 0, D_MODEL), (1, 0, D_MODEL), (2, 0, HEAD_DIM), (2, 128, HEAD_DIM), (2, 256, N_Q_HEADS))
    loss_slot = (2, 384, 128)

    def body(*refs):
        loss_ref, dg1_ref, dg2_ref, dgq_ref, dgk_ref, dsink_ref = refs[:6]
        p_refs, half_ref, out_refs, other_ref = refs[6:21], refs[21], refs[22:43], refs[43]
        mine, gathered, send_sems, recv_sems = refs[44:]
        x, y, c = _position()
        me = 4 * x + 2 * y + c
        halves = pltpu.make_async_remote_copy(
            src_ref=half_ref, dst_ref=other_ref, send_sem=send_sems.at[N_DEV - 1], recv_sem=recv_sems.at[N_DEV - 1],
            device_id=(x, y, 1 - c), device_id_type=MESH)
        halves.start()
        mine[...] = jnp.zeros_like(mine)
        for (row, lane, n), val in zip(slots + (loss_slot,), (
                jnp.sum(dg1_ref[...], axis=0, keepdims=True), jnp.sum(dg2_ref[...], axis=0, keepdims=True),
                dgq_ref[...], dgk_ref[...], dsink_ref[...], jnp.sum(loss_ref[...], axis=0, keepdims=True))):
            mine[row:row + 1, lane:lane + n] = val
        copies = []
        for k in range(1, N_DEV):
            flip = (k >> 2) & 1, (k >> 1) & 1, k & 1
            to = (x ^ flip[0], y ^ flip[1], c ^ flip[2])
            cp = pltpu.make_async_remote_copy(
                src_ref=mine, dst_ref=gathered.at[me], send_sem=send_sems.at[k - 1], recv_sem=recv_sems.at[k - 1],
                device_id=to, device_id_type=MESH)
            cp.start()
            copies.append(cp)
        gathered[me] = mine[...]
        for k in range(1, N_DEV):
            flip = (k >> 2) & 1, (k >> 1) & 1, k & 1
            src = 4 * (x ^ flip[0]) + 2 * (y ^ flip[1]) + (c ^ flip[2])
            pltpu.make_async_remote_copy(
                src_ref=mine, dst_ref=gathered.at[src], send_sem=send_sems.at[k - 1], recv_sem=recv_sems.at[k - 1],
                device_id=(x, y, c), device_id_type=MESH).wait_recv()
        for cp in copies:
            cp.wait_send()
        total = gathered[0]
        for k in range(1, N_DEV):
            total = total + gathered[k]
        row, lane, n = loss_slot
        out_refs[0][...] = total[row:row + 1, lane:lane + n]
        for i, (row, lane, n) in enumerate(slots):
            g = total[row:row + 1, lane:lane + n]
            d, nm, nv = _adamw_math(p_refs[i][...], g, p_refs[5 + i][...], p_refs[10 + i][...])
            for kind, val in enumerate((g, d, nm, nv)):
                out_refs[1 + 5 * kind + i][...] = val
        halves.wait()

    vm = pl.BlockSpec(memory_space=pltpu.VMEM)
    shapes = [jax.ShapeDtypeStruct((1, 128), F32)] + [jax.ShapeDtypeStruct((1, n), F32) for _, _, n in slots] * 4
    return pl.pallas_call(
        body, in_specs=[vm] * 21 + [_ANY], out_specs=[vm] * 21 + [_ANY],
        out_shape=shapes + [jax.ShapeDtypeStruct(w_in_half.shape, w_in_half.dtype)],
        scratch_shapes=[pltpu.VMEM((SMALL_ROWS, D_MODEL), F32), pltpu.VMEM((N_DEV, SMALL_ROWS, D_MODEL), F32),
                        pltpu.SemaphoreType.DMA((N_DEV,)), pltpu.SemaphoreType.DMA((N_DEV,))],
        name="small_step",
    )(*partials, *params, w_in_half)


def kernel(x, norm_mix_gain, w_in, q_norm_gain, k_norm_gain, attn_sinks, w_branch_attn, w_branch_ret, w_out, norm_ffn_gain, w_ffn_gate, w_ffn_up, w_ffn_down, loss_target, m_norm_mix_gain, m_w_in, m_q_norm_gain, m_k_norm_gain, m_attn_sinks, m_w_branch_attn, m_w_branch_ret, m_w_out, m_norm_ffn_gain, m_w_ffn_gate, m_w_ffn_up, m_w_ffn_down, v_norm_mix_gain, v_w_in, v_q_norm_gain, v_k_norm_gain, v_attn_sinks, v_w_branch_attn, v_w_branch_ret, v_w_out, v_norm_ffn_gain, v_w_ffn_gate, v_w_ffn_up, v_w_ffn_down):
    my_chip = 2 * lax.axis_index("x") + lax.axis_index("y")
    c_arr = lax.axis_index("c").astype(jnp.int32).reshape(1)
    chip_arr = my_chip.astype(jnp.int32).reshape(1)
    x_t, target = x[0], loss_target[0]
    g1, g2, gq, gk, sinks = norm_mix_gain, norm_ffn_gain, q_norm_gain, k_norm_gain, attn_sinks

    tr = lambda a: jnp.transpose(a[0])
    own_w_in = _bf(tr(w_in))
    own_rest = [_bf(a) for a in (tr(w_ffn_gate), tr(w_ffn_up), w_ffn_down[0], w_branch_attn[0], w_branch_ret[0],
                                 w_out[0])]
    tables, (got_w_in,) = _ret_tables(x_t.shape[0], _gather_exchange([own_w_in], 0.9))
    w_in_t = got_w_in.reshape(D_IN, D_MODEL)
    h1, q_a, kv_a, q_r, k_r, v_r, g_r, z_a, z_r, *got_rest = _proj_fwd(x_t, g1, w_in_t, _gather_exchange(own_rest, 0.8))
    wg_t, wu_t, wd, wba, wbr, wout = [got.reshape(-1, D_MODEL) for got in got_rest]

    gq_col, gk_col = gq.reshape(HEAD_DIM, 1), gk.reshape(HEAD_DIM, 1)
    attn, probs, sink_probs, o_ret, ret, states = _fused(
        [_attn_fwd(q_a, kv_a, gq_col, gk, sinks), _ret_fwd(q_r, k_r, v_r, g_r, tables)],
        grid=(x_t.shape[0] // BLOCK,), name="mixers_fwd")
    ba, br, merged, x1, h2 = _mix_fwd(attn, ret, z_a, z_r, x_t, wba, wbr, wout, g2)
    act, dgate, dup, dyb, dx1, dx1b, loss_p, dg2_p = _ffn_fwd_bwd(h2, x1, target, wg_t, wu_t, wd, g2)

    def pairs(row0, rows):
        return lambda i: [(h * rows, rows, (2 * i + h, pl.ds(row0, rows), slice(None))) for h in range(2)]

    f_block = jax.ShapeDtypeStruct((N_CHIPS, 3 * FF_SH, D_MODEL), F32)
    f_block, = _dw([dgate, dup], h2, tm=4 * FF_SH, tk=1024, buf=f_block, name="dw_gate_up",
                   place=lambda i: [((2 * j + h) * FF_SH, FF_SH, (2 * i + h, pl.ds(j * FF_SH, FF_SH), slice(None)))
                                    for j in range(2) for h in range(2)])
    f_block, = _dw(act, dyb, tm=2 * FF_SH, place=pairs(2 * FF_SH, FF_SH), buf=f_block, name="dw_down")
    (dba, dbr, d_attn, d_o, d_gz, sib_ffn) = _mix_bwd(
        dx1b, z_a, z_r, ba, br, g_r, o_ret, wout, wba, wbr, _pair_exchange([f_block]))
    f_sum = _pair_sum(f_block, [sib_ffn], c_arr, tile=528, name="pair_sum_ffn")

    def quarters(row0, rows):
        return lambda i: [(k * rows, rows, (k, pl.ds(row0, rows), slice(None))) for k in range(N_CHIPS)]

    m_block = jax.ShapeDtypeStruct((N_CHIPS, D_MODEL, D_MODEL), F32)
    m_block, = _dw(attn, dba, tm=ATT_Q, place=quarters(0, 256), buf=m_block, name="dw_ba")
    m_block, = _dw(ret, dbr, tm=D_MODEL, place=pairs(256, 512), buf=m_block, name="dw_br")
    m_block, = _dw(merged, dx1b, tm=D_MODEL, place=quarters(768, 256), buf=m_block, name="dw_out")

    def w_in_rows(off, w):
        tm = min(w, D_MODEL)
        return dict(tm=tm, place=lambda i: [(0, tm, (pl.ds(off + i * tm, tm), slice(None)))])

    w_block = jax.ShapeDtypeStruct((D_IN, D_MODEL), F32)
    w_block, sib_mix = _dw(d_gz, h1, buf=w_block, name="dw_in_gz", exchange=_pair_exchange([m_block]),
                           **w_in_rows(P_GR[0], d_gz.shape[1]))
    m_sum = _pair_sum(m_block, [sib_mix], c_arr, tile=256, name="pair_sum_mix")

    (dq_a, dkv_a, dgq, dgk, dsinks, d_ret, got_ffn_sums, got_mix_sums) = _fused(
        [_attn_bwd(q_a, kv_a, d_attn, probs, sink_probs, gq_col, gk, gk_col),
         _ret_bwd(q_r, k_r, v_r, d_o, states, tables)],
        grid=(x_t.shape[0] // BLOCK + 1,), name="mixers_bwd", exchange=_scatter_to_owners([f_sum, m_sum]))
    dgq = dgq.reshape(1, HEAD_DIM)
    ffn_half = _sum_chips(f_sum, got_ffn_sums, chip_arr, tile=528, name="sum_chips_ffn")
    mix_half = _sum_chips(m_sum, got_mix_sums, chip_arr, tile=256, name="sum_chips_mix")
    w_block, ffn_other, mix_other, sib_w_in_3 = _dw(
        d_ret, h1, buf=w_block, name="dw_in_ret", exchange=_share_halves([ffn_half, mix_half], w_in_slabs=(3,)),
        **w_in_rows(P_QR[0], d_ret.shape[1]))
    qkv_rows = P_QA[1] + P_KVA[1]
    w_block, sib_w_in_12 = _dw([dq_a, dkv_a], h1, buf=w_block, name="dw_in_q_kv", tm=qkv_rows,
                               exchange=_pair_exchange_w_in((1, 2)),
                               place=lambda i: [(0, qkv_rows, (pl.ds(P_QA[0], qkv_rows), slice(None)))])
    sib_w_in_0, = _run_exchange(_pair_exchange_w_in((0,), w_block), "pair_exchange_w_in")
    w_sum = _pair_sum(w_block.reshape(N_CHIPS, W_IN_SH, D_MODEL), [sib_w_in_0, sib_w_in_12, sib_w_in_3], c_arr,
                      tile=592, name="pair_sum_w_in")
    d_pieces = [dq_a, dkv_a, d_ret, d_gz]
    grad_x, dg1_p, got_w_in_sums = _proj_bwd(d_pieces, x_t, dx1, w_in_t, g1, _scatter_to_owners([w_sum]))
    w_in_half = _sum_chips(w_sum, got_w_in_sums, chip_arr, tile=592, name="sum_chips_w_in")
    loss_row, *small, w_in_other = _small_step(
        [loss_p.reshape(-1, 128), dg1_p.reshape(-1, D_MODEL), dg2_p.reshape(-1, D_MODEL), dgq, dgk, dsinks],
        [norm_mix_gain, norm_ffn_gain, q_norm_gain, k_norm_gain, attn_sinks,
         m_norm_mix_gain, m_norm_ffn_gain, m_q_norm_gain, m_k_norm_gain, m_attn_sinks,
         v_norm_mix_gain, v_norm_ffn_gain, v_q_norm_gain, v_k_norm_gain, v_attn_sinks], w_in_half)
    loss = loss_row[0, 0]

    def update(name, g_half, g_other, tile, mats):
        outs = _adamw([tuple(tr(a) if t else a[0] for a in wmv) + (off,) for _, *wmv, off, t in mats],
                      g_half, g_other, c_arr, tile=tile, name=f"adamw_{name}")
        return {key: [jnp.transpose(o) if t else o for o in res] for (key, _, _, _, _, t), res in zip(mats, outs)}

    big = {
        **update("w_in", w_in_half, w_in_other, 592, [("w_in", w_in, m_w_in, v_w_in, 0, True)]),
        **update("ffn", ffn_half, ffn_other, 176, [
            ("wg", w_ffn_gate, m_w_ffn_gate, v_w_ffn_gate, 0, True),
            ("wu", w_ffn_up, m_w_ffn_up, v_w_ffn_up, FF_SH, True),
            ("wd", w_ffn_down, m_w_ffn_down, v_w_ffn_down, 2 * FF_SH, False)]),
        **update("mix", mix_half, mix_other, 128, [
            ("wba", w_branch_attn, m_w_branch_attn, v_w_branch_attn, 0, False),
            ("wbr", w_branch_ret, m_w_branch_ret, v_w_branch_ret, 256, False),
            ("wout", w_out, m_w_out, v_w_out, 768, False)])}

    def leaves(i):
        b = [big[n][i][None] for n in ("w_in", "wba", "wbr", "wout", "wg", "wu", "wd")]
        s1, s2, sq, sk, ss = small[5 * i:5 * i + 5]
        return [s1, b[0], sq, sk, ss, b[1], b[2], b[3], s2, b[4], b[5], b[6]]

    return (loss, grad_x[None], *leaves(0), *leaves(1), *leaves(2), *leaves(3))
```
